```python
import jax, jax.numpy as jnp
from jax import lax
import numpy as np

D_MODEL = 1024
BATCH = 8
SEQ = 2048
DEPTH = 4

N_A_LAYERS = DEPTH // 2
N_B_LAYERS = DEPTH - N_A_LAYERS
EPS = 1e-5
N_MOD = 9
D_FF = 2816
FFN_HALF = 0.5
SSM_EXPAND = 2
D_INNER = SSM_EXPAND * D_MODEL
SSM_HEADDIM = 64
SSM_HEADS = D_INNER // SSM_HEADDIM
SSM_GROUPS = 8
SSM_STATE = 128
CONV_WIDTH = 4
CHUNK = 128
CONV_DIM = D_INNER + 2 * SSM_GROUPS * SSM_STATE
IN_PROJ_DIM = D_INNER + CONV_DIM + SSM_HEADS
ATT_HEADS = 16
KV_HEADS = 4
HEAD_DIM = 64
WINDOW = 128

kernel_name = "yoco_mamba2_swa_sink_macaron_adaln"


def rmsnorm(x, w):
    xf = x.astype(jnp.float32)
    y = xf * lax.rsqrt(jnp.mean(xf * xf, axis=-1, keepdims=True) + EPS)
    return (y * w.astype(jnp.float32)).astype(x.dtype)


def modulate(h, shift, scale):
    return h * (1 + scale[:, None, :]) + shift[:, None, :]


def swiglu(h, w_gu, w_down):
    g, u = jnp.split(h @ w_gu, 2, axis=-1)
    return (jax.nn.silu(g) * u) @ w_down


def causal_dwconv(u, w, b):
    y = lax.conv_general_dilated(u, w[:, None, :], window_strides=(1,),
                                 padding=[(CONV_WIDTH - 1, 0)],
                                 dimension_numbers=("NWC", "WIO", "NWC"),
                                 feature_group_count=u.shape[-1])
    return y + b


def ssd_scan(x, dt, A, Bm, Cm):
    b, l, h, p = x.shape
    g, n = Bm.shape[-2:]
    r = h // g
    nc = l // CHUNK
    X = (x * dt[..., None]).reshape(b, nc, CHUNK, g, r, p)
    a = jnp.moveaxis((dt * A).reshape(b, nc, CHUNK, g, r), 2, -1)
    a_cs = jnp.cumsum(a, axis=-1)
    Bc = Bm.reshape(b, nc, CHUNK, g, n)
    Cc = Cm.reshape(b, nc, CHUNK, g, n)
    idx = jnp.arange(CHUNK)
    causal = idx[:, None] >= idx[None, :]
    seg = a_cs[..., :, None] - a_cs[..., None, :]
    Lmat = jnp.exp(jnp.where(causal, seg, -jnp.inf))
    CB = jnp.einsum("bclgn,bcsgn->bcgls", Cc, Bc)
    y_diag = jnp.einsum("bcgrls,bcsgrp->bclgrp", CB[:, :, :, None] * Lmat, X)
    decay_to_end = jnp.exp(a_cs[..., -1:] - a_cs)
    chunk_states = jnp.einsum("bclgn,bcgrl,bclgrp->bcgrpn", Bc, decay_to_end, X)
    chunk_decay = jnp.exp(a_cs[..., -1])

    def step(state, inp):
        s_c, d_c = inp
        return state * d_c[..., None, None] + s_c, state

    init = jnp.zeros((b, g, r, p, n), jnp.float32)
    _, prev = lax.scan(step, init, (jnp.moveaxis(chunk_states, 1, 0),
                                    jnp.moveaxis(chunk_decay, 1, 0)))
    prev = jnp.moveaxis(prev, 0, 1)
    y_off = jnp.einsum("bclgn,bcgrpn,bcgrl->bclgrp", Cc, prev, jnp.exp(a_cs))
    return (y_diag + y_off).reshape(b, l, h, p)


def mamba2_mixer(h, w_in, conv_w, conv_b, dt_bias, a_log, d_skip, norm_w, w_out):
    b, l, _ = h.shape
    f32 = jnp.float32
    z, xbc, dt = jnp.split(h @ w_in, [D_INNER, D_INNER + CONV_DIM], axis=-1)
    xbc = jax.nn.silu(causal_dwconv(xbc, conv_w, conv_b))
    xs, Bm, Cm = jnp.split(xbc, [D_INNER, D_INNER + SSM_GROUPS * SSM_STATE], axis=-1)
    dt = jax.nn.softplus(dt.astype(f32) + dt_bias.astype(f32))
    A = -jnp.exp(a_log.astype(f32))
    xs_h = xs.astype(f32).reshape(b, l, SSM_HEADS, SSM_HEADDIM)
    y = ssd_scan(xs_h, dt, A,
                 Bm.astype(f32).reshape(b, l, SSM_GROUPS, SSM_STATE),
                 Cm.astype(f32).reshape(b, l, SSM_GROUPS, SSM_STATE))
    y = y + d_skip.astype(f32)[:, None] * xs_h
    y = y.reshape(b, l, D_INNER) * jax.nn.silu(z.astype(f32))
    yg = y.reshape(b, l, SSM_GROUPS, D_INNER // SSM_GROUPS)
    yg = yg * lax.rsqrt(jnp.mean(yg * yg, axis=-1, keepdims=True) + EPS)
    y = yg.reshape(b, l, D_INNER) * norm_w.astype(f32)
    return y.astype(h.dtype) @ w_out


def to_bands(t):
    b, l = t.shape[:2]
    nb = l // WINDOW
    blocks = t.reshape(b, nb, WINDOW, *t.shape[2:])
    prev = jnp.pad(blocks[:, :-1], ((0, 0), (1, 0)) + ((0, 0),) * (blocks.ndim - 2))
    return jnp.concatenate([prev, blocks], axis=2)


def shared_kv(x, norm_w, shift, scale, w_kv, b_kv):
    b, l, _ = x.shape
    h = modulate(rmsnorm(x, norm_w), shift, scale)
    k, v = jnp.split(h @ w_kv + b_kv, 2, axis=-1)
    k = k.reshape(b, l, KV_HEADS, HEAD_DIM)
    v = v.reshape(b, l, KV_HEADS, HEAD_DIM)
    return to_bands(k), to_bands(v)


def swa_sink_attention(h, k_band, v_band, w_q, b_q, sinks, w_o, b_o):
    b, l, _ = h.shape
    nb = l // WINDOW
    r = ATT_HEADS // KV_HEADS
    f32 = jnp.float32
    q = (h @ w_q + b_q).reshape(b, nb, WINDOW, KV_HEADS, r, HEAD_DIM)
    s = jnp.einsum("bnqkrd,bnskd->bkrnqs", q.astype(f32), k_band.astype(f32)) * HEAD_DIM ** -0.5
    qpos = jnp.arange(WINDOW)[:, None] + WINDOW
    kpos = jnp.arange(2 * WINDOW)[None, :]
    local = (kpos <= qpos) & (kpos > qpos - WINDOW)
    mask = local[None] & ((jnp.arange(nb)[:, None, None] > 0) | (kpos[None] >= WINDOW))
    logits = jnp.where(mask, s, -jnp.inf)
    sink = sinks.astype(f32).reshape(1, KV_HEADS, r, 1, 1)
    m = jnp.maximum(jnp.max(logits, axis=-1), sink)
    p = jnp.exp(logits - m[..., None])
    denom = jnp.sum(p, axis=-1) + jnp.exp(sink - m)
    probs = (p / denom[..., None]).astype(v_band.dtype)
    o = jnp.einsum("bkrnqs,bnskd->bnqkrd", probs, v_band)
    return o.reshape(b, l, ATT_HEADS * HEAD_DIM) @ w_o + b_o


def _fwd_setup_inputs(seed: int = 0) -> dict:
    key = jax.random.key(seed)
    ks = jax.random.split(key, 32)
    f32 = jnp.float32
    nrm = lambda k, shape, s: jax.random.normal(k, shape, f32) * s
    D = D_MODEL
    dt0 = jnp.exp(jax.random.uniform(ks[10], (N_A_LAYERS, SSM_HEADS), f32,
                                     np.log(1e-3), np.log(1e-1)))
    dt_bias = dt0 + jnp.log(-jnp.expm1(-dt0))
    return {
        "x": nrm(ks[0], (BATCH, SEQ, D), 1.0),
        "c": nrm(ks[1], (BATCH, D), 1.0),
        "ffn_norm_w": 1.0 + nrm(ks[2], (DEPTH, 2, D), 0.02),
        "ffn_w_gu": nrm(ks[3], (DEPTH, 2, D, 2 * D_FF), D ** -0.5),
        "ffn_w_down": nrm(ks[4], (DEPTH, 2, D_FF, D), D_FF ** -0.5),
        "mod_w": nrm(ks[5], (DEPTH, D, N_MOD * D), 0.5 * D ** -0.5),
        "mod_b": nrm(ks[6], (DEPTH, N_MOD * D), 0.02),
        "mix_norm_w": 1.0 + nrm(ks[7], (DEPTH, D), 0.02),
        "ssm_w_in": nrm(ks[8], (N_A_LAYERS, D, IN_PROJ_DIM), D ** -0.5),
        "ssm_conv_w": nrm(ks[9], (N_A_LAYERS, CONV_WIDTH, CONV_DIM), CONV_WIDTH ** -0.5),
        "ssm_conv_b": nrm(ks[11], (N_A_LAYERS, CONV_DIM), 0.02),
        "ssm_dt_bias": dt_bias,
        "ssm_a_log": jnp.log(jax.random.uniform(ks[12], (N_A_LAYERS, SSM_HEADS), f32, 1.0, 16.0)),
        "ssm_d": 1.0 + nrm(ks[13], (N_A_LAYERS, SSM_HEADS), 0.1),
        "ssm_norm_w": 1.0 + nrm(ks[14], (N_A_LAYERS, D_INNER), 0.02),
        "ssm_w_out": nrm(ks[15], (N_A_LAYERS, D_INNER, D), D_INNER ** -0.5),
        "kv_norm_w": 1.0 + nrm(ks[16], (D,), 0.02),
        "kv_mod_w": nrm(ks[17], (D, 2 * D), 0.5 * D ** -0.5),
        "kv_mod_b": nrm(ks[18], (2 * D,), 0.02),
        "w_kv": nrm(ks[19], (D, 2 * KV_HEADS * HEAD_DIM), D ** -0.5),
        "b_kv": nrm(ks[20], (2 * KV_HEADS * HEAD_DIM,), 0.02),
        "attn_w_q": nrm(ks[21], (N_B_LAYERS, D, ATT_HEADS * HEAD_DIM), D ** -0.5),
        "attn_b_q": nrm(ks[22], (N_B_LAYERS, ATT_HEADS * HEAD_DIM), 0.02),
        "attn_sinks": nrm(ks[23], (N_B_LAYERS, ATT_HEADS), 0.5),
        "attn_w_o": nrm(ks[24], (N_B_LAYERS, ATT_HEADS * HEAD_DIM, D), (ATT_HEADS * HEAD_DIM) ** -0.5),
        "attn_b_o": nrm(ks[25], (N_B_LAYERS, D), 0.02),
        "final_norm_w": 1.0 + nrm(ks[26], (D,), 0.02),
    }


def _fwd_reference(x, c, ffn_norm_w, ffn_w_gu, ffn_w_down, mod_w, mod_b, mix_norm_w,
              ssm_w_in, ssm_conv_w, ssm_conv_b, ssm_dt_bias, ssm_a_log, ssm_d,
              ssm_norm_w, ssm_w_out, kv_norm_w, kv_mod_w, kv_mod_b, w_kv, b_kv,
              attn_w_q, attn_b_q, attn_sinks, attn_w_o, attn_b_o, final_norm_w):
    c_act = jax.nn.silu(c)
    k_band = v_band = None
    for i in range(DEPTH):
        if i == N_A_LAYERS:
            kv_shift, kv_scale = jnp.split(c_act @ kv_mod_w + kv_mod_b, 2, axis=-1)
            k_band, v_band = shared_kv(x, kv_norm_w, kv_shift, kv_scale, w_kv, b_kv)
        mod = c_act @ mod_w[i] + mod_b[i]
        sh1, sc1, g1, shm, scm, gm, sh2, sc2, g2 = jnp.split(mod, N_MOD, axis=-1)
        h = modulate(rmsnorm(x, ffn_norm_w[i, 0]), sh1, sc1)
        x = x + FFN_HALF * g1[:, None, :] * swiglu(h, ffn_w_gu[i, 0], ffn_w_down[i, 0])
        h = modulate(rmsnorm(x, mix_norm_w[i]), shm, scm)
        if i < N_A_LAYERS:
            j = i
            y = mamba2_mixer(h, ssm_w_in[j], ssm_conv_w[j], ssm_conv_b[j], ssm_dt_bias[j],
                             ssm_a_log[j], ssm_d[j], ssm_norm_w[j], ssm_w_out[j])
        else:
            j = i - N_A_LAYERS
            y = swa_sink_attention(h, k_band, v_band, attn_w_q[j], attn_b_q[j],
                                   attn_sinks[j], attn_w_o[j], attn_b_o[j])
        x = x + gm[:, None, :] * y
        h = modulate(rmsnorm(x, ffn_norm_w[i, 1]), sh2, sc2)
        x = x + FFN_HALF * g2[:, None, :] * swiglu(h, ffn_w_gu[i, 1], ffn_w_down[i, 1])
    return rmsnorm(x, final_norm_w)


import jax as _jax
import jax.numpy as _jnp

TWIN_FORMAT = 'train_step'
FWD_PARAMS = ['x', 'c', 'ffn_norm_w', 'ffn_w_gu', 'ffn_w_down', 'mod_w', 'mod_b', 'mix_norm_w', 'ssm_w_in', 'ssm_conv_w', 'ssm_conv_b', 'ssm_dt_bias', 'ssm_a_log', 'ssm_d', 'ssm_norm_w', 'ssm_w_out', 'kv_norm_w', 'kv_mod_w', 'kv_mod_b', 'w_kv', 'b_kv', 'attn_w_q', 'attn_b_q', 'attn_sinks', 'attn_w_o', 'attn_b_o', 'final_norm_w']
TWIN_WEIGHTS = ['ffn_norm_w', 'ffn_w_gu', 'ffn_w_down', 'mod_w', 'mod_b', 'mix_norm_w', 'ssm_w_in', 'ssm_conv_w', 'ssm_conv_b', 'ssm_dt_bias', 'ssm_a_log', 'ssm_d', 'ssm_norm_w', 'ssm_w_out', 'kv_norm_w', 'kv_mod_w', 'kv_mod_b', 'w_kv', 'b_kv', 'attn_w_q', 'attn_b_q', 'attn_sinks', 'attn_w_o', 'attn_b_o', 'final_norm_w']
TWIN_DIFF_INPUT = 'x'
TWIN_INPUTS = ['x', 'c', 'ffn_norm_w', 'ffn_w_gu', 'ffn_w_down', 'mod_w', 'mod_b', 'mix_norm_w', 'ssm_w_in', 'ssm_conv_w', 'ssm_conv_b', 'ssm_dt_bias', 'ssm_a_log', 'ssm_d', 'ssm_norm_w', 'ssm_w_out', 'kv_norm_w', 'kv_mod_w', 'kv_mod_b', 'w_kv', 'b_kv', 'attn_w_q', 'attn_b_q', 'attn_sinks', 'attn_w_o', 'attn_b_o', 'final_norm_w', 'loss_target', 'm_ffn_norm_w', 'm_ffn_w_gu', 'm_ffn_w_down', 'm_mod_w', 'm_mod_b', 'm_mix_norm_w', 'm_ssm_w_in', 'm_ssm_conv_w', 'm_ssm_conv_b', 'm_ssm_dt_bias', 'm_ssm_a_log', 'm_ssm_d', 'm_ssm_norm_w', 'm_ssm_w_out', 'm_kv_norm_w', 'm_kv_mod_w', 'm_kv_mod_b', 'm_w_kv', 'm_b_kv', 'm_attn_w_q', 'm_attn_b_q', 'm_attn_sinks', 'm_attn_w_o', 'm_attn_b_o', 'm_final_norm_w', 'v_ffn_norm_w', 'v_ffn_w_gu', 'v_ffn_w_down', 'v_mod_w', 'v_mod_b', 'v_mix_norm_w', 'v_ssm_w_in', 'v_ssm_conv_w', 'v_ssm_conv_b', 'v_ssm_dt_bias', 'v_ssm_a_log', 'v_ssm_d', 'v_ssm_norm_w', 'v_ssm_w_out', 'v_kv_norm_w', 'v_kv_mod_w', 'v_kv_mod_b', 'v_w_kv', 'v_b_kv', 'v_attn_w_q', 'v_attn_b_q', 'v_attn_sinks', 'v_attn_w_o', 'v_attn_b_o', 'v_final_norm_w']
TWIN_OUTPUTS = ['loss', 'grad_x', 'grad_ffn_norm_w', 'grad_ffn_w_gu', 'grad_ffn_w_down', 'grad_mod_w', 'grad_mod_b', 'grad_mix_norm_w', 'grad_ssm_w_in', 'grad_ssm_conv_w', 'grad_ssm_conv_b', 'grad_ssm_dt_bias', 'grad_ssm_a_log', 'grad_ssm_d', 'grad_ssm_norm_w', 'grad_ssm_w_out', 'grad_kv_norm_w', 'grad_kv_mod_w', 'grad_kv_mod_b', 'grad_w_kv', 'grad_b_kv', 'grad_attn_w_q', 'grad_attn_b_q', 'grad_attn_sinks', 'grad_attn_w_o', 'grad_attn_b_o', 'grad_final_norm_w', 'delta_ffn_norm_w', 'delta_ffn_w_gu', 'delta_ffn_w_down', 'delta_mod_w', 'delta_mod_b', 'delta_mix_norm_w', 'delta_ssm_w_in', 'delta_ssm_conv_w', 'delta_ssm_conv_b', 'delta_ssm_dt_bias', 'delta_ssm_a_log', 'delta_ssm_d', 'delta_ssm_norm_w', 'delta_ssm_w_out', 'delta_kv_norm_w', 'delta_kv_mod_w', 'delta_kv_mod_b', 'delta_w_kv', 'delta_b_kv', 'delta_attn_w_q', 'delta_attn_b_q', 'delta_attn_sinks', 'delta_attn_w_o', 'delta_attn_b_o', 'delta_final_norm_w', 'new_m_ffn_norm_w', 'new_m_ffn_w_gu', 'new_m_ffn_w_down', 'new_m_mod_w', 'new_m_mod_b', 'new_m_mix_norm_w', 'new_m_ssm_w_in', 'new_m_ssm_conv_w', 'new_m_ssm_conv_b', 'new_m_ssm_dt_bias', 'new_m_ssm_a_log', 'new_m_ssm_d', 'new_m_ssm_norm_w', 'new_m_ssm_w_out', 'new_m_kv_norm_w', 'new_m_kv_mod_w', 'new_m_kv_mod_b', 'new_m_w_kv', 'new_m_b_kv', 'new_m_attn_w_q', 'new_m_attn_b_q', 'new_m_attn_sinks', 'new_m_attn_w_o', 'new_m_attn_b_o', 'new_m_final_norm_w', 'new_v_ffn_norm_w', 'new_v_ffn_w_gu', 'new_v_ffn_w_down', 'new_v_mod_w', 'new_v_mod_b', 'new_v_mix_norm_w', 'new_v_ssm_w_in', 'new_v_ssm_conv_w', 'new_v_ssm_conv_b', 'new_v_ssm_dt_bias', 'new_v_ssm_a_log', 'new_v_ssm_d', 'new_v_ssm_norm_w', 'new_v_ssm_w_out', 'new_v_kv_norm_w', 'new_v_kv_mod_w', 'new_v_kv_mod_b', 'new_v_w_kv', 'new_v_b_kv', 'new_v_attn_w_q', 'new_v_attn_b_q', 'new_v_attn_sinks', 'new_v_attn_w_o', 'new_v_attn_b_o', 'new_v_final_norm_w']
TWIN_LEAF_KINDS = {'loss': 'loss', 'grad_x': 'grad_x', 'grad_ffn_norm_w': 'grad_w', 'grad_ffn_w_gu': 'grad_w', 'grad_ffn_w_down': 'grad_w', 'grad_mod_w': 'grad_w', 'grad_mod_b': 'grad_w', 'grad_mix_norm_w': 'grad_w', 'grad_ssm_w_in': 'grad_w', 'grad_ssm_conv_w': 'grad_w', 'grad_ssm_conv_b': 'grad_w', 'grad_ssm_dt_bias': 'grad_w', 'grad_ssm_a_log': 'grad_w', 'grad_ssm_d': 'grad_w', 'grad_ssm_norm_w': 'grad_w', 'grad_ssm_w_out': 'grad_w', 'grad_kv_norm_w': 'grad_w', 'grad_kv_mod_w': 'grad_w', 'grad_kv_mod_b': 'grad_w', 'grad_w_kv': 'grad_w', 'grad_b_kv': 'grad_w', 'grad_attn_w_q': 'grad_w', 'grad_attn_b_q': 'grad_w', 'grad_attn_sinks': 'grad_w', 'grad_attn_w_o': 'grad_w', 'grad_attn_b_o': 'grad_w', 'grad_final_norm_w': 'grad_w', 'delta_ffn_norm_w': 'delta_w', 'delta_ffn_w_gu': 'delta_w', 'delta_ffn_w_down': 'delta_w', 'delta_mod_w': 'delta_w', 'delta_mod_b': 'delta_w', 'delta_mix_norm_w': 'delta_w', 'delta_ssm_w_in': 'delta_w', 'delta_ssm_conv_w': 'delta_w', 'delta_ssm_conv_b': 'delta_w', 'delta_ssm_dt_bias': 'delta_w', 'delta_ssm_a_log': 'delta_w', 'delta_ssm_d': 'delta_w', 'delta_ssm_norm_w': 'delta_w', 'delta_ssm_w_out': 'delta_w', 'delta_kv_norm_w': 'delta_w', 'delta_kv_mod_w': 'delta_w', 'delta_kv_mod_b': 'delta_w', 'delta_w_kv': 'delta_w', 'delta_b_kv': 'delta_w', 'delta_attn_w_q': 'delta_w', 'delta_attn_b_q': 'delta_w', 'delta_attn_sinks': 'delta_w', 'delta_attn_w_o': 'delta_w', 'delta_attn_b_o': 'delta_w', 'delta_final_norm_w': 'delta_w', 'new_m_ffn_norm_w': 'new_m', 'new_m_ffn_w_gu': 'new_m', 'new_m_ffn_w_down': 'new_m', 'new_m_mod_w': 'new_m', 'new_m_mod_b': 'new_m', 'new_m_mix_norm_w': 'new_m', 'new_m_ssm_w_in': 'new_m', 'new_m_ssm_conv_w': 'new_m', 'new_m_ssm_conv_b': 'new_m', 'new_m_ssm_dt_bias': 'new_m', 'new_m_ssm_a_log': 'new_m', 'new_m_ssm_d': 'new_m', 'new_m_ssm_norm_w': 'new_m', 'new_m_ssm_w_out': 'new_m', 'new_m_kv_norm_w': 'new_m', 'new_m_kv_mod_w': 'new_m', 'new_m_kv_mod_b': 'new_m', 'new_m_w_kv': 'new_m', 'new_m_b_kv': 'new_m', 'new_m_attn_w_q': 'new_m', 'new_m_attn_b_q': 'new_m', 'new_m_attn_sinks': 'new_m', 'new_m_attn_w_o': 'new_m', 'new_m_attn_b_o': 'new_m', 'new_m_final_norm_w': 'new_m', 'new_v_ffn_norm_w': 'new_v', 'new_v_ffn_w_gu': 'new_v', 'new_v_ffn_w_down': 'new_v', 'new_v_mod_w': 'new_v', 'new_v_mod_b': 'new_v', 'new_v_mix_norm_w': 'new_v', 'new_v_ssm_w_in': 'new_v', 'new_v_ssm_conv_w': 'new_v', 'new_v_ssm_conv_b': 'new_v', 'new_v_ssm_dt_bias': 'new_v', 'new_v_ssm_a_log': 'new_v', 'new_v_ssm_d': 'new_v', 'new_v_ssm_norm_w': 'new_v', 'new_v_ssm_w_out': 'new_v', 'new_v_kv_norm_w': 'new_v', 'new_v_kv_mod_w': 'new_v', 'new_v_kv_mod_b': 'new_v', 'new_v_w_kv': 'new_v', 'new_v_b_kv': 'new_v', 'new_v_attn_w_q': 'new_v', 'new_v_attn_b_q': 'new_v', 'new_v_attn_sinks': 'new_v', 'new_v_attn_w_o': 'new_v', 'new_v_attn_b_o': 'new_v', 'new_v_final_norm_w': 'new_v'}


def _forward(args):
    return _fwd_reference(*[args[k] for k in FWD_PARAMS])


def _output_shape():
    out = _jax.eval_shape(lambda: _forward(_fwd_setup_inputs(0)))
    return out.shape, out.dtype

N_MICROBATCH = 1
ADAM_LR = 0.001
ADAM_B1 = 0.9
ADAM_B2 = 0.999
ADAM_EPS = 1e-08
ADAM_WD = 0.01
ADAM_STEP = 10
PER_EXAMPLE_BATCH_AXIS = {'x': 0, 'c': 0, 'loss_target': 0}
SHARED_INPUTS = []
_WEIGHT_DTYPES = {'ffn_norm_w': _jnp.float32, 'ffn_w_gu': _jnp.float32, 'ffn_w_down': _jnp.float32, 'mod_w': _jnp.float32, 'mod_b': _jnp.float32, 'mix_norm_w': _jnp.float32, 'ssm_w_in': _jnp.float32, 'ssm_conv_w': _jnp.float32, 'ssm_conv_b': _jnp.float32, 'ssm_dt_bias': _jnp.float32, 'ssm_a_log': _jnp.float32, 'ssm_d': _jnp.float32, 'ssm_norm_w': _jnp.float32, 'ssm_w_out': _jnp.float32, 'kv_norm_w': _jnp.float32, 'kv_mod_w': _jnp.float32, 'kv_mod_b': _jnp.float32, 'w_kv': _jnp.float32, 'b_kv': _jnp.float32, 'attn_w_q': _jnp.float32, 'attn_b_q': _jnp.float32, 'attn_sinks': _jnp.float32, 'attn_w_o': _jnp.float32, 'attn_b_o': _jnp.float32, 'final_norm_w': _jnp.float32}
MOMENT_SCALE = {'ffn_norm_w': 1.851340e-02, 'ffn_w_gu': 8.164225e-03, 'ffn_w_down': 1.332157e-02, 'mod_w': 2.805185e-02, 'mod_b': 4.570695e-02, 'mix_norm_w': 3.953518e-02, 'ssm_w_in': 2.371902e-02, 'ssm_conv_w': 2.054200e-02, 'ssm_conv_b': 2.465526e-02, 'ssm_dt_bias': 6.146926e-02, 'ssm_a_log': 7.961834e-02, 'ssm_d': 1.185990e-01, 'ssm_norm_w': 2.743125e-02, 'ssm_w_out': 3.833606e-02, 'kv_norm_w': 1.672498e-02, 'kv_mod_w': 2.536056e-02, 'kv_mod_b': 4.699774e-02, 'w_kv': 3.068884e-02, 'b_kv': 9.487309e-02, 'attn_w_q': 7.702242e-03, 'attn_b_q': 7.185514e-03, 'attn_sinks': 5.700992e-03, 'attn_w_o': 1.367754e-02, 'attn_b_o': 3.648657e-02, 'final_norm_w': 1.604681e+01}


def _to_microbatches(a, axis):
    t = _jnp.moveaxis(a, axis, 0)
    t = t.reshape((N_MICROBATCH, t.shape[0] // N_MICROBATCH) + t.shape[1:])
    return _jnp.moveaxis(t, 1, axis + 1)


def setup_inputs(seed: int = 0) -> dict:
    inp = _fwd_setup_inputs(seed)
    key = _jax.random.fold_in(_jax.random.key(seed), 7919)
    shape, _ = _output_shape()
    out = dict(inp)
    out["loss_target"] = _jax.random.normal(_jax.random.fold_in(key, 0), shape, _jnp.float32)
    for i, name in enumerate(TWIN_WEIGHTS):
        w = inp[name].astype(_jnp.float32)
        if MOMENT_SCALE is None:
            s = _jnp.sqrt(_jnp.mean(_jnp.square(w)) + 1e-30)
        else:
            s = MOMENT_SCALE[name]
        km, kv = _jax.random.split(_jax.random.fold_in(key, i + 1))
        out[name] = w
        out["m_" + name] = s * _jax.random.normal(km, w.shape, _jnp.float32)
        out["v_" + name] = (s * s) * _jax.random.uniform(kv, w.shape, _jnp.float32, 0.5, 1.5)
    if N_MICROBATCH > 1:
        for name, axis in PER_EXAMPLE_BATCH_AXIS.items():
            out[name] = _to_microbatches(out[name], axis)
    return {'x': out['x'], 'c': out['c'], 'ffn_norm_w': out['ffn_norm_w'], 'ffn_w_gu': out['ffn_w_gu'], 'ffn_w_down': out['ffn_w_down'], 'mod_w': out['mod_w'], 'mod_b': out['mod_b'], 'mix_norm_w': out['mix_norm_w'], 'ssm_w_in': out['ssm_w_in'], 'ssm_conv_w': out['ssm_conv_w'], 'ssm_conv_b': out['ssm_conv_b'], 'ssm_dt_bias': out['ssm_dt_bias'], 'ssm_a_log': out['ssm_a_log'], 'ssm_d': out['ssm_d'], 'ssm_norm_w': out['ssm_norm_w'], 'ssm_w_out': out['ssm_w_out'], 'kv_norm_w': out['kv_norm_w'], 'kv_mod_w': out['kv_mod_w'], 'kv_mod_b': out['kv_mod_b'], 'w_kv': out['w_kv'], 'b_kv': out['b_kv'], 'attn_w_q': out['attn_w_q'], 'attn_b_q': out['attn_b_q'], 'attn_sinks': out['attn_sinks'], 'attn_w_o': out['attn_w_o'], 'attn_b_o': out['attn_b_o'], 'final_norm_w': out['final_norm_w'], 'loss_target': out['loss_target'], 'm_ffn_norm_w': out['m_ffn_norm_w'], 'm_ffn_w_gu': out['m_ffn_w_gu'], 'm_ffn_w_down': out['m_ffn_w_down'], 'm_mod_w': out['m_mod_w'], 'm_mod_b': out['m_mod_b'], 'm_mix_norm_w': out['m_mix_norm_w'], 'm_ssm_w_in': out['m_ssm_w_in'], 'm_ssm_conv_w': out['m_ssm_conv_w'], 'm_ssm_conv_b': out['m_ssm_conv_b'], 'm_ssm_dt_bias': out['m_ssm_dt_bias'], 'm_ssm_a_log': out['m_ssm_a_log'], 'm_ssm_d': out['m_ssm_d'], 'm_ssm_norm_w': out['m_ssm_norm_w'], 'm_ssm_w_out': out['m_ssm_w_out'], 'm_kv_norm_w': out['m_kv_norm_w'], 'm_kv_mod_w': out['m_kv_mod_w'], 'm_kv_mod_b': out['m_kv_mod_b'], 'm_w_kv': out['m_w_kv'], 'm_b_kv': out['m_b_kv'], 'm_attn_w_q': out['m_attn_w_q'], 'm_attn_b_q': out['m_attn_b_q'], 'm_attn_sinks': out['m_attn_sinks'], 'm_attn_w_o': out['m_attn_w_o'], 'm_attn_b_o': out['m_attn_b_o'], 'm_final_norm_w': out['m_final_norm_w'], 'v_ffn_norm_w': out['v_ffn_norm_w'], 'v_ffn_w_gu': out['v_ffn_w_gu'], 'v_ffn_w_down': out['v_ffn_w_down'], 'v_mod_w': out['v_mod_w'], 'v_mod_b': out['v_mod_b'], 'v_mix_norm_w': out['v_mix_norm_w'], 'v_ssm_w_in': out['v_ssm_w_in'], 'v_ssm_conv_w': out['v_ssm_conv_w'], 'v_ssm_conv_b': out['v_ssm_conv_b'], 'v_ssm_dt_bias': out['v_ssm_dt_bias'], 'v_ssm_a_log': out['v_ssm_a_log'], 'v_ssm_d': out['v_ssm_d'], 'v_ssm_norm_w': out['v_ssm_norm_w'], 'v_ssm_w_out': out['v_ssm_w_out'], 'v_kv_norm_w': out['v_kv_norm_w'], 'v_kv_mod_w': out['v_kv_mod_w'], 'v_kv_mod_b': out['v_kv_mod_b'], 'v_w_kv': out['v_w_kv'], 'v_b_kv': out['v_b_kv'], 'v_attn_w_q': out['v_attn_w_q'], 'v_attn_b_q': out['v_attn_b_q'], 'v_attn_sinks': out['v_attn_sinks'], 'v_attn_w_o': out['v_attn_w_o'], 'v_attn_b_o': out['v_attn_b_o'], 'v_final_norm_w': out['v_final_norm_w']}


def _loss(weights, diff, rest, loss_target):
    with _jax.named_scope("forward"):
        args = {**rest, TWIN_DIFF_INPUT: diff, **{k: w.astype(_WEIGHT_DTYPES[k]) for k, w in weights.items()}}
        y = _forward(args)
    with _jax.named_scope("loss_head"):
        err = _jnp.square(y.astype(_jnp.float32) - loss_target)
        return 0.5 * _jnp.sum(_jnp.mean(err, axis=-1)) if err.ndim else 0.5 * err


def _adamw(w, g, m, v):
    m = ADAM_B1 * m + (1.0 - ADAM_B1) * g
    v = ADAM_B2 * v + (1.0 - ADAM_B2) * _jnp.square(g)
    m_hat = m / (1.0 - ADAM_B1 ** ADAM_STEP)
    v_hat = v / (1.0 - ADAM_B2 ** ADAM_STEP)
    delta = -ADAM_LR * (m_hat / (_jnp.sqrt(v_hat) + ADAM_EPS) + ADAM_WD * w)
    return delta, m, v


def reference(x, c, ffn_norm_w, ffn_w_gu, ffn_w_down, mod_w, mod_b, mix_norm_w, ssm_w_in, ssm_conv_w, ssm_conv_b, ssm_dt_bias, ssm_a_log, ssm_d, ssm_norm_w, ssm_w_out, kv_norm_w, kv_mod_w, kv_mod_b, w_kv, b_kv, attn_w_q, attn_b_q, attn_sinks, attn_w_o, attn_b_o, final_norm_w, loss_target, m_ffn_norm_w, m_ffn_w_gu, m_ffn_w_down, m_mod_w, m_mod_b, m_mix_norm_w, m_ssm_w_in, m_ssm_conv_w, m_ssm_conv_b, m_ssm_dt_bias, m_ssm_a_log, m_ssm_d, m_ssm_norm_w, m_ssm_w_out, m_kv_norm_w, m_kv_mod_w, m_kv_mod_b, m_w_kv, m_b_kv, m_attn_w_q, m_attn_b_q, m_attn_sinks, m_attn_w_o, m_attn_b_o, m_final_norm_w, v_ffn_norm_w, v_ffn_w_gu, v_ffn_w_down, v_mod_w, v_mod_b, v_mix_norm_w, v_ssm_w_in, v_ssm_conv_w, v_ssm_conv_b, v_ssm_dt_bias, v_ssm_a_log, v_ssm_d, v_ssm_norm_w, v_ssm_w_out, v_kv_norm_w, v_kv_mod_w, v_kv_mod_b, v_w_kv, v_b_kv, v_attn_w_q, v_attn_b_q, v_attn_sinks, v_attn_w_o, v_attn_b_o, v_final_norm_w):
    given = dict(x=x, c=c, ffn_norm_w=ffn_norm_w, ffn_w_gu=ffn_w_gu, ffn_w_down=ffn_w_down, mod_w=mod_w, mod_b=mod_b, mix_norm_w=mix_norm_w, ssm_w_in=ssm_w_in, ssm_conv_w=ssm_conv_w, ssm_conv_b=ssm_conv_b, ssm_dt_bias=ssm_dt_bias, ssm_a_log=ssm_a_log, ssm_d=ssm_d, ssm_norm_w=ssm_norm_w, ssm_w_out=ssm_w_out, kv_norm_w=kv_norm_w, kv_mod_w=kv_mod_w, kv_mod_b=kv_mod_b, w_kv=w_kv, b_kv=b_kv, attn_w_q=attn_w_q, attn_b_q=attn_b_q, attn_sinks=attn_sinks, attn_w_o=attn_w_o, attn_b_o=attn_b_o, final_norm_w=final_norm_w, loss_target=loss_target, m_ffn_norm_w=m_ffn_norm_w, m_ffn_w_gu=m_ffn_w_gu, m_ffn_w_down=m_ffn_w_down, m_mod_w=m_mod_w, m_mod_b=m_mod_b, m_mix_norm_w=m_mix_norm_w, m_ssm_w_in=m_ssm_w_in, m_ssm_conv_w=m_ssm_conv_w, m_ssm_conv_b=m_ssm_conv_b, m_ssm_dt_bias=m_ssm_dt_bias, m_ssm_a_log=m_ssm_a_log, m_ssm_d=m_ssm_d, m_ssm_norm_w=m_ssm_norm_w, m_ssm_w_out=m_ssm_w_out, m_kv_norm_w=m_kv_norm_w, m_kv_mod_w=m_kv_mod_w, m_kv_mod_b=m_kv_mod_b, m_w_kv=m_w_kv, m_b_kv=m_b_kv, m_attn_w_q=m_attn_w_q, m_attn_b_q=m_attn_b_q, m_attn_sinks=m_attn_sinks, m_attn_w_o=m_attn_w_o, m_attn_b_o=m_attn_b_o, m_final_norm_w=m_final_norm_w, v_ffn_norm_w=v_ffn_norm_w, v_ffn_w_gu=v_ffn_w_gu, v_ffn_w_down=v_ffn_w_down, v_mod_w=v_mod_w, v_mod_b=v_mod_b, v_mix_norm_w=v_mix_norm_w, v_ssm_w_in=v_ssm_w_in, v_ssm_conv_w=v_ssm_conv_w, v_ssm_conv_b=v_ssm_conv_b, v_ssm_dt_bias=v_ssm_dt_bias, v_ssm_a_log=v_ssm_a_log, v_ssm_d=v_ssm_d, v_ssm_norm_w=v_ssm_norm_w, v_ssm_w_out=v_ssm_w_out, v_kv_norm_w=v_kv_norm_w, v_kv_mod_w=v_kv_mod_w, v_kv_mod_b=v_kv_mod_b, v_w_kv=v_w_kv, v_b_kv=v_b_kv, v_attn_w_q=v_attn_w_q, v_attn_b_q=v_attn_b_q, v_attn_sinks=v_attn_sinks, v_attn_w_o=v_attn_w_o, v_attn_b_o=v_attn_b_o, v_final_norm_w=v_final_norm_w)
    weights = {n: given[n] for n in TWIN_WEIGHTS}
    shared = {n: given[n] for n in SHARED_INPUTS}
    per_example = {n: given[n] for n in ['x', 'c']}
    grad_fn = _jax.value_and_grad(_loss, argnums=(0, 1))

    def one_microbatch(ex, loss_target):
        ex = dict(ex)
        diff = ex.pop(TWIN_DIFF_INPUT)
        return grad_fn(weights, diff, {**shared, **ex}, loss_target)

    if N_MICROBATCH == 1:
        loss, (grad_w, grad_x) = one_microbatch(per_example, given["loss_target"])
    else:
        def body(carry, xs):
            loss_sum, grad_sum = carry
            l_k, (gw_k, gx_k) = one_microbatch(xs[0], xs[1])
            with _jax.named_scope("update"):
                return (loss_sum + l_k, _jax.tree.map(_jnp.add, grad_sum, gw_k)), gx_k

        init = (_jnp.zeros((), _jnp.float32), _jax.tree.map(_jnp.zeros_like, weights))
        (loss, grad_w), grad_x = _jax.lax.scan(body, init, (per_example, given["loss_target"]))
    with _jax.named_scope("update"):
        delta_w, new_m, new_v = {}, {}, {}
        for n in TWIN_WEIGHTS:
            delta_w[n], new_m[n], new_v[n] = _adamw(weights[n], grad_w[n], given["m_" + n], given["v_" + n])
    return (loss, grad_x, *[grad_w[n] for n in TWIN_WEIGHTS], *[delta_w[n] for n in TWIN_WEIGHTS],
            *[new_m[n] for n in TWIN_WEIGHTS], *[new_v[n] for n in TWIN_WEIGHTS])
```

```python
import functools

import jax
import jax.numpy as jnp
from jax import lax
from jax.experimental import pallas as pl
from jax.experimental.pallas import tpu as pltpu

F32 = jnp.float32
BF16 = jnp.bfloat16
HIGHEST = lax.Precision.HIGHEST
MESH = pl.DeviceIdType.MESH

EPS = 1e-5
N_MOD = 9
FFN_HALF = 0.5
SSM_HEADDIM = 64
SSM_GROUPS = 8
SSM_STATE = 128
CONV_WIDTH = 4
CHUNK = 128
KV_HEADS = 4
HEAD_DIM = 64
WINDOW = 128
N_CHIPS = 4
N_DEV = 8

ADAM_LR = 0.001
ADAM_B1 = 0.9
ADAM_B2 = 0.999
ADAM_EPS = 1e-08
ADAM_WD = 0.01
ADAM_STEP = 10

LANE = 128


def _pick(n, pref, align, whole_if_small=False):
    best = 0
    t = align
    while t <= min(n, pref):
        if n % t == 0:
            best = t
        t += align
    if best == 0 or (whole_if_small and best < 256 and n <= 2048):
        return n
    return best


def _sigmoid(x):
    return 1.0 / (1.0 + jnp.exp(-x))


def _silu(x):
    return x * _sigmoid(x)


def _dsilu(x):
    s = _sigmoid(x)
    return s * (1.0 + x * (1.0 - s))


def _params(*sem):
    return pltpu.CompilerParams(dimension_semantics=sem)


def mm(a, b, *, mode="nn", reduce_s=False, out_dtype=F32, bias=None, a_lead=(), b_lead=(), out_buf=None,
       out_lead=(), name):
    a_s = a.ndim - len(a_lead) == 3
    b_s = b.ndim - len(b_lead) == 3
    S = a.shape[len(a_lead)] if a_s else (b.shape[len(b_lead)] if b_s else 1)
    a2 = a.shape[-2:]
    b2 = b.shape[-2:]
    if mode == "nn":
        (M, K), (K2, N) = a2, b2
    elif mode == "nt":
        (M, K), (N, K2) = a2, b2
    else:
        (K, M), (K2, N) = a2, b2
    assert K == K2, (a.shape, b.shape, mode)
    batch = (a_s or b_s) and not reduce_s
    sb = S if batch else 1
    sr = S if ((a_s or b_s) and reduce_s) else 1
    tm = _pick(M, 512, LANE if mode == "tn" else 16, True)
    tn = _pick(N, 512, LANE, True)
    tk = _pick(K, 512, LANE if mode != "tn" else 16, True)
    nk = K // tk
    grid = (sb, M // tm, N // tn, sr, nk)

    def s_of(isb, isr):
        return isb if batch else isr

    def a_map(isb, i, j, isr, k):
        idx = (k, i) if mode == "tn" else (i, k)
        return tuple(a_lead) + (((s_of(isb, isr),) + idx) if a_s else idx)

    def b_map(isb, i, j, isr, k):
        idx = (j, k) if mode == "nt" else (k, j)
        return tuple(b_lead) + (((s_of(isb, isr),) + idx) if b_s else idx)

    def o_map(isb, i, j, isr, k):
        return tuple(out_lead) + ((isb, i, j) if batch else (i, j))

    def lead_blk(lead, has_s, blk):
        return (None,) * (len(lead) + (1 if has_s else 0)) + blk

    a_blk = (tk, tm) if mode == "tn" else (tm, tk)
    b_blk = (tn, tk) if mode == "nt" else (tk, tn)
    in_specs = [pl.BlockSpec(lead_blk(a_lead, a_s, a_blk), a_map), pl.BlockSpec(lead_blk(b_lead, b_s, b_blk), b_map)]
    args = [a, b]
    if bias is not None:
        bias_s = bias.ndim == 3
        in_specs.append(
            pl.BlockSpec(
                ((None, 1, tn) if bias_s else (1, tn)),
                (lambda isb, i, j, isr, k: (isb, 0, j)) if bias_s else (lambda isb, i, j, isr, k: (0, j)),
            )
        )
        args.append(bias)
    aliases = {}
    if out_buf is not None:
        in_specs.append(pl.BlockSpec(memory_space=pl.ANY))
        aliases = {len(args): 0}
        args.append(out_buf)
        out_shape = jax.ShapeDtypeStruct(out_buf.shape, out_buf.dtype)
        want = tuple(out_buf.shape[len(out_lead):])
        assert want == ((sb, M, N) if batch else (M, N)), (want, sb, M, N)
    else:
        out_shape = jax.ShapeDtypeStruct(((sb, M, N) if batch else (M, N)), out_dtype)
    dims = {"nn": (((1,), (0,)), ((), ())), "nt": (((1,), (1,)), ((), ())), "tn": (((0,), (0,)), ((), ()))}[mode]
    n_in = len(args)

    def body(*refs):
        a_ref, b_ref = refs[0], refs[1]
        bias_ref = refs[2] if bias is not None else None
        o_ref, acc = refs[n_in], refs[n_in + 1]
        isr = pl.program_id(3)
        k = pl.program_id(4)

        @pl.when((isr == 0) & (k == 0))
        def _():
            acc[...] = jnp.zeros_like(acc)

        acc[...] += lax.dot_general(
            a_ref[...].astype(BF16), b_ref[...].astype(BF16), dims, preferred_element_type=F32
        )

        @pl.when((isr == sr - 1) & (k == nk - 1))
        def _():
            r = acc[...]
            if bias is not None:
                r = r + bias_ref[...]
            o_ref[...] = r.astype(o_ref.dtype)

    return pl.pallas_call(
        body,
        out_shape=out_shape,
        grid=grid,
        in_specs=in_specs,
        out_specs=pl.BlockSpec(lead_blk(out_lead, batch, (tm, tn)), o_map),
        scratch_shapes=[pltpu.VMEM((tm, tn), F32)],
        input_output_aliases=aliases,
        compiler_params=_params("parallel", "parallel", "parallel", "arbitrary", "arbitrary"),
        name=name,
    )(*args)


def norm_mod_fwd(x, nw, sc, sh, *, name):
    L, D = x.shape
    tl = _pick(L, 512, 16)

    def body(x_ref, nw_ref, sc_ref, sh_ref, h_ref):
        xv = x_ref[...]
        r = lax.rsqrt(jnp.mean(xv * xv, axis=-1, keepdims=True) + EPS)
        n = (xv * r) * nw_ref[...]
        h_ref[...] = (n * (1.0 + sc_ref[...]) + sh_ref[...]).astype(h_ref.dtype)

    row = pl.BlockSpec((1, D), lambda i: (0, 0))
    return pl.pallas_call(
        body,
        out_shape=jax.ShapeDtypeStruct((L, D), BF16),
        grid=(L // tl,),
        in_specs=[pl.BlockSpec((tl, D), lambda i: (i, 0)), row, row, row],
        out_specs=pl.BlockSpec((tl, D), lambda i: (i, 0)),
        compiler_params=_params("parallel"),
        name=name,
    )(x, nw, sc, sh)


def norm_mod_bwd(x, nw, sc, dh, dx_in, *, name):
    L, D = x.shape
    tl = _pick(L, 512, 16)

    def body(x_ref, nw_ref, sc_ref, dh_ref, dxi_ref, dx_ref, dnw_ref, dsc_ref, dsh_ref):
        @pl.when(pl.program_id(0) == 0)
        def _():
            dnw_ref[...] = jnp.zeros_like(dnw_ref)
            dsc_ref[...] = jnp.zeros_like(dsc_ref)
            dsh_ref[...] = jnp.zeros_like(dsh_ref)

        xv = x_ref[...]
        dh_v = dh_ref[...]
        r = lax.rsqrt(jnp.mean(xv * xv, axis=-1, keepdims=True) + EPS)
        xhat = xv * r
        nw_v = nw_ref[...]
        n = xhat * nw_v
        dsh_ref[...] += jnp.sum(dh_v, axis=0, keepdims=True)
        dsc_ref[...] += jnp.sum(dh_v * n, axis=0, keepdims=True)
        dn = dh_v * (1.0 + sc_ref[...])
        dnw_ref[...] += jnp.sum(dn * xhat, axis=0, keepdims=True)
        dxhat = dn * nw_v
        dx_ref[...] = dxi_ref[...] + r * (dxhat - xhat * jnp.mean(dxhat * xhat, axis=-1, keepdims=True))

    row = pl.BlockSpec((1, D), lambda i: (0, 0))
    tile = pl.BlockSpec((tl, D), lambda i: (i, 0))
    vec = jax.ShapeDtypeStruct((1, D), F32)
    return pl.pallas_call(
        body,
        out_shape=(jax.ShapeDtypeStruct((L, D), F32), vec, vec, vec),
        grid=(L // tl,),
        in_specs=[tile, row, row, tile, tile],
        out_specs=(tile, row, row, row),
        compiler_params=_params("arbitrary"),
        name=name,
    )(x, nw, sc, dh, dx_in)


def gate_fwd(x, f, gate, scale, *, name):
    L, D = x.shape
    tl = _pick(L, 512, 8)

    def body(x_ref, f_ref, g_ref, o_ref):
        o_ref[...] = x_ref[...] + (scale * g_ref[...]) * f_ref[...]

    tile = pl.BlockSpec((tl, D), lambda i: (i, 0))
    return pl.pallas_call(
        body,
        out_shape=jax.ShapeDtypeStruct((L, D), F32),
        grid=(L // tl,),
        in_specs=[tile, tile, pl.BlockSpec((1, D), lambda i: (0, 0))],
        out_specs=tile,
        compiler_params=_params("parallel"),
        name=name,
    )(x, f, gate)


def gate_bwd(f, dx, gate, scale, *, name):
    L, D = f.shape
    tl = _pick(L, 512, 16)

    def body(f_ref, dx_ref, g_ref, df_ref, dg_ref, dfsum_ref):
        @pl.when(pl.program_id(0) == 0)
        def _():
            dg_ref[...] = jnp.zeros_like(dg_ref)
            dfsum_ref[...] = jnp.zeros_like(dfsum_ref)

        dxv = dx_ref[...]
        df = (scale * g_ref[...]) * dxv
        df_ref[...] = df.astype(df_ref.dtype)
        dfsum_ref[...] += jnp.sum(df, axis=0, keepdims=True)
        dg_ref[...] += scale * jnp.sum(f_ref[...] * dxv, axis=0, keepdims=True)

    tile = pl.BlockSpec((tl, D), lambda i: (i, 0))
    row = pl.BlockSpec((1, D), lambda i: (0, 0))
    vec = jax.ShapeDtypeStruct((1, D), F32)
    return pl.pallas_call(
        body,
        out_shape=(jax.ShapeDtypeStruct((L, D), BF16), vec, vec),
        grid=(L // tl,),
        in_specs=[tile, tile, row],
        out_specs=(tile, row, row),
        compiler_params=_params("arbitrary"),
        name=name,
    )(f, dx, gate)


def colsum(x, *, name):
    L, N = x.shape
    tl = _pick(L, 512, 8)

    def body(x_ref, o_ref):
        @pl.when(pl.program_id(0) == 0)
        def _():
            o_ref[...] = jnp.zeros_like(o_ref)

        o_ref[...] += jnp.sum(x_ref[...], axis=0, keepdims=True)

    return pl.pallas_call(
        body,
        out_shape=jax.ShapeDtypeStruct((1, N), F32),
        grid=(L // tl,),
        in_specs=[pl.BlockSpec((tl, N), lambda i: (i, 0))],
        out_specs=pl.BlockSpec((1, N), lambda i: (0, 0)),
        compiler_params=_params("arbitrary"),
        name=name,
    )(x)


def swiglu_fwd(gu, *, name):
    _, L, T = gu.shape
    tl = _pick(L, 256, 16)

    def body(g_ref, u_ref, a_ref):
        a_ref[...] = (_silu(g_ref[...]) * u_ref[...]).astype(a_ref.dtype)

    return pl.pallas_call(
        body,
        out_shape=jax.ShapeDtypeStruct((2, L, T), BF16),
        grid=(2, L // tl),
        in_specs=[
            pl.BlockSpec((None, tl, T), lambda j, i: (j, i, 0)),
            pl.BlockSpec((None, tl, T), lambda j, i: (j + 2, i, 0)),
        ],
        out_specs=pl.BlockSpec((None, tl, T), lambda j, i: (j, i, 0)),
        compiler_params=_params("parallel", "parallel"),
        name=name,
    )(gu, gu)


def swiglu_bwd(gu, da, *, name):
    _, L, T = gu.shape
    tl = _pick(L, 256, 16)

    def body(g_ref, u_ref, da_ref, d_ref):
        g = g_ref[...]
        dav = da_ref[...]
        d_ref[0] = (dav * u_ref[...] * _dsilu(g)).astype(d_ref.dtype)
        d_ref[1] = (dav * _silu(g)).astype(d_ref.dtype)

    out = pl.pallas_call(
        body,
        out_shape=jax.ShapeDtypeStruct((2, 2, L, T), BF16),
        grid=(2, L // tl),
        in_specs=[
            pl.BlockSpec((None, tl, T), lambda j, i: (j, i, 0)),
            pl.BlockSpec((None, tl, T), lambda j, i: (j + 2, i, 0)),
            pl.BlockSpec((None, tl, T), lambda j, i: (j, i, 0)),
        ],
        out_specs=pl.BlockSpec((2, None, tl, T), lambda j, i: (0, j, i, 0)),
        compiler_params=_params("parallel", "parallel"),
        name=name,
    )(gu, gu, da)
    return out.reshape(4, L, T)


def _shift_down(u, k, rows):
    if k == 0:
        return u
    return jnp.where(rows >= k, pltpu.roll(u, k, 0), 0.0)


def _shift_up(u, k, rows, n):
    if k == 0:
        return u
    return jnp.where(rows < n - k, pltpu.roll(u, n - k, 0), 0.0)


def _conv_pre(u, w_ref, b_ref, rows):
    pre = b_ref[...] + w_ref[CONV_WIDTH - 1 : CONV_WIDTH, :] * u
    for k in range(1, CONV_WIDTH):
        pre = pre + w_ref[CONV_WIDTH - 1 - k : CONV_WIDTH - k, :] * _shift_down(u, k, rows)
    return pre


def conv_fwd(zx, conv_w, conv_b, d_inner, *, name):
    L = zx.shape[0]
    C = conv_w.shape[1]
    tc = 256
    off = d_inner // tc

    def body(u_ref, w_ref, b_ref, o_ref):
        rows = lax.broadcasted_iota(jnp.int32, (L, tc), 0)
        o_ref[...] = _silu(_conv_pre(u_ref[...], w_ref, b_ref, rows))

    return pl.pallas_call(
        body,
        out_shape=jax.ShapeDtypeStruct((L, C), F32),
        grid=(C // tc,),
        in_specs=[
            pl.BlockSpec((L, tc), lambda j: (0, off + j)),
            pl.BlockSpec((CONV_WIDTH, tc), lambda j: (0, j)),
            pl.BlockSpec((1, tc), lambda j: (0, j)),
        ],
        out_specs=pl.BlockSpec((L, tc), lambda j: (0, j)),
        compiler_params=_params("parallel"),
        name=name,
    )(zx, conv_w, conv_b)


def conv_bwd(zx, conv_w, conv_b, dxbc, d_inner, *, name):
    L = zx.shape[0]
    C = conv_w.shape[1]
    tc = 256
    off = d_inner // tc

    def body(u_ref, w_ref, b_ref, d_ref, du_ref, dw_ref, db_ref):
        rows = lax.broadcasted_iota(jnp.int32, (L, tc), 0)
        u = u_ref[...]
        dpre = d_ref[...] * _dsilu(_conv_pre(u, w_ref, b_ref, rows))
        db_ref[...] = jnp.sum(dpre, axis=0, keepdims=True)
        du = w_ref[CONV_WIDTH - 1 : CONV_WIDTH, :] * dpre
        dw_ref[CONV_WIDTH - 1 : CONV_WIDTH, :] = jnp.sum(dpre * u, axis=0, keepdims=True)
        for k in range(1, CONV_WIDTH):
            j = CONV_WIDTH - 1 - k
            dw_ref[j : j + 1, :] = jnp.sum(dpre * _shift_down(u, k, rows), axis=0, keepdims=True)
            du = du + w_ref[j : j + 1, :] * _shift_up(dpre, k, rows, L)
        du_ref[...] = du

    return pl.pallas_call(
        body,
        out_shape=(
            jax.ShapeDtypeStruct((L, C), F32),
            jax.ShapeDtypeStruct((CONV_WIDTH, C), F32),
            jax.ShapeDtypeStruct((1, C), F32),
        ),
        grid=(C // tc,),
        in_specs=[
            pl.BlockSpec((L, tc), lambda j: (0, off + j)),
            pl.BlockSpec((CONV_WIDTH, tc), lambda j: (0, j)),
            pl.BlockSpec((1, tc), lambda j: (0, j)),
            pl.BlockSpec((L, tc), lambda j: (0, j)),
        ],
        out_specs=(
            pl.BlockSpec((L, tc), lambda j: (0, j)),
            pl.BlockSpec((CONV_WIDTH, tc), lambda j: (0, j)),
            pl.BlockSpec((1, tc), lambda j: (0, j)),
        ),
        compiler_params=_params("parallel"),
        name=name,
    )(zx, conv_w, conv_b, dxbc)


def _ssd_head(xs, dt, a_neg, dsk, bm, cm, prev):
    q = xs.shape[0]
    p = xs.shape[1]
    li = lax.broadcasted_iota(jnp.int32, (q, q), 0)
    si = lax.broadcasted_iota(jnp.int32, (q, q), 1)
    causal = li >= si
    a_b = jnp.broadcast_to(dt * a_neg, (q, q))
    acs = jnp.dot(causal.astype(F32), a_b, precision=HIGHEST, preferred_element_type=F32)
    tot = jnp.dot(jnp.ones((q, q), F32), a_b, precision=HIGHEST, preferred_element_type=F32)
    tot_p = jnp.dot(jnp.ones((p, q), F32), a_b, precision=HIGHEST, preferred_element_type=F32)
    lmat = jnp.exp(jnp.where(causal, acs - acs.T, -jnp.inf))
    cb = lax.dot_general(cm.astype(BF16), bm.astype(BF16), (((1,), (1,)), ((), ())), preferred_element_type=F32)
    xdt = xs * dt
    y = jnp.dot((cb * lmat).astype(BF16), xdt.astype(BF16), preferred_element_type=F32)
    y = y + lax.dot_general(
        (cm * jnp.exp(acs)).astype(BF16), prev.astype(BF16), (((1,), (1,)), ((), ())), preferred_element_type=F32
    )
    y = y + dsk * xs
    st = lax.dot_general(
        xdt.astype(BF16), (bm * jnp.exp(tot - acs)).astype(BF16), (((0,), (0,)), ((), ())), preferred_element_type=F32
    )
    return y, prev * jnp.exp(tot_p) + st


def _pick_lane(v, h):
    lanes = lax.broadcasted_iota(jnp.int32, v.shape, 1)
    return jnp.sum(jnp.where(lanes == h, v, 0.0), axis=1, keepdims=True)


def _softplus(x):
    return jnp.maximum(x, 0.0) + jnp.log(1.0 + jnp.exp(-jnp.abs(x)))


def _ssd_specs(L, d_inner, H, nc, rev):
    R = H // SSM_GROUPS
    P, N, Q = SSM_HEADDIM, SSM_STATE, CHUNK
    ngrp = SSM_GROUPS

    def ci(c):
        return (nc - 1 - c) if rev else c

    b_off = d_inner // N
    c_off = b_off + ngrp
    xs = pl.BlockSpec((Q, R * P), lambda c, g: (ci(c), g))
    bm = pl.BlockSpec((Q, N), lambda c, g: (ci(c), b_off + g))
    cm = pl.BlockSpec((Q, N), lambda c, g: (ci(c), c_off + g))
    dt = pl.BlockSpec((Q, H), lambda c, g: (ci(c), 0))
    hv = pl.BlockSpec((1, H), lambda c, g: (0, 0))
    y = pl.BlockSpec((Q, R * P), lambda c, g: (ci(c), g))
    st = pl.BlockSpec((None, R * P, N), lambda c, g: (ci(c), g, 0))
    return R, xs, bm, cm, dt, hv, y, st


def ssd_fwd(xbc, dt_raw, dt_bias, a_log, d_skip, d_inner, *, name):
    L = xbc.shape[0]
    H = dt_raw.shape[1]
    nc = L // CHUNK
    P, N = SSM_HEADDIM, SSM_STATE
    R, xs_s, bm_s, cm_s, dt_s, hv_s, y_s, st_s = _ssd_specs(L, d_inner, H, nc, False)

    def body(xs_ref, bm_ref, cm_ref, dt_ref, bias_ref, alog_ref, dsk_ref, y_ref, st_ref, state):
        c = pl.program_id(0)
        g = pl.program_id(1)

        @pl.when(c == 0)
        def _():
            for r in range(R):
                state[g * R + r] = jnp.zeros((P, N), F32)

        dtb = _softplus(dt_ref[...] + bias_ref[...])
        a_all = -jnp.exp(alog_ref[...])
        for r in range(R):
            h = g * R + r
            prev = state[h]
            st_ref[r * P : (r + 1) * P, :] = prev
            y, new = _ssd_head(
                xs_ref[:, r * P : (r + 1) * P],
                _pick_lane(dtb, h),
                _pick_lane(a_all, h),
                _pick_lane(dsk_ref[...], h),
                bm_ref[...],
                cm_ref[...],
                prev,
            )
            y_ref[:, r * P : (r + 1) * P] = y
            state[h] = new

    return pl.pallas_call(
        body,
        out_shape=(jax.ShapeDtypeStruct((L, d_inner), F32), jax.ShapeDtypeStruct((nc, H * P, N), F32)),
        grid=(nc, SSM_GROUPS),
        in_specs=[xs_s, bm_s, cm_s, dt_s, hv_s, hv_s, hv_s],
        out_specs=(y_s, st_s),
        scratch_shapes=[pltpu.VMEM((H, P, N), F32)],
        compiler_params=_params("arbitrary", "arbitrary"),
        name=name,
    )(xbc, xbc, xbc, dt_raw, dt_bias, a_log, d_skip)


def ssd_bwd(xbc, dt_raw, dt_bias, a_log, d_skip, states, dy, d_inner, *, name):
    L, C = xbc.shape
    H = dt_raw.shape[1]
    nc = L // CHUNK
    P, N, Q = SSM_HEADDIM, SSM_STATE, CHUNK
    R, xs_s, bm_s, cm_s, dt_s, hv_s, y_s, st_s = _ssd_specs(L, d_inner, H, nc, True)

    def body(xs_ref, bm_ref, cm_ref, dt_ref, bias_ref, alog_ref, dsk_ref, st_ref, dy_ref,
             dxs_ref, dbm_ref, dcm_ref, ddt_ref, dbias_ref, dalog_ref, ddsk_ref, dstate):
        c = pl.program_id(0)
        g = pl.program_id(1)

        @pl.when(c == 0)
        def _():
            for r in range(R):
                dstate[g * R + r] = jnp.zeros((P, N), F32)

        @pl.when((c == 0) & (g == 0))
        def _():
            dbias_ref[...] = jnp.zeros_like(dbias_ref)
            dalog_ref[...] = jnp.zeros_like(dalog_ref)
            ddsk_ref[...] = jnp.zeros_like(ddsk_ref)

        @pl.when(g == 0)
        def _():
            ddt_ref[...] = jnp.zeros_like(ddt_ref)

        pre = dt_ref[...] + bias_ref[...]
        dtb = _softplus(pre)
        a_all = -jnp.exp(alog_ref[...])
        lanes_q = lax.broadcasted_iota(jnp.int32, (Q, H), 1)
        lanes_1 = lax.broadcasted_iota(jnp.int32, (1, H), 1)
        bm = bm_ref[...]
        cm = cm_ref[...]
        dbm = jnp.zeros((Q, N), F32)
        dcm = jnp.zeros((Q, N), F32)
        ddt_blk = jnp.zeros((Q, H), F32)
        da_row = jnp.zeros((1, H), F32)
        dd_row = jnp.zeros((1, H), F32)
        for r in range(R):
            h = g * R + r
            args = (
                xs_ref[:, r * P : (r + 1) * P],
                _pick_lane(dtb, h),
                _pick_lane(a_all, h),
                _pick_lane(dsk_ref[...], h),
                bm,
                cm,
                st_ref[r * P : (r + 1) * P, :],
            )
            _, vjp = jax.vjp(_ssd_head, *args)
            dxs, ddt, da, dd, dbm_h, dcm_h, dprev = vjp((dy_ref[:, r * P : (r + 1) * P], dstate[h]))
            dxs_ref[:, r * P : (r + 1) * P] = dxs
            dstate[h] = dprev
            dbm = dbm + dbm_h
            dcm = dcm + dcm_h
            ddt_blk = ddt_blk + jnp.where(lanes_q == h, ddt, 0.0)
            da_row = da_row + jnp.where(lanes_1 == h, da, 0.0)
            dd_row = dd_row + jnp.where(lanes_1 == h, dd, 0.0)
        dbm_ref[...] = dbm
        dcm_ref[...] = dcm
        ddt_pre = ddt_blk * _sigmoid(pre)
        ddt_ref[...] += ddt_pre
        dbias_ref[...] += jnp.sum(ddt_pre, axis=0, keepdims=True)
        dalog_ref[...] += da_row * a_all
        ddsk_ref[...] += dd_row

    ngrp = SSM_GROUPS
    hrow = jax.ShapeDtypeStruct((1, H), F32)
    dxs, dbm, dcm, ddt, dbias, dalog, ddsk = pl.pallas_call(
        body,
        out_shape=(
            jax.ShapeDtypeStruct((L, d_inner), F32),
            jax.ShapeDtypeStruct((L, ngrp * N), F32),
            jax.ShapeDtypeStruct((L, ngrp * N), F32),
            jax.ShapeDtypeStruct((L, H), F32),
            hrow,
            hrow,
            hrow,
        ),
        grid=(nc, ngrp),
        in_specs=[xs_s, bm_s, cm_s, dt_s, hv_s, hv_s, hv_s, st_s, y_s],
        out_specs=(
            y_s,
            pl.BlockSpec((Q, N), lambda c, g: (nc - 1 - c, g)),
            pl.BlockSpec((Q, N), lambda c, g: (nc - 1 - c, g)),
            dt_s,
            hv_s,
            hv_s,
            hv_s,
        ),
        scratch_shapes=[pltpu.VMEM((H, P, N), F32)],
        compiler_params=_params("arbitrary", "arbitrary"),
        name=name,
    )(xbc, xbc, xbc, dt_raw, dt_bias, a_log, d_skip, states, dy)
    return jnp.concatenate([dxs, dbm, dcm], axis=1), ddt, dbias, dalog, ddsk


def gnorm_fwd(y, zx, nw, *, name):
    L, DI = y.shape
    gw = DI // SSM_GROUPS
    tl = _pick(L, 512, 16)

    def body(y_ref, z_ref, nw_ref, o_ref):
        yz = y_ref[...] * _silu(z_ref[...])
        r = lax.rsqrt(jnp.mean(yz * yz, axis=-1, keepdims=True) + EPS)
        o_ref[...] = ((yz * r) * nw_ref[...]).astype(o_ref.dtype)

    tile = pl.BlockSpec((tl, gw), lambda i, g: (i, g))
    return pl.pallas_call(
        body,
        out_shape=jax.ShapeDtypeStruct((L, DI), BF16),
        grid=(L // tl, SSM_GROUPS),
        in_specs=[tile, tile, pl.BlockSpec((1, gw), lambda i, g: (0, g))],
        out_specs=tile,
        compiler_params=_params("parallel", "parallel"),
        name=name,
    )(y, zx, nw)


def gnorm_bwd(y, zx, nw, dout, *, name):
    L, DI = y.shape
    gw = DI // SSM_GROUPS
    tl = _pick(L, 512, 16)

    def body(y_ref, z_ref, nw_ref, do_ref, dy_ref, dz_ref, dnw_ref):
        @pl.when(pl.program_id(1) == 0)
        def _():
            dnw_ref[...] = jnp.zeros_like(dnw_ref)

        yv = y_ref[...]
        zv = z_ref[...]
        sz = _silu(zv)
        yz = yv * sz
        r = lax.rsqrt(jnp.mean(yz * yz, axis=-1, keepdims=True) + EPS)
        n = yz * r
        dov = do_ref[...]
        dnw_ref[...] += jnp.sum(dov * n, axis=0, keepdims=True)
        dn = dov * nw_ref[...]
        dyz = r * (dn - n * jnp.mean(dn * n, axis=-1, keepdims=True))
        dy_ref[...] = dyz * sz
        dz_ref[...] = dyz * yv * _dsilu(zv)

    tile = pl.BlockSpec((tl, gw), lambda g, i: (i, g))
    row = pl.BlockSpec((1, gw), lambda g, i: (0, g))
    return pl.pallas_call(
        body,
        out_shape=(
            jax.ShapeDtypeStruct((L, DI), F32),
            jax.ShapeDtypeStruct((L, DI), F32),
            jax.ShapeDtypeStruct((1, DI), F32),
        ),
        grid=(SSM_GROUPS, L // tl),
        in_specs=[tile, tile, row, tile],
        out_specs=(tile, tile, row),
        compiler_params=_params("parallel", "arbitrary"),
        name=name,
    )(y, zx, nw, dout)


def _attn_head(q, kp, kc, vp, vc, sink, has_prev):
    w = q.shape[0]
    nt = (((1,), (1,)), ((), ()))
    qb = q.astype(BF16)
    sc = lax.dot_general(qb, kc.astype(BF16), nt, preferred_element_type=F32) * HEAD_DIM ** -0.5
    sp = lax.dot_general(qb, kp.astype(BF16), nt, preferred_element_type=F32) * HEAD_DIM ** -0.5
    ii = lax.broadcasted_iota(jnp.int32, (w, w), 0)
    jj = lax.broadcasted_iota(jnp.int32, (w, w), 1)
    lc = jnp.where(jj <= ii, sc, -jnp.inf)
    lp = jnp.where((jj > ii) & has_prev, sp, -jnp.inf)
    m = jnp.maximum(jnp.maximum(jnp.max(lc, axis=1, keepdims=True), jnp.max(lp, axis=1, keepdims=True)), sink)
    m = lax.stop_gradient(m)
    pc = jnp.exp(lc - m)
    pp = jnp.exp(lp - m)
    denom = jnp.sum(pc, axis=1, keepdims=True) + jnp.sum(pp, axis=1, keepdims=True) + jnp.exp(sink - m)
    o = jnp.dot((pc / denom).astype(BF16), vc.astype(BF16), preferred_element_type=F32)
    return o + jnp.dot((pp / denom).astype(BF16), vp.astype(BF16), preferred_element_type=F32)


def attn_fwd(q, kv, sinks, *, name):
    L, DQ = q.shape
    heads = DQ // HEAD_DIM
    rep = heads // KV_HEADS
    nb = L // WINDOW
    kw = KV_HEADS * HEAD_DIM
    W, HD = WINDOW, HEAD_DIM

    def body(q_ref, kp_ref, kc_ref, vp_ref, vc_ref, s_ref, o_ref):
        has_prev = pl.program_id(0) > 0
        for h in range(heads):
            kh = h // rep
            ks = slice(kh * HD, (kh + 1) * HD)
            o = _attn_head(
                q_ref[:, h * HD : (h + 1) * HD], kp_ref[:, ks], kc_ref[:, ks], vp_ref[:, ks], vc_ref[:, ks],
                s_ref[:, h : h + 1], has_prev,
            )
            o_ref[:, h * HD : (h + 1) * HD] = o.astype(o_ref.dtype)

    return pl.pallas_call(
        body,
        out_shape=jax.ShapeDtypeStruct((L, DQ), BF16),
        grid=(nb,),
        in_specs=[
            pl.BlockSpec((W, DQ), lambda n: (n, 0)),
            pl.BlockSpec((W, kw), lambda n: (jnp.maximum(n - 1, 0), 0)),
            pl.BlockSpec((W, kw), lambda n: (n, 0)),
            pl.BlockSpec((W, kw), lambda n: (jnp.maximum(n - 1, 0), 1)),
            pl.BlockSpec((W, kw), lambda n: (n, 1)),
            pl.BlockSpec((1, heads), lambda n: (0, 0)),
        ],
        out_specs=pl.BlockSpec((W, DQ), lambda n: (n, 0)),
        compiler_params=_params("parallel"),
        name=name,
    )(q, kv, kv, kv, kv, sinks)


def attn_bwd(q, kv, sinks, do, dkv_in, *, name):
    L, DQ = q.shape
    heads = DQ // HEAD_DIM
    rep = heads // KV_HEADS
    nb = L // WINDOW
    kw = KV_HEADS * HEAD_DIM
    W, HD = WINDOW, HEAD_DIM

    def blk(n):
        return jnp.minimum(n, nb - 1)

    def prev(n):
        return jnp.maximum(blk(n) - 1, 0)

    def outb(n):
        return jnp.maximum(n - 1, 0)

    def body(q_ref, kp_ref, kc_ref, vp_ref, vc_ref, s_ref, do_ref, dki_ref, dvi_ref,
             dq_ref, dk_ref, dv_ref, ds_ref, dk_cur, dv_cur):
        n = pl.program_id(0)
        has_prev = n > 0

        @pl.when(n == 0)
        def _():
            ds_ref[...] = jnp.zeros_like(ds_ref)
            dk_cur[...] = jnp.zeros_like(dk_cur)
            dv_cur[...] = jnp.zeros_like(dv_cur)

        @pl.when(n == nb)
        def _():
            dk_ref[...] = dki_ref[...] + dk_cur[...]
            dv_ref[...] = dvi_ref[...] + dv_cur[...]

        @pl.when(n < nb)
        def _():
            lanes = lax.broadcasted_iota(jnp.int32, (1, heads), 1)
            ds_row = jnp.zeros((1, heads), F32)
            for kh in range(KV_HEADS):
                ks = slice(kh * HD, (kh + 1) * HD)
                kp, kc, vp, vc = kp_ref[:, ks], kc_ref[:, ks], vp_ref[:, ks], vc_ref[:, ks]
                dkp = jnp.zeros((W, HD), F32)
                dkc = jnp.zeros((W, HD), F32)
                dvp = jnp.zeros((W, HD), F32)
                dvc = jnp.zeros((W, HD), F32)
                for rr in range(rep):
                    h = kh * rep + rr
                    hs = slice(h * HD, (h + 1) * HD)
                    _, vjp = jax.vjp(
                        functools.partial(_attn_head, has_prev=has_prev),
                        q_ref[:, hs], kp, kc, vp, vc, s_ref[:, h : h + 1],
                    )
                    dq, a, b, c, d, dsk = vjp(do_ref[:, hs])
                    dq_ref[:, hs] = dq
                    dkp, dkc, dvp, dvc = dkp + a, dkc + b, dvp + c, dvc + d
                    ds_row = ds_row + jnp.where(lanes == h, dsk, 0.0)
                dk_ref[:, ks] = dki_ref[:, ks] + dk_cur[:, ks] + dkp
                dv_ref[:, ks] = dvi_ref[:, ks] + dv_cur[:, ks] + dvp
                dk_cur[:, ks] = dkc
                dv_cur[:, ks] = dvc
            ds_ref[...] += ds_row

    dq, dk, dv, ds = pl.pallas_call(
        body,
        out_shape=(
            jax.ShapeDtypeStruct((L, DQ), F32),
            jax.ShapeDtypeStruct((L, kw), F32),
            jax.ShapeDtypeStruct((L, kw), F32),
            jax.ShapeDtypeStruct((1, heads), F32),
        ),
        grid=(nb + 1,),
        in_specs=[
            pl.BlockSpec((W, DQ), lambda n: (blk(n), 0)),
            pl.BlockSpec((W, kw), lambda n: (prev(n), 0)),
            pl.BlockSpec((W, kw), lambda n: (blk(n), 0)),
            pl.BlockSpec((W, kw), lambda n: (prev(n), 1)),
            pl.BlockSpec((W, kw), lambda n: (blk(n), 1)),
            pl.BlockSpec((1, heads), lambda n: (0, 0)),
            pl.BlockSpec((W, DQ), lambda n: (blk(n), 0)),
            pl.BlockSpec((W, kw), lambda n: (outb(n), 0)),
            pl.BlockSpec((W, kw), lambda n: (outb(n), 1)),
        ],
        out_specs=(
            pl.BlockSpec((W, DQ), lambda n: (blk(n), 0)),
            pl.BlockSpec((W, kw), lambda n: (outb(n), 0)),
            pl.BlockSpec((W, kw), lambda n: (outb(n), 0)),
            pl.BlockSpec((1, heads), lambda n: (0, 0)),
        ),
        scratch_shapes=[pltpu.VMEM((W, kw), F32), pltpu.VMEM((W, kw), F32)],
        compiler_params=_params("arbitrary"),
        name=name,
    )(q, kv, kv, kv, kv, sinks, do, dkv_in, dkv_in)
    return dq, jnp.concatenate([dk, dv], axis=1), ds


def final_loss(x, fw, target, *, name):
    L, D = x.shape
    tl = _pick(L, 512, 8)

    def body(x_ref, fw_ref, t_ref, loss_ref, dx_ref, dfw_ref):
        @pl.when(pl.program_id(0) == 0)
        def _():
            loss_ref[...] = jnp.zeros_like(loss_ref)
            dfw_ref[...] = jnp.zeros_like(dfw_ref)

        xv = x_ref[...]
        fwv = fw_ref[...]
        r = lax.rsqrt(jnp.mean(xv * xv, axis=-1, keepdims=True) + EPS)
        xhat = xv * r
        err = xhat * fwv - t_ref[...]
        loss_ref[...] += 0.5 * jnp.sum(jnp.mean(err * err, axis=-1, keepdims=True), axis=0, keepdims=True)
        dy = err * (1.0 / D)
        dfw_ref[...] += jnp.sum(dy * xhat, axis=0, keepdims=True)
        dxhat = dy * fwv
        dx_ref[...] = r * (dxhat - xhat * jnp.mean(dxhat * xhat, axis=-1, keepdims=True))

    tile = pl.BlockSpec((tl, D), lambda i: (i, 0))
    row = pl.BlockSpec((1, D), lambda i: (0, 0))
    return pl.pallas_call(
        body,
        out_shape=(
            jax.ShapeDtypeStruct((1, 1), F32),
            jax.ShapeDtypeStruct((L, D), F32),
            jax.ShapeDtypeStruct((1, D), F32),
        ),
        grid=(L // tl,),
        in_specs=[tile, row, tile],
        out_specs=(pl.BlockSpec((1, 1), lambda i: (0, 0)), tile, row),
        compiler_params=_params("arbitrary"),
        name=name,
    )(x, fw, target)


def outer8(ct, d, *, name):
    D, B = ct.shape
    S, _, N = d.shape
    tm = _pick(D, 512, 8)
    tn = _pick(N, 256, LANE)

    def body(c_ref, d_ref, o_ref):
        acc = c_ref[:, 0:1] * d_ref[0:1, :]
        for b in range(1, B):
            acc = acc + c_ref[:, b : b + 1] * d_ref[b : b + 1, :]
        o_ref[...] = acc

    return pl.pallas_call(
        body,
        out_shape=jax.ShapeDtypeStruct((S, D, N), F32),
        grid=(S, D // tm, N // tn),
        in_specs=[
            pl.BlockSpec((tm, B), lambda s, i, j: (i, 0)),
            pl.BlockSpec((None, B, tn), lambda s, i, j: (s, 0, j)),
        ],
        out_specs=pl.BlockSpec((None, tm, tn), lambda s, i, j: (s, i, j)),
        compiler_params=_params("parallel", "parallel", "parallel"),
        name=name,
    )(ct, d)


def reduce8(g, *, name):
    nd, R, N = g.shape

    def body(g_ref, o_ref):
        acc = g_ref[0]
        for b in range(1, nd):
            acc = acc + g_ref[b]
        o_ref[...] = acc

    return pl.pallas_call(
        body,
        out_shape=jax.ShapeDtypeStruct((R, N), F32),
        name=name,
    )(g)


def _as3(a):
    if a.ndim == 1:
        return a.reshape(1, 1, -1)
    if a.ndim == 2:
        return a.reshape((1,) + a.shape)
    return a.reshape((-1,) + a.shape[-2:])


def adamw(w, g, m, v, *, name):
    shape = w.shape
    w3, g3, m3, v3 = _as3(w), _as3(g), _as3(m), _as3(v)
    B, R, C = w3.shape
    tr = _pick(R, max(8, (1 << 19) // max(C, 1) // 8 * 8), 8)

    def body(w_ref, g_ref, m_ref, v_ref, d_ref, nm_ref, nv_ref):
        gv = g_ref[...]
        mn = ADAM_B1 * m_ref[...] + (1.0 - ADAM_B1) * gv
        vn = ADAM_B2 * v_ref[...] + (1.0 - ADAM_B2) * (gv * gv)
        m_hat = mn / (1.0 - ADAM_B1 ** ADAM_STEP)
        v_hat = vn / (1.0 - ADAM_B2 ** ADAM_STEP)
        d_ref[...] = -ADAM_LR * (m_hat / (jnp.sqrt(v_hat) + ADAM_EPS) + ADAM_WD * w_ref[...])
        nm_ref[...] = mn
        nv_ref[...] = vn

    tile = pl.BlockSpec((None, tr, C), lambda b, i: (b, i, 0))
    sds = jax.ShapeDtypeStruct((B, R, C), F32)
    d, nm, nv = pl.pallas_call(
        body,
        out_shape=(sds, sds, sds),
        grid=(B, R // tr),
        in_specs=[tile, tile, tile, tile],
        out_specs=(tile, tile, tile),
        compiler_params=_params("parallel", "parallel"),
        name=name,
    )(w3, g3, m3, v3)
    return d.reshape(shape), nm.reshape(shape), nv.reshape(shape)


def _place():
    return lax.axis_index("x"), lax.axis_index("y"), lax.axis_index("c")


def _flip(v, bit):
    return (1 - v) if bit else v


def ag8(v, *, act=None, name):
    R, N = v.shape

    def body(v_ref, out_ref, stage, send_sems, recv_sems):
        x, y, c = _place()
        me = 4 * x + 2 * y + c
        val = v_ref[...]
        if act is not None:
            val = act(val)
        stage[...] = val
        out_ref[me] = val
        sends = []
        for k in range(1, N_DEV):
            px, py, pc = _flip(x, k & 4), _flip(y, k & 2), _flip(c, k & 1)
            cp = pltpu.make_async_remote_copy(
                src_ref=stage, dst_ref=out_ref.at[me], send_sem=send_sems.at[k - 1], recv_sem=recv_sems.at[k - 1],
                device_id=(px, py, pc), device_id_type=MESH,
            )
            cp.start()
            sends.append(cp)
        for k in range(1, N_DEV):
            px, py, pc = _flip(x, k & 4), _flip(y, k & 2), _flip(c, k & 1)
            pltpu.make_async_remote_copy(
                src_ref=stage, dst_ref=out_ref.at[4 * px + 2 * py + pc], send_sem=send_sems.at[k - 1],
                recv_sem=recv_sems.at[k - 1], device_id=(px, py, pc), device_id_type=MESH,
            ).wait_recv()
        for cp in sends:
            cp.wait_send()

    return pl.pallas_call(
        body,
        out_shape=jax.ShapeDtypeStruct((N_DEV, R, N), F32),
        in_specs=[pl.BlockSpec(memory_space=pltpu.VMEM)],
        out_specs=pl.BlockSpec(memory_space=pltpu.VMEM),
        scratch_shapes=[
            pltpu.VMEM((R, N), F32),
            pltpu.SemaphoreType.DMA((N_DEV - 1,)),
            pltpu.SemaphoreType.DMA((N_DEV - 1,)),
        ],
        name=name,
    )(v)


def _other_chips(x, y):
    chips = [(1 - x, y), (x, 1 - y), (1 - x, 1 - y)]
    return chips, [2 * px + py for px, py in chips]


_HBM = pl.BlockSpec(memory_space=pltpu.HBM)


def ag_weights(shards, *, name):
    n = len(shards)

    def body(*refs):
        ins, outs = refs[:n], refs[n : 2 * n]
        send_sems, recv_sems, local_sems = refs[2 * n :]
        x, y, c = _place()
        k_me = 2 * x + y
        chips, kidx = _other_chips(x, y)
        sends, locals_ = [], []
        for w in range(n):
            lc = pltpu.make_async_copy(ins[w], outs[w].at[:, :, k_me], local_sems.at[w])
            lc.start()
            locals_.append(lc)
            for j, (px, py) in enumerate(chips):
                cp = pltpu.make_async_remote_copy(
                    src_ref=ins[w].at[c], dst_ref=outs[w].at[c, :, k_me], send_sem=send_sems.at[w, j],
                    recv_sem=recv_sems.at[w, j], device_id=(px, py, c), device_id_type=MESH,
                )
                cp.start()
                sends.append(cp)
        for w in range(n):
            for j, (px, py) in enumerate(chips):
                landed = outs[w].at[c, :, kidx[j]]
                pltpu.make_async_remote_copy(
                    src_ref=ins[w].at[c], dst_ref=landed, send_sem=send_sems.at[w, j], recv_sem=recv_sems.at[w, j],
                    device_id=(px, py, c), device_id_type=MESH,
                ).wait_recv()
                fw = pltpu.make_async_remote_copy(
                    src_ref=landed, dst_ref=landed, send_sem=send_sems.at[w, 3 + j], recv_sem=recv_sems.at[w, 3 + j],
                    device_id=(x, y, 1 - c), device_id_type=MESH,
                )
                fw.start()
                sends.append(fw)
        for w in range(n):
            for j in range(3):
                got = outs[w].at[1 - c, :, kidx[j]]
                pltpu.make_async_remote_copy(
                    src_ref=got, dst_ref=got, send_sem=send_sems.at[w, 3 + j], recv_sem=recv_sems.at[w, 3 + j],
                    device_id=(x, y, 1 - c), device_id_type=MESH,
                ).wait_recv()
        for cp in sends:
            cp.wait_send()
        for lc in locals_:
            lc.wait()

    return pl.pallas_call(
        body,
        out_shape=tuple(jax.ShapeDtypeStruct(s.shape[:2] + (N_CHIPS,) + s.shape[2:], s.dtype) for s in shards),
        in_specs=[_HBM] * n,
        out_specs=tuple([_HBM] * n),
        scratch_shapes=[
            pltpu.SemaphoreType.DMA((n, 6)),
            pltpu.SemaphoreType.DMA((n, 6)),
            pltpu.SemaphoreType.DMA((n,)),
        ],
        name=name,
    )(*shards)


def rs_sibling(grads, *, name):
    n = len(grads)

    def body(*refs):
        ins, outs = refs[:n], refs[n : 2 * n]
        send_sems, recv_sems = refs[2 * n :]
        x, y, c = _place()
        cps = []
        for w in range(n):
            cp = pltpu.make_async_remote_copy(
                src_ref=ins[w].at[1 - c], dst_ref=outs[w], send_sem=send_sems.at[w], recv_sem=recv_sems.at[w],
                device_id=(x, y, 1 - c), device_id_type=MESH,
            )
            cp.start()
            cps.append(cp)
        for cp in cps:
            cp.wait()

    return pl.pallas_call(
        body,
        out_shape=tuple(jax.ShapeDtypeStruct(g.shape[1:], g.dtype) for g in grads),
        in_specs=[_HBM] * n,
        out_specs=tuple([_HBM] * n),
        scratch_shapes=[pltpu.SemaphoreType.DMA((n,)), pltpu.SemaphoreType.DMA((n,))],
        name=name,
    )(*grads)


def rs_chips(parts, *, name):
    n = len(parts)

    def body(*refs):
        ins, outs = refs[:n], refs[n : 2 * n]
        send_sems, recv_sems = refs[2 * n :]
        x, y, c = _place()
        chips, kidx = _other_chips(x, y)
        cps = []
        for w in range(n):
            for j, (px, py) in enumerate(chips):
                cp = pltpu.make_async_remote_copy(
                    src_ref=ins[w].at[:, kidx[j]], dst_ref=outs[w].at[j], send_sem=send_sems.at[w, j],
                    recv_sem=recv_sems.at[w, j], device_id=(px, py, c), device_id_type=MESH,
                )
                cp.start()
                cps.append(cp)
        for cp in cps:
            cp.wait()

    return pl.pallas_call(
        body,
        out_shape=tuple(jax.ShapeDtypeStruct((3, s.shape[0]) + s.shape[2:], s.dtype) for s in parts),
        in_specs=[_HBM] * n,
        out_specs=tuple([_HBM] * n),
        scratch_shapes=[pltpu.SemaphoreType.DMA((n, 3)), pltpu.SemaphoreType.DMA((n, 3))],
        name=name,
    )(*parts)


def rs_share(halves, *, name):
    n = len(halves)

    def body(*refs):
        ins, outs = refs[:n], refs[n : 2 * n]
        send_sems, recv_sems, local_sems = refs[2 * n :]
        x, y, c = _place()
        cps = []
        for w in range(n):
            lc = pltpu.make_async_copy(ins[w], outs[w].at[c], local_sems.at[w])
            lc.start()
            cp = pltpu.make_async_remote_copy(
                src_ref=ins[w], dst_ref=outs[w].at[c], send_sem=send_sems.at[w], recv_sem=recv_sems.at[w],
                device_id=(x, y, 1 - c), device_id_type=MESH,
            )
            cp.start()
            cps.append((lc, cp))
        for w, (lc, cp) in enumerate(cps):
            cp.wait_send()
            pltpu.make_async_remote_copy(
                src_ref=ins[w], dst_ref=outs[w].at[1 - c], send_sem=send_sems.at[w], recv_sem=recv_sems.at[w],
                device_id=(x, y, 1 - c), device_id_type=MESH,
            ).wait_recv()
            lc.wait()

    return pl.pallas_call(
        body,
        out_shape=tuple(jax.ShapeDtypeStruct((2,) + h.shape, h.dtype) for h in halves),
        in_specs=[_HBM] * n,
        out_specs=tuple([_HBM] * n),
        scratch_shapes=[
            pltpu.SemaphoreType.DMA((n,)),
            pltpu.SemaphoreType.DMA((n,)),
            pltpu.SemaphoreType.DMA((n,)),
        ],
        name=name,
    )(*halves)


def _row_tile(R, C):
    return _pick(R, max(16, (1 << 19) // C // 16 * 16), 16)


def rs_add_pair(g, r, kc_idx, *, name):
    _, M, K, R, C = g.shape
    tr = _row_tile(R, C)

    def body(kc_ref, g_ref, r_ref, o_ref):
        o_ref[...] = (g_ref[...].astype(F32) + r_ref[...].astype(F32)).astype(o_ref.dtype)

    blk = pl.BlockSpec((None, None, tr, C), lambda m, k, i, kc: (m, k, i, 0))
    return pl.pallas_call(
        body,
        out_shape=jax.ShapeDtypeStruct((M, K, R, C), BF16),
        grid_spec=pltpu.PrefetchScalarGridSpec(
            num_scalar_prefetch=1,
            grid=(M, K, R // tr),
            in_specs=[pl.BlockSpec((None, None, None, tr, C), lambda m, k, i, kc: (kc[1], m, k, i, 0)), blk],
            out_specs=blk,
        ),
        compiler_params=_params("parallel", "parallel", "parallel"),
        name=name,
    )(kc_idx, g, r)


def rs_add_final(g, r, t, kc_idx, *, name):
    _, M, K, R, C = g.shape
    tr = _row_tile(R, C)

    def body(kc_ref, g_ref, r_ref, t_ref, o_ref):
        acc = g_ref[...].astype(F32) + r_ref[...].astype(F32)
        for j in range(3):
            acc = acc + t_ref[j].astype(F32)
        o_ref[...] = acc

    return pl.pallas_call(
        body,
        out_shape=jax.ShapeDtypeStruct((M, R, C), F32),
        grid_spec=pltpu.PrefetchScalarGridSpec(
            num_scalar_prefetch=1,
            grid=(M, R // tr),
            in_specs=[
                pl.BlockSpec((None, None, None, tr, C), lambda m, i, kc: (kc[1], m, kc[0], i, 0)),
                pl.BlockSpec((None, None, tr, C), lambda m, i, kc: (m, kc[0], i, 0)),
                pl.BlockSpec((3, None, tr, C), lambda m, i, kc: (0, m, i, 0)),
            ],
            out_specs=pl.BlockSpec((None, tr, C), lambda m, i, kc: (m, i, 0)),
        ),
        compiler_params=_params("parallel", "parallel"),
        name=name,
    )(kc_idx, g, r, t)


WEIGHTS = ["ffn_norm_w", "ffn_w_gu", "ffn_w_down", "mod_w", "mod_b", "mix_norm_w", "ssm_w_in", "ssm_conv_w", "ssm_conv_b",
           "ssm_dt_bias", "ssm_a_log", "ssm_d", "ssm_norm_w", "ssm_w_out", "kv_norm_w", "kv_mod_w", "kv_mod_b", "w_kv", "b_kv",
           "attn_w_q", "attn_b_q", "attn_sinks", "attn_w_o", "attn_b_o", "final_norm_w"]
GATHERED = ["ffn_w_gu", "ffn_w_down", "ssm_w_in", "ssm_w_out", "w_kv", "attn_w_q", "attn_w_o"]
COLUMN_PARALLEL = ["mod_w", "kv_mod_w"]
SMALL_SHARDED = ["ffn_norm_w", "ssm_conv_w", "ssm_conv_b", "ssm_norm_w"]
SMALL = [n for n in WEIGHTS if n not in GATHERED and n not in COLUMN_PARALLEL]


def _halved(a):
    if a.ndim == 2:
        return a.reshape(2, 1, a.shape[0] // 2, a.shape[1])
    return a.reshape((2, -1) + a.shape[-2:])


def _pack(arrs, rows=8):
    flat = jnp.concatenate([a.reshape(-1) for a in arrs])
    n = flat.shape[0]
    pad = (-n) % (rows * LANE)
    return jnp.pad(flat, (0, pad)).reshape(rows, -1), n


def _unpack(flat, like):
    out, o = [], 0
    for s in like:
        k = 1
        for d in s:
            k *= d
        out.append(flat[o : o + k].reshape(s))
        o += k
    return out


def kernel(x, c, ffn_norm_w, ffn_w_gu, ffn_w_down, mod_w, mod_b, mix_norm_w, ssm_w_in, ssm_conv_w, ssm_conv_b, ssm_dt_bias, ssm_a_log, ssm_d, ssm_norm_w, ssm_w_out, kv_norm_w, kv_mod_w, kv_mod_b, w_kv, b_kv, attn_w_q, attn_b_q, attn_sinks, attn_w_o, attn_b_o, final_norm_w, loss_target, m_ffn_norm_w, m_ffn_w_gu, m_ffn_w_down, m_mod_w, m_mod_b, m_mix_norm_w, m_ssm_w_in, m_ssm_conv_w, m_ssm_conv_b, m_ssm_dt_bias, m_ssm_a_log, m_ssm_d, m_ssm_norm_w, m_ssm_w_out, m_kv_norm_w, m_kv_mod_w, m_kv_mod_b, m_w_kv, m_b_kv, m_attn_w_q, m_attn_b_q, m_attn_sinks, m_attn_w_o, m_attn_b_o, m_final_norm_w, v_ffn_norm_w, v_ffn_w_gu, v_ffn_w_down, v_mod_w, v_mod_b, v_mix_norm_w, v_ssm_w_in, v_ssm_conv_w, v_ssm_conv_b, v_ssm_dt_bias, v_ssm_a_log, v_ssm_d, v_ssm_norm_w, v_ssm_w_out, v_kv_norm_w, v_kv_mod_w, v_kv_mod_b, v_w_kv, v_b_kv, v_attn_w_q, v_attn_b_q, v_attn_sinks, v_attn_w_o, v_attn_b_o, v_final_norm_w):
    env = dict(locals())
    W = {n: env[n] for n in WEIGHTS}
    MOM = {n: env["m_" + n] for n in WEIGHTS}
    VAR = {n: env["v_" + n] for n in WEIGHTS}

    ax, ay, ac = _place()
    kme = 2 * ax + ay
    me = 4 * ax + 2 * ay + ac
    kc_idx = jnp.stack([kme, ac]).astype(jnp.int32)

    xs = x[0]
    target = loss_target[0]
    L, D = xs.shape
    depth, n_a = ffn_w_gu.shape[0], ssm_w_in.shape[0]
    n_b = depth - n_a
    T = ffn_w_gu.shape[-1]
    DI = ssm_w_out.shape[1] * N_CHIPS
    CI = ssm_w_in.shape[2]
    CC = ssm_conv_w.shape[2] * N_CHIPS
    MW = mod_w.shape[2]
    KW = kv_mod_w.shape[1]
    KVD = w_kv.shape[1]

    def chip_cols(a, width):
        return lax.dynamic_slice_in_dim(a, kme * width, width, axis=a.ndim - 1)

    g_gu, g_dn, g_in, g_out, g_kv, g_q, g_o = ag_weights([_halved(W[n].astype(BF16)) for n in GATHERED], name="ag_weights")
    g_gu = g_gu.reshape(depth, 2, N_CHIPS, D, T)
    g_dn = g_dn.reshape(depth, 2, 2, T, D)
    g_in = g_in.reshape(n_a, N_CHIPS, D, CI)
    g_out = g_out.reshape(n_a, DI, D)
    g_kv = jnp.transpose(g_kv.reshape(2, N_CHIPS, D // 8, KVD), (1, 0, 2, 3)).reshape(D, KVD)
    g_q = g_q.reshape(n_b, D, D)
    g_o = g_o.reshape(n_b, D, D)

    sm_like = [W[n].shape for n in SMALL_SHARDED]
    sm_pack, sm_n = _pack([W[n] for n in SMALL_SHARDED])
    sm_all = ag8(sm_pack, name="ag_small_w")[0::2].reshape(N_CHIPS, -1)[:, :sm_n]
    full = {}
    for n, part in zip(SMALL_SHARDED, zip(*[_unpack(sm_all[k], sm_like) for k in range(N_CHIPS)])):
        full[n] = jnp.concatenate(part, axis=-1)

    c_all = ag8(c, act=_silu, name="ag_c").reshape(N_DEV, D)
    p_mod = mm(c_all, mod_w, bias=chip_cols(mod_b, MW)[:, None, :], name="mod_mm")
    p_kv = mm(c_all, kv_mod_w, bias=chip_cols(kv_mod_b, KW)[None, :], name="kvmod_mm")
    p_all = jnp.concatenate([jnp.transpose(p_mod, (1, 0, 2)).reshape(N_DEV, depth * MW), p_kv], axis=1)
    p_mine = lax.dynamic_index_in_dim(ag8(p_all, name="ag_mod")[0::2], me, axis=1, keepdims=False)
    mod = jnp.transpose(p_mine[:, : depth * MW].reshape(N_CHIPS, depth, MW), (1, 0, 2)).reshape(depth, N_MOD * D)
    kvmod = p_mine[:, depth * MW :].reshape(1, 2 * D)
    mods = [[mod[i : i + 1, j * D : (j + 1) * D] for j in range(N_MOD)] for i in range(depth)]
    kv_shift, kv_scale = kvmod[:, :D], kvmod[:, D:]

    def ffn_fwd(xin, i, j, sh, sc, gt):
        h = norm_mod_fwd(xin, full["ffn_norm_w"][i, j][None], sc, sh, name=f"ffn_norm_{i}_{j}")
        gu = mm(h, g_gu, b_lead=(i, j), name=f"ffn_gu_{i}_{j}")
        a = swiglu_fwd(gu, name=f"ffn_act_{i}_{j}")
        f = mm(a, g_dn, reduce_s=True, b_lead=(i, j), name=f"ffn_down_{i}_{j}")
        return gate_fwd(xin, f, gt, FFN_HALF, name=f"ffn_res_{i}_{j}"), (xin, gu, a, f)

    def ssm_fwd(xin, i, sh, sc, gt):
        h = norm_mod_fwd(xin, mix_norm_w[i][None], sc, sh, name=f"mix_norm_{i}")
        zx4 = mm(h, g_in, b_lead=(i,), name=f"ssm_in_{i}")
        zx = jnp.transpose(zx4, (1, 0, 2)).reshape(L, N_CHIPS * CI)
        xbc = conv_fwd(zx, full["ssm_conv_w"][i], full["ssm_conv_b"][i][None], DI, name=f"ssm_conv_{i}")
        dt_raw = zx[:, DI + CC :]
        y, states = ssd_fwd(xbc, dt_raw, ssm_dt_bias[i][None], ssm_a_log[i][None], ssm_d[i][None], DI, name=f"ssd_{i}")
        yn = gnorm_fwd(y, zx, full["ssm_norm_w"][i][None], name=f"ssm_gnorm_{i}")
        f = mm(yn, g_out, b_lead=(i,), name=f"ssm_out_{i}")
        return gate_fwd(xin, f, gt, 1.0, name=f"mix_res_{i}"), (xin, zx, xbc, dt_raw, y, states, yn, f)

    def att_fwd(xin, i, kv, sh, sc, gt):
        l = i - n_a
        h = norm_mod_fwd(xin, mix_norm_w[i][None], sc, sh, name=f"mix_norm_{i}")
        q = mm(h, g_q, b_lead=(l,), bias=attn_b_q[l][None], name=f"att_q_{i}")
        o = attn_fwd(q, kv, attn_sinks[l][None], name=f"att_{i}")
        f = mm(o, g_o, b_lead=(l,), bias=attn_b_o[l][None], name=f"att_o_{i}")
        return gate_fwd(xin, f, gt, 1.0, name=f"mix_res_{i}"), (xin, q, o, f)

    saved = []
    xc = xs
    kv = x_kv = None
    for i in range(depth):
        sh1, sc1, g1, shm, scm, gm, sh2, sc2, g2 = mods[i]
        if i == n_a:
            x_kv = xc
            hkv = norm_mod_fwd(xc, kv_norm_w[None], kv_scale, kv_shift, name="kv_norm")
            kv = mm(hkv, g_kv, bias=b_kv[None], name="kv_proj")
        xc, s1 = ffn_fwd(xc, i, 0, sh1, sc1, g1)
        xc, sm = ssm_fwd(xc, i, shm, scm, gm) if i < n_a else att_fwd(xc, i, kv, shm, scm, gm)
        xc, s2 = ffn_fwd(xc, i, 1, sh2, sc2, g2)
        saved.append((s1, sm, s2))

    loss_part, dx, d_final = final_loss(xc, final_norm_w[None], target, name="loss_head")
    loss = lax.psum(loss_part[0, 0], ("x", "y", "c"))

    b_gu = jnp.zeros((depth, 2, N_CHIPS, D, T), BF16)
    b_dn = jnp.zeros((depth, 2, 2, T, D), BF16)
    b_in = jnp.zeros((n_a, N_CHIPS, D, CI), BF16)
    b_out = jnp.zeros((n_a, DI, D), BF16)
    b_q = jnp.zeros((n_b, D, D), BF16)
    b_o = jnp.zeros((n_b, D, D), BF16)
    sg = {
        "ffn_norm_w": [[None, None] for _ in range(depth)], "mix_norm_w": [None] * depth, "mod": [None] * depth,
        "ssm_conv_w": [None] * n_a, "ssm_conv_b": [None] * n_a, "ssm_dt_bias": [None] * n_a, "ssm_a_log": [None] * n_a,
        "ssm_d": [None] * n_a, "ssm_norm_w": [None] * n_a, "attn_b_q": [None] * n_b, "attn_sinks": [None] * n_b,
        "attn_b_o": [None] * n_b,
    }

    def ffn_bwd(dxo, i, j, sv, sh, sc, gt):
        nonlocal b_gu, b_dn
        xin, gu, a, f = sv
        df, dgt, _ = gate_bwd(f, dxo, gt, FFN_HALF, name=f"ffn_res_bwd_{i}_{j}")
        da = mm(df, g_dn, mode="nt", b_lead=(i, j), name=f"ffn_down_dx_{i}_{j}")
        b_dn = mm(a, df, mode="tn", out_buf=b_dn, out_lead=(i, j), name=f"ffn_down_dw_{i}_{j}")
        dgu = swiglu_bwd(gu, da, name=f"ffn_act_bwd_{i}_{j}")
        nw = full["ffn_norm_w"][i, j][None]
        h = norm_mod_fwd(xin, nw, sc, sh, name=f"ffn_norm_re_{i}_{j}")
        dh = mm(dgu, g_gu, mode="nt", reduce_s=True, b_lead=(i, j), name=f"ffn_gu_dx_{i}_{j}")
        b_gu = mm(h, dgu, mode="tn", out_buf=b_gu, out_lead=(i, j), name=f"ffn_gu_dw_{i}_{j}")
        dxi, dnw, dsc, dsh = norm_mod_bwd(xin, nw, sc, dh, dxo, name=f"ffn_norm_bwd_{i}_{j}")
        sg["ffn_norm_w"][i][j] = dnw
        return dxi, (dsh, dsc, dgt)

    def ssm_bwd(dxo, i, sv, sh, sc, gt):
        nonlocal b_in, b_out
        xin, zx, xbc, dt_raw, y, states, yn, f = sv
        df, dgt, _ = gate_bwd(f, dxo, gt, 1.0, name=f"mix_res_bwd_{i}")
        dyn = mm(df, g_out, mode="nt", b_lead=(i,), name=f"ssm_out_dx_{i}")
        b_out = mm(yn, df, mode="tn", out_buf=b_out, out_lead=(i,), name=f"ssm_out_dw_{i}")
        dy, dz, dnorm = gnorm_bwd(y, zx, full["ssm_norm_w"][i][None], dyn, name=f"ssm_gnorm_bwd_{i}")
        dxbc, ddt, dbias, dalog, ddsk = ssd_bwd(
            xbc, dt_raw, ssm_dt_bias[i][None], ssm_a_log[i][None], ssm_d[i][None], states, dy, DI, name=f"ssd_bwd_{i}"
        )
        du, dcw, dcb = conv_bwd(zx, full["ssm_conv_w"][i], full["ssm_conv_b"][i][None], dxbc, DI, name=f"ssm_conv_bwd_{i}")
        dzx = jnp.concatenate([dz, du, ddt], axis=1).astype(BF16)
        dzx4 = jnp.transpose(dzx.reshape(L, N_CHIPS, CI), (1, 0, 2))
        nw = mix_norm_w[i][None]
        h = norm_mod_fwd(xin, nw, sc, sh, name=f"mix_norm_re_{i}")
        dh = mm(dzx4, g_in, mode="nt", reduce_s=True, b_lead=(i,), name=f"ssm_in_dx_{i}")
        b_in = mm(h, dzx4, mode="tn", out_buf=b_in, out_lead=(i,), name=f"ssm_in_dw_{i}")
        dxi, dnw, dsc, dsh = norm_mod_bwd(xin, nw, sc, dh, dxo, name=f"mix_norm_bwd_{i}")
        sg["mix_norm_w"][i] = dnw
        sg["ssm_conv_w"][i], sg["ssm_conv_b"][i], sg["ssm_norm_w"][i] = dcw, dcb, dnorm
        sg["ssm_dt_bias"][i], sg["ssm_a_log"][i], sg["ssm_d"][i] = dbias, dalog, ddsk
        return dxi, (dsh, dsc, dgt)

    def att_bwd(dxo, i, sv, dkv, sh, sc, gt):
        nonlocal b_q, b_o
        l = i - n_a
        xin, q, o, f = sv
        df, dgt, dfsum = gate_bwd(f, dxo, gt, 1.0, name=f"mix_res_bwd_{i}")
        do = mm(df, g_o, mode="nt", b_lead=(l,), name=f"att_o_dx_{i}")
        b_o = mm(o, df, mode="tn", out_buf=b_o, out_lead=(l,), name=f"att_o_dw_{i}")
        dq, dkv, dsink = attn_bwd(q, kv, attn_sinks[l][None], do, dkv, name=f"att_bwd_{i}")
        nw = mix_norm_w[i][None]
        h = norm_mod_fwd(xin, nw, sc, sh, name=f"mix_norm_re_{i}")
        dh = mm(dq, g_q, mode="nt", b_lead=(l,), name=f"att_q_dx_{i}")
        b_q = mm(h, dq, mode="tn", out_buf=b_q, out_lead=(l,), name=f"att_q_dw_{i}")
        dxi, dnw, dsc, dsh = norm_mod_bwd(xin, nw, sc, dh, dxo, name=f"mix_norm_bwd_{i}")
        sg["mix_norm_w"][i] = dnw
        sg["attn_b_q"][l], sg["attn_sinks"][l], sg["attn_b_o"][l] = colsum(dq, name=f"att_bq_{i}"), dsink, dfsum
        return dxi, dkv, (dsh, dsc, dgt)

    dkv = jnp.zeros((L, KVD), F32)
    d_kvnorm = d_kvmod = d_bkv = b_kv_w = None
    for i in reversed(range(depth)):
        sh1, sc1, g1, shm, scm, gm, sh2, sc2, g2 = mods[i]
        s1, sm, s2 = saved[i]
        dx, dm2 = ffn_bwd(dx, i, 1, s2, sh2, sc2, g2)
        if i < n_a:
            dx, dmm = ssm_bwd(dx, i, sm, shm, scm, gm)
        else:
            dx, dkv, dmm = att_bwd(dx, i, sm, dkv, shm, scm, gm)
        dx, dm1 = ffn_bwd(dx, i, 0, s1, sh1, sc1, g1)
        sg["mod"][i] = jnp.concatenate(list(dm1) + list(dmm) + list(dm2), axis=1)
        if i == n_a:
            d_bkv = colsum(dkv, name="kv_bias_bwd")
            hkv = norm_mod_fwd(x_kv, kv_norm_w[None], kv_scale, kv_shift, name="kv_norm_re")
            dh = mm(dkv, g_kv, mode="nt", name="kv_proj_dx")
            b_kv_w = mm(hkv, dkv, mode="tn", out_dtype=BF16, name="kv_proj_dw")
            dx, d_kvnorm, dsc, dsh = norm_mod_bwd(x_kv, kv_norm_w[None], kv_scale, dh, dx, name="kv_norm_bwd")
            d_kvmod = jnp.concatenate([dsh, dsc], axis=1)
    grad_x = dx[None]

    b_kv_w = jnp.transpose(b_kv_w.reshape(N_CHIPS, 2, D // 8, KVD), (1, 0, 2, 3))
    parts = [
        b_gu.reshape(2, -1, N_CHIPS, D, T), b_dn.reshape(2, -1, N_CHIPS, T // 2, D), b_in.reshape(2, -1, N_CHIPS, D, CI),
        b_out.reshape(2, -1, N_CHIPS, DI // N_CHIPS, D), b_kv_w.reshape(2, 1, N_CHIPS, D // 8, KVD),
        b_q.reshape(2, -1, N_CHIPS, D // N_CHIPS, D), b_o.reshape(2, -1, N_CHIPS, D // N_CHIPS, D),
    ]
    from_sib = rs_sibling(parts, name="rs_sibling")
    pair = [rs_add_pair(g, r, kc_idx, name=f"rs_pair_{n}") for n, g, r in zip(GATHERED, parts, from_sib)]
    from_chips = rs_chips(pair, name="rs_chips")
    halves = [rs_add_final(g, r, t, kc_idx, name=f"rs_final_{n}") for n, g, r, t in zip(GATHERED, parts, from_sib, from_chips)]
    shared = rs_share(halves, name="rs_share")
    grads = {n: s.reshape(W[n].shape) for n, s in zip(GATHERED, shared)}

    small = {
        "ffn_norm_w": jnp.stack([jnp.stack([r[0] for r in row]) for row in sg["ffn_norm_w"]]),
        "mod_b": jnp.stack([r[0] for r in sg["mod"]]),
        "mix_norm_w": jnp.stack([r[0] for r in sg["mix_norm_w"]]),
        "ssm_conv_w": jnp.stack(sg["ssm_conv_w"]),
        "ssm_conv_b": jnp.stack([r[0] for r in sg["ssm_conv_b"]]),
        "ssm_dt_bias": jnp.stack([r[0] for r in sg["ssm_dt_bias"]]),
        "ssm_a_log": jnp.stack([r[0] for r in sg["ssm_a_log"]]),
        "ssm_d": jnp.stack([r[0] for r in sg["ssm_d"]]),
        "ssm_norm_w": jnp.stack([r[0] for r in sg["ssm_norm_w"]]),
        "kv_norm_w": d_kvnorm[0],
        "kv_mod_b": d_kvmod[0],
        "b_kv": d_bkv[0],
        "attn_b_q": jnp.stack([r[0] for r in sg["attn_b_q"]]),
        "attn_sinks": jnp.stack([r[0] for r in sg["attn_sinks"]]),
        "attn_b_o": jnp.stack([r[0] for r in sg["attn_b_o"]]),
        "final_norm_w": d_final[0],
    }
    small_like = [small[n].shape for n in SMALL]
    sv_pack, sv_n = _pack([small[n] for n in SMALL])
    sv_all = ag8(sv_pack, name="ag_small_g")
    sv_sum = reduce8(sv_all, name="small_g_sum").reshape(-1)[:sv_n]
    for n, gfull in zip(SMALL, _unpack(sv_sum, small_like)):
        grads[n] = chip_cols(gfull, W[n].shape[-1]) if n in SMALL_SHARDED else gfull

    per_dev = [_unpack(sv_all[b].reshape(-1)[:sv_n], small_like) for b in range(N_DEV)]
    i_modb, i_kvb = SMALL.index("mod_b"), SMALL.index("kv_mod_b")
    dmod_all = jnp.stack([chip_cols(p[i_modb], MW) for p in per_dev], axis=1)
    dkv_all = jnp.stack([chip_cols(p[i_kvb], KW) for p in per_dev], axis=0)[None]
    c_t = jnp.transpose(c_all)
    grads["mod_w"] = outer8(c_t, dmod_all, name="mod_w_grad")
    grads["kv_mod_w"] = outer8(c_t, dkv_all, name="kv_mod_w_grad")[0]

    delta, new_m, new_v = {}, {}, {}
    for n in GATHERED + COLUMN_PARALLEL:
        delta[n], new_m[n], new_v[n] = adamw(W[n], grads[n], MOM[n], VAR[n], name=f"adamw_{n}")
    like = [W[n].shape for n in SMALL]
    packs = [_pack([d[n] for n in SMALL])[0] for d in (W, grads, MOM, VAR)]
    n_small = sum(int(W[n].size) for n in SMALL)
    for dst, res in zip((delta, new_m, new_v), adamw(*packs, name="adamw_small")):
        for n, a in zip(SMALL, _unpack(res.reshape(-1)[:n_small], like)):
            dst[n] = a

    return (loss, grad_x, *[grads[n] for n in WEIGHTS], *[delta[n] for n in WEIGHTS], *[new_m[n] for n in WEIGHTS],
            *[new_v[n] for n in WEIGHTS])
```

```python
import functools

import jax
import jax.numpy as jnp
from jax import lax
from jax.experimental import pallas as pl
from jax.experimental.pallas import tpu as pltpu

F32 = jnp.float32
BF16 = jnp.bfloat16
HIGHEST = lax.Precision.HIGHEST
MESH = pl.DeviceIdType.MESH

EPS = 1e-5
N_MOD = 9
FFN_HALF = 0.5
SSM_HEADDIM = 64
SSM_GROUPS = 8
SSM_STATE = 128
CONV_WIDTH = 4
CHUNK = 128
KV_HEADS = 4
HEAD_DIM = 64
WINDOW = 128
N_CHIPS = 4
N_DEV = 8

ADAM_LR = 0.001
ADAM_B1 = 0.9
ADAM_B2 = 0.999
ADAM_EPS = 1e-08
ADAM_WD = 0.01
ADAM_STEP = 10

LANE = 128
MM_TILE = 1024


def _pick(n, pref, align, whole_if_small=False):
    best = 0
    t = align
    while t <= min(n, pref):
        if n % t == 0:
            best = t
        t += align
    if best == 0 or (whole_if_small and best < 256 and n <= 2048):
        return n
    return best


def _sigmoid(x):
    return 1.0 / (1.0 + jnp.exp(-x))


def _silu(x):
    return x * _sigmoid(x)


def _dsilu(x):
    s = _sigmoid(x)
    return s * (1.0 + x * (1.0 - s))


def _params(*sem):
    return pltpu.CompilerParams(dimension_semantics=sem)


def mm(a, b, *, mode="nn", reduce_s=False, out_dtype=F32, bias=None, a_lead=(), b_lead=(), out_buf=None,
       out_lead=(), name):
    a_s = a.ndim - len(a_lead) == 3
    b_s = b.ndim - len(b_lead) == 3
    S = a.shape[len(a_lead)] if a_s else (b.shape[len(b_lead)] if b_s else 1)
    a2 = a.shape[-2:]
    b2 = b.shape[-2:]
    if mode == "nn":
        (M, K), (K2, N) = a2, b2
    elif mode == "nt":
        (M, K), (N, K2) = a2, b2
    else:
        (K, M), (K2, N) = a2, b2
    assert K == K2, (a.shape, b.shape, mode)
    batch = (a_s or b_s) and not reduce_s
    sb = S if batch else 1
    sr = S if ((a_s or b_s) and reduce_s) else 1
    tm = _pick(M, MM_TILE, LANE if mode == "tn" else 16, True)
    tn = _pick(N, MM_TILE, LANE, True)
    tk = _pick(K, MM_TILE, LANE if mode != "tn" else 16, True)
    nk = K // tk
    grid = (sb, M // tm, N // tn, sr, nk)

    def s_of(isb, isr):
        return isb if batch else isr

    def a_map(isb, i, j, isr, k):
        idx = (k, i) if mode == "tn" else (i, k)
        return tuple(a_lead) + (((s_of(isb, isr),) + idx) if a_s else idx)

    def b_map(isb, i, j, isr, k):
        idx = (j, k) if mode == "nt" else (k, j)
        return tuple(b_lead) + (((s_of(isb, isr),) + idx) if b_s else idx)

    def o_map(isb, i, j, isr, k):
        return tuple(out_lead) + ((isb, i, j) if batch else (i, j))

    def lead_blk(lead, has_s, blk):
        return (None,) * (len(lead) + (1 if has_s else 0)) + blk

    a_blk = (tk, tm) if mode == "tn" else (tm, tk)
    b_blk = (tn, tk) if mode == "nt" else (tk, tn)
    in_specs = [pl.BlockSpec(lead_blk(a_lead, a_s, a_blk), a_map), pl.BlockSpec(lead_blk(b_lead, b_s, b_blk), b_map)]
    args = [a, b]
    if bias is not None:
        bias_s = bias.ndim == 3
        in_specs.append(
            pl.BlockSpec(
                ((None, 1, tn) if bias_s else (1, tn)),
                (lambda isb, i, j, isr, k: (isb, 0, j)) if bias_s else (lambda isb, i, j, isr, k: (0, j)),
            )
        )
        args.append(bias)
    aliases = {}
    if out_buf is not None:
        in_specs.append(pl.BlockSpec(memory_space=pl.ANY))
        aliases = {len(args): 0}
        args.append(out_buf)
        out_shape = jax.ShapeDtypeStruct(out_buf.shape, out_buf.dtype)
        want = tuple(out_buf.shape[len(out_lead):])
        assert want == ((sb, M, N) if batch else (M, N)), (want, sb, M, N)
    else:
        out_shape = jax.ShapeDtypeStruct(((sb, M, N) if batch else (M, N)), out_dtype)
    dims = {"nn": (((1,), (0,)), ((), ())), "nt": (((1,), (1,)), ((), ())), "tn": (((0,), (0,)), ((), ()))}[mode]
    n_in = len(args)
    one_step = sr * nk == 1

    def body(*refs):
        a_ref, b_ref = refs[0], refs[1]
        bias_ref = refs[2] if bias is not None else None
        o_ref = refs[n_in]

        def finish(r):
            if bias is not None:
                r = r + bias_ref[...]
            o_ref[...] = r.astype(o_ref.dtype)

        part = lax.dot_general(a_ref[...].astype(BF16), b_ref[...].astype(BF16), dims, preferred_element_type=F32)
        if one_step:
            finish(part)
            return
        acc = refs[n_in + 1]
        isr = pl.program_id(3)
        k = pl.program_id(4)

        @pl.when((isr == 0) & (k == 0))
        def _():
            acc[...] = part

        @pl.when((isr > 0) | (k > 0))
        def _():
            acc[...] += part

        @pl.when((isr == sr - 1) & (k == nk - 1))
        def _():
            finish(acc[...])

    return pl.pallas_call(
        body,
        out_shape=out_shape,
        grid=grid,
        in_specs=in_specs,
        out_specs=pl.BlockSpec(lead_blk(out_lead, batch, (tm, tn)), o_map),
        scratch_shapes=[] if one_step else [pltpu.VMEM((tm, tn), F32)],
        input_output_aliases=aliases,
        compiler_params=_params("parallel", "parallel", "parallel", "arbitrary", "arbitrary"),
        name=name,
    )(*args)


def norm_mod_fwd(x, nw, sc, sh, *, name):
    L, D = x.shape
    tl = _pick(L, 512, 16)

    def body(x_ref, nw_ref, sc_ref, sh_ref, h_ref):
        xv = x_ref[...]
        r = lax.rsqrt(jnp.mean(xv * xv, axis=-1, keepdims=True) + EPS)
        n = (xv * r) * nw_ref[...]
        h_ref[...] = (n * (1.0 + sc_ref[...]) + sh_ref[...]).astype(h_ref.dtype)

    row = pl.BlockSpec((1, D), lambda i: (0, 0))
    return pl.pallas_call(
        body,
        out_shape=jax.ShapeDtypeStruct((L, D), BF16),
        grid=(L // tl,),
        in_specs=[pl.BlockSpec((tl, D), lambda i: (i, 0)), row, row, row],
        out_specs=pl.BlockSpec((tl, D), lambda i: (i, 0)),
        compiler_params=_params("parallel"),
        name=name,
    )(x, nw, sc, sh)


def norm_mod_bwd(x, nw, sc, dh, dx_in, *, name):
    L, D = x.shape
    tl = _pick(L, 512, 16)

    def body(x_ref, nw_ref, sc_ref, dh_ref, dxi_ref, dx_ref, dnw_ref, dsc_ref, dsh_ref):
        @pl.when(pl.program_id(0) == 0)
        def _():
            dnw_ref[...] = jnp.zeros_like(dnw_ref)
            dsc_ref[...] = jnp.zeros_like(dsc_ref)
            dsh_ref[...] = jnp.zeros_like(dsh_ref)

        xv = x_ref[...]
        dh_v = dh_ref[...]
        r = lax.rsqrt(jnp.mean(xv * xv, axis=-1, keepdims=True) + EPS)
        xhat = xv * r
        nw_v = nw_ref[...]
        n = xhat * nw_v
        dsh_ref[...] += jnp.sum(dh_v, axis=0, keepdims=True)
        dsc_ref[...] += jnp.sum(dh_v * n, axis=0, keepdims=True)
        dn = dh_v * (1.0 + sc_ref[...])
        dnw_ref[...] += jnp.sum(dn * xhat, axis=0, keepdims=True)
        dxhat = dn * nw_v
        dx_ref[...] = dxi_ref[...] + r * (dxhat - xhat * jnp.mean(dxhat * xhat, axis=-1, keepdims=True))

    row = pl.BlockSpec((1, D), lambda i: (0, 0))
    tile = pl.BlockSpec((tl, D), lambda i: (i, 0))
    vec = jax.ShapeDtypeStruct((1, D), F32)
    return pl.pallas_call(
        body,
        out_shape=(jax.ShapeDtypeStruct((L, D), F32), vec, vec, vec),
        grid=(L // tl,),
        in_specs=[tile, row, row, tile, tile],
        out_specs=(tile, row, row, row),
        compiler_params=_params("arbitrary"),
        name=name,
    )(x, nw, sc, dh, dx_in)


def gate_fwd(x, f, gate, scale, *, name):
    L, D = x.shape
    tl = _pick(L, 512, 8)

    def body(x_ref, f_ref, g_ref, o_ref):
        o_ref[...] = x_ref[...] + (scale * g_ref[...]) * f_ref[...]

    tile = pl.BlockSpec((tl, D), lambda i: (i, 0))
    return pl.pallas_call(
        body,
        out_shape=jax.ShapeDtypeStruct((L, D), F32),
        grid=(L // tl,),
        in_specs=[tile, tile, pl.BlockSpec((1, D), lambda i: (0, 0))],
        out_specs=tile,
        compiler_params=_params("parallel"),
        name=name,
    )(x, f, gate)


def gate_bwd(f, dx, gate, scale, *, name):
    L, D = f.shape
    tl = _pick(L, 512, 16)

    def body(f_ref, dx_ref, g_ref, df_ref, dg_ref, dfsum_ref):
        @pl.when(pl.program_id(0) == 0)
        def _():
            dg_ref[...] = jnp.zeros_like(dg_ref)
            dfsum_ref[...] = jnp.zeros_like(dfsum_ref)

        dxv = dx_ref[...]
        df = (scale * g_ref[...]) * dxv
        df_ref[...] = df.astype(df_ref.dtype)
        dfsum_ref[...] += jnp.sum(df, axis=0, keepdims=True)
        dg_ref[...] += scale * jnp.sum(f_ref[...] * dxv, axis=0, keepdims=True)

    tile = pl.BlockSpec((tl, D), lambda i: (i, 0))
    row = pl.BlockSpec((1, D), lambda i: (0, 0))
    vec = jax.ShapeDtypeStruct((1, D), F32)
    return pl.pallas_call(
        body,
        out_shape=(jax.ShapeDtypeStruct((L, D), BF16), vec, vec),
        grid=(L // tl,),
        in_specs=[tile, tile, row],
        out_specs=(tile, row, row),
        compiler_params=_params("arbitrary"),
        name=name,
    )(f, dx, gate)


def colsum(x, *, name):
    L, N = x.shape
    tl = _pick(L, 512, 8)

    def body(x_ref, o_ref):
        @pl.when(pl.program_id(0) == 0)
        def _():
            o_ref[...] = jnp.zeros_like(o_ref)

        o_ref[...] += jnp.sum(x_ref[...], axis=0, keepdims=True)

    return pl.pallas_call(
        body,
        out_shape=jax.ShapeDtypeStruct((1, N), F32),
        grid=(L // tl,),
        in_specs=[pl.BlockSpec((tl, N), lambda i: (i, 0))],
        out_specs=pl.BlockSpec((1, N), lambda i: (0, 0)),
        compiler_params=_params("arbitrary"),
        name=name,
    )(x)


def swiglu_fwd(gu, *, name):
    _, L, T = gu.shape
    tl = _pick(L, 256, 16)

    def body(g_ref, u_ref, a_ref):
        a_ref[...] = (_silu(g_ref[...]) * u_ref[...]).astype(a_ref.dtype)

    return pl.pallas_call(
        body,
        out_shape=jax.ShapeDtypeStruct((2, L, T), BF16),
        grid=(2, L // tl),
        in_specs=[
            pl.BlockSpec((None, tl, T), lambda j, i: (j, i, 0)),
            pl.BlockSpec((None, tl, T), lambda j, i: (j + 2, i, 0)),
        ],
        out_specs=pl.BlockSpec((None, tl, T), lambda j, i: (j, i, 0)),
        compiler_params=_params("parallel", "parallel"),
        name=name,
    )(gu, gu)


def swiglu_bwd(gu, da, *, name):
    _, L, T = gu.shape
    tl = _pick(L, 256, 16)

    def body(g_ref, u_ref, da_ref, d_ref):
        g = g_ref[...]
        dav = da_ref[...]
        d_ref[0] = (dav * u_ref[...] * _dsilu(g)).astype(d_ref.dtype)
        d_ref[1] = (dav * _silu(g)).astype(d_ref.dtype)

    out = pl.pallas_call(
        body,
        out_shape=jax.ShapeDtypeStruct((2, 2, L, T), BF16),
        grid=(2, L // tl),
        in_specs=[
            pl.BlockSpec((None, tl, T), lambda j, i: (j, i, 0)),
            pl.BlockSpec((None, tl, T), lambda j, i: (j + 2, i, 0)),
            pl.BlockSpec((None, tl, T), lambda j, i: (j, i, 0)),
        ],
        out_specs=pl.BlockSpec((2, None, tl, T), lambda j, i: (0, j, i, 0)),
        compiler_params=_params("parallel", "parallel"),
        name=name,
    )(gu, gu, da)
    return out.reshape(4, L, T)


def _shift_down(u, k, rows):
    if k == 0:
        return u
    return jnp.where(rows >= k, pltpu.roll(u, k, 0), 0.0)


def _shift_up(u, k, rows, n):
    if k == 0:
        return u
    return jnp.where(rows < n - k, pltpu.roll(u, n - k, 0), 0.0)


def _conv_pre(u, w_ref, b_ref, rows):
    pre = b_ref[...] + w_ref[CONV_WIDTH - 1 : CONV_WIDTH, :] * u
    for k in range(1, CONV_WIDTH):
        pre = pre + w_ref[CONV_WIDTH - 1 - k : CONV_WIDTH - k, :] * _shift_down(u, k, rows)
    return pre


def conv_fwd(zx, conv_w, conv_b, d_inner, *, name):
    L = zx.shape[0]
    C = conv_w.shape[1]
    tc = 256
    off = d_inner // tc

    def body(u_ref, w_ref, b_ref, o_ref):
        rows = lax.broadcasted_iota(jnp.int32, (L, tc), 0)
        o_ref[...] = _silu(_conv_pre(u_ref[...], w_ref, b_ref, rows))

    return pl.pallas_call(
        body,
        out_shape=jax.ShapeDtypeStruct((L, C), F32),
        grid=(C // tc,),
        in_specs=[
            pl.BlockSpec((L, tc), lambda j: (0, off + j)),
            pl.BlockSpec((CONV_WIDTH, tc), lambda j: (0, j)),
            pl.BlockSpec((1, tc), lambda j: (0, j)),
        ],
        out_specs=pl.BlockSpec((L, tc), lambda j: (0, j)),
        compiler_params=_params("parallel"),
        name=name,
    )(zx, conv_w, conv_b)


def conv_bwd(zx, conv_w, conv_b, dxbc, d_inner, *, name):
    L = zx.shape[0]
    C = conv_w.shape[1]
    tc = 256
    off = d_inner // tc

    def body(u_ref, w_ref, b_ref, d_ref, du_ref, dw_ref, db_ref):
        rows = lax.broadcasted_iota(jnp.int32, (L, tc), 0)
        u = u_ref[...]
        dpre = d_ref[...] * _dsilu(_conv_pre(u, w_ref, b_ref, rows))
        db_ref[...] = jnp.sum(dpre, axis=0, keepdims=True)
        du = w_ref[CONV_WIDTH - 1 : CONV_WIDTH, :] * dpre
        dw_ref[CONV_WIDTH - 1 : CONV_WIDTH, :] = jnp.sum(dpre * u, axis=0, keepdims=True)
        for k in range(1, CONV_WIDTH):
            j = CONV_WIDTH - 1 - k
            dw_ref[j : j + 1, :] = jnp.sum(dpre * _shift_down(u, k, rows), axis=0, keepdims=True)
            du = du + w_ref[j : j + 1, :] * _shift_up(dpre, k, rows, L)
        du_ref[...] = du

    return pl.pallas_call(
        body,
        out_shape=(
            jax.ShapeDtypeStruct((L, C), F32),
            jax.ShapeDtypeStruct((CONV_WIDTH, C), F32),
            jax.ShapeDtypeStruct((1, C), F32),
        ),
        grid=(C // tc,),
        in_specs=[
            pl.BlockSpec((L, tc), lambda j: (0, off + j)),
            pl.BlockSpec((CONV_WIDTH, tc), lambda j: (0, j)),
            pl.BlockSpec((1, tc), lambda j: (0, j)),
            pl.BlockSpec((L, tc), lambda j: (0, j)),
        ],
        out_specs=(
            pl.BlockSpec((L, tc), lambda j: (0, j)),
            pl.BlockSpec((CONV_WIDTH, tc), lambda j: (0, j)),
            pl.BlockSpec((1, tc), lambda j: (0, j)),
        ),
        compiler_params=_params("parallel"),
        name=name,
    )(zx, conv_w, conv_b, dxbc)


def _ssd_head(xs, dt, a_neg, dsk, bm, cm, prev):
    q = xs.shape[0]
    li = lax.broadcasted_iota(jnp.int32, (q, q), 0)
    si = lax.broadcasted_iota(jnp.int32, (q, q), 1)
    causal = li >= si
    a = dt * a_neg
    acs = jnp.dot(causal.astype(F32), jnp.broadcast_to(a, (q, q)), precision=HIGHEST, preferred_element_type=F32)
    tot = jnp.sum(a, axis=0, keepdims=True)
    lmat = jnp.exp(jnp.where(causal, acs - acs.T, -jnp.inf))
    cb = lax.dot_general(cm.astype(BF16), bm.astype(BF16), (((1,), (1,)), ((), ())), preferred_element_type=F32)
    xdt = xs * dt
    y = jnp.dot((cb * lmat).astype(BF16), xdt.astype(BF16), preferred_element_type=F32)
    y = y + lax.dot_general(
        (cm * jnp.exp(acs)).astype(BF16), prev.astype(BF16), (((1,), (1,)), ((), ())), preferred_element_type=F32
    )
    y = y + dsk * xs
    st = lax.dot_general(
        xdt.astype(BF16), (bm * jnp.exp(tot - acs)).astype(BF16), (((0,), (0,)), ((), ())), preferred_element_type=F32
    )
    return y, prev * jnp.exp(tot) + st


def _pick_lane(v, h):
    lanes = lax.broadcasted_iota(jnp.int32, v.shape, 1)
    return jnp.sum(jnp.where(lanes == h, v, 0.0), axis=1, keepdims=True)


def _softplus(x):
    return jnp.maximum(x, 0.0) + jnp.log(1.0 + jnp.exp(-jnp.abs(x)))


def _ssd_specs(L, d_inner, H, nc, rev):
    R = H // SSM_GROUPS
    P, N, Q = SSM_HEADDIM, SSM_STATE, CHUNK
    ngrp = SSM_GROUPS

    def ci(c):
        return (nc - 1 - c) if rev else c

    b_off = d_inner // N
    c_off = b_off + ngrp
    xs = pl.BlockSpec((Q, R * P), lambda c, g: (ci(c), g))
    bm = pl.BlockSpec((Q, N), lambda c, g: (ci(c), b_off + g))
    cm = pl.BlockSpec((Q, N), lambda c, g: (ci(c), c_off + g))
    dt = pl.BlockSpec((Q, H), lambda c, g: (ci(c), 0))
    hv = pl.BlockSpec((1, H), lambda c, g: (0, 0))
    y = pl.BlockSpec((Q, R * P), lambda c, g: (ci(c), g))
    st = pl.BlockSpec((None, R * P, N), lambda c, g: (ci(c), g, 0))
    return R, xs, bm, cm, dt, hv, y, st


def ssd_fwd(xbc, dt_raw, dt_bias, a_log, d_skip, d_inner, *, name):
    L = xbc.shape[0]
    H = dt_raw.shape[1]
    nc = L // CHUNK
    P, N = SSM_HEADDIM, SSM_STATE
    R, xs_s, bm_s, cm_s, dt_s, hv_s, y_s, st_s = _ssd_specs(L, d_inner, H, nc, False)

    def body(xs_ref, bm_ref, cm_ref, dt_ref, bias_ref, alog_ref, dsk_ref, y_ref, st_ref, state):
        c = pl.program_id(0)
        g = pl.program_id(1)

        @pl.when(c == 0)
        def _():
            for r in range(R):
                state[g * R + r] = jnp.zeros((P, N), F32)

        dtb = _softplus(dt_ref[...] + bias_ref[...])
        a_all = -jnp.exp(alog_ref[...])
        for r in range(R):
            h = g * R + r
            prev = state[h]
            st_ref[r * P : (r + 1) * P, :] = prev
            y, new = _ssd_head(
                xs_ref[:, r * P : (r + 1) * P],
                _pick_lane(dtb, h),
                _pick_lane(a_all, h),
                _pick_lane(dsk_ref[...], h),
                bm_ref[...],
                cm_ref[...],
                prev,
            )
            y_ref[:, r * P : (r + 1) * P] = y
            state[h] = new

    return pl.pallas_call(
        body,
        out_shape=(jax.ShapeDtypeStruct((L, d_inner), F32), jax.ShapeDtypeStruct((nc, H * P, N), F32)),
        grid=(nc, SSM_GROUPS),
        in_specs=[xs_s, bm_s, cm_s, dt_s, hv_s, hv_s, hv_s],
        out_specs=(y_s, st_s),
        scratch_shapes=[pltpu.VMEM((H, P, N), F32)],
        compiler_params=_params("arbitrary", "arbitrary"),
        name=name,
    )(xbc, xbc, xbc, dt_raw, dt_bias, a_log, d_skip)


def ssd_bwd(xbc, dt_raw, dt_bias, a_log, d_skip, states, dy, d_inner, *, name):
    L, C = xbc.shape
    H = dt_raw.shape[1]
    nc = L // CHUNK
    P, N, Q = SSM_HEADDIM, SSM_STATE, CHUNK
    R, xs_s, bm_s, cm_s, dt_s, hv_s, y_s, st_s = _ssd_specs(L, d_inner, H, nc, True)

    def body(xs_ref, bm_ref, cm_ref, dt_ref, bias_ref, alog_ref, dsk_ref, st_ref, dy_ref,
             dxs_ref, dbm_ref, dcm_ref, ddt_ref, dbias_ref, dalog_ref, ddsk_ref, dstate):
        c = pl.program_id(0)
        g = pl.program_id(1)

        @pl.when(c == 0)
        def _():
            for r in range(R):
                dstate[g * R + r] = jnp.zeros((P, N), F32)

        @pl.when((c == 0) & (g == 0))
        def _():
            dbias_ref[...] = jnp.zeros_like(dbias_ref)
            dalog_ref[...] = jnp.zeros_like(dalog_ref)
            ddsk_ref[...] = jnp.zeros_like(ddsk_ref)

        @pl.when(g == 0)
        def _():
            ddt_ref[...] = jnp.zeros_like(ddt_ref)

        pre = dt_ref[...] + bias_ref[...]
        dtb = _softplus(pre)
        a_all = -jnp.exp(alog_ref[...])
        lanes_q = lax.broadcasted_iota(jnp.int32, (Q, H), 1)
        lanes_1 = lax.broadcasted_iota(jnp.int32, (1, H), 1)
        bm = bm_ref[...]
        cm = cm_ref[...]
        dbm = jnp.zeros((Q, N), F32)
        dcm = jnp.zeros((Q, N), F32)
        ddt_blk = jnp.zeros((Q, H), F32)
        da_row = jnp.zeros((1, H), F32)
        dd_row = jnp.zeros((1, H), F32)
        for r in range(R):
            h = g * R + r
            args = (
                xs_ref[:, r * P : (r + 1) * P],
                _pick_lane(dtb, h),
                _pick_lane(a_all, h),
                _pick_lane(dsk_ref[...], h),
                bm,
                cm,
                st_ref[r * P : (r + 1) * P, :],
            )
            _, vjp = jax.vjp(_ssd_head, *args)
            dxs, ddt, da, dd, dbm_h, dcm_h, dprev = vjp((dy_ref[:, r * P : (r + 1) * P], dstate[h]))
            dxs_ref[:, r * P : (r + 1) * P] = dxs
            dstate[h] = dprev
            dbm = dbm + dbm_h
            dcm = dcm + dcm_h
            ddt_blk = ddt_blk + jnp.where(lanes_q == h, ddt, 0.0)
            da_row = da_row + jnp.where(lanes_1 == h, da, 0.0)
            dd_row = dd_row + jnp.where(lanes_1 == h, dd, 0.0)
        dbm_ref[...] = dbm
        dcm_ref[...] = dcm
        ddt_pre = ddt_blk * _sigmoid(pre)
        ddt_ref[...] += ddt_pre
        dbias_ref[...] += jnp.sum(ddt_pre, axis=0, keepdims=True)
        dalog_ref[...] += da_row * a_all
        ddsk_ref[...] += dd_row

    ngrp = SSM_GROUPS
    hrow = jax.ShapeDtypeStruct((1, H), F32)
    dxs, dbm, dcm, ddt, dbias, dalog, ddsk = pl.pallas_call(
        body,
        out_shape=(
            jax.ShapeDtypeStruct((L, d_inner), F32),
            jax.ShapeDtypeStruct((L, ngrp * N), F32),
            jax.ShapeDtypeStruct((L, ngrp * N), F32),
            jax.ShapeDtypeStruct((L, H), F32),
            hrow,
            hrow,
            hrow,
        ),
        grid=(nc, ngrp),
        in_specs=[xs_s, bm_s, cm_s, dt_s, hv_s, hv_s, hv_s, st_s, y_s],
        out_specs=(
            y_s,
            pl.BlockSpec((Q, N), lambda c, g: (nc - 1 - c, g)),
            pl.BlockSpec((Q, N), lambda c, g: (nc - 1 - c, g)),
            dt_s,
            hv_s,
            hv_s,
            hv_s,
        ),
        scratch_shapes=[pltpu.VMEM((H, P, N), F32)],
        compiler_params=_params("arbitrary", "arbitrary"),
        name=name,
    )(xbc, xbc, xbc, dt_raw, dt_bias, a_log, d_skip, states, dy)
    return jnp.concatenate([dxs, dbm, dcm], axis=1), ddt, dbias, dalog, ddsk


def gnorm_fwd(y, zx, nw, *, name):
    L, DI = y.shape
    gw = DI // SSM_GROUPS
    tl = _pick(L, 512, 16)

    def body(y_ref, z_ref, nw_ref, o_ref):
        yz = y_ref[...] * _silu(z_ref[...])
        r = lax.rsqrt(jnp.mean(yz * yz, axis=-1, keepdims=True) + EPS)
        o_ref[...] = ((yz * r) * nw_ref[...]).astype(o_ref.dtype)

    tile = pl.BlockSpec((tl, gw), lambda i, g: (i, g))
    return pl.pallas_call(
        body,
        out_shape=jax.ShapeDtypeStruct((L, DI), BF16),
        grid=(L // tl, SSM_GROUPS),
        in_specs=[tile, tile, pl.BlockSpec((1, gw), lambda i, g: (0, g))],
        out_specs=tile,
        compiler_params=_params("parallel", "parallel"),
        name=name,
    )(y, zx, nw)


def gnorm_bwd(y, zx, nw, dout, *, name):
    L, DI = y.shape
    gw = DI // SSM_GROUPS
    tl = _pick(L, 512, 16)

    def body(y_ref, z_ref, nw_ref, do_ref, dy_ref, dz_ref, dnw_ref):
        @pl.when(pl.program_id(1) == 0)
        def _():
            dnw_ref[...] = jnp.zeros_like(dnw_ref)

        yv = y_ref[...]
        zv = z_ref[...]
        sz = _silu(zv)
        yz = yv * sz
        r = lax.rsqrt(jnp.mean(yz * yz, axis=-1, keepdims=True) + EPS)
        n = yz * r
        dov = do_ref[...]
        dnw_ref[...] += jnp.sum(dov * n, axis=0, keepdims=True)
        dn = dov * nw_ref[...]
        dyz = r * (dn - n * jnp.mean(dn * n, axis=-1, keepdims=True))
        dy_ref[...] = dyz * sz
        dz_ref[...] = dyz * yv * _dsilu(zv)

    tile = pl.BlockSpec((tl, gw), lambda g, i: (i, g))
    row = pl.BlockSpec((1, gw), lambda g, i: (0, g))
    return pl.pallas_call(
        body,
        out_shape=(
            jax.ShapeDtypeStruct((L, DI), F32),
            jax.ShapeDtypeStruct((L, DI), F32),
            jax.ShapeDtypeStruct((1, DI), F32),
        ),
        grid=(SSM_GROUPS, L // tl),
        in_specs=[tile, tile, row, tile],
        out_specs=(tile, tile, row),
        compiler_params=_params("parallel", "arbitrary"),
        name=name,
    )(y, zx, nw, dout)


def _attn_head(q, kp, kc, vp, vc, sink, has_prev):
    w = q.shape[0]
    nt = (((1,), (1,)), ((), ()))
    qb = q.astype(BF16)
    sc = lax.dot_general(qb, kc.astype(BF16), nt, preferred_element_type=F32) * HEAD_DIM ** -0.5
    sp = lax.dot_general(qb, kp.astype(BF16), nt, preferred_element_type=F32) * HEAD_DIM ** -0.5
    ii = lax.broadcasted_iota(jnp.int32, (w, w), 0)
    jj = lax.broadcasted_iota(jnp.int32, (w, w), 1)
    lc = jnp.where(jj <= ii, sc, -jnp.inf)
    lp = jnp.where((jj > ii) & has_prev, sp, -jnp.inf)
    m = jnp.maximum(jnp.maximum(jnp.max(lc, axis=1, keepdims=True), jnp.max(lp, axis=1, keepdims=True)), sink)
    m = lax.stop_gradient(m)
    pc = jnp.exp(lc - m)
    pp = jnp.exp(lp - m)
    denom = jnp.sum(pc, axis=1, keepdims=True) + jnp.sum(pp, axis=1, keepdims=True) + jnp.exp(sink - m)
    o = jnp.dot((pc / denom).astype(BF16), vc.astype(BF16), preferred_element_type=F32)
    return o + jnp.dot((pp / denom).astype(BF16), vp.astype(BF16), preferred_element_type=F32)


def attn_fwd(q, kv, sinks, *, name):
    L, DQ = q.shape
    heads = DQ // HEAD_DIM
    rep = heads // KV_HEADS
    nb = L // WINDOW
    kw = KV_HEADS * HEAD_DIM
    W, HD = WINDOW, HEAD_DIM

    def body(q_ref, kp_ref, kc_ref, vp_ref, vc_ref, s_ref, o_ref):
        has_prev = pl.program_id(0) > 0
        for h in range(heads):
            kh = h // rep
            ks = slice(kh * HD, (kh + 1) * HD)
            o = _attn_head(
                q_ref[:, h * HD : (h + 1) * HD], kp_ref[:, ks], kc_ref[:, ks], vp_ref[:, ks], vc_ref[:, ks],
                s_ref[:, h : h + 1], has_prev,
            )
            o_ref[:, h * HD : (h + 1) * HD] = o.astype(o_ref.dtype)

    return pl.pallas_call(
        body,
        out_shape=jax.ShapeDtypeStruct((L, DQ), BF16),
        grid=(nb,),
        in_specs=[
            pl.BlockSpec((W, DQ), lambda n: (n, 0)),
            pl.BlockSpec((W, kw), lambda n: (jnp.maximum(n - 1, 0), 0)),
            pl.BlockSpec((W, kw), lambda n: (n, 0)),
            pl.BlockSpec((W, kw), lambda n: (jnp.maximum(n - 1, 0), 1)),
            pl.BlockSpec((W, kw), lambda n: (n, 1)),
            pl.BlockSpec((1, heads), lambda n: (0, 0)),
        ],
        out_specs=pl.BlockSpec((W, DQ), lambda n: (n, 0)),
        compiler_params=_params("parallel"),
        name=name,
    )(q, kv, kv, kv, kv, sinks)


def attn_bwd(q, kv, sinks, do, dkv_in, *, name):
    L, DQ = q.shape
    heads = DQ // HEAD_DIM
    rep = heads // KV_HEADS
    nb = L // WINDOW
    kw = KV_HEADS * HEAD_DIM
    W, HD = WINDOW, HEAD_DIM

    def blk(n):
        return jnp.minimum(n, nb - 1)

    def prev(n):
        return jnp.maximum(blk(n) - 1, 0)

    def outb(n):
        return jnp.maximum(n - 1, 0)

    def body(q_ref, kp_ref, kc_ref, vp_ref, vc_ref, s_ref, do_ref, dki_ref, dvi_ref,
             dq_ref, dk_ref, dv_ref, ds_ref, dk_cur, dv_cur):
        n = pl.program_id(0)
        has_prev = n > 0

        @pl.when(n == 0)
        def _():
            ds_ref[...] = jnp.zeros_like(ds_ref)
            dk_cur[...] = jnp.zeros_like(dk_cur)
            dv_cur[...] = jnp.zeros_like(dv_cur)

        @pl.when(n == nb)
        def _():
            dk_ref[...] = dki_ref[...] + dk_cur[...]
            dv_ref[...] = dvi_ref[...] + dv_cur[...]

        @pl.when(n < nb)
        def _():
            lanes = lax.broadcasted_iota(jnp.int32, (1, heads), 1)
            ds_row = jnp.zeros((1, heads), F32)
            for kh in range(KV_HEADS):
                ks = slice(kh * HD, (kh + 1) * HD)
                kp, kc, vp, vc = kp_ref[:, ks], kc_ref[:, ks], vp_ref[:, ks], vc_ref[:, ks]
                dkp = jnp.zeros((W, HD), F32)
                dkc = jnp.zeros((W, HD), F32)
                dvp = jnp.zeros((W, HD), F32)
                dvc = jnp.zeros((W, HD), F32)
                for rr in range(rep):
                    h = kh * rep + rr
                    hs = slice(h * HD, (h + 1) * HD)
                    _, vjp = jax.vjp(
                        functools.partial(_attn_head, has_prev=has_prev),
                        q_ref[:, hs], kp, kc, vp, vc, s_ref[:, h : h + 1],
                    )
                    dq, a, b, c, d, dsk = vjp(do_ref[:, hs])
                    dq_ref[:, hs] = dq
                    dkp, dkc, dvp, dvc = dkp + a, dkc + b, dvp + c, dvc + d
                    ds_row = ds_row + jnp.where(lanes == h, dsk, 0.0)
                dk_ref[:, ks] = dki_ref[:, ks] + dk_cur[:, ks] + dkp
                dv_ref[:, ks] = dvi_ref[:, ks] + dv_cur[:, ks] + dvp
                dk_cur[:, ks] = dkc
                dv_cur[:, ks] = dvc
            ds_ref[...] += ds_row

    dq, dk, dv, ds = pl.pallas_call(
        body,
        out_shape=(
            jax.ShapeDtypeStruct((L, DQ), F32),
            jax.ShapeDtypeStruct((L, kw), F32),
            jax.ShapeDtypeStruct((L, kw), F32),
            jax.ShapeDtypeStruct((1, heads), F32),
        ),
        grid=(nb + 1,),
        in_specs=[
            pl.BlockSpec((W, DQ), lambda n: (blk(n), 0)),
            pl.BlockSpec((W, kw), lambda n: (prev(n), 0)),
            pl.BlockSpec((W, kw), lambda n: (blk(n), 0)),
            pl.BlockSpec((W, kw), lambda n: (prev(n), 1)),
            pl.BlockSpec((W, kw), lambda n: (blk(n), 1)),
            pl.BlockSpec((1, heads), lambda n: (0, 0)),
            pl.BlockSpec((W, DQ), lambda n: (blk(n), 0)),
            pl.BlockSpec((W, kw), lambda n: (outb(n), 0)),
            pl.BlockSpec((W, kw), lambda n: (outb(n), 1)),
        ],
        out_specs=(
            pl.BlockSpec((W, DQ), lambda n: (blk(n), 0)),
            pl.BlockSpec((W, kw), lambda n: (outb(n), 0)),
            pl.BlockSpec((W, kw), lambda n: (outb(n), 0)),
            pl.BlockSpec((1, heads), lambda n: (0, 0)),
        ),
        scratch_shapes=[pltpu.VMEM((W, kw), F32), pltpu.VMEM((W, kw), F32)],
        compiler_params=_params("arbitrary"),
        name=name,
    )(q, kv, kv, kv, kv, sinks, do, dkv_in, dkv_in)
    return dq, jnp.concatenate([dk, dv], axis=1), ds


def final_loss(x, fw, target, *, name):
    L, D = x.shape
    tl = _pick(L, 512, 8)

    def body(x_ref, fw_ref, t_ref, loss_ref, dx_ref, dfw_ref):
        @pl.when(pl.program_id(0) == 0)
        def _():
            loss_ref[...] = jnp.zeros_like(loss_ref)
            dfw_ref[...] = jnp.zeros_like(dfw_ref)

        xv = x_ref[...]
        fwv = fw_ref[...]
        r = lax.rsqrt(jnp.mean(xv * xv, axis=-1, keepdims=True) + EPS)
        xhat = xv * r
        err = xhat * fwv - t_ref[...]
        loss_ref[...] += 0.5 * jnp.sum(jnp.mean(err * err, axis=-1, keepdims=True), axis=0, keepdims=True)
        dy = err * (1.0 / D)
        dfw_ref[...] += jnp.sum(dy * xhat, axis=0, keepdims=True)
        dxhat = dy * fwv
        dx_ref[...] = r * (dxhat - xhat * jnp.mean(dxhat * xhat, axis=-1, keepdims=True))

    tile = pl.BlockSpec((tl, D), lambda i: (i, 0))
    row = pl.BlockSpec((1, D), lambda i: (0, 0))
    return pl.pallas_call(
        body,
        out_shape=(
            jax.ShapeDtypeStruct((1, 1), F32),
            jax.ShapeDtypeStruct((L, D), F32),
            jax.ShapeDtypeStruct((1, D), F32),
        ),
        grid=(L // tl,),
        in_specs=[tile, row, tile],
        out_specs=(pl.BlockSpec((1, 1), lambda i: (0, 0)), tile, row),
        compiler_params=_params("arbitrary"),
        name=name,
    )(x, fw, target)


def outer8(ct, d, *, name):
    D, B = ct.shape
    S, _, N = d.shape
    tm = _pick(D, 512, 8)
    tn = _pick(N, 256, LANE)

    def body(c_ref, d_ref, o_ref):
        acc = c_ref[:, 0:1] * d_ref[0:1, :]
        for b in range(1, B):
            acc = acc + c_ref[:, b : b + 1] * d_ref[b : b + 1, :]
        o_ref[...] = acc

    return pl.pallas_call(
        body,
        out_shape=jax.ShapeDtypeStruct((S, D, N), F32),
        grid=(S, D // tm, N // tn),
        in_specs=[
            pl.BlockSpec((tm, B), lambda s, i, j: (i, 0)),
            pl.BlockSpec((None, B, tn), lambda s, i, j: (s, 0, j)),
        ],
        out_specs=pl.BlockSpec((None, tm, tn), lambda s, i, j: (s, i, j)),
        compiler_params=_params("parallel", "parallel", "parallel"),
        name=name,
    )(ct, d)


def reduce8(g, *, name):
    nd, R, N = g.shape

    def body(g_ref, o_ref):
        acc = g_ref[0]
        for b in range(1, nd):
            acc = acc + g_ref[b]
        o_ref[...] = acc

    return pl.pallas_call(
        body,
        out_shape=jax.ShapeDtypeStruct((R, N), F32),
        name=name,
    )(g)


def _as3(a):
    if a.ndim == 1:
        return a.reshape(1, 1, -1)
    if a.ndim == 2:
        return a.reshape((1,) + a.shape)
    return a.reshape((-1,) + a.shape[-2:])


def adamw(w, g, m, v, *, name):
    shape = w.shape
    w3, g3, m3, v3 = _as3(w), _as3(g), _as3(m), _as3(v)
    B, R, C = w3.shape
    tr = _pick(R, max(8, (1 << 19) // max(C, 1) // 8 * 8), 8)

    def body(w_ref, g_ref, m_ref, v_ref, d_ref, nm_ref, nv_ref):
        gv = g_ref[...]
        mn = ADAM_B1 * m_ref[...] + (1.0 - ADAM_B1) * gv
        vn = ADAM_B2 * v_ref[...] + (1.0 - ADAM_B2) * (gv * gv)
        m_hat = mn / (1.0 - ADAM_B1 ** ADAM_STEP)
        v_hat = vn / (1.0 - ADAM_B2 ** ADAM_STEP)
        d_ref[...] = -ADAM_LR * (m_hat / (jnp.sqrt(v_hat) + ADAM_EPS) + ADAM_WD * w_ref[...])
        nm_ref[...] = mn
        nv_ref[...] = vn

    tile = pl.BlockSpec((None, tr, C), lambda b, i: (b, i, 0))
    sds = jax.ShapeDtypeStruct((B, R, C), F32)
    d, nm, nv = pl.pallas_call(
        body,
        out_shape=(sds, sds, sds),
        grid=(B, R // tr),
        in_specs=[tile, tile, tile, tile],
        out_specs=(tile, tile, tile),
        compiler_params=_params("parallel", "parallel"),
        name=name,
    )(w3, g3, m3, v3)
    return d.reshape(shape), nm.reshape(shape), nv.reshape(shape)


def _place():
    return lax.axis_index("x"), lax.axis_index("y"), lax.axis_index("c")


def _flip(v, bit):
    return (1 - v) if bit else v


def ag8(v, *, act=None, name):
    R, N = v.shape

    def body(v_ref, out_ref, stage, send_sems, recv_sems):
        x, y, c = _place()
        me = 4 * x + 2 * y + c
        val = v_ref[...]
        if act is not None:
            val = act(val)
        stage[...] = val
        out_ref[me] = val
        sends = []
        for k in range(1, N_DEV):
            px, py, pc = _flip(x, k & 4), _flip(y, k & 2), _flip(c, k & 1)
            cp = pltpu.make_async_remote_copy(
                src_ref=stage, dst_ref=out_ref.at[me], send_sem=send_sems.at[k - 1], recv_sem=recv_sems.at[k - 1],
                device_id=(px, py, pc), device_id_type=MESH,
            )
            cp.start()
            sends.append(cp)
        for k in range(1, N_DEV):
            px, py, pc = _flip(x, k & 4), _flip(y, k & 2), _flip(c, k & 1)
            pltpu.make_async_remote_copy(
                src_ref=stage, dst_ref=out_ref.at[4 * px + 2 * py + pc], send_sem=send_sems.at[k - 1],
                recv_sem=recv_sems.at[k - 1], device_id=(px, py, pc), device_id_type=MESH,
            ).wait_recv()
        for cp in sends:
            cp.wait_send()

    return pl.pallas_call(
        body,
        out_shape=jax.ShapeDtypeStruct((N_DEV, R, N), F32),
        in_specs=[pl.BlockSpec(memory_space=pltpu.VMEM)],
        out_specs=pl.BlockSpec(memory_space=pltpu.VMEM),
        scratch_shapes=[
            pltpu.VMEM((R, N), F32),
            pltpu.SemaphoreType.DMA((N_DEV - 1,)),
            pltpu.SemaphoreType.DMA((N_DEV - 1,)),
        ],
        name=name,
    )(v)


def _other_chips(x, y):
    chips = [(1 - x, y), (x, 1 - y), (1 - x, 1 - y)]
    return chips, [2 * px + py for px, py in chips]


_HBM = pl.BlockSpec(memory_space=pltpu.HBM)


def ag_weights(shards, *, name):
    n = len(shards)

    def body(*refs):
        ins, outs = refs[:n], refs[n : 2 * n]
        send_sems, recv_sems = refs[2 * n :]
        x, y, c = _place()
        k_me = 2 * x + y
        chips, kidx = _other_chips(x, y)
        sends = []
        for w in range(n):
            own = pltpu.make_async_remote_copy(
                src_ref=ins[w], dst_ref=outs[w].at[:, :, k_me], send_sem=send_sems.at[w, 6], recv_sem=recv_sems.at[w, 6],
                device_id=(x, y, 1 - c), device_id_type=MESH,
            )
            own.start()
            sends.append(own)
            for j, (px, py) in enumerate(chips):
                cp = pltpu.make_async_remote_copy(
                    src_ref=ins[w].at[c], dst_ref=outs[w].at[c, :, k_me], send_sem=send_sems.at[w, j],
                    recv_sem=recv_sems.at[w, j], device_id=(px, py, c), device_id_type=MESH,
                )
                cp.start()
                sends.append(cp)
        for w in range(n):
            for j, (px, py) in enumerate(chips):
                landed = outs[w].at[c, :, kidx[j]]
                pltpu.make_async_remote_copy(
                    src_ref=ins[w].at[c], dst_ref=landed, send_sem=send_sems.at[w, j], recv_sem=recv_sems.at[w, j],
                    device_id=(px, py, c), device_id_type=MESH,
                ).wait_recv()
                fw = pltpu.make_async_remote_copy(
                    src_ref=landed, dst_ref=landed, send_sem=send_sems.at[w, 3 + j], recv_sem=recv_sems.at[w, 3 + j],
                    device_id=(x, y, 1 - c), device_id_type=MESH,
                )
                fw.start()
                sends.append(fw)
        for w in range(n):
            for j in range(3):
                got = outs[w].at[1 - c, :, kidx[j]]
                pltpu.make_async_remote_copy(
                    src_ref=got, dst_ref=got, send_sem=send_sems.at[w, 3 + j], recv_sem=recv_sems.at[w, 3 + j],
                    device_id=(x, y, 1 - c), device_id_type=MESH,
                ).wait_recv()
            pltpu.make_async_remote_copy(
                src_ref=ins[w], dst_ref=outs[w].at[:, :, k_me], send_sem=send_sems.at[w, 6], recv_sem=recv_sems.at[w, 6],
                device_id=(x, y, 1 - c), device_id_type=MESH,
            ).wait_recv()
        for cp in sends:
            cp.wait_send()

    return pl.pallas_call(
        body,
        out_shape=tuple(jax.ShapeDtypeStruct(s.shape[:2] + (N_CHIPS,) + s.shape[2:], s.dtype) for s in shards),
        in_specs=[_HBM] * n,
        out_specs=tuple([_HBM] * n),
        scratch_shapes=[pltpu.SemaphoreType.DMA((n, 7)), pltpu.SemaphoreType.DMA((n, 7))],
        name=name,
    )(*shards)


def rs_sibling(grads, *, name):
    n = len(grads)

    def body(*refs):
        ins, outs = refs[:n], refs[n : 2 * n]
        send_sems, recv_sems = refs[2 * n :]
        x, y, c = _place()
        cps = []
        for w in range(n):
            cp = pltpu.make_async_remote_copy(
                src_ref=ins[w].at[1 - c], dst_ref=outs[w], send_sem=send_sems.at[w], recv_sem=recv_sems.at[w],
                device_id=(x, y, 1 - c), device_id_type=MESH,
            )
            cp.start()
            cps.append(cp)
        for cp in cps:
            cp.wait()

    return pl.pallas_call(
        body,
        out_shape=tuple(jax.ShapeDtypeStruct(g.shape[1:], g.dtype) for g in grads),
        in_specs=[_HBM] * n,
        out_specs=tuple([_HBM] * n),
        scratch_shapes=[pltpu.SemaphoreType.DMA((n,)), pltpu.SemaphoreType.DMA((n,))],
        name=name,
    )(*grads)


def rs_chips(parts, *, name):
    n = len(parts)

    def body(*refs):
        ins, outs = refs[:n], refs[n : 2 * n]
        send_sems, recv_sems = refs[2 * n :]
        x, y, c = _place()
        chips, kidx = _other_chips(x, y)
        cps = []
        for w in range(n):
            for j, (px, py) in enumerate(chips):
                cp = pltpu.make_async_remote_copy(
                    src_ref=ins[w].at[:, kidx[j]], dst_ref=outs[w].at[j], send_sem=send_sems.at[w, j],
                    recv_sem=recv_sems.at[w, j], device_id=(px, py, c), device_id_type=MESH,
                )
                cp.start()
                cps.append(cp)
        for cp in cps:
            cp.wait()

    return pl.pallas_call(
        body,
        out_shape=tuple(jax.ShapeDtypeStruct((3, s.shape[0]) + s.shape[2:], s.dtype) for s in parts),
        in_specs=[_HBM] * n,
        out_specs=tuple([_HBM] * n),
        scratch_shapes=[pltpu.SemaphoreType.DMA((n, 3)), pltpu.SemaphoreType.DMA((n, 3))],
        name=name,
    )(*parts)


def rs_share(halves, *, name):
    n = len(halves)

    def body(*refs):
        outs = refs[n : 2 * n]
        send_sems, recv_sems = refs[2 * n :]
        x, y, c = _place()
        cps = []
        for w in range(n):
            cp = pltpu.make_async_remote_copy(
                src_ref=outs[w].at[c], dst_ref=outs[w].at[c], send_sem=send_sems.at[w], recv_sem=recv_sems.at[w],
                device_id=(x, y, 1 - c), device_id_type=MESH,
            )
            cp.start()
            cps.append(cp)
        for w, cp in enumerate(cps):
            cp.wait_send()
            pltpu.make_async_remote_copy(
                src_ref=outs[w].at[c], dst_ref=outs[w].at[1 - c], send_sem=send_sems.at[w], recv_sem=recv_sems.at[w],
                device_id=(x, y, 1 - c), device_id_type=MESH,
            ).wait_recv()

    return pl.pallas_call(
        body,
        out_shape=tuple(jax.ShapeDtypeStruct(h.shape, h.dtype) for h in halves),
        in_specs=[_HBM] * n,
        out_specs=tuple([_HBM] * n),
        scratch_shapes=[pltpu.SemaphoreType.DMA((n,)), pltpu.SemaphoreType.DMA((n,))],
        input_output_aliases={w: w for w in range(n)},
        name=name,
    )(*halves)


def _row_tile(R, C):
    return _pick(R, max(16, (1 << 19) // C // 16 * 16), 16)


def _my_core():
    return lax.axis_index("c")


def _my_chip():
    return 2 * lax.axis_index("x") + lax.axis_index("y")


def rs_add_pair(g, r, *, name):
    _, M, K, R, C = g.shape
    tr = _row_tile(R, C)

    def body(g_ref, r_ref, o_ref):
        o_ref[...] = (g_ref[...].astype(F32) + r_ref[...].astype(F32)).astype(o_ref.dtype)

    blk = pl.BlockSpec((None, None, tr, C), lambda m, k, i: (m, k, i, 0))
    return pl.pallas_call(
        body,
        out_shape=jax.ShapeDtypeStruct((M, K, R, C), BF16),
        grid=(M, K, R // tr),
        in_specs=[pl.BlockSpec((None, None, None, tr, C), lambda m, k, i: (_my_core(), m, k, i, 0)), blk],
        out_specs=blk,
        compiler_params=_params("parallel", "parallel", "parallel"),
        name=name,
    )(g, r)


def rs_add_final(g, r, t, *, name):
    _, M, K, R, C = g.shape
    tr = _row_tile(R, C)

    def body(g_ref, r_ref, t_ref, o_ref):
        acc = g_ref[...].astype(F32) + r_ref[...].astype(F32)
        for j in range(3):
            acc = acc + t_ref[j].astype(F32)
        o_ref[...] = acc

    return pl.pallas_call(
        body,
        out_shape=jax.ShapeDtypeStruct((2, M, R, C), F32),
        grid=(M, R // tr),
        in_specs=[
            pl.BlockSpec((None, None, None, tr, C), lambda m, i: (_my_core(), m, _my_chip(), i, 0)),
            pl.BlockSpec((None, None, tr, C), lambda m, i: (m, _my_chip(), i, 0)),
            pl.BlockSpec((3, None, tr, C), lambda m, i: (0, m, i, 0)),
        ],
        out_specs=pl.BlockSpec((None, None, tr, C), lambda m, i: (_my_core(), m, i, 0)),
        compiler_params=_params("parallel", "parallel"),
        name=name,
    )(g, r, t)


WEIGHTS = ["ffn_norm_w", "ffn_w_gu", "ffn_w_down", "mod_w", "mod_b", "mix_norm_w", "ssm_w_in", "ssm_conv_w", "ssm_conv_b",
           "ssm_dt_bias", "ssm_a_log", "ssm_d", "ssm_norm_w", "ssm_w_out", "kv_norm_w", "kv_mod_w", "kv_mod_b", "w_kv", "b_kv",
           "attn_w_q", "attn_b_q", "attn_sinks", "attn_w_o", "attn_b_o", "final_norm_w"]
GATHERED = ["ffn_w_gu", "ffn_w_down", "ssm_w_in", "ssm_w_out", "w_kv", "attn_w_q", "attn_w_o"]
COLUMN_PARALLEL = ["mod_w", "kv_mod_w"]
SMALL_SHARDED = ["ffn_norm_w", "ssm_conv_w", "ssm_conv_b", "ssm_norm_w"]
SMALL = [n for n in WEIGHTS if n not in GATHERED and n not in COLUMN_PARALLEL]


def _halved(a):
    if a.ndim == 2:
        return a.reshape(2, 1, a.shape[0] // 2, a.shape[1])
    return a.reshape((2, -1) + a.shape[-2:])


def _pack(arrs, rows=8):
    flat = jnp.concatenate([a.reshape(-1) for a in arrs])
    n = flat.shape[0]
    pad = (-n) % (rows * LANE)
    return jnp.pad(flat, (0, pad)).reshape(rows, -1), n


def _unpack(flat, like):
    out, o = [], 0
    for s in like:
        k = 1
        for d in s:
            k *= d
        out.append(flat[o : o + k].reshape(s))
        o += k
    return out


def kernel(x, c, ffn_norm_w, ffn_w_gu, ffn_w_down, mod_w, mod_b, mix_norm_w, ssm_w_in, ssm_conv_w, ssm_conv_b, ssm_dt_bias, ssm_a_log, ssm_d, ssm_norm_w, ssm_w_out, kv_norm_w, kv_mod_w, kv_mod_b, w_kv, b_kv, attn_w_q, attn_b_q, attn_sinks, attn_w_o, attn_b_o, final_norm_w, loss_target, m_ffn_norm_w, m_ffn_w_gu, m_ffn_w_down, m_mod_w, m_mod_b, m_mix_norm_w, m_ssm_w_in, m_ssm_conv_w, m_ssm_conv_b, m_ssm_dt_bias, m_ssm_a_log, m_ssm_d, m_ssm_norm_w, m_ssm_w_out, m_kv_norm_w, m_kv_mod_w, m_kv_mod_b, m_w_kv, m_b_kv, m_attn_w_q, m_attn_b_q, m_attn_sinks, m_attn_w_o, m_attn_b_o, m_final_norm_w, v_ffn_norm_w, v_ffn_w_gu, v_ffn_w_down, v_mod_w, v_mod_b, v_mix_norm_w, v_ssm_w_in, v_ssm_conv_w, v_ssm_conv_b, v_ssm_dt_bias, v_ssm_a_log, v_ssm_d, v_ssm_norm_w, v_ssm_w_out, v_kv_norm_w, v_kv_mod_w, v_kv_mod_b, v_w_kv, v_b_kv, v_attn_w_q, v_attn_b_q, v_attn_sinks, v_attn_w_o, v_attn_b_o, v_final_norm_w):
    env = dict(locals())
    W = {n: env[n] for n in WEIGHTS}
    MOM = {n: env["m_" + n] for n in WEIGHTS}
    VAR = {n: env["v_" + n] for n in WEIGHTS}

    ax, ay, ac = _place()
    kme = 2 * ax + ay
    me = 4 * ax + 2 * ay + ac

    xs = x[0]
    target = loss_target[0]
    L, D = xs.shape
    depth, n_a = ffn_w_gu.shape[0], ssm_w_in.shape[0]
    n_b = depth - n_a
    T = ffn_w_gu.shape[-1]
    DI = ssm_w_out.shape[1] * N_CHIPS
    CI = ssm_w_in.shape[2]
    CC = ssm_conv_w.shape[2] * N_CHIPS
    MW = mod_w.shape[2]
    KW = kv_mod_w.shape[1]
    KVD = w_kv.shape[1]

    def chip_cols(a, width):
        return lax.dynamic_slice_in_dim(a, kme * width, width, axis=a.ndim - 1)

    g_gu, g_dn, g_in, g_out, g_kv, g_q, g_o = ag_weights([_halved(W[n].astype(BF16)) for n in GATHERED], name="ag_weights")
    g_gu = g_gu.reshape(depth, 2, N_CHIPS, D, T)
    g_dn = g_dn.reshape(depth, 2, 2, T, D)
    g_in = g_in.reshape(n_a, N_CHIPS, D, CI)
    g_out = g_out.reshape(n_a, DI, D)
    g_kv = jnp.transpose(g_kv.reshape(2, N_CHIPS, D // 8, KVD), (1, 0, 2, 3)).reshape(D, KVD)
    g_q = g_q.reshape(n_b, D, D)
    g_o = g_o.reshape(n_b, D, D)

    sm_like = [W[n].shape for n in SMALL_SHARDED]
    sm_pack, sm_n = _pack([W[n] for n in SMALL_SHARDED])
    sm_all = ag8(sm_pack, name="ag_small_w")[0::2].reshape(N_CHIPS, -1)[:, :sm_n]
    full = {}
    for n, part in zip(SMALL_SHARDED, zip(*[_unpack(sm_all[k], sm_like) for k in range(N_CHIPS)])):
        full[n] = jnp.concatenate(part, axis=-1)

    c_all = ag8(c, act=_silu, name="ag_c").reshape(N_DEV, D)
    p_mod = mm(c_all, mod_w, bias=chip_cols(mod_b, MW)[:, None, :], name="mod_mm")
    p_kv = mm(c_all, kv_mod_w, bias=chip_cols(kv_mod_b, KW)[None, :], name="kvmod_mm")
    p_all = jnp.concatenate([jnp.transpose(p_mod, (1, 0, 2)).reshape(N_DEV, depth * MW), p_kv], axis=1)
    p_mine = lax.dynamic_index_in_dim(ag8(p_all, name="ag_mod")[0::2], me, axis=1, keepdims=False)
    mod = jnp.transpose(p_mine[:, : depth * MW].reshape(N_CHIPS, depth, MW), (1, 0, 2)).reshape(depth, N_MOD * D)
    kvmod = p_mine[:, depth * MW :].reshape(1, 2 * D)
    mods = [[mod[i : i + 1, j * D : (j + 1) * D] for j in range(N_MOD)] for i in range(depth)]
    kv_shift, kv_scale = kvmod[:, :D], kvmod[:, D:]

    def ffn_fwd(xin, i, j, sh, sc, gt):
        h = norm_mod_fwd(xin, full["ffn_norm_w"][i, j][None], sc, sh, name=f"ffn_norm_{i}_{j}")
        gu = mm(h, g_gu, b_lead=(i, j), name=f"ffn_gu_{i}_{j}")
        a = swiglu_fwd(gu, name=f"ffn_act_{i}_{j}")
        f = mm(a, g_dn, reduce_s=True, b_lead=(i, j), name=f"ffn_down_{i}_{j}")
        return gate_fwd(xin, f, gt, FFN_HALF, name=f"ffn_res_{i}_{j}"), (xin, gu, a, f)

    def ssm_fwd(xin, i, sh, sc, gt):
        h = norm_mod_fwd(xin, mix_norm_w[i][None], sc, sh, name=f"mix_norm_{i}")
        zx4 = mm(h, g_in, b_lead=(i,), name=f"ssm_in_{i}")
        zx = jnp.transpose(zx4, (1, 0, 2)).reshape(L, N_CHIPS * CI)
        xbc = conv_fwd(zx, full["ssm_conv_w"][i], full["ssm_conv_b"][i][None], DI, name=f"ssm_conv_{i}")
        dt_raw = zx[:, DI + CC :]
        y, states = ssd_fwd(xbc, dt_raw, ssm_dt_bias[i][None], ssm_a_log[i][None], ssm_d[i][None], DI, name=f"ssd_{i}")
        yn = gnorm_fwd(y, zx, full["ssm_norm_w"][i][None], name=f"ssm_gnorm_{i}")
        f = mm(yn, g_out, b_lead=(i,), name=f"ssm_out_{i}")
        return gate_fwd(xin, f, gt, 1.0, name=f"mix_res_{i}"), (xin, zx, xbc, dt_raw, y, states, yn, f)

    def att_fwd(xin, i, kv, sh, sc, gt):
        l = i - n_a
        h = norm_mod_fwd(xin, mix_norm_w[i][None], sc, sh, name=f"mix_norm_{i}")
        q = mm(h, g_q, b_lead=(l,), bias=attn_b_q[l][None], name=f"att_q_{i}")
        o = attn_fwd(q, kv, attn_sinks[l][None], name=f"att_{i}")
        f = mm(o, g_o, b_lead=(l,), bias=attn_b_o[l][None], name=f"att_o_{i}")
        return gate_fwd(xin, f, gt, 1.0, name=f"mix_res_{i}"), (xin, q, o, f)

    saved = []
    xc = xs
    kv = x_kv = None
    for i in range(depth):
        sh1, sc1, g1, shm, scm, gm, sh2, sc2, g2 = mods[i]
        if i == n_a:
            x_kv = xc
            hkv = norm_mod_fwd(xc, kv_norm_w[None], kv_scale, kv_shift, name="kv_norm")
            kv = mm(hkv, g_kv, bias=b_kv[None], name="kv_proj")
        xc, s1 = ffn_fwd(xc, i, 0, sh1, sc1, g1)
        xc, sm = ssm_fwd(xc, i, shm, scm, gm) if i < n_a else att_fwd(xc, i, kv, shm, scm, gm)
        xc, s2 = ffn_fwd(xc, i, 1, sh2, sc2, g2)
        saved.append((s1, sm, s2))

    loss_part, dx, d_final = final_loss(xc, final_norm_w[None], target, name="loss_head")
    loss = lax.psum(loss_part[0, 0], ("x", "y", "c"))

    b_gu = jnp.zeros((depth, 2, N_CHIPS, D, T), BF16)
    b_dn = jnp.zeros((depth, 2, 2, T, D), BF16)
    b_in = jnp.zeros((n_a, N_CHIPS, D, CI), BF16)
    b_out = jnp.zeros((n_a, DI, D), BF16)
    b_q = jnp.zeros((n_b, D, D), BF16)
    b_o = jnp.zeros((n_b, D, D), BF16)
    sg = {
        "ffn_norm_w": [[None, None] for _ in range(depth)], "mix_norm_w": [None] * depth, "mod": [None] * depth,
        "ssm_conv_w": [None] * n_a, "ssm_conv_b": [None] * n_a, "ssm_dt_bias": [None] * n_a, "ssm_a_log": [None] * n_a,
        "ssm_d": [None] * n_a, "ssm_norm_w": [None] * n_a, "attn_b_q": [None] * n_b, "attn_sinks": [None] * n_b,
        "attn_b_o": [None] * n_b,
    }

    def ffn_bwd(dxo, i, j, sv, sh, sc, gt):
        nonlocal b_gu, b_dn
        xin, gu, a, f = sv
        df, dgt, _ = gate_bwd(f, dxo, gt, FFN_HALF, name=f"ffn_res_bwd_{i}_{j}")
        da = mm(df, g_dn, mode="nt", b_lead=(i, j), name=f"ffn_down_dx_{i}_{j}")
        b_dn = mm(a, df, mode="tn", out_buf=b_dn, out_lead=(i, j), name=f"ffn_down_dw_{i}_{j}")
        dgu = swiglu_bwd(gu, da, name=f"ffn_act_bwd_{i}_{j}")
        nw = full["ffn_norm_w"][i, j][None]
        h = norm_mod_fwd(xin, nw, sc, sh, name=f"ffn_norm_re_{i}_{j}")
        dh = mm(dgu, g_gu, mode="nt", reduce_s=True, b_lead=(i, j), name=f"ffn_gu_dx_{i}_{j}")
        b_gu = mm(h, dgu, mode="tn", out_buf=b_gu, out_lead=(i, j), name=f"ffn_gu_dw_{i}_{j}")
        dxi, dnw, dsc, dsh = norm_mod_bwd(xin, nw, sc, dh, dxo, name=f"ffn_norm_bwd_{i}_{j}")
        sg["ffn_norm_w"][i][j] = dnw
        return dxi, (dsh, dsc, dgt)

    def ssm_bwd(dxo, i, sv, sh, sc, gt):
        nonlocal b_in, b_out
        xin, zx, xbc, dt_raw, y, states, yn, f = sv
        df, dgt, _ = gate_bwd(f, dxo, gt, 1.0, name=f"mix_res_bwd_{i}")
        dyn = mm(df, g_out, mode="nt", b_lead=(i,), name=f"ssm_out_dx_{i}")
        b_out = mm(yn, df, mode="tn", out_buf=b_out, out_lead=(i,), name=f"ssm_out_dw_{i}")
        dy, dz, dnorm = gnorm_bwd(y, zx, full["ssm_norm_w"][i][None], dyn, name=f"ssm_gnorm_bwd_{i}")
        dxbc, ddt, dbias, dalog, ddsk = ssd_bwd(
            xbc, dt_raw, ssm_dt_bias[i][None], ssm_a_log[i][None], ssm_d[i][None], states, dy, DI, name=f"ssd_bwd_{i}"
        )
        du, dcw, dcb = conv_bwd(zx, full["ssm_conv_w"][i], full["ssm_conv_b"][i][None], dxbc, DI, name=f"ssm_conv_bwd_{i}")
        dzx = jnp.concatenate([dz, du, ddt], axis=1).astype(BF16)
        dzx4 = jnp.transpose(dzx.reshape(L, N_CHIPS, CI), (1, 0, 2))
        nw = mix_norm_w[i][None]
        h = norm_mod_fwd(xin, nw, sc, sh, name=f"mix_norm_re_{i}")
        dh = mm(dzx4, g_in, mode="nt", reduce_s=True, b_lead=(i,), name=f"ssm_in_dx_{i}")
        b_in = mm(h, dzx4, mode="tn", out_buf=b_in, out_lead=(i,), name=f"ssm_in_dw_{i}")
        dxi, dnw, dsc, dsh = norm_mod_bwd(xin, nw, sc, dh, dxo, name=f"mix_norm_bwd_{i}")
        sg["mix_norm_w"][i] = dnw
        sg["ssm_conv_w"][i], sg["ssm_conv_b"][i], sg["ssm_norm_w"][i] = dcw, dcb, dnorm
        sg["ssm_dt_bias"][i], sg["ssm_a_log"][i], sg["ssm_d"][i] = dbias, dalog, ddsk
        return dxi, (dsh, dsc, dgt)

    def att_bwd(dxo, i, sv, dkv, sh, sc, gt):
        nonlocal b_q, b_o
        l = i - n_a
        xin, q, o, f = sv
        df, dgt, dfsum = gate_bwd(f, dxo, gt, 1.0, name=f"mix_res_bwd_{i}")
        do = mm(df, g_o, mode="nt", b_lead=(l,), name=f"att_o_dx_{i}")
        b_o = mm(o, df, mode="tn", out_buf=b_o, out_lead=(l,), name=f"att_o_dw_{i}")
        dq, dkv, dsink = attn_bwd(q, kv, attn_sinks[l][None], do, dkv, name=f"att_bwd_{i}")
        nw = mix_norm_w[i][None]
        h = norm_mod_fwd(xin, nw, sc, sh, name=f"mix_norm_re_{i}")
        dh = mm(dq, g_q, mode="nt", b_lead=(l,), name=f"att_q_dx_{i}")
        b_q = mm(h, dq, mode="tn", out_buf=b_q, out_lead=(l,), name=f"att_q_dw_{i}")
        dxi, dnw, dsc, dsh = norm_mod_bwd(xin, nw, sc, dh, dxo, name=f"mix_norm_bwd_{i}")
        sg["mix_norm_w"][i] = dnw
        sg["attn_b_q"][l], sg["attn_sinks"][l], sg["attn_b_o"][l] = colsum(dq, name=f"att_bq_{i}"), dsink, dfsum
        return dxi, dkv, (dsh, dsc, dgt)

    dkv = jnp.zeros((L, KVD), F32)
    d_kvnorm = d_kvmod = d_bkv = b_kv_w = None
    for i in reversed(range(depth)):
        sh1, sc1, g1, shm, scm, gm, sh2, sc2, g2 = mods[i]
        s1, sm, s2 = saved[i]
        dx, dm2 = ffn_bwd(dx, i, 1, s2, sh2, sc2, g2)
        if i < n_a:
            dx, dmm = ssm_bwd(dx, i, sm, shm, scm, gm)
        else:
            dx, dkv, dmm = att_bwd(dx, i, sm, dkv, shm, scm, gm)
        dx, dm1 = ffn_bwd(dx, i, 0, s1, sh1, sc1, g1)
        sg["mod"][i] = jnp.concatenate(list(dm1) + list(dmm) + list(dm2), axis=1)
        if i == n_a:
            d_bkv = colsum(dkv, name="kv_bias_bwd")
            hkv = norm_mod_fwd(x_kv, kv_norm_w[None], kv_scale, kv_shift, name="kv_norm_re")
            dh = mm(dkv, g_kv, mode="nt", name="kv_proj_dx")
            b_kv_w = mm(hkv, dkv, mode="tn", out_dtype=BF16, name="kv_proj_dw")
            dx, d_kvnorm, dsc, dsh = norm_mod_bwd(x_kv, kv_norm_w[None], kv_scale, dh, dx, name="kv_norm_bwd")
            d_kvmod = jnp.concatenate([dsh, dsc], axis=1)
    grad_x = dx[None]

    b_kv_w = jnp.transpose(b_kv_w.reshape(N_CHIPS, 2, D // 8, KVD), (1, 0, 2, 3))
    parts = [
        b_gu.reshape(2, -1, N_CHIPS, D, T), b_dn.reshape(2, -1, N_CHIPS, T // 2, D), b_in.reshape(2, -1, N_CHIPS, D, CI),
        b_out.reshape(2, -1, N_CHIPS, DI // N_CHIPS, D), b_kv_w.reshape(2, 1, N_CHIPS, D // 8, KVD),
        b_q.reshape(2, -1, N_CHIPS, D // N_CHIPS, D), b_o.reshape(2, -1, N_CHIPS, D // N_CHIPS, D),
    ]
    from_sib = rs_sibling(parts, name="rs_sibling")
    pair = [rs_add_pair(g, r, name=f"rs_pair_{n}") for n, g, r in zip(GATHERED, parts, from_sib)]
    from_chips = rs_chips(pair, name="rs_chips")
    halves = [rs_add_final(g, r, t, name=f"rs_final_{n}") for n, g, r, t in zip(GATHERED, parts, from_sib, from_chips)]
    shared = rs_share(halves, name="rs_share")
    grads = {n: s.reshape(W[n].shape) for n, s in zip(GATHERED, shared)}

    small = {
        "ffn_norm_w": jnp.stack([jnp.stack([r[0] for r in row]) for row in sg["ffn_norm_w"]]),
        "mod_b": jnp.stack([r[0] for r in sg["mod"]]),
        "mix_norm_w": jnp.stack([r[0] for r in sg["mix_norm_w"]]),
        "ssm_conv_w": jnp.stack(sg["ssm_conv_w"]),
        "ssm_conv_b": jnp.stack([r[0] for r in sg["ssm_conv_b"]]),
        "ssm_dt_bias": jnp.stack([r[0] for r in sg["ssm_dt_bias"]]),
        "ssm_a_log": jnp.stack([r[0] for r in sg["ssm_a_log"]]),
        "ssm_d": jnp.stack([r[0] for r in sg["ssm_d"]]),
        "ssm_norm_w": jnp.stack([r[0] for r in sg["ssm_norm_w"]]),
        "kv_norm_w": d_kvnorm[0],
        "kv_mod_b": d_kvmod[0],
        "b_kv": d_bkv[0],
        "attn_b_q": jnp.stack([r[0] for r in sg["attn_b_q"]]),
        "attn_sinks": jnp.stack([r[0] for r in sg["attn_sinks"]]),
        "attn_b_o": jnp.stack([r[0] for r in sg["attn_b_o"]]),
        "final_norm_w": d_final[0],
    }
    small_like = [small[n].shape for n in SMALL]
    sv_pack, sv_n = _pack([small[n] for n in SMALL])
    sv_all = ag8(sv_pack, name="ag_small_g")
    sv_sum = reduce8(sv_all, name="small_g_sum").reshape(-1)[:sv_n]
    for n, gfull in zip(SMALL, _unpack(sv_sum, small_like)):
        grads[n] = chip_cols(gfull, W[n].shape[-1]) if n in SMALL_SHARDED else gfull

    per_dev = [_unpack(sv_all[b].reshape(-1)[:sv_n], small_like) for b in range(N_DEV)]
    i_modb, i_kvb = SMALL.index("mod_b"), SMALL.index("kv_mod_b")
    dmod_all = jnp.stack([chip_cols(p[i_modb], MW) for p in per_dev], axis=1)
    dkv_all = jnp.stack([chip_cols(p[i_kvb], KW) for p in per_dev], axis=0)[None]
    c_t = jnp.transpose(c_all)
    grads["mod_w"] = outer8(c_t, dmod_all, name="mod_w_grad")
    grads["kv_mod_w"] = outer8(c_t, dkv_all, name="kv_mod_w_grad")[0]

    delta, new_m, new_v = {}, {}, {}
    for n in GATHERED + COLUMN_PARALLEL:
        delta[n], new_m[n], new_v[n] = adamw(W[n], grads[n], MOM[n], VAR[n], name=f"adamw_{n}")
    like = [W[n].shape for n in SMALL]
    packs = [_pack([d[n] for n in SMALL])[0] for d in (W, grads, MOM, VAR)]
    n_small = sum(int(W[n].size) for n in SMALL)
    for dst, res in zip((delta, new_m, new_v), adamw(*packs, name="adamw_small")):
        for n, a in zip(SMALL, _unpack(res.reshape(-1)[:n_small], like)):
            dst[n] = a

    return (loss, grad_x, *[grads[n] for n in WEIGHTS], *[delta[n] for n in WEIGHTS], *[new_m[n] for n in WEIGHTS],
            *[new_v[n] for n in WEIGHTS])
```

```python
import functools

import jax
import jax.numpy as jnp
from jax import lax
from jax.experimental import pallas as pl
from jax.experimental.pallas import tpu as pltpu

F32 = jnp.float32
BF16 = jnp.bfloat16
HIGHEST = lax.Precision.HIGHEST
MESH = pl.DeviceIdType.MESH

EPS = 1e-5
N_MOD = 9
FFN_HALF = 0.5
SSM_HEADDIM = 64
SSM_GROUPS = 8
SSM_STATE = 128
CONV_WIDTH = 4
CHUNK = 128
KV_HEADS = 4
HEAD_DIM = 64
WINDOW = 128
N_CHIPS = 4
N_DEV = 8

ADAM_LR = 0.001
ADAM_B1 = 0.9
ADAM_B2 = 0.999
ADAM_EPS = 1e-08
ADAM_WD = 0.01
ADAM_STEP = 10

LANE = 128
MM_TILE = 1024


def _pick(n, pref, align, whole_if_small=False):
    best = 0
    t = align
    while t <= min(n, pref):
        if n % t == 0:
            best = t
        t += align
    if best == 0 or (whole_if_small and best < 256 and n <= 2048):
        return n
    return best


def _sigmoid(x):
    return 1.0 / (1.0 + jnp.exp(-x))


def _silu(x):
    return x * _sigmoid(x)


def _dsilu(x):
    s = _sigmoid(x)
    return s * (1.0 + x * (1.0 - s))


def _params(*sem):
    return pltpu.CompilerParams(dimension_semantics=sem)


def mm(a, b, *, mode="nn", reduce_s=False, out_dtype=F32, bias=None, a_lead=(), b_lead=(), out_buf=None,
       out_lead=(), name):
    a_s = a.ndim - len(a_lead) == 3
    b_s = b.ndim - len(b_lead) == 3
    S = a.shape[len(a_lead)] if a_s else (b.shape[len(b_lead)] if b_s else 1)
    a2 = a.shape[-2:]
    b2 = b.shape[-2:]
    if mode == "nn":
        (M, K), (K2, N) = a2, b2
    elif mode == "nt":
        (M, K), (N, K2) = a2, b2
    else:
        (K, M), (K2, N) = a2, b2
    assert K == K2, (a.shape, b.shape, mode)
    batch = (a_s or b_s) and not reduce_s
    sb = S if batch else 1
    sr = S if ((a_s or b_s) and reduce_s) else 1
    tm = _pick(M, MM_TILE, LANE if mode == "tn" else 16, True)
    tn = _pick(N, MM_TILE, LANE, True)
    tk = _pick(K, MM_TILE, LANE if mode != "tn" else 16, True)
    nk = K // tk
    grid = (sb, M // tm, N // tn, sr, nk)

    def s_of(isb, isr):
        return isb if batch else isr

    def a_map(isb, i, j, isr, k):
        idx = (k, i) if mode == "tn" else (i, k)
        return tuple(a_lead) + (((s_of(isb, isr),) + idx) if a_s else idx)

    def b_map(isb, i, j, isr, k):
        idx = (j, k) if mode == "nt" else (k, j)
        return tuple(b_lead) + (((s_of(isb, isr),) + idx) if b_s else idx)

    def o_map(isb, i, j, isr, k):
        return tuple(out_lead) + ((isb, i, j) if batch else (i, j))

    def lead_blk(lead, has_s, blk):
        return (None,) * (len(lead) + (1 if has_s else 0)) + blk

    a_blk = (tk, tm) if mode == "tn" else (tm, tk)
    b_blk = (tn, tk) if mode == "nt" else (tk, tn)
    in_specs = [pl.BlockSpec(lead_blk(a_lead, a_s, a_blk), a_map), pl.BlockSpec(lead_blk(b_lead, b_s, b_blk), b_map)]
    args = [a, b]
    if bias is not None:
        bias_s = bias.ndim == 3
        in_specs.append(
            pl.BlockSpec(
                ((None, 1, tn) if bias_s else (1, tn)),
                (lambda isb, i, j, isr, k: (isb, 0, j)) if bias_s else (lambda isb, i, j, isr, k: (0, j)),
            )
        )
        args.append(bias)
    aliases = {}
    if out_buf is not None:
        in_specs.append(pl.BlockSpec(memory_space=pl.ANY))
        aliases = {len(args): 0}
        args.append(out_buf)
        out_shape = jax.ShapeDtypeStruct(out_buf.shape, out_buf.dtype)
        want = tuple(out_buf.shape[len(out_lead):])
        assert want == ((sb, M, N) if batch else (M, N)), (want, sb, M, N)
    else:
        out_shape = jax.ShapeDtypeStruct(((sb, M, N) if batch else (M, N)), out_dtype)
    dims = {"nn": (((1,), (0,)), ((), ())), "nt": (((1,), (1,)), ((), ())), "tn": (((0,), (0,)), ((), ()))}[mode]
    n_in = len(args)
    one_step = sr * nk == 1

    def body(*refs):
        a_ref, b_ref = refs[0], refs[1]
        bias_ref = refs[2] if bias is not None else None
        o_ref = refs[n_in]

        def finish(r):
            if bias is not None:
                r = r + bias_ref[...]
            o_ref[...] = r.astype(o_ref.dtype)

        part = lax.dot_general(a_ref[...].astype(BF16), b_ref[...].astype(BF16), dims, preferred_element_type=F32)
        if one_step:
            finish(part)
            return
        acc = refs[n_in + 1]
        isr = pl.program_id(3)
        k = pl.program_id(4)

        @pl.when((isr == 0) & (k == 0))
        def _():
            acc[...] = part

        @pl.when((isr > 0) | (k > 0))
        def _():
            acc[...] += part

        @pl.when((isr == sr - 1) & (k == nk - 1))
        def _():
            finish(acc[...])

    return pl.pallas_call(
        body,
        out_shape=out_shape,
        grid=grid,
        in_specs=in_specs,
        out_specs=pl.BlockSpec(lead_blk(out_lead, batch, (tm, tn)), o_map),
        scratch_shapes=[] if one_step else [pltpu.VMEM((tm, tn), F32)],
        input_output_aliases=aliases,
        compiler_params=_params("parallel", "parallel", "parallel", "arbitrary", "arbitrary"),
        name=name,
    )(*args)


def norm_mod_fwd(x, nw, sc, sh, *, name):
    L, D = x.shape
    tl = _pick(L, 512, 16)

    def body(x_ref, nw_ref, sc_ref, sh_ref, h_ref):
        xv = x_ref[...]
        r = lax.rsqrt(jnp.mean(xv * xv, axis=-1, keepdims=True) + EPS)
        n = (xv * r) * nw_ref[...]
        h_ref[...] = (n * (1.0 + sc_ref[...]) + sh_ref[...]).astype(h_ref.dtype)

    row = pl.BlockSpec((1, D), lambda i: (0, 0))
    return pl.pallas_call(
        body,
        out_shape=jax.ShapeDtypeStruct((L, D), BF16),
        grid=(L // tl,),
        in_specs=[pl.BlockSpec((tl, D), lambda i: (i, 0)), row, row, row],
        out_specs=pl.BlockSpec((tl, D), lambda i: (i, 0)),
        compiler_params=_params("parallel"),
        name=name,
    )(x, nw, sc, sh)


def norm_mod_bwd(x, nw, sc, dh, dx_in, *, name):
    L, D = x.shape
    tl = _pick(L, 512, 16)

    def body(x_ref, nw_ref, sc_ref, dh_ref, dxi_ref, dx_ref, dnw_ref, dsc_ref, dsh_ref):
        @pl.when(pl.program_id(0) == 0)
        def _():
            dnw_ref[...] = jnp.zeros_like(dnw_ref)
            dsc_ref[...] = jnp.zeros_like(dsc_ref)
            dsh_ref[...] = jnp.zeros_like(dsh_ref)

        xv = x_ref[...]
        dh_v = dh_ref[...]
        r = lax.rsqrt(jnp.mean(xv * xv, axis=-1, keepdims=True) + EPS)
        xhat = xv * r
        nw_v = nw_ref[...]
        n = xhat * nw_v
        dsh_ref[...] += jnp.sum(dh_v, axis=0, keepdims=True)
        dsc_ref[...] += jnp.sum(dh_v * n, axis=0, keepdims=True)
        dn = dh_v * (1.0 + sc_ref[...])
        dnw_ref[...] += jnp.sum(dn * xhat, axis=0, keepdims=True)
        dxhat = dn * nw_v
        dx_ref[...] = dxi_ref[...] + r * (dxhat - xhat * jnp.mean(dxhat * xhat, axis=-1, keepdims=True))

    row = pl.BlockSpec((1, D), lambda i: (0, 0))
    tile = pl.BlockSpec((tl, D), lambda i: (i, 0))
    vec = jax.ShapeDtypeStruct((1, D), F32)
    return pl.pallas_call(
        body,
        out_shape=(jax.ShapeDtypeStruct((L, D), F32), vec, vec, vec),
        grid=(L // tl,),
        in_specs=[tile, row, row, tile, tile],
        out_specs=(tile, row, row, row),
        compiler_params=_params("arbitrary"),
        name=name,
    )(x, nw, sc, dh, dx_in)


def gate_fwd(x, f, gate, scale, *, name):
    L, D = x.shape
    tl = _pick(L, 512, 8)

    def body(x_ref, f_ref, g_ref, o_ref):
        o_ref[...] = x_ref[...] + (scale * g_ref[...]) * f_ref[...]

    tile = pl.BlockSpec((tl, D), lambda i: (i, 0))
    return pl.pallas_call(
        body,
        out_shape=jax.ShapeDtypeStruct((L, D), F32),
        grid=(L // tl,),
        in_specs=[tile, tile, pl.BlockSpec((1, D), lambda i: (0, 0))],
        out_specs=tile,
        compiler_params=_params("parallel"),
        name=name,
    )(x, f, gate)


def gate_bwd(f, dx, gate, scale, *, name):
    L, D = f.shape
    tl = _pick(L, 512, 16)

    def body(f_ref, dx_ref, g_ref, df_ref, dg_ref, dfsum_ref):
        @pl.when(pl.program_id(0) == 0)
        def _():
            dg_ref[...] = jnp.zeros_like(dg_ref)
            dfsum_ref[...] = jnp.zeros_like(dfsum_ref)

        dxv = dx_ref[...]
        df = (scale * g_ref[...]) * dxv
        df_ref[...] = df.astype(df_ref.dtype)
        dfsum_ref[...] += jnp.sum(df, axis=0, keepdims=True)
        dg_ref[...] += scale * jnp.sum(f_ref[...] * dxv, axis=0, keepdims=True)

    tile = pl.BlockSpec((tl, D), lambda i: (i, 0))
    row = pl.BlockSpec((1, D), lambda i: (0, 0))
    vec = jax.ShapeDtypeStruct((1, D), F32)
    return pl.pallas_call(
        body,
        out_shape=(jax.ShapeDtypeStruct((L, D), BF16), vec, vec),
        grid=(L // tl,),
        in_specs=[tile, tile, row],
        out_specs=(tile, row, row),
        compiler_params=_params("arbitrary"),
        name=name,
    )(f, dx, gate)


def colsum(x, *, name):
    L, N = x.shape
    tl = _pick(L, 512, 8)

    def body(x_ref, o_ref):
        @pl.when(pl.program_id(0) == 0)
        def _():
            o_ref[...] = jnp.zeros_like(o_ref)

        o_ref[...] += jnp.sum(x_ref[...], axis=0, keepdims=True)

    return pl.pallas_call(
        body,
        out_shape=jax.ShapeDtypeStruct((1, N), F32),
        grid=(L // tl,),
        in_specs=[pl.BlockSpec((tl, N), lambda i: (i, 0))],
        out_specs=pl.BlockSpec((1, N), lambda i: (0, 0)),
        compiler_params=_params("arbitrary"),
        name=name,
    )(x)


def swiglu_fwd(gu, *, name):
    _, L, T = gu.shape
    tl = _pick(L, 256, 16)

    def body(g_ref, u_ref, a_ref):
        a_ref[...] = (_silu(g_ref[...]) * u_ref[...]).astype(a_ref.dtype)

    return pl.pallas_call(
        body,
        out_shape=jax.ShapeDtypeStruct((2, L, T), BF16),
        grid=(2, L // tl),
        in_specs=[
            pl.BlockSpec((None, tl, T), lambda j, i: (j, i, 0)),
            pl.BlockSpec((None, tl, T), lambda j, i: (j + 2, i, 0)),
        ],
        out_specs=pl.BlockSpec((None, tl, T), lambda j, i: (j, i, 0)),
        compiler_params=_params("parallel", "parallel"),
        name=name,
    )(gu, gu)


def swiglu_bwd(gu, da, *, name):
    _, L, T = gu.shape
    tl = _pick(L, 256, 16)

    def body(g_ref, u_ref, da_ref, d_ref):
        g = g_ref[...]
        dav = da_ref[...]
        d_ref[0] = (dav * u_ref[...] * _dsilu(g)).astype(d_ref.dtype)
        d_ref[1] = (dav * _silu(g)).astype(d_ref.dtype)

    out = pl.pallas_call(
        body,
        out_shape=jax.ShapeDtypeStruct((2, 2, L, T), BF16),
        grid=(2, L // tl),
        in_specs=[
            pl.BlockSpec((None, tl, T), lambda j, i: (j, i, 0)),
            pl.BlockSpec((None, tl, T), lambda j, i: (j + 2, i, 0)),
            pl.BlockSpec((None, tl, T), lambda j, i: (j, i, 0)),
        ],
        out_specs=pl.BlockSpec((2, None, tl, T), lambda j, i: (0, j, i, 0)),
        compiler_params=_params("parallel", "parallel"),
        name=name,
    )(gu, gu, da)
    return out.reshape(4, L, T)


def _shift_down(u, k, rows):
    if k == 0:
        return u
    return jnp.where(rows >= k, pltpu.roll(u, k, 0), 0.0)


def _shift_up(u, k, rows, n):
    if k == 0:
        return u
    return jnp.where(rows < n - k, pltpu.roll(u, n - k, 0), 0.0)


def _conv_pre(u, w_ref, b_ref, rows):
    pre = b_ref[...] + w_ref[CONV_WIDTH - 1 : CONV_WIDTH, :] * u
    for k in range(1, CONV_WIDTH):
        pre = pre + w_ref[CONV_WIDTH - 1 - k : CONV_WIDTH - k, :] * _shift_down(u, k, rows)
    return pre


def conv_fwd(zx, conv_w, conv_b, d_inner, *, name):
    L = zx.shape[0]
    C = conv_w.shape[1]
    tc = 256
    off = d_inner // tc

    def body(u_ref, w_ref, b_ref, o_ref):
        rows = lax.broadcasted_iota(jnp.int32, (L, tc), 0)
        o_ref[...] = _silu(_conv_pre(u_ref[...], w_ref, b_ref, rows))

    return pl.pallas_call(
        body,
        out_shape=jax.ShapeDtypeStruct((L, C), F32),
        grid=(C // tc,),
        in_specs=[
            pl.BlockSpec((L, tc), lambda j: (0, off + j)),
            pl.BlockSpec((CONV_WIDTH, tc), lambda j: (0, j)),
            pl.BlockSpec((1, tc), lambda j: (0, j)),
        ],
        out_specs=pl.BlockSpec((L, tc), lambda j: (0, j)),
        compiler_params=_params("parallel"),
        name=name,
    )(zx, conv_w, conv_b)


def conv_bwd(zx, conv_w, conv_b, dxbc, d_inner, *, name):
    L = zx.shape[0]
    C = conv_w.shape[1]
    tc = 256
    off = d_inner // tc

    def body(u_ref, w_ref, b_ref, d_ref, du_ref, dw_ref, db_ref):
        rows = lax.broadcasted_iota(jnp.int32, (L, tc), 0)
        u = u_ref[...]
        dpre = d_ref[...] * _dsilu(_conv_pre(u, w_ref, b_ref, rows))
        db_ref[...] = jnp.sum(dpre, axis=0, keepdims=True)
        du = w_ref[CONV_WIDTH - 1 : CONV_WIDTH, :] * dpre
        dw_ref[CONV_WIDTH - 1 : CONV_WIDTH, :] = jnp.sum(dpre * u, axis=0, keepdims=True)
        for k in range(1, CONV_WIDTH):
            j = CONV_WIDTH - 1 - k
            dw_ref[j : j + 1, :] = jnp.sum(dpre * _shift_down(u, k, rows), axis=0, keepdims=True)
            du = du + w_ref[j : j + 1, :] * _shift_up(dpre, k, rows, L)
        du_ref[...] = du

    return pl.pallas_call(
        body,
        out_shape=(
            jax.ShapeDtypeStruct((L, C), F32),
            jax.ShapeDtypeStruct((CONV_WIDTH, C), F32),
            jax.ShapeDtypeStruct((1, C), F32),
        ),
        grid=(C // tc,),
        in_specs=[
            pl.BlockSpec((L, tc), lambda j: (0, off + j)),
            pl.BlockSpec((CONV_WIDTH, tc), lambda j: (0, j)),
            pl.BlockSpec((1, tc), lambda j: (0, j)),
            pl.BlockSpec((L, tc), lambda j: (0, j)),
        ],
        out_specs=(
            pl.BlockSpec((L, tc), lambda j: (0, j)),
            pl.BlockSpec((CONV_WIDTH, tc), lambda j: (0, j)),
            pl.BlockSpec((1, tc), lambda j: (0, j)),
        ),
        compiler_params=_params("parallel"),
        name=name,
    )(zx, conv_w, conv_b, dxbc)


def _ssd_head(xs, dt, a_neg, dsk, bm, cm, prev):
    q = xs.shape[0]
    li = lax.broadcasted_iota(jnp.int32, (q, q), 0)
    si = lax.broadcasted_iota(jnp.int32, (q, q), 1)
    causal = li >= si
    a = dt * a_neg
    acs = jnp.dot(causal.astype(F32), jnp.broadcast_to(a, (q, q)), precision=HIGHEST, preferred_element_type=F32)
    tot = jnp.sum(a, axis=0, keepdims=True)
    lmat = jnp.exp(jnp.where(causal, acs - acs.T, -jnp.inf))
    cb = lax.dot_general(cm.astype(BF16), bm.astype(BF16), (((1,), (1,)), ((), ())), preferred_element_type=F32)
    xdt = xs * dt
    y = jnp.dot((cb * lmat).astype(BF16), xdt.astype(BF16), preferred_element_type=F32)
    y = y + lax.dot_general(
        (cm * jnp.exp(acs)).astype(BF16), prev.astype(BF16), (((1,), (1,)), ((), ())), preferred_element_type=F32
    )
    y = y + dsk * xs
    st = lax.dot_general(
        xdt.astype(BF16), (bm * jnp.exp(tot - acs)).astype(BF16), (((0,), (0,)), ((), ())), preferred_element_type=F32
    )
    return y, prev * jnp.exp(tot) + st


def _pick_lane(v, h):
    lanes = lax.broadcasted_iota(jnp.int32, v.shape, 1)
    return jnp.sum(jnp.where(lanes == h, v, 0.0), axis=1, keepdims=True)


def _softplus(x):
    return jnp.maximum(x, 0.0) + jnp.log(1.0 + jnp.exp(-jnp.abs(x)))


def _ssd_specs(L, d_inner, H, nc, rev):
    R = H // SSM_GROUPS
    P, N, Q = SSM_HEADDIM, SSM_STATE, CHUNK
    ngrp = SSM_GROUPS

    def ci(c):
        return (nc - 1 - c) if rev else c

    b_off = d_inner // N
    c_off = b_off + ngrp
    xs = pl.BlockSpec((Q, R * P), lambda c, g: (ci(c), g))
    bm = pl.BlockSpec((Q, N), lambda c, g: (ci(c), b_off + g))
    cm = pl.BlockSpec((Q, N), lambda c, g: (ci(c), c_off + g))
    dt = pl.BlockSpec((Q, H), lambda c, g: (ci(c), 0))
    hv = pl.BlockSpec((1, H), lambda c, g: (0, 0))
    y = pl.BlockSpec((Q, R * P), lambda c, g: (ci(c), g))
    st = pl.BlockSpec((None, R * P, N), lambda c, g: (ci(c), g, 0))
    return R, xs, bm, cm, dt, hv, y, st


def ssd_fwd(xbc, dt_raw, dt_bias, a_log, d_skip, d_inner, *, name):
    L = xbc.shape[0]
    H = dt_raw.shape[1]
    nc = L // CHUNK
    P, N = SSM_HEADDIM, SSM_STATE
    R, xs_s, bm_s, cm_s, dt_s, hv_s, y_s, st_s = _ssd_specs(L, d_inner, H, nc, False)

    def body(xs_ref, bm_ref, cm_ref, dt_ref, bias_ref, alog_ref, dsk_ref, y_ref, st_ref, state):
        c = pl.program_id(0)
        g = pl.program_id(1)

        @pl.when(c == 0)
        def _():
            for r in range(R):
                state[g * R + r] = jnp.zeros((P, N), F32)

        dtb = _softplus(dt_ref[...] + bias_ref[...])
        a_all = -jnp.exp(alog_ref[...])
        prevs = [state[g * R + r] for r in range(R)]
        res = []
        for r in range(R):
            h = g * R + r
            res.append(_ssd_head(
                xs_ref[:, r * P : (r + 1) * P],
                _pick_lane(dtb, h),
                _pick_lane(a_all, h),
                _pick_lane(dsk_ref[...], h),
                bm_ref[...],
                cm_ref[...],
                prevs[r],
            ))
        for r in range(R):
            st_ref[r * P : (r + 1) * P, :] = prevs[r]
            y_ref[:, r * P : (r + 1) * P] = res[r][0]
            state[g * R + r] = res[r][1]

    return pl.pallas_call(
        body,
        out_shape=(jax.ShapeDtypeStruct((L, d_inner), F32), jax.ShapeDtypeStruct((nc, H * P, N), F32)),
        grid=(nc, SSM_GROUPS),
        in_specs=[xs_s, bm_s, cm_s, dt_s, hv_s, hv_s, hv_s],
        out_specs=(y_s, st_s),
        scratch_shapes=[pltpu.VMEM((H, P, N), F32)],
        compiler_params=_params("arbitrary", "arbitrary"),
        name=name,
    )(xbc, xbc, xbc, dt_raw, dt_bias, a_log, d_skip)


def ssd_bwd(xbc, dt_raw, dt_bias, a_log, d_skip, states, dy, d_inner, *, name):
    L, C = xbc.shape
    H = dt_raw.shape[1]
    nc = L // CHUNK
    P, N, Q = SSM_HEADDIM, SSM_STATE, CHUNK
    R, xs_s, bm_s, cm_s, dt_s, hv_s, y_s, st_s = _ssd_specs(L, d_inner, H, nc, True)

    def body(xs_ref, bm_ref, cm_ref, dt_ref, bias_ref, alog_ref, dsk_ref, st_ref, dy_ref,
             dxs_ref, dbm_ref, dcm_ref, ddt_ref, dbias_ref, dalog_ref, ddsk_ref, dstate):
        c = pl.program_id(0)
        g = pl.program_id(1)

        @pl.when(c == 0)
        def _():
            for r in range(R):
                dstate[g * R + r] = jnp.zeros((P, N), F32)

        @pl.when((c == 0) & (g == 0))
        def _():
            dbias_ref[...] = jnp.zeros_like(dbias_ref)
            dalog_ref[...] = jnp.zeros_like(dalog_ref)
            ddsk_ref[...] = jnp.zeros_like(ddsk_ref)

        @pl.when(g == 0)
        def _():
            ddt_ref[...] = jnp.zeros_like(ddt_ref)

        pre = dt_ref[...] + bias_ref[...]
        dtb = _softplus(pre)
        a_all = -jnp.exp(alog_ref[...])
        lanes_q = lax.broadcasted_iota(jnp.int32, (Q, H), 1)
        lanes_1 = lax.broadcasted_iota(jnp.int32, (1, H), 1)
        bm = bm_ref[...]
        cm = cm_ref[...]
        dbm = jnp.zeros((Q, N), F32)
        dcm = jnp.zeros((Q, N), F32)
        ddt_blk = jnp.zeros((Q, H), F32)
        da_row = jnp.zeros((1, H), F32)
        dd_row = jnp.zeros((1, H), F32)
        dstates = [dstate[g * R + r] for r in range(R)]
        dprevs = []
        for r in range(R):
            h = g * R + r
            args = (
                xs_ref[:, r * P : (r + 1) * P],
                _pick_lane(dtb, h),
                _pick_lane(a_all, h),
                _pick_lane(dsk_ref[...], h),
                bm,
                cm,
                st_ref[r * P : (r + 1) * P, :],
            )
            _, vjp = jax.vjp(_ssd_head, *args)
            dxs, ddt, da, dd, dbm_h, dcm_h, dprev = vjp((dy_ref[:, r * P : (r + 1) * P], dstates[r]))
            dxs_ref[:, r * P : (r + 1) * P] = dxs
            dprevs.append(dprev)
            dbm = dbm + dbm_h
            dcm = dcm + dcm_h
            ddt_blk = ddt_blk + jnp.where(lanes_q == h, ddt, 0.0)
            da_row = da_row + jnp.where(lanes_1 == h, da, 0.0)
            dd_row = dd_row + jnp.where(lanes_1 == h, dd, 0.0)
        for r in range(R):
            dstate[g * R + r] = dprevs[r]
        dbm_ref[...] = dbm
        dcm_ref[...] = dcm
        ddt_pre = ddt_blk * _sigmoid(pre)
        ddt_ref[...] += ddt_pre
        dbias_ref[...] += jnp.sum(ddt_pre, axis=0, keepdims=True)
        dalog_ref[...] += da_row * a_all
        ddsk_ref[...] += dd_row

    ngrp = SSM_GROUPS
    hrow = jax.ShapeDtypeStruct((1, H), F32)
    dxs, dbm, dcm, ddt, dbias, dalog, ddsk = pl.pallas_call(
        body,
        out_shape=(
            jax.ShapeDtypeStruct((L, d_inner), F32),
            jax.ShapeDtypeStruct((L, ngrp * N), F32),
            jax.ShapeDtypeStruct((L, ngrp * N), F32),
            jax.ShapeDtypeStruct((L, H), F32),
            hrow,
            hrow,
            hrow,
        ),
        grid=(nc, ngrp),
        in_specs=[xs_s, bm_s, cm_s, dt_s, hv_s, hv_s, hv_s, st_s, y_s],
        out_specs=(
            y_s,
            pl.BlockSpec((Q, N), lambda c, g: (nc - 1 - c, g)),
            pl.BlockSpec((Q, N), lambda c, g: (nc - 1 - c, g)),
            dt_s,
            hv_s,
            hv_s,
            hv_s,
        ),
        scratch_shapes=[pltpu.VMEM((H, P, N), F32)],
        compiler_params=_params("arbitrary", "arbitrary"),
        name=name,
    )(xbc, xbc, xbc, dt_raw, dt_bias, a_log, d_skip, states, dy)
    return jnp.concatenate([dxs, dbm, dcm], axis=1), ddt, dbias, dalog, ddsk


def gnorm_fwd(y, zx, nw, *, name):
    L, DI = y.shape
    gw = DI // SSM_GROUPS
    tl = _pick(L, 512, 16)

    def body(y_ref, z_ref, nw_ref, o_ref):
        yz = y_ref[...] * _silu(z_ref[...])
        r = lax.rsqrt(jnp.mean(yz * yz, axis=-1, keepdims=True) + EPS)
        o_ref[...] = ((yz * r) * nw_ref[...]).astype(o_ref.dtype)

    tile = pl.BlockSpec((tl, gw), lambda i, g: (i, g))
    return pl.pallas_call(
        body,
        out_shape=jax.ShapeDtypeStruct((L, DI), BF16),
        grid=(L // tl, SSM_GROUPS),
        in_specs=[tile, tile, pl.BlockSpec((1, gw), lambda i, g: (0, g))],
        out_specs=tile,
        compiler_params=_params("parallel", "parallel"),
        name=name,
    )(y, zx, nw)


def gnorm_bwd(y, zx, nw, dout, *, name):
    L, DI = y.shape
    gw = DI // SSM_GROUPS
    tl = _pick(L, 512, 16)

    def body(y_ref, z_ref, nw_ref, do_ref, dy_ref, dz_ref, dnw_ref):
        @pl.when(pl.program_id(1) == 0)
        def _():
            dnw_ref[...] = jnp.zeros_like(dnw_ref)

        yv = y_ref[...]
        zv = z_ref[...]
        sz = _silu(zv)
        yz = yv * sz
        r = lax.rsqrt(jnp.mean(yz * yz, axis=-1, keepdims=True) + EPS)
        n = yz * r
        dov = do_ref[...]
        dnw_ref[...] += jnp.sum(dov * n, axis=0, keepdims=True)
        dn = dov * nw_ref[...]
        dyz = r * (dn - n * jnp.mean(dn * n, axis=-1, keepdims=True))
        dy_ref[...] = dyz * sz
        dz_ref[...] = dyz * yv * _dsilu(zv)

    tile = pl.BlockSpec((tl, gw), lambda g, i: (i, g))
    row = pl.BlockSpec((1, gw), lambda g, i: (0, g))
    return pl.pallas_call(
        body,
        out_shape=(
            jax.ShapeDtypeStruct((L, DI), F32),
            jax.ShapeDtypeStruct((L, DI), F32),
            jax.ShapeDtypeStruct((1, DI), F32),
        ),
        grid=(SSM_GROUPS, L // tl),
        in_specs=[tile, tile, row, tile],
        out_specs=(tile, tile, row),
        compiler_params=_params("parallel", "arbitrary"),
        name=name,
    )(y, zx, nw, dout)


def _attn_head(q, kp, kc, vp, vc, sink, has_prev):
    w = q.shape[0]
    nt = (((1,), (1,)), ((), ()))
    qb = q.astype(BF16)
    sc = lax.dot_general(qb, kc.astype(BF16), nt, preferred_element_type=F32) * HEAD_DIM ** -0.5
    sp = lax.dot_general(qb, kp.astype(BF16), nt, preferred_element_type=F32) * HEAD_DIM ** -0.5
    ii = lax.broadcasted_iota(jnp.int32, (w, w), 0)
    jj = lax.broadcasted_iota(jnp.int32, (w, w), 1)
    lc = jnp.where(jj <= ii, sc, -jnp.inf)
    lp = jnp.where((jj > ii) & has_prev, sp, -jnp.inf)
    m = jnp.maximum(jnp.maximum(jnp.max(lc, axis=1, keepdims=True), jnp.max(lp, axis=1, keepdims=True)), sink)
    m = lax.stop_gradient(m)
    pc = jnp.exp(lc - m)
    pp = jnp.exp(lp - m)
    denom = jnp.sum(pc, axis=1, keepdims=True) + jnp.sum(pp, axis=1, keepdims=True) + jnp.exp(sink - m)
    o = jnp.dot((pc / denom).astype(BF16), vc.astype(BF16), preferred_element_type=F32)
    return o + jnp.dot((pp / denom).astype(BF16), vp.astype(BF16), preferred_element_type=F32)


def attn_fwd(q, kv, sinks, *, name):
    L, DQ = q.shape
    heads = DQ // HEAD_DIM
    rep = heads // KV_HEADS
    nb = L // WINDOW
    kw = KV_HEADS * HEAD_DIM
    W, HD = WINDOW, HEAD_DIM

    def body(q_ref, kp_ref, kc_ref, vp_ref, vc_ref, s_ref, o_ref):
        has_prev = pl.program_id(0) > 0
        for h in range(heads):
            kh = h // rep
            ks = slice(kh * HD, (kh + 1) * HD)
            o = _attn_head(
                q_ref[:, h * HD : (h + 1) * HD], kp_ref[:, ks], kc_ref[:, ks], vp_ref[:, ks], vc_ref[:, ks],
                s_ref[:, h : h + 1], has_prev,
            )
            o_ref[:, h * HD : (h + 1) * HD] = o.astype(o_ref.dtype)

    return pl.pallas_call(
        body,
        out_shape=jax.ShapeDtypeStruct((L, DQ), BF16),
        grid=(nb,),
        in_specs=[
            pl.BlockSpec((W, DQ), lambda n: (n, 0)),
            pl.BlockSpec((W, kw), lambda n: (jnp.maximum(n - 1, 0), 0)),
            pl.BlockSpec((W, kw), lambda n: (n, 0)),
            pl.BlockSpec((W, kw), lambda n: (jnp.maximum(n - 1, 0), 1)),
            pl.BlockSpec((W, kw), lambda n: (n, 1)),
            pl.BlockSpec((1, heads), lambda n: (0, 0)),
        ],
        out_specs=pl.BlockSpec((W, DQ), lambda n: (n, 0)),
        compiler_params=_params("parallel"),
        name=name,
    )(q, kv, kv, kv, kv, sinks)


def attn_bwd(q, kv, sinks, do, dkv_in, *, name):
    L, DQ = q.shape
    heads = DQ // HEAD_DIM
    rep = heads // KV_HEADS
    nb = L // WINDOW
    kw = KV_HEADS * HEAD_DIM
    W, HD = WINDOW, HEAD_DIM

    def blk(n):
        return jnp.minimum(n, nb - 1)

    def prev(n):
        return jnp.maximum(blk(n) - 1, 0)

    def outb(n):
        return jnp.maximum(n - 1, 0)

    def body(q_ref, kp_ref, kc_ref, vp_ref, vc_ref, s_ref, do_ref, dki_ref, dvi_ref,
             dq_ref, dk_ref, dv_ref, ds_ref, dk_cur, dv_cur):
        n = pl.program_id(0)
        has_prev = n > 0

        @pl.when(n == 0)
        def _():
            ds_ref[...] = jnp.zeros_like(ds_ref)
            dk_cur[...] = jnp.zeros_like(dk_cur)
            dv_cur[...] = jnp.zeros_like(dv_cur)

        @pl.when(n == nb)
        def _():
            dk_ref[...] = dki_ref[...] + dk_cur[...]
            dv_ref[...] = dvi_ref[...] + dv_cur[...]

        @pl.when(n < nb)
        def _():
            lanes = lax.broadcasted_iota(jnp.int32, (1, heads), 1)
            ds_row = jnp.zeros((1, heads), F32)
            for kh in range(KV_HEADS):
                ks = slice(kh * HD, (kh + 1) * HD)
                kp, kc, vp, vc = kp_ref[:, ks], kc_ref[:, ks], vp_ref[:, ks], vc_ref[:, ks]
                dkp = jnp.zeros((W, HD), F32)
                dkc = jnp.zeros((W, HD), F32)
                dvp = jnp.zeros((W, HD), F32)
                dvc = jnp.zeros((W, HD), F32)
                for rr in range(rep):
                    h = kh * rep + rr
                    hs = slice(h * HD, (h + 1) * HD)
                    _, vjp = jax.vjp(
                        functools.partial(_attn_head, has_prev=has_prev),
                        q_ref[:, hs], kp, kc, vp, vc, s_ref[:, h : h + 1],
                    )
                    dq, a, b, c, d, dsk = vjp(do_ref[:, hs])
                    dq_ref[:, hs] = dq
                    dkp, dkc, dvp, dvc = dkp + a, dkc + b, dvp + c, dvc + d
                    ds_row = ds_row + jnp.where(lanes == h, dsk, 0.0)
                dk_ref[:, ks] = dki_ref[:, ks] + dk_cur[:, ks] + dkp
                dv_ref[:, ks] = dvi_ref[:, ks] + dv_cur[:, ks] + dvp
                dk_cur[:, ks] = dkc
                dv_cur[:, ks] = dvc
            ds_ref[...] += ds_row

    dq, dk, dv, ds = pl.pallas_call(
        body,
        out_shape=(
            jax.ShapeDtypeStruct((L, DQ), F32),
            jax.ShapeDtypeStruct((L, kw), F32),
            jax.ShapeDtypeStruct((L, kw), F32),
            jax.ShapeDtypeStruct((1, heads), F32),
        ),
        grid=(nb + 1,),
        in_specs=[
            pl.BlockSpec((W, DQ), lambda n: (blk(n), 0)),
            pl.BlockSpec((W, kw), lambda n: (prev(n), 0)),
            pl.BlockSpec((W, kw), lambda n: (blk(n), 0)),
            pl.BlockSpec((W, kw), lambda n: (prev(n), 1)),
            pl.BlockSpec((W, kw), lambda n: (blk(n), 1)),
            pl.BlockSpec((1, heads), lambda n: (0, 0)),
            pl.BlockSpec((W, DQ), lambda n: (blk(n), 0)),
            pl.BlockSpec((W, kw), lambda n: (outb(n), 0)),
            pl.BlockSpec((W, kw), lambda n: (outb(n), 1)),
        ],
        out_specs=(
            pl.BlockSpec((W, DQ), lambda n: (blk(n), 0)),
            pl.BlockSpec((W, kw), lambda n: (outb(n), 0)),
            pl.BlockSpec((W, kw), lambda n: (outb(n), 0)),
            pl.BlockSpec((1, heads), lambda n: (0, 0)),
        ),
        scratch_shapes=[pltpu.VMEM((W, kw), F32), pltpu.VMEM((W, kw), F32)],
        compiler_params=_params("arbitrary"),
        name=name,
    )(q, kv, kv, kv, kv, sinks, do, dkv_in, dkv_in)
    return dq, jnp.concatenate([dk, dv], axis=1), ds


def final_loss(x, fw, target, *, name):
    L, D = x.shape
    tl = _pick(L, 512, 8)

    def body(x_ref, fw_ref, t_ref, loss_ref, dx_ref, dfw_ref):
        @pl.when(pl.program_id(0) == 0)
        def _():
            loss_ref[...] = jnp.zeros_like(loss_ref)
            dfw_ref[...] = jnp.zeros_like(dfw_ref)

        xv = x_ref[...]
        fwv = fw_ref[...]
        r = lax.rsqrt(jnp.mean(xv * xv, axis=-1, keepdims=True) + EPS)
        xhat = xv * r
        err = xhat * fwv - t_ref[...]
        loss_ref[...] += 0.5 * jnp.sum(jnp.mean(err * err, axis=-1, keepdims=True), axis=0, keepdims=True)
        dy = err * (1.0 / D)
        dfw_ref[...] += jnp.sum(dy * xhat, axis=0, keepdims=True)
        dxhat = dy * fwv
        dx_ref[...] = r * (dxhat - xhat * jnp.mean(dxhat * xhat, axis=-1, keepdims=True))

    tile = pl.BlockSpec((tl, D), lambda i: (i, 0))
    row = pl.BlockSpec((1, D), lambda i: (0, 0))
    return pl.pallas_call(
        body,
        out_shape=(
            jax.ShapeDtypeStruct((1, 1), F32),
            jax.ShapeDtypeStruct((L, D), F32),
            jax.ShapeDtypeStruct((1, D), F32),
        ),
        grid=(L // tl,),
        in_specs=[tile, row, tile],
        out_specs=(pl.BlockSpec((1, 1), lambda i: (0, 0)), tile, row),
        compiler_params=_params("arbitrary"),
        name=name,
    )(x, fw, target)


def outer8(ct, d, *, name):
    D, B = ct.shape
    S, _, N = d.shape
    tm = _pick(D, 512, 8)
    tn = _pick(N, 256, LANE)

    def body(c_ref, d_ref, o_ref):
        acc = c_ref[:, 0:1] * d_ref[0:1, :]
        for b in range(1, B):
            acc = acc + c_ref[:, b : b + 1] * d_ref[b : b + 1, :]
        o_ref[...] = acc

    return pl.pallas_call(
        body,
        out_shape=jax.ShapeDtypeStruct((S, D, N), F32),
        grid=(S, D // tm, N // tn),
        in_specs=[
            pl.BlockSpec((tm, B), lambda s, i, j: (i, 0)),
            pl.BlockSpec((None, B, tn), lambda s, i, j: (s, 0, j)),
        ],
        out_specs=pl.BlockSpec((None, tm, tn), lambda s, i, j: (s, i, j)),
        compiler_params=_params("parallel", "parallel", "parallel"),
        name=name,
    )(ct, d)


def reduce8(g, *, name):
    nd, R, N = g.shape

    def body(g_ref, o_ref):
        acc = g_ref[0]
        for b in range(1, nd):
            acc = acc + g_ref[b]
        o_ref[...] = acc

    return pl.pallas_call(
        body,
        out_shape=jax.ShapeDtypeStruct((R, N), F32),
        name=name,
    )(g)


def _as3(a):
    if a.ndim == 1:
        return a.reshape(1, 1, -1)
    if a.ndim == 2:
        return a.reshape((1,) + a.shape)
    return a.reshape((-1,) + a.shape[-2:])


def adamw(w, g, m, v, *, name):
    shape = w.shape
    w3, g3, m3, v3 = _as3(w), _as3(g), _as3(m), _as3(v)
    B, R, C = w3.shape
    tr = _pick(R, max(8, (1 << 19) // max(C, 1) // 8 * 8), 8)

    def body(w_ref, g_ref, m_ref, v_ref, d_ref, nm_ref, nv_ref):
        gv = g_ref[...]
        mn = ADAM_B1 * m_ref[...] + (1.0 - ADAM_B1) * gv
        vn = ADAM_B2 * v_ref[...] + (1.0 - ADAM_B2) * (gv * gv)
        m_hat = mn / (1.0 - ADAM_B1 ** ADAM_STEP)
        v_hat = vn / (1.0 - ADAM_B2 ** ADAM_STEP)
        d_ref[...] = -ADAM_LR * (m_hat / (jnp.sqrt(v_hat) + ADAM_EPS) + ADAM_WD * w_ref[...])
        nm_ref[...] = mn
        nv_ref[...] = vn

    tile = pl.BlockSpec((None, tr, C), lambda b, i: (b, i, 0))
    sds = jax.ShapeDtypeStruct((B, R, C), F32)
    d, nm, nv = pl.pallas_call(
        body,
        out_shape=(sds, sds, sds),
        grid=(B, R // tr),
        in_specs=[tile, tile, tile, tile],
        out_specs=(tile, tile, tile),
        compiler_params=_params("parallel", "parallel"),
        name=name,
    )(w3, g3, m3, v3)
    return d.reshape(shape), nm.reshape(shape), nv.reshape(shape)


def _place():
    return lax.axis_index("x"), lax.axis_index("y"), lax.axis_index("c")


def _flip(v, bit):
    return (1 - v) if bit else v


def ag8(v, *, act=None, name):
    R, N = v.shape

    def body(v_ref, out_ref, stage, send_sems, recv_sems):
        x, y, c = _place()
        me = 4 * x + 2 * y + c
        val = v_ref[...]
        if act is not None:
            val = act(val)
        stage[...] = val
        out_ref[me] = val
        sends = []
        for k in range(1, N_DEV):
            px, py, pc = _flip(x, k & 4), _flip(y, k & 2), _flip(c, k & 1)
            cp = pltpu.make_async_remote_copy(
                src_ref=stage, dst_ref=out_ref.at[me], send_sem=send_sems.at[k - 1], recv_sem=recv_sems.at[k - 1],
                device_id=(px, py, pc), device_id_type=MESH,
            )
            cp.start()
            sends.append(cp)
        for k in range(1, N_DEV):
            px, py, pc = _flip(x, k & 4), _flip(y, k & 2), _flip(c, k & 1)
            pltpu.make_async_remote_copy(
                src_ref=stage, dst_ref=out_ref.at[4 * px + 2 * py + pc], send_sem=send_sems.at[k - 1],
                recv_sem=recv_sems.at[k - 1], device_id=(px, py, pc), device_id_type=MESH,
            ).wait_recv()
        for cp in sends:
            cp.wait_send()

    return pl.pallas_call(
        body,
        out_shape=jax.ShapeDtypeStruct((N_DEV, R, N), F32),
        in_specs=[pl.BlockSpec(memory_space=pltpu.VMEM)],
        out_specs=pl.BlockSpec(memory_space=pltpu.VMEM),
        scratch_shapes=[
            pltpu.VMEM((R, N), F32),
            pltpu.SemaphoreType.DMA((N_DEV - 1,)),
            pltpu.SemaphoreType.DMA((N_DEV - 1,)),
        ],
        name=name,
    )(v)


def _other_chips(x, y):
    chips = [(1 - x, y), (x, 1 - y), (1 - x, 1 - y)]
    return chips, [2 * px + py for px, py in chips]


_HBM = pl.BlockSpec(memory_space=pltpu.HBM)


_SEM = pl.BlockSpec(memory_space=pltpu.SEMAPHORE)
_ANY = pl.BlockSpec(memory_space=pl.ANY)
_EFFECT = pltpu.SideEffectType.DATAFLOW_SIDE_EFFECTING


def _gather_copies(srcs, lands, send_sems, recv_sems):
    x, y, c = _place()
    k_me = 2 * x + y
    chips, _ = _other_chips(x, y)
    cps = []
    for w in range(len(srcs)):
        for j, (px, py) in enumerate(chips):
            cps.append(pltpu.make_async_remote_copy(
                src_ref=srcs[w].at[:, c], dst_ref=lands[w].at[:, k_me, c], send_sem=send_sems.at[3 * w + j],
                recv_sem=recv_sems.at[3 * w + j], device_id=(px, py, c), device_id_type=MESH,
            ))
    return cps


def _reduce_copies(srcs, lands, send_sems, recv_sems):
    x, y, c = _place()
    chips, kidx = _other_chips(x, y)
    cps = []
    for w in range(len(srcs)):
        for j, (px, py) in enumerate(chips):
            cps.append(pltpu.make_async_remote_copy(
                src_ref=srcs[w].at[:, kidx[j]], dst_ref=lands[w].at[j], send_sem=send_sems.at[3 * w + j],
                recv_sem=recv_sems.at[3 * w + j], device_id=(px, py, c), device_id_type=MESH,
            ))
    return cps


def split_start(copies, srcs, land_shapes, after, *, name):
    n = len(srcs)

    def body(*refs):
        src_refs, land_refs = refs[:n], refs[n : 2 * n]
        send_sems, recv_sems = refs[2 * n + 1], refs[2 * n + 2]
        token = refs[-1]
        for cp in copies(src_refs, land_refs, send_sems, recv_sems):
            cp.start()
        token[...] = jnp.zeros_like(token)

    lands = [pltpu.with_memory_space_constraint(lax.empty(sh, s.dtype), pltpu.HBM) for sh, s in zip(land_shapes, srcs)]
    srcs = [pltpu.with_memory_space_constraint(s, pltpu.HBM) for s in srcs]
    out = pl.pallas_call(
        body,
        out_shape=(
            pltpu.SemaphoreType.DMA((3 * n,)), pltpu.SemaphoreType.DMA((3 * n,)),
            *[pltpu.HBM(s.shape, s.dtype) for s in srcs],
            *[pltpu.HBM(sh, s.dtype) for sh, s in zip(land_shapes, srcs)],
            jax.ShapeDtypeStruct((8, LANE), F32),
        ),
        in_specs=[_HBM] * (2 * n) + [_ANY],
        out_specs=(_SEM, _SEM, *([_HBM] * (2 * n)), pl.BlockSpec(memory_space=pltpu.VMEM)),
        input_output_aliases={i: 2 + i for i in range(2 * n)},
        compiler_params=pltpu.CompilerParams(has_side_effects=_EFFECT),
        name=name,
    )(*srcs, *lands, after)
    return out[0], out[1], list(out[2 : 2 + n]), list(out[2 + n : 2 + 2 * n]), out[-1]


def split_wait(copies, send_sems, recv_sems, srcs, lands, after, *, name):
    n = len(srcs)

    def body(*refs):
        src_refs, land_refs = refs[:n], refs[n : 2 * n]
        send_ref, recv_ref = refs[2 * n], refs[2 * n + 1]
        for cp in copies(src_refs, land_refs, send_ref, recv_ref):
            cp.wait_send()
            cp.wait_recv()

    out = pl.pallas_call(
        body,
        out_shape=tuple(pltpu.HBM(a.shape, a.dtype) for a in list(srcs) + list(lands)),
        in_specs=[_HBM] * (2 * n) + [_SEM, _SEM, _ANY],
        out_specs=tuple([_HBM] * (2 * n)),
        input_output_aliases={i: i for i in range(2 * n)},
        compiler_params=pltpu.CompilerParams(has_side_effects=_EFFECT),
        name=name,
    )(*srcs, *lands, send_sems, recv_sems, after)
    return list(out[n:])


def gather_fill(shards, lands, *, name):
    n = len(shards)

    def body(*refs):
        ins = refs[:n]
        outs = refs[2 * n : 3 * n]
        send_sems, recv_sems = refs[3 * n :]
        x, y, c = _place()
        k_me = 2 * x + y
        _, kidx = _other_chips(x, y)
        sib = (x, y, 1 - c)
        cps = []
        for w in range(n):
            cps.append(pltpu.make_async_remote_copy(
                src_ref=ins[w], dst_ref=outs[w].at[:, k_me], send_sem=send_sems.at[4 * w + 3], recv_sem=recv_sems.at[4 * w + 3],
                device_id=sib, device_id_type=MESH,
            ))
            for j in range(3):
                part = outs[w].at[:, kidx[j], c]
                cps.append(pltpu.make_async_remote_copy(
                    src_ref=part, dst_ref=part, send_sem=send_sems.at[4 * w + j], recv_sem=recv_sems.at[4 * w + j],
                    device_id=sib, device_id_type=MESH,
                ))
        for cp in cps:
            cp.start()
        for w in range(n):
            pltpu.make_async_remote_copy(
                src_ref=ins[w], dst_ref=outs[w].at[:, k_me], send_sem=send_sems.at[4 * w + 3], recv_sem=recv_sems.at[4 * w + 3],
                device_id=sib, device_id_type=MESH,
            ).wait_recv()
            for j in range(3):
                got = outs[w].at[:, kidx[j], 1 - c]
                pltpu.make_async_remote_copy(
                    src_ref=got, dst_ref=got, send_sem=send_sems.at[4 * w + j], recv_sem=recv_sems.at[4 * w + j],
                    device_id=sib, device_id_type=MESH,
                ).wait_recv()
        for cp in cps:
            cp.wait_send()

    return list(pl.pallas_call(
        body,
        out_shape=tuple(jax.ShapeDtypeStruct(a.shape, a.dtype) for a in lands),
        in_specs=[_HBM] * (2 * n),
        out_specs=tuple([_HBM] * n),
        scratch_shapes=[pltpu.SemaphoreType.DMA((4 * n,)), pltpu.SemaphoreType.DMA((4 * n,))],
        input_output_aliases={n + w: w for w in range(n)},
        name=name,
    )(*shards, *lands))


def rs_sibling(grads, *, name):
    n = len(grads)

    def body(*refs):
        ins, outs = refs[:n], refs[n : 2 * n]
        send_sems, recv_sems = refs[2 * n :]
        x, y, c = _place()
        cps = []
        for w in range(n):
            cp = pltpu.make_async_remote_copy(
                src_ref=ins[w].at[:, :, 1 - c], dst_ref=outs[w], send_sem=send_sems.at[w], recv_sem=recv_sems.at[w],
                device_id=(x, y, 1 - c), device_id_type=MESH,
            )
            cp.start()
            cps.append(cp)
        for cp in cps:
            cp.wait()

    return list(pl.pallas_call(
        body,
        out_shape=tuple(jax.ShapeDtypeStruct(g.shape[:2] + g.shape[3:], g.dtype) for g in grads),
        in_specs=[_HBM] * n,
        out_specs=tuple([_HBM] * n),
        scratch_shapes=[pltpu.SemaphoreType.DMA((n,)), pltpu.SemaphoreType.DMA((n,))],
        name=name,
    )(*grads))


def rs_share(halves, *, name):
    n = len(halves)

    def body(*refs):
        outs = refs[n : 2 * n]
        send_sems, recv_sems = refs[2 * n :]
        x, y, c = _place()
        cps = []
        for w in range(n):
            cp = pltpu.make_async_remote_copy(
                src_ref=outs[w].at[:, c], dst_ref=outs[w].at[:, c], send_sem=send_sems.at[w], recv_sem=recv_sems.at[w],
                device_id=(x, y, 1 - c), device_id_type=MESH,
            )
            cp.start()
            cps.append(cp)
        for w, cp in enumerate(cps):
            cp.wait_send()
            pltpu.make_async_remote_copy(
                src_ref=outs[w].at[:, c], dst_ref=outs[w].at[:, 1 - c], send_sem=send_sems.at[w], recv_sem=recv_sems.at[w],
                device_id=(x, y, 1 - c), device_id_type=MESH,
            ).wait_recv()

    return pl.pallas_call(
        body,
        out_shape=tuple(jax.ShapeDtypeStruct(h.shape, h.dtype) for h in halves),
        in_specs=[_HBM] * n,
        out_specs=tuple([_HBM] * n),
        scratch_shapes=[pltpu.SemaphoreType.DMA((n,)), pltpu.SemaphoreType.DMA((n,))],
        input_output_aliases={w: w for w in range(n)},
        name=name,
    )(*halves)


def _row_tile(R, C):
    return _pick(R, max(16, (1 << 19) // C // 16 * 16), 16)


def _my_core():
    return lax.axis_index("c")


def _my_chip():
    return 2 * lax.axis_index("x") + lax.axis_index("y")


def rs_add_pair(g, r, *, name):
    M, K, _, R, C = g.shape
    tr = _row_tile(R, C)

    def body(g_ref, r_ref, o_ref):
        o_ref[...] = (g_ref[...].astype(F32) + r_ref[...].astype(F32)).astype(o_ref.dtype)

    blk = pl.BlockSpec((None, None, tr, C), lambda m, k, i: (m, k, i, 0))
    return pl.pallas_call(
        body,
        out_shape=jax.ShapeDtypeStruct((M, K, R, C), BF16),
        grid=(M, K, R // tr),
        in_specs=[pl.BlockSpec((None, None, None, tr, C), lambda m, k, i: (m, k, _my_core(), i, 0)), blk],
        out_specs=blk,
        compiler_params=_params("parallel", "parallel", "parallel"),
        name=name,
    )(g, r)


def rs_add_final(g, r, t, full, m0, *, name):
    M, K, _, R, C = g.shape
    tr = _row_tile(R, C)

    def body(g_ref, r_ref, t_ref, full_ref, o_ref):
        acc = g_ref[...].astype(F32) + r_ref[...].astype(F32)
        for j in range(3):
            acc = acc + t_ref[j].astype(F32)
        o_ref[...] = acc

    return pl.pallas_call(
        body,
        out_shape=jax.ShapeDtypeStruct(full.shape, full.dtype),
        grid=(M, R // tr),
        in_specs=[
            pl.BlockSpec((None, None, None, tr, C), lambda m, i: (m, _my_chip(), _my_core(), i, 0)),
            pl.BlockSpec((None, None, tr, C), lambda m, i: (m, _my_chip(), i, 0)),
            pl.BlockSpec((3, None, tr, C), lambda m, i: (0, m, i, 0)),
            _ANY,
        ],
        out_specs=pl.BlockSpec((None, None, tr, C), lambda m, i: (m0 + m, _my_core(), i, 0)),
        input_output_aliases={3: 0},
        compiler_params=_params("parallel", "parallel"),
        name=name,
    )(g, r, t, full)


WEIGHTS = ["ffn_norm_w", "ffn_w_gu", "ffn_w_down", "mod_w", "mod_b", "mix_norm_w", "ssm_w_in", "ssm_conv_w", "ssm_conv_b",
           "ssm_dt_bias", "ssm_a_log", "ssm_d", "ssm_norm_w", "ssm_w_out", "kv_norm_w", "kv_mod_w", "kv_mod_b", "w_kv", "b_kv",
           "attn_w_q", "attn_b_q", "attn_sinks", "attn_w_o", "attn_b_o", "final_norm_w"]
GATHERED = ["ffn_w_gu", "ffn_w_down", "ssm_w_in", "ssm_w_out", "w_kv", "attn_w_q", "attn_w_o"]
COLUMN_PARALLEL = ["mod_w", "kv_mod_w"]
SMALL_SHARDED = ["ffn_norm_w", "ssm_conv_w", "ssm_conv_b", "ssm_norm_w"]
SMALL = [n for n in WEIGHTS if n not in GATHERED and n not in COLUMN_PARALLEL]


def _row_halves(a):
    a = a.reshape((-1,) + a.shape[-2:])
    return a.reshape(a.shape[0], 2, a.shape[1] // 2, a.shape[2])


def _pack(arrs, rows=8):
    flat = jnp.concatenate([a.reshape(-1) for a in arrs])
    n = flat.shape[0]
    pad = (-n) % (rows * LANE)
    return jnp.pad(flat, (0, pad)).reshape(rows, -1), n


def _unpack(flat, like):
    out, o = [], 0
    for s in like:
        k = 1
        for d in s:
            k *= d
        out.append(flat[o : o + k].reshape(s))
        o += k
    return out


def kernel(x, c, ffn_norm_w, ffn_w_gu, ffn_w_down, mod_w, mod_b, mix_norm_w, ssm_w_in, ssm_conv_w, ssm_conv_b, ssm_dt_bias, ssm_a_log, ssm_d, ssm_norm_w, ssm_w_out, kv_norm_w, kv_mod_w, kv_mod_b, w_kv, b_kv, attn_w_q, attn_b_q, attn_sinks, attn_w_o, attn_b_o, final_norm_w, loss_target, m_ffn_norm_w, m_ffn_w_gu, m_ffn_w_down, m_mod_w, m_mod_b, m_mix_norm_w, m_ssm_w_in, m_ssm_conv_w, m_ssm_conv_b, m_ssm_dt_bias, m_ssm_a_log, m_ssm_d, m_ssm_norm_w, m_ssm_w_out, m_kv_norm_w, m_kv_mod_w, m_kv_mod_b, m_w_kv, m_b_kv, m_attn_w_q, m_attn_b_q, m_attn_sinks, m_attn_w_o, m_attn_b_o, m_final_norm_w, v_ffn_norm_w, v_ffn_w_gu, v_ffn_w_down, v_mod_w, v_mod_b, v_mix_norm_w, v_ssm_w_in, v_ssm_conv_w, v_ssm_conv_b, v_ssm_dt_bias, v_ssm_a_log, v_ssm_d, v_ssm_norm_w, v_ssm_w_out, v_kv_norm_w, v_kv_mod_w, v_kv_mod_b, v_w_kv, v_b_kv, v_attn_w_q, v_attn_b_q, v_attn_sinks, v_attn_w_o, v_attn_b_o, v_final_norm_w):
    env = dict(locals())
    W = {n: env[n] for n in WEIGHTS}
    MOM = {n: env["m_" + n] for n in WEIGHTS}
    VAR = {n: env["v_" + n] for n in WEIGHTS}

    ax, ay, ac = _place()
    kme = 2 * ax + ay
    me = 4 * ax + 2 * ay + ac

    xs = x[0]
    target = loss_target[0]
    L, D = xs.shape
    depth, n_a = ffn_w_gu.shape[0], ssm_w_in.shape[0]
    n_b = depth - n_a
    T = ffn_w_gu.shape[-1]
    DI = ssm_w_out.shape[1] * N_CHIPS
    CI = ssm_w_in.shape[2]
    CC = ssm_conv_w.shape[2] * N_CHIPS
    MW = mod_w.shape[2]
    KW = kv_mod_w.shape[1]
    KVD = w_kv.shape[1]

    def chip_cols(a, width):
        return lax.dynamic_slice_in_dim(a, kme * width, width, axis=a.ndim - 1)

    def stage_arrays(i):
        out = [("ffn_w_gu", 2 * i, ffn_w_gu[i]), ("ffn_w_down", 2 * i, ffn_w_down[i])]
        if i < n_a:
            out += [("ssm_w_in", i, ssm_w_in[i]), ("ssm_w_out", i, ssm_w_out[i])]
        else:
            out += [("attn_w_q", i - n_a, attn_w_q[i - n_a]), ("attn_w_o", i - n_a, attn_w_o[i - n_a])]
        if i == n_a:
            out.append(("w_kv", 0, w_kv))
        return out

    gw, inflight = {}, {}

    def gather_begin(i, after):
        names = [n for n, _, _ in stage_arrays(i)]
        shards = [_row_halves(a.astype(BF16)) for _, _, a in stage_arrays(i)]
        land_shapes = [(s.shape[0], N_CHIPS) + s.shape[1:] for s in shards]
        ssem, rsem, srcs, lands, token = split_start(_gather_copies, shards, land_shapes, after, name=f"gather_start_{i}")
        inflight[i] = (names, ssem, rsem, srcs, lands)
        return token[0:1, 0:1]

    def gather_end(i, after):
        names, ssem, rsem, srcs, lands = inflight.pop(i)
        lands = split_wait(_gather_copies, ssem, rsem, srcs, lands, after, name=f"gather_wait_{i}")
        lands = gather_fill(srcs, lands, name=f"gather_fill_{i}")
        g = dict(zip(names, lands))
        gw[i] = {
            "gu": g["ffn_w_gu"].reshape(2, N_CHIPS, D, T),
            "dn": g["ffn_w_down"].reshape(2, 2, T, D),
        }
        if i < n_a:
            gw[i]["in"] = g["ssm_w_in"].reshape(N_CHIPS, D, CI)
            gw[i]["out"] = g["ssm_w_out"].reshape(DI, D)
        else:
            gw[i]["q"] = g["attn_w_q"].reshape(D, D)
            gw[i]["o"] = g["attn_w_o"].reshape(D, D)
        if i == n_a:
            gw[i]["kv"] = g["w_kv"].reshape(D, KVD)
        return lands[0]

    c = c + gather_begin(0, c)

    sm_like = [W[n].shape for n in SMALL_SHARDED]
    sm_pack, sm_n = _pack([W[n] for n in SMALL_SHARDED])
    sm_all = ag8(sm_pack, name="ag_small_w")[0::2].reshape(N_CHIPS, -1)[:, :sm_n]
    full = {}
    for n, part in zip(SMALL_SHARDED, zip(*[_unpack(sm_all[k], sm_like) for k in range(N_CHIPS)])):
        full[n] = jnp.concatenate(part, axis=-1)

    c_all = ag8(c, act=_silu, name="ag_c").reshape(N_DEV, D)
    p_mod = mm(c_all, mod_w, bias=chip_cols(mod_b, MW)[:, None, :], name="mod_mm")
    p_kv = mm(c_all, kv_mod_w, bias=chip_cols(kv_mod_b, KW)[None, :], name="kvmod_mm")
    p_all = jnp.concatenate([jnp.transpose(p_mod, (1, 0, 2)).reshape(N_DEV, depth * MW), p_kv], axis=1)
    p_mine = lax.dynamic_index_in_dim(ag8(p_all, name="ag_mod")[0::2], me, axis=1, keepdims=False)
    mod = jnp.transpose(p_mine[:, : depth * MW].reshape(N_CHIPS, depth, MW), (1, 0, 2)).reshape(depth, N_MOD * D)
    kvmod = p_mine[:, depth * MW :].reshape(1, 2 * D)
    mods = [[mod[i : i + 1, j * D : (j + 1) * D] for j in range(N_MOD)] for i in range(depth)]
    kv_shift, kv_scale = kvmod[:, :D], kvmod[:, D:]

    def ffn_fwd(xin, i, j, sh, sc, gt):
        h = norm_mod_fwd(xin, full["ffn_norm_w"][i, j][None], sc, sh, name=f"ffn_norm_{i}_{j}")
        gu = mm(h, gw[i]["gu"], b_lead=(j,), name=f"ffn_gu_{i}_{j}")
        a = swiglu_fwd(gu, name=f"ffn_act_{i}_{j}")
        f = mm(a, gw[i]["dn"], reduce_s=True, b_lead=(j,), name=f"ffn_down_{i}_{j}")
        return gate_fwd(xin, f, gt, FFN_HALF, name=f"ffn_res_{i}_{j}"), (xin, gu, a, f)

    def ssm_fwd(xin, i, sh, sc, gt):
        h = norm_mod_fwd(xin, mix_norm_w[i][None], sc, sh, name=f"mix_norm_{i}")
        zx4 = mm(h, gw[i]["in"], name=f"ssm_in_{i}")
        zx = jnp.transpose(zx4, (1, 0, 2)).reshape(L, N_CHIPS * CI)
        xbc = conv_fwd(zx, full["ssm_conv_w"][i], full["ssm_conv_b"][i][None], DI, name=f"ssm_conv_{i}")
        dt_raw = zx[:, DI + CC :]
        y, states = ssd_fwd(xbc, dt_raw, ssm_dt_bias[i][None], ssm_a_log[i][None], ssm_d[i][None], DI, name=f"ssd_{i}")
        yn = gnorm_fwd(y, zx, full["ssm_norm_w"][i][None], name=f"ssm_gnorm_{i}")
        f = mm(yn, gw[i]["out"], name=f"ssm_out_{i}")
        return gate_fwd(xin, f, gt, 1.0, name=f"mix_res_{i}"), (xin, zx, xbc, dt_raw, y, states, yn, f)

    def att_fwd(xin, i, kv, sh, sc, gt):
        l = i - n_a
        h = norm_mod_fwd(xin, mix_norm_w[i][None], sc, sh, name=f"mix_norm_{i}")
        q = mm(h, gw[i]["q"], bias=attn_b_q[l][None], name=f"att_q_{i}")
        o = attn_fwd(q, kv, attn_sinks[l][None], name=f"att_{i}")
        f = mm(o, gw[i]["o"], bias=attn_b_o[l][None], name=f"att_o_{i}")
        return gate_fwd(xin, f, gt, 1.0, name=f"mix_res_{i}"), (xin, q, o, f)

    saved = []
    xc = xs
    kv = x_kv = None
    landed = gather_end(0, kvmod)
    for i in range(depth):
        sh1, sc1, g1, shm, scm, gm, sh2, sc2, g2 = mods[i]
        tok = gather_begin(i + 1, landed) if i + 1 < depth else 0.0
        sh1 = sh1 + tok
        if i == n_a:
            x_kv = xc
            hkv = norm_mod_fwd(xc, kv_norm_w[None], kv_scale, kv_shift + tok, name="kv_norm")
            kv = mm(hkv, gw[i]["kv"], bias=b_kv[None], name="kv_proj")
        xc, s1 = ffn_fwd(xc, i, 0, sh1, sc1, g1)
        xc, sm = ssm_fwd(xc, i, shm, scm, gm) if i < n_a else att_fwd(xc, i, kv, shm, scm, gm)
        xc, s2 = ffn_fwd(xc, i, 1, sh2, sc2, g2)
        if i + 1 < depth:
            landed = gather_end(i + 1, xc)
        saved.append((s1, sm, s2))

    loss_part, dx, d_final = final_loss(xc, final_norm_w[None], target, name="loss_head")
    loss = lax.psum(loss_part[0, 0], ("x", "y", "c"))

    wg = {}
    sg = {
        "ffn_norm_w": [[None, None] for _ in range(depth)], "mix_norm_w": [None] * depth, "mod": [None] * depth,
        "ssm_conv_w": [None] * n_a, "ssm_conv_b": [None] * n_a, "ssm_dt_bias": [None] * n_a, "ssm_a_log": [None] * n_a,
        "ssm_d": [None] * n_a, "ssm_norm_w": [None] * n_a, "attn_b_q": [None] * n_b, "attn_sinks": [None] * n_b,
        "attn_b_o": [None] * n_b,
    }

    def ffn_bwd(dxo, i, j, sv, sh, sc, gt):
        xin, gu, a, f = sv
        df, dgt, _ = gate_bwd(f, dxo, gt, FFN_HALF, name=f"ffn_res_bwd_{i}_{j}")
        da = mm(df, gw[i]["dn"], mode="nt", b_lead=(j,), name=f"ffn_down_dx_{i}_{j}")
        wg["ffn_w_down"] = mm(a, df, mode="tn", out_buf=wg["ffn_w_down"], out_lead=(j,), name=f"ffn_down_dw_{i}_{j}")
        dgu = swiglu_bwd(gu, da, name=f"ffn_act_bwd_{i}_{j}")
        nw = full["ffn_norm_w"][i, j][None]
        h = norm_mod_fwd(xin, nw, sc, sh, name=f"ffn_norm_re_{i}_{j}")
        dh = mm(dgu, gw[i]["gu"], mode="nt", reduce_s=True, b_lead=(j,), name=f"ffn_gu_dx_{i}_{j}")
        wg["ffn_w_gu"] = mm(h, dgu, mode="tn", out_buf=wg["ffn_w_gu"], out_lead=(j,), name=f"ffn_gu_dw_{i}_{j}")
        dxi, dnw, dsc, dsh = norm_mod_bwd(xin, nw, sc, dh, dxo, name=f"ffn_norm_bwd_{i}_{j}")
        sg["ffn_norm_w"][i][j] = dnw
        return dxi, (dsh, dsc, dgt)

    def ssm_bwd(dxo, i, sv, sh, sc, gt):
        xin, zx, xbc, dt_raw, y, states, yn, f = sv
        df, dgt, _ = gate_bwd(f, dxo, gt, 1.0, name=f"mix_res_bwd_{i}")
        dyn = mm(df, gw[i]["out"], mode="nt", name=f"ssm_out_dx_{i}")
        wg["ssm_w_out"] = mm(yn, df, mode="tn", out_dtype=BF16, name=f"ssm_out_dw_{i}")
        dy, dz, dnorm = gnorm_bwd(y, zx, full["ssm_norm_w"][i][None], dyn, name=f"ssm_gnorm_bwd_{i}")
        dxbc, ddt, dbias, dalog, ddsk = ssd_bwd(
            xbc, dt_raw, ssm_dt_bias[i][None], ssm_a_log[i][None], ssm_d[i][None], states, dy, DI, name=f"ssd_bwd_{i}"
        )
        du, dcw, dcb = conv_bwd(zx, full["ssm_conv_w"][i], full["ssm_conv_b"][i][None], dxbc, DI, name=f"ssm_conv_bwd_{i}")
        dzx = jnp.concatenate([dz, du, ddt], axis=1).astype(BF16)
        dzx4 = jnp.transpose(dzx.reshape(L, N_CHIPS, CI), (1, 0, 2))
        nw = mix_norm_w[i][None]
        h = norm_mod_fwd(xin, nw, sc, sh, name=f"mix_norm_re_{i}")
        dh = mm(dzx4, gw[i]["in"], mode="nt", reduce_s=True, name=f"ssm_in_dx_{i}")
        wg["ssm_w_in"] = mm(h, dzx4, mode="tn", out_dtype=BF16, name=f"ssm_in_dw_{i}")
        dxi, dnw, dsc, dsh = norm_mod_bwd(xin, nw, sc, dh, dxo, name=f"mix_norm_bwd_{i}")
        sg["mix_norm_w"][i] = dnw
        sg["ssm_conv_w"][i], sg["ssm_conv_b"][i], sg["ssm_norm_w"][i] = dcw, dcb, dnorm
        sg["ssm_dt_bias"][i], sg["ssm_a_log"][i], sg["ssm_d"][i] = dbias, dalog, ddsk
        return dxi, (dsh, dsc, dgt)

    def att_bwd(dxo, i, sv, dkv, sh, sc, gt):
        l = i - n_a
        xin, q, o, f = sv
        df, dgt, dfsum = gate_bwd(f, dxo, gt, 1.0, name=f"mix_res_bwd_{i}")
        do = mm(df, gw[i]["o"], mode="nt", name=f"att_o_dx_{i}")
        wg["attn_w_o"] = mm(o, df, mode="tn", out_dtype=BF16, name=f"att_o_dw_{i}")
        dq, dkv, dsink = attn_bwd(q, kv, attn_sinks[l][None], do, dkv, name=f"att_bwd_{i}")
        nw = mix_norm_w[i][None]
        h = norm_mod_fwd(xin, nw, sc, sh, name=f"mix_norm_re_{i}")
        dh = mm(dq, gw[i]["q"], mode="nt", name=f"att_q_dx_{i}")
        wg["attn_w_q"] = mm(h, dq, mode="tn", out_dtype=BF16, name=f"att_q_dw_{i}")
        dxi, dnw, dsc, dsh = norm_mod_bwd(xin, nw, sc, dh, dxo, name=f"mix_norm_bwd_{i}")
        sg["mix_norm_w"][i] = dnw
        sg["attn_b_q"][l], sg["attn_sinks"][l], sg["attn_b_o"][l] = colsum(dq, name=f"att_bq_{i}"), dsink, dfsum
        return dxi, dkv, (dsh, dsc, dgt)

    gfull = {n: lax.empty(_row_halves(W[n]).shape, F32) for n in GATHERED}

    def reduce_begin(i, after):
        items = stage_arrays(i)
        parts = []
        for n, _, a in items:
            m, _, rh, cc = _row_halves(a).shape
            parts.append(wg[n].reshape(m, N_CHIPS, 2, rh, cc))
        from_sib = rs_sibling(parts, name=f"rs_sibling_{i}")
        pair = [rs_add_pair(g, r, name=f"rs_pair_{n}_{i}") for (n, _, _), g, r in zip(items, parts, from_sib)]
        land_shapes = [(3, p.shape[0]) + p.shape[2:] for p in pair]
        ssem, rsem, srcs, lands, token = split_start(_reduce_copies, pair, land_shapes, after, name=f"rs_chips_start_{i}")
        return (i, items, parts, from_sib, ssem, rsem, srcs, lands), token[0:1, 0:1]

    def reduce_end(p, after):
        i, items, parts, from_sib, ssem, rsem, srcs, lands = p
        from_chips = split_wait(_reduce_copies, ssem, rsem, srcs, lands, after, name=f"rs_chips_wait_{i}")
        for (n, m0, _), g, r, t in zip(items, parts, from_sib, from_chips):
            gfull[n] = rs_add_final(g, r, t, gfull[n], m0, name=f"rs_final_{n}_{i}")

    dkv = jnp.zeros((L, KVD), F32)
    d_kvnorm = d_kvmod = d_bkv = None
    pending, tok = None, 0.0
    for i in reversed(range(depth)):
        sh1, sc1, g1, shm, scm, gm, sh2, sc2, g2 = mods[i]
        s1, sm, s2 = saved[i]
        wg["ffn_w_gu"] = lax.empty((2, N_CHIPS, D, T), BF16)
        wg["ffn_w_down"] = lax.empty((2, 2, T, D), BF16)
        dx, dm2 = ffn_bwd(dx, i, 1, s2, sh2, sc2, g2 + tok)
        if i < n_a:
            dx, dmm = ssm_bwd(dx, i, sm, shm, scm, gm)
        else:
            dx, dkv, dmm = att_bwd(dx, i, sm, dkv, shm, scm, gm)
        dx, dm1 = ffn_bwd(dx, i, 0, s1, sh1, sc1, g1)
        sg["mod"][i] = jnp.concatenate(list(dm1) + list(dmm) + list(dm2), axis=1)
        if i == n_a:
            d_bkv = colsum(dkv, name="kv_bias_bwd")
            hkv = norm_mod_fwd(x_kv, kv_norm_w[None], kv_scale, kv_shift, name="kv_norm_re")
            dh = mm(dkv, gw[i]["kv"], mode="nt", name="kv_proj_dx")
            wg["w_kv"] = mm(hkv, dkv, mode="tn", out_dtype=BF16, name="kv_proj_dw")
            dx, d_kvnorm, dsc, dsh = norm_mod_bwd(x_kv, kv_norm_w[None], kv_scale, dh, dx, name="kv_norm_bwd")
            d_kvmod = jnp.concatenate([dsh, dsc], axis=1)
        if pending is not None:
            reduce_end(pending, dx)
        pending, tok = reduce_begin(i, dx)
    grad_x = dx[None]
    grads = {}

    small = {
        "ffn_norm_w": jnp.stack([jnp.stack([r[0] for r in row]) for row in sg["ffn_norm_w"]]),
        "mod_b": jnp.stack([r[0] for r in sg["mod"]]),
        "mix_norm_w": jnp.stack([r[0] for r in sg["mix_norm_w"]]),
        "ssm_conv_w": jnp.stack(sg["ssm_conv_w"]),
        "ssm_conv_b": jnp.stack([r[0] for r in sg["ssm_conv_b"]]),
        "ssm_dt_bias": jnp.stack([r[0] for r in sg["ssm_dt_bias"]]),
        "ssm_a_log": jnp.stack([r[0] for r in sg["ssm_a_log"]]),
        "ssm_d": jnp.stack([r[0] for r in sg["ssm_d"]]),
        "ssm_norm_w": jnp.stack([r[0] for r in sg["ssm_norm_w"]]),
        "kv_norm_w": d_kvnorm[0],
        "kv_mod_b": d_kvmod[0],
        "b_kv": d_bkv[0],
        "attn_b_q": jnp.stack([r[0] for r in sg["attn_b_q"]]),
        "attn_sinks": jnp.stack([r[0] for r in sg["attn_sinks"]]),
        "attn_b_o": jnp.stack([r[0] for r in sg["attn_b_o"]]),
        "final_norm_w": d_final[0],
    }
    small_like = [small[n].shape for n in SMALL]
    sv_pack, sv_n = _pack([small[n] for n in SMALL])
    sv_all = ag8(sv_pack + tok, name="ag_small_g")
    sv_sum = reduce8(sv_all, name="small_g_sum").reshape(-1)[:sv_n]
    for n, gsum in zip(SMALL, _unpack(sv_sum, small_like)):
        grads[n] = chip_cols(gsum, W[n].shape[-1]) if n in SMALL_SHARDED else gsum

    per_dev = [_unpack(sv_all[b].reshape(-1)[:sv_n], small_like) for b in range(N_DEV)]
    i_modb, i_kvb = SMALL.index("mod_b"), SMALL.index("kv_mod_b")
    dmod_all = jnp.stack([chip_cols(p[i_modb], MW) for p in per_dev], axis=1)
    dkv_all = jnp.stack([chip_cols(p[i_kvb], KW) for p in per_dev], axis=0)[None]
    c_t = jnp.transpose(c_all)
    grads["mod_w"] = outer8(c_t, dmod_all, name="mod_w_grad")
    grads["kv_mod_w"] = outer8(c_t, dkv_all, name="kv_mod_w_grad")[0]

    delta, new_m, new_v = {}, {}, {}
    for n in COLUMN_PARALLEL:
        delta[n], new_m[n], new_v[n] = adamw(W[n], grads[n], MOM[n], VAR[n], name=f"adamw_{n}")
    like = [W[n].shape for n in SMALL]
    packs = [_pack([d[n] for n in SMALL])[0] for d in (W, grads, MOM, VAR)]
    n_small = sum(int(W[n].size) for n in SMALL)
    for dst, res in zip((delta, new_m, new_v), adamw(*packs, name="adamw_small")):
        for n, a in zip(SMALL, _unpack(res.reshape(-1)[:n_small], like)):
            dst[n] = a

    reduce_end(pending, delta["mod_w"])
    for n, s in zip(GATHERED, rs_share([gfull[n] for n in GATHERED], name="rs_share")):
        grads[n] = s.reshape(W[n].shape)
        delta[n], new_m[n], new_v[n] = adamw(W[n], grads[n], MOM[n], VAR[n], name=f"adamw_{n}")

    return (loss, grad_x, *[grads[n] for n in WEIGHTS], *[delta[n] for n in WEIGHTS], *[new_m[n] for n in WEIGHTS],
            *[new_v[n] for n in WEIGHTS])
```

```python
import functools

import jax
import jax.numpy as jnp
from jax import lax
from jax.experimental import pallas as pl
from jax.experimental.pallas import tpu as pltpu

F32 = jnp.float32
BF16 = jnp.bfloat16
HIGHEST = lax.Precision.HIGHEST
MESH = pl.DeviceIdType.MESH

EPS = 1e-5
N_MOD = 9
FFN_HALF = 0.5
SSM_HEADDIM = 64
SSM_GROUPS = 8
SSM_STATE = 128
CONV_WIDTH = 4
CHUNK = 128
KV_HEADS = 4
HEAD_DIM = 64
WINDOW = 128
N_CHIPS = 4
N_DEV = 8

ADAM_LR = 0.001
ADAM_B1 = 0.9
ADAM_B2 = 0.999
ADAM_EPS = 1e-08
ADAM_WD = 0.01
ADAM_STEP = 10

LANE = 128
MM_TILE = 1024


def _pick(n, pref, align, whole_if_small=False):
    best = 0
    t = align
    while t <= min(n, pref):
        if n % t == 0:
            best = t
        t += align
    if best == 0 or (whole_if_small and best < 256 and n <= 2048):
        return n
    return best


def _sigmoid(x):
    return 1.0 / (1.0 + jnp.exp(-x))


def _silu(x):
    return x * _sigmoid(x)


def _dsilu(x):
    s = _sigmoid(x)
    return s * (1.0 + x * (1.0 - s))


def _params(*sem):
    return pltpu.CompilerParams(dimension_semantics=sem)


def mm(a, b, *, mode="nn", reduce_s=False, out_dtype=F32, bias=None, a_lead=(), b_lead=(), out_buf=None,
       out_lead=(), name):
    a_s = a.ndim - len(a_lead) == 3
    b_s = b.ndim - len(b_lead) == 3
    S = a.shape[len(a_lead)] if a_s else (b.shape[len(b_lead)] if b_s else 1)
    a2 = a.shape[-2:]
    b2 = b.shape[-2:]
    if mode == "nn":
        (M, K), (K2, N) = a2, b2
    elif mode == "nt":
        (M, K), (N, K2) = a2, b2
    else:
        (K, M), (K2, N) = a2, b2
    assert K == K2, (a.shape, b.shape, mode)
    batch = (a_s or b_s) and not reduce_s
    sb = S if batch else 1
    sr = S if ((a_s or b_s) and reduce_s) else 1
    tm = _pick(M, MM_TILE, LANE if mode == "tn" else 16, True)
    tn = _pick(N, MM_TILE, LANE, True)
    tk = _pick(K, MM_TILE, LANE if mode != "tn" else 16, True)
    nk = K // tk
    grid = (sb, M // tm, N // tn, sr, nk)

    def s_of(isb, isr):
        return isb if batch else isr

    def a_map(isb, i, j, isr, k):
        idx = (k, i) if mode == "tn" else (i, k)
        return tuple(a_lead) + (((s_of(isb, isr),) + idx) if a_s else idx)

    def b_map(isb, i, j, isr, k):
        idx = (j, k) if mode == "nt" else (k, j)
        return tuple(b_lead) + (((s_of(isb, isr),) + idx) if b_s else idx)

    def o_map(isb, i, j, isr, k):
        return tuple(out_lead) + ((isb, i, j) if batch else (i, j))

    def lead_blk(lead, has_s, blk):
        return (None,) * (len(lead) + (1 if has_s else 0)) + blk

    a_blk = (tk, tm) if mode == "tn" else (tm, tk)
    b_blk = (tn, tk) if mode == "nt" else (tk, tn)
    in_specs = [pl.BlockSpec(lead_blk(a_lead, a_s, a_blk), a_map), pl.BlockSpec(lead_blk(b_lead, b_s, b_blk), b_map)]
    args = [a, b]
    if bias is not None:
        bias_s = bias.ndim == 3
        in_specs.append(
            pl.BlockSpec(
                ((None, 1, tn) if bias_s else (1, tn)),
                (lambda isb, i, j, isr, k: (isb, 0, j)) if bias_s else (lambda isb, i, j, isr, k: (0, j)),
            )
        )
        args.append(bias)
    aliases = {}
    if out_buf is not None:
        in_specs.append(pl.BlockSpec(memory_space=pl.ANY))
        aliases = {len(args): 0}
        args.append(out_buf)
        out_shape = jax.ShapeDtypeStruct(out_buf.shape, out_buf.dtype)
        want = tuple(out_buf.shape[len(out_lead):])
        assert want == ((sb, M, N) if batch else (M, N)), (want, sb, M, N)
    else:
        out_shape = jax.ShapeDtypeStruct(((sb, M, N) if batch else (M, N)), out_dtype)
    dims = {"nn": (((1,), (0,)), ((), ())), "nt": (((1,), (1,)), ((), ())), "tn": (((0,), (0,)), ((), ()))}[mode]
    n_in = len(args)
    one_step = sr * nk == 1

    def body(*refs):
        a_ref, b_ref = refs[0], refs[1]
        bias_ref = refs[2] if bias is not None else None
        o_ref = refs[n_in]

        def finish(r):
            if bias is not None:
                r = r + bias_ref[...]
            o_ref[...] = r.astype(o_ref.dtype)

        part = lax.dot_general(a_ref[...].astype(BF16), b_ref[...].astype(BF16), dims, preferred_element_type=F32)
        if one_step:
            finish(part)
            return
        acc = refs[n_in + 1]
        isr = pl.program_id(3)
        k = pl.program_id(4)

        @pl.when((isr == 0) & (k == 0))
        def _():
            acc[...] = part

        @pl.when((isr > 0) | (k > 0))
        def _():
            acc[...] += part

        @pl.when((isr == sr - 1) & (k == nk - 1))
        def _():
            finish(acc[...])

    return pl.pallas_call(
        body,
        out_shape=out_shape,
        grid=grid,
        in_specs=in_specs,
        out_specs=pl.BlockSpec(lead_blk(out_lead, batch, (tm, tn)), o_map),
        scratch_shapes=[] if one_step else [pltpu.VMEM((tm, tn), F32)],
        input_output_aliases=aliases,
        compiler_params=_params("parallel", "parallel", "parallel", "arbitrary", "arbitrary"),
        name=name,
    )(*args)


def norm_mod_fwd(x, nw, sc, sh, *, name):
    L, D = x.shape
    tl = _pick(L, 512, 16)

    def body(x_ref, nw_ref, sc_ref, sh_ref, h_ref):
        xv = x_ref[...]
        r = lax.rsqrt(jnp.mean(xv * xv, axis=-1, keepdims=True) + EPS)
        n = (xv * r) * nw_ref[...]
        h_ref[...] = (n * (1.0 + sc_ref[...]) + sh_ref[...]).astype(h_ref.dtype)

    row = pl.BlockSpec((1, D), lambda i: (0, 0))
    return pl.pallas_call(
        body,
        out_shape=jax.ShapeDtypeStruct((L, D), BF16),
        grid=(L // tl,),
        in_specs=[pl.BlockSpec((tl, D), lambda i: (i, 0)), row, row, row],
        out_specs=pl.BlockSpec((tl, D), lambda i: (i, 0)),
        compiler_params=_params("parallel"),
        name=name,
    )(x, nw, sc, sh)


def norm_mod_bwd(x, nw, sc, dh, dx_in, *, name):
    L, D = x.shape
    tl = _pick(L, 512, 16)

    def body(x_ref, nw_ref, sc_ref, dh_ref, dxi_ref, dx_ref, dnw_ref, dsc_ref, dsh_ref):
        @pl.when(pl.program_id(0) == 0)
        def _():
            dnw_ref[...] = jnp.zeros_like(dnw_ref)
            dsc_ref[...] = jnp.zeros_like(dsc_ref)
            dsh_ref[...] = jnp.zeros_like(dsh_ref)

        xv = x_ref[...]
        dh_v = dh_ref[...]
        r = lax.rsqrt(jnp.mean(xv * xv, axis=-1, keepdims=True) + EPS)
        xhat = xv * r
        nw_v = nw_ref[...]
        n = xhat * nw_v
        dsh_ref[...] += jnp.sum(dh_v, axis=0, keepdims=True)
        dsc_ref[...] += jnp.sum(dh_v * n, axis=0, keepdims=True)
        dn = dh_v * (1.0 + sc_ref[...])
        dnw_ref[...] += jnp.sum(dn * xhat, axis=0, keepdims=True)
        dxhat = dn * nw_v
        dx_ref[...] = dxi_ref[...] + r * (dxhat - xhat * jnp.mean(dxhat * xhat, axis=-1, keepdims=True))

    row = pl.BlockSpec((1, D), lambda i: (0, 0))
    tile = pl.BlockSpec((tl, D), lambda i: (i, 0))
    vec = jax.ShapeDtypeStruct((1, D), F32)
    return pl.pallas_call(
        body,
        out_shape=(jax.ShapeDtypeStruct((L, D), F32), vec, vec, vec),
        grid=(L // tl,),
        in_specs=[tile, row, row, tile, tile],
        out_specs=(tile, row, row, row),
        compiler_params=_params("arbitrary"),
        name=name,
    )(x, nw, sc, dh, dx_in)


def gate_fwd(x, f, gate, scale, *, name):
    L, D = x.shape
    tl = _pick(L, 512, 8)

    def body(x_ref, f_ref, g_ref, o_ref):
        o_ref[...] = x_ref[...] + (scale * g_ref[...]) * f_ref[...]

    tile = pl.BlockSpec((tl, D), lambda i: (i, 0))
    return pl.pallas_call(
        body,
        out_shape=jax.ShapeDtypeStruct((L, D), F32),
        grid=(L // tl,),
        in_specs=[tile, tile, pl.BlockSpec((1, D), lambda i: (0, 0))],
        out_specs=tile,
        compiler_params=_params("parallel"),
        name=name,
    )(x, f, gate)


def gate_bwd(f, dx, gate, scale, *, name):
    L, D = f.shape
    tl = _pick(L, 512, 16)

    def body(f_ref, dx_ref, g_ref, df_ref, dg_ref, dfsum_ref):
        @pl.when(pl.program_id(0) == 0)
        def _():
            dg_ref[...] = jnp.zeros_like(dg_ref)
            dfsum_ref[...] = jnp.zeros_like(dfsum_ref)

        dxv = dx_ref[...]
        df = (scale * g_ref[...]) * dxv
        df_ref[...] = df.astype(df_ref.dtype)
        dfsum_ref[...] += jnp.sum(df, axis=0, keepdims=True)
        dg_ref[...] += scale * jnp.sum(f_ref[...] * dxv, axis=0, keepdims=True)

    tile = pl.BlockSpec((tl, D), lambda i: (i, 0))
    row = pl.BlockSpec((1, D), lambda i: (0, 0))
    vec = jax.ShapeDtypeStruct((1, D), F32)
    return pl.pallas_call(
        body,
        out_shape=(jax.ShapeDtypeStruct((L, D), BF16), vec, vec),
        grid=(L // tl,),
        in_specs=[tile, tile, row],
        out_specs=(tile, row, row),
        compiler_params=_params("arbitrary"),
        name=name,
    )(f, dx, gate)


def colsum(x, *, name):
    L, N = x.shape
    tl = _pick(L, 512, 8)

    def body(x_ref, o_ref):
        @pl.when(pl.program_id(0) == 0)
        def _():
            o_ref[...] = jnp.zeros_like(o_ref)

        o_ref[...] += jnp.sum(x_ref[...], axis=0, keepdims=True)

    return pl.pallas_call(
        body,
        out_shape=jax.ShapeDtypeStruct((1, N), F32),
        grid=(L // tl,),
        in_specs=[pl.BlockSpec((tl, N), lambda i: (i, 0))],
        out_specs=pl.BlockSpec((1, N), lambda i: (0, 0)),
        compiler_params=_params("arbitrary"),
        name=name,
    )(x)


def swiglu_fwd(gu, *, name):
    _, L, T = gu.shape
    tl = _pick(L, 256, 16)

    def body(g_ref, u_ref, a_ref):
        a_ref[...] = (_silu(g_ref[...]) * u_ref[...]).astype(a_ref.dtype)

    return pl.pallas_call(
        body,
        out_shape=jax.ShapeDtypeStruct((2, L, T), BF16),
        grid=(2, L // tl),
        in_specs=[
            pl.BlockSpec((None, tl, T), lambda j, i: (j, i, 0)),
            pl.BlockSpec((None, tl, T), lambda j, i: (j + 2, i, 0)),
        ],
        out_specs=pl.BlockSpec((None, tl, T), lambda j, i: (j, i, 0)),
        compiler_params=_params("parallel", "parallel"),
        name=name,
    )(gu, gu)


def swiglu_bwd(gu, da, *, name):
    _, L, T = gu.shape
    tl = _pick(L, 256, 16)

    def body(g_ref, u_ref, da_ref, d_ref):
        g = g_ref[...]
        dav = da_ref[...]
        d_ref[0] = (dav * u_ref[...] * _dsilu(g)).astype(d_ref.dtype)
        d_ref[1] = (dav * _silu(g)).astype(d_ref.dtype)

    out = pl.pallas_call(
        body,
        out_shape=jax.ShapeDtypeStruct((2, 2, L, T), BF16),
        grid=(2, L // tl),
        in_specs=[
            pl.BlockSpec((None, tl, T), lambda j, i: (j, i, 0)),
            pl.BlockSpec((None, tl, T), lambda j, i: (j + 2, i, 0)),
            pl.BlockSpec((None, tl, T), lambda j, i: (j, i, 0)),
        ],
        out_specs=pl.BlockSpec((2, None, tl, T), lambda j, i: (0, j, i, 0)),
        compiler_params=_params("parallel", "parallel"),
        name=name,
    )(gu, gu, da)
    return out.reshape(4, L, T)


def _shift_down(u, k, rows):
    if k == 0:
        return u
    return jnp.where(rows >= k, pltpu.roll(u, k, 0), 0.0)


def _shift_up(u, k, rows, n):
    if k == 0:
        return u
    return jnp.where(rows < n - k, pltpu.roll(u, n - k, 0), 0.0)


def _conv_pre(u, w_ref, b_ref, rows):
    pre = b_ref[...] + w_ref[CONV_WIDTH - 1 : CONV_WIDTH, :] * u
    for k in range(1, CONV_WIDTH):
        pre = pre + w_ref[CONV_WIDTH - 1 - k : CONV_WIDTH - k, :] * _shift_down(u, k, rows)
    return pre


def conv_fwd(zx, conv_w, conv_b, d_inner, *, name):
    L = zx.shape[0]
    C = conv_w.shape[1]
    tc = 256
    off = d_inner // tc

    def body(u_ref, w_ref, b_ref, o_ref):
        rows = lax.broadcasted_iota(jnp.int32, (L, tc), 0)
        o_ref[...] = _silu(_conv_pre(u_ref[...], w_ref, b_ref, rows))

    return pl.pallas_call(
        body,
        out_shape=jax.ShapeDtypeStruct((L, C), F32),
        grid=(C // tc,),
        in_specs=[
            pl.BlockSpec((L, tc), lambda j: (0, off + j)),
            pl.BlockSpec((CONV_WIDTH, tc), lambda j: (0, j)),
            pl.BlockSpec((1, tc), lambda j: (0, j)),
        ],
        out_specs=pl.BlockSpec((L, tc), lambda j: (0, j)),
        compiler_params=_params("parallel"),
        name=name,
    )(zx, conv_w, conv_b)


def conv_bwd(zx, conv_w, conv_b, dxbc, d_inner, *, name):
    L = zx.shape[0]
    C = conv_w.shape[1]
    tc = 256
    off = d_inner // tc

    def body(u_ref, w_ref, b_ref, d_ref, du_ref, dw_ref, db_ref):
        rows = lax.broadcasted_iota(jnp.int32, (L, tc), 0)
        u = u_ref[...]
        dpre = d_ref[...] * _dsilu(_conv_pre(u, w_ref, b_ref, rows))
        db_ref[...] = jnp.sum(dpre, axis=0, keepdims=True)
        du = w_ref[CONV_WIDTH - 1 : CONV_WIDTH, :] * dpre
        dw_ref[CONV_WIDTH - 1 : CONV_WIDTH, :] = jnp.sum(dpre * u, axis=0, keepdims=True)
        for k in range(1, CONV_WIDTH):
            j = CONV_WIDTH - 1 - k
            dw_ref[j : j + 1, :] = jnp.sum(dpre * _shift_down(u, k, rows), axis=0, keepdims=True)
            du = du + w_ref[j : j + 1, :] * _shift_up(dpre, k, rows, L)
        du_ref[...] = du

    return pl.pallas_call(
        body,
        out_shape=(
            jax.ShapeDtypeStruct((L, C), F32),
            jax.ShapeDtypeStruct((CONV_WIDTH, C), F32),
            jax.ShapeDtypeStruct((1, C), F32),
        ),
        grid=(C // tc,),
        in_specs=[
            pl.BlockSpec((L, tc), lambda j: (0, off + j)),
            pl.BlockSpec((CONV_WIDTH, tc), lambda j: (0, j)),
            pl.BlockSpec((1, tc), lambda j: (0, j)),
            pl.BlockSpec((L, tc), lambda j: (0, j)),
        ],
        out_specs=(
            pl.BlockSpec((L, tc), lambda j: (0, j)),
            pl.BlockSpec((CONV_WIDTH, tc), lambda j: (0, j)),
            pl.BlockSpec((1, tc), lambda j: (0, j)),
        ),
        compiler_params=_params("parallel"),
        name=name,
    )(zx, conv_w, conv_b, dxbc)


def _ssd_head(xs, dt, acs, tot, dsk, cb, bm, cm, prev):
    q = xs.shape[0]
    li = lax.broadcasted_iota(jnp.int32, (q, q), 0)
    si = lax.broadcasted_iota(jnp.int32, (q, q), 1)
    causal = li >= si
    lmat = jnp.exp(jnp.where(causal, acs - acs.T, -jnp.inf))
    xdt = xs * dt
    y = jnp.dot((cb * lmat).astype(BF16), xdt.astype(BF16), preferred_element_type=F32)
    y = y + lax.dot_general(
        (cm * jnp.exp(acs)).astype(BF16), prev.astype(BF16), (((1,), (1,)), ((), ())), preferred_element_type=F32
    )
    y = y + dsk * xs
    st = lax.dot_general(
        xdt.astype(BF16), (bm * jnp.exp(tot - acs)).astype(BF16), (((0,), (0,)), ((), ())), preferred_element_type=F32
    )
    return y, prev * jnp.exp(tot) + st


def _pick_lane(v, h):
    lanes = lax.broadcasted_iota(jnp.int32, v.shape, 1)
    return jnp.sum(jnp.where(lanes == h, v, 0.0), axis=1, keepdims=True)


def _tri_cols(cols, upper):
    q = cols[0].shape[0]
    assert 3 * len(cols) <= LANE
    li = lax.broadcasted_iota(jnp.int32, (q, q), 0)
    si = lax.broadcasted_iota(jnp.int32, (q, q), 1)
    tri = ((li <= si) if upper else (li >= si)).astype(BF16)
    lanes = lax.broadcasted_iota(jnp.int32, (q, LANE), 1)
    rhs = jnp.zeros((q, LANE), F32)
    for r, col in enumerate(cols):
        hi = col.astype(BF16).astype(F32)
        mid = (col - hi).astype(BF16).astype(F32)
        lo = col - hi - mid
        for t, term in enumerate((hi, mid, lo)):
            rhs = jnp.where(lanes == 3 * r + t, term, rhs)
    out = jnp.dot(tri, rhs.astype(BF16), preferred_element_type=F32)
    return [jnp.sum(jnp.where((lanes >= 3 * r) & (lanes < 3 * r + 3), out, 0.0), axis=1, keepdims=True)
            for r in range(len(cols))]


def _softplus(x):
    return jnp.maximum(x, 0.0) + jnp.log(1.0 + jnp.exp(-jnp.abs(x)))


def _ssd_specs(L, d_inner, H, nc, rev):
    R = H // SSM_GROUPS
    P, N, Q = SSM_HEADDIM, SSM_STATE, CHUNK
    ngrp = SSM_GROUPS

    def ci(c):
        return (nc - 1 - c) if rev else c

    b_off = d_inner // N
    c_off = b_off + ngrp
    xs = pl.BlockSpec((Q, R * P), lambda c, g: (ci(c), g))
    bm = pl.BlockSpec((Q, N), lambda c, g: (ci(c), b_off + g))
    cm = pl.BlockSpec((Q, N), lambda c, g: (ci(c), c_off + g))
    dt = pl.BlockSpec((Q, H), lambda c, g: (ci(c), 0))
    hv = pl.BlockSpec((1, H), lambda c, g: (0, 0))
    y = pl.BlockSpec((Q, R * P), lambda c, g: (ci(c), g))
    st = pl.BlockSpec((None, R * P, N), lambda c, g: (ci(c), g, 0))
    return R, xs, bm, cm, dt, hv, y, st


def ssd_fwd(xbc, dt_raw, dt_bias, a_log, d_skip, d_inner, *, name):
    L = xbc.shape[0]
    H = dt_raw.shape[1]
    nc = L // CHUNK
    P, N = SSM_HEADDIM, SSM_STATE
    R, xs_s, bm_s, cm_s, dt_s, hv_s, y_s, st_s = _ssd_specs(L, d_inner, H, nc, False)

    def body(xs_ref, bm_ref, cm_ref, dt_ref, bias_ref, alog_ref, dsk_ref, y_ref, st_ref, state):
        c = pl.program_id(0)
        g = pl.program_id(1)

        @pl.when(c == 0)
        def _():
            for r in range(R):
                state[g * R + r] = jnp.zeros((P, N), F32)

        dtb = _softplus(dt_ref[...] + bias_ref[...])
        a_all = -jnp.exp(alog_ref[...])
        bm, cm = bm_ref[...], cm_ref[...]
        cb = lax.dot_general(cm.astype(BF16), bm.astype(BF16), (((1,), (1,)), ((), ())), preferred_element_type=F32)
        dts = [_pick_lane(dtb, g * R + r) for r in range(R)]
        a_cols = [dts[r] * _pick_lane(a_all, g * R + r) for r in range(R)]
        acs = _tri_cols(a_cols, upper=False)
        prevs = [state[g * R + r] for r in range(R)]
        res = []
        for r in range(R):
            res.append(_ssd_head(
                xs_ref[:, r * P : (r + 1) * P],
                dts[r],
                jnp.broadcast_to(acs[r], (CHUNK, CHUNK)),
                jnp.sum(a_cols[r], axis=0, keepdims=True),
                _pick_lane(dsk_ref[...], g * R + r),
                cb,
                bm,
                cm,
                prevs[r],
            ))
        for r in range(R):
            st_ref[r * P : (r + 1) * P, :] = prevs[r]
            y_ref[:, r * P : (r + 1) * P] = res[r][0]
            state[g * R + r] = res[r][1]

    return pl.pallas_call(
        body,
        out_shape=(jax.ShapeDtypeStruct((L, d_inner), F32), jax.ShapeDtypeStruct((nc, H * P, N), F32)),
        grid=(nc, SSM_GROUPS),
        in_specs=[xs_s, bm_s, cm_s, dt_s, hv_s, hv_s, hv_s],
        out_specs=(y_s, st_s),
        scratch_shapes=[pltpu.VMEM((H, P, N), F32)],
        compiler_params=_params("arbitrary", "arbitrary"),
        name=name,
    )(xbc, xbc, xbc, dt_raw, dt_bias, a_log, d_skip)


def ssd_bwd(xbc, dt_raw, dt_bias, a_log, d_skip, states, dy, d_inner, *, name):
    L, C = xbc.shape
    H = dt_raw.shape[1]
    nc = L // CHUNK
    P, N, Q = SSM_HEADDIM, SSM_STATE, CHUNK
    R, xs_s, bm_s, cm_s, dt_s, hv_s, y_s, st_s = _ssd_specs(L, d_inner, H, nc, True)

    def body(xs_ref, bm_ref, cm_ref, dt_ref, bias_ref, alog_ref, dsk_ref, st_ref, dy_ref,
             dxs_ref, dbm_ref, dcm_ref, ddt_ref, dbias_ref, dalog_ref, ddsk_ref, dstate):
        c = pl.program_id(0)
        g = pl.program_id(1)

        @pl.when(c == 0)
        def _():
            for r in range(R):
                dstate[g * R + r] = jnp.zeros((P, N), F32)

        @pl.when((c == 0) & (g == 0))
        def _():
            dbias_ref[...] = jnp.zeros_like(dbias_ref)
            dalog_ref[...] = jnp.zeros_like(dalog_ref)
            ddsk_ref[...] = jnp.zeros_like(ddsk_ref)

        @pl.when(g == 0)
        def _():
            ddt_ref[...] = jnp.zeros_like(ddt_ref)

        pre = dt_ref[...] + bias_ref[...]
        dtb = _softplus(pre)
        a_all = -jnp.exp(alog_ref[...])
        lanes_q = lax.broadcasted_iota(jnp.int32, (Q, H), 1)
        lanes_1 = lax.broadcasted_iota(jnp.int32, (1, H), 1)
        bm = bm_ref[...]
        cm = cm_ref[...]
        nt = (((1,), (1,)), ((), ()))
        cb = lax.dot_general(cm.astype(BF16), bm.astype(BF16), nt, preferred_element_type=F32)
        dts = [_pick_lane(dtb, g * R + r) for r in range(R)]
        a_negs = [_pick_lane(a_all, g * R + r) for r in range(R)]
        a_cols = [dts[r] * a_negs[r] for r in range(R)]
        acs = _tri_cols(a_cols, upper=False)
        dbm = jnp.zeros((Q, N), F32)
        dcm = jnp.zeros((Q, N), F32)
        dcb = jnp.zeros((Q, Q), F32)
        dd_row = jnp.zeros((1, H), F32)
        dstates = [dstate[g * R + r] for r in range(R)]
        dprevs, ddts, dacs_cols, dtots = [], [], [], []
        for r in range(R):
            h = g * R + r
            args = (
                xs_ref[:, r * P : (r + 1) * P],
                dts[r],
                jnp.broadcast_to(acs[r], (Q, Q)),
                jnp.sum(a_cols[r], axis=0, keepdims=True),
                _pick_lane(dsk_ref[...], h),
                cb,
                bm,
                cm,
                st_ref[r * P : (r + 1) * P, :],
            )
            _, vjp = jax.vjp(_ssd_head, *args)
            dxs, ddt, dacs, dtot, dd, dcb_h, dbm_h, dcm_h, dprev = vjp((dy_ref[:, r * P : (r + 1) * P], dstates[r]))
            dxs_ref[:, r * P : (r + 1) * P] = dxs
            dprevs.append(dprev)
            ddts.append(ddt)
            dacs_cols.append(jnp.sum(dacs, axis=1, keepdims=True))
            dtots.append(dtot)
            dbm = dbm + dbm_h
            dcm = dcm + dcm_h
            dcb = dcb + dcb_h
            dd_row = dd_row + jnp.where(lanes_1 == h, dd, 0.0)
        for r in range(R):
            dstate[g * R + r] = dprevs[r]
        ddt_blk = jnp.zeros((Q, H), F32)
        da_row = jnp.zeros((1, H), F32)
        for r, da_col in enumerate(_tri_cols(dacs_cols, upper=True)):
            h = g * R + r
            da_col = da_col + dtots[r]
            ddt_blk = ddt_blk + jnp.where(lanes_q == h, ddts[r] + da_col * a_negs[r], 0.0)
            da_row = da_row + jnp.where(lanes_1 == h, jnp.sum(da_col * dts[r], axis=0, keepdims=True), 0.0)
        dcb16 = dcb.astype(BF16)
        dbm_ref[...] = dbm + lax.dot_general(dcb16, cm.astype(BF16), (((0,), (0,)), ((), ())), preferred_element_type=F32)
        dcm_ref[...] = dcm + jnp.dot(dcb16, bm.astype(BF16), preferred_element_type=F32)
        ddt_pre = ddt_blk * _sigmoid(pre)
        ddt_ref[...] += ddt_pre
        dbias_ref[...] += jnp.sum(ddt_pre, axis=0, keepdims=True)
        dalog_ref[...] += da_row * a_all
        ddsk_ref[...] += dd_row

    ngrp = SSM_GROUPS
    hrow = jax.ShapeDtypeStruct((1, H), F32)
    dxs, dbm, dcm, ddt, dbias, dalog, ddsk = pl.pallas_call(
        body,
        out_shape=(
            jax.ShapeDtypeStruct((L, d_inner), F32),
            jax.ShapeDtypeStruct((L, ngrp * N), F32),
            jax.ShapeDtypeStruct((L, ngrp * N), F32),
            jax.ShapeDtypeStruct((L, H), F32),
            hrow,
            hrow,
            hrow,
        ),
        grid=(nc, ngrp),
        in_specs=[xs_s, bm_s, cm_s, dt_s, hv_s, hv_s, hv_s, st_s, y_s],
        out_specs=(
            y_s,
            pl.BlockSpec((Q, N), lambda c, g: (nc - 1 - c, g)),
            pl.BlockSpec((Q, N), lambda c, g: (nc - 1 - c, g)),
            dt_s,
            hv_s,
            hv_s,
            hv_s,
        ),
        scratch_shapes=[pltpu.VMEM((H, P, N), F32)],
        compiler_params=_params("arbitrary", "arbitrary"),
        name=name,
    )(xbc, xbc, xbc, dt_raw, dt_bias, a_log, d_skip, states, dy)
    return jnp.concatenate([dxs, dbm, dcm], axis=1), ddt, dbias, dalog, ddsk


def gnorm_fwd(y, zx, nw, *, name):
    L, DI = y.shape
    gw = DI // SSM_GROUPS
    tl = _pick(L, 512, 16)

    def body(y_ref, z_ref, nw_ref, o_ref):
        yz = y_ref[...] * _silu(z_ref[...])
        r = lax.rsqrt(jnp.mean(yz * yz, axis=-1, keepdims=True) + EPS)
        o_ref[...] = ((yz * r) * nw_ref[...]).astype(o_ref.dtype)

    tile = pl.BlockSpec((tl, gw), lambda i, g: (i, g))
    return pl.pallas_call(
        body,
        out_shape=jax.ShapeDtypeStruct((L, DI), BF16),
        grid=(L // tl, SSM_GROUPS),
        in_specs=[tile, tile, pl.BlockSpec((1, gw), lambda i, g: (0, g))],
        out_specs=tile,
        compiler_params=_params("parallel", "parallel"),
        name=name,
    )(y, zx, nw)


def gnorm_bwd(y, zx, nw, dout, *, name):
    L, DI = y.shape
    gw = DI // SSM_GROUPS
    tl = _pick(L, 512, 16)

    def body(y_ref, z_ref, nw_ref, do_ref, dy_ref, dz_ref, dnw_ref):
        @pl.when(pl.program_id(1) == 0)
        def _():
            dnw_ref[...] = jnp.zeros_like(dnw_ref)

        yv = y_ref[...]
        zv = z_ref[...]
        sz = _silu(zv)
        yz = yv * sz
        r = lax.rsqrt(jnp.mean(yz * yz, axis=-1, keepdims=True) + EPS)
        n = yz * r
        dov = do_ref[...]
        dnw_ref[...] += jnp.sum(dov * n, axis=0, keepdims=True)
        dn = dov * nw_ref[...]
        dyz = r * (dn - n * jnp.mean(dn * n, axis=-1, keepdims=True))
        dy_ref[...] = dyz * sz
        dz_ref[...] = dyz * yv * _dsilu(zv)

    tile = pl.BlockSpec((tl, gw), lambda g, i: (i, g))
    row = pl.BlockSpec((1, gw), lambda g, i: (0, g))
    return pl.pallas_call(
        body,
        out_shape=(
            jax.ShapeDtypeStruct((L, DI), F32),
            jax.ShapeDtypeStruct((L, DI), F32),
            jax.ShapeDtypeStruct((1, DI), F32),
        ),
        grid=(SSM_GROUPS, L // tl),
        in_specs=[tile, tile, row, tile],
        out_specs=(tile, tile, row),
        compiler_params=_params("parallel", "arbitrary"),
        name=name,
    )(y, zx, nw, dout)


def _attn_head(q, kp, kc, vp, vc, sink, has_prev):
    w = q.shape[0]
    nt = (((1,), (1,)), ((), ()))
    qb = q.astype(BF16)
    sc = lax.dot_general(qb, kc.astype(BF16), nt, preferred_element_type=F32) * HEAD_DIM ** -0.5
    sp = lax.dot_general(qb, kp.astype(BF16), nt, preferred_element_type=F32) * HEAD_DIM ** -0.5
    ii = lax.broadcasted_iota(jnp.int32, (w, w), 0)
    jj = lax.broadcasted_iota(jnp.int32, (w, w), 1)
    lc = jnp.where(jj <= ii, sc, -jnp.inf)
    lp = jnp.where((jj > ii) & has_prev, sp, -jnp.inf)
    m = jnp.maximum(jnp.maximum(jnp.max(lc, axis=1, keepdims=True), jnp.max(lp, axis=1, keepdims=True)), sink)
    m = lax.stop_gradient(m)
    pc = jnp.exp(lc - m)
    pp = jnp.exp(lp - m)
    denom = jnp.sum(pc, axis=1, keepdims=True) + jnp.sum(pp, axis=1, keepdims=True) + jnp.exp(sink - m)
    o = jnp.dot((pc / denom).astype(BF16), vc.astype(BF16), preferred_element_type=F32)
    return o + jnp.dot((pp / denom).astype(BF16), vp.astype(BF16), preferred_element_type=F32)


def attn_fwd(q, kv, sinks, *, name):
    L, DQ = q.shape
    heads = DQ // HEAD_DIM
    rep = heads // KV_HEADS
    nb = L // WINDOW
    kw = KV_HEADS * HEAD_DIM
    W, HD = WINDOW, HEAD_DIM

    def body(q_ref, kp_ref, kc_ref, vp_ref, vc_ref, s_ref, o_ref):
        has_prev = pl.program_id(0) > 0
        for h in range(heads):
            kh = h // rep
            ks = slice(kh * HD, (kh + 1) * HD)
            o = _attn_head(
                q_ref[:, h * HD : (h + 1) * HD], kp_ref[:, ks], kc_ref[:, ks], vp_ref[:, ks], vc_ref[:, ks],
                s_ref[:, h : h + 1], has_prev,
            )
            o_ref[:, h * HD : (h + 1) * HD] = o.astype(o_ref.dtype)

    return pl.pallas_call(
        body,
        out_shape=jax.ShapeDtypeStruct((L, DQ), BF16),
        grid=(nb,),
        in_specs=[
            pl.BlockSpec((W, DQ), lambda n: (n, 0)),
            pl.BlockSpec((W, kw), lambda n: (jnp.maximum(n - 1, 0), 0)),
            pl.BlockSpec((W, kw), lambda n: (n, 0)),
            pl.BlockSpec((W, kw), lambda n: (jnp.maximum(n - 1, 0), 1)),
            pl.BlockSpec((W, kw), lambda n: (n, 1)),
            pl.BlockSpec((1, heads), lambda n: (0, 0)),
        ],
        out_specs=pl.BlockSpec((W, DQ), lambda n: (n, 0)),
        compiler_params=_params("parallel"),
        name=name,
    )(q, kv, kv, kv, kv, sinks)


def attn_bwd(q, kv, sinks, do, dkv_in, *, name):
    L, DQ = q.shape
    heads = DQ // HEAD_DIM
    rep = heads // KV_HEADS
    nb = L // WINDOW
    kw = KV_HEADS * HEAD_DIM
    W, HD = WINDOW, HEAD_DIM

    def blk(n):
        return jnp.minimum(n, nb - 1)

    def prev(n):
        return jnp.maximum(blk(n) - 1, 0)

    def outb(n):
        return jnp.maximum(n - 1, 0)

    def body(q_ref, kp_ref, kc_ref, vp_ref, vc_ref, s_ref, do_ref, dki_ref, dvi_ref,
             dq_ref, dk_ref, dv_ref, ds_ref, dk_cur, dv_cur):
        n = pl.program_id(0)
        has_prev = n > 0

        @pl.when(n == 0)
        def _():
            ds_ref[...] = jnp.zeros_like(ds_ref)
            dk_cur[...] = jnp.zeros_like(dk_cur)
            dv_cur[...] = jnp.zeros_like(dv_cur)

        @pl.when(n == nb)
        def _():
            dk_ref[...] = dki_ref[...] + dk_cur[...]
            dv_ref[...] = dvi_ref[...] + dv_cur[...]

        @pl.when(n < nb)
        def _():
            lanes = lax.broadcasted_iota(jnp.int32, (1, heads), 1)
            ds_row = jnp.zeros((1, heads), F32)
            for kh in range(KV_HEADS):
                ks = slice(kh * HD, (kh + 1) * HD)
                kp, kc, vp, vc = kp_ref[:, ks], kc_ref[:, ks], vp_ref[:, ks], vc_ref[:, ks]
                dkp = jnp.zeros((W, HD), F32)
                dkc = jnp.zeros((W, HD), F32)
                dvp = jnp.zeros((W, HD), F32)
                dvc = jnp.zeros((W, HD), F32)
                for rr in range(rep):
                    h = kh * rep + rr
                    hs = slice(h * HD, (h + 1) * HD)
                    _, vjp = jax.vjp(
                        functools.partial(_attn_head, has_prev=has_prev),
                        q_ref[:, hs], kp, kc, vp, vc, s_ref[:, h : h + 1],
                    )
                    dq, a, b, c, d, dsk = vjp(do_ref[:, hs])
                    dq_ref[:, hs] = dq
                    dkp, dkc, dvp, dvc = dkp + a, dkc + b, dvp + c, dvc + d
                    ds_row = ds_row + jnp.where(lanes == h, dsk, 0.0)
                dk_ref[:, ks] = dki_ref[:, ks] + dk_cur[:, ks] + dkp
                dv_ref[:, ks] = dvi_ref[:, ks] + dv_cur[:, ks] + dvp
                dk_cur[:, ks] = dkc
                dv_cur[:, ks] = dvc
            ds_ref[...] += ds_row

    dq, dk, dv, ds = pl.pallas_call(
        body,
        out_shape=(
            jax.ShapeDtypeStruct((L, DQ), F32),
            jax.ShapeDtypeStruct((L, kw), F32),
            jax.ShapeDtypeStruct((L, kw), F32),
            jax.ShapeDtypeStruct((1, heads), F32),
        ),
        grid=(nb + 1,),
        in_specs=[
            pl.BlockSpec((W, DQ), lambda n: (blk(n), 0)),
            pl.BlockSpec((W, kw), lambda n: (prev(n), 0)),
            pl.BlockSpec((W, kw), lambda n: (blk(n), 0)),
            pl.BlockSpec((W, kw), lambda n: (prev(n), 1)),
            pl.BlockSpec((W, kw), lambda n: (blk(n), 1)),
            pl.BlockSpec((1, heads), lambda n: (0, 0)),
            pl.BlockSpec((W, DQ), lambda n: (blk(n), 0)),
            pl.BlockSpec((W, kw), lambda n: (outb(n), 0)),
            pl.BlockSpec((W, kw), lambda n: (outb(n), 1)),
        ],
        out_specs=(
            pl.BlockSpec((W, DQ), lambda n: (blk(n), 0)),
            pl.BlockSpec((W, kw), lambda n: (outb(n), 0)),
            pl.BlockSpec((W, kw), lambda n: (outb(n), 0)),
            pl.BlockSpec((1, heads), lambda n: (0, 0)),
        ),
        scratch_shapes=[pltpu.VMEM((W, kw), F32), pltpu.VMEM((W, kw), F32)],
        compiler_params=_params("arbitrary"),
        name=name,
    )(q, kv, kv, kv, kv, sinks, do, dkv_in, dkv_in)
    return dq, jnp.concatenate([dk, dv], axis=1), ds


def final_loss(x, fw, target, *, name):
    L, D = x.shape
    tl = _pick(L, 512, 8)

    def body(x_ref, fw_ref, t_ref, loss_ref, dx_ref, dfw_ref):
        @pl.when(pl.program_id(0) == 0)
        def _():
            loss_ref[...] = jnp.zeros_like(loss_ref)
            dfw_ref[...] = jnp.zeros_like(dfw_ref)

        xv = x_ref[...]
        fwv = fw_ref[...]
        r = lax.rsqrt(jnp.mean(xv * xv, axis=-1, keepdims=True) + EPS)
        xhat = xv * r
        err = xhat * fwv - t_ref[...]
        loss_ref[...] += 0.5 * jnp.sum(jnp.mean(err * err, axis=-1, keepdims=True), axis=0, keepdims=True)
        dy = err * (1.0 / D)
        dfw_ref[...] += jnp.sum(dy * xhat, axis=0, keepdims=True)
        dxhat = dy * fwv
        dx_ref[...] = r * (dxhat - xhat * jnp.mean(dxhat * xhat, axis=-1, keepdims=True))

    tile = pl.BlockSpec((tl, D), lambda i: (i, 0))
    row = pl.BlockSpec((1, D), lambda i: (0, 0))
    return pl.pallas_call(
        body,
        out_shape=(
            jax.ShapeDtypeStruct((1, 1), F32),
            jax.ShapeDtypeStruct((L, D), F32),
            jax.ShapeDtypeStruct((1, D), F32),
        ),
        grid=(L // tl,),
        in_specs=[tile, row, tile],
        out_specs=(pl.BlockSpec((1, 1), lambda i: (0, 0)), tile, row),
        compiler_params=_params("arbitrary"),
        name=name,
    )(x, fw, target)


def outer8(ct, d, *, name):
    D, B = ct.shape
    S, _, N = d.shape
    tm = _pick(D, 512, 8)
    tn = _pick(N, 256, LANE)

    def body(c_ref, d_ref, o_ref):
        acc = c_ref[:, 0:1] * d_ref[0:1, :]
        for b in range(1, B):
            acc = acc + c_ref[:, b : b + 1] * d_ref[b : b + 1, :]
        o_ref[...] = acc

    return pl.pallas_call(
        body,
        out_shape=jax.ShapeDtypeStruct((S, D, N), F32),
        grid=(S, D // tm, N // tn),
        in_specs=[
            pl.BlockSpec((tm, B), lambda s, i, j: (i, 0)),
            pl.BlockSpec((None, B, tn), lambda s, i, j: (s, 0, j)),
        ],
        out_specs=pl.BlockSpec((None, tm, tn), lambda s, i, j: (s, i, j)),
        compiler_params=_params("parallel", "parallel", "parallel"),
        name=name,
    )(ct, d)


def reduce8(g, *, name):
    nd, R, N = g.shape

    def body(g_ref, o_ref):
        acc = g_ref[0]
        for b in range(1, nd):
            acc = acc + g_ref[b]
        o_ref[...] = acc

    return pl.pallas_call(
        body,
        out_shape=jax.ShapeDtypeStruct((R, N), F32),
        name=name,
    )(g)


def _as3(a):
    if a.ndim == 1:
        return a.reshape(1, 1, -1)
    if a.ndim == 2:
        return a.reshape((1,) + a.shape)
    return a.reshape((-1,) + a.shape[-2:])


def adamw(w, g, m, v, *, name):
    shape = w.shape
    w3, g3, m3, v3 = _as3(w), _as3(g), _as3(m), _as3(v)
    B, R, C = w3.shape
    tr = _pick(R, max(8, (1 << 19) // max(C, 1) // 8 * 8), 8)

    def body(w_ref, g_ref, m_ref, v_ref, d_ref, nm_ref, nv_ref):
        gv = g_ref[...]
        mn = ADAM_B1 * m_ref[...] + (1.0 - ADAM_B1) * gv
        vn = ADAM_B2 * v_ref[...] + (1.0 - ADAM_B2) * (gv * gv)
        m_hat = mn / (1.0 - ADAM_B1 ** ADAM_STEP)
        v_hat = vn / (1.0 - ADAM_B2 ** ADAM_STEP)
        d_ref[...] = -ADAM_LR * (m_hat / (jnp.sqrt(v_hat) + ADAM_EPS) + ADAM_WD * w_ref[...])
        nm_ref[...] = mn
        nv_ref[...] = vn

    tile = pl.BlockSpec((None, tr, C), lambda b, i: (b, i, 0))
    sds = jax.ShapeDtypeStruct((B, R, C), F32)
    d, nm, nv = pl.pallas_call(
        body,
        out_shape=(sds, sds, sds),
        grid=(B, R // tr),
        in_specs=[tile, tile, tile, tile],
        out_specs=(tile, tile, tile),
        compiler_params=_params("parallel", "parallel"),
        name=name,
    )(w3, g3, m3, v3)
    return d.reshape(shape), nm.reshape(shape), nv.reshape(shape)


def _place():
    return lax.axis_index("x"), lax.axis_index("y"), lax.axis_index("c")


def _flip(v, bit):
    return (1 - v) if bit else v


def ag8(v, *, act=None, name):
    R, N = v.shape

    def body(v_ref, out_ref, stage, send_sems, recv_sems):
        x, y, c = _place()
        me = 4 * x + 2 * y + c
        val = v_ref[...]
        if act is not None:
            val = act(val)
        stage[...] = val
        out_ref[me] = val
        sends = []
        for k in range(1, N_DEV):
            px, py, pc = _flip(x, k & 4), _flip(y, k & 2), _flip(c, k & 1)
            cp = pltpu.make_async_remote_copy(
                src_ref=stage, dst_ref=out_ref.at[me], send_sem=send_sems.at[k - 1], recv_sem=recv_sems.at[k - 1],
                device_id=(px, py, pc), device_id_type=MESH,
            )
            cp.start()
            sends.append(cp)
        for k in range(1, N_DEV):
            px, py, pc = _flip(x, k & 4), _flip(y, k & 2), _flip(c, k & 1)
            pltpu.make_async_remote_copy(
                src_ref=stage, dst_ref=out_ref.at[4 * px + 2 * py + pc], send_sem=send_sems.at[k - 1],
                recv_sem=recv_sems.at[k - 1], device_id=(px, py, pc), device_id_type=MESH,
            ).wait_recv()
        for cp in sends:
            cp.wait_send()

    return pl.pallas_call(
        body,
        out_shape=jax.ShapeDtypeStruct((N_DEV, R, N), F32),
        in_specs=[pl.BlockSpec(memory_space=pltpu.VMEM)],
        out_specs=pl.BlockSpec(memory_space=pltpu.VMEM),
        scratch_shapes=[
            pltpu.VMEM((R, N), F32),
            pltpu.SemaphoreType.DMA((N_DEV - 1,)),
            pltpu.SemaphoreType.DMA((N_DEV - 1,)),
        ],
        name=name,
    )(v)


def _other_chips(x, y):
    chips = [(1 - x, y), (x, 1 - y), (1 - x, 1 - y)]
    return chips, [2 * px + py for px, py in chips]


_HBM = pl.BlockSpec(memory_space=pltpu.HBM)


_SEM = pl.BlockSpec(memory_space=pltpu.SEMAPHORE)
_ANY = pl.BlockSpec(memory_space=pl.ANY)
_EFFECT = pltpu.SideEffectType.DATAFLOW_SIDE_EFFECTING


def _gather_copies(srcs, lands, send_sems, recv_sems):
    x, y, c = _place()
    k_me = 2 * x + y
    chips, _ = _other_chips(x, y)
    cps = []
    for w in range(len(srcs)):
        for j, (px, py) in enumerate(chips):
            cps.append(pltpu.make_async_remote_copy(
                src_ref=srcs[w].at[:, c], dst_ref=lands[w].at[:, k_me, c], send_sem=send_sems.at[3 * w + j],
                recv_sem=recv_sems.at[3 * w + j], device_id=(px, py, c), device_id_type=MESH,
            ))
    return cps


def _reduce_copies(srcs, lands, send_sems, recv_sems):
    x, y, c = _place()
    chips, kidx = _other_chips(x, y)
    cps = []
    for w in range(len(srcs)):
        for j, (px, py) in enumerate(chips):
            cps.append(pltpu.make_async_remote_copy(
                src_ref=srcs[w].at[:, kidx[j]], dst_ref=lands[w].at[j], send_sem=send_sems.at[3 * w + j],
                recv_sem=recv_sems.at[3 * w + j], device_id=(px, py, c), device_id_type=MESH,
            ))
    return cps


def split_start(copies, srcs, land_shapes, after, *, name):
    n = len(srcs)

    def body(*refs):
        src_refs, land_refs = refs[:n], refs[n : 2 * n]
        send_sems, recv_sems = refs[2 * n + 1], refs[2 * n + 2]
        token = refs[-1]
        for cp in copies(src_refs, land_refs, send_sems, recv_sems):
            cp.start()
        token[...] = jnp.zeros_like(token)

    lands = [pltpu.with_memory_space_constraint(lax.empty(sh, s.dtype), pltpu.HBM) for sh, s in zip(land_shapes, srcs)]
    srcs = [pltpu.with_memory_space_constraint(s, pltpu.HBM) for s in srcs]
    out = pl.pallas_call(
        body,
        out_shape=(
            pltpu.SemaphoreType.DMA((3 * n,)), pltpu.SemaphoreType.DMA((3 * n,)),
            *[pltpu.HBM(s.shape, s.dtype) for s in srcs],
            *[pltpu.HBM(sh, s.dtype) for sh, s in zip(land_shapes, srcs)],
            jax.ShapeDtypeStruct((8, LANE), F32),
        ),
        in_specs=[_HBM] * (2 * n) + [_ANY],
        out_specs=(_SEM, _SEM, *([_HBM] * (2 * n)), pl.BlockSpec(memory_space=pltpu.VMEM)),
        input_output_aliases={i: 2 + i for i in range(2 * n)},
        compiler_params=pltpu.CompilerParams(has_side_effects=_EFFECT),
        name=name,
    )(*srcs, *lands, after)
    return out[0], out[1], list(out[2 : 2 + n]), list(out[2 + n : 2 + 2 * n]), out[-1]


def split_wait(copies, send_sems, recv_sems, srcs, lands, after, *, name):
    n = len(srcs)

    def body(*refs):
        src_refs, land_refs = refs[:n], refs[n : 2 * n]
        send_ref, recv_ref = refs[2 * n], refs[2 * n + 1]
        for cp in copies(src_refs, land_refs, send_ref, recv_ref):
            cp.wait_send()
            cp.wait_recv()

    out = pl.pallas_call(
        body,
        out_shape=tuple(pltpu.HBM(a.shape, a.dtype) for a in list(srcs) + list(lands)),
        in_specs=[_HBM] * (2 * n) + [_SEM, _SEM, _ANY],
        out_specs=tuple([_HBM] * (2 * n)),
        input_output_aliases={i: i for i in range(2 * n)},
        compiler_params=pltpu.CompilerParams(has_side_effects=_EFFECT),
        name=name,
    )(*srcs, *lands, send_sems, recv_sems, after)
    return list(out[n:])


def gather_fill(shards, lands, *, name):
    n = len(shards)

    def body(*refs):
        ins = refs[:n]
        outs = refs[2 * n : 3 * n]
        send_sems, recv_sems = refs[3 * n :]
        x, y, c = _place()
        k_me = 2 * x + y
        _, kidx = _other_chips(x, y)
        sib = (x, y, 1 - c)
        cps = []
        for w in range(n):
            cps.append(pltpu.make_async_remote_copy(
                src_ref=ins[w], dst_ref=outs[w].at[:, k_me], send_sem=send_sems.at[4 * w + 3], recv_sem=recv_sems.at[4 * w + 3],
                device_id=sib, device_id_type=MESH,
            ))
            for j in range(3):
                part = outs[w].at[:, kidx[j], c]
                cps.append(pltpu.make_async_remote_copy(
                    src_ref=part, dst_ref=part, send_sem=send_sems.at[4 * w + j], recv_sem=recv_sems.at[4 * w + j],
                    device_id=sib, device_id_type=MESH,
                ))
        for cp in cps:
            cp.start()
        for w in range(n):
            pltpu.make_async_remote_copy(
                src_ref=ins[w], dst_ref=outs[w].at[:, k_me], send_sem=send_sems.at[4 * w + 3], recv_sem=recv_sems.at[4 * w + 3],
                device_id=sib, device_id_type=MESH,
            ).wait_recv()
            for j in range(3):
                got = outs[w].at[:, kidx[j], 1 - c]
                pltpu.make_async_remote_copy(
                    src_ref=got, dst_ref=got, send_sem=send_sems.at[4 * w + j], recv_sem=recv_sems.at[4 * w + j],
                    device_id=sib, device_id_type=MESH,
                ).wait_recv()
        for cp in cps:
            cp.wait_send()

    return list(pl.pallas_call(
        body,
        out_shape=tuple(jax.ShapeDtypeStruct(a.shape, a.dtype) for a in lands),
        in_specs=[_HBM] * (2 * n),
        out_specs=tuple([_HBM] * n),
        scratch_shapes=[pltpu.SemaphoreType.DMA((4 * n,)), pltpu.SemaphoreType.DMA((4 * n,))],
        input_output_aliases={n + w: w for w in range(n)},
        name=name,
    )(*shards, *lands))


def rs_sibling(grads, *, name):
    n = len(grads)

    def body(*refs):
        ins, outs = refs[:n], refs[n : 2 * n]
        send_sems, recv_sems = refs[2 * n :]
        x, y, c = _place()
        cps = []
        for w in range(n):
            cp = pltpu.make_async_remote_copy(
                src_ref=ins[w].at[:, :, 1 - c], dst_ref=outs[w], send_sem=send_sems.at[w], recv_sem=recv_sems.at[w],
                device_id=(x, y, 1 - c), device_id_type=MESH,
            )
            cp.start()
            cps.append(cp)
        for cp in cps:
            cp.wait()

    return list(pl.pallas_call(
        body,
        out_shape=tuple(jax.ShapeDtypeStruct(g.shape[:2] + g.shape[3:], g.dtype) for g in grads),
        in_specs=[_HBM] * n,
        out_specs=tuple([_HBM] * n),
        scratch_shapes=[pltpu.SemaphoreType.DMA((n,)), pltpu.SemaphoreType.DMA((n,))],
        name=name,
    )(*grads))


def rs_share(halves, *, name):
    n = len(halves)

    def body(*refs):
        outs = refs[n : 2 * n]
        send_sems, recv_sems = refs[2 * n :]
        x, y, c = _place()
        cps = []
        for w in range(n):
            cp = pltpu.make_async_remote_copy(
                src_ref=outs[w].at[:, c], dst_ref=outs[w].at[:, c], send_sem=send_sems.at[w], recv_sem=recv_sems.at[w],
                device_id=(x, y, 1 - c), device_id_type=MESH,
            )
            cp.start()
            cps.append(cp)
        for w, cp in enumerate(cps):
            cp.wait_send()
            pltpu.make_async_remote_copy(
                src_ref=outs[w].at[:, c], dst_ref=outs[w].at[:, 1 - c], send_sem=send_sems.at[w], recv_sem=recv_sems.at[w],
                device_id=(x, y, 1 - c), device_id_type=MESH,
            ).wait_recv()

    return pl.pallas_call(
        body,
        out_shape=tuple(jax.ShapeDtypeStruct(h.shape, h.dtype) for h in halves),
        in_specs=[_HBM] * n,
        out_specs=tuple([_HBM] * n),
        scratch_shapes=[pltpu.SemaphoreType.DMA((n,)), pltpu.SemaphoreType.DMA((n,))],
        input_output_aliases={w: w for w in range(n)},
        name=name,
    )(*halves)


def _row_tile(R, C):
    return _pick(R, max(16, (1 << 19) // C // 16 * 16), 16)


def _my_core():
    return lax.axis_index("c")


def _my_chip():
    return 2 * lax.axis_index("x") + lax.axis_index("y")


def rs_add_pair(g, r, *, name):
    M, K, _, R, C = g.shape
    tr = _row_tile(R, C)

    def body(g_ref, r_ref, o_ref):
        o_ref[...] = (g_ref[...].astype(F32) + r_ref[...].astype(F32)).astype(o_ref.dtype)

    blk = pl.BlockSpec((None, None, tr, C), lambda m, k, i: (m, k, i, 0))
    return pl.pallas_call(
        body,
        out_shape=jax.ShapeDtypeStruct((M, K, R, C), BF16),
        grid=(M, K, R // tr),
        in_specs=[pl.BlockSpec((None, None, None, tr, C), lambda m, k, i: (m, k, _my_core(), i, 0)), blk],
        out_specs=blk,
        compiler_params=_params("parallel", "parallel", "parallel"),
        name=name,
    )(g, r)


def rs_add_final(g, r, t, full, m0, *, name):
    M, K, _, R, C = g.shape
    tr = _row_tile(R, C)

    def body(g_ref, r_ref, t_ref, full_ref, o_ref):
        acc = g_ref[...].astype(F32) + r_ref[...].astype(F32)
        for j in range(3):
            acc = acc + t_ref[j].astype(F32)
        o_ref[...] = acc

    return pl.pallas_call(
        body,
        out_shape=jax.ShapeDtypeStruct(full.shape, full.dtype),
        grid=(M, R // tr),
        in_specs=[
            pl.BlockSpec((None, None, None, tr, C), lambda m, i: (m, _my_chip(), _my_core(), i, 0)),
            pl.BlockSpec((None, None, tr, C), lambda m, i: (m, _my_chip(), i, 0)),
            pl.BlockSpec((3, None, tr, C), lambda m, i: (0, m, i, 0)),
            _ANY,
        ],
        out_specs=pl.BlockSpec((None, None, tr, C), lambda m, i: (m0 + m, _my_core(), i, 0)),
        input_output_aliases={3: 0},
        compiler_params=_params("parallel", "parallel"),
        name=name,
    )(g, r, t, full)


WEIGHTS = ["ffn_norm_w", "ffn_w_gu", "ffn_w_down", "mod_w", "mod_b", "mix_norm_w", "ssm_w_in", "ssm_conv_w", "ssm_conv_b",
           "ssm_dt_bias", "ssm_a_log", "ssm_d", "ssm_norm_w", "ssm_w_out", "kv_norm_w", "kv_mod_w", "kv_mod_b", "w_kv", "b_kv",
           "attn_w_q", "attn_b_q", "attn_sinks", "attn_w_o", "attn_b_o", "final_norm_w"]
GATHERED = ["ffn_w_gu", "ffn_w_down", "ssm_w_in", "ssm_w_out", "w_kv", "attn_w_q", "attn_w_o"]
COLUMN_PARALLEL = ["mod_w", "kv_mod_w"]
SMALL_SHARDED = ["ffn_norm_w", "ssm_conv_w", "ssm_conv_b", "ssm_norm_w"]
SMALL = [n for n in WEIGHTS if n not in GATHERED and n not in COLUMN_PARALLEL]


def _row_halves(a):
    a = a.reshape((-1,) + a.shape[-2:])
    return a.reshape(a.shape[0], 2, a.shape[1] // 2, a.shape[2])


def _pack(arrs, rows=8):
    flat = jnp.concatenate([a.reshape(-1) for a in arrs])
    n = flat.shape[0]
    pad = (-n) % (rows * LANE)
    return jnp.pad(flat, (0, pad)).reshape(rows, -1), n


def _unpack(flat, like):
    out, o = [], 0
    for s in like:
        k = 1
        for d in s:
            k *= d
        out.append(flat[o : o + k].reshape(s))
        o += k
    return out


def kernel(x, c, ffn_norm_w, ffn_w_gu, ffn_w_down, mod_w, mod_b, mix_norm_w, ssm_w_in, ssm_conv_w, ssm_conv_b, ssm_dt_bias, ssm_a_log, ssm_d, ssm_norm_w, ssm_w_out, kv_norm_w, kv_mod_w, kv_mod_b, w_kv, b_kv, attn_w_q, attn_b_q, attn_sinks, attn_w_o, attn_b_o, final_norm_w, loss_target, m_ffn_norm_w, m_ffn_w_gu, m_ffn_w_down, m_mod_w, m_mod_b, m_mix_norm_w, m_ssm_w_in, m_ssm_conv_w, m_ssm_conv_b, m_ssm_dt_bias, m_ssm_a_log, m_ssm_d, m_ssm_norm_w, m_ssm_w_out, m_kv_norm_w, m_kv_mod_w, m_kv_mod_b, m_w_kv, m_b_kv, m_attn_w_q, m_attn_b_q, m_attn_sinks, m_attn_w_o, m_attn_b_o, m_final_norm_w, v_ffn_norm_w, v_ffn_w_gu, v_ffn_w_down, v_mod_w, v_mod_b, v_mix_norm_w, v_ssm_w_in, v_ssm_conv_w, v_ssm_conv_b, v_ssm_dt_bias, v_ssm_a_log, v_ssm_d, v_ssm_norm_w, v_ssm_w_out, v_kv_norm_w, v_kv_mod_w, v_kv_mod_b, v_w_kv, v_b_kv, v_attn_w_q, v_attn_b_q, v_attn_sinks, v_attn_w_o, v_attn_b_o, v_final_norm_w):
    env = dict(locals())
    W = {n: env[n] for n in WEIGHTS}
    MOM = {n: env["m_" + n] for n in WEIGHTS}
    VAR = {n: env["v_" + n] for n in WEIGHTS}

    ax, ay, ac = _place()
    kme = 2 * ax + ay
    me = 4 * ax + 2 * ay + ac

    xs = x[0]
    target = loss_target[0]
    L, D = xs.shape
    depth, n_a = ffn_w_gu.shape[0], ssm_w_in.shape[0]
    n_b = depth - n_a
    T = ffn_w_gu.shape[-1]
    DI = ssm_w_out.shape[1] * N_CHIPS
    CI = ssm_w_in.shape[2]
    CC = ssm_conv_w.shape[2] * N_CHIPS
    MW = mod_w.shape[2]
    KW = kv_mod_w.shape[1]
    KVD = w_kv.shape[1]

    def chip_cols(a, width):
        return lax.dynamic_slice_in_dim(a, kme * width, width, axis=a.ndim - 1)

    def ffn_items(i, j):
        return [("ffn_w_gu", 2 * i + j, ffn_w_gu[i, j]), ("ffn_w_down", 2 * i + j, ffn_w_down[i, j])]

    def mix_items(i):
        if i < n_a:
            return [("ssm_w_in", i, ssm_w_in[i]), ("ssm_w_out", i, ssm_w_out[i])]
        return [("attn_w_q", i - n_a, attn_w_q[i - n_a]), ("attn_w_o", i - n_a, attn_w_o[i - n_a])]

    def layer_items(i, order):
        kv_items = [("w_kv", 0, w_kv)] if i == n_a else []
        if order == "fwd":
            return kv_items + ffn_items(i, 0) + mix_items(i) + ffn_items(i, 1)
        return ffn_items(i, 1) + mix_items(i) + ffn_items(i, 0) + kv_items

    fwd_stages = [ffn_items(0, 0), mix_items(0) + ffn_items(0, 1)] + [layer_items(i, "fwd") for i in range(1, depth)]
    bwd_stages = [layer_items(i, "bwd") for i in range(depth - 1, 0, -1)] + [ffn_items(0, 1) + mix_items(0), ffn_items(0, 0)]

    gw, inflight = {}, {}

    def gather_begin(s, after):
        keys = [(n, m0) for n, m0, _ in fwd_stages[s]]
        shards = [_row_halves(a.astype(BF16)) for _, _, a in fwd_stages[s]]
        land_shapes = [(sh.shape[0], N_CHIPS) + sh.shape[1:] for sh in shards]
        ssem, rsem, srcs, lands, token = split_start(_gather_copies, shards, land_shapes, after, name=f"gather_start_{s}")
        inflight[s] = (keys, ssem, rsem, srcs, lands)
        return token[0:1, 0:1]

    def gather_end(s, after):
        keys, ssem, rsem, srcs, lands = inflight.pop(s)
        lands = split_wait(_gather_copies, ssem, rsem, srcs, lands, after, name=f"gather_wait_{s}")
        lands = gather_fill(srcs, lands, name=f"gather_fill_{s}")
        gw.update(zip(keys, lands))
        return lands[0]

    def g_gu(i, j):
        return gw["ffn_w_gu", 2 * i + j].reshape(N_CHIPS, D, T)

    def g_dn(i, j):
        return gw["ffn_w_down", 2 * i + j].reshape(2, T, D)

    def g_full(n, m0):
        a = gw[n, m0]
        return a.reshape(N_CHIPS * 2 * a.shape[-2], a.shape[-1])

    c = c + gather_begin(0, c)

    sm_like = [W[n].shape for n in SMALL_SHARDED]
    sm_pack, sm_n = _pack([W[n] for n in SMALL_SHARDED])
    sm_all = ag8(sm_pack, name="ag_small_w")[0::2].reshape(N_CHIPS, -1)[:, :sm_n]
    full = {}
    for n, part in zip(SMALL_SHARDED, zip(*[_unpack(sm_all[k], sm_like) for k in range(N_CHIPS)])):
        full[n] = jnp.concatenate(part, axis=-1)

    c_all = ag8(c, act=_silu, name="ag_c").reshape(N_DEV, D)
    p_mod = mm(c_all, mod_w, bias=chip_cols(mod_b, MW)[:, None, :], name="mod_mm")
    p_kv = mm(c_all, kv_mod_w, bias=chip_cols(kv_mod_b, KW)[None, :], name="kvmod_mm")
    p_all = jnp.concatenate([jnp.transpose(p_mod, (1, 0, 2)).reshape(N_DEV, depth * MW), p_kv], axis=1)
    p_mine = lax.dynamic_index_in_dim(ag8(p_all, name="ag_mod")[0::2], me, axis=1, keepdims=False)
    mod = jnp.transpose(p_mine[:, : depth * MW].reshape(N_CHIPS, depth, MW), (1, 0, 2)).reshape(depth, N_MOD * D)
    kvmod = p_mine[:, depth * MW :].reshape(1, 2 * D)
    mods = [[mod[i : i + 1, j * D : (j + 1) * D] for j in range(N_MOD)] for i in range(depth)]
    kv_shift, kv_scale = kvmod[:, :D], kvmod[:, D:]

    def ffn_fwd(xin, i, j, sh, sc, gt):
        h = norm_mod_fwd(xin, full["ffn_norm_w"][i, j][None], sc, sh, name=f"ffn_norm_{i}_{j}")
        gu = mm(h, g_gu(i, j), name=f"ffn_gu_{i}_{j}")
        a = swiglu_fwd(gu, name=f"ffn_act_{i}_{j}")
        f = mm(a, g_dn(i, j), reduce_s=True, name=f"ffn_down_{i}_{j}")
        return gate_fwd(xin, f, gt, FFN_HALF, name=f"ffn_res_{i}_{j}"), (xin, gu, a, f)

    def ssm_fwd(xin, i, sh, sc, gt):
        h = norm_mod_fwd(xin, mix_norm_w[i][None], sc, sh, name=f"mix_norm_{i}")
        zx4 = mm(h, gw["ssm_w_in", i].reshape(N_CHIPS, D, CI), name=f"ssm_in_{i}")
        zx = jnp.transpose(zx4, (1, 0, 2)).reshape(L, N_CHIPS * CI)
        xbc = conv_fwd(zx, full["ssm_conv_w"][i], full["ssm_conv_b"][i][None], DI, name=f"ssm_conv_{i}")
        dt_raw = zx[:, DI + CC :]
        y, states = ssd_fwd(xbc, dt_raw, ssm_dt_bias[i][None], ssm_a_log[i][None], ssm_d[i][None], DI, name=f"ssd_{i}")
        yn = gnorm_fwd(y, zx, full["ssm_norm_w"][i][None], name=f"ssm_gnorm_{i}")
        f = mm(yn, g_full("ssm_w_out", i), name=f"ssm_out_{i}")
        return gate_fwd(xin, f, gt, 1.0, name=f"mix_res_{i}"), (xin, zx, xbc, dt_raw, y, states, yn, f)

    def att_fwd(xin, i, kv, sh, sc, gt):
        l = i - n_a
        h = norm_mod_fwd(xin, mix_norm_w[i][None], sc, sh, name=f"mix_norm_{i}")
        q = mm(h, g_full("attn_w_q", l), bias=attn_b_q[l][None], name=f"att_q_{i}")
        o = attn_fwd(q, kv, attn_sinks[l][None], name=f"att_{i}")
        f = mm(o, g_full("attn_w_o", l), bias=attn_b_o[l][None], name=f"att_o_{i}")
        return gate_fwd(xin, f, gt, 1.0, name=f"mix_res_{i}"), (xin, q, o, f)

    saved = []
    xc = xs
    kv = x_kv = None
    landed = gather_end(0, kvmod)
    n_stage = len(fwd_stages)

    def next_stage(s, landed):
        return gather_begin(s + 1, landed) if s + 1 < n_stage else 0.0

    for i in range(depth):
        sh1, sc1, g1, shm, scm, gm, sh2, sc2, g2 = mods[i]
        s = i + 1
        tok = next_stage(0 if i == 0 else s, landed)
        if i == n_a:
            x_kv = xc
            hkv = norm_mod_fwd(xc, kv_norm_w[None], kv_scale, kv_shift + tok, name="kv_norm")
            kv = mm(hkv, g_full("w_kv", 0), bias=b_kv[None], name="kv_proj")
        xc, s1 = ffn_fwd(xc, i, 0, sh1 + tok, sc1, g1)
        if i == 0:
            landed = gather_end(1, xc)
            shm = shm + next_stage(1, landed)
        xc, sm = ssm_fwd(xc, i, shm, scm, gm) if i < n_a else att_fwd(xc, i, kv, shm, scm, gm)
        xc, s2 = ffn_fwd(xc, i, 1, sh2, sc2, g2)
        if s + 1 < n_stage:
            landed = gather_end(s + 1, xc)
        saved.append((s1, sm, s2))

    loss_part, dx, d_final = final_loss(xc, final_norm_w[None], target, name="loss_head")
    loss = lax.psum(loss_part[0, 0], ("x", "y", "c"))

    wg = {}
    sg = {
        "ffn_norm_w": [[None, None] for _ in range(depth)], "mix_norm_w": [None] * depth, "mod": [None] * depth,
        "ssm_conv_w": [None] * n_a, "ssm_conv_b": [None] * n_a, "ssm_dt_bias": [None] * n_a, "ssm_a_log": [None] * n_a,
        "ssm_d": [None] * n_a, "ssm_norm_w": [None] * n_a, "attn_b_q": [None] * n_b, "attn_sinks": [None] * n_b,
        "attn_b_o": [None] * n_b,
    }

    def ffn_bwd(dxo, i, j, sv, sh, sc, gt):
        xin, gu, a, f = sv
        df, dgt, _ = gate_bwd(f, dxo, gt, FFN_HALF, name=f"ffn_res_bwd_{i}_{j}")
        da = mm(df, g_dn(i, j), mode="nt", name=f"ffn_down_dx_{i}_{j}")
        wg["ffn_w_down", 2 * i + j] = mm(a, df, mode="tn", out_dtype=BF16, name=f"ffn_down_dw_{i}_{j}")
        dgu = swiglu_bwd(gu, da, name=f"ffn_act_bwd_{i}_{j}")
        nw = full["ffn_norm_w"][i, j][None]
        h = norm_mod_fwd(xin, nw, sc, sh, name=f"ffn_norm_re_{i}_{j}")
        dh = mm(dgu, g_gu(i, j), mode="nt", reduce_s=True, name=f"ffn_gu_dx_{i}_{j}")
        wg["ffn_w_gu", 2 * i + j] = mm(h, dgu, mode="tn", out_dtype=BF16, name=f"ffn_gu_dw_{i}_{j}")
        dxi, dnw, dsc, dsh = norm_mod_bwd(xin, nw, sc, dh, dxo, name=f"ffn_norm_bwd_{i}_{j}")
        sg["ffn_norm_w"][i][j] = dnw
        return dxi, (dsh, dsc, dgt)

    def ssm_bwd(dxo, i, sv, sh, sc, gt):
        xin, zx, xbc, dt_raw, y, states, yn, f = sv
        df, dgt, _ = gate_bwd(f, dxo, gt, 1.0, name=f"mix_res_bwd_{i}")
        dyn = mm(df, g_full("ssm_w_out", i), mode="nt", name=f"ssm_out_dx_{i}")
        wg["ssm_w_out", i] = mm(yn, df, mode="tn", out_dtype=BF16, name=f"ssm_out_dw_{i}")
        dy, dz, dnorm = gnorm_bwd(y, zx, full["ssm_norm_w"][i][None], dyn, name=f"ssm_gnorm_bwd_{i}")
        dxbc, ddt, dbias, dalog, ddsk = ssd_bwd(
            xbc, dt_raw, ssm_dt_bias[i][None], ssm_a_log[i][None], ssm_d[i][None], states, dy, DI, name=f"ssd_bwd_{i}"
        )
        du, dcw, dcb = conv_bwd(zx, full["ssm_conv_w"][i], full["ssm_conv_b"][i][None], dxbc, DI, name=f"ssm_conv_bwd_{i}")
        dzx = jnp.concatenate([dz, du, ddt], axis=1).astype(BF16)
        dzx4 = jnp.transpose(dzx.reshape(L, N_CHIPS, CI), (1, 0, 2))
        nw = mix_norm_w[i][None]
        h = norm_mod_fwd(xin, nw, sc, sh, name=f"mix_norm_re_{i}")
        dh = mm(dzx4, gw["ssm_w_in", i].reshape(N_CHIPS, D, CI), mode="nt", reduce_s=True, name=f"ssm_in_dx_{i}")
        wg["ssm_w_in", i] = mm(h, dzx4, mode="tn", out_dtype=BF16, name=f"ssm_in_dw_{i}")
        dxi, dnw, dsc, dsh = norm_mod_bwd(xin, nw, sc, dh, dxo, name=f"mix_norm_bwd_{i}")
        sg["mix_norm_w"][i] = dnw
        sg["ssm_conv_w"][i], sg["ssm_conv_b"][i], sg["ssm_norm_w"][i] = dcw, dcb, dnorm
        sg["ssm_dt_bias"][i], sg["ssm_a_log"][i], sg["ssm_d"][i] = dbias, dalog, ddsk
        return dxi, (dsh, dsc, dgt)

    def att_bwd(dxo, i, sv, dkv, sh, sc, gt):
        l = i - n_a
        xin, q, o, f = sv
        df, dgt, dfsum = gate_bwd(f, dxo, gt, 1.0, name=f"mix_res_bwd_{i}")
        do = mm(df, g_full("attn_w_o", l), mode="nt", name=f"att_o_dx_{i}")
        wg["attn_w_o", l] = mm(o, df, mode="tn", out_dtype=BF16, name=f"att_o_dw_{i}")
        dq, dkv, dsink = attn_bwd(q, kv, attn_sinks[l][None], do, dkv, name=f"att_bwd_{i}")
        nw = mix_norm_w[i][None]
        h = norm_mod_fwd(xin, nw, sc, sh, name=f"mix_norm_re_{i}")
        dh = mm(dq, g_full("attn_w_q", l), mode="nt", name=f"att_q_dx_{i}")
        wg["attn_w_q", l] = mm(h, dq, mode="tn", out_dtype=BF16, name=f"att_q_dw_{i}")
        dxi, dnw, dsc, dsh = norm_mod_bwd(xin, nw, sc, dh, dxo, name=f"mix_norm_bwd_{i}")
        sg["mix_norm_w"][i] = dnw
        sg["attn_b_q"][l], sg["attn_sinks"][l], sg["attn_b_o"][l] = colsum(dq, name=f"att_bq_{i}"), dsink, dfsum
        return dxi, dkv, (dsh, dsc, dgt)

    gfull = {n: lax.empty(_row_halves(W[n]).shape, F32) for n in GATHERED}

    pending = []

    def reduce_step(s, after):
        if pending:
            reduce_end(after)
        items = bwd_stages[s]
        parts = []
        for n, m0, a in items:
            m, _, rh, cc = _row_halves(a).shape
            parts.append(wg.pop((n, m0)).reshape(m, N_CHIPS, 2, rh, cc))
        from_sib = rs_sibling(parts, name=f"rs_sibling_{s}")
        pair = [rs_add_pair(g, r, name=f"rs_pair_{n}_{m0}") for (n, m0, _), g, r in zip(items, parts, from_sib)]
        land_shapes = [(3, p.shape[0]) + p.shape[2:] for p in pair]
        ssem, rsem, srcs, lands, token = split_start(_reduce_copies, pair, land_shapes, after, name=f"rs_chips_start_{s}")
        pending.append((s, items, parts, from_sib, ssem, rsem, srcs, lands))
        return token[0:1, 0:1]

    def reduce_end(after):
        s, items, parts, from_sib, ssem, rsem, srcs, lands = pending.pop()
        from_chips = split_wait(_reduce_copies, ssem, rsem, srcs, lands, after, name=f"rs_chips_wait_{s}")
        for (n, m0, _), g, r, t in zip(items, parts, from_sib, from_chips):
            gfull[n] = rs_add_final(g, r, t, gfull[n], m0, name=f"rs_final_{n}_{m0}")

    dkv = jnp.zeros((L, KVD), F32)
    d_kvnorm = d_kvmod = d_bkv = None
    tok = 0.0
    for i in reversed(range(depth)):
        sh1, sc1, g1, shm, scm, gm, sh2, sc2, g2 = mods[i]
        s1, sm, s2 = saved[i]
        dx, dm2 = ffn_bwd(dx, i, 1, s2, sh2, sc2, g2 + tok)
        if i < n_a:
            dx, dmm = ssm_bwd(dx, i, sm, shm, scm, gm)
        else:
            dx, dkv, dmm = att_bwd(dx, i, sm, dkv, shm, scm, gm)
        if i == 0:
            g1 = g1 + reduce_step(depth - 1, dx)
        dx, dm1 = ffn_bwd(dx, i, 0, s1, sh1, sc1, g1)
        sg["mod"][i] = jnp.concatenate(list(dm1) + list(dmm) + list(dm2), axis=1)
        if i == n_a:
            d_bkv = colsum(dkv, name="kv_bias_bwd")
            hkv = norm_mod_fwd(x_kv, kv_norm_w[None], kv_scale, kv_shift, name="kv_norm_re")
            dh = mm(dkv, g_full("w_kv", 0), mode="nt", name="kv_proj_dx")
            wg["w_kv", 0] = mm(hkv, dkv, mode="tn", out_dtype=BF16, name="kv_proj_dw")
            dx, d_kvnorm, dsc, dsh = norm_mod_bwd(x_kv, kv_norm_w[None], kv_scale, dh, dx, name="kv_norm_bwd")
            d_kvmod = jnp.concatenate([dsh, dsc], axis=1)
        tok = reduce_step(depth - i if i == 0 else depth - 1 - i, dx)
    grad_x = dx[None]
    grads = {}

    small = {
        "ffn_norm_w": jnp.stack([jnp.stack([r[0] for r in row]) for row in sg["ffn_norm_w"]]),
        "mod_b": jnp.stack([r[0] for r in sg["mod"]]),
        "mix_norm_w": jnp.stack([r[0] for r in sg["mix_norm_w"]]),
        "ssm_conv_w": jnp.stack(sg["ssm_conv_w"]),
        "ssm_conv_b": jnp.stack([r[0] for r in sg["ssm_conv_b"]]),
        "ssm_dt_bias": jnp.stack([r[0] for r in sg["ssm_dt_bias"]]),
        "ssm_a_log": jnp.stack([r[0] for r in sg["ssm_a_log"]]),
        "ssm_d": jnp.stack([r[0] for r in sg["ssm_d"]]),
        "ssm_norm_w": jnp.stack([r[0] for r in sg["ssm_norm_w"]]),
        "kv_norm_w": d_kvnorm[0],
        "kv_mod_b": d_kvmod[0],
        "b_kv": d_bkv[0],
        "attn_b_q": jnp.stack([r[0] for r in sg["attn_b_q"]]),
        "attn_sinks": jnp.stack([r[0] for r in sg["attn_sinks"]]),
        "attn_b_o": jnp.stack([r[0] for r in sg["attn_b_o"]]),
        "final_norm_w": d_final[0],
    }
    small_like = [small[n].shape for n in SMALL]
    sv_pack, sv_n = _pack([small[n] for n in SMALL])
    sv_all = ag8(sv_pack + tok, name="ag_small_g")
    sv_sum = reduce8(sv_all, name="small_g_sum").reshape(-1)[:sv_n]
    for n, gsum in zip(SMALL, _unpack(sv_sum, small_like)):
        grads[n] = chip_cols(gsum, W[n].shape[-1]) if n in SMALL_SHARDED else gsum

    per_dev = [_unpack(sv_all[b].reshape(-1)[:sv_n], small_like) for b in range(N_DEV)]
    i_modb, i_kvb = SMALL.index("mod_b"), SMALL.index("kv_mod_b")
    dmod_all = jnp.stack([chip_cols(p[i_modb], MW) for p in per_dev], axis=1)
    dkv_all = jnp.stack([chip_cols(p[i_kvb], KW) for p in per_dev], axis=0)[None]
    c_t = jnp.transpose(c_all)
    grads["mod_w"] = outer8(c_t, dmod_all, name="mod_w_grad")
    grads["kv_mod_w"] = outer8(c_t, dkv_all, name="kv_mod_w_grad")[0]

    delta, new_m, new_v = {}, {}, {}
    for n in COLUMN_PARALLEL:
        delta[n], new_m[n], new_v[n] = adamw(W[n], grads[n], MOM[n], VAR[n], name=f"adamw_{n}")
    like = [W[n].shape for n in SMALL]
    packs = [_pack([d[n] for n in SMALL])[0] for d in (W, grads, MOM, VAR)]
    n_small = sum(int(W[n].size) for n in SMALL)
    for dst, res in zip((delta, new_m, new_v), adamw(*packs, name="adamw_small")):
        for n, a in zip(SMALL, _unpack(res.reshape(-1)[:n_small], like)):
            dst[n] = a

    reduce_end(delta["mod_w"])
    for n, s in zip(GATHERED, rs_share([gfull[n] for n in GATHERED], name="rs_share")):
        grads[n] = s.reshape(W[n].shape)
        delta[n], new_m[n], new_v[n] = adamw(W[n], grads[n], MOM[n], VAR[n], name=f"adamw_{n}")

    return (loss, grad_x, *[grads[n] for n in WEIGHTS], *[delta[n] for n in WEIGHTS], *[new_m[n] for n in WEIGHTS],
            *[new_v[n] for n in WEIGHTS])
```

```python
import functools

import jax
import jax.numpy as jnp
from jax import lax
from jax.experimental import pallas as pl
from jax.experimental.pallas import tpu as pltpu

F32 = jnp.float32
BF16 = jnp.bfloat16
HIGHEST = lax.Precision.HIGHEST
MESH = pl.DeviceIdType.MESH

EPS = 1e-5
N_MOD = 9
FFN_HALF = 0.5
SSM_HEADDIM = 64
SSM_GROUPS = 8
SSM_STATE = 128
CONV_WIDTH = 4
CHUNK = 128
KV_HEADS = 4
HEAD_DIM = 64
WINDOW = 128
N_CHIPS = 4
N_DEV = 8

ADAM_LR = 0.001
ADAM_B1 = 0.9
ADAM_B2 = 0.999
ADAM_EPS = 1e-08
ADAM_WD = 0.01
ADAM_STEP = 10

LANE = 128
MM_TILE = 1024


def _pick(n, pref, align, whole_if_small=False):
    best = 0
    t = align
    while t <= min(n, pref):
        if n % t == 0:
            best = t
        t += align
    if best == 0 or (whole_if_small and best < 256 and n <= 2048):
        return n
    return best


def _sigmoid(x):
    return 1.0 / (1.0 + jnp.exp(-x))


def _silu(x):
    return x * _sigmoid(x)


def _dsilu(x):
    s = _sigmoid(x)
    return s * (1.0 + x * (1.0 - s))


def _params(*sem):
    return pltpu.CompilerParams(dimension_semantics=sem)


def mm(a, b, *, mode="nn", reduce_s=False, out_dtype=F32, bias=None, a_lead=(), b_lead=(), out_buf=None,
       out_lead=(), name):
    a_s = a.ndim - len(a_lead) == 3
    b_s = b.ndim - len(b_lead) == 3
    S = a.shape[len(a_lead)] if a_s else (b.shape[len(b_lead)] if b_s else 1)
    a2 = a.shape[-2:]
    b2 = b.shape[-2:]
    if mode == "nn":
        (M, K), (K2, N) = a2, b2
    elif mode == "nt":
        (M, K), (N, K2) = a2, b2
    else:
        (K, M), (K2, N) = a2, b2
    assert K == K2, (a.shape, b.shape, mode)
    batch = (a_s or b_s) and not reduce_s
    sb = S if batch else 1
    sr = S if ((a_s or b_s) and reduce_s) else 1
    tm = _pick(M, MM_TILE, LANE if mode == "tn" else 16, True)
    tn = _pick(N, MM_TILE, LANE, True)
    tk = _pick(K, MM_TILE, LANE if mode != "tn" else 16, True)
    nk = K // tk
    grid = (sb, M // tm, N // tn, sr, nk)

    def s_of(isb, isr):
        return isb if batch else isr

    def a_map(isb, i, j, isr, k):
        idx = (k, i) if mode == "tn" else (i, k)
        return tuple(a_lead) + (((s_of(isb, isr),) + idx) if a_s else idx)

    def b_map(isb, i, j, isr, k):
        idx = (j, k) if mode == "nt" else (k, j)
        return tuple(b_lead) + (((s_of(isb, isr),) + idx) if b_s else idx)

    def o_map(isb, i, j, isr, k):
        return tuple(out_lead) + ((isb, i, j) if batch else (i, j))

    def lead_blk(lead, has_s, blk):
        return (None,) * (len(lead) + (1 if has_s else 0)) + blk

    a_blk = (tk, tm) if mode == "tn" else (tm, tk)
    b_blk = (tn, tk) if mode == "nt" else (tk, tn)
    in_specs = [pl.BlockSpec(lead_blk(a_lead, a_s, a_blk), a_map), pl.BlockSpec(lead_blk(b_lead, b_s, b_blk), b_map)]
    args = [a, b]
    if bias is not None:
        bias_s = bias.ndim == 3
        in_specs.append(
            pl.BlockSpec(
                ((None, 1, tn) if bias_s else (1, tn)),
                (lambda isb, i, j, isr, k: (isb, 0, j)) if bias_s else (lambda isb, i, j, isr, k: (0, j)),
            )
        )
        args.append(bias)
    aliases = {}
    if out_buf is not None:
        in_specs.append(pl.BlockSpec(memory_space=pl.ANY))
        aliases = {len(args): 0}
        args.append(out_buf)
        out_shape = jax.ShapeDtypeStruct(out_buf.shape, out_buf.dtype)
        want = tuple(out_buf.shape[len(out_lead):])
        assert want == ((sb, M, N) if batch else (M, N)), (want, sb, M, N)
    else:
        out_shape = jax.ShapeDtypeStruct(((sb, M, N) if batch else (M, N)), out_dtype)
    dims = {"nn": (((1,), (0,)), ((), ())), "nt": (((1,), (1,)), ((), ())), "tn": (((0,), (0,)), ((), ()))}[mode]
    n_in = len(args)
    one_step = sr * nk == 1

    def body(*refs):
        a_ref, b_ref = refs[0], refs[1]
        bias_ref = refs[2] if bias is not None else None
        o_ref = refs[n_in]

        def finish(r):
            if bias is not None:
                r = r + bias_ref[...]
            o_ref[...] = r.astype(o_ref.dtype)

        part = lax.dot_general(a_ref[...].astype(BF16), b_ref[...].astype(BF16), dims, preferred_element_type=F32)
        if one_step:
            finish(part)
            return
        acc = refs[n_in + 1]
        isr = pl.program_id(3)
        k = pl.program_id(4)

        @pl.when((isr == 0) & (k == 0))
        def _():
            acc[...] = part

        @pl.when((isr > 0) | (k > 0))
        def _():
            acc[...] += part

        @pl.when((isr == sr - 1) & (k == nk - 1))
        def _():
            finish(acc[...])

    return pl.pallas_call(
        body,
        out_shape=out_shape,
        grid=grid,
        in_specs=in_specs,
        out_specs=pl.BlockSpec(lead_blk(out_lead, batch, (tm, tn)), o_map),
        scratch_shapes=[] if one_step else [pltpu.VMEM((tm, tn), F32)],
        input_output_aliases=aliases,
        compiler_params=_params("parallel", "parallel", "parallel", "arbitrary", "arbitrary"),
        name=name,
    )(*args)


def norm_mod_fwd(x, nw, sc, sh, *, name):
    L, D = x.shape
    tl = _pick(L, 512, 16)

    def body(x_ref, nw_ref, sc_ref, sh_ref, h_ref):
        xv = x_ref[...]
        r = lax.rsqrt(jnp.mean(xv * xv, axis=-1, keepdims=True) + EPS)
        n = (xv * r) * nw_ref[...]
        h_ref[...] = (n * (1.0 + sc_ref[...]) + sh_ref[...]).astype(h_ref.dtype)

    row = pl.BlockSpec((1, D), lambda i: (0, 0))
    return pl.pallas_call(
        body,
        out_shape=jax.ShapeDtypeStruct((L, D), BF16),
        grid=(L // tl,),
        in_specs=[pl.BlockSpec((tl, D), lambda i: (i, 0)), row, row, row],
        out_specs=pl.BlockSpec((tl, D), lambda i: (i, 0)),
        compiler_params=_params("parallel"),
        name=name,
    )(x, nw, sc, sh)


def norm_mod_bwd(x, nw, sc, dh, dx_in, *, name):
    L, D = x.shape
    tl = _pick(L, 512, 16)

    def body(x_ref, nw_ref, sc_ref, dh_ref, dxi_ref, dx_ref, dnw_ref, dsc_ref, dsh_ref):
        @pl.when(pl.program_id(0) == 0)
        def _():
            dnw_ref[...] = jnp.zeros_like(dnw_ref)
            dsc_ref[...] = jnp.zeros_like(dsc_ref)
            dsh_ref[...] = jnp.zeros_like(dsh_ref)

        xv = x_ref[...]
        dh_v = dh_ref[...]
        r = lax.rsqrt(jnp.mean(xv * xv, axis=-1, keepdims=True) + EPS)
        xhat = xv * r
        nw_v = nw_ref[...]
        n = xhat * nw_v
        dsh_ref[...] += jnp.sum(dh_v, axis=0, keepdims=True)
        dsc_ref[...] += jnp.sum(dh_v * n, axis=0, keepdims=True)
        dn = dh_v * (1.0 + sc_ref[...])
        dnw_ref[...] += jnp.sum(dn * xhat, axis=0, keepdims=True)
        dxhat = dn * nw_v
        dx_ref[...] = dxi_ref[...] + r * (dxhat - xhat * jnp.mean(dxhat * xhat, axis=-1, keepdims=True))

    row = pl.BlockSpec((1, D), lambda i: (0, 0))
    tile = pl.BlockSpec((tl, D), lambda i: (i, 0))
    vec = jax.ShapeDtypeStruct((1, D), F32)
    return pl.pallas_call(
        body,
        out_shape=(jax.ShapeDtypeStruct((L, D), F32), vec, vec, vec),
        grid=(L // tl,),
        in_specs=[tile, row, row, tile, tile],
        out_specs=(tile, row, row, row),
        compiler_params=_params("arbitrary"),
        name=name,
    )(x, nw, sc, dh, dx_in)


def gate_fwd(x, f, gate, scale, *, name):
    L, D = x.shape
    tl = _pick(L, 512, 8)

    def body(x_ref, f_ref, g_ref, o_ref):
        o_ref[...] = x_ref[...] + (scale * g_ref[...]) * f_ref[...]

    tile = pl.BlockSpec((tl, D), lambda i: (i, 0))
    return pl.pallas_call(
        body,
        out_shape=jax.ShapeDtypeStruct((L, D), F32),
        grid=(L // tl,),
        in_specs=[tile, tile, pl.BlockSpec((1, D), lambda i: (0, 0))],
        out_specs=tile,
        compiler_params=_params("parallel"),
        name=name,
    )(x, f, gate)


def gate_bwd(f, dx, gate, scale, *, name):
    L, D = f.shape
    tl = _pick(L, 512, 16)

    def body(f_ref, dx_ref, g_ref, df_ref, dg_ref, dfsum_ref):
        @pl.when(pl.program_id(0) == 0)
        def _():
            dg_ref[...] = jnp.zeros_like(dg_ref)
            dfsum_ref[...] = jnp.zeros_like(dfsum_ref)

        dxv = dx_ref[...]
        df = (scale * g_ref[...]) * dxv
        df_ref[...] = df.astype(df_ref.dtype)
        dfsum_ref[...] += jnp.sum(df, axis=0, keepdims=True)
        dg_ref[...] += scale * jnp.sum(f_ref[...] * dxv, axis=0, keepdims=True)

    tile = pl.BlockSpec((tl, D), lambda i: (i, 0))
    row = pl.BlockSpec((1, D), lambda i: (0, 0))
    vec = jax.ShapeDtypeStruct((1, D), F32)
    return pl.pallas_call(
        body,
        out_shape=(jax.ShapeDtypeStruct((L, D), BF16), vec, vec),
        grid=(L // tl,),
        in_specs=[tile, tile, row],
        out_specs=(tile, row, row),
        compiler_params=_params("arbitrary"),
        name=name,
    )(f, dx, gate)


def colsum(x, *, name):
    L, N = x.shape
    tl = _pick(L, 512, 8)

    def body(x_ref, o_ref):
        @pl.when(pl.program_id(0) == 0)
        def _():
            o_ref[...] = jnp.zeros_like(o_ref)

        o_ref[...] += jnp.sum(x_ref[...], axis=0, keepdims=True)

    return pl.pallas_call(
        body,
        out_shape=jax.ShapeDtypeStruct((1, N), F32),
        grid=(L // tl,),
        in_specs=[pl.BlockSpec((tl, N), lambda i: (i, 0))],
        out_specs=pl.BlockSpec((1, N), lambda i: (0, 0)),
        compiler_params=_params("arbitrary"),
        name=name,
    )(x)


def ffn_up(h, wgu, *, name):
    L, D = h.shape
    T = wgu.shape[-1]
    tm = _pick(L, 512, 16)

    def body(h_ref, w_ref, gu_ref, a_ref):
        hb = h_ref[...].astype(BF16)
        g = jnp.dot(hb, w_ref[0].astype(BF16), preferred_element_type=F32)
        u = jnp.dot(hb, w_ref[1].astype(BF16), preferred_element_type=F32)
        gu_ref[0] = g
        gu_ref[1] = u
        a_ref[...] = (_silu(g) * u).astype(a_ref.dtype)

    gu, a = pl.pallas_call(
        body,
        out_shape=(jax.ShapeDtypeStruct((2, 2, L, T), F32), jax.ShapeDtypeStruct((2, L, T), BF16)),
        grid=(2, L // tm),
        in_specs=[
            pl.BlockSpec((tm, D), lambda j, i: (i, 0)),
            pl.BlockSpec((2, None, D, T), lambda j, i: (0, j, 0, 0)),
        ],
        out_specs=(
            pl.BlockSpec((2, None, tm, T), lambda j, i: (0, j, i, 0)),
            pl.BlockSpec((None, tm, T), lambda j, i: (j, i, 0)),
        ),
        compiler_params=_params("parallel", "parallel"),
        name=name,
    )(h, wgu.reshape(2, 2, D, T))
    return gu.reshape(4, L, T), a


def ffn_down_bwd(df, wdn, gu, *, name):
    L, D = df.shape
    T = wdn.shape[1]
    tm = _pick(L, 512, 16)

    def body(df_ref, w_ref, gu_ref, d_ref):
        da = lax.dot_general(
            df_ref[...].astype(BF16), w_ref[...].astype(BF16), (((1,), (1,)), ((), ())), preferred_element_type=F32
        )
        g = gu_ref[0]
        d_ref[0] = (da * gu_ref[1] * _dsilu(g)).astype(d_ref.dtype)
        d_ref[1] = (da * _silu(g)).astype(d_ref.dtype)

    out = pl.pallas_call(
        body,
        out_shape=jax.ShapeDtypeStruct((2, 2, L, T), BF16),
        grid=(2, L // tm),
        in_specs=[
            pl.BlockSpec((tm, D), lambda j, i: (i, 0)),
            pl.BlockSpec((None, T, D), lambda j, i: (j, 0, 0)),
            pl.BlockSpec((2, None, tm, T), lambda j, i: (0, j, i, 0)),
        ],
        out_specs=pl.BlockSpec((2, None, tm, T), lambda j, i: (0, j, i, 0)),
        compiler_params=_params("parallel", "parallel"),
        name=name,
    )(df, wdn, gu.reshape(2, 2, L, T))
    return out.reshape(4, L, T)


def _shift_down(u, k, rows):
    if k == 0:
        return u
    return jnp.where(rows >= k, pltpu.roll(u, k, 0), 0.0)


def _shift_up(u, k, rows, n):
    if k == 0:
        return u
    return jnp.where(rows < n - k, pltpu.roll(u, n - k, 0), 0.0)


def _conv_pre(u, w_ref, b_ref, rows):
    pre = b_ref[...] + w_ref[CONV_WIDTH - 1 : CONV_WIDTH, :] * u
    for k in range(1, CONV_WIDTH):
        pre = pre + w_ref[CONV_WIDTH - 1 - k : CONV_WIDTH - k, :] * _shift_down(u, k, rows)
    return pre


def conv_fwd(zx, conv_w, conv_b, d_inner, *, name):
    L = zx.shape[0]
    C = conv_w.shape[1]
    tc = 256
    off = d_inner // tc

    def body(u_ref, w_ref, b_ref, o_ref):
        rows = lax.broadcasted_iota(jnp.int32, (L, tc), 0)
        o_ref[...] = _silu(_conv_pre(u_ref[...], w_ref, b_ref, rows))

    return pl.pallas_call(
        body,
        out_shape=jax.ShapeDtypeStruct((L, C), F32),
        grid=(C // tc,),
        in_specs=[
            pl.BlockSpec((L, tc), lambda j: (0, off + j)),
            pl.BlockSpec((CONV_WIDTH, tc), lambda j: (0, j)),
            pl.BlockSpec((1, tc), lambda j: (0, j)),
        ],
        out_specs=pl.BlockSpec((L, tc), lambda j: (0, j)),
        compiler_params=_params("parallel"),
        name=name,
    )(zx, conv_w, conv_b)


def conv_bwd(zx, conv_w, conv_b, dxbc, d_inner, *, name):
    L = zx.shape[0]
    C = conv_w.shape[1]
    tc = 256
    off = d_inner // tc

    def body(u_ref, w_ref, b_ref, d_ref, du_ref, dw_ref, db_ref):
        rows = lax.broadcasted_iota(jnp.int32, (L, tc), 0)
        u = u_ref[...]
        dpre = d_ref[...] * _dsilu(_conv_pre(u, w_ref, b_ref, rows))
        db_ref[...] = jnp.sum(dpre, axis=0, keepdims=True)
        du = w_ref[CONV_WIDTH - 1 : CONV_WIDTH, :] * dpre
        dw_ref[CONV_WIDTH - 1 : CONV_WIDTH, :] = jnp.sum(dpre * u, axis=0, keepdims=True)
        for k in range(1, CONV_WIDTH):
            j = CONV_WIDTH - 1 - k
            dw_ref[j : j + 1, :] = jnp.sum(dpre * _shift_down(u, k, rows), axis=0, keepdims=True)
            du = du + w_ref[j : j + 1, :] * _shift_up(dpre, k, rows, L)
        du_ref[...] = du

    return pl.pallas_call(
        body,
        out_shape=(
            jax.ShapeDtypeStruct((L, C), F32),
            jax.ShapeDtypeStruct((CONV_WIDTH, C), F32),
            jax.ShapeDtypeStruct((1, C), F32),
        ),
        grid=(C // tc,),
        in_specs=[
            pl.BlockSpec((L, tc), lambda j: (0, off + j)),
            pl.BlockSpec((CONV_WIDTH, tc), lambda j: (0, j)),
            pl.BlockSpec((1, tc), lambda j: (0, j)),
            pl.BlockSpec((L, tc), lambda j: (0, j)),
        ],
        out_specs=(
            pl.BlockSpec((L, tc), lambda j: (0, j)),
            pl.BlockSpec((CONV_WIDTH, tc), lambda j: (0, j)),
            pl.BlockSpec((1, tc), lambda j: (0, j)),
        ),
        compiler_params=_params("parallel"),
        name=name,
    )(zx, conv_w, conv_b, dxbc)


def _ssd_head(xs, dt, acs, tot, dsk, cb, bm, cm, prev):
    q = xs.shape[0]
    li = lax.broadcasted_iota(jnp.int32, (q, q), 0)
    si = lax.broadcasted_iota(jnp.int32, (q, q), 1)
    causal = li >= si
    lmat = jnp.exp(jnp.where(causal, acs - acs.T, -jnp.inf))
    xdt = xs * dt
    y = jnp.dot((cb * lmat).astype(BF16), xdt.astype(BF16), preferred_element_type=F32)
    y = y + lax.dot_general(
        (cm * jnp.exp(acs)).astype(BF16), prev.astype(BF16), (((1,), (1,)), ((), ())), preferred_element_type=F32
    )
    y = y + dsk * xs
    st = lax.dot_general(
        xdt.astype(BF16), (bm * jnp.exp(tot - acs)).astype(BF16), (((0,), (0,)), ((), ())), preferred_element_type=F32
    )
    return y, prev * jnp.exp(tot) + st


def _pick_lane(v, h):
    lanes = lax.broadcasted_iota(jnp.int32, v.shape, 1)
    return jnp.sum(jnp.where(lanes == h, v, 0.0), axis=1, keepdims=True)


def _tri_cols(cols, upper):
    q = cols[0].shape[0]
    assert 3 * len(cols) <= LANE
    li = lax.broadcasted_iota(jnp.int32, (q, q), 0)
    si = lax.broadcasted_iota(jnp.int32, (q, q), 1)
    tri = ((li <= si) if upper else (li >= si)).astype(BF16)
    lanes = lax.broadcasted_iota(jnp.int32, (q, LANE), 1)
    rhs = jnp.zeros((q, LANE), F32)
    for r, col in enumerate(cols):
        hi = col.astype(BF16).astype(F32)
        mid = (col - hi).astype(BF16).astype(F32)
        lo = col - hi - mid
        for t, term in enumerate((hi, mid, lo)):
            rhs = jnp.where(lanes == 3 * r + t, term, rhs)
    out = jnp.dot(tri, rhs.astype(BF16), preferred_element_type=F32)
    return [jnp.sum(jnp.where((lanes >= 3 * r) & (lanes < 3 * r + 3), out, 0.0), axis=1, keepdims=True)
            for r in range(len(cols))]


def _softplus(x):
    return jnp.maximum(x, 0.0) + jnp.log(1.0 + jnp.exp(-jnp.abs(x)))


def _ssd_specs(L, d_inner, H, nc, rev):
    R = H // SSM_GROUPS
    P, N, Q = SSM_HEADDIM, SSM_STATE, CHUNK
    ngrp = SSM_GROUPS

    def ci(c):
        return (nc - 1 - c) if rev else c

    b_off = d_inner // N
    c_off = b_off + ngrp
    xs = pl.BlockSpec((Q, R * P), lambda c, g: (ci(c), g))
    bm = pl.BlockSpec((Q, N), lambda c, g: (ci(c), b_off + g))
    cm = pl.BlockSpec((Q, N), lambda c, g: (ci(c), c_off + g))
    dt = pl.BlockSpec((Q, H), lambda c, g: (ci(c), 0))
    hv = pl.BlockSpec((1, H), lambda c, g: (0, 0))
    y = pl.BlockSpec((Q, R * P), lambda c, g: (ci(c), g))
    st = pl.BlockSpec((None, R * P, N), lambda c, g: (ci(c), g, 0))
    return R, xs, bm, cm, dt, hv, y, st


def ssd_fwd(xbc, dt_raw, dt_bias, a_log, d_skip, d_inner, *, name):
    L = xbc.shape[0]
    H = dt_raw.shape[1]
    nc = L // CHUNK
    P, N = SSM_HEADDIM, SSM_STATE
    R, xs_s, bm_s, cm_s, dt_s, hv_s, y_s, st_s = _ssd_specs(L, d_inner, H, nc, False)

    def body(xs_ref, bm_ref, cm_ref, dt_ref, bias_ref, alog_ref, dsk_ref, y_ref, st_ref, state):
        c = pl.program_id(0)
        g = pl.program_id(1)

        @pl.when(c == 0)
        def _():
            for r in range(R):
                state[g * R + r] = jnp.zeros((P, N), F32)

        dtb = _softplus(dt_ref[...] + bias_ref[...])
        a_all = -jnp.exp(alog_ref[...])
        bm, cm = bm_ref[...], cm_ref[...]
        cb = lax.dot_general(cm.astype(BF16), bm.astype(BF16), (((1,), (1,)), ((), ())), preferred_element_type=F32)
        dts = [_pick_lane(dtb, g * R + r) for r in range(R)]
        a_cols = [dts[r] * _pick_lane(a_all, g * R + r) for r in range(R)]
        acs = _tri_cols(a_cols, upper=False)
        prevs = [state[g * R + r] for r in range(R)]
        res = []
        for r in range(R):
            res.append(_ssd_head(
                xs_ref[:, r * P : (r + 1) * P],
                dts[r],
                jnp.broadcast_to(acs[r], (CHUNK, CHUNK)),
                jnp.sum(a_cols[r], axis=0, keepdims=True),
                _pick_lane(dsk_ref[...], g * R + r),
                cb,
                bm,
                cm,
                prevs[r],
            ))
        for r in range(R):
            st_ref[r * P : (r + 1) * P, :] = prevs[r]
            y_ref[:, r * P : (r + 1) * P] = res[r][0]
            state[g * R + r] = res[r][1]

    return pl.pallas_call(
        body,
        out_shape=(jax.ShapeDtypeStruct((L, d_inner), F32), jax.ShapeDtypeStruct((nc, H * P, N), F32)),
        grid=(nc, SSM_GROUPS),
        in_specs=[xs_s, bm_s, cm_s, dt_s, hv_s, hv_s, hv_s],
        out_specs=(y_s, st_s),
        scratch_shapes=[pltpu.VMEM((H, P, N), F32)],
        compiler_params=_params("arbitrary", "arbitrary"),
        name=name,
    )(xbc, xbc, xbc, dt_raw, dt_bias, a_log, d_skip)


def ssd_bwd(xbc, dt_raw, dt_bias, a_log, d_skip, states, dy, d_inner, *, name):
    L, C = xbc.shape
    H = dt_raw.shape[1]
    nc = L // CHUNK
    P, N, Q = SSM_HEADDIM, SSM_STATE, CHUNK
    R, xs_s, bm_s, cm_s, dt_s, hv_s, y_s, st_s = _ssd_specs(L, d_inner, H, nc, True)

    def body(xs_ref, bm_ref, cm_ref, dt_ref, bias_ref, alog_ref, dsk_ref, st_ref, dy_ref,
             dxs_ref, dbm_ref, dcm_ref, ddt_ref, dbias_ref, dalog_ref, ddsk_ref, dstate):
        c = pl.program_id(0)
        g = pl.program_id(1)

        @pl.when(c == 0)
        def _():
            for r in range(R):
                dstate[g * R + r] = jnp.zeros((P, N), F32)

        @pl.when((c == 0) & (g == 0))
        def _():
            dbias_ref[...] = jnp.zeros_like(dbias_ref)
            dalog_ref[...] = jnp.zeros_like(dalog_ref)
            ddsk_ref[...] = jnp.zeros_like(ddsk_ref)

        @pl.when(g == 0)
        def _():
            ddt_ref[...] = jnp.zeros_like(ddt_ref)

        pre = dt_ref[...] + bias_ref[...]
        dtb = _softplus(pre)
        a_all = -jnp.exp(alog_ref[...])
        lanes_q = lax.broadcasted_iota(jnp.int32, (Q, H), 1)
        lanes_1 = lax.broadcasted_iota(jnp.int32, (1, H), 1)
        bm = bm_ref[...]
        cm = cm_ref[...]
        nt = (((1,), (1,)), ((), ()))
        cb = lax.dot_general(cm.astype(BF16), bm.astype(BF16), nt, preferred_element_type=F32)
        dts = [_pick_lane(dtb, g * R + r) for r in range(R)]
        a_negs = [_pick_lane(a_all, g * R + r) for r in range(R)]
        a_cols = [dts[r] * a_negs[r] for r in range(R)]
        acs = _tri_cols(a_cols, upper=False)
        dbm = jnp.zeros((Q, N), F32)
        dcm = jnp.zeros((Q, N), F32)
        dcb = jnp.zeros((Q, Q), F32)
        dd_row = jnp.zeros((1, H), F32)
        dstates = [dstate[g * R + r] for r in range(R)]
        dprevs, ddts, dacs_cols, dtots = [], [], [], []
        for r in range(R):
            h = g * R + r
            args = (
                xs_ref[:, r * P : (r + 1) * P],
                dts[r],
                jnp.broadcast_to(acs[r], (Q, Q)),
                jnp.sum(a_cols[r], axis=0, keepdims=True),
                _pick_lane(dsk_ref[...], h),
                cb,
                bm,
                cm,
                st_ref[r * P : (r + 1) * P, :],
            )
            _, vjp = jax.vjp(_ssd_head, *args)
            dxs, ddt, dacs, dtot, dd, dcb_h, dbm_h, dcm_h, dprev = vjp((dy_ref[:, r * P : (r + 1) * P], dstates[r]))
            dxs_ref[:, r * P : (r + 1) * P] = dxs
            dprevs.append(dprev)
            ddts.append(ddt)
            dacs_cols.append(jnp.sum(dacs, axis=1, keepdims=True))
            dtots.append(dtot)
            dbm = dbm + dbm_h
            dcm = dcm + dcm_h
            dcb = dcb + dcb_h
            dd_row = dd_row + jnp.where(lanes_1 == h, dd, 0.0)
        for r in range(R):
            dstate[g * R + r] = dprevs[r]
        ddt_blk = jnp.zeros((Q, H), F32)
        da_row = jnp.zeros((1, H), F32)
        for r, da_col in enumerate(_tri_cols(dacs_cols, upper=True)):
            h = g * R + r
            da_col = da_col + dtots[r]
            ddt_blk = ddt_blk + jnp.where(lanes_q == h, ddts[r] + da_col * a_negs[r], 0.0)
            da_row = da_row + jnp.where(lanes_1 == h, jnp.sum(da_col * dts[r], axis=0, keepdims=True), 0.0)
        dcb16 = dcb.astype(BF16)
        dbm_ref[...] = dbm + lax.dot_general(dcb16, cm.astype(BF16), (((0,), (0,)), ((), ())), preferred_element_type=F32)
        dcm_ref[...] = dcm + jnp.dot(dcb16, bm.astype(BF16), preferred_element_type=F32)
        ddt_pre = ddt_blk * _sigmoid(pre)
        ddt_ref[...] += ddt_pre
        dbias_ref[...] += jnp.sum(ddt_pre, axis=0, keepdims=True)
        dalog_ref[...] += da_row * a_all
        ddsk_ref[...] += dd_row

    ngrp = SSM_GROUPS
    hrow = jax.ShapeDtypeStruct((1, H), F32)
    dxs, dbm, dcm, ddt, dbias, dalog, ddsk = pl.pallas_call(
        body,
        out_shape=(
            jax.ShapeDtypeStruct((L, d_inner), F32),
            jax.ShapeDtypeStruct((L, ngrp * N), F32),
            jax.ShapeDtypeStruct((L, ngrp * N), F32),
            jax.ShapeDtypeStruct((L, H), F32),
            hrow,
            hrow,
            hrow,
        ),
        grid=(nc, ngrp),
        in_specs=[xs_s, bm_s, cm_s, dt_s, hv_s, hv_s, hv_s, st_s, y_s],
        out_specs=(
            y_s,
            pl.BlockSpec((Q, N), lambda c, g: (nc - 1 - c, g)),
            pl.BlockSpec((Q, N), lambda c, g: (nc - 1 - c, g)),
            dt_s,
            hv_s,
            hv_s,
            hv_s,
        ),
        scratch_shapes=[pltpu.VMEM((H, P, N), F32)],
        compiler_params=_params("arbitrary", "arbitrary"),
        name=name,
    )(xbc, xbc, xbc, dt_raw, dt_bias, a_log, d_skip, states, dy)
    return jnp.concatenate([dxs, dbm, dcm], axis=1), ddt, dbias, dalog, ddsk


def gnorm_fwd(y, zx, nw, *, name):
    L, DI = y.shape
    gw = DI // SSM_GROUPS
    tl = _pick(L, 512, 16)

    def body(y_ref, z_ref, nw_ref, o_ref):
        yz = y_ref[...] * _silu(z_ref[...])
        r = lax.rsqrt(jnp.mean(yz * yz, axis=-1, keepdims=True) + EPS)
        o_ref[...] = ((yz * r) * nw_ref[...]).astype(o_ref.dtype)

    tile = pl.BlockSpec((tl, gw), lambda i, g: (i, g))
    return pl.pallas_call(
        body,
        out_shape=jax.ShapeDtypeStruct((L, DI), BF16),
        grid=(L // tl, SSM_GROUPS),
        in_specs=[tile, tile, pl.BlockSpec((1, gw), lambda i, g: (0, g))],
        out_specs=tile,
        compiler_params=_params("parallel", "parallel"),
        name=name,
    )(y, zx, nw)


def gnorm_bwd(y, zx, nw, dout, *, name):
    L, DI = y.shape
    gw = DI // SSM_GROUPS
    tl = _pick(L, 512, 16)

    def body(y_ref, z_ref, nw_ref, do_ref, dy_ref, dz_ref, dnw_ref):
        @pl.when(pl.program_id(1) == 0)
        def _():
            dnw_ref[...] = jnp.zeros_like(dnw_ref)

        yv = y_ref[...]
        zv = z_ref[...]
        sz = _silu(zv)
        yz = yv * sz
        r = lax.rsqrt(jnp.mean(yz * yz, axis=-1, keepdims=True) + EPS)
        n = yz * r
        dov = do_ref[...]
        dnw_ref[...] += jnp.sum(dov * n, axis=0, keepdims=True)
        dn = dov * nw_ref[...]
        dyz = r * (dn - n * jnp.mean(dn * n, axis=-1, keepdims=True))
        dy_ref[...] = dyz * sz
        dz_ref[...] = dyz * yv * _dsilu(zv)

    tile = pl.BlockSpec((tl, gw), lambda g, i: (i, g))
    row = pl.BlockSpec((1, gw), lambda g, i: (0, g))
    return pl.pallas_call(
        body,
        out_shape=(
            jax.ShapeDtypeStruct((L, DI), F32),
            jax.ShapeDtypeStruct((L, DI), F32),
            jax.ShapeDtypeStruct((1, DI), F32),
        ),
        grid=(SSM_GROUPS, L // tl),
        in_specs=[tile, tile, row, tile],
        out_specs=(tile, tile, row),
        compiler_params=_params("parallel", "arbitrary"),
        name=name,
    )(y, zx, nw, dout)


def _attn_head(q, kp, kc, vp, vc, sink, has_prev):
    rows, w = q.shape[0], kc.shape[0]
    nt = (((1,), (1,)), ((), ()))
    qb = q.astype(BF16)
    sc = lax.dot_general(qb, kc.astype(BF16), nt, preferred_element_type=F32) * HEAD_DIM ** -0.5
    sp = lax.dot_general(qb, kp.astype(BF16), nt, preferred_element_type=F32) * HEAD_DIM ** -0.5
    ii = jnp.bitwise_and(lax.broadcasted_iota(jnp.int32, (rows, w), 0), w - 1)
    jj = lax.broadcasted_iota(jnp.int32, (rows, w), 1)
    lc = jnp.where(jj <= ii, sc, -jnp.inf)
    lp = jnp.where((jj > ii) & has_prev, sp, -jnp.inf)
    m = jnp.maximum(jnp.maximum(jnp.max(lc, axis=1, keepdims=True), jnp.max(lp, axis=1, keepdims=True)), sink)
    m = lax.stop_gradient(m)
    pc = jnp.exp(lc - m)
    pp = jnp.exp(lp - m)
    denom = jnp.sum(pc, axis=1, keepdims=True) + jnp.sum(pp, axis=1, keepdims=True) + jnp.exp(sink - m)
    o = jnp.dot((pc / denom).astype(BF16), vc.astype(BF16), preferred_element_type=F32)
    return o + jnp.dot((pp / denom).astype(BF16), vp.astype(BF16), preferred_element_type=F32)


def attn_fwd(q, kv, sinks, *, name):
    L, DQ = q.shape
    heads = DQ // HEAD_DIM
    rep = heads // KV_HEADS
    nb = L // WINDOW
    kw = KV_HEADS * HEAD_DIM
    W, HD = WINDOW, HEAD_DIM

    def body(q_ref, kp_ref, kc_ref, vp_ref, vc_ref, s_ref, o_ref):
        has_prev = pl.program_id(0) > 0
        for kh in range(KV_HEADS):
            ks = slice(kh * HD, (kh + 1) * HD)
            hs = [kh * rep + rr for rr in range(rep)]
            o = _attn_head(
                jnp.concatenate([q_ref[:, h * HD : (h + 1) * HD] for h in hs], axis=0),
                kp_ref[:, ks], kc_ref[:, ks], vp_ref[:, ks], vc_ref[:, ks],
                jnp.concatenate([jnp.broadcast_to(s_ref[:, h : h + 1], (W, 1)) for h in hs], axis=0), has_prev,
            )
            for rr, h in enumerate(hs):
                o_ref[:, h * HD : (h + 1) * HD] = o[rr * W : (rr + 1) * W].astype(o_ref.dtype)

    return pl.pallas_call(
        body,
        out_shape=jax.ShapeDtypeStruct((L, DQ), BF16),
        grid=(nb,),
        in_specs=[
            pl.BlockSpec((W, DQ), lambda n: (n, 0)),
            pl.BlockSpec((W, kw), lambda n: (jnp.maximum(n - 1, 0), 0)),
            pl.BlockSpec((W, kw), lambda n: (n, 0)),
            pl.BlockSpec((W, kw), lambda n: (jnp.maximum(n - 1, 0), 1)),
            pl.BlockSpec((W, kw), lambda n: (n, 1)),
            pl.BlockSpec((1, heads), lambda n: (0, 0)),
        ],
        out_specs=pl.BlockSpec((W, DQ), lambda n: (n, 0)),
        compiler_params=_params("parallel"),
        name=name,
    )(q, kv, kv, kv, kv, sinks)


def attn_bwd(q, kv, sinks, do, dkv_in, *, name):
    L, DQ = q.shape
    heads = DQ // HEAD_DIM
    rep = heads // KV_HEADS
    nb = L // WINDOW
    kw = KV_HEADS * HEAD_DIM
    W, HD = WINDOW, HEAD_DIM

    def blk(n):
        return jnp.minimum(n, nb - 1)

    def prev(n):
        return jnp.maximum(blk(n) - 1, 0)

    def outb(n):
        return jnp.maximum(n - 1, 0)

    def body(q_ref, kp_ref, kc_ref, vp_ref, vc_ref, s_ref, do_ref, dki_ref, dvi_ref,
             dq_ref, dk_ref, dv_ref, ds_ref, dk_cur, dv_cur):
        n = pl.program_id(0)
        has_prev = n > 0

        @pl.when(n == 0)
        def _():
            ds_ref[...] = jnp.zeros_like(ds_ref)
            dk_cur[...] = jnp.zeros_like(dk_cur)
            dv_cur[...] = jnp.zeros_like(dv_cur)

        @pl.when(n == nb)
        def _():
            dk_ref[...] = dki_ref[...] + dk_cur[...]
            dv_ref[...] = dvi_ref[...] + dv_cur[...]

        @pl.when(n < nb)
        def _():
            lanes = lax.broadcasted_iota(jnp.int32, (1, heads), 1)
            ds_row = jnp.zeros((1, heads), F32)
            for kh in range(KV_HEADS):
                ks = slice(kh * HD, (kh + 1) * HD)
                hs = [kh * rep + rr for rr in range(rep)]
                _, vjp = jax.vjp(
                    functools.partial(_attn_head, has_prev=has_prev),
                    jnp.concatenate([q_ref[:, h * HD : (h + 1) * HD] for h in hs], axis=0),
                    kp_ref[:, ks], kc_ref[:, ks], vp_ref[:, ks], vc_ref[:, ks],
                    jnp.concatenate([jnp.broadcast_to(s_ref[:, h : h + 1], (W, 1)) for h in hs], axis=0),
                )
                dq, dkp, dkc, dvp, dvc, dsk = vjp(
                    jnp.concatenate([do_ref[:, h * HD : (h + 1) * HD] for h in hs], axis=0))
                for rr, h in enumerate(hs):
                    dq_ref[:, h * HD : (h + 1) * HD] = dq[rr * W : (rr + 1) * W]
                    ds_row = ds_row + jnp.where(lanes == h, jnp.sum(dsk[rr * W : (rr + 1) * W], axis=0, keepdims=True), 0.0)
                dk_ref[:, ks] = dki_ref[:, ks] + dk_cur[:, ks] + dkp
                dv_ref[:, ks] = dvi_ref[:, ks] + dv_cur[:, ks] + dvp
                dk_cur[:, ks] = dkc
                dv_cur[:, ks] = dvc
            ds_ref[...] += ds_row

    dq, dk, dv, ds = pl.pallas_call(
        body,
        out_shape=(
            jax.ShapeDtypeStruct((L, DQ), F32),
            jax.ShapeDtypeStruct((L, kw), F32),
            jax.ShapeDtypeStruct((L, kw), F32),
            jax.ShapeDtypeStruct((1, heads), F32),
        ),
        grid=(nb + 1,),
        in_specs=[
            pl.BlockSpec((W, DQ), lambda n: (blk(n), 0)),
            pl.BlockSpec((W, kw), lambda n: (prev(n), 0)),
            pl.BlockSpec((W, kw), lambda n: (blk(n), 0)),
            pl.BlockSpec((W, kw), lambda n: (prev(n), 1)),
            pl.BlockSpec((W, kw), lambda n: (blk(n), 1)),
            pl.BlockSpec((1, heads), lambda n: (0, 0)),
            pl.BlockSpec((W, DQ), lambda n: (blk(n), 0)),
            pl.BlockSpec((W, kw), lambda n: (outb(n), 0)),
            pl.BlockSpec((W, kw), lambda n: (outb(n), 1)),
        ],
        out_specs=(
            pl.BlockSpec((W, DQ), lambda n: (blk(n), 0)),
            pl.BlockSpec((W, kw), lambda n: (outb(n), 0)),
            pl.BlockSpec((W, kw), lambda n: (outb(n), 0)),
            pl.BlockSpec((1, heads), lambda n: (0, 0)),
        ),
        scratch_shapes=[pltpu.VMEM((W, kw), F32), pltpu.VMEM((W, kw), F32)],
        compiler_params=_params("arbitrary"),
        name=name,
    )(q, kv, kv, kv, kv, sinks, do, dkv_in, dkv_in)
    return dq, jnp.concatenate([dk, dv], axis=1), ds


def final_loss(x, fw, target, *, name):
    L, D = x.shape
    tl = _pick(L, 512, 8)

    def body(x_ref, fw_ref, t_ref, loss_ref, dx_ref, dfw_ref):
        @pl.when(pl.program_id(0) == 0)
        def _():
            loss_ref[...] = jnp.zeros_like(loss_ref)
            dfw_ref[...] = jnp.zeros_like(dfw_ref)

        xv = x_ref[...]
        fwv = fw_ref[...]
        r = lax.rsqrt(jnp.mean(xv * xv, axis=-1, keepdims=True) + EPS)
        xhat = xv * r
        err = xhat * fwv - t_ref[...]
        loss_ref[...] += 0.5 * jnp.sum(jnp.mean(err * err, axis=-1, keepdims=True), axis=0, keepdims=True)
        dy = err * (1.0 / D)
        dfw_ref[...] += jnp.sum(dy * xhat, axis=0, keepdims=True)
        dxhat = dy * fwv
        dx_ref[...] = r * (dxhat - xhat * jnp.mean(dxhat * xhat, axis=-1, keepdims=True))

    tile = pl.BlockSpec((tl, D), lambda i: (i, 0))
    row = pl.BlockSpec((1, D), lambda i: (0, 0))
    return pl.pallas_call(
        body,
        out_shape=(
            jax.ShapeDtypeStruct((1, 1), F32),
            jax.ShapeDtypeStruct((L, D), F32),
            jax.ShapeDtypeStruct((1, D), F32),
        ),
        grid=(L // tl,),
        in_specs=[tile, row, tile],
        out_specs=(pl.BlockSpec((1, 1), lambda i: (0, 0)), tile, row),
        compiler_params=_params("arbitrary"),
        name=name,
    )(x, fw, target)


def outer8(ct, d, *, name):
    D, B = ct.shape
    S, _, N = d.shape
    tm = _pick(D, 512, 8)
    tn = _pick(N, 256, LANE)

    def body(c_ref, d_ref, o_ref):
        acc = c_ref[:, 0:1] * d_ref[0:1, :]
        for b in range(1, B):
            acc = acc + c_ref[:, b : b + 1] * d_ref[b : b + 1, :]
        o_ref[...] = acc

    return pl.pallas_call(
        body,
        out_shape=jax.ShapeDtypeStruct((S, D, N), F32),
        grid=(S, D // tm, N // tn),
        in_specs=[
            pl.BlockSpec((tm, B), lambda s, i, j: (i, 0)),
            pl.BlockSpec((None, B, tn), lambda s, i, j: (s, 0, j)),
        ],
        out_specs=pl.BlockSpec((None, tm, tn), lambda s, i, j: (s, i, j)),
        compiler_params=_params("parallel", "parallel", "parallel"),
        name=name,
    )(ct, d)


def reduce8(g, *, name):
    nd, R, N = g.shape

    def body(g_ref, o_ref):
        acc = g_ref[0]
        for b in range(1, nd):
            acc = acc + g_ref[b]
        o_ref[...] = acc

    return pl.pallas_call(
        body,
        out_shape=jax.ShapeDtypeStruct((R, N), F32),
        name=name,
    )(g)


def _as3(a):
    if a.ndim == 1:
        return a.reshape(1, 1, -1)
    if a.ndim == 2:
        return a.reshape((1,) + a.shape)
    return a.reshape((-1,) + a.shape[-2:])


def adamw(w, g, m, v, *, name):
    shape = w.shape
    w3, g3, m3, v3 = _as3(w), _as3(g), _as3(m), _as3(v)
    B, R, C = w3.shape
    tr = _pick(R, max(8, (1 << 19) // max(C, 1) // 8 * 8), 8)

    def body(w_ref, g_ref, m_ref, v_ref, d_ref, nm_ref, nv_ref):
        gv = g_ref[...]
        mn = ADAM_B1 * m_ref[...] + (1.0 - ADAM_B1) * gv
        vn = ADAM_B2 * v_ref[...] + (1.0 - ADAM_B2) * (gv * gv)
        m_hat = mn / (1.0 - ADAM_B1 ** ADAM_STEP)
        v_hat = vn / (1.0 - ADAM_B2 ** ADAM_STEP)
        d_ref[...] = -ADAM_LR * (m_hat / (jnp.sqrt(v_hat) + ADAM_EPS) + ADAM_WD * w_ref[...])
        nm_ref[...] = mn
        nv_ref[...] = vn

    tile = pl.BlockSpec((None, tr, C), lambda b, i: (b, i, 0))
    sds = jax.ShapeDtypeStruct((B, R, C), F32)
    d, nm, nv = pl.pallas_call(
        body,
        out_shape=(sds, sds, sds),
        grid=(B, R // tr),
        in_specs=[tile, tile, tile, tile],
        out_specs=(tile, tile, tile),
        compiler_params=_params("parallel", "parallel"),
        name=name,
    )(w3, g3, m3, v3)
    return d.reshape(shape), nm.reshape(shape), nv.reshape(shape)


def _place():
    return lax.axis_index("x"), lax.axis_index("y"), lax.axis_index("c")


def _flip(v, bit):
    return (1 - v) if bit else v


def ag8(v, *, act=None, name):
    R, N = v.shape

    def body(v_ref, out_ref, stage, send_sems, recv_sems):
        x, y, c = _place()
        me = 4 * x + 2 * y + c
        val = v_ref[...]
        if act is not None:
            val = act(val)
        stage[...] = val
        out_ref[me] = val
        sends = []
        for k in range(1, N_DEV):
            px, py, pc = _flip(x, k & 4), _flip(y, k & 2), _flip(c, k & 1)
            cp = pltpu.make_async_remote_copy(
                src_ref=stage, dst_ref=out_ref.at[me], send_sem=send_sems.at[k - 1], recv_sem=recv_sems.at[k - 1],
                device_id=(px, py, pc), device_id_type=MESH,
            )
            cp.start()
            sends.append(cp)
        for k in range(1, N_DEV):
            px, py, pc = _flip(x, k & 4), _flip(y, k & 2), _flip(c, k & 1)
            pltpu.make_async_remote_copy(
                src_ref=stage, dst_ref=out_ref.at[4 * px + 2 * py + pc], send_sem=send_sems.at[k - 1],
                recv_sem=recv_sems.at[k - 1], device_id=(px, py, pc), device_id_type=MESH,
            ).wait_recv()
        for cp in sends:
            cp.wait_send()

    return pl.pallas_call(
        body,
        out_shape=jax.ShapeDtypeStruct((N_DEV, R, N), F32),
        in_specs=[pl.BlockSpec(memory_space=pltpu.VMEM)],
        out_specs=pl.BlockSpec(memory_space=pltpu.VMEM),
        scratch_shapes=[
            pltpu.VMEM((R, N), F32),
            pltpu.SemaphoreType.DMA((N_DEV - 1,)),
            pltpu.SemaphoreType.DMA((N_DEV - 1,)),
        ],
        name=name,
    )(v)


def _other_chips(x, y):
    chips = [(1 - x, y), (x, 1 - y), (1 - x, 1 - y)]
    return chips, [2 * px + py for px, py in chips]


_HBM = pl.BlockSpec(memory_space=pltpu.HBM)


_SEM = pl.BlockSpec(memory_space=pltpu.SEMAPHORE)
_ANY = pl.BlockSpec(memory_space=pl.ANY)
_EFFECT = pltpu.SideEffectType.DATAFLOW_SIDE_EFFECTING


def _gather_copies(srcs, lands, send_sems, recv_sems):
    x, y, c = _place()
    k_me = 2 * x + y
    chips, _ = _other_chips(x, y)
    cps = []
    for w in range(len(srcs)):
        for j, (px, py) in enumerate(chips):
            cps.append(pltpu.make_async_remote_copy(
                src_ref=srcs[w].at[:, c], dst_ref=lands[w].at[:, k_me, c], send_sem=send_sems.at[3 * w + j],
                recv_sem=recv_sems.at[3 * w + j], device_id=(px, py, c), device_id_type=MESH,
            ))
    return cps


def _reduce_copies(srcs, lands, send_sems, recv_sems):
    x, y, c = _place()
    chips, kidx = _other_chips(x, y)
    cps = []
    for w in range(len(srcs)):
        for j, (px, py) in enumerate(chips):
            cps.append(pltpu.make_async_remote_copy(
                src_ref=srcs[w].at[:, kidx[j]], dst_ref=lands[w].at[j], send_sem=send_sems.at[3 * w + j],
                recv_sem=recv_sems.at[3 * w + j], device_id=(px, py, c), device_id_type=MESH,
            ))
    return cps


def split_start(copies, srcs, land_shapes, after, *, name):
    n = len(srcs)

    def body(*refs):
        src_refs, land_refs = refs[:n], refs[n : 2 * n]
        send_sems, recv_sems = refs[2 * n + 1], refs[2 * n + 2]
        token = refs[-1]
        for cp in copies(src_refs, land_refs, send_sems, recv_sems):
            cp.start()
        token[...] = jnp.zeros_like(token)

    lands = [pltpu.with_memory_space_constraint(lax.empty(sh, s.dtype), pltpu.HBM) for sh, s in zip(land_shapes, srcs)]
    srcs = [pltpu.with_memory_space_constraint(s, pltpu.HBM) for s in srcs]
    out = pl.pallas_call(
        body,
        out_shape=(
            pltpu.SemaphoreType.DMA((3 * n,)), pltpu.SemaphoreType.DMA((3 * n,)),
            *[pltpu.HBM(s.shape, s.dtype) for s in srcs],
            *[pltpu.HBM(sh, s.dtype) for sh, s in zip(land_shapes, srcs)],
            jax.ShapeDtypeStruct((8, LANE), F32),
        ),
        in_specs=[_HBM] * (2 * n) + [_ANY],
        out_specs=(_SEM, _SEM, *([_HBM] * (2 * n)), pl.BlockSpec(memory_space=pltpu.VMEM)),
        input_output_aliases={i: 2 + i for i in range(2 * n)},
        compiler_params=pltpu.CompilerParams(has_side_effects=_EFFECT),
        name=name,
    )(*srcs, *lands, after)
    return out[0], out[1], list(out[2 : 2 + n]), list(out[2 + n : 2 + 2 * n]), out[-1]


def split_wait(copies, send_sems, recv_sems, srcs, lands, after, *, name):
    n = len(srcs)

    def body(*refs):
        src_refs, land_refs = refs[:n], refs[n : 2 * n]
        send_ref, recv_ref = refs[2 * n], refs[2 * n + 1]
        for cp in copies(src_refs, land_refs, send_ref, recv_ref):
            cp.wait_send()
            cp.wait_recv()

    out = pl.pallas_call(
        body,
        out_shape=tuple(pltpu.HBM(a.shape, a.dtype) for a in list(srcs) + list(lands)),
        in_specs=[_HBM] * (2 * n) + [_SEM, _SEM, _ANY],
        out_specs=tuple([_HBM] * (2 * n)),
        input_output_aliases={i: i for i in range(2 * n)},
        compiler_params=pltpu.CompilerParams(has_side_effects=_EFFECT),
        name=name,
    )(*srcs, *lands, send_sems, recv_sems, after)
    return list(out[n:])


def gather_fill(shards, lands, *, name):
    n = len(shards)

    def body(*refs):
        ins = refs[:n]
        outs = refs[2 * n : 3 * n]
        send_sems, recv_sems = refs[3 * n :]
        x, y, c = _place()
        k_me = 2 * x + y
        _, kidx = _other_chips(x, y)
        sib = (x, y, 1 - c)
        cps = []
        for w in range(n):
            cps.append(pltpu.make_async_remote_copy(
                src_ref=ins[w], dst_ref=outs[w].at[:, k_me], send_sem=send_sems.at[4 * w + 3], recv_sem=recv_sems.at[4 * w + 3],
                device_id=sib, device_id_type=MESH,
            ))
            for j in range(3):
                part = outs[w].at[:, kidx[j], c]
                cps.append(pltpu.make_async_remote_copy(
                    src_ref=part, dst_ref=part, send_sem=send_sems.at[4 * w + j], recv_sem=recv_sems.at[4 * w + j],
                    device_id=sib, device_id_type=MESH,
                ))
        for cp in cps:
            cp.start()
        for w in range(n):
            pltpu.make_async_remote_copy(
                src_ref=ins[w], dst_ref=outs[w].at[:, k_me], send_sem=send_sems.at[4 * w + 3], recv_sem=recv_sems.at[4 * w + 3],
                device_id=sib, device_id_type=MESH,
            ).wait_recv()
            for j in range(3):
                got = outs[w].at[:, kidx[j], 1 - c]
                pltpu.make_async_remote_copy(
                    src_ref=got, dst_ref=got, send_sem=send_sems.at[4 * w + j], recv_sem=recv_sems.at[4 * w + j],
                    device_id=sib, device_id_type=MESH,
                ).wait_recv()
        for cp in cps:
            cp.wait_send()

    return list(pl.pallas_call(
        body,
        out_shape=tuple(jax.ShapeDtypeStruct(a.shape, a.dtype) for a in lands),
        in_specs=[_HBM] * (2 * n),
        out_specs=tuple([_HBM] * n),
        scratch_shapes=[pltpu.SemaphoreType.DMA((4 * n,)), pltpu.SemaphoreType.DMA((4 * n,))],
        input_output_aliases={n + w: w for w in range(n)},
        name=name,
    )(*shards, *lands))


def rs_sibling(grads, *, name):
    n = len(grads)

    def body(*refs):
        ins, outs = refs[:n], refs[n : 2 * n]
        send_sems, recv_sems = refs[2 * n :]
        x, y, c = _place()
        cps = []
        for w in range(n):
            cp = pltpu.make_async_remote_copy(
                src_ref=ins[w].at[:, :, 1 - c], dst_ref=outs[w], send_sem=send_sems.at[w], recv_sem=recv_sems.at[w],
                device_id=(x, y, 1 - c), device_id_type=MESH,
            )
            cp.start()
            cps.append(cp)
        for cp in cps:
            cp.wait()

    return list(pl.pallas_call(
        body,
        out_shape=tuple(jax.ShapeDtypeStruct(g.shape[:2] + g.shape[3:], g.dtype) for g in grads),
        in_specs=[_HBM] * n,
        out_specs=tuple([_HBM] * n),
        scratch_shapes=[pltpu.SemaphoreType.DMA((n,)), pltpu.SemaphoreType.DMA((n,))],
        name=name,
    )(*grads))


def rs_share(halves, *, name):
    n = len(halves)

    def body(*refs):
        outs = refs[n : 2 * n]
        send_sems, recv_sems = refs[2 * n :]
        x, y, c = _place()
        cps = []
        for w in range(n):
            cp = pltpu.make_async_remote_copy(
                src_ref=outs[w].at[:, c], dst_ref=outs[w].at[:, c], send_sem=send_sems.at[w], recv_sem=recv_sems.at[w],
                device_id=(x, y, 1 - c), device_id_type=MESH,
            )
            cp.start()
            cps.append(cp)
        for w, cp in enumerate(cps):
            cp.wait_send()
            pltpu.make_async_remote_copy(
                src_ref=outs[w].at[:, c], dst_ref=outs[w].at[:, 1 - c], send_sem=send_sems.at[w], recv_sem=recv_sems.at[w],
                device_id=(x, y, 1 - c), device_id_type=MESH,
            ).wait_recv()

    return pl.pallas_call(
        body,
        out_shape=tuple(jax.ShapeDtypeStruct(h.shape, h.dtype) for h in halves),
        in_specs=[_HBM] * n,
        out_specs=tuple([_HBM] * n),
        scratch_shapes=[pltpu.SemaphoreType.DMA((n,)), pltpu.SemaphoreType.DMA((n,))],
        input_output_aliases={w: w for w in range(n)},
        name=name,
    )(*halves)


def _row_tile(R, C):
    return _pick(R, max(16, (1 << 19) // C // 16 * 16), 16)


def _my_core():
    return lax.axis_index("c")


def _my_chip():
    return 2 * lax.axis_index("x") + lax.axis_index("y")


def rs_add_pair(g, r, *, name):
    M, K, _, R, C = g.shape
    tr = _row_tile(R, C)

    def body(g_ref, r_ref, o_ref):
        o_ref[...] = (g_ref[...].astype(F32) + r_ref[...].astype(F32)).astype(o_ref.dtype)

    blk = pl.BlockSpec((None, None, tr, C), lambda m, k, i: (m, k, i, 0))
    return pl.pallas_call(
        body,
        out_shape=jax.ShapeDtypeStruct((M, K, R, C), BF16),
        grid=(M, K, R // tr),
        in_specs=[pl.BlockSpec((None, None, None, tr, C), lambda m, k, i: (m, k, _my_core(), i, 0)), blk],
        out_specs=blk,
        compiler_params=_params("parallel", "parallel", "parallel"),
        name=name,
    )(g, r)


def rs_add_final(g, r, t, full, m0, *, name):
    M, K, _, R, C = g.shape
    tr = _row_tile(R, C)

    def body(g_ref, r_ref, t_ref, full_ref, o_ref):
        acc = g_ref[...].astype(F32) + r_ref[...].astype(F32)
        for j in range(3):
            acc = acc + t_ref[j].astype(F32)
        o_ref[...] = acc

    return pl.pallas_call(
        body,
        out_shape=jax.ShapeDtypeStruct(full.shape, full.dtype),
        grid=(M, R // tr),
        in_specs=[
            pl.BlockSpec((None, None, None, tr, C), lambda m, i: (m, _my_chip(), _my_core(), i, 0)),
            pl.BlockSpec((None, None, tr, C), lambda m, i: (m, _my_chip(), i, 0)),
            pl.BlockSpec((3, None, tr, C), lambda m, i: (0, m, i, 0)),
            _ANY,
        ],
        out_specs=pl.BlockSpec((None, None, tr, C), lambda m, i: (m0 + m, _my_core(), i, 0)),
        input_output_aliases={3: 0},
        compiler_params=_params("parallel", "parallel"),
        name=name,
    )(g, r, t, full)


WEIGHTS = ["ffn_norm_w", "ffn_w_gu", "ffn_w_down", "mod_w", "mod_b", "mix_norm_w", "ssm_w_in", "ssm_conv_w", "ssm_conv_b",
           "ssm_dt_bias", "ssm_a_log", "ssm_d", "ssm_norm_w", "ssm_w_out", "kv_norm_w", "kv_mod_w", "kv_mod_b", "w_kv", "b_kv",
           "attn_w_q", "attn_b_q", "attn_sinks", "attn_w_o", "attn_b_o", "final_norm_w"]
GATHERED = ["ffn_w_gu", "ffn_w_down", "ssm_w_in", "ssm_w_out", "w_kv", "attn_w_q", "attn_w_o"]
COLUMN_PARALLEL = ["mod_w", "kv_mod_w"]
SMALL_SHARDED = ["ffn_norm_w", "ssm_conv_w", "ssm_conv_b", "ssm_norm_w"]
SMALL = [n for n in WEIGHTS if n not in GATHERED and n not in COLUMN_PARALLEL]


def _row_halves(a):
    a = a.reshape((-1,) + a.shape[-2:])
    return a.reshape(a.shape[0], 2, a.shape[1] // 2, a.shape[2])


def _pack(arrs, rows=8):
    flat = jnp.concatenate([a.reshape(-1) for a in arrs])
    n = flat.shape[0]
    pad = (-n) % (rows * LANE)
    return jnp.pad(flat, (0, pad)).reshape(rows, -1), n


def _unpack(flat, like):
    out, o = [], 0
    for s in like:
        k = 1
        for d in s:
            k *= d
        out.append(flat[o : o + k].reshape(s))
        o += k
    return out


def kernel(x, c, ffn_norm_w, ffn_w_gu, ffn_w_down, mod_w, mod_b, mix_norm_w, ssm_w_in, ssm_conv_w, ssm_conv_b, ssm_dt_bias, ssm_a_log, ssm_d, ssm_norm_w, ssm_w_out, kv_norm_w, kv_mod_w, kv_mod_b, w_kv, b_kv, attn_w_q, attn_b_q, attn_sinks, attn_w_o, attn_b_o, final_norm_w, loss_target, m_ffn_norm_w, m_ffn_w_gu, m_ffn_w_down, m_mod_w, m_mod_b, m_mix_norm_w, m_ssm_w_in, m_ssm_conv_w, m_ssm_conv_b, m_ssm_dt_bias, m_ssm_a_log, m_ssm_d, m_ssm_norm_w, m_ssm_w_out, m_kv_norm_w, m_kv_mod_w, m_kv_mod_b, m_w_kv, m_b_kv, m_attn_w_q, m_attn_b_q, m_attn_sinks, m_attn_w_o, m_attn_b_o, m_final_norm_w, v_ffn_norm_w, v_ffn_w_gu, v_ffn_w_down, v_mod_w, v_mod_b, v_mix_norm_w, v_ssm_w_in, v_ssm_conv_w, v_ssm_conv_b, v_ssm_dt_bias, v_ssm_a_log, v_ssm_d, v_ssm_norm_w, v_ssm_w_out, v_kv_norm_w, v_kv_mod_w, v_kv_mod_b, v_w_kv, v_b_kv, v_attn_w_q, v_attn_b_q, v_attn_sinks, v_attn_w_o, v_attn_b_o, v_final_norm_w):
    env = dict(locals())
    W = {n: env[n] for n in WEIGHTS}
    MOM = {n: env["m_" + n] for n in WEIGHTS}
    VAR = {n: env["v_" + n] for n in WEIGHTS}

    ax, ay, ac = _place()
    kme = 2 * ax + ay
    me = 4 * ax + 2 * ay + ac

    xs = x[0]
    target = loss_target[0]
    L, D = xs.shape
    depth, n_a = ffn_w_gu.shape[0], ssm_w_in.shape[0]
    n_b = depth - n_a
    T = ffn_w_gu.shape[-1]
    DI = ssm_w_out.shape[1] * N_CHIPS
    CI = ssm_w_in.shape[2]
    CC = ssm_conv_w.shape[2] * N_CHIPS
    MW = mod_w.shape[2]
    KW = kv_mod_w.shape[1]
    KVD = w_kv.shape[1]

    def chip_cols(a, width):
        return lax.dynamic_slice_in_dim(a, kme * width, width, axis=a.ndim - 1)

    def ffn_items(i, j):
        return [("ffn_w_gu", 2 * i + j, ffn_w_gu[i, j]), ("ffn_w_down", 2 * i + j, ffn_w_down[i, j])]

    def mix_items(i):
        if i < n_a:
            return [("ssm_w_in", i, ssm_w_in[i]), ("ssm_w_out", i, ssm_w_out[i])]
        return [("attn_w_q", i - n_a, attn_w_q[i - n_a]), ("attn_w_o", i - n_a, attn_w_o[i - n_a])]

    def layer_items(i, order):
        kv_items = [("w_kv", 0, w_kv)] if i == n_a else []
        if order == "fwd":
            return kv_items + ffn_items(i, 0) + mix_items(i) + ffn_items(i, 1)
        return ffn_items(i, 1) + mix_items(i) + ffn_items(i, 0) + kv_items

    fwd_stages = [ffn_items(0, 0), mix_items(0) + ffn_items(0, 1)] + [layer_items(i, "fwd") for i in range(1, depth)]
    bwd_stages = [layer_items(i, "bwd") for i in range(depth - 1, 0, -1)] + [ffn_items(0, 1) + mix_items(0), ffn_items(0, 0)]

    gw, inflight = {}, {}

    def gather_begin(s, after):
        keys = [(n, m0) for n, m0, _ in fwd_stages[s]]
        shards = [_row_halves(a.astype(BF16)) for _, _, a in fwd_stages[s]]
        land_shapes = [(sh.shape[0], N_CHIPS) + sh.shape[1:] for sh in shards]
        ssem, rsem, srcs, lands, token = split_start(_gather_copies, shards, land_shapes, after, name=f"gather_start_{s}")
        inflight[s] = (keys, ssem, rsem, srcs, lands)
        return token[0:1, 0:1]

    def gather_end(s, after):
        keys, ssem, rsem, srcs, lands = inflight.pop(s)
        lands = split_wait(_gather_copies, ssem, rsem, srcs, lands, after, name=f"gather_wait_{s}")
        lands = gather_fill(srcs, lands, name=f"gather_fill_{s}")
        gw.update(zip(keys, lands))
        return lands[0]

    def g_gu(i, j):
        return gw["ffn_w_gu", 2 * i + j].reshape(N_CHIPS, D, T)

    def g_dn(i, j):
        return gw["ffn_w_down", 2 * i + j].reshape(2, T, D)

    def g_full(n, m0):
        a = gw[n, m0]
        return a.reshape(N_CHIPS * 2 * a.shape[-2], a.shape[-1])

    c = c + gather_begin(0, c)

    sm_like = [W[n].shape for n in SMALL_SHARDED]
    sm_pack, sm_n = _pack([W[n] for n in SMALL_SHARDED])
    sm_all = ag8(sm_pack, name="ag_small_w")[0::2].reshape(N_CHIPS, -1)[:, :sm_n]
    full = {}
    for n, part in zip(SMALL_SHARDED, zip(*[_unpack(sm_all[k], sm_like) for k in range(N_CHIPS)])):
        full[n] = jnp.concatenate(part, axis=-1)

    c_all = ag8(c, act=_silu, name="ag_c").reshape(N_DEV, D)
    p_mod = mm(c_all, mod_w, bias=chip_cols(mod_b, MW)[:, None, :], name="mod_mm")
    p_kv = mm(c_all, kv_mod_w, bias=chip_cols(kv_mod_b, KW)[None, :], name="kvmod_mm")
    p_all = jnp.concatenate([jnp.transpose(p_mod, (1, 0, 2)).reshape(N_DEV, depth * MW), p_kv], axis=1)
    p_mine = lax.dynamic_index_in_dim(ag8(p_all, name="ag_mod")[0::2], me, axis=1, keepdims=False)
    mod = jnp.transpose(p_mine[:, : depth * MW].reshape(N_CHIPS, depth, MW), (1, 0, 2)).reshape(depth, N_MOD * D)
    kvmod = p_mine[:, depth * MW :].reshape(1, 2 * D)
    mods = [[mod[i : i + 1, j * D : (j + 1) * D] for j in range(N_MOD)] for i in range(depth)]
    kv_shift, kv_scale = kvmod[:, :D], kvmod[:, D:]

    def ffn_fwd(xin, i, j, sh, sc, gt):
        h = norm_mod_fwd(xin, full["ffn_norm_w"][i, j][None], sc, sh, name=f"ffn_norm_{i}_{j}")
        gu, a = ffn_up(h, g_gu(i, j), name=f"ffn_gu_{i}_{j}")
        f = mm(a, g_dn(i, j), reduce_s=True, name=f"ffn_down_{i}_{j}")
        return gate_fwd(xin, f, gt, FFN_HALF, name=f"ffn_res_{i}_{j}"), (xin, gu, a, f)

    def ssm_fwd(xin, i, sh, sc, gt):
        h = norm_mod_fwd(xin, mix_norm_w[i][None], sc, sh, name=f"mix_norm_{i}")
        zx4 = mm(h, gw["ssm_w_in", i].reshape(N_CHIPS, D, CI), name=f"ssm_in_{i}")
        zx = jnp.transpose(zx4, (1, 0, 2)).reshape(L, N_CHIPS * CI)
        xbc = conv_fwd(zx, full["ssm_conv_w"][i], full["ssm_conv_b"][i][None], DI, name=f"ssm_conv_{i}")
        dt_raw = zx[:, DI + CC :]
        y, states = ssd_fwd(xbc, dt_raw, ssm_dt_bias[i][None], ssm_a_log[i][None], ssm_d[i][None], DI, name=f"ssd_{i}")
        yn = gnorm_fwd(y, zx, full["ssm_norm_w"][i][None], name=f"ssm_gnorm_{i}")
        f = mm(yn, g_full("ssm_w_out", i), name=f"ssm_out_{i}")
        return gate_fwd(xin, f, gt, 1.0, name=f"mix_res_{i}"), (xin, zx, xbc, dt_raw, y, states, yn, f)

    def att_fwd(xin, i, kv, sh, sc, gt):
        l = i - n_a
        h = norm_mod_fwd(xin, mix_norm_w[i][None], sc, sh, name=f"mix_norm_{i}")
        q = mm(h, g_full("attn_w_q", l), bias=attn_b_q[l][None], name=f"att_q_{i}")
        o = attn_fwd(q, kv, attn_sinks[l][None], name=f"att_{i}")
        f = mm(o, g_full("attn_w_o", l), bias=attn_b_o[l][None], name=f"att_o_{i}")
        return gate_fwd(xin, f, gt, 1.0, name=f"mix_res_{i}"), (xin, q, o, f)

    saved = []
    xc = xs
    kv = x_kv = None
    landed = gather_end(0, kvmod)
    n_stage = len(fwd_stages)

    def next_stage(s, landed):
        return gather_begin(s + 1, landed) if s + 1 < n_stage else 0.0

    for i in range(depth):
        sh1, sc1, g1, shm, scm, gm, sh2, sc2, g2 = mods[i]
        s = i + 1
        tok = next_stage(0 if i == 0 else s, landed)
        if i == n_a:
            x_kv = xc
            hkv = norm_mod_fwd(xc, kv_norm_w[None], kv_scale, kv_shift + tok, name="kv_norm")
            kv = mm(hkv, g_full("w_kv", 0), bias=b_kv[None], name="kv_proj")
        xc, s1 = ffn_fwd(xc, i, 0, sh1 + tok, sc1, g1)
        if i == 0:
            landed = gather_end(1, xc)
            shm = shm + next_stage(1, landed)
        xc, sm = ssm_fwd(xc, i, shm, scm, gm) if i < n_a else att_fwd(xc, i, kv, shm, scm, gm)
        xc, s2 = ffn_fwd(xc, i, 1, sh2, sc2, g2)
        if s + 1 < n_stage:
            landed = gather_end(s + 1, xc)
        saved.append((s1, sm, s2))

    loss_part, dx, d_final = final_loss(xc, final_norm_w[None], target, name="loss_head")
    loss = lax.psum(loss_part[0, 0], ("x", "y", "c"))

    wg = {}
    sg = {
        "ffn_norm_w": [[None, None] for _ in range(depth)], "mix_norm_w": [None] * depth, "mod": [None] * depth,
        "ssm_conv_w": [None] * n_a, "ssm_conv_b": [None] * n_a, "ssm_dt_bias": [None] * n_a, "ssm_a_log": [None] * n_a,
        "ssm_d": [None] * n_a, "ssm_norm_w": [None] * n_a, "attn_b_q": [None] * n_b, "attn_sinks": [None] * n_b,
        "attn_b_o": [None] * n_b,
    }

    def ffn_bwd(dxo, i, j, sv, sh, sc, gt):
        xin, gu, a, f = sv
        df, dgt, _ = gate_bwd(f, dxo, gt, FFN_HALF, name=f"ffn_res_bwd_{i}_{j}")
        dgu = ffn_down_bwd(df, g_dn(i, j), gu, name=f"ffn_down_dx_{i}_{j}")
        wg["ffn_w_down", 2 * i + j] = mm(a, df, mode="tn", out_dtype=BF16, name=f"ffn_down_dw_{i}_{j}")
        nw = full["ffn_norm_w"][i, j][None]
        h = norm_mod_fwd(xin, nw, sc, sh, name=f"ffn_norm_re_{i}_{j}")
        dh = mm(dgu, g_gu(i, j), mode="nt", reduce_s=True, name=f"ffn_gu_dx_{i}_{j}")
        wg["ffn_w_gu", 2 * i + j] = mm(h, dgu, mode="tn", out_dtype=BF16, name=f"ffn_gu_dw_{i}_{j}")
        dxi, dnw, dsc, dsh = norm_mod_bwd(xin, nw, sc, dh, dxo, name=f"ffn_norm_bwd_{i}_{j}")
        sg["ffn_norm_w"][i][j] = dnw
        return dxi, (dsh, dsc, dgt)

    def ssm_bwd(dxo, i, sv, sh, sc, gt):
        xin, zx, xbc, dt_raw, y, states, yn, f = sv
        df, dgt, _ = gate_bwd(f, dxo, gt, 1.0, name=f"mix_res_bwd_{i}")
        dyn = mm(df, g_full("ssm_w_out", i), mode="nt", name=f"ssm_out_dx_{i}")
        wg["ssm_w_out", i] = mm(yn, df, mode="tn", out_dtype=BF16, name=f"ssm_out_dw_{i}")
        dy, dz, dnorm = gnorm_bwd(y, zx, full["ssm_norm_w"][i][None], dyn, name=f"ssm_gnorm_bwd_{i}")
        dxbc, ddt, dbias, dalog, ddsk = ssd_bwd(
            xbc, dt_raw, ssm_dt_bias[i][None], ssm_a_log[i][None], ssm_d[i][None], states, dy, DI, name=f"ssd_bwd_{i}"
        )
        du, dcw, dcb = conv_bwd(zx, full["ssm_conv_w"][i], full["ssm_conv_b"][i][None], dxbc, DI, name=f"ssm_conv_bwd_{i}")
        dzx = jnp.concatenate([dz, du, ddt], axis=1).astype(BF16)
        dzx4 = jnp.transpose(dzx.reshape(L, N_CHIPS, CI), (1, 0, 2))
        nw = mix_norm_w[i][None]
        h = norm_mod_fwd(xin, nw, sc, sh, name=f"mix_norm_re_{i}")
        dh = mm(dzx4, gw["ssm_w_in", i].reshape(N_CHIPS, D, CI), mode="nt", reduce_s=True, name=f"ssm_in_dx_{i}")
        wg["ssm_w_in", i] = mm(h, dzx4, mode="tn", out_dtype=BF16, name=f"ssm_in_dw_{i}")
        dxi, dnw, dsc, dsh = norm_mod_bwd(xin, nw, sc, dh, dxo, name=f"mix_norm_bwd_{i}")
        sg["mix_norm_w"][i] = dnw
        sg["ssm_conv_w"][i], sg["ssm_conv_b"][i], sg["ssm_norm_w"][i] = dcw, dcb, dnorm
        sg["ssm_dt_bias"][i], sg["ssm_a_log"][i], sg["ssm_d"][i] = dbias, dalog, ddsk
        return dxi, (dsh, dsc, dgt)

    def att_bwd(dxo, i, sv, dkv, sh, sc, gt):
        l = i - n_a
        xin, q, o, f = sv
        df, dgt, dfsum = gate_bwd(f, dxo, gt, 1.0, name=f"mix_res_bwd_{i}")
        do = mm(df, g_full("attn_w_o", l), mode="nt", name=f"att_o_dx_{i}")
        wg["attn_w_o", l] = mm(o, df, mode="tn", out_dtype=BF16, name=f"att_o_dw_{i}")
        dq, dkv, dsink = attn_bwd(q, kv, attn_sinks[l][None], do, dkv, name=f"att_bwd_{i}")
        nw = mix_norm_w[i][None]
        h = norm_mod_fwd(xin, nw, sc, sh, name=f"mix_norm_re_{i}")
        dh = mm(dq, g_full("attn_w_q", l), mode="nt", name=f"att_q_dx_{i}")
        wg["attn_w_q", l] = mm(h, dq, mode="tn", out_dtype=BF16, name=f"att_q_dw_{i}")
        dxi, dnw, dsc, dsh = norm_mod_bwd(xin, nw, sc, dh, dxo, name=f"mix_norm_bwd_{i}")
        sg["mix_norm_w"][i] = dnw
        sg["attn_b_q"][l], sg["attn_sinks"][l], sg["attn_b_o"][l] = colsum(dq, name=f"att_bq_{i}"), dsink, dfsum
        return dxi, dkv, (dsh, dsc, dgt)

    gfull = {n: lax.empty(_row_halves(W[n]).shape, F32) for n in GATHERED}

    pending = []

    def reduce_step(s, after):
        if pending:
            reduce_end(after)
        items = bwd_stages[s]
        parts = []
        for n, m0, a in items:
            m, _, rh, cc = _row_halves(a).shape
            parts.append(wg.pop((n, m0)).reshape(m, N_CHIPS, 2, rh, cc))
        from_sib = rs_sibling(parts, name=f"rs_sibling_{s}")
        pair = [rs_add_pair(g, r, name=f"rs_pair_{n}_{m0}") for (n, m0, _), g, r in zip(items, parts, from_sib)]
        land_shapes = [(3, p.shape[0]) + p.shape[2:] for p in pair]
        ssem, rsem, srcs, lands, token = split_start(_reduce_copies, pair, land_shapes, after, name=f"rs_chips_start_{s}")
        pending.append((s, items, parts, from_sib, ssem, rsem, srcs, lands))
        return token[0:1, 0:1]

    def reduce_end(after):
        s, items, parts, from_sib, ssem, rsem, srcs, lands = pending.pop()
        from_chips = split_wait(_reduce_copies, ssem, rsem, srcs, lands, after, name=f"rs_chips_wait_{s}")
        for (n, m0, _), g, r, t in zip(items, parts, from_sib, from_chips):
            gfull[n] = rs_add_final(g, r, t, gfull[n], m0, name=f"rs_final_{n}_{m0}")

    dkv = jnp.zeros((L, KVD), F32)
    d_kvnorm = d_kvmod = d_bkv = None
    tok = 0.0
    for i in reversed(range(depth)):
        sh1, sc1, g1, shm, scm, gm, sh2, sc2, g2 = mods[i]
        s1, sm, s2 = saved[i]
        dx, dm2 = ffn_bwd(dx, i, 1, s2, sh2, sc2, g2 + tok)
        if i < n_a:
            dx, dmm = ssm_bwd(dx, i, sm, shm, scm, gm)
        else:
            dx, dkv, dmm = att_bwd(dx, i, sm, dkv, shm, scm, gm)
        if i == 0:
            g1 = g1 + reduce_step(depth - 1, dx)
        dx, dm1 = ffn_bwd(dx, i, 0, s1, sh1, sc1, g1)
        sg["mod"][i] = jnp.concatenate(list(dm1) + list(dmm) + list(dm2), axis=1)
        if i == n_a:
            d_bkv = colsum(dkv, name="kv_bias_bwd")
            hkv = norm_mod_fwd(x_kv, kv_norm_w[None], kv_scale, kv_shift, name="kv_norm_re")
            dh = mm(dkv, g_full("w_kv", 0), mode="nt", name="kv_proj_dx")
            wg["w_kv", 0] = mm(hkv, dkv, mode="tn", out_dtype=BF16, name="kv_proj_dw")
            dx, d_kvnorm, dsc, dsh = norm_mod_bwd(x_kv, kv_norm_w[None], kv_scale, dh, dx, name="kv_norm_bwd")
            d_kvmod = jnp.concatenate([dsh, dsc], axis=1)
        tok = reduce_step(depth - i if i == 0 else depth - 1 - i, dx)
    grad_x = dx[None]
    grads = {}

    small = {
        "ffn_norm_w": jnp.stack([jnp.stack([r[0] for r in row]) for row in sg["ffn_norm_w"]]),
        "mod_b": jnp.stack([r[0] for r in sg["mod"]]),
        "mix_norm_w": jnp.stack([r[0] for r in sg["mix_norm_w"]]),
        "ssm_conv_w": jnp.stack(sg["ssm_conv_w"]),
        "ssm_conv_b": jnp.stack([r[0] for r in sg["ssm_conv_b"]]),
        "ssm_dt_bias": jnp.stack([r[0] for r in sg["ssm_dt_bias"]]),
        "ssm_a_log": jnp.stack([r[0] for r in sg["ssm_a_log"]]),
        "ssm_d": jnp.stack([r[0] for r in sg["ssm_d"]]),
        "ssm_norm_w": jnp.stack([r[0] for r in sg["ssm_norm_w"]]),
        "kv_norm_w": d_kvnorm[0],
        "kv_mod_b": d_kvmod[0],
        "b_kv": d_bkv[0],
        "attn_b_q": jnp.stack([r[0] for r in sg["attn_b_q"]]),
        "attn_sinks": jnp.stack([r[0] for r in sg["attn_sinks"]]),
        "attn_b_o": jnp.stack([r[0] for r in sg["attn_b_o"]]),
        "final_norm_w": d_final[0],
    }
    small_like = [small[n].shape for n in SMALL]
    sv_pack, sv_n = _pack([small[n] for n in SMALL])
    sv_all = ag8(sv_pack + tok, name="ag_small_g")
    sv_sum = reduce8(sv_all, name="small_g_sum").reshape(-1)[:sv_n]
    for n, gsum in zip(SMALL, _unpack(sv_sum, small_like)):
        grads[n] = chip_cols(gsum, W[n].shape[-1]) if n in SMALL_SHARDED else gsum

    per_dev = [_unpack(sv_all[b].reshape(-1)[:sv_n], small_like) for b in range(N_DEV)]
    i_modb, i_kvb = SMALL.index("mod_b"), SMALL.index("kv_mod_b")
    dmod_all = jnp.stack([chip_cols(p[i_modb], MW) for p in per_dev], axis=1)
    dkv_all = jnp.stack([chip_cols(p[i_kvb], KW) for p in per_dev], axis=0)[None]
    c_t = jnp.transpose(c_all)
    grads["mod_w"] = outer8(c_t, dmod_all, name="mod_w_grad")
    grads["kv_mod_w"] = outer8(c_t, dkv_all, name="kv_mod_w_grad")[0]

    delta, new_m, new_v = {}, {}, {}
    for n in COLUMN_PARALLEL:
        delta[n], new_m[n], new_v[n] = adamw(W[n], grads[n], MOM[n], VAR[n], name=f"adamw_{n}")
    like = [W[n].shape for n in SMALL]
    packs = [_pack([d[n] for n in SMALL])[0] for d in (W, grads, MOM, VAR)]
    n_small = sum(int(W[n].size) for n in SMALL)
    for dst, res in zip((delta, new_m, new_v), adamw(*packs, name="adamw_small")):
        for n, a in zip(SMALL, _unpack(res.reshape(-1)[:n_small], like)):
            dst[n] = a

    reduce_end(delta["mod_w"])
    for n, s in zip(GATHERED, rs_share([gfull[n] for n in GATHERED], name="rs_share")):
        grads[n] = s.reshape(W[n].shape)
        delta[n], new_m[n], new_v[n] = adamw(W[n], grads[n], MOM[n], VAR[n], name=f"adamw_{n}")

    return (loss, grad_x, *[grads[n] for n in WEIGHTS], *[delta[n] for n in WEIGHTS], *[new_m[n] for n in WEIGHTS],
            *[new_v[n] for n in WEIGHTS])
```

```python
import functools

import jax
import jax.numpy as jnp
from jax import lax
from jax.experimental import pallas as pl
from jax.experimental.pallas import tpu as pltpu

F32 = jnp.float32
BF16 = jnp.bfloat16
HIGHEST = lax.Precision.HIGHEST
MESH = pl.DeviceIdType.MESH

EPS = 1e-5
N_MOD = 9
FFN_HALF = 0.5
SSM_HEADDIM = 64
SSM_GROUPS = 8
SSM_STATE = 128
CONV_WIDTH = 4
CHUNK = 128
KV_HEADS = 4
HEAD_DIM = 64
WINDOW = 128
N_CHIPS = 4
N_DEV = 8

ADAM_LR = 0.001
ADAM_B1 = 0.9
ADAM_B2 = 0.999
ADAM_EPS = 1e-08
ADAM_WD = 0.01
ADAM_STEP = 10

LANE = 128
MM_TILE = 1024


def _pick(n, pref, align, whole_if_small=False):
    best = 0
    t = align
    while t <= min(n, pref):
        if n % t == 0:
            best = t
        t += align
    if best == 0 or (whole_if_small and best < 256 and n <= 2048):
        return n
    return best


def _sigmoid(x):
    return 1.0 / (1.0 + jnp.exp(-x))


def _silu(x):
    return x * _sigmoid(x)


def _dsilu(x):
    s = _sigmoid(x)
    return s * (1.0 + x * (1.0 - s))


def _params(*sem):
    return pltpu.CompilerParams(dimension_semantics=sem)


def mm(a, b, *, mode="nn", reduce_s=False, out_dtype=F32, bias=None, a_lead=(), b_lead=(), out_buf=None,
       out_lead=(), name):
    a_s = a.ndim - len(a_lead) == 3
    b_s = b.ndim - len(b_lead) == 3
    S = a.shape[len(a_lead)] if a_s else (b.shape[len(b_lead)] if b_s else 1)
    a2 = a.shape[-2:]
    b2 = b.shape[-2:]
    if mode == "nn":
        (M, K), (K2, N) = a2, b2
    elif mode == "nt":
        (M, K), (N, K2) = a2, b2
    else:
        (K, M), (K2, N) = a2, b2
    assert K == K2, (a.shape, b.shape, mode)
    batch = (a_s or b_s) and not reduce_s
    sb = S if batch else 1
    sr = S if ((a_s or b_s) and reduce_s) else 1
    tm = _pick(M, MM_TILE, LANE if mode == "tn" else 16, True)
    tn = _pick(N, MM_TILE, LANE, True)
    tk = _pick(K, MM_TILE, LANE if mode != "tn" else 16, True)
    nk = K // tk
    grid = (sb, M // tm, N // tn, sr, nk)

    def s_of(isb, isr):
        return isb if batch else isr

    def a_map(isb, i, j, isr, k):
        idx = (k, i) if mode == "tn" else (i, k)
        return tuple(a_lead) + (((s_of(isb, isr),) + idx) if a_s else idx)

    def b_map(isb, i, j, isr, k):
        idx = (j, k) if mode == "nt" else (k, j)
        return tuple(b_lead) + (((s_of(isb, isr),) + idx) if b_s else idx)

    def o_map(isb, i, j, isr, k):
        return tuple(out_lead) + ((isb, i, j) if batch else (i, j))

    def lead_blk(lead, has_s, blk):
        return (None,) * (len(lead) + (1 if has_s else 0)) + blk

    a_blk = (tk, tm) if mode == "tn" else (tm, tk)
    b_blk = (tn, tk) if mode == "nt" else (tk, tn)
    in_specs = [pl.BlockSpec(lead_blk(a_lead, a_s, a_blk), a_map), pl.BlockSpec(lead_blk(b_lead, b_s, b_blk), b_map)]
    args = [a, b]
    if bias is not None:
        bias_s = bias.ndim == 3
        in_specs.append(
            pl.BlockSpec(
                ((None, 1, tn) if bias_s else (1, tn)),
                (lambda isb, i, j, isr, k: (isb, 0, j)) if bias_s else (lambda isb, i, j, isr, k: (0, j)),
            )
        )
        args.append(bias)
    aliases = {}
    if out_buf is not None:
        in_specs.append(pl.BlockSpec(memory_space=pl.ANY))
        aliases = {len(args): 0}
        args.append(out_buf)
        out_shape = jax.ShapeDtypeStruct(out_buf.shape, out_buf.dtype)
        want = tuple(out_buf.shape[len(out_lead):])
        assert want == ((sb, M, N) if batch else (M, N)), (want, sb, M, N)
    else:
        out_shape = jax.ShapeDtypeStruct(((sb, M, N) if batch else (M, N)), out_dtype)
    dims = {"nn": (((1,), (0,)), ((), ())), "nt": (((1,), (1,)), ((), ())), "tn": (((0,), (0,)), ((), ()))}[mode]
    n_in = len(args)
    one_step = sr * nk == 1

    def body(*refs):
        a_ref, b_ref = refs[0], refs[1]
        bias_ref = refs[2] if bias is not None else None
        o_ref = refs[n_in]

        def finish(r):
            if bias is not None:
                r = r + bias_ref[...]
            o_ref[...] = r.astype(o_ref.dtype)

        part = lax.dot_general(a_ref[...].astype(BF16), b_ref[...].astype(BF16), dims, preferred_element_type=F32)
        if one_step:
            finish(part)
            return
        acc = refs[n_in + 1]
        isr = pl.program_id(3)
        k = pl.program_id(4)

        @pl.when((isr == 0) & (k == 0))
        def _():
            acc[...] = part

        @pl.when((isr > 0) | (k > 0))
        def _():
            acc[...] += part

        @pl.when((isr == sr - 1) & (k == nk - 1))
        def _():
            finish(acc[...])

    return pl.pallas_call(
        body,
        out_shape=out_shape,
        grid=grid,
        in_specs=in_specs,
        out_specs=pl.BlockSpec(lead_blk(out_lead, batch, (tm, tn)), o_map),
        scratch_shapes=[] if one_step else [pltpu.VMEM((tm, tn), F32)],
        input_output_aliases=aliases,
        compiler_params=_params("parallel", "parallel", "parallel", "arbitrary", "arbitrary"),
        name=name,
    )(*args)


def norm_mod_fwd(x, nw, sc, sh, *, name):
    L, D = x.shape
    tl = _pick(L, 512, 16)

    def body(x_ref, nw_ref, sc_ref, sh_ref, h_ref):
        xv = x_ref[...]
        r = lax.rsqrt(jnp.mean(xv * xv, axis=-1, keepdims=True) + EPS)
        n = (xv * r) * nw_ref[...]
        h_ref[...] = (n * (1.0 + sc_ref[...]) + sh_ref[...]).astype(h_ref.dtype)

    row = pl.BlockSpec((1, D), lambda i: (0, 0))
    return pl.pallas_call(
        body,
        out_shape=jax.ShapeDtypeStruct((L, D), BF16),
        grid=(L // tl,),
        in_specs=[pl.BlockSpec((tl, D), lambda i: (i, 0)), row, row, row],
        out_specs=pl.BlockSpec((tl, D), lambda i: (i, 0)),
        compiler_params=_params("parallel"),
        name=name,
    )(x, nw, sc, sh)


def norm_mod_bwd(x, nw, sc, dh, dx_in, *, name):
    L, D = x.shape
    tl = _pick(L, 512, 16)

    def body(x_ref, nw_ref, sc_ref, dh_ref, dxi_ref, dx_ref, dnw_ref, dsc_ref, dsh_ref):
        @pl.when(pl.program_id(0) == 0)
        def _():
            dnw_ref[...] = jnp.zeros_like(dnw_ref)
            dsc_ref[...] = jnp.zeros_like(dsc_ref)
            dsh_ref[...] = jnp.zeros_like(dsh_ref)

        xv = x_ref[...]
        dh_v = dh_ref[...]
        r = lax.rsqrt(jnp.mean(xv * xv, axis=-1, keepdims=True) + EPS)
        xhat = xv * r
        nw_v = nw_ref[...]
        n = xhat * nw_v
        dsh_ref[...] += jnp.sum(dh_v, axis=0, keepdims=True)
        dsc_ref[...] += jnp.sum(dh_v * n, axis=0, keepdims=True)
        dn = dh_v * (1.0 + sc_ref[...])
        dnw_ref[...] += jnp.sum(dn * xhat, axis=0, keepdims=True)
        dxhat = dn * nw_v
        dx_ref[...] = dxi_ref[...] + r * (dxhat - xhat * jnp.mean(dxhat * xhat, axis=-1, keepdims=True))

    row = pl.BlockSpec((1, D), lambda i: (0, 0))
    tile = pl.BlockSpec((tl, D), lambda i: (i, 0))
    vec = jax.ShapeDtypeStruct((1, D), F32)
    return pl.pallas_call(
        body,
        out_shape=(jax.ShapeDtypeStruct((L, D), F32), vec, vec, vec),
        grid=(L // tl,),
        in_specs=[tile, row, row, tile, tile],
        out_specs=(tile, row, row, row),
        compiler_params=_params("arbitrary"),
        name=name,
    )(x, nw, sc, dh, dx_in)


def gate_fwd(x, f, gate, scale, *, name):
    L, D = x.shape
    tl = _pick(L, 512, 8)

    def body(x_ref, f_ref, g_ref, o_ref):
        o_ref[...] = x_ref[...] + (scale * g_ref[...]) * f_ref[...]

    tile = pl.BlockSpec((tl, D), lambda i: (i, 0))
    return pl.pallas_call(
        body,
        out_shape=jax.ShapeDtypeStruct((L, D), F32),
        grid=(L // tl,),
        in_specs=[tile, tile, pl.BlockSpec((1, D), lambda i: (0, 0))],
        out_specs=tile,
        compiler_params=_params("parallel"),
        name=name,
    )(x, f, gate)


def gate_bwd(f, dx, gate, scale, *, name):
    L, D = f.shape
    tl = _pick(L, 512, 16)

    def body(f_ref, dx_ref, g_ref, df_ref, dg_ref, dfsum_ref):
        @pl.when(pl.program_id(0) == 0)
        def _():
            dg_ref[...] = jnp.zeros_like(dg_ref)
            dfsum_ref[...] = jnp.zeros_like(dfsum_ref)

        dxv = dx_ref[...]
        df = (scale * g_ref[...]) * dxv
        df_ref[...] = df.astype(df_ref.dtype)
        dfsum_ref[...] += jnp.sum(df, axis=0, keepdims=True)
        dg_ref[...] += scale * jnp.sum(f_ref[...] * dxv, axis=0, keepdims=True)

    tile = pl.BlockSpec((tl, D), lambda i: (i, 0))
    row = pl.BlockSpec((1, D), lambda i: (0, 0))
    vec = jax.ShapeDtypeStruct((1, D), F32)
    return pl.pallas_call(
        body,
        out_shape=(jax.ShapeDtypeStruct((L, D), BF16), vec, vec),
        grid=(L // tl,),
        in_specs=[tile, tile, row],
        out_specs=(tile, row, row),
        compiler_params=_params("arbitrary"),
        name=name,
    )(f, dx, gate)


def colsum(x, *, name):
    L, N = x.shape
    tl = _pick(L, 512, 8)

    def body(x_ref, o_ref):
        @pl.when(pl.program_id(0) == 0)
        def _():
            o_ref[...] = jnp.zeros_like(o_ref)

        o_ref[...] += jnp.sum(x_ref[...], axis=0, keepdims=True)

    return pl.pallas_call(
        body,
        out_shape=jax.ShapeDtypeStruct((1, N), F32),
        grid=(L // tl,),
        in_specs=[pl.BlockSpec((tl, N), lambda i: (i, 0))],
        out_specs=pl.BlockSpec((1, N), lambda i: (0, 0)),
        compiler_params=_params("arbitrary"),
        name=name,
    )(x)


def ffn_up(h, wgu, *, name):
    L, D = h.shape
    T = wgu.shape[-1]
    tm = _pick(L, 512, 16)

    def body(h_ref, w_ref, gu_ref, a_ref):
        hb = h_ref[...].astype(BF16)
        g = jnp.dot(hb, w_ref[0].astype(BF16), preferred_element_type=F32)
        u = jnp.dot(hb, w_ref[1].astype(BF16), preferred_element_type=F32)
        gu_ref[0] = g
        gu_ref[1] = u
        a_ref[...] = (_silu(g) * u).astype(a_ref.dtype)

    gu, a = pl.pallas_call(
        body,
        out_shape=(jax.ShapeDtypeStruct((2, 2, L, T), F32), jax.ShapeDtypeStruct((2, L, T), BF16)),
        grid=(2, L // tm),
        in_specs=[
            pl.BlockSpec((tm, D), lambda j, i: (i, 0)),
            pl.BlockSpec((2, None, D, T), lambda j, i: (0, j, 0, 0)),
        ],
        out_specs=(
            pl.BlockSpec((2, None, tm, T), lambda j, i: (0, j, i, 0)),
            pl.BlockSpec((None, tm, T), lambda j, i: (j, i, 0)),
        ),
        compiler_params=_params("parallel", "parallel"),
        name=name,
    )(h, wgu.reshape(2, 2, D, T))
    return gu.reshape(4, L, T), a


def ffn_down_bwd(df, wdn, gu, *, name):
    L, D = df.shape
    T = wdn.shape[1]
    tm = _pick(L, 512, 16)

    def body(df_ref, w_ref, gu_ref, d_ref):
        da = lax.dot_general(
            df_ref[...].astype(BF16), w_ref[...].astype(BF16), (((1,), (1,)), ((), ())), preferred_element_type=F32
        )
        g = gu_ref[0]
        d_ref[0] = (da * gu_ref[1] * _dsilu(g)).astype(d_ref.dtype)
        d_ref[1] = (da * _silu(g)).astype(d_ref.dtype)

    out = pl.pallas_call(
        body,
        out_shape=jax.ShapeDtypeStruct((2, 2, L, T), BF16),
        grid=(2, L // tm),
        in_specs=[
            pl.BlockSpec((tm, D), lambda j, i: (i, 0)),
            pl.BlockSpec((None, T, D), lambda j, i: (j, 0, 0)),
            pl.BlockSpec((2, None, tm, T), lambda j, i: (0, j, i, 0)),
        ],
        out_specs=pl.BlockSpec((2, None, tm, T), lambda j, i: (0, j, i, 0)),
        compiler_params=_params("parallel", "parallel"),
        name=name,
    )(df, wdn, gu.reshape(2, 2, L, T))
    return out.reshape(4, L, T)


def _shift_down(u, k, rows):
    if k == 0:
        return u
    return jnp.where(rows >= k, pltpu.roll(u, k, 0), 0.0)


def _shift_up(u, k, rows, n):
    if k == 0:
        return u
    return jnp.where(rows < n - k, pltpu.roll(u, n - k, 0), 0.0)


def _conv_pre(u, w_ref, b_ref, rows):
    pre = b_ref[...] + w_ref[CONV_WIDTH - 1 : CONV_WIDTH, :] * u
    for k in range(1, CONV_WIDTH):
        pre = pre + w_ref[CONV_WIDTH - 1 - k : CONV_WIDTH - k, :] * _shift_down(u, k, rows)
    return pre


def conv_fwd(zx, conv_w, conv_b, d_inner, *, name):
    L = zx.shape[0]
    C = conv_w.shape[1]
    tc = 256
    off = d_inner // tc

    def body(u_ref, w_ref, b_ref, o_ref):
        rows = lax.broadcasted_iota(jnp.int32, (L, tc), 0)
        o_ref[...] = _silu(_conv_pre(u_ref[...], w_ref, b_ref, rows))

    return pl.pallas_call(
        body,
        out_shape=jax.ShapeDtypeStruct((L, C), F32),
        grid=(C // tc,),
        in_specs=[
            pl.BlockSpec((L, tc), lambda j: (0, off + j)),
            pl.BlockSpec((CONV_WIDTH, tc), lambda j: (0, j)),
            pl.BlockSpec((1, tc), lambda j: (0, j)),
        ],
        out_specs=pl.BlockSpec((L, tc), lambda j: (0, j)),
        compiler_params=_params("parallel"),
        name=name,
    )(zx, conv_w, conv_b)


def conv_bwd(zx, conv_w, conv_b, dxbc, d_inner, *, name):
    L = zx.shape[0]
    C = conv_w.shape[1]
    tc = 256
    off = d_inner // tc

    def body(u_ref, w_ref, b_ref, d_ref, du_ref, dw_ref, db_ref):
        rows = lax.broadcasted_iota(jnp.int32, (L, tc), 0)
        u = u_ref[...]
        dpre = d_ref[...] * _dsilu(_conv_pre(u, w_ref, b_ref, rows))
        db_ref[...] = jnp.sum(dpre, axis=0, keepdims=True)
        du = w_ref[CONV_WIDTH - 1 : CONV_WIDTH, :] * dpre
        dw_ref[CONV_WIDTH - 1 : CONV_WIDTH, :] = jnp.sum(dpre * u, axis=0, keepdims=True)
        for k in range(1, CONV_WIDTH):
            j = CONV_WIDTH - 1 - k
            dw_ref[j : j + 1, :] = jnp.sum(dpre * _shift_down(u, k, rows), axis=0, keepdims=True)
            du = du + w_ref[j : j + 1, :] * _shift_up(dpre, k, rows, L)
        du_ref[...] = du

    return pl.pallas_call(
        body,
        out_shape=(
            jax.ShapeDtypeStruct((L, C), F32),
            jax.ShapeDtypeStruct((CONV_WIDTH, C), F32),
            jax.ShapeDtypeStruct((1, C), F32),
        ),
        grid=(C // tc,),
        in_specs=[
            pl.BlockSpec((L, tc), lambda j: (0, off + j)),
            pl.BlockSpec((CONV_WIDTH, tc), lambda j: (0, j)),
            pl.BlockSpec((1, tc), lambda j: (0, j)),
            pl.BlockSpec((L, tc), lambda j: (0, j)),
        ],
        out_specs=(
            pl.BlockSpec((L, tc), lambda j: (0, j)),
            pl.BlockSpec((CONV_WIDTH, tc), lambda j: (0, j)),
            pl.BlockSpec((1, tc), lambda j: (0, j)),
        ),
        compiler_params=_params("parallel"),
        name=name,
    )(zx, conv_w, conv_b, dxbc)


def _ssd_head(xs, dt, acs, tot, dsk, cb, bm, cm, prev):
    q = xs.shape[0]
    li = lax.broadcasted_iota(jnp.int32, (q, q), 0)
    si = lax.broadcasted_iota(jnp.int32, (q, q), 1)
    causal = li >= si
    lmat = jnp.exp(jnp.where(causal, acs - acs.T, -jnp.inf))
    xdt = xs * dt
    y = jnp.dot((cb * lmat).astype(BF16), xdt.astype(BF16), preferred_element_type=F32)
    y = y + lax.dot_general(
        (cm * jnp.exp(acs)).astype(BF16), prev.astype(BF16), (((1,), (1,)), ((), ())), preferred_element_type=F32
    )
    y = y + dsk * xs
    st = lax.dot_general(
        xdt.astype(BF16), (bm * jnp.exp(tot - acs)).astype(BF16), (((0,), (0,)), ((), ())), preferred_element_type=F32
    )
    return y, prev * jnp.exp(tot) + st


def _pick_lane(v, h):
    lanes = lax.broadcasted_iota(jnp.int32, v.shape, 1)
    return jnp.sum(jnp.where(lanes == h, v, 0.0), axis=1, keepdims=True)


def _tri_cols(cols, upper):
    q = cols[0].shape[0]
    assert 3 * len(cols) <= LANE
    li = lax.broadcasted_iota(jnp.int32, (q, q), 0)
    si = lax.broadcasted_iota(jnp.int32, (q, q), 1)
    tri = ((li <= si) if upper else (li >= si)).astype(BF16)
    lanes = lax.broadcasted_iota(jnp.int32, (q, LANE), 1)
    rhs = jnp.zeros((q, LANE), F32)
    for r, col in enumerate(cols):
        hi = col.astype(BF16).astype(F32)
        mid = (col - hi).astype(BF16).astype(F32)
        lo = col - hi - mid
        for t, term in enumerate((hi, mid, lo)):
            rhs = jnp.where(lanes == 3 * r + t, term, rhs)
    out = jnp.dot(tri, rhs.astype(BF16), preferred_element_type=F32)
    return [jnp.sum(jnp.where((lanes >= 3 * r) & (lanes < 3 * r + 3), out, 0.0), axis=1, keepdims=True)
            for r in range(len(cols))]


def _softplus(x):
    return jnp.maximum(x, 0.0) + jnp.log(1.0 + jnp.exp(-jnp.abs(x)))


def _ssd_specs(L, d_inner, H, nc, rev):
    R = H // SSM_GROUPS
    P, N, Q = SSM_HEADDIM, SSM_STATE, CHUNK
    ngrp = SSM_GROUPS

    def ci(c):
        return (nc - 1 - c) if rev else c

    b_off = d_inner // N
    c_off = b_off + ngrp
    xs = pl.BlockSpec((Q, R * P), lambda c, g: (ci(c), g))
    bm = pl.BlockSpec((Q, N), lambda c, g: (ci(c), b_off + g))
    cm = pl.BlockSpec((Q, N), lambda c, g: (ci(c), c_off + g))
    dt = pl.BlockSpec((Q, H), lambda c, g: (ci(c), 0))
    hv = pl.BlockSpec((1, H), lambda c, g: (0, 0))
    y = pl.BlockSpec((Q, R * P), lambda c, g: (ci(c), g))
    st = pl.BlockSpec((None, R * P, N), lambda c, g: (ci(c), g, 0))
    return R, xs, bm, cm, dt, hv, y, st


def ssd_fwd(xbc, dt_raw, dt_bias, a_log, d_skip, d_inner, *, name):
    L = xbc.shape[0]
    H = dt_raw.shape[1]
    nc = L // CHUNK
    P, N = SSM_HEADDIM, SSM_STATE
    R, xs_s, bm_s, cm_s, dt_s, hv_s, y_s, st_s = _ssd_specs(L, d_inner, H, nc, False)

    def body(xs_ref, bm_ref, cm_ref, dt_ref, bias_ref, alog_ref, dsk_ref, y_ref, st_ref, state):
        c = pl.program_id(0)
        g = pl.program_id(1)

        @pl.when(c == 0)
        def _():
            for r in range(R):
                state[g * R + r] = jnp.zeros((P, N), F32)

        dtb = _softplus(dt_ref[...] + bias_ref[...])
        a_all = -jnp.exp(alog_ref[...])
        bm, cm = bm_ref[...], cm_ref[...]
        cb = lax.dot_general(cm.astype(BF16), bm.astype(BF16), (((1,), (1,)), ((), ())), preferred_element_type=F32)
        dts = [_pick_lane(dtb, g * R + r) for r in range(R)]
        a_cols = [dts[r] * _pick_lane(a_all, g * R + r) for r in range(R)]
        acs = _tri_cols(a_cols, upper=False)
        prevs = [state[g * R + r] for r in range(R)]
        res = []
        for r in range(R):
            res.append(_ssd_head(
                xs_ref[:, r * P : (r + 1) * P],
                dts[r],
                jnp.broadcast_to(acs[r], (CHUNK, CHUNK)),
                jnp.sum(a_cols[r], axis=0, keepdims=True),
                _pick_lane(dsk_ref[...], g * R + r),
                cb,
                bm,
                cm,
                prevs[r],
            ))
        for r in range(R):
            st_ref[r * P : (r + 1) * P, :] = prevs[r]
            y_ref[:, r * P : (r + 1) * P] = res[r][0]
            state[g * R + r] = res[r][1]

    return pl.pallas_call(
        body,
        out_shape=(jax.ShapeDtypeStruct((L, d_inner), F32), jax.ShapeDtypeStruct((nc, H * P, N), F32)),
        grid=(nc, SSM_GROUPS),
        in_specs=[xs_s, bm_s, cm_s, dt_s, hv_s, hv_s, hv_s],
        out_specs=(y_s, st_s),
        scratch_shapes=[pltpu.VMEM((H, P, N), F32)],
        compiler_params=_params("arbitrary", "arbitrary"),
        name=name,
    )(xbc, xbc, xbc, dt_raw, dt_bias, a_log, d_skip)


def ssd_bwd(xbc, dt_raw, dt_bias, a_log, d_skip, states, dy, d_inner, *, name):
    L, C = xbc.shape
    H = dt_raw.shape[1]
    nc = L // CHUNK
    P, N, Q = SSM_HEADDIM, SSM_STATE, CHUNK
    R, xs_s, bm_s, cm_s, dt_s, hv_s, y_s, st_s = _ssd_specs(L, d_inner, H, nc, True)

    def body(xs_ref, bm_ref, cm_ref, dt_ref, bias_ref, alog_ref, dsk_ref, st_ref, dy_ref,
             dxs_ref, dbm_ref, dcm_ref, ddt_ref, dbias_ref, dalog_ref, ddsk_ref, dstate):
        c = pl.program_id(0)
        g = pl.program_id(1)

        @pl.when(c == 0)
        def _():
            for r in range(R):
                dstate[g * R + r] = jnp.zeros((P, N), F32)

        @pl.when((c == 0) & (g == 0))
        def _():
            dbias_ref[...] = jnp.zeros_like(dbias_ref)
            dalog_ref[...] = jnp.zeros_like(dalog_ref)
            ddsk_ref[...] = jnp.zeros_like(ddsk_ref)

        @pl.when(g == 0)
        def _():
            ddt_ref[...] = jnp.zeros_like(ddt_ref)

        pre = dt_ref[...] + bias_ref[...]
        dtb = _softplus(pre)
        a_all = -jnp.exp(alog_ref[...])
        lanes_q = lax.broadcasted_iota(jnp.int32, (Q, H), 1)
        lanes_1 = lax.broadcasted_iota(jnp.int32, (1, H), 1)
        bm = bm_ref[...]
        cm = cm_ref[...]
        nt = (((1,), (1,)), ((), ()))
        cb = lax.dot_general(cm.astype(BF16), bm.astype(BF16), nt, preferred_element_type=F32)
        dts = [_pick_lane(dtb, g * R + r) for r in range(R)]
        a_negs = [_pick_lane(a_all, g * R + r) for r in range(R)]
        a_cols = [dts[r] * a_negs[r] for r in range(R)]
        acs = _tri_cols(a_cols, upper=False)
        dbm = jnp.zeros((Q, N), F32)
        dcm = jnp.zeros((Q, N), F32)
        dcb = jnp.zeros((Q, Q), F32)
        dd_row = jnp.zeros((1, H), F32)
        dstates = [dstate[g * R + r] for r in range(R)]
        dprevs, ddts, dacs_cols, dtots = [], [], [], []
        for r in range(R):
            h = g * R + r
            args = (
                xs_ref[:, r * P : (r + 1) * P],
                dts[r],
                jnp.broadcast_to(acs[r], (Q, Q)),
                jnp.sum(a_cols[r], axis=0, keepdims=True),
                _pick_lane(dsk_ref[...], h),
                cb,
                bm,
                cm,
                st_ref[r * P : (r + 1) * P, :],
            )
            _, vjp = jax.vjp(_ssd_head, *args)
            dxs, ddt, dacs, dtot, dd, dcb_h, dbm_h, dcm_h, dprev = vjp((dy_ref[:, r * P : (r + 1) * P], dstates[r]))
            dxs_ref[:, r * P : (r + 1) * P] = dxs
            dprevs.append(dprev)
            ddts.append(ddt)
            dacs_cols.append(jnp.sum(dacs, axis=1, keepdims=True))
            dtots.append(dtot)
            dbm = dbm + dbm_h
            dcm = dcm + dcm_h
            dcb = dcb + dcb_h
            dd_row = dd_row + jnp.where(lanes_1 == h, dd, 0.0)
        for r in range(R):
            dstate[g * R + r] = dprevs[r]
        ddt_blk = jnp.zeros((Q, H), F32)
        da_row = jnp.zeros((1, H), F32)
        for r, da_col in enumerate(_tri_cols(dacs_cols, upper=True)):
            h = g * R + r
            da_col = da_col + dtots[r]
            ddt_blk = ddt_blk + jnp.where(lanes_q == h, ddts[r] + da_col * a_negs[r], 0.0)
            da_row = da_row + jnp.where(lanes_1 == h, jnp.sum(da_col * dts[r], axis=0, keepdims=True), 0.0)
        dcb16 = dcb.astype(BF16)
        dbm_ref[...] = dbm + lax.dot_general(dcb16, cm.astype(BF16), (((0,), (0,)), ((), ())), preferred_element_type=F32)
        dcm_ref[...] = dcm + jnp.dot(dcb16, bm.astype(BF16), preferred_element_type=F32)
        ddt_pre = ddt_blk * _sigmoid(pre)
        ddt_ref[...] += ddt_pre
        dbias_ref[...] += jnp.sum(ddt_pre, axis=0, keepdims=True)
        dalog_ref[...] += da_row * a_all
        ddsk_ref[...] += dd_row

    ngrp = SSM_GROUPS
    hrow = jax.ShapeDtypeStruct((1, H), F32)
    dxs, dbm, dcm, ddt, dbias, dalog, ddsk = pl.pallas_call(
        body,
        out_shape=(
            jax.ShapeDtypeStruct((L, d_inner), F32),
            jax.ShapeDtypeStruct((L, ngrp * N), F32),
            jax.ShapeDtypeStruct((L, ngrp * N), F32),
            jax.ShapeDtypeStruct((L, H), F32),
            hrow,
            hrow,
            hrow,
        ),
        grid=(nc, ngrp),
        in_specs=[xs_s, bm_s, cm_s, dt_s, hv_s, hv_s, hv_s, st_s, y_s],
        out_specs=(
            y_s,
            pl.BlockSpec((Q, N), lambda c, g: (nc - 1 - c, g)),
            pl.BlockSpec((Q, N), lambda c, g: (nc - 1 - c, g)),
            dt_s,
            hv_s,
            hv_s,
            hv_s,
        ),
        scratch_shapes=[pltpu.VMEM((H, P, N), F32)],
        compiler_params=_params("arbitrary", "arbitrary"),
        name=name,
    )(xbc, xbc, xbc, dt_raw, dt_bias, a_log, d_skip, states, dy)
    return jnp.concatenate([dxs, dbm, dcm], axis=1), ddt, dbias, dalog, ddsk


def gnorm_fwd(y, zx, nw, *, name):
    L, DI = y.shape
    gw = DI // SSM_GROUPS
    tl = _pick(L, 512, 16)

    def body(y_ref, z_ref, nw_ref, o_ref):
        yz = y_ref[...] * _silu(z_ref[...])
        r = lax.rsqrt(jnp.mean(yz * yz, axis=-1, keepdims=True) + EPS)
        o_ref[...] = ((yz * r) * nw_ref[...]).astype(o_ref.dtype)

    tile = pl.BlockSpec((tl, gw), lambda i, g: (i, g))
    return pl.pallas_call(
        body,
        out_shape=jax.ShapeDtypeStruct((L, DI), BF16),
        grid=(L // tl, SSM_GROUPS),
        in_specs=[tile, tile, pl.BlockSpec((1, gw), lambda i, g: (0, g))],
        out_specs=tile,
        compiler_params=_params("parallel", "parallel"),
        name=name,
    )(y, zx, nw)


def gnorm_bwd(y, zx, nw, dout, *, name):
    L, DI = y.shape
    gw = DI // SSM_GROUPS
    tl = _pick(L, 512, 16)

    def body(y_ref, z_ref, nw_ref, do_ref, dy_ref, dz_ref, dnw_ref):
        @pl.when(pl.program_id(1) == 0)
        def _():
            dnw_ref[...] = jnp.zeros_like(dnw_ref)

        yv = y_ref[...]
        zv = z_ref[...]
        sz = _silu(zv)
        yz = yv * sz
        r = lax.rsqrt(jnp.mean(yz * yz, axis=-1, keepdims=True) + EPS)
        n = yz * r
        dov = do_ref[...]
        dnw_ref[...] += jnp.sum(dov * n, axis=0, keepdims=True)
        dn = dov * nw_ref[...]
        dyz = r * (dn - n * jnp.mean(dn * n, axis=-1, keepdims=True))
        dy_ref[...] = dyz * sz
        dz_ref[...] = dyz * yv * _dsilu(zv)

    tile = pl.BlockSpec((tl, gw), lambda g, i: (i, g))
    row = pl.BlockSpec((1, gw), lambda g, i: (0, g))
    return pl.pallas_call(
        body,
        out_shape=(
            jax.ShapeDtypeStruct((L, DI), F32),
            jax.ShapeDtypeStruct((L, DI), F32),
            jax.ShapeDtypeStruct((1, DI), F32),
        ),
        grid=(SSM_GROUPS, L // tl),
        in_specs=[tile, tile, row, tile],
        out_specs=(tile, tile, row),
        compiler_params=_params("parallel", "arbitrary"),
        name=name,
    )(y, zx, nw, dout)


def _attn_head(q, kp, kc, vp, vc, sink, has_prev):
    rows, w = q.shape[0], kc.shape[0]
    nt = (((1,), (1,)), ((), ()))
    qb = q.astype(BF16)
    sc = lax.dot_general(qb, kc.astype(BF16), nt, preferred_element_type=F32) * HEAD_DIM ** -0.5
    sp = lax.dot_general(qb, kp.astype(BF16), nt, preferred_element_type=F32) * HEAD_DIM ** -0.5
    ii = jnp.bitwise_and(lax.broadcasted_iota(jnp.int32, (rows, w), 0), w - 1)
    jj = lax.broadcasted_iota(jnp.int32, (rows, w), 1)
    lc = jnp.where(jj <= ii, sc, -jnp.inf)
    lp = jnp.where((jj > ii) & has_prev, sp, -jnp.inf)
    m = jnp.maximum(jnp.maximum(jnp.max(lc, axis=1, keepdims=True), jnp.max(lp, axis=1, keepdims=True)), sink)
    m = lax.stop_gradient(m)
    pc = jnp.exp(lc - m)
    pp = jnp.exp(lp - m)
    denom = jnp.sum(pc, axis=1, keepdims=True) + jnp.sum(pp, axis=1, keepdims=True) + jnp.exp(sink - m)
    o = jnp.dot((pc / denom).astype(BF16), vc.astype(BF16), preferred_element_type=F32)
    return o + jnp.dot((pp / denom).astype(BF16), vp.astype(BF16), preferred_element_type=F32)


def attn_fwd(q, kv, sinks, *, name):
    L, DQ = q.shape
    heads = DQ // HEAD_DIM
    rep = heads // KV_HEADS
    nb = L // WINDOW
    kw = KV_HEADS * HEAD_DIM
    W, HD = WINDOW, HEAD_DIM

    def body(q_ref, kp_ref, kc_ref, vp_ref, vc_ref, s_ref, o_ref):
        has_prev = pl.program_id(0) > 0
        for kh in range(KV_HEADS):
            ks = slice(kh * HD, (kh + 1) * HD)
            hs = [kh * rep + rr for rr in range(rep)]
            o = _attn_head(
                jnp.concatenate([q_ref[:, h * HD : (h + 1) * HD] for h in hs], axis=0),
                kp_ref[:, ks], kc_ref[:, ks], vp_ref[:, ks], vc_ref[:, ks],
                jnp.concatenate([jnp.broadcast_to(s_ref[:, h : h + 1], (W, 1)) for h in hs], axis=0), has_prev,
            )
            for rr, h in enumerate(hs):
                o_ref[:, h * HD : (h + 1) * HD] = o[rr * W : (rr + 1) * W].astype(o_ref.dtype)

    return pl.pallas_call(
        body,
        out_shape=jax.ShapeDtypeStruct((L, DQ), BF16),
        grid=(nb,),
        in_specs=[
            pl.BlockSpec((W, DQ), lambda n: (n, 0)),
            pl.BlockSpec((W, kw), lambda n: (jnp.maximum(n - 1, 0), 0)),
            pl.BlockSpec((W, kw), lambda n: (n, 0)),
            pl.BlockSpec((W, kw), lambda n: (jnp.maximum(n - 1, 0), 1)),
            pl.BlockSpec((W, kw), lambda n: (n, 1)),
            pl.BlockSpec((1, heads), lambda n: (0, 0)),
        ],
        out_specs=pl.BlockSpec((W, DQ), lambda n: (n, 0)),
        compiler_params=_params("parallel"),
        name=name,
    )(q, kv, kv, kv, kv, sinks)


def attn_bwd(q, kv, sinks, do, dkv_in, *, name):
    L, DQ = q.shape
    heads = DQ // HEAD_DIM
    rep = heads // KV_HEADS
    nb = L // WINDOW
    kw = KV_HEADS * HEAD_DIM
    W, HD = WINDOW, HEAD_DIM

    def blk(n):
        return jnp.minimum(n, nb - 1)

    def prev(n):
        return jnp.maximum(blk(n) - 1, 0)

    def outb(n):
        return jnp.maximum(n - 1, 0)

    def body(q_ref, kp_ref, kc_ref, vp_ref, vc_ref, s_ref, do_ref, dki_ref, dvi_ref,
             dq_ref, dk_ref, dv_ref, ds_ref, dk_cur, dv_cur):
        n = pl.program_id(0)
        has_prev = n > 0

        @pl.when(n == 0)
        def _():
            ds_ref[...] = jnp.zeros_like(ds_ref)
            dk_cur[...] = jnp.zeros_like(dk_cur)
            dv_cur[...] = jnp.zeros_like(dv_cur)

        @pl.when(n == nb)
        def _():
            dk_ref[...] = dki_ref[...] + dk_cur[...]
            dv_ref[...] = dvi_ref[...] + dv_cur[...]

        @pl.when(n < nb)
        def _():
            lanes = lax.broadcasted_iota(jnp.int32, (1, heads), 1)
            ds_row = jnp.zeros((1, heads), F32)
            for kh in range(KV_HEADS):
                ks = slice(kh * HD, (kh + 1) * HD)
                hs = [kh * rep + rr for rr in range(rep)]
                _, vjp = jax.vjp(
                    functools.partial(_attn_head, has_prev=has_prev),
                    jnp.concatenate([q_ref[:, h * HD : (h + 1) * HD] for h in hs], axis=0),
                    kp_ref[:, ks], kc_ref[:, ks], vp_ref[:, ks], vc_ref[:, ks],
                    jnp.concatenate([jnp.broadcast_to(s_ref[:, h : h + 1], (W, 1)) for h in hs], axis=0),
                )
                dq, dkp, dkc, dvp, dvc, dsk = vjp(
                    jnp.concatenate([do_ref[:, h * HD : (h + 1) * HD] for h in hs], axis=0))
                for rr, h in enumerate(hs):
                    dq_ref[:, h * HD : (h + 1) * HD] = dq[rr * W : (rr + 1) * W]
                    ds_row = ds_row + jnp.where(lanes == h, jnp.sum(dsk[rr * W : (rr + 1) * W], axis=0, keepdims=True), 0.0)
                dk_ref[:, ks] = dki_ref[:, ks] + dk_cur[:, ks] + dkp
                dv_ref[:, ks] = dvi_ref[:, ks] + dv_cur[:, ks] + dvp
                dk_cur[:, ks] = dkc
                dv_cur[:, ks] = dvc
            ds_ref[...] += ds_row

    dq, dk, dv, ds = pl.pallas_call(
        body,
        out_shape=(
            jax.ShapeDtypeStruct((L, DQ), F32),
            jax.ShapeDtypeStruct((L, kw), F32),
            jax.ShapeDtypeStruct((L, kw), F32),
            jax.ShapeDtypeStruct((1, heads), F32),
        ),
        grid=(nb + 1,),
        in_specs=[
            pl.BlockSpec((W, DQ), lambda n: (blk(n), 0)),
            pl.BlockSpec((W, kw), lambda n: (prev(n), 0)),
            pl.BlockSpec((W, kw), lambda n: (blk(n), 0)),
            pl.BlockSpec((W, kw), lambda n: (prev(n), 1)),
            pl.BlockSpec((W, kw), lambda n: (blk(n), 1)),
            pl.BlockSpec((1, heads), lambda n: (0, 0)),
            pl.BlockSpec((W, DQ), lambda n: (blk(n), 0)),
            pl.BlockSpec((W, kw), lambda n: (outb(n), 0)),
            pl.BlockSpec((W, kw), lambda n: (outb(n), 1)),
        ],
        out_specs=(
            pl.BlockSpec((W, DQ), lambda n: (blk(n), 0)),
            pl.BlockSpec((W, kw), lambda n: (outb(n), 0)),
            pl.BlockSpec((W, kw), lambda n: (outb(n), 0)),
            pl.BlockSpec((1, heads), lambda n: (0, 0)),
        ),
        scratch_shapes=[pltpu.VMEM((W, kw), F32), pltpu.VMEM((W, kw), F32)],
        compiler_params=_params("arbitrary"),
        name=name,
    )(q, kv, kv, kv, kv, sinks, do, dkv_in, dkv_in)
    return dq, jnp.concatenate([dk, dv], axis=1), ds


def final_loss(x, fw, target, *, name):
    L, D = x.shape
    tl = _pick(L, 512, 8)

    def body(x_ref, fw_ref, t_ref, loss_ref, dx_ref, dfw_ref):
        @pl.when(pl.program_id(0) == 0)
        def _():
            loss_ref[...] = jnp.zeros_like(loss_ref)
            dfw_ref[...] = jnp.zeros_like(dfw_ref)

        xv = x_ref[...]
        fwv = fw_ref[...]
        r = lax.rsqrt(jnp.mean(xv * xv, axis=-1, keepdims=True) + EPS)
        xhat = xv * r
        err = xhat * fwv - t_ref[...]
        loss_ref[...] += 0.5 * jnp.sum(jnp.mean(err * err, axis=-1, keepdims=True), axis=0, keepdims=True)
        dy = err * (1.0 / D)
        dfw_ref[...] += jnp.sum(dy * xhat, axis=0, keepdims=True)
        dxhat = dy * fwv
        dx_ref[...] = r * (dxhat - xhat * jnp.mean(dxhat * xhat, axis=-1, keepdims=True))

    tile = pl.BlockSpec((tl, D), lambda i: (i, 0))
    row = pl.BlockSpec((1, D), lambda i: (0, 0))
    return pl.pallas_call(
        body,
        out_shape=(
            jax.ShapeDtypeStruct((1, 1), F32),
            jax.ShapeDtypeStruct((L, D), F32),
            jax.ShapeDtypeStruct((1, D), F32),
        ),
        grid=(L // tl,),
        in_specs=[tile, row, tile],
        out_specs=(pl.BlockSpec((1, 1), lambda i: (0, 0)), tile, row),
        compiler_params=_params("arbitrary"),
        name=name,
    )(x, fw, target)


def outer8(ct, d, *, name):
    D, B = ct.shape
    S, _, N = d.shape
    tm = _pick(D, 512, 8)
    tn = _pick(N, 256, LANE)

    def body(c_ref, d_ref, o_ref):
        acc = c_ref[:, 0:1] * d_ref[0:1, :]
        for b in range(1, B):
            acc = acc + c_ref[:, b : b + 1] * d_ref[b : b + 1, :]
        o_ref[...] = acc

    return pl.pallas_call(
        body,
        out_shape=jax.ShapeDtypeStruct((S, D, N), F32),
        grid=(S, D // tm, N // tn),
        in_specs=[
            pl.BlockSpec((tm, B), lambda s, i, j: (i, 0)),
            pl.BlockSpec((None, B, tn), lambda s, i, j: (s, 0, j)),
        ],
        out_specs=pl.BlockSpec((None, tm, tn), lambda s, i, j: (s, i, j)),
        compiler_params=_params("parallel", "parallel", "parallel"),
        name=name,
    )(ct, d)


def reduce8(g, *, name):
    nd, R, N = g.shape

    def body(g_ref, o_ref):
        acc = g_ref[0]
        for b in range(1, nd):
            acc = acc + g_ref[b]
        o_ref[...] = acc

    return pl.pallas_call(
        body,
        out_shape=jax.ShapeDtypeStruct((R, N), F32),
        name=name,
    )(g)


def _as3(a):
    if a.ndim == 1:
        return a.reshape(1, 1, -1)
    if a.ndim == 2:
        return a.reshape((1,) + a.shape)
    return a.reshape((-1,) + a.shape[-2:])


def adamw(w, g, m, v, *, name):
    shape = w.shape
    w3, g3, m3, v3 = _as3(w), _as3(g), _as3(m), _as3(v)
    B, R, C = w3.shape
    tr = _pick(R, max(8, (1 << 19) // max(C, 1) // 8 * 8), 8)

    def body(w_ref, g_ref, m_ref, v_ref, d_ref, nm_ref, nv_ref):
        gv = g_ref[...]
        mn = ADAM_B1 * m_ref[...] + (1.0 - ADAM_B1) * gv
        vn = ADAM_B2 * v_ref[...] + (1.0 - ADAM_B2) * (gv * gv)
        m_hat = mn / (1.0 - ADAM_B1 ** ADAM_STEP)
        v_hat = vn / (1.0 - ADAM_B2 ** ADAM_STEP)
        d_ref[...] = -ADAM_LR * (m_hat / (jnp.sqrt(v_hat) + ADAM_EPS) + ADAM_WD * w_ref[...])
        nm_ref[...] = mn
        nv_ref[...] = vn

    tile = pl.BlockSpec((None, tr, C), lambda b, i: (b, i, 0))
    sds = jax.ShapeDtypeStruct((B, R, C), F32)
    d, nm, nv = pl.pallas_call(
        body,
        out_shape=(sds, sds, sds),
        grid=(B, R // tr),
        in_specs=[tile, tile, tile, tile],
        out_specs=(tile, tile, tile),
        compiler_params=_params("parallel", "parallel"),
        name=name,
    )(w3, g3, m3, v3)
    return d.reshape(shape), nm.reshape(shape), nv.reshape(shape)


def _place():
    return lax.axis_index("x"), lax.axis_index("y"), lax.axis_index("c")


def _flip(v, bit):
    return (1 - v) if bit else v


def ag8(v, *, act=None, name):
    R, N = v.shape

    def body(v_ref, out_ref, stage, send_sems, recv_sems):
        x, y, c = _place()
        me = 4 * x + 2 * y + c
        val = v_ref[...]
        if act is not None:
            val = act(val)
        stage[...] = val
        out_ref[me] = val
        sends = []
        for k in range(1, N_DEV):
            px, py, pc = _flip(x, k & 4), _flip(y, k & 2), _flip(c, k & 1)
            cp = pltpu.make_async_remote_copy(
                src_ref=stage, dst_ref=out_ref.at[me], send_sem=send_sems.at[k - 1], recv_sem=recv_sems.at[k - 1],
                device_id=(px, py, pc), device_id_type=MESH,
            )
            cp.start()
            sends.append(cp)
        for k in range(1, N_DEV):
            px, py, pc = _flip(x, k & 4), _flip(y, k & 2), _flip(c, k & 1)
            pltpu.make_async_remote_copy(
                src_ref=stage, dst_ref=out_ref.at[4 * px + 2 * py + pc], send_sem=send_sems.at[k - 1],
                recv_sem=recv_sems.at[k - 1], device_id=(px, py, pc), device_id_type=MESH,
            ).wait_recv()
        for cp in sends:
            cp.wait_send()

    return pl.pallas_call(
        body,
        out_shape=jax.ShapeDtypeStruct((N_DEV, R, N), F32),
        in_specs=[pl.BlockSpec(memory_space=pltpu.VMEM)],
        out_specs=pl.BlockSpec(memory_space=pltpu.VMEM),
        scratch_shapes=[
            pltpu.VMEM((R, N), F32),
            pltpu.SemaphoreType.DMA((N_DEV - 1,)),
            pltpu.SemaphoreType.DMA((N_DEV - 1,)),
        ],
        name=name,
    )(v)


def _other_chips(x, y):
    chips = [(1 - x, y), (x, 1 - y), (1 - x, 1 - y)]
    return chips, [2 * px + py for px, py in chips]


_HBM = pl.BlockSpec(memory_space=pltpu.HBM)


_SEM = pl.BlockSpec(memory_space=pltpu.SEMAPHORE)
_ANY = pl.BlockSpec(memory_space=pl.ANY)
_EFFECT = pltpu.SideEffectType.DATAFLOW_SIDE_EFFECTING


def _gather_copies(srcs, lands, send_sems, recv_sems):
    x, y, c = _place()
    k_me = 2 * x + y
    chips, kidx = _other_chips(x, y)
    cps = []
    for w in range(len(srcs)):
        for j, (px, py) in enumerate(chips):
            def copy(dst, w=w, j=j, px=px, py=py):
                return pltpu.make_async_remote_copy(
                    src_ref=srcs[w].at[:, c], dst_ref=dst, send_sem=send_sems.at[3 * w + j],
                    recv_sem=recv_sems.at[3 * w + j], device_id=(px, py, c), device_id_type=MESH,
                )
            cps.append((copy(lands[w].at[:, k_me, c]), copy(lands[w].at[:, kidx[j], c])))
    return cps


def _fill_copies(srcs, lands, send_sems, recv_sems):
    x, y, c = _place()
    k_me = 2 * x + y
    _, kidx = _other_chips(x, y)
    sib = (x, y, 1 - c)
    cps = []
    for w in range(len(srcs)):
        own = pltpu.make_async_remote_copy(
            src_ref=srcs[w], dst_ref=lands[w].at[:, k_me], send_sem=send_sems.at[4 * w + 3], recv_sem=recv_sems.at[4 * w + 3],
            device_id=sib, device_id_type=MESH,
        )
        cps.append((own, own))
        for j in range(3):
            def copy(half, w=w, j=j):
                part = lands[w].at[:, kidx[j], half]
                return pltpu.make_async_remote_copy(
                    src_ref=part, dst_ref=part, send_sem=send_sems.at[4 * w + j], recv_sem=recv_sems.at[4 * w + j],
                    device_id=sib, device_id_type=MESH,
                )
            cps.append((copy(c), copy(1 - c)))
    return cps


def _reduce_copies(srcs, lands, send_sems, recv_sems):
    x, y, c = _place()
    chips, kidx = _other_chips(x, y)
    cps = []
    for w in range(len(srcs)):
        for j, (px, py) in enumerate(chips):
            cp = pltpu.make_async_remote_copy(
                src_ref=srcs[w].at[:, kidx[j]], dst_ref=lands[w].at[j], send_sem=send_sems.at[3 * w + j],
                recv_sem=recv_sems.at[3 * w + j], device_id=(px, py, c), device_id_type=MESH,
            )
            cps.append((cp, cp))
    return cps


def split_start(copies, srcs, land_shapes, after, *, name, lands=None, per=3):
    n = len(srcs)

    def body(*refs):
        src_refs, land_refs = refs[:n], refs[n : 2 * n]
        send_sems, recv_sems = refs[2 * n + 1], refs[2 * n + 2]
        token = refs[-1]
        for cp, _ in copies(src_refs, land_refs, send_sems, recv_sems):
            cp.start()
        token[...] = jnp.zeros_like(token)

    if lands is None:
        lands = [lax.empty(sh, s.dtype) for sh, s in zip(land_shapes, srcs)]
    land_shapes = [a.shape for a in lands]
    lands = [pltpu.with_memory_space_constraint(a, pltpu.HBM) for a in lands]
    srcs = [pltpu.with_memory_space_constraint(s, pltpu.HBM) for s in srcs]
    out = pl.pallas_call(
        body,
        out_shape=(
            pltpu.SemaphoreType.DMA((per * n,)), pltpu.SemaphoreType.DMA((per * n,)),
            *[pltpu.HBM(s.shape, s.dtype) for s in srcs],
            *[pltpu.HBM(sh, s.dtype) for sh, s in zip(land_shapes, srcs)],
            jax.ShapeDtypeStruct((8, LANE), F32),
        ),
        in_specs=[_HBM] * (2 * n) + [_ANY],
        out_specs=(_SEM, _SEM, *([_HBM] * (2 * n)), pl.BlockSpec(memory_space=pltpu.VMEM)),
        input_output_aliases={i: 2 + i for i in range(2 * n)},
        compiler_params=pltpu.CompilerParams(has_side_effects=_EFFECT),
        name=name,
    )(*srcs, *lands, after)
    return out[0], out[1], list(out[2 : 2 + n]), list(out[2 + n : 2 + 2 * n]), out[-1]


def split_wait(copies, send_sems, recv_sems, srcs, lands, after, *, name):
    n = len(srcs)

    def body(*refs):
        src_refs, land_refs = refs[:n], refs[n : 2 * n]
        send_ref, recv_ref = refs[2 * n], refs[2 * n + 1]
        for sent, arrives in copies(src_refs, land_refs, send_ref, recv_ref):
            sent.wait_send()
            arrives.wait_recv()

    out = pl.pallas_call(
        body,
        out_shape=tuple(pltpu.HBM(a.shape, a.dtype) for a in list(srcs) + list(lands)),
        in_specs=[_HBM] * (2 * n) + [_SEM, _SEM, _ANY],
        out_specs=tuple([_HBM] * (2 * n)),
        input_output_aliases={i: i for i in range(2 * n)},
        compiler_params=pltpu.CompilerParams(has_side_effects=_EFFECT),
        name=name,
    )(*srcs, *lands, send_sems, recv_sems, after)
    return list(out[n:])


def rs_sibling(grads, *, name):
    n = len(grads)

    def body(*refs):
        ins, outs = refs[:n], refs[n : 2 * n]
        send_sems, recv_sems = refs[2 * n :]
        x, y, c = _place()
        cps = []
        for w in range(n):
            cp = pltpu.make_async_remote_copy(
                src_ref=ins[w].at[:, :, 1 - c], dst_ref=outs[w], send_sem=send_sems.at[w], recv_sem=recv_sems.at[w],
                device_id=(x, y, 1 - c), device_id_type=MESH,
            )
            cp.start()
            cps.append(cp)
        for cp in cps:
            cp.wait()

    return list(pl.pallas_call(
        body,
        out_shape=tuple(jax.ShapeDtypeStruct(g.shape[:2] + g.shape[3:], g.dtype) for g in grads),
        in_specs=[_HBM] * n,
        out_specs=tuple([_HBM] * n),
        scratch_shapes=[pltpu.SemaphoreType.DMA((n,)), pltpu.SemaphoreType.DMA((n,))],
        name=name,
    )(*grads))


def rs_share(halves, *, name):
    n = len(halves)

    def body(*refs):
        outs = refs[n : 2 * n]
        send_sems, recv_sems = refs[2 * n :]
        x, y, c = _place()
        cps = []
        for w in range(n):
            cp = pltpu.make_async_remote_copy(
                src_ref=outs[w].at[:, c], dst_ref=outs[w].at[:, c], send_sem=send_sems.at[w], recv_sem=recv_sems.at[w],
                device_id=(x, y, 1 - c), device_id_type=MESH,
            )
            cp.start()
            cps.append(cp)
        for w, cp in enumerate(cps):
            cp.wait_send()
            pltpu.make_async_remote_copy(
                src_ref=outs[w].at[:, c], dst_ref=outs[w].at[:, 1 - c], send_sem=send_sems.at[w], recv_sem=recv_sems.at[w],
                device_id=(x, y, 1 - c), device_id_type=MESH,
            ).wait_recv()

    return pl.pallas_call(
        body,
        out_shape=tuple(jax.ShapeDtypeStruct(h.shape, h.dtype) for h in halves),
        in_specs=[_HBM] * n,
        out_specs=tuple([_HBM] * n),
        scratch_shapes=[pltpu.SemaphoreType.DMA((n,)), pltpu.SemaphoreType.DMA((n,))],
        input_output_aliases={w: w for w in range(n)},
        name=name,
    )(*halves)


def _row_tile(R, C):
    return _pick(R, max(16, (1 << 19) // C // 16 * 16), 16)


def _my_core():
    return lax.axis_index("c")


def _my_chip():
    return 2 * lax.axis_index("x") + lax.axis_index("y")


def rs_add_pair(g, r, *, name):
    M, K, _, R, C = g.shape
    tr = _row_tile(R, C)

    def body(g_ref, r_ref, o_ref):
        o_ref[...] = (g_ref[...].astype(F32) + r_ref[...].astype(F32)).astype(o_ref.dtype)

    blk = pl.BlockSpec((None, None, tr, C), lambda m, k, i: (m, k, i, 0))
    return pl.pallas_call(
        body,
        out_shape=jax.ShapeDtypeStruct((M, K, R, C), BF16),
        grid=(M, K, R // tr),
        in_specs=[pl.BlockSpec((None, None, None, tr, C), lambda m, k, i: (m, k, _my_core(), i, 0)), blk],
        out_specs=blk,
        compiler_params=_params("parallel", "parallel", "parallel"),
        name=name,
    )(g, r)


def rs_add_final(g, r, t, full, m0, *, name):
    M, K, _, R, C = g.shape
    tr = _row_tile(R, C)

    def body(g_ref, r_ref, t_ref, full_ref, o_ref):
        acc = g_ref[...].astype(F32) + r_ref[...].astype(F32)
        for j in range(3):
            acc = acc + t_ref[j].astype(F32)
        o_ref[...] = acc

    return pl.pallas_call(
        body,
        out_shape=jax.ShapeDtypeStruct(full.shape, full.dtype),
        grid=(M, R // tr),
        in_specs=[
            pl.BlockSpec((None, None, None, tr, C), lambda m, i: (m, _my_chip(), _my_core(), i, 0)),
            pl.BlockSpec((None, None, tr, C), lambda m, i: (m, _my_chip(), i, 0)),
            pl.BlockSpec((3, None, tr, C), lambda m, i: (0, m, i, 0)),
            _ANY,
        ],
        out_specs=pl.BlockSpec((None, None, tr, C), lambda m, i: (m0 + m, _my_core(), i, 0)),
        input_output_aliases={3: 0},
        compiler_params=_params("parallel", "parallel"),
        name=name,
    )(g, r, t, full)


WEIGHTS = ["ffn_norm_w", "ffn_w_gu", "ffn_w_down", "mod_w", "mod_b", "mix_norm_w", "ssm_w_in", "ssm_conv_w", "ssm_conv_b",
           "ssm_dt_bias", "ssm_a_log", "ssm_d", "ssm_norm_w", "ssm_w_out", "kv_norm_w", "kv_mod_w", "kv_mod_b", "w_kv", "b_kv",
           "attn_w_q", "attn_b_q", "attn_sinks", "attn_w_o", "attn_b_o", "final_norm_w"]
GATHERED = ["ffn_w_gu", "ffn_w_down", "ssm_w_in", "ssm_w_out", "w_kv", "attn_w_q", "attn_w_o"]
COLUMN_PARALLEL = ["mod_w", "kv_mod_w"]
SMALL_SHARDED = ["ffn_norm_w", "ssm_conv_w", "ssm_conv_b", "ssm_norm_w"]
SMALL = [n for n in WEIGHTS if n not in GATHERED and n not in COLUMN_PARALLEL]


def _row_halves(a):
    a = a.reshape((-1,) + a.shape[-2:])
    return a.reshape(a.shape[0], 2, a.shape[1] // 2, a.shape[2])


def _pack(arrs, rows=8):
    flat = jnp.concatenate([a.reshape(-1) for a in arrs])
    n = flat.shape[0]
    pad = (-n) % (rows * LANE)
    return jnp.pad(flat, (0, pad)).reshape(rows, -1), n


def _unpack(flat, like):
    out, o = [], 0
    for s in like:
        k = 1
        for d in s:
            k *= d
        out.append(flat[o : o + k].reshape(s))
        o += k
    return out


def kernel(x, c, ffn_norm_w, ffn_w_gu, ffn_w_down, mod_w, mod_b, mix_norm_w, ssm_w_in, ssm_conv_w, ssm_conv_b, ssm_dt_bias, ssm_a_log, ssm_d, ssm_norm_w, ssm_w_out, kv_norm_w, kv_mod_w, kv_mod_b, w_kv, b_kv, attn_w_q, attn_b_q, attn_sinks, attn_w_o, attn_b_o, final_norm_w, loss_target, m_ffn_norm_w, m_ffn_w_gu, m_ffn_w_down, m_mod_w, m_mod_b, m_mix_norm_w, m_ssm_w_in, m_ssm_conv_w, m_ssm_conv_b, m_ssm_dt_bias, m_ssm_a_log, m_ssm_d, m_ssm_norm_w, m_ssm_w_out, m_kv_norm_w, m_kv_mod_w, m_kv_mod_b, m_w_kv, m_b_kv, m_attn_w_q, m_attn_b_q, m_attn_sinks, m_attn_w_o, m_attn_b_o, m_final_norm_w, v_ffn_norm_w, v_ffn_w_gu, v_ffn_w_down, v_mod_w, v_mod_b, v_mix_norm_w, v_ssm_w_in, v_ssm_conv_w, v_ssm_conv_b, v_ssm_dt_bias, v_ssm_a_log, v_ssm_d, v_ssm_norm_w, v_ssm_w_out, v_kv_norm_w, v_kv_mod_w, v_kv_mod_b, v_w_kv, v_b_kv, v_attn_w_q, v_attn_b_q, v_attn_sinks, v_attn_w_o, v_attn_b_o, v_final_norm_w):
    env = dict(locals())
    W = {n: env[n] for n in WEIGHTS}
    MOM = {n: env["m_" + n] for n in WEIGHTS}
    VAR = {n: env["v_" + n] for n in WEIGHTS}

    ax, ay, ac = _place()
    kme = 2 * ax + ay
    me = 4 * ax + 2 * ay + ac

    xs = x[0]
    target = loss_target[0]
    L, D = xs.shape
    depth, n_a = ffn_w_gu.shape[0], ssm_w_in.shape[0]
    n_b = depth - n_a
    T = ffn_w_gu.shape[-1]
    DI = ssm_w_out.shape[1] * N_CHIPS
    CI = ssm_w_in.shape[2]
    CC = ssm_conv_w.shape[2] * N_CHIPS
    MW = mod_w.shape[2]
    KW = kv_mod_w.shape[1]
    KVD = w_kv.shape[1]

    def chip_cols(a, width):
        return lax.dynamic_slice_in_dim(a, kme * width, width, axis=a.ndim - 1)

    def ffn_items(i, j):
        return [("ffn_w_gu", 2 * i + j, ffn_w_gu[i, j]), ("ffn_w_down", 2 * i + j, ffn_w_down[i, j])]

    def mix_items(i):
        if i < n_a:
            return [("ssm_w_in", i, ssm_w_in[i]), ("ssm_w_out", i, ssm_w_out[i])]
        return [("attn_w_q", i - n_a, attn_w_q[i - n_a]), ("attn_w_o", i - n_a, attn_w_o[i - n_a])]

    def layer_items(i, order):
        kv_items = [("w_kv", 0, w_kv)] if i == n_a else []
        if order == "fwd":
            return kv_items + ffn_items(i, 0) + mix_items(i) + ffn_items(i, 1)
        return ffn_items(i, 1) + mix_items(i) + ffn_items(i, 0) + kv_items

    fwd_stages = [ffn_items(0, 0), mix_items(0) + ffn_items(0, 1)] + [layer_items(i, "fwd") for i in range(1, depth)]
    bwd_stages = [layer_items(i, "bwd") for i in range(depth - 1, 0, -1)] + [ffn_items(0, 1) + mix_items(0), ffn_items(0, 0)]

    gw, inflight = {}, {}

    def chips_begin(s, after):
        keys = [(n, m0) for n, m0, _ in fwd_stages[s]]
        shards = [_row_halves(a.astype(BF16)) for _, _, a in fwd_stages[s]]
        land_shapes = [(sh.shape[0], N_CHIPS) + sh.shape[1:] for sh in shards]
        ssem, rsem, srcs, lands, token = split_start(_gather_copies, shards, land_shapes, after, name=f"gather_start_{s}")
        inflight[s] = (keys, ssem, rsem, srcs, lands)
        return token[0:1, 0:1]

    def n_first(s):
        return 2 if s < 2 else len(fwd_stages[s]) - 4

    def cores_begin(s, after):
        keys, ssem, rsem, srcs, lands = inflight.pop(s)
        lands = split_wait(_gather_copies, ssem, rsem, srcs, lands, after, name=f"gather_wait_{s}")
        k = n_first(s)
        a_sem, b_sem, a_src, a_land, tok = split_start(
            _fill_copies, srcs[:k], None, lands[0], name=f"fill_start_{s}a", lands=lands[:k], per=4)
        if k < len(keys):
            inflight[s] = (keys[k:],) + split_start(
                _fill_copies, srcs[k:], None, tok, name=f"fill_start_{s}b", lands=lands[k:], per=4)
            tok = inflight[s][-1]
        gw.update(zip(keys[:k], split_wait(_fill_copies, a_sem, b_sem, a_src, a_land, tok, name=f"fill_wait_{s}a")))
        return tok[0:1, 0:1]

    def cores_end(s, after):
        if s in inflight:
            keys, ssem, rsem, srcs, lands, _ = inflight.pop(s)
            gw.update(zip(keys, split_wait(_fill_copies, ssem, rsem, srcs, lands, after, name=f"fill_wait_{s}b")))

    def g_gu(i, j):
        return gw["ffn_w_gu", 2 * i + j].reshape(N_CHIPS, D, T)

    def g_dn(i, j):
        return gw["ffn_w_down", 2 * i + j].reshape(2, T, D)

    def g_full(n, m0):
        a = gw[n, m0]
        return a.reshape(N_CHIPS * 2 * a.shape[-2], a.shape[-1])

    c = c + chips_begin(0, c)

    sm_like = [W[n].shape for n in SMALL_SHARDED]
    sm_pack, sm_n = _pack([W[n] for n in SMALL_SHARDED])
    sm_all = ag8(sm_pack, name="ag_small_w")[0::2].reshape(N_CHIPS, -1)[:, :sm_n]
    full = {}
    for n, part in zip(SMALL_SHARDED, zip(*[_unpack(sm_all[k], sm_like) for k in range(N_CHIPS)])):
        full[n] = jnp.concatenate(part, axis=-1)

    c_all = ag8(c, act=_silu, name="ag_c").reshape(N_DEV, D)
    p_mod = mm(c_all, mod_w, bias=chip_cols(mod_b, MW)[:, None, :], name="mod_mm")
    p_kv = mm(c_all, kv_mod_w, bias=chip_cols(kv_mod_b, KW)[None, :], name="kvmod_mm")
    p_all = jnp.concatenate([jnp.transpose(p_mod, (1, 0, 2)).reshape(N_DEV, depth * MW), p_kv], axis=1)
    p_mine = lax.dynamic_index_in_dim(ag8(p_all, name="ag_mod")[0::2], me, axis=1, keepdims=False)
    mod = jnp.transpose(p_mine[:, : depth * MW].reshape(N_CHIPS, depth, MW), (1, 0, 2)).reshape(depth, N_MOD * D)
    kvmod = p_mine[:, depth * MW :].reshape(1, 2 * D)
    mods = [[mod[i : i + 1, j * D : (j + 1) * D] for j in range(N_MOD)] for i in range(depth)]
    kv_shift, kv_scale = kvmod[:, :D], kvmod[:, D:]

    def ffn_fwd(xin, i, j, sh, sc, gt):
        h = norm_mod_fwd(xin, full["ffn_norm_w"][i, j][None], sc, sh, name=f"ffn_norm_{i}_{j}")
        gu, a = ffn_up(h, g_gu(i, j), name=f"ffn_gu_{i}_{j}")
        f = mm(a, g_dn(i, j), reduce_s=True, name=f"ffn_down_{i}_{j}")
        return gate_fwd(xin, f, gt, FFN_HALF, name=f"ffn_res_{i}_{j}"), (xin, gu, a, f)

    def ssm_fwd(xin, i, sh, sc, gt):
        h = norm_mod_fwd(xin, mix_norm_w[i][None], sc, sh, name=f"mix_norm_{i}")
        zx4 = mm(h, gw["ssm_w_in", i].reshape(N_CHIPS, D, CI), name=f"ssm_in_{i}")
        zx = jnp.transpose(zx4, (1, 0, 2)).reshape(L, N_CHIPS * CI)
        xbc = conv_fwd(zx, full["ssm_conv_w"][i], full["ssm_conv_b"][i][None], DI, name=f"ssm_conv_{i}")
        dt_raw = zx[:, DI + CC :]
        y, states = ssd_fwd(xbc, dt_raw, ssm_dt_bias[i][None], ssm_a_log[i][None], ssm_d[i][None], DI, name=f"ssd_{i}")
        yn = gnorm_fwd(y, zx, full["ssm_norm_w"][i][None], name=f"ssm_gnorm_{i}")
        f = mm(yn, g_full("ssm_w_out", i), name=f"ssm_out_{i}")
        return gate_fwd(xin, f, gt, 1.0, name=f"mix_res_{i}"), (xin, zx, xbc, dt_raw, y, states, yn, f)

    def att_fwd(xin, i, kv, sh, sc, gt):
        l = i - n_a
        h = norm_mod_fwd(xin, mix_norm_w[i][None], sc, sh, name=f"mix_norm_{i}")
        q = mm(h, g_full("attn_w_q", l), bias=attn_b_q[l][None], name=f"att_q_{i}")
        o = attn_fwd(q, kv, attn_sinks[l][None], name=f"att_{i}")
        f = mm(o, g_full("attn_w_o", l), bias=attn_b_o[l][None], name=f"att_o_{i}")
        return gate_fwd(xin, f, gt, 1.0, name=f"mix_res_{i}"), (xin, q, o, f)

    saved = [[None, None, None] for _ in range(depth)]
    xc = xs
    kv = x_kv = None
    n_stage = len(fwd_stages)

    def run_stage(s, xc, tok):
        nonlocal kv, x_kv
        i = max(s - 1, 0)
        sh1, sc1, g1, shm, scm, gm, sh2, sc2, g2 = mods[i]
        if s != 1:
            if i == n_a:
                x_kv = xc
                hkv = norm_mod_fwd(xc, kv_norm_w[None], kv_scale, kv_shift + tok, name="kv_norm")
                kv = mm(hkv, g_full("w_kv", 0), bias=b_kv[None], name="kv_proj")
            xc, saved[i][0] = ffn_fwd(xc, i, 0, sh1 + tok, sc1, g1)
            if s != 0:
                cores_end(s, xc)
        if s != 0:
            xc, saved[i][1] = ssm_fwd(xc, i, shm + tok, scm, gm) if i < n_a else att_fwd(xc, i, kv, shm + tok, scm, gm)
            if s == 1:
                cores_end(s, xc)
            xc, saved[i][2] = ffn_fwd(xc, i, 1, sh2, sc2, g2)
        return xc

    dep = cores_begin(0, kvmod)
    for s in range(n_stage):
        tok = chips_begin(s + 1, dep) if s + 1 < n_stage else 0.0
        xc = run_stage(s, xc, tok)
        if s + 1 < n_stage:
            dep = cores_begin(s + 1, xc)

    loss_part, dx, d_final = final_loss(xc, final_norm_w[None], target, name="loss_head")
    loss = lax.psum(loss_part[0, 0], ("x", "y", "c"))

    wg = {}
    sg = {
        "ffn_norm_w": [[None, None] for _ in range(depth)], "mix_norm_w": [None] * depth, "mod": [None] * depth,
        "ssm_conv_w": [None] * n_a, "ssm_conv_b": [None] * n_a, "ssm_dt_bias": [None] * n_a, "ssm_a_log": [None] * n_a,
        "ssm_d": [None] * n_a, "ssm_norm_w": [None] * n_a, "attn_b_q": [None] * n_b, "attn_sinks": [None] * n_b,
        "attn_b_o": [None] * n_b,
    }

    def ffn_bwd(dxo, i, j, sv, sh, sc, gt):
        xin, gu, a, f = sv
        df, dgt, _ = gate_bwd(f, dxo, gt, FFN_HALF, name=f"ffn_res_bwd_{i}_{j}")
        dgu = ffn_down_bwd(df, g_dn(i, j), gu, name=f"ffn_down_dx_{i}_{j}")
        wg["ffn_w_down", 2 * i + j] = mm(a, df, mode="tn", out_dtype=BF16, name=f"ffn_down_dw_{i}_{j}")
        nw = full["ffn_norm_w"][i, j][None]
        h = norm_mod_fwd(xin, nw, sc, sh, name=f"ffn_norm_re_{i}_{j}")
        dh = mm(dgu, g_gu(i, j), mode="nt", reduce_s=True, name=f"ffn_gu_dx_{i}_{j}")
        wg["ffn_w_gu", 2 * i + j] = mm(h, dgu, mode="tn", out_dtype=BF16, name=f"ffn_gu_dw_{i}_{j}")
        dxi, dnw, dsc, dsh = norm_mod_bwd(xin, nw, sc, dh, dxo, name=f"ffn_norm_bwd_{i}_{j}")
        sg["ffn_norm_w"][i][j] = dnw
        return dxi, (dsh, dsc, dgt)

    def ssm_bwd(dxo, i, sv, sh, sc, gt):
        xin, zx, xbc, dt_raw, y, states, yn, f = sv
        df, dgt, _ = gate_bwd(f, dxo, gt, 1.0, name=f"mix_res_bwd_{i}")
        dyn = mm(df, g_full("ssm_w_out", i), mode="nt", name=f"ssm_out_dx_{i}")
        wg["ssm_w_out", i] = mm(yn, df, mode="tn", out_dtype=BF16, name=f"ssm_out_dw_{i}")
        dy, dz, dnorm = gnorm_bwd(y, zx, full["ssm_norm_w"][i][None], dyn, name=f"ssm_gnorm_bwd_{i}")
        dxbc, ddt, dbias, dalog, ddsk = ssd_bwd(
            xbc, dt_raw, ssm_dt_bias[i][None], ssm_a_log[i][None], ssm_d[i][None], states, dy, DI, name=f"ssd_bwd_{i}"
        )
        du, dcw, dcb = conv_bwd(zx, full["ssm_conv_w"][i], full["ssm_conv_b"][i][None], dxbc, DI, name=f"ssm_conv_bwd_{i}")
        dzx = jnp.concatenate([dz, du, ddt], axis=1).astype(BF16)
        dzx4 = jnp.transpose(dzx.reshape(L, N_CHIPS, CI), (1, 0, 2))
        nw = mix_norm_w[i][None]
        h = norm_mod_fwd(xin, nw, sc, sh, name=f"mix_norm_re_{i}")
        dh = mm(dzx4, gw["ssm_w_in", i].reshape(N_CHIPS, D, CI), mode="nt", reduce_s=True, name=f"ssm_in_dx_{i}")
        wg["ssm_w_in", i] = mm(h, dzx4, mode="tn", out_dtype=BF16, name=f"ssm_in_dw_{i}")
        dxi, dnw, dsc, dsh = norm_mod_bwd(xin, nw, sc, dh, dxo, name=f"mix_norm_bwd_{i}")
        sg["mix_norm_w"][i] = dnw
        sg["ssm_conv_w"][i], sg["ssm_conv_b"][i], sg["ssm_norm_w"][i] = dcw, dcb, dnorm
        sg["ssm_dt_bias"][i], sg["ssm_a_log"][i], sg["ssm_d"][i] = dbias, dalog, ddsk
        return dxi, (dsh, dsc, dgt)

    def att_bwd(dxo, i, sv, dkv, sh, sc, gt):
        l = i - n_a
        xin, q, o, f = sv
        df, dgt, dfsum = gate_bwd(f, dxo, gt, 1.0, name=f"mix_res_bwd_{i}")
        do = mm(df, g_full("attn_w_o", l), mode="nt", name=f"att_o_dx_{i}")
        wg["attn_w_o", l] = mm(o, df, mode="tn", out_dtype=BF16, name=f"att_o_dw_{i}")
        dq, dkv, dsink = attn_bwd(q, kv, attn_sinks[l][None], do, dkv, name=f"att_bwd_{i}")
        nw = mix_norm_w[i][None]
        h = norm_mod_fwd(xin, nw, sc, sh, name=f"mix_norm_re_{i}")
        dh = mm(dq, g_full("attn_w_q", l), mode="nt", name=f"att_q_dx_{i}")
        wg["attn_w_q", l] = mm(h, dq, mode="tn", out_dtype=BF16, name=f"att_q_dw_{i}")
        dxi, dnw, dsc, dsh = norm_mod_bwd(xin, nw, sc, dh, dxo, name=f"mix_norm_bwd_{i}")
        sg["mix_norm_w"][i] = dnw
        sg["attn_b_q"][l], sg["attn_sinks"][l], sg["attn_b_o"][l] = colsum(dq, name=f"att_bq_{i}"), dsink, dfsum
        return dxi, dkv, (dsh, dsc, dgt)

    gfull = {n: lax.empty(_row_halves(W[n]).shape, F32) for n in GATHERED}

    pending = []

    def reduce_step(s, after):
        if pending:
            reduce_end(after)
        items = bwd_stages[s]
        parts = []
        for n, m0, a in items:
            m, _, rh, cc = _row_halves(a).shape
            parts.append(wg.pop((n, m0)).reshape(m, N_CHIPS, 2, rh, cc))
        from_sib = rs_sibling(parts, name=f"rs_sibling_{s}")
        pair = [rs_add_pair(g, r, name=f"rs_pair_{n}_{m0}") for (n, m0, _), g, r in zip(items, parts, from_sib)]
        land_shapes = [(3, p.shape[0]) + p.shape[2:] for p in pair]
        ssem, rsem, srcs, lands, token = split_start(_reduce_copies, pair, land_shapes, after, name=f"rs_chips_start_{s}")
        pending.append((s, items, parts, from_sib, ssem, rsem, srcs, lands))
        return token[0:1, 0:1]

    def reduce_end(after):
        s, items, parts, from_sib, ssem, rsem, srcs, lands = pending.pop()
        from_chips = split_wait(_reduce_copies, ssem, rsem, srcs, lands, after, name=f"rs_chips_wait_{s}")
        for (n, m0, _), g, r, t in zip(items, parts, from_sib, from_chips):
            gfull[n] = rs_add_final(g, r, t, gfull[n], m0, name=f"rs_final_{n}_{m0}")

    dkv = jnp.zeros((L, KVD), F32)
    d_kvnorm = d_kvmod = d_bkv = None
    tok = 0.0
    for i in reversed(range(depth)):
        sh1, sc1, g1, shm, scm, gm, sh2, sc2, g2 = mods[i]
        s1, sm, s2 = saved[i]
        dx, dm2 = ffn_bwd(dx, i, 1, s2, sh2, sc2, g2 + tok)
        if i < n_a:
            dx, dmm = ssm_bwd(dx, i, sm, shm, scm, gm)
        else:
            dx, dkv, dmm = att_bwd(dx, i, sm, dkv, shm, scm, gm)
        if i == 0:
            g1 = g1 + reduce_step(depth - 1, dx)
        dx, dm1 = ffn_bwd(dx, i, 0, s1, sh1, sc1, g1)
        sg["mod"][i] = jnp.concatenate(list(dm1) + list(dmm) + list(dm2), axis=1)
        if i == n_a:
            d_bkv = colsum(dkv, name="kv_bias_bwd")
            hkv = norm_mod_fwd(x_kv, kv_norm_w[None], kv_scale, kv_shift, name="kv_norm_re")
            dh = mm(dkv, g_full("w_kv", 0), mode="nt", name="kv_proj_dx")
            wg["w_kv", 0] = mm(hkv, dkv, mode="tn", out_dtype=BF16, name="kv_proj_dw")
            dx, d_kvnorm, dsc, dsh = norm_mod_bwd(x_kv, kv_norm_w[None], kv_scale, dh, dx, name="kv_norm_bwd")
            d_kvmod = jnp.concatenate([dsh, dsc], axis=1)
        tok = reduce_step(depth - i if i == 0 else depth - 1 - i, dx)
    grad_x = dx[None]
    grads = {}

    small = {
        "ffn_norm_w": jnp.stack([jnp.stack([r[0] for r in row]) for row in sg["ffn_norm_w"]]),
        "mod_b": jnp.stack([r[0] for r in sg["mod"]]),
        "mix_norm_w": jnp.stack([r[0] for r in sg["mix_norm_w"]]),
        "ssm_conv_w": jnp.stack(sg["ssm_conv_w"]),
        "ssm_conv_b": jnp.stack([r[0] for r in sg["ssm_conv_b"]]),
        "ssm_dt_bias": jnp.stack([r[0] for r in sg["ssm_dt_bias"]]),
        "ssm_a_log": jnp.stack([r[0] for r in sg["ssm_a_log"]]),
        "ssm_d": jnp.stack([r[0] for r in sg["ssm_d"]]),
        "ssm_norm_w": jnp.stack([r[0] for r in sg["ssm_norm_w"]]),
        "kv_norm_w": d_kvnorm[0],
        "kv_mod_b": d_kvmod[0],
        "b_kv": d_bkv[0],
        "attn_b_q": jnp.stack([r[0] for r in sg["attn_b_q"]]),
        "attn_sinks": jnp.stack([r[0] for r in sg["attn_sinks"]]),
        "attn_b_o": jnp.stack([r[0] for r in sg["attn_b_o"]]),
        "final_norm_w": d_final[0],
    }
    small_like = [small[n].shape for n in SMALL]
    sv_pack, sv_n = _pack([small[n] for n in SMALL])
    sv_all = ag8(sv_pack + tok, name="ag_small_g")
    sv_sum = reduce8(sv_all, name="small_g_sum").reshape(-1)[:sv_n]
    for n, gsum in zip(SMALL, _unpack(sv_sum, small_like)):
        grads[n] = chip_cols(gsum, W[n].shape[-1]) if n in SMALL_SHARDED else gsum

    per_dev = [_unpack(sv_all[b].reshape(-1)[:sv_n], small_like) for b in range(N_DEV)]
    i_modb, i_kvb = SMALL.index("mod_b"), SMALL.index("kv_mod_b")
    dmod_all = jnp.stack([chip_cols(p[i_modb], MW) for p in per_dev], axis=1)
    dkv_all = jnp.stack([chip_cols(p[i_kvb], KW) for p in per_dev], axis=0)[None]
    c_t = jnp.transpose(c_all)
    grads["mod_w"] = outer8(c_t, dmod_all, name="mod_w_grad")
    grads["kv_mod_w"] = outer8(c_t, dkv_all, name="kv_mod_w_grad")[0]

    delta, new_m, new_v = {}, {}, {}
    for n in COLUMN_PARALLEL:
        delta[n], new_m[n], new_v[n] = adamw(W[n], grads[n], MOM[n], VAR[n], name=f"adamw_{n}")
    like = [W[n].shape for n in SMALL]
    packs = [_pack([d[n] for n in SMALL])[0] for d in (W, grads, MOM, VAR)]
    n_small = sum(int(W[n].size) for n in SMALL)
    for dst, res in zip((delta, new_m, new_v), adamw(*packs, name="adamw_small")):
        for n, a in zip(SMALL, _unpack(res.reshape(-1)[:n_small], like)):
            dst[n] = a

    reduce_end(delta["mod_w"])
    for n, s in zip(GATHERED, rs_share([gfull[n] for n in GATHERED], name="rs_share")):
        grads[n] = s.reshape(W[n].shape)
        delta[n], new_m[n], new_v[n] = adamw(W[n], grads[n], MOM[n], VAR[n], name=f"adamw_{n}")

    return (loss, grad_x, *[grads[n] for n in WEIGHTS], *[delta[n] for n in WEIGHTS], *[new_m[n] for n in WEIGHTS],
            *[new_v[n] for n in WEIGHTS])
```

```python
import functools

import jax
import jax.numpy as jnp
from jax import lax
from jax.experimental import pallas as pl
from jax.experimental.pallas import tpu as pltpu

F32 = jnp.float32
BF16 = jnp.bfloat16
HIGHEST = lax.Precision.HIGHEST
MESH = pl.DeviceIdType.MESH

EPS = 1e-5
N_MOD = 9
FFN_HALF = 0.5
SSM_HEADDIM = 64
SSM_GROUPS = 8
SSM_STATE = 128
CONV_WIDTH = 4
CHUNK = 128
KV_HEADS = 4
HEAD_DIM = 64
WINDOW = 128
N_CHIPS = 4
N_DEV = 8

ADAM_LR = 0.001
ADAM_B1 = 0.9
ADAM_B2 = 0.999
ADAM_EPS = 1e-08
ADAM_WD = 0.01
ADAM_STEP = 10

LANE = 128
MM_TILE = 1024


def _pick(n, pref, align, whole_if_small=False):
    best = 0
    t = align
    while t <= min(n, pref):
        if n % t == 0:
            best = t
        t += align
    if best == 0 or (whole_if_small and best < 256 and n <= 2048):
        return n
    return best


def _sigmoid(x):
    return 1.0 / (1.0 + jnp.exp(-x))


def _silu(x):
    return x * _sigmoid(x)


def _dsilu(x):
    s = _sigmoid(x)
    return s * (1.0 + x * (1.0 - s))


def _params(*sem):
    return pltpu.CompilerParams(dimension_semantics=sem)


def mm(a, b, *, mode="nn", reduce_s=False, out_dtype=F32, bias=None, resid=None, name):
    a_s = a.ndim == 3
    b_s = b.ndim == 3
    S = a.shape[0] if a_s else (b.shape[0] if b_s else 1)
    a2 = a.shape[-2:]
    b2 = b.shape[-2:]
    if mode == "nn":
        (M, K), (K2, N) = a2, b2
    elif mode == "nt":
        (M, K), (N, K2) = a2, b2
    else:
        (K, M), (K2, N) = a2, b2
    assert K == K2, (a.shape, b.shape, mode)
    batch = (a_s or b_s) and not reduce_s
    sb = S if batch else 1
    sr = S if ((a_s or b_s) and reduce_s) else 1
    tm = _pick(M, MM_TILE // 2 if resid is not None else MM_TILE, LANE if mode == "tn" else 16, True)
    tn = _pick(N, MM_TILE, LANE, True)
    tk = _pick(K, 2 * MM_TILE if mode == "tn" else MM_TILE, LANE if mode != "tn" else 16, True)
    nk = K // tk
    grid = (sb, M // tm, N // tn, sr, nk)

    def s_of(isb, isr):
        return isb if batch else isr

    def a_map(isb, i, j, isr, k):
        idx = (k, i) if mode == "tn" else (i, k)
        return ((s_of(isb, isr),) + idx) if a_s else idx

    def b_map(isb, i, j, isr, k):
        idx = (j, k) if mode == "nt" else (k, j)
        return ((s_of(isb, isr),) + idx) if b_s else idx

    def o_map(isb, i, j, isr, k):
        return (isb, i, j) if batch else (i, j)

    def s_blk(has_s, blk):
        return ((None,) + blk) if has_s else blk

    a_blk = (tk, tm) if mode == "tn" else (tm, tk)
    b_blk = (tn, tk) if mode == "nt" else (tk, tn)
    in_specs = [pl.BlockSpec(s_blk(a_s, a_blk), a_map), pl.BlockSpec(s_blk(b_s, b_blk), b_map)]
    args = [a, b]
    if bias is not None:
        bias_s = bias.ndim == 3
        in_specs.append(
            pl.BlockSpec(
                ((None, 1, tn) if bias_s else (1, tn)),
                (lambda isb, i, j, isr, k: (isb, 0, j)) if bias_s else (lambda isb, i, j, isr, k: (0, j)),
            )
        )
        args.append(bias)
    o_spec = pl.BlockSpec(s_blk(batch, (tm, tn)), o_map)
    out_shape = jax.ShapeDtypeStruct(((sb, M, N) if batch else (M, N)), out_dtype)
    out_specs = o_spec
    if resid is not None:
        assert not batch
        x_res, gate, scale = resid
        in_specs += [pl.BlockSpec((tm, tn), o_map), pl.BlockSpec((1, tn), lambda isb, i, j, isr, k: (0, j))]
        args += [x_res, gate]
        out_shape = (out_shape, jax.ShapeDtypeStruct((M, N), F32))
        out_specs = (o_spec, pl.BlockSpec((tm, tn), o_map))
    dims = {"nn": (((1,), (0,)), ((), ())), "nt": (((1,), (1,)), ((), ())), "tn": (((0,), (0,)), ((), ()))}[mode]
    n_in = len(args)
    n_out = 2 if resid is not None else 1
    one_step = sr * nk == 1

    def body(*refs):
        a_ref, b_ref = refs[0], refs[1]
        bias_ref = refs[2] if bias is not None else None
        o_ref = refs[n_in]

        def finish(r):
            if bias is not None:
                r = r + bias_ref[...]
            o_ref[...] = r.astype(o_ref.dtype)
            if resid is not None:
                refs[n_in + 1][...] = refs[n_in - 2][...] + (scale * refs[n_in - 1][...]) * r

        def part():
            return lax.dot_general(a_ref[...].astype(BF16), b_ref[...].astype(BF16), dims, preferred_element_type=F32)

        if one_step:
            finish(part())
            return
        acc = refs[n_in + n_out]
        isr = pl.program_id(3)
        k = pl.program_id(4)

        @pl.when((isr == 0) & (k == 0))
        def _():
            acc[...] = jnp.zeros_like(acc)

        acc[...] += part()

        @pl.when((isr == sr - 1) & (k == nk - 1))
        def _():
            finish(acc[...])

    return pl.pallas_call(
        body,
        out_shape=out_shape,
        grid=grid,
        in_specs=in_specs,
        out_specs=out_specs,
        scratch_shapes=[] if one_step else [pltpu.VMEM((tm, tn), F32)],
        compiler_params=_params("parallel", "parallel", "parallel", "arbitrary", "arbitrary"),
        name=name,
    )(*args)


def norm_mod_fwd(x, nw, sc, sh, *, name):
    L, D = x.shape
    tl = _pick(L, 512, 16)

    def body(x_ref, nw_ref, sc_ref, sh_ref, h_ref):
        xv = x_ref[...]
        r = lax.rsqrt(jnp.mean(xv * xv, axis=-1, keepdims=True) + EPS)
        n = (xv * r) * nw_ref[...]
        h_ref[...] = (n * (1.0 + sc_ref[...]) + sh_ref[...]).astype(h_ref.dtype)

    row = pl.BlockSpec((1, D), lambda i: (0, 0))
    return pl.pallas_call(
        body,
        out_shape=jax.ShapeDtypeStruct((L, D), BF16),
        grid=(L // tl,),
        in_specs=[pl.BlockSpec((tl, D), lambda i: (i, 0)), row, row, row],
        out_specs=pl.BlockSpec((tl, D), lambda i: (i, 0)),
        compiler_params=_params("parallel"),
        name=name,
    )(x, nw, sc, sh)


def norm_mod_bwd(x, nw, sc, dh, dx_in, *, name):
    L, D = x.shape
    tl = _pick(L, 512, 16)

    def body(x_ref, nw_ref, sc_ref, dh_ref, dxi_ref, dx_ref, dnw_ref, dsc_ref, dsh_ref):
        @pl.when(pl.program_id(0) == 0)
        def _():
            dnw_ref[...] = jnp.zeros_like(dnw_ref)
            dsc_ref[...] = jnp.zeros_like(dsc_ref)
            dsh_ref[...] = jnp.zeros_like(dsh_ref)

        xv = x_ref[...]
        dh_v = dh_ref[...]
        r = lax.rsqrt(jnp.mean(xv * xv, axis=-1, keepdims=True) + EPS)
        xhat = xv * r
        nw_v = nw_ref[...]
        n = xhat * nw_v
        dsh_ref[...] += jnp.sum(dh_v, axis=0, keepdims=True)
        dsc_ref[...] += jnp.sum(dh_v * n, axis=0, keepdims=True)
        dn = dh_v * (1.0 + sc_ref[...])
        dnw_ref[...] += jnp.sum(dn * xhat, axis=0, keepdims=True)
        dxhat = dn * nw_v
        dx_ref[...] = dxi_ref[...] + r * (dxhat - xhat * jnp.mean(dxhat * xhat, axis=-1, keepdims=True))

    row = pl.BlockSpec((1, D), lambda i: (0, 0))
    tile = pl.BlockSpec((tl, D), lambda i: (i, 0))
    vec = jax.ShapeDtypeStruct((1, D), F32)
    return pl.pallas_call(
        body,
        out_shape=(jax.ShapeDtypeStruct((L, D), F32), vec, vec, vec),
        grid=(L // tl,),
        in_specs=[tile, row, row, tile, tile],
        out_specs=(tile, row, row, row),
        compiler_params=_params("arbitrary"),
        name=name,
    )(x, nw, sc, dh, dx_in)


def gate_bwd(f, dx, gate, scale, *, name):
    L, D = f.shape
    tl = _pick(L, 512, 16)

    def body(f_ref, dx_ref, g_ref, df_ref, dg_ref, dfsum_ref):
        @pl.when(pl.program_id(0) == 0)
        def _():
            dg_ref[...] = jnp.zeros_like(dg_ref)
            dfsum_ref[...] = jnp.zeros_like(dfsum_ref)

        dxv = dx_ref[...]
        df = (scale * g_ref[...]) * dxv
        df_ref[...] = df.astype(df_ref.dtype)
        dfsum_ref[...] += jnp.sum(df, axis=0, keepdims=True)
        dg_ref[...] += scale * jnp.sum(f_ref[...] * dxv, axis=0, keepdims=True)

    tile = pl.BlockSpec((tl, D), lambda i: (i, 0))
    row = pl.BlockSpec((1, D), lambda i: (0, 0))
    vec = jax.ShapeDtypeStruct((1, D), F32)
    return pl.pallas_call(
        body,
        out_shape=(jax.ShapeDtypeStruct((L, D), BF16), vec, vec),
        grid=(L // tl,),
        in_specs=[tile, tile, row],
        out_specs=(tile, row, row),
        compiler_params=_params("arbitrary"),
        name=name,
    )(f, dx, gate)


def colsum(x, *, name):
    L, N = x.shape
    tl = _pick(L, 512, 8)

    def body(x_ref, o_ref):
        @pl.when(pl.program_id(0) == 0)
        def _():
            o_ref[...] = jnp.zeros_like(o_ref)

        o_ref[...] += jnp.sum(x_ref[...], axis=0, keepdims=True)

    return pl.pallas_call(
        body,
        out_shape=jax.ShapeDtypeStruct((1, N), F32),
        grid=(L // tl,),
        in_specs=[pl.BlockSpec((tl, N), lambda i: (i, 0))],
        out_specs=pl.BlockSpec((1, N), lambda i: (0, 0)),
        compiler_params=_params("arbitrary"),
        name=name,
    )(x)


def ffn_up(h, wgu, *, name):
    L, D = h.shape
    T = wgu.shape[-1]
    tm = _pick(L, 512, 16)

    def body(h_ref, w_ref, gu_ref, a_ref):
        hb = h_ref[...].astype(BF16)
        g = jnp.dot(hb, w_ref[0].astype(BF16), preferred_element_type=F32)
        u = jnp.dot(hb, w_ref[1].astype(BF16), preferred_element_type=F32)
        gu_ref[0] = g
        gu_ref[1] = u
        a_ref[...] = (_silu(g) * u).astype(a_ref.dtype)

    gu, a = pl.pallas_call(
        body,
        out_shape=(jax.ShapeDtypeStruct((2, 2, L, T), F32), jax.ShapeDtypeStruct((2, L, T), BF16)),
        grid=(2, L // tm),
        in_specs=[
            pl.BlockSpec((tm, D), lambda j, i: (i, 0)),
            pl.BlockSpec((2, None, D, T), lambda j, i: (0, j, 0, 0)),
        ],
        out_specs=(
            pl.BlockSpec((2, None, tm, T), lambda j, i: (0, j, i, 0)),
            pl.BlockSpec((None, tm, T), lambda j, i: (j, i, 0)),
        ),
        compiler_params=_params("parallel", "parallel"),
        name=name,
    )(h, wgu.reshape(2, 2, D, T))
    return gu.reshape(4, L, T), a


def ffn_down_bwd(df, wdn, gu, *, name):
    L, D = df.shape
    T = wdn.shape[1]
    tm = _pick(L, 512, 16)

    def body(df_ref, w_ref, gu_ref, d_ref):
        da = lax.dot_general(
            df_ref[...].astype(BF16), w_ref[...].astype(BF16), (((1,), (1,)), ((), ())), preferred_element_type=F32
        )
        g = gu_ref[0]
        d_ref[0] = (da * gu_ref[1] * _dsilu(g)).astype(d_ref.dtype)
        d_ref[1] = (da * _silu(g)).astype(d_ref.dtype)

    out = pl.pallas_call(
        body,
        out_shape=jax.ShapeDtypeStruct((2, 2, L, T), BF16),
        grid=(2, L // tm),
        in_specs=[
            pl.BlockSpec((tm, D), lambda j, i: (i, 0)),
            pl.BlockSpec((None, T, D), lambda j, i: (j, 0, 0)),
            pl.BlockSpec((2, None, tm, T), lambda j, i: (0, j, i, 0)),
        ],
        out_specs=pl.BlockSpec((2, None, tm, T), lambda j, i: (0, j, i, 0)),
        compiler_params=_params("parallel", "parallel"),
        name=name,
    )(df, wdn, gu.reshape(2, 2, L, T))
    return out.reshape(4, L, T)


def _shift_down(u, k, rows):
    if k == 0:
        return u
    return jnp.where(rows >= k, pltpu.roll(u, k, 0), 0.0)


def _shift_up(u, k, rows, n):
    if k == 0:
        return u
    return jnp.where(rows < n - k, pltpu.roll(u, n - k, 0), 0.0)


def _conv_pre(u, w_ref, b_ref, rows):
    pre = b_ref[...] + w_ref[CONV_WIDTH - 1 : CONV_WIDTH, :] * u
    for k in range(1, CONV_WIDTH):
        pre = pre + w_ref[CONV_WIDTH - 1 - k : CONV_WIDTH - k, :] * _shift_down(u, k, rows)
    return pre


def conv_fwd(zx, conv_w, conv_b, d_inner, *, name):
    L = zx.shape[0]
    C = conv_w.shape[1]
    tc = 256
    off = d_inner // tc

    def body(u_ref, w_ref, b_ref, o_ref):
        rows = lax.broadcasted_iota(jnp.int32, (L, tc), 0)
        o_ref[...] = _silu(_conv_pre(u_ref[...], w_ref, b_ref, rows))

    return pl.pallas_call(
        body,
        out_shape=jax.ShapeDtypeStruct((L, C), F32),
        grid=(C // tc,),
        in_specs=[
            pl.BlockSpec((L, tc), lambda j: (0, off + j)),
            pl.BlockSpec((CONV_WIDTH, tc), lambda j: (0, j)),
            pl.BlockSpec((1, tc), lambda j: (0, j)),
        ],
        out_specs=pl.BlockSpec((L, tc), lambda j: (0, j)),
        compiler_params=_params("parallel"),
        name=name,
    )(zx, conv_w, conv_b)


def conv_bwd(zx, conv_w, conv_b, dxbc, d_inner, *, name):
    L = zx.shape[0]
    C = conv_w.shape[1]
    tc = 256
    off = d_inner // tc

    def body(u_ref, w_ref, b_ref, d_ref, du_ref, dw_ref, db_ref):
        rows = lax.broadcasted_iota(jnp.int32, (L, tc), 0)
        u = u_ref[...]
        dpre = d_ref[...] * _dsilu(_conv_pre(u, w_ref, b_ref, rows))
        db_ref[...] = jnp.sum(dpre, axis=0, keepdims=True)
        du = w_ref[CONV_WIDTH - 1 : CONV_WIDTH, :] * dpre
        dw_ref[CONV_WIDTH - 1 : CONV_WIDTH, :] = jnp.sum(dpre * u, axis=0, keepdims=True)
        for k in range(1, CONV_WIDTH):
            j = CONV_WIDTH - 1 - k
            dw_ref[j : j + 1, :] = jnp.sum(dpre * _shift_down(u, k, rows), axis=0, keepdims=True)
            du = du + w_ref[j : j + 1, :] * _shift_up(dpre, k, rows, L)
        du_ref[...] = du

    return pl.pallas_call(
        body,
        out_shape=(
            jax.ShapeDtypeStruct((L, C), F32),
            jax.ShapeDtypeStruct((CONV_WIDTH, C), F32),
            jax.ShapeDtypeStruct((1, C), F32),
        ),
        grid=(C // tc,),
        in_specs=[
            pl.BlockSpec((L, tc), lambda j: (0, off + j)),
            pl.BlockSpec((CONV_WIDTH, tc), lambda j: (0, j)),
            pl.BlockSpec((1, tc), lambda j: (0, j)),
            pl.BlockSpec((L, tc), lambda j: (0, j)),
        ],
        out_specs=(
            pl.BlockSpec((L, tc), lambda j: (0, j)),
            pl.BlockSpec((CONV_WIDTH, tc), lambda j: (0, j)),
            pl.BlockSpec((1, tc), lambda j: (0, j)),
        ),
        compiler_params=_params("parallel"),
        name=name,
    )(zx, conv_w, conv_b, dxbc)


def _ssd_head(xs, dt, acs, tot, dsk, cb, bm, cm, prev):
    q = xs.shape[0]
    li = lax.broadcasted_iota(jnp.int32, (q, q), 0)
    si = lax.broadcasted_iota(jnp.int32, (q, q), 1)
    causal = li >= si
    lmat = jnp.exp(jnp.where(causal, acs - acs.T, -jnp.inf))
    xdt = xs * dt
    y = jnp.dot((cb * lmat).astype(BF16), xdt.astype(BF16), preferred_element_type=F32)
    y = y + lax.dot_general(
        (cm * jnp.exp(acs)).astype(BF16), prev.astype(BF16), (((1,), (1,)), ((), ())), preferred_element_type=F32
    )
    y = y + dsk * xs
    st = lax.dot_general(
        xdt.astype(BF16), (bm * jnp.exp(tot - acs)).astype(BF16), (((0,), (0,)), ((), ())), preferred_element_type=F32
    )
    return y, prev * jnp.exp(tot) + st


def _pick_lane(v, h):
    lanes = lax.broadcasted_iota(jnp.int32, v.shape, 1)
    return jnp.sum(jnp.where(lanes == h, v, 0.0), axis=1, keepdims=True)


def _tri_cols(cols, upper):
    q = cols[0].shape[0]
    assert 3 * len(cols) <= LANE
    li = lax.broadcasted_iota(jnp.int32, (q, q), 0)
    si = lax.broadcasted_iota(jnp.int32, (q, q), 1)
    tri = ((li <= si) if upper else (li >= si)).astype(BF16)
    lanes = lax.broadcasted_iota(jnp.int32, (q, LANE), 1)
    rhs = jnp.zeros((q, LANE), F32)
    for r, col in enumerate(cols):
        hi = col.astype(BF16).astype(F32)
        mid = (col - hi).astype(BF16).astype(F32)
        lo = col - hi - mid
        for t, term in enumerate((hi, mid, lo)):
            rhs = jnp.where(lanes == 3 * r + t, term, rhs)
    out = jnp.dot(tri, rhs.astype(BF16), preferred_element_type=F32)
    return [jnp.sum(jnp.where((lanes >= 3 * r) & (lanes < 3 * r + 3), out, 0.0), axis=1, keepdims=True)
            for r in range(len(cols))]


def _softplus(x):
    return jnp.maximum(x, 0.0) + jnp.log(1.0 + jnp.exp(-jnp.abs(x)))


def _ssd_specs(L, d_inner, H, nc, rev):
    R = H // SSM_GROUPS
    P, N, Q = SSM_HEADDIM, SSM_STATE, CHUNK
    ngrp = SSM_GROUPS

    def ci(c):
        return (nc - 1 - c) if rev else c

    b_off = d_inner // N
    c_off = b_off + ngrp
    xs = pl.BlockSpec((Q, R * P), lambda c, g: (ci(c), g))
    bm = pl.BlockSpec((Q, N), lambda c, g: (ci(c), b_off + g))
    cm = pl.BlockSpec((Q, N), lambda c, g: (ci(c), c_off + g))
    dt = pl.BlockSpec((Q, H), lambda c, g: (ci(c), 0))
    hv = pl.BlockSpec((1, H), lambda c, g: (0, 0))
    y = pl.BlockSpec((Q, R * P), lambda c, g: (ci(c), g))
    st = pl.BlockSpec((None, R * P, N), lambda c, g: (ci(c), g, 0))
    return R, xs, bm, cm, dt, hv, y, st


def ssd_fwd(xbc, dt_raw, dt_bias, a_log, d_skip, d_inner, *, name):
    L = xbc.shape[0]
    H = dt_raw.shape[1]
    nc = L // CHUNK
    P, N = SSM_HEADDIM, SSM_STATE
    R, xs_s, bm_s, cm_s, dt_s, hv_s, y_s, st_s = _ssd_specs(L, d_inner, H, nc, False)

    def body(xs_ref, bm_ref, cm_ref, dt_ref, bias_ref, alog_ref, dsk_ref, y_ref, st_ref, state):
        c = pl.program_id(0)
        g = pl.program_id(1)

        @pl.when(c == 0)
        def _():
            for r in range(R):
                state[g * R + r] = jnp.zeros((P, N), F32)

        dtb = _softplus(dt_ref[...] + bias_ref[...])
        a_all = -jnp.exp(alog_ref[...])
        bm, cm = bm_ref[...], cm_ref[...]
        cb = lax.dot_general(cm.astype(BF16), bm.astype(BF16), (((1,), (1,)), ((), ())), preferred_element_type=F32)
        dts = [_pick_lane(dtb, g * R + r) for r in range(R)]
        a_cols = [dts[r] * _pick_lane(a_all, g * R + r) for r in range(R)]
        acs = _tri_cols(a_cols, upper=False)
        prevs = [state[g * R + r] for r in range(R)]
        res = []
        for r in range(R):
            res.append(_ssd_head(
                xs_ref[:, r * P : (r + 1) * P],
                dts[r],
                jnp.broadcast_to(acs[r], (CHUNK, CHUNK)),
                jnp.sum(a_cols[r], axis=0, keepdims=True),
                _pick_lane(dsk_ref[...], g * R + r),
                cb,
                bm,
                cm,
                prevs[r],
            ))
        for r in range(R):
            st_ref[r * P : (r + 1) * P, :] = prevs[r]
            y_ref[:, r * P : (r + 1) * P] = res[r][0]
            state[g * R + r] = res[r][1]

    return pl.pallas_call(
        body,
        out_shape=(jax.ShapeDtypeStruct((L, d_inner), F32), jax.ShapeDtypeStruct((nc, H * P, N), F32)),
        grid=(nc, SSM_GROUPS),
        in_specs=[xs_s, bm_s, cm_s, dt_s, hv_s, hv_s, hv_s],
        out_specs=(y_s, st_s),
        scratch_shapes=[pltpu.VMEM((H, P, N), F32)],
        compiler_params=_params("arbitrary", "arbitrary"),
        name=name,
    )(xbc, xbc, xbc, dt_raw, dt_bias, a_log, d_skip)


def ssd_bwd(xbc, dt_raw, dt_bias, a_log, d_skip, states, dy, d_inner, *, name):
    L, C = xbc.shape
    H = dt_raw.shape[1]
    nc = L // CHUNK
    P, N, Q = SSM_HEADDIM, SSM_STATE, CHUNK
    R, xs_s, bm_s, cm_s, dt_s, hv_s, y_s, st_s = _ssd_specs(L, d_inner, H, nc, True)

    def body(xs_ref, bm_ref, cm_ref, dt_ref, bias_ref, alog_ref, dsk_ref, st_ref, dy_ref,
             dxs_ref, dbm_ref, dcm_ref, ddt_ref, dbias_ref, dalog_ref, ddsk_ref, dstate):
        c = pl.program_id(0)
        g = pl.program_id(1)

        @pl.when(c == 0)
        def _():
            for r in range(R):
                dstate[g * R + r] = jnp.zeros((P, N), F32)

        @pl.when((c == 0) & (g == 0))
        def _():
            dbias_ref[...] = jnp.zeros_like(dbias_ref)
            dalog_ref[...] = jnp.zeros_like(dalog_ref)
            ddsk_ref[...] = jnp.zeros_like(ddsk_ref)

        @pl.when(g == 0)
        def _():
            ddt_ref[...] = jnp.zeros_like(ddt_ref)

        pre = dt_ref[...] + bias_ref[...]
        dtb = _softplus(pre)
        a_all = -jnp.exp(alog_ref[...])
        lanes_q = lax.broadcasted_iota(jnp.int32, (Q, H), 1)
        lanes_1 = lax.broadcasted_iota(jnp.int32, (1, H), 1)
        bm = bm_ref[...]
        cm = cm_ref[...]
        nt = (((1,), (1,)), ((), ()))
        cb = lax.dot_general(cm.astype(BF16), bm.astype(BF16), nt, preferred_element_type=F32)
        dts = [_pick_lane(dtb, g * R + r) for r in range(R)]
        a_negs = [_pick_lane(a_all, g * R + r) for r in range(R)]
        a_cols = [dts[r] * a_negs[r] for r in range(R)]
        acs = _tri_cols(a_cols, upper=False)
        dbm = jnp.zeros((Q, N), F32)
        dcm = jnp.zeros((Q, N), F32)
        dcb = jnp.zeros((Q, Q), F32)
        dd_row = jnp.zeros((1, H), F32)
        dstates = [dstate[g * R + r] for r in range(R)]
        dprevs, ddts, dacs_cols, dtots = [], [], [], []
        for r in range(R):
            h = g * R + r
            args = (
                xs_ref[:, r * P : (r + 1) * P],
                dts[r],
                jnp.broadcast_to(acs[r], (Q, Q)),
                jnp.sum(a_cols[r], axis=0, keepdims=True),
                _pick_lane(dsk_ref[...], h),
                cb,
                bm,
                cm,
                st_ref[r * P : (r + 1) * P, :],
            )
            _, vjp = jax.vjp(_ssd_head, *args)
            dxs, ddt, dacs, dtot, dd, dcb_h, dbm_h, dcm_h, dprev = vjp((dy_ref[:, r * P : (r + 1) * P], dstates[r]))
            dxs_ref[:, r * P : (r + 1) * P] = dxs
            dprevs.append(dprev)
            ddts.append(ddt)
            dacs_cols.append(jnp.sum(dacs, axis=1, keepdims=True))
            dtots.append(dtot)
            dbm = dbm + dbm_h
            dcm = dcm + dcm_h
            dcb = dcb + dcb_h
            dd_row = dd_row + jnp.where(lanes_1 == h, dd, 0.0)
        for r in range(R):
            dstate[g * R + r] = dprevs[r]
        ddt_blk = jnp.zeros((Q, H), F32)
        da_row = jnp.zeros((1, H), F32)
        for r, da_col in enumerate(_tri_cols(dacs_cols, upper=True)):
            h = g * R + r
            da_col = da_col + dtots[r]
            ddt_blk = ddt_blk + jnp.where(lanes_q == h, ddts[r] + da_col * a_negs[r], 0.0)
            da_row = da_row + jnp.where(lanes_1 == h, jnp.sum(da_col * dts[r], axis=0, keepdims=True), 0.0)
        dcb16 = dcb.astype(BF16)
        dbm_ref[...] = dbm + lax.dot_general(dcb16, cm.astype(BF16), (((0,), (0,)), ((), ())), preferred_element_type=F32)
        dcm_ref[...] = dcm + jnp.dot(dcb16, bm.astype(BF16), preferred_element_type=F32)
        ddt_pre = ddt_blk * _sigmoid(pre)
        ddt_ref[...] += ddt_pre
        dbias_ref[...] += jnp.sum(ddt_pre, axis=0, keepdims=True)
        dalog_ref[...] += da_row * a_all
        ddsk_ref[...] += dd_row

    ngrp = SSM_GROUPS
    hrow = jax.ShapeDtypeStruct((1, H), F32)
    dxs, dbm, dcm, ddt, dbias, dalog, ddsk = pl.pallas_call(
        body,
        out_shape=(
            jax.ShapeDtypeStruct((L, d_inner), F32),
            jax.ShapeDtypeStruct((L, ngrp * N), F32),
            jax.ShapeDtypeStruct((L, ngrp * N), F32),
            jax.ShapeDtypeStruct((L, H), F32),
            hrow,
            hrow,
            hrow,
        ),
        grid=(nc, ngrp),
        in_specs=[xs_s, bm_s, cm_s, dt_s, hv_s, hv_s, hv_s, st_s, y_s],
        out_specs=(
            y_s,
            pl.BlockSpec((Q, N), lambda c, g: (nc - 1 - c, g)),
            pl.BlockSpec((Q, N), lambda c, g: (nc - 1 - c, g)),
            dt_s,
            hv_s,
            hv_s,
            hv_s,
        ),
        scratch_shapes=[pltpu.VMEM((H, P, N), F32)],
        compiler_params=_params("arbitrary", "arbitrary"),
        name=name,
    )(xbc, xbc, xbc, dt_raw, dt_bias, a_log, d_skip, states, dy)
    return jnp.concatenate([dxs, dbm, dcm], axis=1), ddt, dbias, dalog, ddsk


def gnorm_fwd(y, zx, nw, *, name):
    L, DI = y.shape
    gw = DI // SSM_GROUPS
    tl = _pick(L, 512, 16)

    def body(y_ref, z_ref, nw_ref, o_ref):
        yz = y_ref[...] * _silu(z_ref[...])
        r = lax.rsqrt(jnp.mean(yz * yz, axis=-1, keepdims=True) + EPS)
        o_ref[...] = ((yz * r) * nw_ref[...]).astype(o_ref.dtype)

    tile = pl.BlockSpec((tl, gw), lambda i, g: (i, g))
    return pl.pallas_call(
        body,
        out_shape=jax.ShapeDtypeStruct((L, DI), BF16),
        grid=(L // tl, SSM_GROUPS),
        in_specs=[tile, tile, pl.BlockSpec((1, gw), lambda i, g: (0, g))],
        out_specs=tile,
        compiler_params=_params("parallel", "parallel"),
        name=name,
    )(y, zx, nw)


def gnorm_bwd(y, zx, nw, dout, *, name):
    L, DI = y.shape
    gw = DI // SSM_GROUPS
    tl = _pick(L, 512, 16)

    def body(y_ref, z_ref, nw_ref, do_ref, dy_ref, dz_ref, dnw_ref):
        @pl.when(pl.program_id(1) == 0)
        def _():
            dnw_ref[...] = jnp.zeros_like(dnw_ref)

        yv = y_ref[...]
        zv = z_ref[...]
        sz = _silu(zv)
        yz = yv * sz
        r = lax.rsqrt(jnp.mean(yz * yz, axis=-1, keepdims=True) + EPS)
        n = yz * r
        dov = do_ref[...]
        dnw_ref[...] += jnp.sum(dov * n, axis=0, keepdims=True)
        dn = dov * nw_ref[...]
        dyz = r * (dn - n * jnp.mean(dn * n, axis=-1, keepdims=True))
        dy_ref[...] = dyz * sz
        dz_ref[...] = dyz * yv * _dsilu(zv)

    tile = pl.BlockSpec((tl, gw), lambda g, i: (i, g))
    row = pl.BlockSpec((1, gw), lambda g, i: (0, g))
    return pl.pallas_call(
        body,
        out_shape=(
            jax.ShapeDtypeStruct((L, DI), F32),
            jax.ShapeDtypeStruct((L, DI), F32),
            jax.ShapeDtypeStruct((1, DI), F32),
        ),
        grid=(SSM_GROUPS, L // tl),
        in_specs=[tile, tile, row, tile],
        out_specs=(tile, tile, row),
        compiler_params=_params("parallel", "arbitrary"),
        name=name,
    )(y, zx, nw, dout)


def _attn_head(q, kp, kc, vp, vc, sink, has_prev):
    rows, w = q.shape[0], kc.shape[0]
    nt = (((1,), (1,)), ((), ()))
    qb = q.astype(BF16)
    sc = lax.dot_general(qb, kc.astype(BF16), nt, preferred_element_type=F32) * HEAD_DIM ** -0.5
    sp = lax.dot_general(qb, kp.astype(BF16), nt, preferred_element_type=F32) * HEAD_DIM ** -0.5
    ii = jnp.bitwise_and(lax.broadcasted_iota(jnp.int32, (rows, w), 0), w - 1)
    jj = lax.broadcasted_iota(jnp.int32, (rows, w), 1)
    lc = jnp.where(jj <= ii, sc, -jnp.inf)
    lp = jnp.where((jj > ii) & has_prev, sp, -jnp.inf)
    m = jnp.maximum(jnp.maximum(jnp.max(lc, axis=1, keepdims=True), jnp.max(lp, axis=1, keepdims=True)), sink)
    m = lax.stop_gradient(m)
    pc = jnp.exp(lc - m)
    pp = jnp.exp(lp - m)
    denom = jnp.sum(pc, axis=1, keepdims=True) + jnp.sum(pp, axis=1, keepdims=True) + jnp.exp(sink - m)
    o = jnp.dot((pc / denom).astype(BF16), vc.astype(BF16), preferred_element_type=F32)
    return o + jnp.dot((pp / denom).astype(BF16), vp.astype(BF16), preferred_element_type=F32)


def attn_fwd(q, kv, sinks, *, name):
    L, DQ = q.shape
    heads = DQ // HEAD_DIM
    rep = heads // KV_HEADS
    nb = L // WINDOW
    kw = KV_HEADS * HEAD_DIM
    W, HD = WINDOW, HEAD_DIM

    def body(q_ref, kp_ref, kc_ref, vp_ref, vc_ref, s_ref, o_ref):
        has_prev = pl.program_id(0) > 0
        for kh in range(KV_HEADS):
            ks = slice(kh * HD, (kh + 1) * HD)
            hs = [kh * rep + rr for rr in range(rep)]
            o = _attn_head(
                jnp.concatenate([q_ref[:, h * HD : (h + 1) * HD] for h in hs], axis=0),
                kp_ref[:, ks], kc_ref[:, ks], vp_ref[:, ks], vc_ref[:, ks],
                jnp.concatenate([jnp.broadcast_to(s_ref[:, h : h + 1], (W, 1)) for h in hs], axis=0), has_prev,
            )
            for rr, h in enumerate(hs):
                o_ref[:, h * HD : (h + 1) * HD] = o[rr * W : (rr + 1) * W].astype(o_ref.dtype)

    return pl.pallas_call(
        body,
        out_shape=jax.ShapeDtypeStruct((L, DQ), BF16),
        grid=(nb,),
        in_specs=[
            pl.BlockSpec((W, DQ), lambda n: (n, 0)),
            pl.BlockSpec((W, kw), lambda n: (jnp.maximum(n - 1, 0), 0)),
            pl.BlockSpec((W, kw), lambda n: (n, 0)),
            pl.BlockSpec((W, kw), lambda n: (jnp.maximum(n - 1, 0), 1)),
            pl.BlockSpec((W, kw), lambda n: (n, 1)),
            pl.BlockSpec((1, heads), lambda n: (0, 0)),
        ],
        out_specs=pl.BlockSpec((W, DQ), lambda n: (n, 0)),
        compiler_params=_params("parallel"),
        name=name,
    )(q, kv, kv, kv, kv, sinks)


def attn_bwd(q, kv, sinks, do, dkv_in, *, name):
    L, DQ = q.shape
    heads = DQ // HEAD_DIM
    rep = heads // KV_HEADS
    nb = L // WINDOW
    kw = KV_HEADS * HEAD_DIM
    W, HD = WINDOW, HEAD_DIM

    def blk(n):
        return jnp.minimum(n, nb - 1)

    def prev(n):
        return jnp.maximum(blk(n) - 1, 0)

    def outb(n):
        return jnp.maximum(n - 1, 0)

    def body(q_ref, kp_ref, kc_ref, vp_ref, vc_ref, s_ref, do_ref, dki_ref, dvi_ref,
             dq_ref, dk_ref, dv_ref, ds_ref, dk_cur, dv_cur):
        n = pl.program_id(0)
        has_prev = n > 0

        @pl.when(n == 0)
        def _():
            ds_ref[...] = jnp.zeros_like(ds_ref)
            dk_cur[...] = jnp.zeros_like(dk_cur)
            dv_cur[...] = jnp.zeros_like(dv_cur)

        @pl.when(n == nb)
        def _():
            dk_ref[...] = dki_ref[...] + dk_cur[...]
            dv_ref[...] = dvi_ref[...] + dv_cur[...]

        @pl.when(n < nb)
        def _():
            lanes = lax.broadcasted_iota(jnp.int32, (1, heads), 1)
            ds_row = jnp.zeros((1, heads), F32)
            for kh in range(KV_HEADS):
                ks = slice(kh * HD, (kh + 1) * HD)
                hs = [kh * rep + rr for rr in range(rep)]
                _, vjp = jax.vjp(
                    functools.partial(_attn_head, has_prev=has_prev),
                    jnp.concatenate([q_ref[:, h * HD : (h + 1) * HD] for h in hs], axis=0),
                    kp_ref[:, ks], kc_ref[:, ks], vp_ref[:, ks], vc_ref[:, ks],
                    jnp.concatenate([jnp.broadcast_to(s_ref[:, h : h + 1], (W, 1)) for h in hs], axis=0),
                )
                dq, dkp, dkc, dvp, dvc, dsk = vjp(
                    jnp.concatenate([do_ref[:, h * HD : (h + 1) * HD] for h in hs], axis=0))
                for rr, h in enumerate(hs):
                    dq_ref[:, h * HD : (h + 1) * HD] = dq[rr * W : (rr + 1) * W]
                    ds_row = ds_row + jnp.where(lanes == h, jnp.sum(dsk[rr * W : (rr + 1) * W], axis=0, keepdims=True), 0.0)
                dk_ref[:, ks] = dki_ref[:, ks] + dk_cur[:, ks] + dkp
                dv_ref[:, ks] = dvi_ref[:, ks] + dv_cur[:, ks] + dvp
                dk_cur[:, ks] = dkc
                dv_cur[:, ks] = dvc
            ds_ref[...] += ds_row

    dq, dk, dv, ds = pl.pallas_call(
        body,
        out_shape=(
            jax.ShapeDtypeStruct((L, DQ), F32),
            jax.ShapeDtypeStruct((L, kw), F32),
            jax.ShapeDtypeStruct((L, kw), F32),
            jax.ShapeDtypeStruct((1, heads), F32),
        ),
        grid=(nb + 1,),
        in_specs=[
            pl.BlockSpec((W, DQ), lambda n: (blk(n), 0)),
            pl.BlockSpec((W, kw), lambda n: (prev(n), 0)),
            pl.BlockSpec((W, kw), lambda n: (blk(n), 0)),
            pl.BlockSpec((W, kw), lambda n: (prev(n), 1)),
            pl.BlockSpec((W, kw), lambda n: (blk(n), 1)),
            pl.BlockSpec((1, heads), lambda n: (0, 0)),
            pl.BlockSpec((W, DQ), lambda n: (blk(n), 0)),
            pl.BlockSpec((W, kw), lambda n: (outb(n), 0)),
            pl.BlockSpec((W, kw), lambda n: (outb(n), 1)),
        ],
        out_specs=(
            pl.BlockSpec((W, DQ), lambda n: (blk(n), 0)),
            pl.BlockSpec((W, kw), lambda n: (outb(n), 0)),
            pl.BlockSpec((W, kw), lambda n: (outb(n), 0)),
            pl.BlockSpec((1, heads), lambda n: (0, 0)),
        ),
        scratch_shapes=[pltpu.VMEM((W, kw), F32), pltpu.VMEM((W, kw), F32)],
        compiler_params=_params("arbitrary"),
        name=name,
    )(q, kv, kv, kv, kv, sinks, do, dkv_in, dkv_in)
    return dq, jnp.concatenate([dk, dv], axis=1), ds


def final_loss(x, fw, target, *, name):
    L, D = x.shape
    tl = _pick(L, 512, 8)

    def body(x_ref, fw_ref, t_ref, loss_ref, dx_ref, dfw_ref):
        @pl.when(pl.program_id(0) == 0)
        def _():
            loss_ref[...] = jnp.zeros_like(loss_ref)
            dfw_ref[...] = jnp.zeros_like(dfw_ref)

        xv = x_ref[...]
        fwv = fw_ref[...]
        r = lax.rsqrt(jnp.mean(xv * xv, axis=-1, keepdims=True) + EPS)
        xhat = xv * r
        err = xhat * fwv - t_ref[...]
        loss_ref[...] += 0.5 * jnp.sum(jnp.mean(err * err, axis=-1, keepdims=True), axis=0, keepdims=True)
        dy = err * (1.0 / D)
        dfw_ref[...] += jnp.sum(dy * xhat, axis=0, keepdims=True)
        dxhat = dy * fwv
        dx_ref[...] = r * (dxhat - xhat * jnp.mean(dxhat * xhat, axis=-1, keepdims=True))

    tile = pl.BlockSpec((tl, D), lambda i: (i, 0))
    row = pl.BlockSpec((1, D), lambda i: (0, 0))
    return pl.pallas_call(
        body,
        out_shape=(
            jax.ShapeDtypeStruct((1, 1), F32),
            jax.ShapeDtypeStruct((L, D), F32),
            jax.ShapeDtypeStruct((1, D), F32),
        ),
        grid=(L // tl,),
        in_specs=[tile, row, tile],
        out_specs=(pl.BlockSpec((1, 1), lambda i: (0, 0)), tile, row),
        compiler_params=_params("arbitrary"),
        name=name,
    )(x, fw, target)


def outer8(ct, d, *, name):
    D, B = ct.shape
    S, _, N = d.shape
    tm = _pick(D, 512, 8)
    tn = _pick(N, 256, LANE)

    def body(c_ref, d_ref, o_ref):
        acc = c_ref[:, 0:1] * d_ref[0:1, :]
        for b in range(1, B):
            acc = acc + c_ref[:, b : b + 1] * d_ref[b : b + 1, :]
        o_ref[...] = acc

    return pl.pallas_call(
        body,
        out_shape=jax.ShapeDtypeStruct((S, D, N), F32),
        grid=(S, D // tm, N // tn),
        in_specs=[
            pl.BlockSpec((tm, B), lambda s, i, j: (i, 0)),
            pl.BlockSpec((None, B, tn), lambda s, i, j: (s, 0, j)),
        ],
        out_specs=pl.BlockSpec((None, tm, tn), lambda s, i, j: (s, i, j)),
        compiler_params=_params("parallel", "parallel", "parallel"),
        name=name,
    )(ct, d)


def reduce8(g, *, name):
    nd, R, N = g.shape

    def body(g_ref, o_ref):
        acc = g_ref[0]
        for b in range(1, nd):
            acc = acc + g_ref[b]
        o_ref[...] = acc

    return pl.pallas_call(
        body,
        out_shape=jax.ShapeDtypeStruct((R, N), F32),
        name=name,
    )(g)


def _as3(a):
    if a.ndim == 1:
        return a.reshape(1, 1, -1)
    if a.ndim == 2:
        return a.reshape((1,) + a.shape)
    return a.reshape((-1,) + a.shape[-2:])


def adamw(w, g, m, v, *, name):
    shape = w.shape
    w3, g3, m3, v3 = _as3(w), _as3(g), _as3(m), _as3(v)
    B, R, C = w3.shape
    tr = _pick(R, max(8, (1 << 19) // max(C, 1) // 8 * 8), 8)

    def body(w_ref, g_ref, m_ref, v_ref, d_ref, nm_ref, nv_ref):
        gv = g_ref[...]
        mn = ADAM_B1 * m_ref[...] + (1.0 - ADAM_B1) * gv
        vn = ADAM_B2 * v_ref[...] + (1.0 - ADAM_B2) * (gv * gv)
        m_hat = mn / (1.0 - ADAM_B1 ** ADAM_STEP)
        v_hat = vn / (1.0 - ADAM_B2 ** ADAM_STEP)
        d_ref[...] = -ADAM_LR * (m_hat / (jnp.sqrt(v_hat) + ADAM_EPS) + ADAM_WD * w_ref[...])
        nm_ref[...] = mn
        nv_ref[...] = vn

    tile = pl.BlockSpec((None, tr, C), lambda b, i: (b, i, 0))
    sds = jax.ShapeDtypeStruct((B, R, C), F32)
    d, nm, nv = pl.pallas_call(
        body,
        out_shape=(sds, sds, sds),
        grid=(B, R // tr),
        in_specs=[tile, tile, tile, tile],
        out_specs=(tile, tile, tile),
        compiler_params=_params("parallel", "parallel"),
        name=name,
    )(w3, g3, m3, v3)
    return d.reshape(shape), nm.reshape(shape), nv.reshape(shape)


def _place():
    return lax.axis_index("x"), lax.axis_index("y"), lax.axis_index("c")


def _flip(v, bit):
    return (1 - v) if bit else v


def ag8(v, *, act=None, name):
    R, N = v.shape

    def body(v_ref, out_ref, stage, send_sems, recv_sems):
        x, y, c = _place()
        me = 4 * x + 2 * y + c
        val = v_ref[...]
        if act is not None:
            val = act(val)
        stage[...] = val
        out_ref[me] = val
        sends = []
        for k in range(1, N_DEV):
            px, py, pc = _flip(x, k & 4), _flip(y, k & 2), _flip(c, k & 1)
            cp = pltpu.make_async_remote_copy(
                src_ref=stage, dst_ref=out_ref.at[me], send_sem=send_sems.at[k - 1], recv_sem=recv_sems.at[k - 1],
                device_id=(px, py, pc), device_id_type=MESH,
            )
            cp.start()
            sends.append(cp)
        for k in range(1, N_DEV):
            px, py, pc = _flip(x, k & 4), _flip(y, k & 2), _flip(c, k & 1)
            pltpu.make_async_remote_copy(
                src_ref=stage, dst_ref=out_ref.at[4 * px + 2 * py + pc], send_sem=send_sems.at[k - 1],
                recv_sem=recv_sems.at[k - 1], device_id=(px, py, pc), device_id_type=MESH,
            ).wait_recv()
        for cp in sends:
            cp.wait_send()

    return pl.pallas_call(
        body,
        out_shape=jax.ShapeDtypeStruct((N_DEV, R, N), F32),
        in_specs=[pl.BlockSpec(memory_space=pltpu.VMEM)],
        out_specs=pl.BlockSpec(memory_space=pltpu.VMEM),
        scratch_shapes=[
            pltpu.VMEM((R, N), F32),
            pltpu.SemaphoreType.DMA((N_DEV - 1,)),
            pltpu.SemaphoreType.DMA((N_DEV - 1,)),
        ],
        name=name,
    )(v)


def _other_chips(x, y):
    chips = [(1 - x, y), (x, 1 - y), (1 - x, 1 - y)]
    return chips, [2 * px + py for px, py in chips]


_HBM = pl.BlockSpec(memory_space=pltpu.HBM)


_SEM = pl.BlockSpec(memory_space=pltpu.SEMAPHORE)
_ANY = pl.BlockSpec(memory_space=pl.ANY)
_EFFECT = pltpu.SideEffectType.DATAFLOW_SIDE_EFFECTING


def _gather_copies(srcs, lands, send_sems, recv_sems):
    x, y, c = _place()
    k_me = 2 * x + y
    chips, kidx = _other_chips(x, y)
    cps = []
    for w in range(len(srcs)):
        for j, (px, py) in enumerate(chips):
            def copy(dst, w=w, j=j, px=px, py=py):
                return pltpu.make_async_remote_copy(
                    src_ref=srcs[w].at[:, c], dst_ref=dst, send_sem=send_sems.at[3 * w + j],
                    recv_sem=recv_sems.at[3 * w + j], device_id=(px, py, c), device_id_type=MESH,
                )
            cps.append((copy(lands[w].at[:, k_me, c]), copy(lands[w].at[:, kidx[j], c])))
    return cps


def _fill_copies(srcs, lands, send_sems, recv_sems):
    x, y, c = _place()
    k_me = 2 * x + y
    _, kidx = _other_chips(x, y)
    sib = (x, y, 1 - c)
    cps = []
    for w in range(len(srcs)):
        own = pltpu.make_async_remote_copy(
            src_ref=srcs[w], dst_ref=lands[w].at[:, k_me], send_sem=send_sems.at[4 * w + 3], recv_sem=recv_sems.at[4 * w + 3],
            device_id=sib, device_id_type=MESH,
        )
        cps.append((own, own))
        for j in range(3):
            def copy(half, w=w, j=j):
                part = lands[w].at[:, kidx[j], half]
                return pltpu.make_async_remote_copy(
                    src_ref=part, dst_ref=part, send_sem=send_sems.at[4 * w + j], recv_sem=recv_sems.at[4 * w + j],
                    device_id=sib, device_id_type=MESH,
                )
            cps.append((copy(c), copy(1 - c)))
    return cps


def _reduce_copies(srcs, lands, send_sems, recv_sems):
    x, y, c = _place()
    chips, kidx = _other_chips(x, y)
    cps = []
    for w in range(len(srcs)):
        for j, (px, py) in enumerate(chips):
            cp = pltpu.make_async_remote_copy(
                src_ref=srcs[w].at[:, kidx[j]], dst_ref=lands[w].at[j], send_sem=send_sems.at[3 * w + j],
                recv_sem=recv_sems.at[3 * w + j], device_id=(px, py, c), device_id_type=MESH,
            )
            cps.append((cp, cp))
    return cps


def split_start(copies, srcs, land_shapes, after, *, name, lands=None, per=3):
    n = len(srcs)

    def body(*refs):
        src_refs, land_refs = refs[:n], refs[n : 2 * n]
        send_sems, recv_sems = refs[2 * n + 1], refs[2 * n + 2]
        token = refs[-1]
        for cp, _ in copies(src_refs, land_refs, send_sems, recv_sems):
            cp.start()
        token[...] = jnp.zeros_like(token)

    if lands is None:
        lands = [lax.empty(sh, s.dtype) for sh, s in zip(land_shapes, srcs)]
    land_shapes = [a.shape for a in lands]
    lands = [pltpu.with_memory_space_constraint(a, pltpu.HBM) for a in lands]
    srcs = [pltpu.with_memory_space_constraint(s, pltpu.HBM) for s in srcs]
    out = pl.pallas_call(
        body,
        out_shape=(
            pltpu.SemaphoreType.DMA((per * n,)), pltpu.SemaphoreType.DMA((per * n,)),
            *[pltpu.HBM(s.shape, s.dtype) for s in srcs],
            *[pltpu.HBM(sh, s.dtype) for sh, s in zip(land_shapes, srcs)],
            jax.ShapeDtypeStruct((8, LANE), F32),
        ),
        in_specs=[_HBM] * (2 * n) + [_ANY],
        out_specs=(_SEM, _SEM, *([_HBM] * (2 * n)), pl.BlockSpec(memory_space=pltpu.VMEM)),
        input_output_aliases={i: 2 + i for i in range(2 * n)},
        compiler_params=pltpu.CompilerParams(has_side_effects=_EFFECT),
        name=name,
    )(*srcs, *lands, after)
    return out[0], out[1], list(out[2 : 2 + n]), list(out[2 + n : 2 + 2 * n]), out[-1]


def split_wait(copies, send_sems, recv_sems, srcs, lands, after, *, name):
    n = len(srcs)

    def body(*refs):
        src_refs, land_refs = refs[:n], refs[n : 2 * n]
        send_ref, recv_ref = refs[2 * n], refs[2 * n + 1]
        for sent, arrives in copies(src_refs, land_refs, send_ref, recv_ref):
            sent.wait_send()
            arrives.wait_recv()

    out = pl.pallas_call(
        body,
        out_shape=tuple(pltpu.HBM(a.shape, a.dtype) for a in list(srcs) + list(lands)),
        in_specs=[_HBM] * (2 * n) + [_SEM, _SEM, _ANY],
        out_specs=tuple([_HBM] * (2 * n)),
        input_output_aliases={i: i for i in range(2 * n)},
        compiler_params=pltpu.CompilerParams(has_side_effects=_EFFECT),
        name=name,
    )(*srcs, *lands, send_sems, recv_sems, after)
    return list(out[n:])


def rs_sibling(grads, *, name):
    n = len(grads)

    def body(*refs):
        ins, outs = refs[:n], refs[n : 2 * n]
        send_sems, recv_sems = refs[2 * n :]
        x, y, c = _place()
        cps = []
        for w in range(n):
            cp = pltpu.make_async_remote_copy(
                src_ref=ins[w].at[:, :, 1 - c], dst_ref=outs[w], send_sem=send_sems.at[w], recv_sem=recv_sems.at[w],
                device_id=(x, y, 1 - c), device_id_type=MESH,
            )
            cp.start()
            cps.append(cp)
        for cp in cps:
            cp.wait()

    return list(pl.pallas_call(
        body,
        out_shape=tuple(jax.ShapeDtypeStruct(g.shape[:2] + g.shape[3:], g.dtype) for g in grads),
        in_specs=[_HBM] * n,
        out_specs=tuple([_HBM] * n),
        scratch_shapes=[pltpu.SemaphoreType.DMA((n,)), pltpu.SemaphoreType.DMA((n,))],
        name=name,
    )(*grads))


def rs_share(halves, *, name):
    n = len(halves)

    def body(*refs):
        outs = refs[n : 2 * n]
        send_sems, recv_sems = refs[2 * n :]
        x, y, c = _place()
        cps = []
        for w in range(n):
            cp = pltpu.make_async_remote_copy(
                src_ref=outs[w].at[:, c], dst_ref=outs[w].at[:, c], send_sem=send_sems.at[w], recv_sem=recv_sems.at[w],
                device_id=(x, y, 1 - c), device_id_type=MESH,
            )
            cp.start()
            cps.append(cp)
        for w, cp in enumerate(cps):
            cp.wait_send()
            pltpu.make_async_remote_copy(
                src_ref=outs[w].at[:, c], dst_ref=outs[w].at[:, 1 - c], send_sem=send_sems.at[w], recv_sem=recv_sems.at[w],
                device_id=(x, y, 1 - c), device_id_type=MESH,
            ).wait_recv()

    return pl.pallas_call(
        body,
        out_shape=tuple(jax.ShapeDtypeStruct(h.shape, h.dtype) for h in halves),
        in_specs=[_HBM] * n,
        out_specs=tuple([_HBM] * n),
        scratch_shapes=[pltpu.SemaphoreType.DMA((n,)), pltpu.SemaphoreType.DMA((n,))],
        input_output_aliases={w: w for w in range(n)},
        name=name,
    )(*halves)


def _row_tile(R, C):
    return _pick(R, max(16, (1 << 19) // C // 16 * 16), 16)


def _my_core():
    return lax.axis_index("c")


def _my_chip():
    return 2 * lax.axis_index("x") + lax.axis_index("y")


def rs_add_pair(g, r, *, name):
    M, K, _, R, C = g.shape
    tr = _row_tile(R, C)

    def body(g_ref, r_ref, o_ref):
        o_ref[...] = (g_ref[...].astype(F32) + r_ref[...].astype(F32)).astype(o_ref.dtype)

    blk = pl.BlockSpec((None, None, tr, C), lambda m, k, i: (m, k, i, 0))
    return pl.pallas_call(
        body,
        out_shape=jax.ShapeDtypeStruct((M, K, R, C), BF16),
        grid=(M, K, R // tr),
        in_specs=[pl.BlockSpec((None, None, None, tr, C), lambda m, k, i: (m, k, _my_core(), i, 0)), blk],
        out_specs=blk,
        compiler_params=_params("parallel", "parallel", "parallel"),
        name=name,
    )(g, r)


def rs_add_final(g, r, t, full, m0, *, name):
    M, K, _, R, C = g.shape
    tr = _row_tile(R, C)

    def body(g_ref, r_ref, t_ref, full_ref, o_ref):
        acc = g_ref[...].astype(F32) + r_ref[...].astype(F32)
        for j in range(3):
            acc = acc + t_ref[j].astype(F32)
        o_ref[...] = acc

    return pl.pallas_call(
        body,
        out_shape=jax.ShapeDtypeStruct(full.shape, full.dtype),
        grid=(M, R // tr),
        in_specs=[
            pl.BlockSpec((None, None, None, tr, C), lambda m, i: (m, _my_chip(), _my_core(), i, 0)),
            pl.BlockSpec((None, None, tr, C), lambda m, i: (m, _my_chip(), i, 0)),
            pl.BlockSpec((3, None, tr, C), lambda m, i: (0, m, i, 0)),
            _ANY,
        ],
        out_specs=pl.BlockSpec((None, None, tr, C), lambda m, i: (m0 + m, _my_core(), i, 0)),
        input_output_aliases={3: 0},
        compiler_params=_params("parallel", "parallel"),
        name=name,
    )(g, r, t, full)


WEIGHTS = ["ffn_norm_w", "ffn_w_gu", "ffn_w_down", "mod_w", "mod_b", "mix_norm_w", "ssm_w_in", "ssm_conv_w", "ssm_conv_b",
           "ssm_dt_bias", "ssm_a_log", "ssm_d", "ssm_norm_w", "ssm_w_out", "kv_norm_w", "kv_mod_w", "kv_mod_b", "w_kv", "b_kv",
           "attn_w_q", "attn_b_q", "attn_sinks", "attn_w_o", "attn_b_o", "final_norm_w"]
GATHERED = ["ffn_w_gu", "ffn_w_down", "ssm_w_in", "ssm_w_out", "w_kv", "attn_w_q", "attn_w_o"]
COLUMN_PARALLEL = ["mod_w", "kv_mod_w"]
SMALL_SHARDED = ["ffn_norm_w", "ssm_conv_w", "ssm_conv_b", "ssm_norm_w"]
SMALL = [n for n in WEIGHTS if n not in GATHERED and n not in COLUMN_PARALLEL]


def _row_halves(a):
    a = a.reshape((-1,) + a.shape[-2:])
    return a.reshape(a.shape[0], 2, a.shape[1] // 2, a.shape[2])


def _pack(arrs, rows=8):
    flat = jnp.concatenate([a.reshape(-1) for a in arrs])
    n = flat.shape[0]
    pad = (-n) % (rows * LANE)
    return jnp.pad(flat, (0, pad)).reshape(rows, -1), n


def _unpack(flat, like):
    out, o = [], 0
    for s in like:
        k = 1
        for d in s:
            k *= d
        out.append(flat[o : o + k].reshape(s))
        o += k
    return out


def kernel(x, c, ffn_norm_w, ffn_w_gu, ffn_w_down, mod_w, mod_b, mix_norm_w, ssm_w_in, ssm_conv_w, ssm_conv_b, ssm_dt_bias, ssm_a_log, ssm_d, ssm_norm_w, ssm_w_out, kv_norm_w, kv_mod_w, kv_mod_b, w_kv, b_kv, attn_w_q, attn_b_q, attn_sinks, attn_w_o, attn_b_o, final_norm_w, loss_target, m_ffn_norm_w, m_ffn_w_gu, m_ffn_w_down, m_mod_w, m_mod_b, m_mix_norm_w, m_ssm_w_in, m_ssm_conv_w, m_ssm_conv_b, m_ssm_dt_bias, m_ssm_a_log, m_ssm_d, m_ssm_norm_w, m_ssm_w_out, m_kv_norm_w, m_kv_mod_w, m_kv_mod_b, m_w_kv, m_b_kv, m_attn_w_q, m_attn_b_q, m_attn_sinks, m_attn_w_o, m_attn_b_o, m_final_norm_w, v_ffn_norm_w, v_ffn_w_gu, v_ffn_w_down, v_mod_w, v_mod_b, v_mix_norm_w, v_ssm_w_in, v_ssm_conv_w, v_ssm_conv_b, v_ssm_dt_bias, v_ssm_a_log, v_ssm_d, v_ssm_norm_w, v_ssm_w_out, v_kv_norm_w, v_kv_mod_w, v_kv_mod_b, v_w_kv, v_b_kv, v_attn_w_q, v_attn_b_q, v_attn_sinks, v_attn_w_o, v_attn_b_o, v_final_norm_w):
    env = dict(locals())
    W = {n: env[n] for n in WEIGHTS}
    MOM = {n: env["m_" + n] for n in WEIGHTS}
    VAR = {n: env["v_" + n] for n in WEIGHTS}

    ax, ay, ac = _place()
    kme = 2 * ax + ay
    me = 4 * ax + 2 * ay + ac

    xs = x[0]
    target = loss_target[0]
    L, D = xs.shape
    depth, n_a = ffn_w_gu.shape[0], ssm_w_in.shape[0]
    n_b = depth - n_a
    T = ffn_w_gu.shape[-1]
    DI = ssm_w_out.shape[1] * N_CHIPS
    CI = ssm_w_in.shape[2]
    CC = ssm_conv_w.shape[2] * N_CHIPS
    MW = mod_w.shape[2]
    KW = kv_mod_w.shape[1]
    KVD = w_kv.shape[1]

    def chip_cols(a, width):
        return lax.dynamic_slice_in_dim(a, kme * width, width, axis=a.ndim - 1)

    def ffn_items(i, j):
        return [("ffn_w_gu", 2 * i + j, ffn_w_gu[i, j]), ("ffn_w_down", 2 * i + j, ffn_w_down[i, j])]

    def mix_items(i):
        if i < n_a:
            return [("ssm_w_in", i, ssm_w_in[i]), ("ssm_w_out", i, ssm_w_out[i])]
        return [("attn_w_q", i - n_a, attn_w_q[i - n_a]), ("attn_w_o", i - n_a, attn_w_o[i - n_a])]

    def layer_items(i, order):
        kv_items = [("w_kv", 0, w_kv)] if i == n_a else []
        if order == "fwd":
            return kv_items + ffn_items(i, 0) + mix_items(i) + ffn_items(i, 1)
        return ffn_items(i, 1) + mix_items(i) + ffn_items(i, 0) + kv_items

    fwd_stages = [ffn_items(0, 0), mix_items(0) + ffn_items(0, 1)] + [layer_items(i, "fwd") for i in range(1, depth)]
    bwd_stages = [layer_items(i, "bwd") for i in range(depth - 1, 0, -1)] + [ffn_items(0, 1) + mix_items(0), ffn_items(0, 0)]

    gw, inflight = {}, {}

    def chips_begin(s, after):
        keys = [(n, m0) for n, m0, _ in fwd_stages[s]]
        shards = [_row_halves(a.astype(BF16)) for _, _, a in fwd_stages[s]]
        land_shapes = [(sh.shape[0], N_CHIPS) + sh.shape[1:] for sh in shards]
        ssem, rsem, srcs, lands, token = split_start(_gather_copies, shards, land_shapes, after, name=f"gather_start_{s}")
        inflight[s] = (keys, ssem, rsem, srcs, lands)
        return token[0:1, 0:1]

    def n_first(s):
        return 2 if s < 2 else len(fwd_stages[s]) - 4

    def cores_begin(s, after):
        keys, ssem, rsem, srcs, lands = inflight.pop(s)
        lands = split_wait(_gather_copies, ssem, rsem, srcs, lands, after, name=f"gather_wait_{s}")
        k = n_first(s)
        a_sem, b_sem, a_src, a_land, tok = split_start(
            _fill_copies, srcs[:k], None, lands[0], name=f"fill_start_{s}a", lands=lands[:k], per=4)
        if k < len(keys):
            inflight[s] = (keys[k:],) + split_start(
                _fill_copies, srcs[k:], None, tok, name=f"fill_start_{s}b", lands=lands[k:], per=4)
            tok = inflight[s][-1]
        gw.update(zip(keys[:k], split_wait(_fill_copies, a_sem, b_sem, a_src, a_land, tok, name=f"fill_wait_{s}a")))
        return tok[0:1, 0:1]

    def cores_end(s, after):
        if s in inflight:
            keys, ssem, rsem, srcs, lands, _ = inflight.pop(s)
            gw.update(zip(keys, split_wait(_fill_copies, ssem, rsem, srcs, lands, after, name=f"fill_wait_{s}b")))

    def g_gu(i, j):
        return gw["ffn_w_gu", 2 * i + j].reshape(N_CHIPS, D, T)

    def g_dn(i, j):
        return gw["ffn_w_down", 2 * i + j].reshape(2, T, D)

    def g_full(n, m0):
        a = gw[n, m0]
        return a.reshape(N_CHIPS * 2 * a.shape[-2], a.shape[-1])

    c = c + chips_begin(0, c)

    sm_like = [W[n].shape for n in SMALL_SHARDED]
    sm_pack, sm_n = _pack([W[n] for n in SMALL_SHARDED])
    sm_all = ag8(sm_pack, name="ag_small_w")[0::2].reshape(N_CHIPS, -1)[:, :sm_n]
    full = {}
    for n, part in zip(SMALL_SHARDED, zip(*[_unpack(sm_all[k], sm_like) for k in range(N_CHIPS)])):
        full[n] = jnp.concatenate(part, axis=-1)

    c_all = ag8(c, act=_silu, name="ag_c").reshape(N_DEV, D)
    p_mod = mm(c_all, mod_w, bias=chip_cols(mod_b, MW)[:, None, :], name="mod_mm")
    p_kv = mm(c_all, kv_mod_w, bias=chip_cols(kv_mod_b, KW)[None, :], name="kvmod_mm")
    p_all = jnp.concatenate([jnp.transpose(p_mod, (1, 0, 2)).reshape(N_DEV, depth * MW), p_kv], axis=1)
    p_mine = lax.dynamic_index_in_dim(ag8(p_all, name="ag_mod")[0::2], me, axis=1, keepdims=False)
    mod = jnp.transpose(p_mine[:, : depth * MW].reshape(N_CHIPS, depth, MW), (1, 0, 2)).reshape(depth, N_MOD * D)
    kvmod = p_mine[:, depth * MW :].reshape(1, 2 * D)
    mods = [[mod[i : i + 1, j * D : (j + 1) * D] for j in range(N_MOD)] for i in range(depth)]
    kv_shift, kv_scale = kvmod[:, :D], kvmod[:, D:]

    def ffn_fwd(xin, i, j, sh, sc, gt):
        h = norm_mod_fwd(xin, full["ffn_norm_w"][i, j][None], sc, sh, name=f"ffn_norm_{i}_{j}")
        gu, a = ffn_up(h, g_gu(i, j), name=f"ffn_gu_{i}_{j}")
        f, xo = mm(a, g_dn(i, j), reduce_s=True, resid=(xin, gt, FFN_HALF), name=f"ffn_down_{i}_{j}")
        return xo, (xin, gu, a, f, h)

    def ssm_fwd(xin, i, sh, sc, gt):
        h = norm_mod_fwd(xin, mix_norm_w[i][None], sc, sh, name=f"mix_norm_{i}")
        zx4 = mm(h, gw["ssm_w_in", i].reshape(N_CHIPS, D, CI), name=f"ssm_in_{i}")
        zx = jnp.transpose(zx4, (1, 0, 2)).reshape(L, N_CHIPS * CI)
        xbc = conv_fwd(zx, full["ssm_conv_w"][i], full["ssm_conv_b"][i][None], DI, name=f"ssm_conv_{i}")
        dt_raw = zx[:, DI + CC :]
        y, states = ssd_fwd(xbc, dt_raw, ssm_dt_bias[i][None], ssm_a_log[i][None], ssm_d[i][None], DI, name=f"ssd_{i}")
        yn = gnorm_fwd(y, zx, full["ssm_norm_w"][i][None], name=f"ssm_gnorm_{i}")
        f, xo = mm(yn, g_full("ssm_w_out", i), resid=(xin, gt, 1.0), name=f"ssm_out_{i}")
        return xo, (xin, zx, xbc, dt_raw, y, states, yn, f, h)

    def att_fwd(xin, i, kv, sh, sc, gt):
        l = i - n_a
        h = norm_mod_fwd(xin, mix_norm_w[i][None], sc, sh, name=f"mix_norm_{i}")
        q = mm(h, g_full("attn_w_q", l), bias=attn_b_q[l][None], name=f"att_q_{i}")
        o = attn_fwd(q, kv, attn_sinks[l][None], name=f"att_{i}")
        f, xo = mm(o, g_full("attn_w_o", l), bias=attn_b_o[l][None], resid=(xin, gt, 1.0), name=f"att_o_{i}")
        return xo, (xin, q, o, f, h)

    saved = [[None, None, None] for _ in range(depth)]
    xc = xs
    kv = x_kv = None
    n_stage = len(fwd_stages)

    def run_stage(s, xc, tok):
        nonlocal kv, x_kv
        i = max(s - 1, 0)
        sh1, sc1, g1, shm, scm, gm, sh2, sc2, g2 = mods[i]
        if s != 1:
            if i == n_a:
                x_kv = xc
                hkv = norm_mod_fwd(xc, kv_norm_w[None], kv_scale, kv_shift + tok, name="kv_norm")
                kv = mm(hkv, g_full("w_kv", 0), bias=b_kv[None], name="kv_proj")
            xc, saved[i][0] = ffn_fwd(xc, i, 0, sh1 + tok, sc1, g1)
            if s != 0:
                cores_end(s, xc)
        if s != 0:
            xc, saved[i][1] = ssm_fwd(xc, i, shm + tok, scm, gm) if i < n_a else att_fwd(xc, i, kv, shm + tok, scm, gm)
            if s == 1:
                cores_end(s, xc)
            xc, saved[i][2] = ffn_fwd(xc, i, 1, sh2, sc2, g2)
        return xc

    dep = cores_begin(0, kvmod)
    for s in range(n_stage):
        tok = chips_begin(s + 1, dep) if s + 1 < n_stage else 0.0
        xc = run_stage(s, xc, tok)
        if s + 1 < n_stage:
            dep = cores_begin(s + 1, xc)

    loss_part, dx, d_final = final_loss(xc, final_norm_w[None], target, name="loss_head")
    loss = lax.psum(loss_part[0, 0], ("x", "y", "c"))

    wg = {}
    sg = {
        "ffn_norm_w": [[None, None] for _ in range(depth)], "mix_norm_w": [None] * depth, "mod": [None] * depth,
        "ssm_conv_w": [None] * n_a, "ssm_conv_b": [None] * n_a, "ssm_dt_bias": [None] * n_a, "ssm_a_log": [None] * n_a,
        "ssm_d": [None] * n_a, "ssm_norm_w": [None] * n_a, "attn_b_q": [None] * n_b, "attn_sinks": [None] * n_b,
        "attn_b_o": [None] * n_b,
    }

    def ffn_bwd(dxo, i, j, sv, sh, sc, gt):
        xin, gu, a, f, h = sv
        df, dgt, _ = gate_bwd(f, dxo, gt, FFN_HALF, name=f"ffn_res_bwd_{i}_{j}")
        dgu = ffn_down_bwd(df, g_dn(i, j), gu, name=f"ffn_down_dx_{i}_{j}")
        wg["ffn_w_down", 2 * i + j] = mm(a, df, mode="tn", out_dtype=BF16, name=f"ffn_down_dw_{i}_{j}")
        nw = full["ffn_norm_w"][i, j][None]
        dh = mm(dgu, g_gu(i, j), mode="nt", reduce_s=True, name=f"ffn_gu_dx_{i}_{j}")
        wg["ffn_w_gu", 2 * i + j] = mm(h, dgu, mode="tn", out_dtype=BF16, name=f"ffn_gu_dw_{i}_{j}")
        dxi, dnw, dsc, dsh = norm_mod_bwd(xin, nw, sc, dh, dxo, name=f"ffn_norm_bwd_{i}_{j}")
        sg["ffn_norm_w"][i][j] = dnw
        return dxi, (dsh, dsc, dgt)

    def ssm_bwd(dxo, i, sv, sh, sc, gt):
        xin, zx, xbc, dt_raw, y, states, yn, f, h = sv
        df, dgt, _ = gate_bwd(f, dxo, gt, 1.0, name=f"mix_res_bwd_{i}")
        dyn = mm(df, g_full("ssm_w_out", i), mode="nt", name=f"ssm_out_dx_{i}")
        wg["ssm_w_out", i] = mm(yn, df, mode="tn", out_dtype=BF16, name=f"ssm_out_dw_{i}")
        dy, dz, dnorm = gnorm_bwd(y, zx, full["ssm_norm_w"][i][None], dyn, name=f"ssm_gnorm_bwd_{i}")
        dxbc, ddt, dbias, dalog, ddsk = ssd_bwd(
            xbc, dt_raw, ssm_dt_bias[i][None], ssm_a_log[i][None], ssm_d[i][None], states, dy, DI, name=f"ssd_bwd_{i}"
        )
        du, dcw, dcb = conv_bwd(zx, full["ssm_conv_w"][i], full["ssm_conv_b"][i][None], dxbc, DI, name=f"ssm_conv_bwd_{i}")
        dzx = jnp.concatenate([dz, du, ddt], axis=1).astype(BF16)
        dzx4 = jnp.transpose(dzx.reshape(L, N_CHIPS, CI), (1, 0, 2))
        nw = mix_norm_w[i][None]
        dh = mm(dzx4, gw["ssm_w_in", i].reshape(N_CHIPS, D, CI), mode="nt", reduce_s=True, name=f"ssm_in_dx_{i}")
        wg["ssm_w_in", i] = mm(h, dzx4, mode="tn", out_dtype=BF16, name=f"ssm_in_dw_{i}")
        dxi, dnw, dsc, dsh = norm_mod_bwd(xin, nw, sc, dh, dxo, name=f"mix_norm_bwd_{i}")
        sg["mix_norm_w"][i] = dnw
        sg["ssm_conv_w"][i], sg["ssm_conv_b"][i], sg["ssm_norm_w"][i] = dcw, dcb, dnorm
        sg["ssm_dt_bias"][i], sg["ssm_a_log"][i], sg["ssm_d"][i] = dbias, dalog, ddsk
        return dxi, (dsh, dsc, dgt)

    def att_bwd(dxo, i, sv, dkv, sh, sc, gt):
        l = i - n_a
        xin, q, o, f, h = sv
        df, dgt, dfsum = gate_bwd(f, dxo, gt, 1.0, name=f"mix_res_bwd_{i}")
        do = mm(df, g_full("attn_w_o", l), mode="nt", name=f"att_o_dx_{i}")
        wg["attn_w_o", l] = mm(o, df, mode="tn", out_dtype=BF16, name=f"att_o_dw_{i}")
        dq, dkv, dsink = attn_bwd(q, kv, attn_sinks[l][None], do, dkv, name=f"att_bwd_{i}")
        nw = mix_norm_w[i][None]
        dh = mm(dq, g_full("attn_w_q", l), mode="nt", name=f"att_q_dx_{i}")
        wg["attn_w_q", l] = mm(h, dq, mode="tn", out_dtype=BF16, name=f"att_q_dw_{i}")
        dxi, dnw, dsc, dsh = norm_mod_bwd(xin, nw, sc, dh, dxo, name=f"mix_norm_bwd_{i}")
        sg["mix_norm_w"][i] = dnw
        sg["attn_b_q"][l], sg["attn_sinks"][l], sg["attn_b_o"][l] = colsum(dq, name=f"att_bq_{i}"), dsink, dfsum
        return dxi, dkv, (dsh, dsc, dgt)

    gfull = {n: lax.empty(_row_halves(W[n]).shape, F32) for n in GATHERED}

    pending = []

    def reduce_step(s, after):
        if pending:
            reduce_end(after)
        items = bwd_stages[s]
        parts = []
        for n, m0, a in items:
            m, _, rh, cc = _row_halves(a).shape
            parts.append(wg.pop((n, m0)).reshape(m, N_CHIPS, 2, rh, cc))
        from_sib = rs_sibling(parts, name=f"rs_sibling_{s}")
        pair = [rs_add_pair(g, r, name=f"rs_pair_{n}_{m0}") for (n, m0, _), g, r in zip(items, parts, from_sib)]
        land_shapes = [(3, p.shape[0]) + p.shape[2:] for p in pair]
        ssem, rsem, srcs, lands, token = split_start(_reduce_copies, pair, land_shapes, after, name=f"rs_chips_start_{s}")
        pending.append((s, items, parts, from_sib, ssem, rsem, srcs, lands))
        return token[0:1, 0:1]

    def reduce_end(after):
        s, items, parts, from_sib, ssem, rsem, srcs, lands = pending.pop()
        from_chips = split_wait(_reduce_copies, ssem, rsem, srcs, lands, after, name=f"rs_chips_wait_{s}")
        for (n, m0, _), g, r, t in zip(items, parts, from_sib, from_chips):
            gfull[n] = rs_add_final(g, r, t, gfull[n], m0, name=f"rs_final_{n}_{m0}")

    dkv = jnp.zeros((L, KVD), F32)
    d_kvnorm = d_kvmod = d_bkv = None
    tok = 0.0
    for i in reversed(range(depth)):
        sh1, sc1, g1, shm, scm, gm, sh2, sc2, g2 = mods[i]
        s1, sm, s2 = saved[i]
        dx, dm2 = ffn_bwd(dx, i, 1, s2, sh2, sc2, g2 + tok)
        if i < n_a:
            dx, dmm = ssm_bwd(dx, i, sm, shm, scm, gm)
        else:
            dx, dkv, dmm = att_bwd(dx, i, sm, dkv, shm, scm, gm)
        if i == 0:
            g1 = g1 + reduce_step(depth - 1, dx)
        dx, dm1 = ffn_bwd(dx, i, 0, s1, sh1, sc1, g1)
        sg["mod"][i] = jnp.concatenate(list(dm1) + list(dmm) + list(dm2), axis=1)
        if i == n_a:
            d_bkv = colsum(dkv, name="kv_bias_bwd")
            hkv = norm_mod_fwd(x_kv, kv_norm_w[None], kv_scale, kv_shift, name="kv_norm_re")
            dh = mm(dkv, g_full("w_kv", 0), mode="nt", name="kv_proj_dx")
            wg["w_kv", 0] = mm(hkv, dkv, mode="tn", out_dtype=BF16, name="kv_proj_dw")
            dx, d_kvnorm, dsc, dsh = norm_mod_bwd(x_kv, kv_norm_w[None], kv_scale, dh, dx, name="kv_norm_bwd")
            d_kvmod = jnp.concatenate([dsh, dsc], axis=1)
        tok = reduce_step(depth - i if i == 0 else depth - 1 - i, dx)
    grad_x = dx[None]
    grads = {}

    small = {
        "ffn_norm_w": jnp.stack([jnp.stack([r[0] for r in row]) for row in sg["ffn_norm_w"]]),
        "mod_b": jnp.stack([r[0] for r in sg["mod"]]),
        "mix_norm_w": jnp.stack([r[0] for r in sg["mix_norm_w"]]),
        "ssm_conv_w": jnp.stack(sg["ssm_conv_w"]),
        "ssm_conv_b": jnp.stack([r[0] for r in sg["ssm_conv_b"]]),
        "ssm_dt_bias": jnp.stack([r[0] for r in sg["ssm_dt_bias"]]),
        "ssm_a_log": jnp.stack([r[0] for r in sg["ssm_a_log"]]),
        "ssm_d": jnp.stack([r[0] for r in sg["ssm_d"]]),
        "ssm_norm_w": jnp.stack([r[0] for r in sg["ssm_norm_w"]]),
        "kv_norm_w": d_kvnorm[0],
        "kv_mod_b": d_kvmod[0],
        "b_kv": d_bkv[0],
        "attn_b_q": jnp.stack([r[0] for r in sg["attn_b_q"]]),
        "attn_sinks": jnp.stack([r[0] for r in sg["attn_sinks"]]),
        "attn_b_o": jnp.stack([r[0] for r in sg["attn_b_o"]]),
        "final_norm_w": d_final[0],
    }
    small_like = [small[n].shape for n in SMALL]
    sv_pack, sv_n = _pack([small[n] for n in SMALL])
    sv_all = ag8(sv_pack + tok, name="ag_small_g")
    sv_sum = reduce8(sv_all, name="small_g_sum").reshape(-1)[:sv_n]
    for n, gsum in zip(SMALL, _unpack(sv_sum, small_like)):
        grads[n] = chip_cols(gsum, W[n].shape[-1]) if n in SMALL_SHARDED else gsum

    per_dev = [_unpack(sv_all[b].reshape(-1)[:sv_n], small_like) for b in range(N_DEV)]
    i_modb, i_kvb = SMALL.index("mod_b"), SMALL.index("kv_mod_b")
    dmod_all = jnp.stack([chip_cols(p[i_modb], MW) for p in per_dev], axis=1)
    dkv_all = jnp.stack([chip_cols(p[i_kvb], KW) for p in per_dev], axis=0)[None]
    c_t = jnp.transpose(c_all)
    grads["mod_w"] = outer8(c_t, dmod_all, name="mod_w_grad")
    grads["kv_mod_w"] = outer8(c_t, dkv_all, name="kv_mod_w_grad")[0]

    delta, new_m, new_v = {}, {}, {}
    for n in COLUMN_PARALLEL:
        delta[n], new_m[n], new_v[n] = adamw(W[n], grads[n], MOM[n], VAR[n], name=f"adamw_{n}")
    like = [W[n].shape for n in SMALL]
    packs = [_pack([d[n] for n in SMALL])[0] for d in (W, grads, MOM, VAR)]
    n_small = sum(int(W[n].size) for n in SMALL)
    for dst, res in zip((delta, new_m, new_v), adamw(*packs, name="adamw_small")):
        for n, a in zip(SMALL, _unpack(res.reshape(-1)[:n_small], like)):
            dst[n] = a

    reduce_end(delta["mod_w"])
    for n, s in zip(GATHERED, rs_share([gfull[n] for n in GATHERED], name="rs_share")):
        grads[n] = s.reshape(W[n].shape)
        delta[n], new_m[n], new_v[n] = adamw(W[n], grads[n], MOM[n], VAR[n], name=f"adamw_{n}")

    return (loss, grad_x, *[grads[n] for n in WEIGHTS], *[delta[n] for n in WEIGHTS], *[new_m[n] for n in WEIGHTS],
            *[new_v[n] for n in WEIGHTS])
```

```python
import functools

import jax
import jax.numpy as jnp
from jax import lax
from jax.experimental import pallas as pl
from jax.experimental.pallas import tpu as pltpu

F32 = jnp.float32
BF16 = jnp.bfloat16
HIGHEST = lax.Precision.HIGHEST
MESH = pl.DeviceIdType.MESH

EPS = 1e-5
N_MOD = 9
FFN_HALF = 0.5
SSM_HEADDIM = 64
SSM_GROUPS = 8
SSM_STATE = 128
CONV_WIDTH = 4
CHUNK = 128
KV_HEADS = 4
HEAD_DIM = 64
WINDOW = 128
N_CHIPS = 4
N_DEV = 8

ADAM_LR = 0.001
ADAM_B1 = 0.9
ADAM_B2 = 0.999
ADAM_EPS = 1e-08
ADAM_WD = 0.01
ADAM_STEP = 10

LANE = 128
MM_TILE = 1024


def _pick(n, pref, align, whole_if_small=False):
    best = 0
    t = align
    while t <= min(n, pref):
        if n % t == 0:
            best = t
        t += align
    if best == 0 or (whole_if_small and best < 256 and n <= 2048):
        return n
    return best


def _sigmoid(x):
    return 1.0 / (1.0 + jnp.exp(-x))


def _silu(x):
    return x * _sigmoid(x)


def _dsilu(x):
    s = _sigmoid(x)
    return s * (1.0 + x * (1.0 - s))


def _params(*sem):
    return pltpu.CompilerParams(dimension_semantics=sem)


def mm(a, b, *, mode="nn", reduce_s=False, out_dtype=F32, bias=None, resid=None, name):
    a_s = a.ndim == 3
    b_s = b.ndim == 3
    S = a.shape[0] if a_s else (b.shape[0] if b_s else 1)
    a2 = a.shape[-2:]
    b2 = b.shape[-2:]
    if mode == "nn":
        (M, K), (K2, N) = a2, b2
    elif mode == "nt":
        (M, K), (N, K2) = a2, b2
    else:
        (K, M), (K2, N) = a2, b2
    assert K == K2, (a.shape, b.shape, mode)
    batch = (a_s or b_s) and not reduce_s
    sb = S if batch else 1
    sr = S if ((a_s or b_s) and reduce_s) else 1
    tm = _pick(M, MM_TILE // 2 if resid is not None else MM_TILE, LANE if mode == "tn" else 16, True)
    tn = _pick(N, MM_TILE, LANE, True)
    tk = _pick(K, 2 * MM_TILE if mode == "tn" else MM_TILE, LANE if mode != "tn" else 16, True)
    nk = K // tk
    grid = (sb, M // tm, N // tn, sr, nk)

    def s_of(isb, isr):
        return isb if batch else isr

    def a_map(isb, i, j, isr, k):
        idx = (k, i) if mode == "tn" else (i, k)
        return ((s_of(isb, isr),) + idx) if a_s else idx

    def b_map(isb, i, j, isr, k):
        idx = (j, k) if mode == "nt" else (k, j)
        return ((s_of(isb, isr),) + idx) if b_s else idx

    def o_map(isb, i, j, isr, k):
        return (isb, i, j) if batch else (i, j)

    def s_blk(has_s, blk):
        return ((None,) + blk) if has_s else blk

    a_blk = (tk, tm) if mode == "tn" else (tm, tk)
    b_blk = (tn, tk) if mode == "nt" else (tk, tn)
    in_specs = [pl.BlockSpec(s_blk(a_s, a_blk), a_map), pl.BlockSpec(s_blk(b_s, b_blk), b_map)]
    args = [a, b]
    if bias is not None:
        bias_s = bias.ndim == 3
        in_specs.append(
            pl.BlockSpec(
                ((None, 1, tn) if bias_s else (1, tn)),
                (lambda isb, i, j, isr, k: (isb, 0, j)) if bias_s else (lambda isb, i, j, isr, k: (0, j)),
            )
        )
        args.append(bias)
    o_spec = pl.BlockSpec(s_blk(batch, (tm, tn)), o_map)
    out_shape = jax.ShapeDtypeStruct(((sb, M, N) if batch else (M, N)), out_dtype)
    out_specs = o_spec
    if resid is not None:
        assert not batch
        x_res, gate, scale = resid
        in_specs += [pl.BlockSpec((tm, tn), o_map), pl.BlockSpec((1, tn), lambda isb, i, j, isr, k: (0, j))]
        args += [x_res, gate]
        out_shape = (out_shape, jax.ShapeDtypeStruct((M, N), F32))
        out_specs = (o_spec, pl.BlockSpec((tm, tn), o_map))
    dims = {"nn": (((1,), (0,)), ((), ())), "nt": (((1,), (1,)), ((), ())), "tn": (((0,), (0,)), ((), ()))}[mode]
    n_in = len(args)
    n_out = 2 if resid is not None else 1
    one_step = sr * nk == 1

    def body(*refs):
        a_ref, b_ref = refs[0], refs[1]
        bias_ref = refs[2] if bias is not None else None
        o_ref = refs[n_in]

        def finish(r):
            if bias is not None:
                r = r + bias_ref[...]
            o_ref[...] = r.astype(o_ref.dtype)
            if resid is not None:
                refs[n_in + 1][...] = refs[n_in - 2][...] + (scale * refs[n_in - 1][...]) * r

        def part():
            return lax.dot_general(a_ref[...].astype(BF16), b_ref[...].astype(BF16), dims, preferred_element_type=F32)

        if one_step:
            finish(part())
            return
        acc = refs[n_in + n_out]
        isr = pl.program_id(3)
        k = pl.program_id(4)

        @pl.when((isr == 0) & (k == 0))
        def _():
            acc[...] = jnp.zeros_like(acc)

        acc[...] += part()

        @pl.when((isr == sr - 1) & (k == nk - 1))
        def _():
            finish(acc[...])

    return pl.pallas_call(
        body,
        out_shape=out_shape,
        grid=grid,
        in_specs=in_specs,
        out_specs=out_specs,
        scratch_shapes=[] if one_step else [pltpu.VMEM((tm, tn), F32)],
        compiler_params=_params("parallel", "parallel", "parallel", "arbitrary", "arbitrary"),
        name=name,
    )(*args)


def norm_mod_fwd(x, nw, sc, sh, *, name):
    L, D = x.shape
    tl = _pick(L, 512, 16)

    def body(x_ref, nw_ref, sc_ref, sh_ref, h_ref):
        xv = x_ref[...]
        r = lax.rsqrt(jnp.mean(xv * xv, axis=-1, keepdims=True) + EPS)
        n = (xv * r) * nw_ref[...]
        h_ref[...] = (n * (1.0 + sc_ref[...]) + sh_ref[...]).astype(h_ref.dtype)

    row = pl.BlockSpec((1, D), lambda i: (0, 0))
    return pl.pallas_call(
        body,
        out_shape=jax.ShapeDtypeStruct((L, D), BF16),
        grid=(L // tl,),
        in_specs=[pl.BlockSpec((tl, D), lambda i: (i, 0)), row, row, row],
        out_specs=pl.BlockSpec((tl, D), lambda i: (i, 0)),
        compiler_params=_params("parallel"),
        name=name,
    )(x, nw, sc, sh)


def norm_mod_bwd(x, nw, sc, dh, dx_in, *, name):
    L, D = x.shape
    tl = _pick(L, 512, 16)

    def body(x_ref, nw_ref, sc_ref, dh_ref, dxi_ref, dx_ref, dnw_ref, dsc_ref, dsh_ref):
        @pl.when(pl.program_id(0) == 0)
        def _():
            dnw_ref[...] = jnp.zeros_like(dnw_ref)
            dsc_ref[...] = jnp.zeros_like(dsc_ref)
            dsh_ref[...] = jnp.zeros_like(dsh_ref)

        xv = x_ref[...]
        dh_v = dh_ref[...]
        r = lax.rsqrt(jnp.mean(xv * xv, axis=-1, keepdims=True) + EPS)
        xhat = xv * r
        nw_v = nw_ref[...]
        n = xhat * nw_v
        dsh_ref[...] += jnp.sum(dh_v, axis=0, keepdims=True)
        dsc_ref[...] += jnp.sum(dh_v * n, axis=0, keepdims=True)
        dn = dh_v * (1.0 + sc_ref[...])
        dnw_ref[...] += jnp.sum(dn * xhat, axis=0, keepdims=True)
        dxhat = dn * nw_v
        dx_ref[...] = dxi_ref[...] + r * (dxhat - xhat * jnp.mean(dxhat * xhat, axis=-1, keepdims=True))

    row = pl.BlockSpec((1, D), lambda i: (0, 0))
    tile = pl.BlockSpec((tl, D), lambda i: (i, 0))
    vec = jax.ShapeDtypeStruct((1, D), F32)
    return pl.pallas_call(
        body,
        out_shape=(jax.ShapeDtypeStruct((L, D), F32), vec, vec, vec),
        grid=(L // tl,),
        in_specs=[tile, row, row, tile, tile],
        out_specs=(tile, row, row, row),
        compiler_params=_params("arbitrary"),
        name=name,
    )(x, nw, sc, dh, dx_in)


def gate_bwd(f, dx, gate, scale, *, name):
    L, D = f.shape
    tl = _pick(L, 512, 16)

    def body(f_ref, dx_ref, g_ref, df_ref, dg_ref, dfsum_ref):
        @pl.when(pl.program_id(0) == 0)
        def _():
            dg_ref[...] = jnp.zeros_like(dg_ref)
            dfsum_ref[...] = jnp.zeros_like(dfsum_ref)

        dxv = dx_ref[...]
        df = (scale * g_ref[...]) * dxv
        df_ref[...] = df.astype(df_ref.dtype)
        dfsum_ref[...] += jnp.sum(df, axis=0, keepdims=True)
        dg_ref[...] += scale * jnp.sum(f_ref[...] * dxv, axis=0, keepdims=True)

    tile = pl.BlockSpec((tl, D), lambda i: (i, 0))
    row = pl.BlockSpec((1, D), lambda i: (0, 0))
    vec = jax.ShapeDtypeStruct((1, D), F32)
    return pl.pallas_call(
        body,
        out_shape=(jax.ShapeDtypeStruct((L, D), BF16), vec, vec),
        grid=(L // tl,),
        in_specs=[tile, tile, row],
        out_specs=(tile, row, row),
        compiler_params=_params("arbitrary"),
        name=name,
    )(f, dx, gate)


def colsum(x, *, name):
    L, N = x.shape
    tl = _pick(L, 512, 8)

    def body(x_ref, o_ref):
        @pl.when(pl.program_id(0) == 0)
        def _():
            o_ref[...] = jnp.zeros_like(o_ref)

        o_ref[...] += jnp.sum(x_ref[...], axis=0, keepdims=True)

    return pl.pallas_call(
        body,
        out_shape=jax.ShapeDtypeStruct((1, N), F32),
        grid=(L // tl,),
        in_specs=[pl.BlockSpec((tl, N), lambda i: (i, 0))],
        out_specs=pl.BlockSpec((1, N), lambda i: (0, 0)),
        compiler_params=_params("arbitrary"),
        name=name,
    )(x)


def ffn_up(h, wgu, *, name):
    L, D = h.shape
    T = wgu.shape[-1]
    tm = _pick(L, 512, 16)

    def body(h_ref, w_ref, gu_ref, a_ref):
        hb = h_ref[...].astype(BF16)
        g = jnp.dot(hb, w_ref[0].astype(BF16), preferred_element_type=F32)
        u = jnp.dot(hb, w_ref[1].astype(BF16), preferred_element_type=F32)
        gu_ref[0] = g
        gu_ref[1] = u
        a_ref[...] = (_silu(g) * u).astype(a_ref.dtype)

    gu, a = pl.pallas_call(
        body,
        out_shape=(jax.ShapeDtypeStruct((2, 2, L, T), F32), jax.ShapeDtypeStruct((2, L, T), BF16)),
        grid=(2, L // tm),
        in_specs=[
            pl.BlockSpec((tm, D), lambda j, i: (i, 0)),
            pl.BlockSpec((2, None, D, T), lambda j, i: (0, j, 0, 0)),
        ],
        out_specs=(
            pl.BlockSpec((2, None, tm, T), lambda j, i: (0, j, i, 0)),
            pl.BlockSpec((None, tm, T), lambda j, i: (j, i, 0)),
        ),
        compiler_params=_params("parallel", "parallel"),
        name=name,
    )(h, wgu.reshape(2, 2, D, T))
    return gu.reshape(4, L, T), a


def ffn_down_bwd(df, wdn, gu, *, name):
    L, D = df.shape
    T = wdn.shape[1]
    tm = _pick(L, 512, 16)

    def body(df_ref, w_ref, gu_ref, d_ref):
        da = lax.dot_general(
            df_ref[...].astype(BF16), w_ref[...].astype(BF16), (((1,), (1,)), ((), ())), preferred_element_type=F32
        )
        g = gu_ref[0]
        d_ref[0] = (da * gu_ref[1] * _dsilu(g)).astype(d_ref.dtype)
        d_ref[1] = (da * _silu(g)).astype(d_ref.dtype)

    out = pl.pallas_call(
        body,
        out_shape=jax.ShapeDtypeStruct((2, 2, L, T), BF16),
        grid=(2, L // tm),
        in_specs=[
            pl.BlockSpec((tm, D), lambda j, i: (i, 0)),
            pl.BlockSpec((None, T, D), lambda j, i: (j, 0, 0)),
            pl.BlockSpec((2, None, tm, T), lambda j, i: (0, j, i, 0)),
        ],
        out_specs=pl.BlockSpec((2, None, tm, T), lambda j, i: (0, j, i, 0)),
        compiler_params=_params("parallel", "parallel"),
        name=name,
    )(df, wdn, gu.reshape(2, 2, L, T))
    return out.reshape(4, L, T)


def _shift_down(u, k, rows):
    if k == 0:
        return u
    return jnp.where(rows >= k, pltpu.roll(u, k, 0), 0.0)


def _shift_up(u, k, rows, n):
    if k == 0:
        return u
    return jnp.where(rows < n - k, pltpu.roll(u, n - k, 0), 0.0)


def _conv_pre(u, w_ref, b_ref, rows):
    pre = b_ref[...] + w_ref[CONV_WIDTH - 1 : CONV_WIDTH, :] * u
    for k in range(1, CONV_WIDTH):
        pre = pre + w_ref[CONV_WIDTH - 1 - k : CONV_WIDTH - k, :] * _shift_down(u, k, rows)
    return pre


def conv_fwd(zx, conv_w, conv_b, d_inner, *, name):
    L = zx.shape[0]
    C = conv_w.shape[1]
    tc = 256
    off = d_inner // tc

    def body(u_ref, w_ref, b_ref, o_ref):
        rows = lax.broadcasted_iota(jnp.int32, (L, tc), 0)
        o_ref[...] = _silu(_conv_pre(u_ref[...], w_ref, b_ref, rows))

    return pl.pallas_call(
        body,
        out_shape=jax.ShapeDtypeStruct((L, C), F32),
        grid=(C // tc,),
        in_specs=[
            pl.BlockSpec((L, tc), lambda j: (0, off + j)),
            pl.BlockSpec((CONV_WIDTH, tc), lambda j: (0, j)),
            pl.BlockSpec((1, tc), lambda j: (0, j)),
        ],
        out_specs=pl.BlockSpec((L, tc), lambda j: (0, j)),
        compiler_params=_params("parallel"),
        name=name,
    )(zx, conv_w, conv_b)


def conv_bwd(zx, conv_w, conv_b, dxbc, d_inner, *, name):
    L = zx.shape[0]
    C = conv_w.shape[1]
    tc = 256
    off = d_inner // tc

    def body(u_ref, w_ref, b_ref, d_ref, du_ref, dw_ref, db_ref):
        rows = lax.broadcasted_iota(jnp.int32, (L, tc), 0)
        u = u_ref[...]
        dpre = d_ref[...] * _dsilu(_conv_pre(u, w_ref, b_ref, rows))
        db_ref[...] = jnp.sum(dpre, axis=0, keepdims=True)
        du = w_ref[CONV_WIDTH - 1 : CONV_WIDTH, :] * dpre
        dw_ref[CONV_WIDTH - 1 : CONV_WIDTH, :] = jnp.sum(dpre * u, axis=0, keepdims=True)
        for k in range(1, CONV_WIDTH):
            j = CONV_WIDTH - 1 - k
            dw_ref[j : j + 1, :] = jnp.sum(dpre * _shift_down(u, k, rows), axis=0, keepdims=True)
            du = du + w_ref[j : j + 1, :] * _shift_up(dpre, k, rows, L)
        du_ref[...] = du

    return pl.pallas_call(
        body,
        out_shape=(
            jax.ShapeDtypeStruct((L, C), F32),
            jax.ShapeDtypeStruct((CONV_WIDTH, C), F32),
            jax.ShapeDtypeStruct((1, C), F32),
        ),
        grid=(C // tc,),
        in_specs=[
            pl.BlockSpec((L, tc), lambda j: (0, off + j)),
            pl.BlockSpec((CONV_WIDTH, tc), lambda j: (0, j)),
            pl.BlockSpec((1, tc), lambda j: (0, j)),
            pl.BlockSpec((L, tc), lambda j: (0, j)),
        ],
        out_specs=(
            pl.BlockSpec((L, tc), lambda j: (0, j)),
            pl.BlockSpec((CONV_WIDTH, tc), lambda j: (0, j)),
            pl.BlockSpec((1, tc), lambda j: (0, j)),
        ),
        compiler_params=_params("parallel"),
        name=name,
    )(zx, conv_w, conv_b, dxbc)


def _ssd_head(xs, dt, acs, tot, dsk, cb, bm, cm, prev):
    q = xs.shape[0]
    li = lax.broadcasted_iota(jnp.int32, (q, q), 0)
    si = lax.broadcasted_iota(jnp.int32, (q, q), 1)
    causal = li >= si
    lmat = jnp.exp(jnp.where(causal, acs - acs.T, -jnp.inf))
    xdt = xs * dt
    y = jnp.dot((cb * lmat).astype(BF16), xdt.astype(BF16), preferred_element_type=F32)
    y = y + lax.dot_general(
        (cm * jnp.exp(acs)).astype(BF16), prev.astype(BF16), (((1,), (1,)), ((), ())), preferred_element_type=F32
    )
    y = y + dsk * xs
    st = lax.dot_general(
        xdt.astype(BF16), (bm * jnp.exp(tot - acs)).astype(BF16), (((0,), (0,)), ((), ())), preferred_element_type=F32
    )
    return y, prev * jnp.exp(tot) + st


def _pick_lane(v, h):
    lanes = lax.broadcasted_iota(jnp.int32, v.shape, 1)
    return jnp.sum(jnp.where(lanes == h, v, 0.0), axis=1, keepdims=True)


def _tri_cols(cols, upper):
    q = cols[0].shape[0]
    assert 3 * len(cols) <= LANE
    li = lax.broadcasted_iota(jnp.int32, (q, q), 0)
    si = lax.broadcasted_iota(jnp.int32, (q, q), 1)
    tri = ((li <= si) if upper else (li >= si)).astype(BF16)
    lanes = lax.broadcasted_iota(jnp.int32, (q, LANE), 1)
    rhs = jnp.zeros((q, LANE), F32)
    for r, col in enumerate(cols):
        hi = col.astype(BF16).astype(F32)
        mid = (col - hi).astype(BF16).astype(F32)
        lo = col - hi - mid
        for t, term in enumerate((hi, mid, lo)):
            rhs = jnp.where(lanes == 3 * r + t, term, rhs)
    out = jnp.dot(tri, rhs.astype(BF16), preferred_element_type=F32)
    return [jnp.sum(jnp.where((lanes >= 3 * r) & (lanes < 3 * r + 3), out, 0.0), axis=1, keepdims=True)
            for r in range(len(cols))]


def _softplus(x):
    return jnp.maximum(x, 0.0) + jnp.log(1.0 + jnp.exp(-jnp.abs(x)))


def _ssd_specs(L, d_inner, H, nc, rev):
    R = H // SSM_GROUPS
    P, N, Q = SSM_HEADDIM, SSM_STATE, CHUNK
    ngrp = SSM_GROUPS

    def ci(c):
        return (nc - 1 - c) if rev else c

    b_off = d_inner // N
    c_off = b_off + ngrp
    xs = pl.BlockSpec((Q, R * P), lambda c, g: (ci(c), g))
    bm = pl.BlockSpec((Q, N), lambda c, g: (ci(c), b_off + g))
    cm = pl.BlockSpec((Q, N), lambda c, g: (ci(c), c_off + g))
    dt = pl.BlockSpec((Q, H), lambda c, g: (ci(c), 0))
    hv = pl.BlockSpec((1, H), lambda c, g: (0, 0))
    y = pl.BlockSpec((Q, R * P), lambda c, g: (ci(c), g))
    st = pl.BlockSpec((None, R * P, N), lambda c, g: (ci(c), g, 0))
    return R, xs, bm, cm, dt, hv, y, st


def ssd_fwd(xbc, dt_raw, dt_bias, a_log, d_skip, d_inner, *, name):
    L = xbc.shape[0]
    H = dt_raw.shape[1]
    nc = L // CHUNK
    P, N = SSM_HEADDIM, SSM_STATE
    R, xs_s, bm_s, cm_s, dt_s, hv_s, y_s, st_s = _ssd_specs(L, d_inner, H, nc, False)

    def body(xs_ref, bm_ref, cm_ref, dt_ref, bias_ref, alog_ref, dsk_ref, y_ref, st_ref, state):
        c = pl.program_id(0)
        g = pl.program_id(1)

        @pl.when(c == 0)
        def _():
            for r in range(R):
                state[g * R + r] = jnp.zeros((P, N), F32)

        dtb = _softplus(dt_ref[...] + bias_ref[...])
        a_all = -jnp.exp(alog_ref[...])
        bm, cm = bm_ref[...], cm_ref[...]
        cb = lax.dot_general(cm.astype(BF16), bm.astype(BF16), (((1,), (1,)), ((), ())), preferred_element_type=F32)
        dts = [_pick_lane(dtb, g * R + r) for r in range(R)]
        a_cols = [dts[r] * _pick_lane(a_all, g * R + r) for r in range(R)]
        acs = _tri_cols(a_cols, upper=False)
        prevs = [state[g * R + r] for r in range(R)]
        res = []
        for r in range(R):
            res.append(_ssd_head(
                xs_ref[:, r * P : (r + 1) * P],
                dts[r],
                jnp.broadcast_to(acs[r], (CHUNK, CHUNK)),
                jnp.sum(a_cols[r], axis=0, keepdims=True),
                _pick_lane(dsk_ref[...], g * R + r),
                cb,
                bm,
                cm,
                prevs[r],
            ))
        for r in range(R):
            st_ref[r * P : (r + 1) * P, :] = prevs[r]
            y_ref[:, r * P : (r + 1) * P] = res[r][0]
            state[g * R + r] = res[r][1]

    return pl.pallas_call(
        body,
        out_shape=(jax.ShapeDtypeStruct((L, d_inner), F32), jax.ShapeDtypeStruct((nc, H * P, N), F32)),
        grid=(nc, SSM_GROUPS),
        in_specs=[xs_s, bm_s, cm_s, dt_s, hv_s, hv_s, hv_s],
        out_specs=(y_s, st_s),
        scratch_shapes=[pltpu.VMEM((H, P, N), F32)],
        compiler_params=_params("arbitrary", "arbitrary"),
        name=name,
    )(xbc, xbc, xbc, dt_raw, dt_bias, a_log, d_skip)


def ssd_bwd(xbc, dt_raw, dt_bias, a_log, d_skip, states, dy, d_inner, *, name):
    L, C = xbc.shape
    H = dt_raw.shape[1]
    nc = L // CHUNK
    P, N, Q = SSM_HEADDIM, SSM_STATE, CHUNK
    R, xs_s, bm_s, cm_s, dt_s, hv_s, y_s, st_s = _ssd_specs(L, d_inner, H, nc, True)

    def body(xs_ref, bm_ref, cm_ref, dt_ref, bias_ref, alog_ref, dsk_ref, st_ref, dy_ref,
             dxs_ref, dbm_ref, dcm_ref, ddt_ref, dbias_ref, dalog_ref, ddsk_ref, dstate):
        c = pl.program_id(0)
        g = pl.program_id(1)

        @pl.when(c == 0)
        def _():
            for r in range(R):
                dstate[g * R + r] = jnp.zeros((P, N), F32)

        @pl.when((c == 0) & (g == 0))
        def _():
            dbias_ref[...] = jnp.zeros_like(dbias_ref)
            dalog_ref[...] = jnp.zeros_like(dalog_ref)
            ddsk_ref[...] = jnp.zeros_like(ddsk_ref)

        @pl.when(g == 0)
        def _():
            ddt_ref[...] = jnp.zeros_like(ddt_ref)

        pre = dt_ref[...] + bias_ref[...]
        dtb = _softplus(pre)
        a_all = -jnp.exp(alog_ref[...])
        lanes_q = lax.broadcasted_iota(jnp.int32, (Q, H), 1)
        lanes_1 = lax.broadcasted_iota(jnp.int32, (1, H), 1)
        bm = bm_ref[...]
        cm = cm_ref[...]
        nt = (((1,), (1,)), ((), ()))
        cb = lax.dot_general(cm.astype(BF16), bm.astype(BF16), nt, preferred_element_type=F32)
        dts = [_pick_lane(dtb, g * R + r) for r in range(R)]
        a_negs = [_pick_lane(a_all, g * R + r) for r in range(R)]
        a_cols = [dts[r] * a_negs[r] for r in range(R)]
        acs = _tri_cols(a_cols, upper=False)
        dbm = jnp.zeros((Q, N), F32)
        dcm = jnp.zeros((Q, N), F32)
        dcb = jnp.zeros((Q, Q), F32)
        dd_row = jnp.zeros((1, H), F32)
        dstates = [dstate[g * R + r] for r in range(R)]
        dprevs, ddts, dacs_cols, dtots = [], [], [], []
        for r in range(R):
            h = g * R + r
            args = (
                xs_ref[:, r * P : (r + 1) * P],
                dts[r],
                jnp.broadcast_to(acs[r], (Q, Q)),
                jnp.sum(a_cols[r], axis=0, keepdims=True),
                _pick_lane(dsk_ref[...], h),
                cb,
                bm,
                cm,
                st_ref[r * P : (r + 1) * P, :],
            )
            _, vjp = jax.vjp(_ssd_head, *args)
            dxs, ddt, dacs, dtot, dd, dcb_h, dbm_h, dcm_h, dprev = vjp((dy_ref[:, r * P : (r + 1) * P], dstates[r]))
            dxs_ref[:, r * P : (r + 1) * P] = dxs
            dprevs.append(dprev)
            ddts.append(ddt)
            dacs_cols.append(jnp.sum(dacs, axis=1, keepdims=True))
            dtots.append(dtot)
            dbm = dbm + dbm_h
            dcm = dcm + dcm_h
            dcb = dcb + dcb_h
            dd_row = dd_row + jnp.where(lanes_1 == h, dd, 0.0)
        for r in range(R):
            dstate[g * R + r] = dprevs[r]
        ddt_blk = jnp.zeros((Q, H), F32)
        da_row = jnp.zeros((1, H), F32)
        for r, da_col in enumerate(_tri_cols(dacs_cols, upper=True)):
            h = g * R + r
            da_col = da_col + dtots[r]
            ddt_blk = ddt_blk + jnp.where(lanes_q == h, ddts[r] + da_col * a_negs[r], 0.0)
            da_row = da_row + jnp.where(lanes_1 == h, jnp.sum(da_col * dts[r], axis=0, keepdims=True), 0.0)
        dcb16 = dcb.astype(BF16)
        dbm_ref[...] = dbm + lax.dot_general(dcb16, cm.astype(BF16), (((0,), (0,)), ((), ())), preferred_element_type=F32)
        dcm_ref[...] = dcm + jnp.dot(dcb16, bm.astype(BF16), preferred_element_type=F32)
        ddt_pre = ddt_blk * _sigmoid(pre)
        ddt_ref[...] += ddt_pre
        dbias_ref[...] += jnp.sum(ddt_pre, axis=0, keepdims=True)
        dalog_ref[...] += da_row * a_all
        ddsk_ref[...] += dd_row

    ngrp = SSM_GROUPS
    hrow = jax.ShapeDtypeStruct((1, H), F32)
    dxs, dbm, dcm, ddt, dbias, dalog, ddsk = pl.pallas_call(
        body,
        out_shape=(
            jax.ShapeDtypeStruct((L, d_inner), F32),
            jax.ShapeDtypeStruct((L, ngrp * N), F32),
            jax.ShapeDtypeStruct((L, ngrp * N), F32),
            jax.ShapeDtypeStruct((L, H), F32),
            hrow,
            hrow,
            hrow,
        ),
        grid=(nc, ngrp),
        in_specs=[xs_s, bm_s, cm_s, dt_s, hv_s, hv_s, hv_s, st_s, y_s],
        out_specs=(
            y_s,
            pl.BlockSpec((Q, N), lambda c, g: (nc - 1 - c, g)),
            pl.BlockSpec((Q, N), lambda c, g: (nc - 1 - c, g)),
            dt_s,
            hv_s,
            hv_s,
            hv_s,
        ),
        scratch_shapes=[pltpu.VMEM((H, P, N), F32)],
        compiler_params=_params("arbitrary", "arbitrary"),
        name=name,
    )(xbc, xbc, xbc, dt_raw, dt_bias, a_log, d_skip, states, dy)
    return jnp.concatenate([dxs, dbm, dcm], axis=1), ddt, dbias, dalog, ddsk


def gnorm_fwd(y, zx, nw, *, name):
    L, DI = y.shape
    gw = DI // SSM_GROUPS
    tl = _pick(L, 512, 16)

    def body(y_ref, z_ref, nw_ref, o_ref):
        yz = y_ref[...] * _silu(z_ref[...])
        r = lax.rsqrt(jnp.mean(yz * yz, axis=-1, keepdims=True) + EPS)
        o_ref[...] = ((yz * r) * nw_ref[...]).astype(o_ref.dtype)

    tile = pl.BlockSpec((tl, gw), lambda i, g: (i, g))
    return pl.pallas_call(
        body,
        out_shape=jax.ShapeDtypeStruct((L, DI), BF16),
        grid=(L // tl, SSM_GROUPS),
        in_specs=[tile, tile, pl.BlockSpec((1, gw), lambda i, g: (0, g))],
        out_specs=tile,
        compiler_params=_params("parallel", "parallel"),
        name=name,
    )(y, zx, nw)


def gnorm_bwd(y, zx, nw, dout, *, name):
    L, DI = y.shape
    gw = DI // SSM_GROUPS
    tl = _pick(L, 512, 16)

    def body(y_ref, z_ref, nw_ref, do_ref, dy_ref, dz_ref, dnw_ref):
        @pl.when(pl.program_id(1) == 0)
        def _():
            dnw_ref[...] = jnp.zeros_like(dnw_ref)

        yv = y_ref[...]
        zv = z_ref[...]
        sz = _silu(zv)
        yz = yv * sz
        r = lax.rsqrt(jnp.mean(yz * yz, axis=-1, keepdims=True) + EPS)
        n = yz * r
        dov = do_ref[...]
        dnw_ref[...] += jnp.sum(dov * n, axis=0, keepdims=True)
        dn = dov * nw_ref[...]
        dyz = r * (dn - n * jnp.mean(dn * n, axis=-1, keepdims=True))
        dy_ref[...] = dyz * sz
        dz_ref[...] = dyz * yv * _dsilu(zv)

    tile = pl.BlockSpec((tl, gw), lambda g, i: (i, g))
    row = pl.BlockSpec((1, gw), lambda g, i: (0, g))
    return pl.pallas_call(
        body,
        out_shape=(
            jax.ShapeDtypeStruct((L, DI), F32),
            jax.ShapeDtypeStruct((L, DI), F32),
            jax.ShapeDtypeStruct((1, DI), F32),
        ),
        grid=(SSM_GROUPS, L // tl),
        in_specs=[tile, tile, row, tile],
        out_specs=(tile, tile, row),
        compiler_params=_params("parallel", "arbitrary"),
        name=name,
    )(y, zx, nw, dout)


def _attn_head(q, kp, kc, vp, vc, sink, has_prev):
    rows, w = q.shape[0], kc.shape[0]
    nt = (((1,), (1,)), ((), ()))
    qb = q.astype(BF16)
    sc = lax.dot_general(qb, kc.astype(BF16), nt, preferred_element_type=F32) * HEAD_DIM ** -0.5
    sp = lax.dot_general(qb, kp.astype(BF16), nt, preferred_element_type=F32) * HEAD_DIM ** -0.5
    ii = jnp.bitwise_and(lax.broadcasted_iota(jnp.int32, (rows, w), 0), w - 1)
    jj = lax.broadcasted_iota(jnp.int32, (rows, w), 1)
    lc = jnp.where(jj <= ii, sc, -jnp.inf)
    lp = jnp.where((jj > ii) & has_prev, sp, -jnp.inf)
    m = jnp.maximum(jnp.maximum(jnp.max(lc, axis=1, keepdims=True), jnp.max(lp, axis=1, keepdims=True)), sink)
    m = lax.stop_gradient(m)
    pc = jnp.exp(lc - m)
    pp = jnp.exp(lp - m)
    denom = jnp.sum(pc, axis=1, keepdims=True) + jnp.sum(pp, axis=1, keepdims=True) + jnp.exp(sink - m)
    o = jnp.dot((pc / denom).astype(BF16), vc.astype(BF16), preferred_element_type=F32)
    return o + jnp.dot((pp / denom).astype(BF16), vp.astype(BF16), preferred_element_type=F32)


def attn_fwd(q, kv, sinks, *, name):
    L, DQ = q.shape
    heads = DQ // HEAD_DIM
    rep = heads // KV_HEADS
    nb = L // WINDOW
    kw = KV_HEADS * HEAD_DIM
    W, HD = WINDOW, HEAD_DIM

    def body(q_ref, kp_ref, kc_ref, vp_ref, vc_ref, s_ref, o_ref):
        has_prev = pl.program_id(0) > 0
        for kh in range(KV_HEADS):
            ks = slice(kh * HD, (kh + 1) * HD)
            hs = [kh * rep + rr for rr in range(rep)]
            o = _attn_head(
                jnp.concatenate([q_ref[:, h * HD : (h + 1) * HD] for h in hs], axis=0),
                kp_ref[:, ks], kc_ref[:, ks], vp_ref[:, ks], vc_ref[:, ks],
                jnp.concatenate([jnp.broadcast_to(s_ref[:, h : h + 1], (W, 1)) for h in hs], axis=0), has_prev,
            )
            for rr, h in enumerate(hs):
                o_ref[:, h * HD : (h + 1) * HD] = o[rr * W : (rr + 1) * W].astype(o_ref.dtype)

    return pl.pallas_call(
        body,
        out_shape=jax.ShapeDtypeStruct((L, DQ), BF16),
        grid=(nb,),
        in_specs=[
            pl.BlockSpec((W, DQ), lambda n: (n, 0)),
            pl.BlockSpec((W, kw), lambda n: (jnp.maximum(n - 1, 0), 0)),
            pl.BlockSpec((W, kw), lambda n: (n, 0)),
            pl.BlockSpec((W, kw), lambda n: (jnp.maximum(n - 1, 0), 1)),
            pl.BlockSpec((W, kw), lambda n: (n, 1)),
            pl.BlockSpec((1, heads), lambda n: (0, 0)),
        ],
        out_specs=pl.BlockSpec((W, DQ), lambda n: (n, 0)),
        compiler_params=_params("parallel"),
        name=name,
    )(q, kv, kv, kv, kv, sinks)


def attn_bwd(q, kv, sinks, do, dkv_in, *, name):
    L, DQ = q.shape
    heads = DQ // HEAD_DIM
    rep = heads // KV_HEADS
    nb = L // WINDOW
    kw = KV_HEADS * HEAD_DIM
    W, HD = WINDOW, HEAD_DIM

    def blk(n):
        return jnp.minimum(n, nb - 1)

    def prev(n):
        return jnp.maximum(blk(n) - 1, 0)

    def outb(n):
        return jnp.maximum(n - 1, 0)

    def body(q_ref, kp_ref, kc_ref, vp_ref, vc_ref, s_ref, do_ref, dki_ref, dvi_ref,
             dq_ref, dk_ref, dv_ref, ds_ref, dk_cur, dv_cur):
        n = pl.program_id(0)
        has_prev = n > 0

        @pl.when(n == 0)
        def _():
            ds_ref[...] = jnp.zeros_like(ds_ref)
            dk_cur[...] = jnp.zeros_like(dk_cur)
            dv_cur[...] = jnp.zeros_like(dv_cur)

        @pl.when(n == nb)
        def _():
            dk_ref[...] = dki_ref[...] + dk_cur[...]
            dv_ref[...] = dvi_ref[...] + dv_cur[...]

        @pl.when(n < nb)
        def _():
            lanes = lax.broadcasted_iota(jnp.int32, (1, heads), 1)
            ds_row = jnp.zeros((1, heads), F32)
            for kh in range(KV_HEADS):
                ks = slice(kh * HD, (kh + 1) * HD)
                hs = [kh * rep + rr for rr in range(rep)]
                _, vjp = jax.vjp(
                    functools.partial(_attn_head, has_prev=has_prev),
                    jnp.concatenate([q_ref[:, h * HD : (h + 1) * HD] for h in hs], axis=0),
                    kp_ref[:, ks], kc_ref[:, ks], vp_ref[:, ks], vc_ref[:, ks],
                    jnp.concatenate([jnp.broadcast_to(s_ref[:, h : h + 1], (W, 1)) for h in hs], axis=0),
                )
                dq, dkp, dkc, dvp, dvc, dsk = vjp(
                    jnp.concatenate([do_ref[:, h * HD : (h + 1) * HD] for h in hs], axis=0))
                for rr, h in enumerate(hs):
                    dq_ref[:, h * HD : (h + 1) * HD] = dq[rr * W : (rr + 1) * W]
                    ds_row = ds_row + jnp.where(lanes == h, jnp.sum(dsk[rr * W : (rr + 1) * W], axis=0, keepdims=True), 0.0)
                dk_ref[:, ks] = dki_ref[:, ks] + dk_cur[:, ks] + dkp
                dv_ref[:, ks] = dvi_ref[:, ks] + dv_cur[:, ks] + dvp
                dk_cur[:, ks] = dkc
                dv_cur[:, ks] = dvc
            ds_ref[...] += ds_row

    dq, dk, dv, ds = pl.pallas_call(
        body,
        out_shape=(
            jax.ShapeDtypeStruct((L, DQ), F32),
            jax.ShapeDtypeStruct((L, kw), F32),
            jax.ShapeDtypeStruct((L, kw), F32),
            jax.ShapeDtypeStruct((1, heads), F32),
        ),
        grid=(nb + 1,),
        in_specs=[
            pl.BlockSpec((W, DQ), lambda n: (blk(n), 0)),
            pl.BlockSpec((W, kw), lambda n: (prev(n), 0)),
            pl.BlockSpec((W, kw), lambda n: (blk(n), 0)),
            pl.BlockSpec((W, kw), lambda n: (prev(n), 1)),
            pl.BlockSpec((W, kw), lambda n: (blk(n), 1)),
            pl.BlockSpec((1, heads), lambda n: (0, 0)),
            pl.BlockSpec((W, DQ), lambda n: (blk(n), 0)),
            pl.BlockSpec((W, kw), lambda n: (outb(n), 0)),
            pl.BlockSpec((W, kw), lambda n: (outb(n), 1)),
        ],
        out_specs=(
            pl.BlockSpec((W, DQ), lambda n: (blk(n), 0)),
            pl.BlockSpec((W, kw), lambda n: (outb(n), 0)),
            pl.BlockSpec((W, kw), lambda n: (outb(n), 0)),
            pl.BlockSpec((1, heads), lambda n: (0, 0)),
        ),
        scratch_shapes=[pltpu.VMEM((W, kw), F32), pltpu.VMEM((W, kw), F32)],
        compiler_params=_params("arbitrary"),
        name=name,
    )(q, kv, kv, kv, kv, sinks, do, dkv_in, dkv_in)
    return dq, jnp.concatenate([dk, dv], axis=1), ds


def final_loss(x, fw, target, *, name):
    L, D = x.shape
    tl = _pick(L, 512, 8)

    def body(x_ref, fw_ref, t_ref, loss_ref, dx_ref, dfw_ref):
        @pl.when(pl.program_id(0) == 0)
        def _():
            loss_ref[...] = jnp.zeros_like(loss_ref)
            dfw_ref[...] = jnp.zeros_like(dfw_ref)

        xv = x_ref[...]
        fwv = fw_ref[...]
        r = lax.rsqrt(jnp.mean(xv * xv, axis=-1, keepdims=True) + EPS)
        xhat = xv * r
        err = xhat * fwv - t_ref[...]
        loss_ref[...] += 0.5 * jnp.sum(jnp.mean(err * err, axis=-1, keepdims=True), axis=0, keepdims=True)
        dy = err * (1.0 / D)
        dfw_ref[...] += jnp.sum(dy * xhat, axis=0, keepdims=True)
        dxhat = dy * fwv
        dx_ref[...] = r * (dxhat - xhat * jnp.mean(dxhat * xhat, axis=-1, keepdims=True))

    tile = pl.BlockSpec((tl, D), lambda i: (i, 0))
    row = pl.BlockSpec((1, D), lambda i: (0, 0))
    return pl.pallas_call(
        body,
        out_shape=(
            jax.ShapeDtypeStruct((1, 1), F32),
            jax.ShapeDtypeStruct((L, D), F32),
            jax.ShapeDtypeStruct((1, D), F32),
        ),
        grid=(L // tl,),
        in_specs=[tile, row, tile],
        out_specs=(pl.BlockSpec((1, 1), lambda i: (0, 0)), tile, row),
        compiler_params=_params("arbitrary"),
        name=name,
    )(x, fw, target)


def outer8(ct, d, *, name):
    D, B = ct.shape
    S, _, N = d.shape
    tm = _pick(D, 512, 8)
    tn = _pick(N, 256, LANE)

    def body(c_ref, d_ref, o_ref):
        acc = c_ref[:, 0:1] * d_ref[0:1, :]
        for b in range(1, B):
            acc = acc + c_ref[:, b : b + 1] * d_ref[b : b + 1, :]
        o_ref[...] = acc

    return pl.pallas_call(
        body,
        out_shape=jax.ShapeDtypeStruct((S, D, N), F32),
        grid=(S, D // tm, N // tn),
        in_specs=[
            pl.BlockSpec((tm, B), lambda s, i, j: (i, 0)),
            pl.BlockSpec((None, B, tn), lambda s, i, j: (s, 0, j)),
        ],
        out_specs=pl.BlockSpec((None, tm, tn), lambda s, i, j: (s, i, j)),
        compiler_params=_params("parallel", "parallel", "parallel"),
        name=name,
    )(ct, d)


def reduce8(g, *, name):
    nd, R, N = g.shape

    def body(g_ref, o_ref):
        acc = g_ref[0]
        for b in range(1, nd):
            acc = acc + g_ref[b]
        o_ref[...] = acc

    return pl.pallas_call(
        body,
        out_shape=jax.ShapeDtypeStruct((R, N), F32),
        name=name,
    )(g)


def _as3(a):
    if a.ndim == 1:
        return a.reshape(1, 1, -1)
    if a.ndim == 2:
        return a.reshape((1,) + a.shape)
    return a.reshape((-1,) + a.shape[-2:])


def adamw(w, g, m, v, *, name):
    shape = w.shape
    w3, g3, m3, v3 = _as3(w), _as3(g), _as3(m), _as3(v)
    B, R, C = w3.shape
    tr = _pick(R, max(8, (1 << 19) // max(C, 1) // 8 * 8), 8)

    def body(w_ref, g_ref, m_ref, v_ref, d_ref, nm_ref, nv_ref):
        gv = g_ref[...]
        mn = ADAM_B1 * m_ref[...] + (1.0 - ADAM_B1) * gv
        vn = ADAM_B2 * v_ref[...] + (1.0 - ADAM_B2) * (gv * gv)
        m_hat = mn / (1.0 - ADAM_B1 ** ADAM_STEP)
        v_hat = vn / (1.0 - ADAM_B2 ** ADAM_STEP)
        d_ref[...] = -ADAM_LR * (m_hat / (jnp.sqrt(v_hat) + ADAM_EPS) + ADAM_WD * w_ref[...])
        nm_ref[...] = mn
        nv_ref[...] = vn

    tile = pl.BlockSpec((None, tr, C), lambda b, i: (b, i, 0))
    sds = jax.ShapeDtypeStruct((B, R, C), F32)
    d, nm, nv = pl.pallas_call(
        body,
        out_shape=(sds, sds, sds),
        grid=(B, R // tr),
        in_specs=[tile, tile, tile, tile],
        out_specs=(tile, tile, tile),
        compiler_params=_params("parallel", "parallel"),
        name=name,
    )(w3, g3, m3, v3)
    return d.reshape(shape), nm.reshape(shape), nv.reshape(shape)


def _place():
    return lax.axis_index("x"), lax.axis_index("y"), lax.axis_index("c")


def _flip(v, bit):
    return (1 - v) if bit else v


def ag8(v, *, act=None, name):
    R, N = v.shape

    def body(v_ref, out_ref, stage, send_sems, recv_sems):
        x, y, c = _place()
        me = 4 * x + 2 * y + c
        val = v_ref[...]
        if act is not None:
            val = act(val)
        stage[...] = val
        out_ref[me] = val
        sends = []
        for k in range(1, N_DEV):
            px, py, pc = _flip(x, k & 4), _flip(y, k & 2), _flip(c, k & 1)
            cp = pltpu.make_async_remote_copy(
                src_ref=stage, dst_ref=out_ref.at[me], send_sem=send_sems.at[k - 1], recv_sem=recv_sems.at[k - 1],
                device_id=(px, py, pc), device_id_type=MESH,
            )
            cp.start()
            sends.append(cp)
        for k in range(1, N_DEV):
            px, py, pc = _flip(x, k & 4), _flip(y, k & 2), _flip(c, k & 1)
            pltpu.make_async_remote_copy(
                src_ref=stage, dst_ref=out_ref.at[4 * px + 2 * py + pc], send_sem=send_sems.at[k - 1],
                recv_sem=recv_sems.at[k - 1], device_id=(px, py, pc), device_id_type=MESH,
            ).wait_recv()
        for cp in sends:
            cp.wait_send()

    return pl.pallas_call(
        body,
        out_shape=jax.ShapeDtypeStruct((N_DEV, R, N), F32),
        in_specs=[pl.BlockSpec(memory_space=pltpu.VMEM)],
        out_specs=pl.BlockSpec(memory_space=pltpu.VMEM),
        scratch_shapes=[
            pltpu.VMEM((R, N), F32),
            pltpu.SemaphoreType.DMA((N_DEV - 1,)),
            pltpu.SemaphoreType.DMA((N_DEV - 1,)),
        ],
        name=name,
    )(v)


def _other_chips(x, y):
    chips = [(1 - x, y), (x, 1 - y), (1 - x, 1 - y)]
    return chips, [2 * px + py for px, py in chips]


_HBM = pl.BlockSpec(memory_space=pltpu.HBM)


_SEM = pl.BlockSpec(memory_space=pltpu.SEMAPHORE)
_ANY = pl.BlockSpec(memory_space=pl.ANY)
_EFFECT = pltpu.SideEffectType.DATAFLOW_SIDE_EFFECTING


def _gather_copies(srcs, lands, send_sems, recv_sems):
    x, y, c = _place()
    k_me = 2 * x + y
    chips, kidx = _other_chips(x, y)
    cps = []
    for w in range(len(srcs)):
        for j, (px, py) in enumerate(chips):
            def copy(dst, w=w, j=j, px=px, py=py):
                return pltpu.make_async_remote_copy(
                    src_ref=srcs[w].at[:, c], dst_ref=dst, send_sem=send_sems.at[3 * w + j],
                    recv_sem=recv_sems.at[3 * w + j], device_id=(px, py, c), device_id_type=MESH,
                )
            cps.append((copy(lands[w].at[:, k_me, c]), copy(lands[w].at[:, kidx[j], c])))
    return cps


def _fill_copies(srcs, lands, send_sems, recv_sems):
    x, y, c = _place()
    k_me = 2 * x + y
    _, kidx = _other_chips(x, y)
    sib = (x, y, 1 - c)
    cps = []
    for w in range(len(srcs)):
        own = pltpu.make_async_remote_copy(
            src_ref=srcs[w], dst_ref=lands[w].at[:, k_me], send_sem=send_sems.at[4 * w + 3], recv_sem=recv_sems.at[4 * w + 3],
            device_id=sib, device_id_type=MESH,
        )
        cps.append((own, own))
        for j in range(3):
            def copy(half, w=w, j=j):
                part = lands[w].at[:, kidx[j], half]
                return pltpu.make_async_remote_copy(
                    src_ref=part, dst_ref=part, send_sem=send_sems.at[4 * w + j], recv_sem=recv_sems.at[4 * w + j],
                    device_id=sib, device_id_type=MESH,
                )
            cps.append((copy(c), copy(1 - c)))
    return cps


def _sibling_copies(srcs, lands, send_sems, recv_sems):
    x, y, c = _place()
    cps = []
    for w in range(len(srcs)):
        cp = pltpu.make_async_remote_copy(
            src_ref=srcs[w].at[:, :, 1 - c], dst_ref=lands[w], send_sem=send_sems.at[w], recv_sem=recv_sems.at[w],
            device_id=(x, y, 1 - c), device_id_type=MESH,
        )
        cps.append((cp, cp))
    return cps


def _reduce_copies(srcs, lands, send_sems, recv_sems):
    x, y, c = _place()
    chips, kidx = _other_chips(x, y)
    cps = []
    for w in range(len(srcs)):
        for j, (px, py) in enumerate(chips):
            cp = pltpu.make_async_remote_copy(
                src_ref=srcs[w].at[:, kidx[j]], dst_ref=lands[w].at[j], send_sem=send_sems.at[3 * w + j],
                recv_sem=recv_sems.at[3 * w + j], device_id=(px, py, c), device_id_type=MESH,
            )
            cps.append((cp, cp))
    return cps


def split_start(copies, srcs, land_shapes, after, *, name, lands=None, per=3):
    n = len(srcs)

    def body(*refs):
        src_refs, land_refs = refs[:n], refs[n : 2 * n]
        send_sems, recv_sems = refs[2 * n + 1], refs[2 * n + 2]
        token = refs[-1]
        for cp, _ in copies(src_refs, land_refs, send_sems, recv_sems):
            cp.start()
        token[...] = jnp.zeros_like(token)

    if lands is None:
        lands = [lax.empty(sh, s.dtype) for sh, s in zip(land_shapes, srcs)]
    land_shapes = [a.shape for a in lands]
    lands = [pltpu.with_memory_space_constraint(a, pltpu.HBM) for a in lands]
    srcs = [pltpu.with_memory_space_constraint(s, pltpu.HBM) for s in srcs]
    out = pl.pallas_call(
        body,
        out_shape=(
            pltpu.SemaphoreType.DMA((per * n,)), pltpu.SemaphoreType.DMA((per * n,)),
            *[pltpu.HBM(s.shape, s.dtype) for s in srcs],
            *[pltpu.HBM(sh, s.dtype) for sh, s in zip(land_shapes, srcs)],
            jax.ShapeDtypeStruct((8, LANE), F32),
        ),
        in_specs=[_HBM] * (2 * n) + [_ANY],
        out_specs=(_SEM, _SEM, *([_HBM] * (2 * n)), pl.BlockSpec(memory_space=pltpu.VMEM)),
        input_output_aliases={i: 2 + i for i in range(2 * n)},
        compiler_params=pltpu.CompilerParams(has_side_effects=_EFFECT),
        name=name,
    )(*srcs, *lands, after)
    return out[0], out[1], list(out[2 : 2 + n]), list(out[2 + n : 2 + 2 * n]), out[-1]


def split_wait(copies, send_sems, recv_sems, srcs, lands, after, *, name):
    n = len(srcs)

    def body(*refs):
        src_refs, land_refs = refs[:n], refs[n : 2 * n]
        send_ref, recv_ref = refs[2 * n], refs[2 * n + 1]
        for sent, arrives in copies(src_refs, land_refs, send_ref, recv_ref):
            sent.wait_send()
            arrives.wait_recv()

    out = pl.pallas_call(
        body,
        out_shape=tuple(pltpu.HBM(a.shape, a.dtype) for a in list(srcs) + list(lands)),
        in_specs=[_HBM] * (2 * n) + [_SEM, _SEM, _ANY],
        out_specs=tuple([_HBM] * (2 * n)),
        input_output_aliases={i: i for i in range(2 * n)},
        compiler_params=pltpu.CompilerParams(has_side_effects=_EFFECT),
        name=name,
    )(*srcs, *lands, send_sems, recv_sems, after)
    return list(out[:n]), list(out[n:])


def rs_share(halves, *, name):
    n = len(halves)

    def body(*refs):
        outs = refs[n : 2 * n]
        send_sems, recv_sems = refs[2 * n :]
        x, y, c = _place()
        cps = []
        for w in range(n):
            cp = pltpu.make_async_remote_copy(
                src_ref=outs[w].at[:, c], dst_ref=outs[w].at[:, c], send_sem=send_sems.at[w], recv_sem=recv_sems.at[w],
                device_id=(x, y, 1 - c), device_id_type=MESH,
            )
            cp.start()
            cps.append(cp)
        for w, cp in enumerate(cps):
            cp.wait_send()
            pltpu.make_async_remote_copy(
                src_ref=outs[w].at[:, c], dst_ref=outs[w].at[:, 1 - c], send_sem=send_sems.at[w], recv_sem=recv_sems.at[w],
                device_id=(x, y, 1 - c), device_id_type=MESH,
            ).wait_recv()

    return pl.pallas_call(
        body,
        out_shape=tuple(jax.ShapeDtypeStruct(h.shape, h.dtype) for h in halves),
        in_specs=[_HBM] * n,
        out_specs=tuple([_HBM] * n),
        scratch_shapes=[pltpu.SemaphoreType.DMA((n,)), pltpu.SemaphoreType.DMA((n,))],
        input_output_aliases={w: w for w in range(n)},
        name=name,
    )(*halves)


def _row_tile(R, C):
    return _pick(R, max(16, (1 << 19) // C // 16 * 16), 16)


def _my_core():
    return lax.axis_index("c")


def _my_chip():
    return 2 * lax.axis_index("x") + lax.axis_index("y")


def rs_add_pair(g, r, *, name):
    M, K, _, R, C = g.shape
    tr = _row_tile(R, C)

    def body(g_ref, r_ref, o_ref):
        o_ref[...] = (g_ref[...].astype(F32) + r_ref[...].astype(F32)).astype(o_ref.dtype)

    blk = pl.BlockSpec((None, None, tr, C), lambda m, k, i: (m, k, i, 0))
    return pl.pallas_call(
        body,
        out_shape=jax.ShapeDtypeStruct((M, K, R, C), BF16),
        grid=(M, K, R // tr),
        in_specs=[pl.BlockSpec((None, None, None, tr, C), lambda m, k, i: (m, k, _my_core(), i, 0)), blk],
        out_specs=blk,
        compiler_params=_params("parallel", "parallel", "parallel"),
        name=name,
    )(g, r)


def rs_add_final(g, r, t, full, m0, *, name):
    M, K, _, R, C = g.shape
    tr = _row_tile(R, C)

    def body(g_ref, r_ref, t_ref, full_ref, o_ref):
        acc = g_ref[...].astype(F32) + r_ref[...].astype(F32)
        for j in range(3):
            acc = acc + t_ref[j].astype(F32)
        o_ref[...] = acc

    return pl.pallas_call(
        body,
        out_shape=jax.ShapeDtypeStruct(full.shape, full.dtype),
        grid=(M, R // tr),
        in_specs=[
            pl.BlockSpec((None, None, None, tr, C), lambda m, i: (m, _my_chip(), _my_core(), i, 0)),
            pl.BlockSpec((None, None, tr, C), lambda m, i: (m, _my_chip(), i, 0)),
            pl.BlockSpec((3, None, tr, C), lambda m, i: (0, m, i, 0)),
            _ANY,
        ],
        out_specs=pl.BlockSpec((None, None, tr, C), lambda m, i: (m0 + m, _my_core(), i, 0)),
        input_output_aliases={3: 0},
        compiler_params=_params("parallel", "parallel"),
        name=name,
    )(g, r, t, full)


WEIGHTS = ["ffn_norm_w", "ffn_w_gu", "ffn_w_down", "mod_w", "mod_b", "mix_norm_w", "ssm_w_in", "ssm_conv_w", "ssm_conv_b",
           "ssm_dt_bias", "ssm_a_log", "ssm_d", "ssm_norm_w", "ssm_w_out", "kv_norm_w", "kv_mod_w", "kv_mod_b", "w_kv", "b_kv",
           "attn_w_q", "attn_b_q", "attn_sinks", "attn_w_o", "attn_b_o", "final_norm_w"]
GATHERED = ["ffn_w_gu", "ffn_w_down", "ssm_w_in", "ssm_w_out", "w_kv", "attn_w_q", "attn_w_o"]
COLUMN_PARALLEL = ["mod_w", "kv_mod_w"]
SMALL_SHARDED = ["ffn_norm_w", "ssm_conv_w", "ssm_conv_b", "ssm_norm_w"]
SMALL = [n for n in WEIGHTS if n not in GATHERED and n not in COLUMN_PARALLEL]


def _row_halves(a):
    a = a.reshape((-1,) + a.shape[-2:])
    return a.reshape(a.shape[0], 2, a.shape[1] // 2, a.shape[2])


def _pack(arrs, rows=8):
    flat = jnp.concatenate([a.reshape(-1) for a in arrs])
    n = flat.shape[0]
    pad = (-n) % (rows * LANE)
    return jnp.pad(flat, (0, pad)).reshape(rows, -1), n


def _unpack(flat, like):
    out, o = [], 0
    for s in like:
        k = 1
        for d in s:
            k *= d
        out.append(flat[o : o + k].reshape(s))
        o += k
    return out


def kernel(x, c, ffn_norm_w, ffn_w_gu, ffn_w_down, mod_w, mod_b, mix_norm_w, ssm_w_in, ssm_conv_w, ssm_conv_b, ssm_dt_bias, ssm_a_log, ssm_d, ssm_norm_w, ssm_w_out, kv_norm_w, kv_mod_w, kv_mod_b, w_kv, b_kv, attn_w_q, attn_b_q, attn_sinks, attn_w_o, attn_b_o, final_norm_w, loss_target, m_ffn_norm_w, m_ffn_w_gu, m_ffn_w_down, m_mod_w, m_mod_b, m_mix_norm_w, m_ssm_w_in, m_ssm_conv_w, m_ssm_conv_b, m_ssm_dt_bias, m_ssm_a_log, m_ssm_d, m_ssm_norm_w, m_ssm_w_out, m_kv_norm_w, m_kv_mod_w, m_kv_mod_b, m_w_kv, m_b_kv, m_attn_w_q, m_attn_b_q, m_attn_sinks, m_attn_w_o, m_attn_b_o, m_final_norm_w, v_ffn_norm_w, v_ffn_w_gu, v_ffn_w_down, v_mod_w, v_mod_b, v_mix_norm_w, v_ssm_w_in, v_ssm_conv_w, v_ssm_conv_b, v_ssm_dt_bias, v_ssm_a_log, v_ssm_d, v_ssm_norm_w, v_ssm_w_out, v_kv_norm_w, v_kv_mod_w, v_kv_mod_b, v_w_kv, v_b_kv, v_attn_w_q, v_attn_b_q, v_attn_sinks, v_attn_w_o, v_attn_b_o, v_final_norm_w):
    env = dict(locals())
    W = {n: env[n] for n in WEIGHTS}
    MOM = {n: env["m_" + n] for n in WEIGHTS}
    VAR = {n: env["v_" + n] for n in WEIGHTS}

    ax, ay, ac = _place()
    kme = 2 * ax + ay
    me = 4 * ax + 2 * ay + ac

    xs = x[0]
    target = loss_target[0]
    L, D = xs.shape
    depth, n_a = ffn_w_gu.shape[0], ssm_w_in.shape[0]
    n_b = depth - n_a
    T = ffn_w_gu.shape[-1]
    DI = ssm_w_out.shape[1] * N_CHIPS
    CI = ssm_w_in.shape[2]
    CC = ssm_conv_w.shape[2] * N_CHIPS
    MW = mod_w.shape[2]
    KW = kv_mod_w.shape[1]
    KVD = w_kv.shape[1]

    def chip_cols(a, width):
        return lax.dynamic_slice_in_dim(a, kme * width, width, axis=a.ndim - 1)

    def ffn_items(i, j):
        return [("ffn_w_gu", 2 * i + j, ffn_w_gu[i, j]), ("ffn_w_down", 2 * i + j, ffn_w_down[i, j])]

    def mix_items(i):
        if i < n_a:
            return [("ssm_w_in", i, ssm_w_in[i]), ("ssm_w_out", i, ssm_w_out[i])]
        return [("attn_w_q", i - n_a, attn_w_q[i - n_a]), ("attn_w_o", i - n_a, attn_w_o[i - n_a])]

    def layer_items(i, order):
        kv_items = [("w_kv", 0, w_kv)] if i == n_a else []
        if order == "fwd":
            return kv_items + ffn_items(i, 0) + mix_items(i) + ffn_items(i, 1)
        return ffn_items(i, 1) + mix_items(i) + ffn_items(i, 0) + kv_items

    fwd_stages = [ffn_items(0, 0), mix_items(0) + ffn_items(0, 1)] + [layer_items(i, "fwd") for i in range(1, depth)]
    bwd_stages = [layer_items(i, "bwd") for i in range(depth - 1, 0, -1)] + [ffn_items(0, 1) + mix_items(0), ffn_items(0, 0)]

    gw, inflight = {}, {}

    def chips_begin(s, after):
        keys = [(n, m0) for n, m0, _ in fwd_stages[s]]
        shards = [_row_halves(a.astype(BF16)) for _, _, a in fwd_stages[s]]
        land_shapes = [(sh.shape[0], N_CHIPS) + sh.shape[1:] for sh in shards]
        ssem, rsem, srcs, lands, token = split_start(_gather_copies, shards, land_shapes, after, name=f"gather_start_{s}")
        inflight[s] = (keys, ssem, rsem, srcs, lands)
        return token[0:1, 0:1]

    def n_first(s):
        return 2 if s < 2 else len(fwd_stages[s]) - 4

    def cores_begin(s, after):
        keys, ssem, rsem, srcs, lands = inflight.pop(s)
        srcs, lands = split_wait(_gather_copies, ssem, rsem, srcs, lands, after, name=f"gather_wait_{s}")
        k = n_first(s)
        a_sem, b_sem, a_src, a_land, tok = split_start(
            _fill_copies, srcs[:k], None, lands[0], name=f"fill_start_{s}a", lands=lands[:k], per=4)
        if k < len(keys):
            inflight[s] = (keys[k:],) + split_start(
                _fill_copies, srcs[k:], None, tok, name=f"fill_start_{s}b", lands=lands[k:], per=4)
            tok = inflight[s][-1]
        gw.update(zip(keys[:k], split_wait(_fill_copies, a_sem, b_sem, a_src, a_land, tok, name=f"fill_wait_{s}a")[1]))
        return tok[0:1, 0:1]

    def cores_end(s, after):
        if s in inflight:
            keys, ssem, rsem, srcs, lands, _ = inflight.pop(s)
            gw.update(zip(keys, split_wait(_fill_copies, ssem, rsem, srcs, lands, after, name=f"fill_wait_{s}b")[1]))

    def g_gu(i, j):
        return gw["ffn_w_gu", 2 * i + j].reshape(N_CHIPS, D, T)

    def g_dn(i, j):
        return gw["ffn_w_down", 2 * i + j].reshape(2, T, D)

    def g_full(n, m0):
        a = gw[n, m0]
        return a.reshape(N_CHIPS * 2 * a.shape[-2], a.shape[-1])

    c = c + chips_begin(0, c)

    sm_like = [W[n].shape for n in SMALL_SHARDED]
    sm_pack, sm_n = _pack([W[n] for n in SMALL_SHARDED])
    sm_all = ag8(sm_pack, name="ag_small_w")[0::2].reshape(N_CHIPS, -1)[:, :sm_n]
    full = {}
    for n, part in zip(SMALL_SHARDED, zip(*[_unpack(sm_all[k], sm_like) for k in range(N_CHIPS)])):
        full[n] = jnp.concatenate(part, axis=-1)

    c_all = ag8(c, act=_silu, name="ag_c").reshape(N_DEV, D)
    p_mod = mm(c_all, mod_w, bias=chip_cols(mod_b, MW)[:, None, :], name="mod_mm")
    p_kv = mm(c_all, kv_mod_w, bias=chip_cols(kv_mod_b, KW)[None, :], name="kvmod_mm")
    p_all = jnp.concatenate([jnp.transpose(p_mod, (1, 0, 2)).reshape(N_DEV, depth * MW), p_kv], axis=1)
    p_mine = lax.dynamic_index_in_dim(ag8(p_all, name="ag_mod")[0::2], me, axis=1, keepdims=False)
    mod = jnp.transpose(p_mine[:, : depth * MW].reshape(N_CHIPS, depth, MW), (1, 0, 2)).reshape(depth, N_MOD * D)
    kvmod = p_mine[:, depth * MW :].reshape(1, 2 * D)
    mods = [[mod[i : i + 1, j * D : (j + 1) * D] for j in range(N_MOD)] for i in range(depth)]
    kv_shift, kv_scale = kvmod[:, :D], kvmod[:, D:]

    def ffn_fwd(xin, i, j, sh, sc, gt):
        h = norm_mod_fwd(xin, full["ffn_norm_w"][i, j][None], sc, sh, name=f"ffn_norm_{i}_{j}")
        gu, a = ffn_up(h, g_gu(i, j), name=f"ffn_gu_{i}_{j}")
        f, xo = mm(a, g_dn(i, j), reduce_s=True, resid=(xin, gt, FFN_HALF), name=f"ffn_down_{i}_{j}")
        return xo, (xin, gu, a, f, h)

    def ssm_fwd(xin, i, sh, sc, gt):
        h = norm_mod_fwd(xin, mix_norm_w[i][None], sc, sh, name=f"mix_norm_{i}")
        zx4 = mm(h, gw["ssm_w_in", i].reshape(N_CHIPS, D, CI), name=f"ssm_in_{i}")
        zx = jnp.transpose(zx4, (1, 0, 2)).reshape(L, N_CHIPS * CI)
        xbc = conv_fwd(zx, full["ssm_conv_w"][i], full["ssm_conv_b"][i][None], DI, name=f"ssm_conv_{i}")
        dt_raw = zx[:, DI + CC :]
        y, states = ssd_fwd(xbc, dt_raw, ssm_dt_bias[i][None], ssm_a_log[i][None], ssm_d[i][None], DI, name=f"ssd_{i}")
        yn = gnorm_fwd(y, zx, full["ssm_norm_w"][i][None], name=f"ssm_gnorm_{i}")
        f, xo = mm(yn, g_full("ssm_w_out", i), resid=(xin, gt, 1.0), name=f"ssm_out_{i}")
        return xo, (xin, zx, xbc, dt_raw, y, states, yn, f, h)

    def att_fwd(xin, i, kv, sh, sc, gt):
        l = i - n_a
        h = norm_mod_fwd(xin, mix_norm_w[i][None], sc, sh, name=f"mix_norm_{i}")
        q = mm(h, g_full("attn_w_q", l), bias=attn_b_q[l][None], name=f"att_q_{i}")
        o = attn_fwd(q, kv, attn_sinks[l][None], name=f"att_{i}")
        f, xo = mm(o, g_full("attn_w_o", l), bias=attn_b_o[l][None], resid=(xin, gt, 1.0), name=f"att_o_{i}")
        return xo, (xin, q, o, f, h)

    saved = [[None, None, None] for _ in range(depth)]
    xc = xs
    kv = x_kv = None
    n_stage = len(fwd_stages)

    def run_stage(s, xc, tok):
        nonlocal kv, x_kv
        i = max(s - 1, 0)
        sh1, sc1, g1, shm, scm, gm, sh2, sc2, g2 = mods[i]
        if s != 1:
            if i == n_a:
                x_kv = xc
                hkv = norm_mod_fwd(xc, kv_norm_w[None], kv_scale, kv_shift + tok, name="kv_norm")
                kv = mm(hkv, g_full("w_kv", 0), bias=b_kv[None], name="kv_proj")
            xc, saved[i][0] = ffn_fwd(xc, i, 0, sh1 + tok, sc1, g1)
            if s != 0:
                cores_end(s, xc)
        if s != 0:
            xc, saved[i][1] = ssm_fwd(xc, i, shm + tok, scm, gm) if i < n_a else att_fwd(xc, i, kv, shm + tok, scm, gm)
            if s == 1:
                cores_end(s, xc)
            xc, saved[i][2] = ffn_fwd(xc, i, 1, sh2, sc2, g2)
        return xc

    dep = cores_begin(0, kvmod)
    for s in range(n_stage):
        tok = chips_begin(s + 1, dep) if s + 1 < n_stage else 0.0
        xc = run_stage(s, xc, tok)
        if s + 1 < n_stage:
            dep = cores_begin(s + 1, xc)

    loss_part, dx, d_final = final_loss(xc, final_norm_w[None], target, name="loss_head")
    loss = lax.psum(loss_part[0, 0], ("x", "y", "c"))

    wg = {}
    sg = {
        "ffn_norm_w": [[None, None] for _ in range(depth)], "mix_norm_w": [None] * depth, "mod": [None] * depth,
        "ssm_conv_w": [None] * n_a, "ssm_conv_b": [None] * n_a, "ssm_dt_bias": [None] * n_a, "ssm_a_log": [None] * n_a,
        "ssm_d": [None] * n_a, "ssm_norm_w": [None] * n_a, "attn_b_q": [None] * n_b, "attn_sinks": [None] * n_b,
        "attn_b_o": [None] * n_b,
    }

    def ffn_bwd(dxo, i, j, sv, sh, sc, gt):
        xin, gu, a, f, h = sv
        df, dgt, _ = gate_bwd(f, dxo, gt, FFN_HALF, name=f"ffn_res_bwd_{i}_{j}")
        dgu = ffn_down_bwd(df, g_dn(i, j), gu, name=f"ffn_down_dx_{i}_{j}")
        wg["ffn_w_down", 2 * i + j] = mm(a, df, mode="tn", out_dtype=BF16, name=f"ffn_down_dw_{i}_{j}")
        nw = full["ffn_norm_w"][i, j][None]
        dh = mm(dgu, g_gu(i, j), mode="nt", reduce_s=True, name=f"ffn_gu_dx_{i}_{j}")
        wg["ffn_w_gu", 2 * i + j] = mm(h, dgu, mode="tn", out_dtype=BF16, name=f"ffn_gu_dw_{i}_{j}")
        dxi, dnw, dsc, dsh = norm_mod_bwd(xin, nw, sc, dh, dxo, name=f"ffn_norm_bwd_{i}_{j}")
        sg["ffn_norm_w"][i][j] = dnw
        return dxi, (dsh, dsc, dgt)

    def ssm_bwd(dxo, i, sv, sh, sc, gt):
        xin, zx, xbc, dt_raw, y, states, yn, f, h = sv
        df, dgt, _ = gate_bwd(f, dxo, gt, 1.0, name=f"mix_res_bwd_{i}")
        dyn = mm(df, g_full("ssm_w_out", i), mode="nt", name=f"ssm_out_dx_{i}")
        wg["ssm_w_out", i] = mm(yn, df, mode="tn", out_dtype=BF16, name=f"ssm_out_dw_{i}")
        dy, dz, dnorm = gnorm_bwd(y, zx, full["ssm_norm_w"][i][None], dyn, name=f"ssm_gnorm_bwd_{i}")
        dxbc, ddt, dbias, dalog, ddsk = ssd_bwd(
            xbc, dt_raw, ssm_dt_bias[i][None], ssm_a_log[i][None], ssm_d[i][None], states, dy, DI, name=f"ssd_bwd_{i}"
        )
        du, dcw, dcb = conv_bwd(zx, full["ssm_conv_w"][i], full["ssm_conv_b"][i][None], dxbc, DI, name=f"ssm_conv_bwd_{i}")
        dzx = jnp.concatenate([dz, du, ddt], axis=1).astype(BF16)
        dzx4 = jnp.transpose(dzx.reshape(L, N_CHIPS, CI), (1, 0, 2))
        nw = mix_norm_w[i][None]
        dh = mm(dzx4, gw["ssm_w_in", i].reshape(N_CHIPS, D, CI), mode="nt", reduce_s=True, name=f"ssm_in_dx_{i}")
        wg["ssm_w_in", i] = mm(h, dzx4, mode="tn", out_dtype=BF16, name=f"ssm_in_dw_{i}")
        dxi, dnw, dsc, dsh = norm_mod_bwd(xin, nw, sc, dh, dxo, name=f"mix_norm_bwd_{i}")
        sg["mix_norm_w"][i] = dnw
        sg["ssm_conv_w"][i], sg["ssm_conv_b"][i], sg["ssm_norm_w"][i] = dcw, dcb, dnorm
        sg["ssm_dt_bias"][i], sg["ssm_a_log"][i], sg["ssm_d"][i] = dbias, dalog, ddsk
        return dxi, (dsh, dsc, dgt)

    def att_bwd(dxo, i, sv, dkv, sh, sc, gt):
        l = i - n_a
        xin, q, o, f, h = sv
        df, dgt, dfsum = gate_bwd(f, dxo, gt, 1.0, name=f"mix_res_bwd_{i}")
        do = mm(df, g_full("attn_w_o", l), mode="nt", name=f"att_o_dx_{i}")
        wg["attn_w_o", l] = mm(o, df, mode="tn", out_dtype=BF16, name=f"att_o_dw_{i}")
        dq, dkv, dsink = attn_bwd(q, kv, attn_sinks[l][None], do, dkv, name=f"att_bwd_{i}")
        nw = mix_norm_w[i][None]
        dh = mm(dq, g_full("attn_w_q", l), mode="nt", name=f"att_q_dx_{i}")
        wg["attn_w_q", l] = mm(h, dq, mode="tn", out_dtype=BF16, name=f"att_q_dw_{i}")
        dxi, dnw, dsc, dsh = norm_mod_bwd(xin, nw, sc, dh, dxo, name=f"mix_norm_bwd_{i}")
        sg["mix_norm_w"][i] = dnw
        sg["attn_b_q"][l], sg["attn_sinks"][l], sg["attn_b_o"][l] = colsum(dq, name=f"att_bq_{i}"), dsink, dfsum
        return dxi, dkv, (dsh, dsc, dgt)

    gfull = {n: lax.empty(_row_halves(W[n]).shape, F32) for n in GATHERED}

    pending, sib = [], []

    def sibling_begin(items):
        parts = []
        for n, m0, a in items:
            m, _, rh, cc = _row_halves(a).shape
            parts.append(wg.pop((n, m0)).reshape(m, N_CHIPS, 2, rh, cc))
        land_shapes = [p.shape[:2] + p.shape[3:] for p in parts]
        tag = f"{items[0][0]}_{items[0][1]}"
        ssem, rsem, srcs, lands, token = split_start(
            _sibling_copies, parts, land_shapes, parts[0], name=f"rs_sibling_start_{tag}", per=1)
        sib.append((tag, [(n, m0) for n, m0, _ in items], ssem, rsem, srcs, lands))
        return token[0:1, 0:1]

    def reduce_step(s, after):
        if pending:
            reduce_end(after)
        items = bwd_stages[s]
        mine, theirs = {}, {}
        while sib:
            tag, keys, ssem, rsem, srcs, lands = sib.pop(0)
            srcs, lands = split_wait(_sibling_copies, ssem, rsem, srcs, lands, after, name=f"rs_sibling_wait_{tag}")
            mine.update(zip(keys, srcs))
            theirs.update(zip(keys, lands))
        parts = [mine[n, m0] for n, m0, _ in items]
        from_sib = [theirs[n, m0] for n, m0, _ in items]
        pair = [rs_add_pair(g, r, name=f"rs_pair_{n}_{m0}") for (n, m0, _), g, r in zip(items, parts, from_sib)]
        land_shapes = [(3, p.shape[0]) + p.shape[2:] for p in pair]
        ssem, rsem, srcs, lands, token = split_start(_reduce_copies, pair, land_shapes, after, name=f"rs_chips_start_{s}")
        pending.append((s, items, parts, from_sib, ssem, rsem, srcs, lands))
        return token[0:1, 0:1]

    def reduce_end(after):
        s, items, parts, from_sib, ssem, rsem, srcs, lands = pending.pop()
        _, from_chips = split_wait(_reduce_copies, ssem, rsem, srcs, lands, after, name=f"rs_chips_wait_{s}")
        for (n, m0, _), g, r, t in zip(items, parts, from_sib, from_chips):
            gfull[n] = rs_add_final(g, r, t, gfull[n], m0, name=f"rs_final_{n}_{m0}")

    dkv = jnp.zeros((L, KVD), F32)
    d_kvnorm = d_kvmod = d_bkv = None
    tok = 0.0
    for i in reversed(range(depth)):
        sh1, sc1, g1, shm, scm, gm, sh2, sc2, g2 = mods[i]
        s1, sm, s2 = saved[i]
        dx, dm2 = ffn_bwd(dx, i, 1, s2, sh2, sc2, g2 + tok)
        tok = sibling_begin(ffn_items(i, 1))
        if i < n_a:
            dx, dmm = ssm_bwd(dx, i, sm, shm, scm, gm + tok)
        else:
            dx, dkv, dmm = att_bwd(dx, i, sm, dkv, shm, scm, gm + tok)
        tok = sibling_begin(mix_items(i))
        if i == 0:
            tok = tok + reduce_step(depth - 1, dx)
        dx, dm1 = ffn_bwd(dx, i, 0, s1, sh1, sc1, g1 + tok)
        tok = sibling_begin(ffn_items(i, 0))
        sg["mod"][i] = jnp.concatenate(list(dm1) + list(dmm) + list(dm2), axis=1)
        if i == n_a:
            d_bkv = colsum(dkv, name="kv_bias_bwd")
            hkv = norm_mod_fwd(x_kv, kv_norm_w[None], kv_scale, kv_shift + tok, name="kv_norm_re")
            dh = mm(dkv, g_full("w_kv", 0), mode="nt", name="kv_proj_dx")
            wg["w_kv", 0] = mm(hkv, dkv, mode="tn", out_dtype=BF16, name="kv_proj_dw")
            dx, d_kvnorm, dsc, dsh = norm_mod_bwd(x_kv, kv_norm_w[None], kv_scale, dh, dx, name="kv_norm_bwd")
            d_kvmod = jnp.concatenate([dsh, dsc], axis=1)
            tok = sibling_begin([("w_kv", 0, w_kv)])
        if i > 0:
            tok = reduce_step(depth - 1 - i, dx)
    grad_x = dx[None]
    grads = {}

    small = {
        "ffn_norm_w": jnp.stack([jnp.stack([r[0] for r in row]) for row in sg["ffn_norm_w"]]),
        "mod_b": jnp.stack([r[0] for r in sg["mod"]]),
        "mix_norm_w": jnp.stack([r[0] for r in sg["mix_norm_w"]]),
        "ssm_conv_w": jnp.stack(sg["ssm_conv_w"]),
        "ssm_conv_b": jnp.stack([r[0] for r in sg["ssm_conv_b"]]),
        "ssm_dt_bias": jnp.stack([r[0] for r in sg["ssm_dt_bias"]]),
        "ssm_a_log": jnp.stack([r[0] for r in sg["ssm_a_log"]]),
        "ssm_d": jnp.stack([r[0] for r in sg["ssm_d"]]),
        "ssm_norm_w": jnp.stack([r[0] for r in sg["ssm_norm_w"]]),
        "kv_norm_w": d_kvnorm[0],
        "kv_mod_b": d_kvmod[0],
        "b_kv": d_bkv[0],
        "attn_b_q": jnp.stack([r[0] for r in sg["attn_b_q"]]),
        "attn_sinks": jnp.stack([r[0] for r in sg["attn_sinks"]]),
        "attn_b_o": jnp.stack([r[0] for r in sg["attn_b_o"]]),
        "final_norm_w": d_final[0],
    }
    small_like = [small[n].shape for n in SMALL]
    sv_pack, sv_n = _pack([small[n] for n in SMALL])
    sv_all = ag8(sv_pack + tok, name="ag_small_g")
    sv_all = sv_all + reduce_step(depth, sv_all)
    sv_sum = reduce8(sv_all, name="small_g_sum").reshape(-1)[:sv_n]
    for n, gsum in zip(SMALL, _unpack(sv_sum, small_like)):
        grads[n] = chip_cols(gsum, W[n].shape[-1]) if n in SMALL_SHARDED else gsum

    per_dev = [_unpack(sv_all[b].reshape(-1)[:sv_n], small_like) for b in range(N_DEV)]
    i_modb, i_kvb = SMALL.index("mod_b"), SMALL.index("kv_mod_b")
    dmod_all = jnp.stack([chip_cols(p[i_modb], MW) for p in per_dev], axis=1)
    dkv_all = jnp.stack([chip_cols(p[i_kvb], KW) for p in per_dev], axis=0)[None]
    c_t = jnp.transpose(c_all)
    grads["mod_w"] = outer8(c_t, dmod_all, name="mod_w_grad")
    grads["kv_mod_w"] = outer8(c_t, dkv_all, name="kv_mod_w_grad")[0]

    delta, new_m, new_v = {}, {}, {}
    for n in COLUMN_PARALLEL:
        delta[n], new_m[n], new_v[n] = adamw(W[n], grads[n], MOM[n], VAR[n], name=f"adamw_{n}")
    like = [W[n].shape for n in SMALL]
    packs = [_pack([d[n] for n in SMALL])[0] for d in (W, grads, MOM, VAR)]
    n_small = sum(int(W[n].size) for n in SMALL)
    for dst, res in zip((delta, new_m, new_v), adamw(*packs, name="adamw_small")):
        for n, a in zip(SMALL, _unpack(res.reshape(-1)[:n_small], like)):
            dst[n] = a

    reduce_end(delta["mod_w"])
    for n, s in zip(GATHERED, rs_share([gfull[n] for n in GATHERED], name="rs_share")):
        grads[n] = s.reshape(W[n].shape)
        delta[n], new_m[n], new_v[n] = adamw(W[n], grads[n], MOM[n], VAR[n], name=f"adamw_{n}")

    return (loss, grad_x, *[grads[n] for n in WEIGHTS], *[delta[n] for n in WEIGHTS], *[new_m[n] for n in WEIGHTS],
            *[new_v[n] for n in WEIGHTS])
```

```python
import functools

import jax
import jax.numpy as jnp
from jax import lax
from jax.experimental import pallas as pl
from jax.experimental.pallas import tpu as pltpu

F32 = jnp.float32
BF16 = jnp.bfloat16
HIGHEST = lax.Precision.HIGHEST
MESH = pl.DeviceIdType.MESH

EPS = 1e-5
N_MOD = 9
FFN_HALF = 0.5
SSM_HEADDIM = 64
SSM_GROUPS = 8
SSM_STATE = 128
CONV_WIDTH = 4
CHUNK = 128
KV_HEADS = 4
HEAD_DIM = 64
WINDOW = 128
N_CHIPS = 4
N_DEV = 8

ADAM_LR = 0.001
ADAM_B1 = 0.9
ADAM_B2 = 0.999
ADAM_EPS = 1e-08
ADAM_WD = 0.01
ADAM_STEP = 10

LANE = 128
MM_TILE = 1024


def _pick(n, pref, align, whole_if_small=False):
    best = 0
    t = align
    while t <= min(n, pref):
        if n % t == 0:
            best = t
        t += align
    if best == 0 or (whole_if_small and best < 256 and n <= 2048):
        return n
    return best


def _sigmoid(x):
    return 1.0 / (1.0 + jnp.exp(-x))


def _silu(x):
    return x * _sigmoid(x)


def _dsilu(x):
    s = _sigmoid(x)
    return s * (1.0 + x * (1.0 - s))


def _params(*sem):
    return pltpu.CompilerParams(dimension_semantics=sem)


def mm(a, b, *, mode="nn", reduce_s=False, out_dtype=F32, bias=None, resid=None, name):
    a_s = a.ndim == 3
    b_s = b.ndim == 3
    S = a.shape[0] if a_s else (b.shape[0] if b_s else 1)
    a2 = a.shape[-2:]
    b2 = b.shape[-2:]
    if mode == "nn":
        (M, K), (K2, N) = a2, b2
    elif mode == "nt":
        (M, K), (N, K2) = a2, b2
    else:
        (K, M), (K2, N) = a2, b2
    assert K == K2, (a.shape, b.shape, mode)
    batch = (a_s or b_s) and not reduce_s
    sb = S if batch else 1
    sr = S if ((a_s or b_s) and reduce_s) else 1
    tm = _pick(M, MM_TILE // 2 if resid is not None else MM_TILE, LANE if mode == "tn" else 16, True)
    tn = _pick(N, MM_TILE, LANE, True)
    tk = _pick(K, 2 * MM_TILE if mode == "tn" else MM_TILE, LANE if mode != "tn" else 16, True)
    nk = K // tk
    grid = (sb, M // tm, N // tn, sr, nk)

    def s_of(isb, isr):
        return isb if batch else isr

    def a_map(isb, i, j, isr, k):
        idx = (k, i) if mode == "tn" else (i, k)
        return ((s_of(isb, isr),) + idx) if a_s else idx

    def b_map(isb, i, j, isr, k):
        idx = (j, k) if mode == "nt" else (k, j)
        return ((s_of(isb, isr),) + idx) if b_s else idx

    def o_map(isb, i, j, isr, k):
        return (isb, i, j) if batch else (i, j)

    def s_blk(has_s, blk):
        return ((None,) + blk) if has_s else blk

    a_blk = (tk, tm) if mode == "tn" else (tm, tk)
    b_blk = (tn, tk) if mode == "nt" else (tk, tn)
    in_specs = [pl.BlockSpec(s_blk(a_s, a_blk), a_map), pl.BlockSpec(s_blk(b_s, b_blk), b_map)]
    args = [a, b]
    if bias is not None:
        bias_s = bias.ndim == 3
        in_specs.append(
            pl.BlockSpec(
                ((None, 1, tn) if bias_s else (1, tn)),
                (lambda isb, i, j, isr, k: (isb, 0, j)) if bias_s else (lambda isb, i, j, isr, k: (0, j)),
            )
        )
        args.append(bias)
    o_spec = pl.BlockSpec(s_blk(batch, (tm, tn)), o_map)
    out_shape = jax.ShapeDtypeStruct(((sb, M, N) if batch else (M, N)), out_dtype)
    out_specs = o_spec
    if resid is not None:
        assert not batch
        x_res, gate, scale = resid
        in_specs += [pl.BlockSpec((tm, tn), o_map), pl.BlockSpec((1, tn), lambda isb, i, j, isr, k: (0, j))]
        args += [x_res, gate]
        out_shape = (out_shape, jax.ShapeDtypeStruct((M, N), F32))
        out_specs = (o_spec, pl.BlockSpec((tm, tn), o_map))
    dims = {"nn": (((1,), (0,)), ((), ())), "nt": (((1,), (1,)), ((), ())), "tn": (((0,), (0,)), ((), ()))}[mode]
    n_in = len(args)
    n_out = 2 if resid is not None else 1
    one_step = sr * nk == 1

    def body(*refs):
        a_ref, b_ref = refs[0], refs[1]
        bias_ref = refs[2] if bias is not None else None
        o_ref = refs[n_in]

        def finish(r):
            if bias is not None:
                r = r + bias_ref[...]
            o_ref[...] = r.astype(o_ref.dtype)
            if resid is not None:
                refs[n_in + 1][...] = refs[n_in - 2][...] + (scale * refs[n_in - 1][...]) * r

        def part():
            return lax.dot_general(a_ref[...].astype(BF16), b_ref[...].astype(BF16), dims, preferred_element_type=F32)

        if one_step:
            finish(part())
            return
        acc = refs[n_in + n_out]
        isr = pl.program_id(3)
        k = pl.program_id(4)

        @pl.when((isr == 0) & (k == 0))
        def _():
            acc[...] = jnp.zeros_like(acc)

        acc[...] += part()

        @pl.when((isr == sr - 1) & (k == nk - 1))
        def _():
            finish(acc[...])

    return pl.pallas_call(
        body,
        out_shape=out_shape,
        grid=grid,
        in_specs=in_specs,
        out_specs=out_specs,
        scratch_shapes=[] if one_step else [pltpu.VMEM((tm, tn), F32)],
        compiler_params=_params("parallel", "parallel", "parallel", "arbitrary", "arbitrary"),
        name=name,
    )(*args)


def norm_mod_fwd(x, nw, sc, sh, *, name):
    L, D = x.shape
    tl = _pick(L, 512, 16)

    def body(x_ref, nw_ref, sc_ref, sh_ref, h_ref):
        xv = x_ref[...]
        r = lax.rsqrt(jnp.mean(xv * xv, axis=-1, keepdims=True) + EPS)
        n = (xv * r) * nw_ref[...]
        h_ref[...] = (n * (1.0 + sc_ref[...]) + sh_ref[...]).astype(h_ref.dtype)

    row = pl.BlockSpec((1, D), lambda i: (0, 0))
    return pl.pallas_call(
        body,
        out_shape=jax.ShapeDtypeStruct((L, D), BF16),
        grid=(L // tl,),
        in_specs=[pl.BlockSpec((tl, D), lambda i: (i, 0)), row, row, row],
        out_specs=pl.BlockSpec((tl, D), lambda i: (i, 0)),
        compiler_params=_params("parallel"),
        name=name,
    )(x, nw, sc, sh)


def mm_norm_bwd(a, b, x, nw, sc, dx_in, *, name):
    has_s = a.ndim == 3
    assert has_s == (b.ndim == 3)
    sr = a.shape[0] if has_s else 1
    M, K = a.shape[-2:]
    D = b.shape[-2]
    tm = _pick(M, MM_TILE // 2, 16, True)
    tk = _pick(K, MM_TILE, LANE, True)
    nk = K // tk

    def body(a_ref, b_ref, x_ref, nw_ref, sc_ref, dxi_ref, dx_ref, dnw_ref, dsc_ref, dsh_ref, acc):
        i, s, k = pl.program_id(0), pl.program_id(1), pl.program_id(2)

        @pl.when((i == 0) & (s == 0) & (k == 0))
        def _():
            dnw_ref[...] = jnp.zeros_like(dnw_ref)
            dsc_ref[...] = jnp.zeros_like(dsc_ref)
            dsh_ref[...] = jnp.zeros_like(dsh_ref)

        @pl.when((s == 0) & (k == 0))
        def _():
            acc[...] = jnp.zeros_like(acc)

        acc[...] += lax.dot_general(
            a_ref[...].astype(BF16), b_ref[...].astype(BF16), (((1,), (1,)), ((), ())), preferred_element_type=F32
        )

        @pl.when((s == sr - 1) & (k == nk - 1))
        def _():
            dh_v = acc[...]
            xv = x_ref[...]
            r = lax.rsqrt(jnp.mean(xv * xv, axis=-1, keepdims=True) + EPS)
            xhat = xv * r
            nw_v = nw_ref[...]
            n = xhat * nw_v
            dsh_ref[...] += jnp.sum(dh_v, axis=0, keepdims=True)
            dsc_ref[...] += jnp.sum(dh_v * n, axis=0, keepdims=True)
            dn = dh_v * (1.0 + sc_ref[...])
            dnw_ref[...] += jnp.sum(dn * xhat, axis=0, keepdims=True)
            dxhat = dn * nw_v
            dx_ref[...] = dxi_ref[...] + r * (dxhat - xhat * jnp.mean(dxhat * xhat, axis=-1, keepdims=True))

    row = pl.BlockSpec((1, D), lambda i, s, k: (0, 0))
    tile = pl.BlockSpec((tm, D), lambda i, s, k: (i, 0))
    vec = jax.ShapeDtypeStruct((1, D), F32)
    return pl.pallas_call(
        body,
        out_shape=(jax.ShapeDtypeStruct((M, D), F32), vec, vec, vec),
        grid=(M // tm, sr, nk),
        in_specs=[
            pl.BlockSpec((None, tm, tk) if has_s else (tm, tk), (lambda i, s, k: (s, i, k)) if has_s else (lambda i, s, k: (i, k))),
            pl.BlockSpec((None, D, tk) if has_s else (D, tk), (lambda i, s, k: (s, 0, k)) if has_s else (lambda i, s, k: (0, k))),
            tile, row, row, tile,
        ],
        out_specs=(tile, row, row, row),
        scratch_shapes=[pltpu.VMEM((tm, D), F32)],
        compiler_params=_params("arbitrary", "arbitrary", "arbitrary"),
        name=name,
    )(a, b, x, nw, sc, dx_in)


def gate_bwd(f, dx, gate, scale, *, name):
    L, D = f.shape
    tl = _pick(L, 512, 16)

    def body(f_ref, dx_ref, g_ref, df_ref, dg_ref, dfsum_ref):
        @pl.when(pl.program_id(0) == 0)
        def _():
            dg_ref[...] = jnp.zeros_like(dg_ref)
            dfsum_ref[...] = jnp.zeros_like(dfsum_ref)

        dxv = dx_ref[...]
        df = (scale * g_ref[...]) * dxv
        df_ref[...] = df.astype(df_ref.dtype)
        dfsum_ref[...] += jnp.sum(df, axis=0, keepdims=True)
        dg_ref[...] += scale * jnp.sum(f_ref[...] * dxv, axis=0, keepdims=True)

    tile = pl.BlockSpec((tl, D), lambda i: (i, 0))
    row = pl.BlockSpec((1, D), lambda i: (0, 0))
    vec = jax.ShapeDtypeStruct((1, D), F32)
    return pl.pallas_call(
        body,
        out_shape=(jax.ShapeDtypeStruct((L, D), BF16), vec, vec),
        grid=(L // tl,),
        in_specs=[tile, tile, row],
        out_specs=(tile, row, row),
        compiler_params=_params("arbitrary"),
        name=name,
    )(f, dx, gate)


def colsum(x, *, name):
    L, N = x.shape
    tl = _pick(L, 512, 8)

    def body(x_ref, o_ref):
        @pl.when(pl.program_id(0) == 0)
        def _():
            o_ref[...] = jnp.zeros_like(o_ref)

        o_ref[...] += jnp.sum(x_ref[...], axis=0, keepdims=True)

    return pl.pallas_call(
        body,
        out_shape=jax.ShapeDtypeStruct((1, N), F32),
        grid=(L // tl,),
        in_specs=[pl.BlockSpec((tl, N), lambda i: (i, 0))],
        out_specs=pl.BlockSpec((1, N), lambda i: (0, 0)),
        compiler_params=_params("arbitrary"),
        name=name,
    )(x)


def ffn_up(h, wgu, *, name):
    L, D = h.shape
    T = wgu.shape[-1]
    tm = _pick(L, 512, 16)

    def body(h_ref, w_ref, gu_ref, a_ref):
        hb = h_ref[...].astype(BF16)
        g = jnp.dot(hb, w_ref[0].astype(BF16), preferred_element_type=F32)
        u = jnp.dot(hb, w_ref[1].astype(BF16), preferred_element_type=F32)
        gu_ref[0] = g
        gu_ref[1] = u
        a_ref[...] = (_silu(g) * u).astype(a_ref.dtype)

    gu, a = pl.pallas_call(
        body,
        out_shape=(jax.ShapeDtypeStruct((2, 2, L, T), F32), jax.ShapeDtypeStruct((2, L, T), BF16)),
        grid=(2, L // tm),
        in_specs=[
            pl.BlockSpec((tm, D), lambda j, i: (i, 0)),
            pl.BlockSpec((2, None, D, T), lambda j, i: (0, j, 0, 0)),
        ],
        out_specs=(
            pl.BlockSpec((2, None, tm, T), lambda j, i: (0, j, i, 0)),
            pl.BlockSpec((None, tm, T), lambda j, i: (j, i, 0)),
        ),
        compiler_params=_params("parallel", "parallel"),
        name=name,
    )(h, wgu.reshape(2, 2, D, T))
    return gu.reshape(4, L, T), a


def ffn_down_bwd(df, wdn, gu, *, name):
    L, D = df.shape
    T = wdn.shape[1]
    tm = _pick(L, 512, 16)

    def body(df_ref, w_ref, gu_ref, d_ref):
        da = lax.dot_general(
            df_ref[...].astype(BF16), w_ref[...].astype(BF16), (((1,), (1,)), ((), ())), preferred_element_type=F32
        )
        g = gu_ref[0]
        d_ref[0] = (da * gu_ref[1] * _dsilu(g)).astype(d_ref.dtype)
        d_ref[1] = (da * _silu(g)).astype(d_ref.dtype)

    out = pl.pallas_call(
        body,
        out_shape=jax.ShapeDtypeStruct((2, 2, L, T), BF16),
        grid=(2, L // tm),
        in_specs=[
            pl.BlockSpec((tm, D), lambda j, i: (i, 0)),
            pl.BlockSpec((None, T, D), lambda j, i: (j, 0, 0)),
            pl.BlockSpec((2, None, tm, T), lambda j, i: (0, j, i, 0)),
        ],
        out_specs=pl.BlockSpec((2, None, tm, T), lambda j, i: (0, j, i, 0)),
        compiler_params=_params("parallel", "parallel"),
        name=name,
    )(df, wdn, gu.reshape(2, 2, L, T))
    return out.reshape(4, L, T)


def _shift_down(u, k, rows):
    if k == 0:
        return u
    return jnp.where(rows >= k, pltpu.roll(u, k, 0), 0.0)


def _shift_up(u, k, rows, n):
    if k == 0:
        return u
    return jnp.where(rows < n - k, pltpu.roll(u, n - k, 0), 0.0)


def _conv_pre(u, w_ref, b_ref, rows):
    pre = b_ref[...] + w_ref[CONV_WIDTH - 1 : CONV_WIDTH, :] * u
    for k in range(1, CONV_WIDTH):
        pre = pre + w_ref[CONV_WIDTH - 1 - k : CONV_WIDTH - k, :] * _shift_down(u, k, rows)
    return pre


def conv_fwd(zx, conv_w, conv_b, d_inner, *, name):
    L = zx.shape[0]
    C = conv_w.shape[1]
    tc = 256
    off = d_inner // tc

    def body(u_ref, w_ref, b_ref, o_ref):
        rows = lax.broadcasted_iota(jnp.int32, (L, tc), 0)
        o_ref[...] = _silu(_conv_pre(u_ref[...], w_ref, b_ref, rows))

    return pl.pallas_call(
        body,
        out_shape=jax.ShapeDtypeStruct((L, C), F32),
        grid=(C // tc,),
        in_specs=[
            pl.BlockSpec((L, tc), lambda j: (0, off + j)),
            pl.BlockSpec((CONV_WIDTH, tc), lambda j: (0, j)),
            pl.BlockSpec((1, tc), lambda j: (0, j)),
        ],
        out_specs=pl.BlockSpec((L, tc), lambda j: (0, j)),
        compiler_params=_params("parallel"),
        name=name,
    )(zx, conv_w, conv_b)


def conv_bwd(zx, conv_w, conv_b, dxbc, d_inner, *, name):
    L = zx.shape[0]
    C = conv_w.shape[1]
    tc = 256
    off = d_inner // tc

    def body(u_ref, w_ref, b_ref, d_ref, du_ref, dw_ref, db_ref):
        rows = lax.broadcasted_iota(jnp.int32, (L, tc), 0)
        u = u_ref[...]
        dpre = d_ref[...] * _dsilu(_conv_pre(u, w_ref, b_ref, rows))
        db_ref[...] = jnp.sum(dpre, axis=0, keepdims=True)
        du = w_ref[CONV_WIDTH - 1 : CONV_WIDTH, :] * dpre
        dw_ref[CONV_WIDTH - 1 : CONV_WIDTH, :] = jnp.sum(dpre * u, axis=0, keepdims=True)
        for k in range(1, CONV_WIDTH):
            j = CONV_WIDTH - 1 - k
            dw_ref[j : j + 1, :] = jnp.sum(dpre * _shift_down(u, k, rows), axis=0, keepdims=True)
            du = du + w_ref[j : j + 1, :] * _shift_up(dpre, k, rows, L)
        du_ref[...] = du

    return pl.pallas_call(
        body,
        out_shape=(
            jax.ShapeDtypeStruct((L, C), F32),
            jax.ShapeDtypeStruct((CONV_WIDTH, C), F32),
            jax.ShapeDtypeStruct((1, C), F32),
        ),
        grid=(C // tc,),
        in_specs=[
            pl.BlockSpec((L, tc), lambda j: (0, off + j)),
            pl.BlockSpec((CONV_WIDTH, tc), lambda j: (0, j)),
            pl.BlockSpec((1, tc), lambda j: (0, j)),
            pl.BlockSpec((L, tc), lambda j: (0, j)),
        ],
        out_specs=(
            pl.BlockSpec((L, tc), lambda j: (0, j)),
            pl.BlockSpec((CONV_WIDTH, tc), lambda j: (0, j)),
            pl.BlockSpec((1, tc), lambda j: (0, j)),
        ),
        compiler_params=_params("parallel"),
        name=name,
    )(zx, conv_w, conv_b, dxbc)


def _ssd_head(xs, dt, acs, tot, dsk, cb, bm, cm, prev):
    q = xs.shape[0]
    li = lax.broadcasted_iota(jnp.int32, (q, q), 0)
    si = lax.broadcasted_iota(jnp.int32, (q, q), 1)
    causal = li >= si
    lmat = jnp.exp(jnp.where(causal, acs - acs.T, -jnp.inf))
    xdt = xs * dt
    y = jnp.dot((cb * lmat).astype(BF16), xdt.astype(BF16), preferred_element_type=F32)
    y = y + lax.dot_general(
        (cm * jnp.exp(acs)).astype(BF16), prev.astype(BF16), (((1,), (1,)), ((), ())), preferred_element_type=F32
    )
    y = y + dsk * xs
    st = lax.dot_general(
        xdt.astype(BF16), (bm * jnp.exp(tot - acs)).astype(BF16), (((0,), (0,)), ((), ())), preferred_element_type=F32
    )
    return y, prev * jnp.exp(tot) + st


def _pick_lane(v, h):
    lanes = lax.broadcasted_iota(jnp.int32, v.shape, 1)
    return jnp.sum(jnp.where(lanes == h, v, 0.0), axis=1, keepdims=True)


def _tri_cols(cols, upper):
    q = cols[0].shape[0]
    assert 3 * len(cols) <= LANE
    li = lax.broadcasted_iota(jnp.int32, (q, q), 0)
    si = lax.broadcasted_iota(jnp.int32, (q, q), 1)
    tri = ((li <= si) if upper else (li >= si)).astype(BF16)
    lanes = lax.broadcasted_iota(jnp.int32, (q, LANE), 1)
    rhs = jnp.zeros((q, LANE), F32)
    for r, col in enumerate(cols):
        hi = col.astype(BF16).astype(F32)
        mid = (col - hi).astype(BF16).astype(F32)
        lo = col - hi - mid
        for t, term in enumerate((hi, mid, lo)):
            rhs = jnp.where(lanes == 3 * r + t, term, rhs)
    out = jnp.dot(tri, rhs.astype(BF16), preferred_element_type=F32)
    return [jnp.sum(jnp.where((lanes >= 3 * r) & (lanes < 3 * r + 3), out, 0.0), axis=1, keepdims=True)
            for r in range(len(cols))]


def _softplus(x):
    return jnp.maximum(x, 0.0) + jnp.log(1.0 + jnp.exp(-jnp.abs(x)))


def _ssd_specs(L, d_inner, H, nc, rev):
    R = H // SSM_GROUPS
    P, N, Q = SSM_HEADDIM, SSM_STATE, CHUNK
    ngrp = SSM_GROUPS

    def ci(c):
        return (nc - 1 - c) if rev else c

    b_off = d_inner // N
    c_off = b_off + ngrp
    xs = pl.BlockSpec((Q, R * P), lambda c, g: (ci(c), g))
    bm = pl.BlockSpec((Q, N), lambda c, g: (ci(c), b_off + g))
    cm = pl.BlockSpec((Q, N), lambda c, g: (ci(c), c_off + g))
    dt = pl.BlockSpec((Q, H), lambda c, g: (ci(c), 0))
    hv = pl.BlockSpec((1, H), lambda c, g: (0, 0))
    y = pl.BlockSpec((Q, R * P), lambda c, g: (ci(c), g))
    st = pl.BlockSpec((None, R * P, N), lambda c, g: (ci(c), g, 0))
    return R, xs, bm, cm, dt, hv, y, st


def ssd_fwd(xbc, dt_raw, dt_bias, a_log, d_skip, d_inner, *, name):
    L = xbc.shape[0]
    H = dt_raw.shape[1]
    nc = L // CHUNK
    P, N = SSM_HEADDIM, SSM_STATE
    R, xs_s, bm_s, cm_s, dt_s, hv_s, y_s, st_s = _ssd_specs(L, d_inner, H, nc, False)

    def body(xs_ref, bm_ref, cm_ref, dt_ref, bias_ref, alog_ref, dsk_ref, y_ref, st_ref, state):
        c = pl.program_id(0)
        g = pl.program_id(1)

        @pl.when(c == 0)
        def _():
            for r in range(R):
                state[g * R + r] = jnp.zeros((P, N), F32)

        dtb = _softplus(dt_ref[...] + bias_ref[...])
        a_all = -jnp.exp(alog_ref[...])
        bm, cm = bm_ref[...], cm_ref[...]
        cb = lax.dot_general(cm.astype(BF16), bm.astype(BF16), (((1,), (1,)), ((), ())), preferred_element_type=F32)
        dts = [_pick_lane(dtb, g * R + r) for r in range(R)]
        a_cols = [dts[r] * _pick_lane(a_all, g * R + r) for r in range(R)]
        acs = _tri_cols(a_cols, upper=False)
        prevs = [state[g * R + r] for r in range(R)]
        res = []
        for r in range(R):
            res.append(_ssd_head(
                xs_ref[:, r * P : (r + 1) * P],
                dts[r],
                jnp.broadcast_to(acs[r], (CHUNK, CHUNK)),
                jnp.sum(a_cols[r], axis=0, keepdims=True),
                _pick_lane(dsk_ref[...], g * R + r),
                cb,
                bm,
                cm,
                prevs[r],
            ))
        for r in range(R):
            st_ref[r * P : (r + 1) * P, :] = prevs[r]
            y_ref[:, r * P : (r + 1) * P] = res[r][0]
            state[g * R + r] = res[r][1]

    return pl.pallas_call(
        body,
        out_shape=(jax.ShapeDtypeStruct((L, d_inner), F32), jax.ShapeDtypeStruct((nc, H * P, N), F32)),
        grid=(nc, SSM_GROUPS),
        in_specs=[xs_s, bm_s, cm_s, dt_s, hv_s, hv_s, hv_s],
        out_specs=(y_s, st_s),
        scratch_shapes=[pltpu.VMEM((H, P, N), F32)],
        compiler_params=_params("arbitrary", "arbitrary"),
        name=name,
    )(xbc, xbc, xbc, dt_raw, dt_bias, a_log, d_skip)


def ssd_bwd(xbc, dt_raw, dt_bias, a_log, d_skip, states, dy, d_inner, *, name):
    L, C = xbc.shape
    H = dt_raw.shape[1]
    nc = L // CHUNK
    P, N, Q = SSM_HEADDIM, SSM_STATE, CHUNK
    R, xs_s, bm_s, cm_s, dt_s, hv_s, y_s, st_s = _ssd_specs(L, d_inner, H, nc, True)

    def body(xs_ref, bm_ref, cm_ref, dt_ref, bias_ref, alog_ref, dsk_ref, st_ref, dy_ref,
             dxs_ref, dbm_ref, dcm_ref, ddt_ref, dbias_ref, dalog_ref, ddsk_ref, dstate):
        c = pl.program_id(0)
        g = pl.program_id(1)

        @pl.when(c == 0)
        def _():
            for r in range(R):
                dstate[g * R + r] = jnp.zeros((P, N), F32)

        @pl.when((c == 0) & (g == 0))
        def _():
            dbias_ref[...] = jnp.zeros_like(dbias_ref)
            dalog_ref[...] = jnp.zeros_like(dalog_ref)
            ddsk_ref[...] = jnp.zeros_like(ddsk_ref)

        @pl.when(g == 0)
        def _():
            ddt_ref[...] = jnp.zeros_like(ddt_ref)

        pre = dt_ref[...] + bias_ref[...]
        dtb = _softplus(pre)
        a_all = -jnp.exp(alog_ref[...])
        lanes_q = lax.broadcasted_iota(jnp.int32, (Q, H), 1)
        lanes_1 = lax.broadcasted_iota(jnp.int32, (1, H), 1)
        bm = bm_ref[...]
        cm = cm_ref[...]
        nt = (((1,), (1,)), ((), ()))
        cb = lax.dot_general(cm.astype(BF16), bm.astype(BF16), nt, preferred_element_type=F32)
        dts = [_pick_lane(dtb, g * R + r) for r in range(R)]
        a_negs = [_pick_lane(a_all, g * R + r) for r in range(R)]
        a_cols = [dts[r] * a_negs[r] for r in range(R)]
        acs = _tri_cols(a_cols, upper=False)
        dbm = jnp.zeros((Q, N), F32)
        dcm = jnp.zeros((Q, N), F32)
        dcb = jnp.zeros((Q, Q), F32)
        dd_row = jnp.zeros((1, H), F32)
        dstates = [dstate[g * R + r] for r in range(R)]
        dprevs, ddts, dacs_cols, dtots = [], [], [], []
        for r in range(R):
            h = g * R + r
            args = (
                xs_ref[:, r * P : (r + 1) * P],
                dts[r],
                jnp.broadcast_to(acs[r], (Q, Q)),
                jnp.sum(a_cols[r], axis=0, keepdims=True),
                _pick_lane(dsk_ref[...], h),
                cb,
                bm,
                cm,
                st_ref[r * P : (r + 1) * P, :],
            )
            _, vjp = jax.vjp(_ssd_head, *args)
            dxs, ddt, dacs, dtot, dd, dcb_h, dbm_h, dcm_h, dprev = vjp((dy_ref[:, r * P : (r + 1) * P], dstates[r]))
            dxs_ref[:, r * P : (r + 1) * P] = dxs
            dprevs.append(dprev)
            ddts.append(ddt)
            dacs_cols.append(jnp.sum(dacs, axis=1, keepdims=True))
            dtots.append(dtot)
            dbm = dbm + dbm_h
            dcm = dcm + dcm_h
            dcb = dcb + dcb_h
            dd_row = dd_row + jnp.where(lanes_1 == h, dd, 0.0)
        for r in range(R):
            dstate[g * R + r] = dprevs[r]
        ddt_blk = jnp.zeros((Q, H), F32)
        da_row = jnp.zeros((1, H), F32)
        for r, da_col in enumerate(_tri_cols(dacs_cols, upper=True)):
            h = g * R + r
            da_col = da_col + dtots[r]
            ddt_blk = ddt_blk + jnp.where(lanes_q == h, ddts[r] + da_col * a_negs[r], 0.0)
            da_row = da_row + jnp.where(lanes_1 == h, jnp.sum(da_col * dts[r], axis=0, keepdims=True), 0.0)
        dcb16 = dcb.astype(BF16)
        dbm_ref[...] = dbm + lax.dot_general(dcb16, cm.astype(BF16), (((0,), (0,)), ((), ())), preferred_element_type=F32)
        dcm_ref[...] = dcm + jnp.dot(dcb16, bm.astype(BF16), preferred_element_type=F32)
        ddt_pre = ddt_blk * _sigmoid(pre)
        ddt_ref[...] += ddt_pre
        dbias_ref[...] += jnp.sum(ddt_pre, axis=0, keepdims=True)
        dalog_ref[...] += da_row * a_all
        ddsk_ref[...] += dd_row

    ngrp = SSM_GROUPS
    hrow = jax.ShapeDtypeStruct((1, H), F32)
    dxs, dbm, dcm, ddt, dbias, dalog, ddsk = pl.pallas_call(
        body,
        out_shape=(
            jax.ShapeDtypeStruct((L, d_inner), F32),
            jax.ShapeDtypeStruct((L, ngrp * N), F32),
            jax.ShapeDtypeStruct((L, ngrp * N), F32),
            jax.ShapeDtypeStruct((L, H), F32),
            hrow,
            hrow,
            hrow,
        ),
        grid=(nc, ngrp),
        in_specs=[xs_s, bm_s, cm_s, dt_s, hv_s, hv_s, hv_s, st_s, y_s],
        out_specs=(
            y_s,
            pl.BlockSpec((Q, N), lambda c, g: (nc - 1 - c, g)),
            pl.BlockSpec((Q, N), lambda c, g: (nc - 1 - c, g)),
            dt_s,
            hv_s,
            hv_s,
            hv_s,
        ),
        scratch_shapes=[pltpu.VMEM((H, P, N), F32)],
        compiler_params=_params("arbitrary", "arbitrary"),
        name=name,
    )(xbc, xbc, xbc, dt_raw, dt_bias, a_log, d_skip, states, dy)
    return jnp.concatenate([dxs, dbm, dcm], axis=1), ddt, dbias, dalog, ddsk


def gnorm_fwd(y, zx, nw, *, name):
    L, DI = y.shape
    gw = DI // SSM_GROUPS
    tl = _pick(L, 512, 16)

    def body(y_ref, z_ref, nw_ref, o_ref):
        yz = y_ref[...] * _silu(z_ref[...])
        r = lax.rsqrt(jnp.mean(yz * yz, axis=-1, keepdims=True) + EPS)
        o_ref[...] = ((yz * r) * nw_ref[...]).astype(o_ref.dtype)

    tile = pl.BlockSpec((tl, gw), lambda i, g: (i, g))
    return pl.pallas_call(
        body,
        out_shape=jax.ShapeDtypeStruct((L, DI), BF16),
        grid=(L // tl, SSM_GROUPS),
        in_specs=[tile, tile, pl.BlockSpec((1, gw), lambda i, g: (0, g))],
        out_specs=tile,
        compiler_params=_params("parallel", "parallel"),
        name=name,
    )(y, zx, nw)


def gnorm_bwd(y, zx, nw, dout, *, name):
    L, DI = y.shape
    gw = DI // SSM_GROUPS
    tl = _pick(L, 512, 16)

    def body(y_ref, z_ref, nw_ref, do_ref, dy_ref, dz_ref, dnw_ref):
        @pl.when(pl.program_id(1) == 0)
        def _():
            dnw_ref[...] = jnp.zeros_like(dnw_ref)

        yv = y_ref[...]
        zv = z_ref[...]
        sz = _silu(zv)
        yz = yv * sz
        r = lax.rsqrt(jnp.mean(yz * yz, axis=-1, keepdims=True) + EPS)
        n = yz * r
        dov = do_ref[...]
        dnw_ref[...] += jnp.sum(dov * n, axis=0, keepdims=True)
        dn = dov * nw_ref[...]
        dyz = r * (dn - n * jnp.mean(dn * n, axis=-1, keepdims=True))
        dy_ref[...] = dyz * sz
        dz_ref[...] = dyz * yv * _dsilu(zv)

    tile = pl.BlockSpec((tl, gw), lambda g, i: (i, g))
    row = pl.BlockSpec((1, gw), lambda g, i: (0, g))
    return pl.pallas_call(
        body,
        out_shape=(
            jax.ShapeDtypeStruct((L, DI), F32),
            jax.ShapeDtypeStruct((L, DI), F32),
            jax.ShapeDtypeStruct((1, DI), F32),
        ),
        grid=(SSM_GROUPS, L // tl),
        in_specs=[tile, tile, row, tile],
        out_specs=(tile, tile, row),
        compiler_params=_params("parallel", "arbitrary"),
        name=name,
    )(y, zx, nw, dout)


def _attn_head(q, kp, kc, vp, vc, sink, has_prev):
    rows, w = q.shape[0], kc.shape[0]
    nt = (((1,), (1,)), ((), ()))
    qb = q.astype(BF16)
    sc = lax.dot_general(qb, kc.astype(BF16), nt, preferred_element_type=F32) * HEAD_DIM ** -0.5
    sp = lax.dot_general(qb, kp.astype(BF16), nt, preferred_element_type=F32) * HEAD_DIM ** -0.5
    ii = jnp.bitwise_and(lax.broadcasted_iota(jnp.int32, (rows, w), 0), w - 1)
    jj = lax.broadcasted_iota(jnp.int32, (rows, w), 1)
    lc = jnp.where(jj <= ii, sc, -jnp.inf)
    lp = jnp.where((jj > ii) & has_prev, sp, -jnp.inf)
    m = jnp.maximum(jnp.maximum(jnp.max(lc, axis=1, keepdims=True), jnp.max(lp, axis=1, keepdims=True)), sink)
    m = lax.stop_gradient(m)
    pc = jnp.exp(lc - m)
    pp = jnp.exp(lp - m)
    denom = jnp.sum(pc, axis=1, keepdims=True) + jnp.sum(pp, axis=1, keepdims=True) + jnp.exp(sink - m)
    o = jnp.dot((pc / denom).astype(BF16), vc.astype(BF16), preferred_element_type=F32)
    return o + jnp.dot((pp / denom).astype(BF16), vp.astype(BF16), preferred_element_type=F32)


def attn_fwd(q, kv, sinks, *, name):
    L, DQ = q.shape
    heads = DQ // HEAD_DIM
    rep = heads // KV_HEADS
    nb = L // WINDOW
    kw = KV_HEADS * HEAD_DIM
    W, HD = WINDOW, HEAD_DIM

    def body(q_ref, kp_ref, kc_ref, vp_ref, vc_ref, s_ref, o_ref):
        has_prev = pl.program_id(0) > 0
        for kh in range(KV_HEADS):
            ks = slice(kh * HD, (kh + 1) * HD)
            hs = [kh * rep + rr for rr in range(rep)]
            o = _attn_head(
                jnp.concatenate([q_ref[:, h * HD : (h + 1) * HD] for h in hs], axis=0),
                kp_ref[:, ks], kc_ref[:, ks], vp_ref[:, ks], vc_ref[:, ks],
                jnp.concatenate([jnp.broadcast_to(s_ref[:, h : h + 1], (W, 1)) for h in hs], axis=0), has_prev,
            )
            for rr, h in enumerate(hs):
                o_ref[:, h * HD : (h + 1) * HD] = o[rr * W : (rr + 1) * W].astype(o_ref.dtype)

    return pl.pallas_call(
        body,
        out_shape=jax.ShapeDtypeStruct((L, DQ), BF16),
        grid=(nb,),
        in_specs=[
            pl.BlockSpec((W, DQ), lambda n: (n, 0)),
            pl.BlockSpec((W, kw), lambda n: (jnp.maximum(n - 1, 0), 0)),
            pl.BlockSpec((W, kw), lambda n: (n, 0)),
            pl.BlockSpec((W, kw), lambda n: (jnp.maximum(n - 1, 0), 1)),
            pl.BlockSpec((W, kw), lambda n: (n, 1)),
            pl.BlockSpec((1, heads), lambda n: (0, 0)),
        ],
        out_specs=pl.BlockSpec((W, DQ), lambda n: (n, 0)),
        compiler_params=_params("parallel"),
        name=name,
    )(q, kv, kv, kv, kv, sinks)


def attn_bwd(q, kv, sinks, do, dkv_in, *, name):
    L, DQ = q.shape
    heads = DQ // HEAD_DIM
    rep = heads // KV_HEADS
    nb = L // WINDOW
    kw = KV_HEADS * HEAD_DIM
    W, HD = WINDOW, HEAD_DIM

    def blk(n):
        return jnp.minimum(n, nb - 1)

    def prev(n):
        return jnp.maximum(blk(n) - 1, 0)

    def outb(n):
        return jnp.maximum(n - 1, 0)

    def body(q_ref, kp_ref, kc_ref, vp_ref, vc_ref, s_ref, do_ref, dki_ref, dvi_ref,
             dq_ref, dk_ref, dv_ref, ds_ref, dk_cur, dv_cur):
        n = pl.program_id(0)
        has_prev = n > 0

        @pl.when(n == 0)
        def _():
            ds_ref[...] = jnp.zeros_like(ds_ref)
            dk_cur[...] = jnp.zeros_like(dk_cur)
            dv_cur[...] = jnp.zeros_like(dv_cur)

        @pl.when(n == nb)
        def _():
            dk_ref[...] = dki_ref[...] + dk_cur[...]
            dv_ref[...] = dvi_ref[...] + dv_cur[...]

        @pl.when(n < nb)
        def _():
            lanes = lax.broadcasted_iota(jnp.int32, (1, heads), 1)
            ds_row = jnp.zeros((1, heads), F32)
            for kh in range(KV_HEADS):
                ks = slice(kh * HD, (kh + 1) * HD)
                hs = [kh * rep + rr for rr in range(rep)]
                _, vjp = jax.vjp(
                    functools.partial(_attn_head, has_prev=has_prev),
                    jnp.concatenate([q_ref[:, h * HD : (h + 1) * HD] for h in hs], axis=0),
                    kp_ref[:, ks], kc_ref[:, ks], vp_ref[:, ks], vc_ref[:, ks],
                    jnp.concatenate([jnp.broadcast_to(s_ref[:, h : h + 1], (W, 1)) for h in hs], axis=0),
                )
                dq, dkp, dkc, dvp, dvc, dsk = vjp(
                    jnp.concatenate([do_ref[:, h * HD : (h + 1) * HD] for h in hs], axis=0))
                for rr, h in enumerate(hs):
                    dq_ref[:, h * HD : (h + 1) * HD] = dq[rr * W : (rr + 1) * W]
                    ds_row = ds_row + jnp.where(lanes == h, jnp.sum(dsk[rr * W : (rr + 1) * W], axis=0, keepdims=True), 0.0)
                dk_ref[:, ks] = dki_ref[:, ks] + dk_cur[:, ks] + dkp
                dv_ref[:, ks] = dvi_ref[:, ks] + dv_cur[:, ks] + dvp
                dk_cur[:, ks] = dkc
                dv_cur[:, ks] = dvc
            ds_ref[...] += ds_row

    dq, dk, dv, ds = pl.pallas_call(
        body,
        out_shape=(
            jax.ShapeDtypeStruct((L, DQ), F32),
            jax.ShapeDtypeStruct((L, kw), F32),
            jax.ShapeDtypeStruct((L, kw), F32),
            jax.ShapeDtypeStruct((1, heads), F32),
        ),
        grid=(nb + 1,),
        in_specs=[
            pl.BlockSpec((W, DQ), lambda n: (blk(n), 0)),
            pl.BlockSpec((W, kw), lambda n: (prev(n), 0)),
            pl.BlockSpec((W, kw), lambda n: (blk(n), 0)),
            pl.BlockSpec((W, kw), lambda n: (prev(n), 1)),
            pl.BlockSpec((W, kw), lambda n: (blk(n), 1)),
            pl.BlockSpec((1, heads), lambda n: (0, 0)),
            pl.BlockSpec((W, DQ), lambda n: (blk(n), 0)),
            pl.BlockSpec((W, kw), lambda n: (outb(n), 0)),
            pl.BlockSpec((W, kw), lambda n: (outb(n), 1)),
        ],
        out_specs=(
            pl.BlockSpec((W, DQ), lambda n: (blk(n), 0)),
            pl.BlockSpec((W, kw), lambda n: (outb(n), 0)),
            pl.BlockSpec((W, kw), lambda n: (outb(n), 0)),
            pl.BlockSpec((1, heads), lambda n: (0, 0)),
        ),
        scratch_shapes=[pltpu.VMEM((W, kw), F32), pltpu.VMEM((W, kw), F32)],
        compiler_params=_params("arbitrary"),
        name=name,
    )(q, kv, kv, kv, kv, sinks, do, dkv_in, dkv_in)
    return dq, jnp.concatenate([dk, dv], axis=1), ds


def final_loss(x, fw, target, *, name):
    L, D = x.shape
    tl = _pick(L, 512, 8)

    def body(x_ref, fw_ref, t_ref, loss_ref, dx_ref, dfw_ref):
        @pl.when(pl.program_id(0) == 0)
        def _():
            loss_ref[...] = jnp.zeros_like(loss_ref)
            dfw_ref[...] = jnp.zeros_like(dfw_ref)

        xv = x_ref[...]
        fwv = fw_ref[...]
        r = lax.rsqrt(jnp.mean(xv * xv, axis=-1, keepdims=True) + EPS)
        xhat = xv * r
        err = xhat * fwv - t_ref[...]
        loss_ref[...] += 0.5 * jnp.sum(jnp.mean(err * err, axis=-1, keepdims=True), axis=0, keepdims=True)
        dy = err * (1.0 / D)
        dfw_ref[...] += jnp.sum(dy * xhat, axis=0, keepdims=True)
        dxhat = dy * fwv
        dx_ref[...] = r * (dxhat - xhat * jnp.mean(dxhat * xhat, axis=-1, keepdims=True))

    tile = pl.BlockSpec((tl, D), lambda i: (i, 0))
    row = pl.BlockSpec((1, D), lambda i: (0, 0))
    return pl.pallas_call(
        body,
        out_shape=(
            jax.ShapeDtypeStruct((1, 1), F32),
            jax.ShapeDtypeStruct((L, D), F32),
            jax.ShapeDtypeStruct((1, D), F32),
        ),
        grid=(L // tl,),
        in_specs=[tile, row, tile],
        out_specs=(pl.BlockSpec((1, 1), lambda i: (0, 0)), tile, row),
        compiler_params=_params("arbitrary"),
        name=name,
    )(x, fw, target)


def outer8(ct, d, *, name):
    D, B = ct.shape
    S, _, N = d.shape
    tm = _pick(D, 512, 8)
    tn = _pick(N, 256, LANE)

    def body(c_ref, d_ref, o_ref):
        acc = c_ref[:, 0:1] * d_ref[0:1, :]
        for b in range(1, B):
            acc = acc + c_ref[:, b : b + 1] * d_ref[b : b + 1, :]
        o_ref[...] = acc

    return pl.pallas_call(
        body,
        out_shape=jax.ShapeDtypeStruct((S, D, N), F32),
        grid=(S, D // tm, N // tn),
        in_specs=[
            pl.BlockSpec((tm, B), lambda s, i, j: (i, 0)),
            pl.BlockSpec((None, B, tn), lambda s, i, j: (s, 0, j)),
        ],
        out_specs=pl.BlockSpec((None, tm, tn), lambda s, i, j: (s, i, j)),
        compiler_params=_params("parallel", "parallel", "parallel"),
        name=name,
    )(ct, d)


def reduce8(g, *, name):
    nd, R, N = g.shape

    def body(g_ref, o_ref):
        acc = g_ref[0]
        for b in range(1, nd):
            acc = acc + g_ref[b]
        o_ref[...] = acc

    return pl.pallas_call(
        body,
        out_shape=jax.ShapeDtypeStruct((R, N), F32),
        name=name,
    )(g)


def _as3(a):
    if a.ndim == 1:
        return a.reshape(1, 1, -1)
    if a.ndim == 2:
        return a.reshape((1,) + a.shape)
    return a.reshape((-1,) + a.shape[-2:])


def adamw(w, g, m, v, *, name):
    shape = w.shape
    w3, g3, m3, v3 = _as3(w), _as3(g), _as3(m), _as3(v)
    B, R, C = w3.shape
    tr = _pick(R, max(8, (1 << 19) // max(C, 1) // 8 * 8), 8)

    def body(w_ref, g_ref, m_ref, v_ref, d_ref, nm_ref, nv_ref):
        gv = g_ref[...]
        mn = ADAM_B1 * m_ref[...] + (1.0 - ADAM_B1) * gv
        vn = ADAM_B2 * v_ref[...] + (1.0 - ADAM_B2) * (gv * gv)
        m_hat = mn / (1.0 - ADAM_B1 ** ADAM_STEP)
        v_hat = vn / (1.0 - ADAM_B2 ** ADAM_STEP)
        d_ref[...] = -ADAM_LR * (m_hat / (jnp.sqrt(v_hat) + ADAM_EPS) + ADAM_WD * w_ref[...])
        nm_ref[...] = mn
        nv_ref[...] = vn

    tile = pl.BlockSpec((None, tr, C), lambda b, i: (b, i, 0))
    sds = jax.ShapeDtypeStruct((B, R, C), F32)
    d, nm, nv = pl.pallas_call(
        body,
        out_shape=(sds, sds, sds),
        grid=(B, R // tr),
        in_specs=[tile, tile, tile, tile],
        out_specs=(tile, tile, tile),
        compiler_params=_params("parallel", "parallel"),
        name=name,
    )(w3, g3, m3, v3)
    return d.reshape(shape), nm.reshape(shape), nv.reshape(shape)


def _place():
    return lax.axis_index("x"), lax.axis_index("y"), lax.axis_index("c")


def _flip(v, bit):
    return (1 - v) if bit else v


def ag8(v, *, act=None, after=None, name):
    R, N = v.shape
    extra = [] if after is None else [after]

    def body(*refs):
        v_ref = refs[0]
        out_ref, stage, send_sems, recv_sems = refs[1 + len(extra):]
        x, y, c = _place()
        me = 4 * x + 2 * y + c
        val = v_ref[...]
        if act is not None:
            val = act(val)
        stage[...] = val
        out_ref[me] = val
        sends = []
        for k in range(1, N_DEV):
            px, py, pc = _flip(x, k & 4), _flip(y, k & 2), _flip(c, k & 1)
            cp = pltpu.make_async_remote_copy(
                src_ref=stage, dst_ref=out_ref.at[me], send_sem=send_sems.at[k - 1], recv_sem=recv_sems.at[k - 1],
                device_id=(px, py, pc), device_id_type=MESH,
            )
            cp.start()
            sends.append(cp)
        for k in range(1, N_DEV):
            px, py, pc = _flip(x, k & 4), _flip(y, k & 2), _flip(c, k & 1)
            pltpu.make_async_remote_copy(
                src_ref=stage, dst_ref=out_ref.at[4 * px + 2 * py + pc], send_sem=send_sems.at[k - 1],
                recv_sem=recv_sems.at[k - 1], device_id=(px, py, pc), device_id_type=MESH,
            ).wait_recv()
        for cp in sends:
            cp.wait_send()

    return pl.pallas_call(
        body,
        out_shape=jax.ShapeDtypeStruct((N_DEV, R, N), F32),
        in_specs=[pl.BlockSpec(memory_space=pltpu.VMEM)] + [_ANY] * len(extra),
        out_specs=pl.BlockSpec(memory_space=pltpu.VMEM),
        scratch_shapes=[
            pltpu.VMEM((R, N), F32),
            pltpu.SemaphoreType.DMA((N_DEV - 1,)),
            pltpu.SemaphoreType.DMA((N_DEV - 1,)),
        ],
        name=name,
    )(v, *extra)


def _other_chips(x, y):
    chips = [(1 - x, y), (x, 1 - y), (1 - x, 1 - y)]
    return chips, [2 * px + py for px, py in chips]


_HBM = pl.BlockSpec(memory_space=pltpu.HBM)


_SEM = pl.BlockSpec(memory_space=pltpu.SEMAPHORE)
_ANY = pl.BlockSpec(memory_space=pl.ANY)
_EFFECT = pltpu.SideEffectType.DATAFLOW_SIDE_EFFECTING


def _gather_copies(srcs, lands, send_sems, recv_sems):
    x, y, c = _place()
    k_me = 2 * x + y
    chips, kidx = _other_chips(x, y)
    cps = []
    for w in range(len(srcs)):
        for j, (px, py) in enumerate(chips):
            def copy(dst, w=w, j=j, px=px, py=py):
                return pltpu.make_async_remote_copy(
                    src_ref=srcs[w].at[:, c], dst_ref=dst, send_sem=send_sems.at[3 * w + j],
                    recv_sem=recv_sems.at[3 * w + j], device_id=(px, py, c), device_id_type=MESH,
                )
            cps.append((copy(lands[w].at[:, k_me, c]), copy(lands[w].at[:, kidx[j], c])))
    return cps


def _fill_copies(srcs, lands, send_sems, recv_sems):
    x, y, c = _place()
    k_me = 2 * x + y
    _, kidx = _other_chips(x, y)
    sib = (x, y, 1 - c)
    cps = []
    for w in range(len(srcs)):
        own = pltpu.make_async_remote_copy(
            src_ref=srcs[w], dst_ref=lands[w].at[:, k_me], send_sem=send_sems.at[4 * w + 3], recv_sem=recv_sems.at[4 * w + 3],
            device_id=sib, device_id_type=MESH,
        )
        cps.append((own, own))
        for j in range(3):
            def copy(half, w=w, j=j):
                part = lands[w].at[:, kidx[j], half]
                return pltpu.make_async_remote_copy(
                    src_ref=part, dst_ref=part, send_sem=send_sems.at[4 * w + j], recv_sem=recv_sems.at[4 * w + j],
                    device_id=sib, device_id_type=MESH,
                )
            cps.append((copy(c), copy(1 - c)))
    return cps


def _sibling_copies(srcs, lands, send_sems, recv_sems):
    x, y, c = _place()
    cps = []
    for w in range(len(srcs)):
        cp = pltpu.make_async_remote_copy(
            src_ref=srcs[w].at[:, :, 1 - c], dst_ref=lands[w], send_sem=send_sems.at[w], recv_sem=recv_sems.at[w],
            device_id=(x, y, 1 - c), device_id_type=MESH,
        )
        cps.append((cp, cp))
    return cps


def _reduce_copies(srcs, lands, send_sems, recv_sems):
    x, y, c = _place()
    chips, kidx = _other_chips(x, y)
    cps = []
    for w in range(len(srcs)):
        for j, (px, py) in enumerate(chips):
            cp = pltpu.make_async_remote_copy(
                src_ref=srcs[w].at[:, kidx[j]], dst_ref=lands[w].at[j], send_sem=send_sems.at[3 * w + j],
                recv_sem=recv_sems.at[3 * w + j], device_id=(px, py, c), device_id_type=MESH,
            )
            cps.append((cp, cp))
    return cps


def split_start(copies, srcs, land_shapes, after, *, name, lands=None, per=3):
    n = len(srcs)

    def body(*refs):
        src_refs, land_refs = refs[:n], refs[n : 2 * n]
        send_sems, recv_sems = refs[2 * n + 1], refs[2 * n + 2]
        token = refs[-1]
        for cp, _ in copies(src_refs, land_refs, send_sems, recv_sems):
            cp.start()
        token[...] = jnp.zeros_like(token)

    if lands is None:
        lands = [lax.empty(sh, s.dtype) for sh, s in zip(land_shapes, srcs)]
    land_shapes = [a.shape for a in lands]
    lands = [pltpu.with_memory_space_constraint(a, pltpu.HBM) for a in lands]
    srcs = [pltpu.with_memory_space_constraint(s, pltpu.HBM) for s in srcs]
    out = pl.pallas_call(
        body,
        out_shape=(
            pltpu.SemaphoreType.DMA((per * n,)), pltpu.SemaphoreType.DMA((per * n,)),
            *[pltpu.HBM(s.shape, s.dtype) for s in srcs],
            *[pltpu.HBM(sh, s.dtype) for sh, s in zip(land_shapes, srcs)],
            jax.ShapeDtypeStruct((8, LANE), F32),
        ),
        in_specs=[_HBM] * (2 * n) + [_ANY],
        out_specs=(_SEM, _SEM, *([_HBM] * (2 * n)), pl.BlockSpec(memory_space=pltpu.VMEM)),
        input_output_aliases={i: 2 + i for i in range(2 * n)},
        compiler_params=pltpu.CompilerParams(has_side_effects=_EFFECT),
        name=name,
    )(*srcs, *lands, after)
    return out[0], out[1], list(out[2 : 2 + n]), list(out[2 + n : 2 + 2 * n]), out[-1]


def split_wait(copies, send_sems, recv_sems, srcs, lands, after, *, name):
    n = len(srcs)

    def body(*refs):
        src_refs, land_refs = refs[:n], refs[n : 2 * n]
        send_ref, recv_ref = refs[2 * n], refs[2 * n + 1]
        for sent, arrives in copies(src_refs, land_refs, send_ref, recv_ref):
            sent.wait_send()
            arrives.wait_recv()

    out = pl.pallas_call(
        body,
        out_shape=tuple(pltpu.HBM(a.shape, a.dtype) for a in list(srcs) + list(lands)),
        in_specs=[_HBM] * (2 * n) + [_SEM, _SEM, _ANY],
        out_specs=tuple([_HBM] * (2 * n)),
        input_output_aliases={i: i for i in range(2 * n)},
        compiler_params=pltpu.CompilerParams(has_side_effects=_EFFECT),
        name=name,
    )(*srcs, *lands, send_sems, recv_sems, after)
    return list(out[:n]), list(out[n:])


def rs_share(halves, *, name):
    n = len(halves)

    def body(*refs):
        outs = refs[n : 2 * n]
        send_sems, recv_sems = refs[2 * n :]
        x, y, c = _place()
        cps = []
        for w in range(n):
            cp = pltpu.make_async_remote_copy(
                src_ref=outs[w].at[:, c], dst_ref=outs[w].at[:, c], send_sem=send_sems.at[w], recv_sem=recv_sems.at[w],
                device_id=(x, y, 1 - c), device_id_type=MESH,
            )
            cp.start()
            cps.append(cp)
        for w, cp in enumerate(cps):
            cp.wait_send()
            pltpu.make_async_remote_copy(
                src_ref=outs[w].at[:, c], dst_ref=outs[w].at[:, 1 - c], send_sem=send_sems.at[w], recv_sem=recv_sems.at[w],
                device_id=(x, y, 1 - c), device_id_type=MESH,
            ).wait_recv()

    return pl.pallas_call(
        body,
        out_shape=tuple(jax.ShapeDtypeStruct(h.shape, h.dtype) for h in halves),
        in_specs=[_HBM] * n,
        out_specs=tuple([_HBM] * n),
        scratch_shapes=[pltpu.SemaphoreType.DMA((n,)), pltpu.SemaphoreType.DMA((n,))],
        input_output_aliases={w: w for w in range(n)},
        name=name,
    )(*halves)


def _row_tile(R, C):
    return _pick(R, max(16, (1 << 19) // C // 16 * 16), 16)


def _my_core():
    return lax.axis_index("c")


def _my_chip():
    return 2 * lax.axis_index("x") + lax.axis_index("y")


def rs_add_pair(g, r, *, name):
    M, K, _, R, C = g.shape
    tr = _row_tile(R, C)

    def body(g_ref, r_ref, o_ref):
        o_ref[...] = (g_ref[...].astype(F32) + r_ref[...].astype(F32)).astype(o_ref.dtype)

    blk = pl.BlockSpec((None, None, tr, C), lambda m, k, i: (m, k, i, 0))
    return pl.pallas_call(
        body,
        out_shape=jax.ShapeDtypeStruct((M, K, R, C), BF16),
        grid=(M, K, R // tr),
        in_specs=[pl.BlockSpec((None, None, None, tr, C), lambda m, k, i: (m, k, _my_core(), i, 0)), blk],
        out_specs=blk,
        compiler_params=_params("parallel", "parallel", "parallel"),
        name=name,
    )(g, r)


def rs_add_final(g, r, t, full, m0, *, name):
    M, K, _, R, C = g.shape
    tr = _row_tile(R, C)

    def body(g_ref, r_ref, t_ref, full_ref, o_ref):
        acc = g_ref[...].astype(F32) + r_ref[...].astype(F32)
        for j in range(3):
            acc = acc + t_ref[j].astype(F32)
        o_ref[...] = acc

    return pl.pallas_call(
        body,
        out_shape=jax.ShapeDtypeStruct(full.shape, full.dtype),
        grid=(M, R // tr),
        in_specs=[
            pl.BlockSpec((None, None, None, tr, C), lambda m, i: (m, _my_chip(), _my_core(), i, 0)),
            pl.BlockSpec((None, None, tr, C), lambda m, i: (m, _my_chip(), i, 0)),
            pl.BlockSpec((3, None, tr, C), lambda m, i: (0, m, i, 0)),
            _ANY,
        ],
        out_specs=pl.BlockSpec((None, None, tr, C), lambda m, i: (m0 + m, _my_core(), i, 0)),
        input_output_aliases={3: 0},
        compiler_params=_params("parallel", "parallel"),
        name=name,
    )(g, r, t, full)


WEIGHTS = ["ffn_norm_w", "ffn_w_gu", "ffn_w_down", "mod_w", "mod_b", "mix_norm_w", "ssm_w_in", "ssm_conv_w", "ssm_conv_b",
           "ssm_dt_bias", "ssm_a_log", "ssm_d", "ssm_norm_w", "ssm_w_out", "kv_norm_w", "kv_mod_w", "kv_mod_b", "w_kv", "b_kv",
           "attn_w_q", "attn_b_q", "attn_sinks", "attn_w_o", "attn_b_o", "final_norm_w"]
GATHERED = ["ffn_w_gu", "ffn_w_down", "ssm_w_in", "ssm_w_out", "w_kv", "attn_w_q", "attn_w_o"]
COLUMN_PARALLEL = ["mod_w", "kv_mod_w"]
SMALL_SHARDED = ["ffn_norm_w", "ssm_conv_w", "ssm_conv_b", "ssm_norm_w"]
SMALL = [n for n in WEIGHTS if n not in GATHERED and n not in COLUMN_PARALLEL]


def _row_halves(a):
    a = a.reshape((-1,) + a.shape[-2:])
    return a.reshape(a.shape[0], 2, a.shape[1] // 2, a.shape[2])


def _pack(arrs, rows=8):
    flat = jnp.concatenate([a.reshape(-1) for a in arrs])
    n = flat.shape[0]
    pad = (-n) % (rows * LANE)
    return jnp.pad(flat, (0, pad)).reshape(rows, -1), n


def _unpack(flat, like):
    out, o = [], 0
    for s in like:
        k = 1
        for d in s:
            k *= d
        out.append(flat[o : o + k].reshape(s))
        o += k
    return out


def kernel(x, c, ffn_norm_w, ffn_w_gu, ffn_w_down, mod_w, mod_b, mix_norm_w, ssm_w_in, ssm_conv_w, ssm_conv_b, ssm_dt_bias, ssm_a_log, ssm_d, ssm_norm_w, ssm_w_out, kv_norm_w, kv_mod_w, kv_mod_b, w_kv, b_kv, attn_w_q, attn_b_q, attn_sinks, attn_w_o, attn_b_o, final_norm_w, loss_target, m_ffn_norm_w, m_ffn_w_gu, m_ffn_w_down, m_mod_w, m_mod_b, m_mix_norm_w, m_ssm_w_in, m_ssm_conv_w, m_ssm_conv_b, m_ssm_dt_bias, m_ssm_a_log, m_ssm_d, m_ssm_norm_w, m_ssm_w_out, m_kv_norm_w, m_kv_mod_w, m_kv_mod_b, m_w_kv, m_b_kv, m_attn_w_q, m_attn_b_q, m_attn_sinks, m_attn_w_o, m_attn_b_o, m_final_norm_w, v_ffn_norm_w, v_ffn_w_gu, v_ffn_w_down, v_mod_w, v_mod_b, v_mix_norm_w, v_ssm_w_in, v_ssm_conv_w, v_ssm_conv_b, v_ssm_dt_bias, v_ssm_a_log, v_ssm_d, v_ssm_norm_w, v_ssm_w_out, v_kv_norm_w, v_kv_mod_w, v_kv_mod_b, v_w_kv, v_b_kv, v_attn_w_q, v_attn_b_q, v_attn_sinks, v_attn_w_o, v_attn_b_o, v_final_norm_w):
    env = dict(locals())
    W = {n: env[n] for n in WEIGHTS}
    MOM = {n: env["m_" + n] for n in WEIGHTS}
    VAR = {n: env["v_" + n] for n in WEIGHTS}

    ax, ay, ac = _place()
    kme = 2 * ax + ay
    me = 4 * ax + 2 * ay + ac

    xs = x[0]
    target = loss_target[0]
    L, D = xs.shape
    depth, n_a = ffn_w_gu.shape[0], ssm_w_in.shape[0]
    n_b = depth - n_a
    T = ffn_w_gu.shape[-1]
    DI = ssm_w_out.shape[1] * N_CHIPS
    CI = ssm_w_in.shape[2]
    CC = ssm_conv_w.shape[2] * N_CHIPS
    MW = mod_w.shape[2]
    KW = kv_mod_w.shape[1]
    KVD = w_kv.shape[1]

    def chip_cols(a, width):
        return lax.dynamic_slice_in_dim(a, kme * width, width, axis=a.ndim - 1)

    def ffn_items(i, j):
        return [("ffn_w_gu", 2 * i + j, ffn_w_gu[i, j]), ("ffn_w_down", 2 * i + j, ffn_w_down[i, j])]

    def mix_items(i):
        if i < n_a:
            return [("ssm_w_in", i, ssm_w_in[i]), ("ssm_w_out", i, ssm_w_out[i])]
        return [("attn_w_q", i - n_a, attn_w_q[i - n_a]), ("attn_w_o", i - n_a, attn_w_o[i - n_a])]

    def layer_items(i, order):
        kv_items = [("w_kv", 0, w_kv)] if i == n_a else []
        if order == "fwd":
            return kv_items + ffn_items(i, 0) + mix_items(i) + ffn_items(i, 1)
        return ffn_items(i, 1) + mix_items(i) + ffn_items(i, 0) + kv_items

    fwd_stages = [ffn_items(0, 0), mix_items(0) + ffn_items(0, 1)] + [layer_items(i, "fwd") for i in range(1, depth)]
    bwd_stages = [layer_items(i, "bwd") for i in range(depth - 1, 0, -1)] + [ffn_items(0, 1) + mix_items(0), ffn_items(0, 0)]

    gw, inflight = {}, {}

    def chips_begin(s, after):
        keys = [(n, m0) for n, m0, _ in fwd_stages[s]]
        shards = [_row_halves(a.astype(BF16)) for _, _, a in fwd_stages[s]]
        land_shapes = [(sh.shape[0], N_CHIPS) + sh.shape[1:] for sh in shards]
        ssem, rsem, srcs, lands, token = split_start(_gather_copies, shards, land_shapes, after, name=f"gather_start_{s}")
        inflight[s] = (keys, ssem, rsem, srcs, lands)
        return token[0:1, 0:1]

    def n_first(s):
        return 2 if s < 2 else len(fwd_stages[s]) - 4

    def cores_begin(s, after):
        keys, ssem, rsem, srcs, lands = inflight.pop(s)
        srcs, lands = split_wait(_gather_copies, ssem, rsem, srcs, lands, after, name=f"gather_wait_{s}")
        k = n_first(s)
        a_sem, b_sem, a_src, a_land, tok = split_start(
            _fill_copies, srcs[:k], None, lands[0], name=f"fill_start_{s}a", lands=lands[:k], per=4)
        if k < len(keys):
            inflight[s] = (keys[k:],) + split_start(
                _fill_copies, srcs[k:], None, tok, name=f"fill_start_{s}b", lands=lands[k:], per=4)
            tok = inflight[s][-1]
        gw.update(zip(keys[:k], split_wait(_fill_copies, a_sem, b_sem, a_src, a_land, tok, name=f"fill_wait_{s}a")[1]))
        return tok[0:1, 0:1]

    def cores_end(s, after):
        if s in inflight:
            keys, ssem, rsem, srcs, lands, _ = inflight.pop(s)
            gw.update(zip(keys, split_wait(_fill_copies, ssem, rsem, srcs, lands, after, name=f"fill_wait_{s}b")[1]))

    def g_gu(i, j):
        return gw["ffn_w_gu", 2 * i + j].reshape(N_CHIPS, D, T)

    def g_dn(i, j):
        return gw["ffn_w_down", 2 * i + j].reshape(2, T, D)

    def g_full(n, m0):
        a = gw[n, m0]
        return a.reshape(N_CHIPS * 2 * a.shape[-2], a.shape[-1])

    sm_like = [W[n].shape for n in SMALL_SHARDED]
    sm_pack, sm_n = _pack([W[n] for n in SMALL_SHARDED])
    sm_gathered = ag8(sm_pack, name="ag_small_w")
    sm_all = sm_gathered[0::2].reshape(N_CHIPS, -1)[:, :sm_n]
    full = {}
    for n, part in zip(SMALL_SHARDED, zip(*[_unpack(sm_all[k], sm_like) for k in range(N_CHIPS)])):
        full[n] = jnp.concatenate(part, axis=-1)

    c_all = ag8(c, act=_silu, after=sm_gathered, name="ag_c").reshape(N_DEV, D)
    c_all = c_all + chips_begin(0, c_all)
    p_mod = mm(c_all, mod_w, bias=chip_cols(mod_b, MW)[:, None, :], name="mod_mm")
    p_kv = mm(c_all, kv_mod_w, bias=chip_cols(kv_mod_b, KW)[None, :], name="kvmod_mm")
    p_all = jnp.concatenate([jnp.transpose(p_mod, (1, 0, 2)).reshape(N_DEV, depth * MW), p_kv], axis=1)
    p_mine = lax.dynamic_index_in_dim(ag8(p_all, name="ag_mod")[0::2], me, axis=1, keepdims=False)
    mod = jnp.transpose(p_mine[:, : depth * MW].reshape(N_CHIPS, depth, MW), (1, 0, 2)).reshape(depth, N_MOD * D)
    kvmod = p_mine[:, depth * MW :].reshape(1, 2 * D)
    mods = [[mod[i : i + 1, j * D : (j + 1) * D] for j in range(N_MOD)] for i in range(depth)]
    kv_shift, kv_scale = kvmod[:, :D], kvmod[:, D:]

    def ffn_fwd(xin, i, j, sh, sc, gt):
        h = norm_mod_fwd(xin, full["ffn_norm_w"][i, j][None], sc, sh, name=f"ffn_norm_{i}_{j}")
        gu, a = ffn_up(h, g_gu(i, j), name=f"ffn_gu_{i}_{j}")
        f, xo = mm(a, g_dn(i, j), reduce_s=True, resid=(xin, gt, FFN_HALF), name=f"ffn_down_{i}_{j}")
        return xo, (xin, gu, a, f, h)

    def ssm_fwd(xin, i, sh, sc, gt):
        h = norm_mod_fwd(xin, mix_norm_w[i][None], sc, sh, name=f"mix_norm_{i}")
        zx4 = mm(h, gw["ssm_w_in", i].reshape(N_CHIPS, D, CI), name=f"ssm_in_{i}")
        zx = jnp.transpose(zx4, (1, 0, 2)).reshape(L, N_CHIPS * CI)
        xbc = conv_fwd(zx, full["ssm_conv_w"][i], full["ssm_conv_b"][i][None], DI, name=f"ssm_conv_{i}")
        dt_raw = zx[:, DI + CC :]
        y, states = ssd_fwd(xbc, dt_raw, ssm_dt_bias[i][None], ssm_a_log[i][None], ssm_d[i][None], DI, name=f"ssd_{i}")
        yn = gnorm_fwd(y, zx, full["ssm_norm_w"][i][None], name=f"ssm_gnorm_{i}")
        f, xo = mm(yn, g_full("ssm_w_out", i), resid=(xin, gt, 1.0), name=f"ssm_out_{i}")
        return xo, (xin, zx, xbc, dt_raw, y, states, yn, f, h)

    def att_fwd(xin, i, kv, sh, sc, gt):
        l = i - n_a
        h = norm_mod_fwd(xin, mix_norm_w[i][None], sc, sh, name=f"mix_norm_{i}")
        q = mm(h, g_full("attn_w_q", l), bias=attn_b_q[l][None], name=f"att_q_{i}")
        o = attn_fwd(q, kv, attn_sinks[l][None], name=f"att_{i}")
        f, xo = mm(o, g_full("attn_w_o", l), bias=attn_b_o[l][None], resid=(xin, gt, 1.0), name=f"att_o_{i}")
        return xo, (xin, q, o, f, h)

    saved = [[None, None, None] for _ in range(depth)]
    xc = xs
    kv = x_kv = None
    n_stage = len(fwd_stages)

    def run_stage(s, xc, tok):
        nonlocal kv, x_kv
        i = max(s - 1, 0)
        sh1, sc1, g1, shm, scm, gm, sh2, sc2, g2 = mods[i]
        if s != 1:
            if i == n_a:
                x_kv = xc
                hkv = norm_mod_fwd(xc, kv_norm_w[None], kv_scale, kv_shift + tok, name="kv_norm")
                kv = mm(hkv, g_full("w_kv", 0), bias=b_kv[None], name="kv_proj")
            xc, saved[i][0] = ffn_fwd(xc, i, 0, sh1 + tok, sc1, g1)
            if s != 0:
                cores_end(s, xc)
        if s != 0:
            xc, saved[i][1] = ssm_fwd(xc, i, shm + tok, scm, gm) if i < n_a else att_fwd(xc, i, kv, shm + tok, scm, gm)
            if s == 1:
                cores_end(s, xc)
            xc, saved[i][2] = ffn_fwd(xc, i, 1, sh2, sc2, g2)
        return xc

    dep = cores_begin(0, kvmod)
    for s in range(n_stage):
        tok = chips_begin(s + 1, dep) if s + 1 < n_stage else 0.0
        xc = run_stage(s, xc, tok)
        if s + 1 < n_stage:
            dep = cores_begin(s + 1, xc)

    loss_part, dx, d_final = final_loss(xc, final_norm_w[None], target, name="loss_head")
    loss = lax.psum(loss_part[0, 0], ("x", "y", "c"))

    wg = {}
    sg = {
        "ffn_norm_w": [[None, None] for _ in range(depth)], "mix_norm_w": [None] * depth, "mod": [None] * depth,
        "ssm_conv_w": [None] * n_a, "ssm_conv_b": [None] * n_a, "ssm_dt_bias": [None] * n_a, "ssm_a_log": [None] * n_a,
        "ssm_d": [None] * n_a, "ssm_norm_w": [None] * n_a, "attn_b_q": [None] * n_b, "attn_sinks": [None] * n_b,
        "attn_b_o": [None] * n_b,
    }

    def ffn_bwd(dxo, i, j, sv, sh, sc, gt):
        xin, gu, a, f, h = sv
        df, dgt, _ = gate_bwd(f, dxo, gt, FFN_HALF, name=f"ffn_res_bwd_{i}_{j}")
        dgu = ffn_down_bwd(df, g_dn(i, j), gu, name=f"ffn_down_dx_{i}_{j}")
        wg["ffn_w_down", 2 * i + j] = mm(a, df, mode="tn", out_dtype=BF16, name=f"ffn_down_dw_{i}_{j}")
        nw = full["ffn_norm_w"][i, j][None]
        dxi, dnw, dsc, dsh = mm_norm_bwd(dgu, g_gu(i, j), xin, nw, sc, dxo, name=f"ffn_gu_dx_{i}_{j}")
        wg["ffn_w_gu", 2 * i + j] = mm(h, dgu, mode="tn", out_dtype=BF16, name=f"ffn_gu_dw_{i}_{j}")
        sg["ffn_norm_w"][i][j] = dnw
        return dxi, (dsh, dsc, dgt)

    def ssm_bwd(dxo, i, sv, sh, sc, gt):
        xin, zx, xbc, dt_raw, y, states, yn, f, h = sv
        df, dgt, _ = gate_bwd(f, dxo, gt, 1.0, name=f"mix_res_bwd_{i}")
        dyn = mm(df, g_full("ssm_w_out", i), mode="nt", name=f"ssm_out_dx_{i}")
        wg["ssm_w_out", i] = mm(yn, df, mode="tn", out_dtype=BF16, name=f"ssm_out_dw_{i}")
        dy, dz, dnorm = gnorm_bwd(y, zx, full["ssm_norm_w"][i][None], dyn, name=f"ssm_gnorm_bwd_{i}")
        dxbc, ddt, dbias, dalog, ddsk = ssd_bwd(
            xbc, dt_raw, ssm_dt_bias[i][None], ssm_a_log[i][None], ssm_d[i][None], states, dy, DI, name=f"ssd_bwd_{i}"
        )
        du, dcw, dcb = conv_bwd(zx, full["ssm_conv_w"][i], full["ssm_conv_b"][i][None], dxbc, DI, name=f"ssm_conv_bwd_{i}")
        dzx = jnp.concatenate([dz, du, ddt], axis=1).astype(BF16)
        dzx4 = jnp.transpose(dzx.reshape(L, N_CHIPS, CI), (1, 0, 2))
        nw = mix_norm_w[i][None]
        dxi, dnw, dsc, dsh = mm_norm_bwd(
            dzx4, gw["ssm_w_in", i].reshape(N_CHIPS, D, CI), xin, nw, sc, dxo, name=f"ssm_in_dx_{i}")
        wg["ssm_w_in", i] = mm(h, dzx4, mode="tn", out_dtype=BF16, name=f"ssm_in_dw_{i}")
        sg["mix_norm_w"][i] = dnw
        sg["ssm_conv_w"][i], sg["ssm_conv_b"][i], sg["ssm_norm_w"][i] = dcw, dcb, dnorm
        sg["ssm_dt_bias"][i], sg["ssm_a_log"][i], sg["ssm_d"][i] = dbias, dalog, ddsk
        return dxi, (dsh, dsc, dgt)

    def att_bwd(dxo, i, sv, dkv, sh, sc, gt):
        l = i - n_a
        xin, q, o, f, h = sv
        df, dgt, dfsum = gate_bwd(f, dxo, gt, 1.0, name=f"mix_res_bwd_{i}")
        do = mm(df, g_full("attn_w_o", l), mode="nt", name=f"att_o_dx_{i}")
        wg["attn_w_o", l] = mm(o, df, mode="tn", out_dtype=BF16, name=f"att_o_dw_{i}")
        dq, dkv, dsink = attn_bwd(q, kv, attn_sinks[l][None], do, dkv, name=f"att_bwd_{i}")
        nw = mix_norm_w[i][None]
        dxi, dnw, dsc, dsh = mm_norm_bwd(dq, g_full("attn_w_q", l), xin, nw, sc, dxo, name=f"att_q_dx_{i}")
        wg["attn_w_q", l] = mm(h, dq, mode="tn", out_dtype=BF16, name=f"att_q_dw_{i}")
        sg["mix_norm_w"][i] = dnw
        sg["attn_b_q"][l], sg["attn_sinks"][l], sg["attn_b_o"][l] = colsum(dq, name=f"att_bq_{i}"), dsink, dfsum
        return dxi, dkv, (dsh, dsc, dgt)

    gfull = {n: lax.empty(_row_halves(W[n]).shape, F32) for n in GATHERED}

    pending, sib = [], []

    def sibling_begin(items):
        parts = []
        for n, m0, a in items:
            m, _, rh, cc = _row_halves(a).shape
            parts.append(wg.pop((n, m0)).reshape(m, N_CHIPS, 2, rh, cc))
        land_shapes = [p.shape[:2] + p.shape[3:] for p in parts]
        tag = f"{items[0][0]}_{items[0][1]}"
        ssem, rsem, srcs, lands, token = split_start(
            _sibling_copies, parts, land_shapes, parts[0], name=f"rs_sibling_start_{tag}", per=1)
        sib.append((tag, [(n, m0) for n, m0, _ in items], ssem, rsem, srcs, lands))
        return token[0:1, 0:1]

    def reduce_step(s, after):
        if pending:
            reduce_end(after)
        items = bwd_stages[s]
        mine, theirs = {}, {}
        while sib:
            tag, keys, ssem, rsem, srcs, lands = sib.pop(0)
            srcs, lands = split_wait(_sibling_copies, ssem, rsem, srcs, lands, after, name=f"rs_sibling_wait_{tag}")
            mine.update(zip(keys, srcs))
            theirs.update(zip(keys, lands))
        parts = [mine[n, m0] for n, m0, _ in items]
        from_sib = [theirs[n, m0] for n, m0, _ in items]
        pair = [rs_add_pair(g, r, name=f"rs_pair_{n}_{m0}") for (n, m0, _), g, r in zip(items, parts, from_sib)]
        land_shapes = [(3, p.shape[0]) + p.shape[2:] for p in pair]
        ssem, rsem, srcs, lands, token = split_start(_reduce_copies, pair, land_shapes, after, name=f"rs_chips_start_{s}")
        pending.append((s, items, parts, from_sib, ssem, rsem, srcs, lands))
        return token[0:1, 0:1]

    def reduce_end(after):
        s, items, parts, from_sib, ssem, rsem, srcs, lands = pending.pop()
        _, from_chips = split_wait(_reduce_copies, ssem, rsem, srcs, lands, after, name=f"rs_chips_wait_{s}")
        for (n, m0, _), g, r, t in zip(items, parts, from_sib, from_chips):
            gfull[n] = rs_add_final(g, r, t, gfull[n], m0, name=f"rs_final_{n}_{m0}")

    dkv = jnp.zeros((L, KVD), F32)
    d_kvnorm = d_kvmod = d_bkv = None
    tok = 0.0
    for i in reversed(range(depth)):
        sh1, sc1, g1, shm, scm, gm, sh2, sc2, g2 = mods[i]
        s1, sm, s2 = saved[i]
        dx, dm2 = ffn_bwd(dx, i, 1, s2, sh2, sc2, g2 + tok)
        tok = sibling_begin(ffn_items(i, 1))
        if i < n_a:
            dx, dmm = ssm_bwd(dx, i, sm, shm, scm, gm + tok)
        else:
            dx, dkv, dmm = att_bwd(dx, i, sm, dkv, shm, scm, gm + tok)
        tok = sibling_begin(mix_items(i))
        if i == 0:
            tok = tok + reduce_step(depth - 1, dx)
        dx, dm1 = ffn_bwd(dx, i, 0, s1, sh1, sc1, g1 + tok)
        tok = sibling_begin(ffn_items(i, 0))
        sg["mod"][i] = jnp.concatenate(list(dm1) + list(dmm) + list(dm2), axis=1)
        if i == n_a:
            d_bkv = colsum(dkv, name="kv_bias_bwd")
            hkv = norm_mod_fwd(x_kv, kv_norm_w[None], kv_scale, kv_shift + tok, name="kv_norm_re")
            wg["w_kv", 0] = mm(hkv, dkv, mode="tn", out_dtype=BF16, name="kv_proj_dw")
            dx, d_kvnorm, dsc, dsh = mm_norm_bwd(
                dkv, g_full("w_kv", 0), x_kv, kv_norm_w[None], kv_scale, dx, name="kv_proj_dx")
            d_kvmod = jnp.concatenate([dsh, dsc], axis=1)
            tok = sibling_begin([("w_kv", 0, w_kv)])
        if i > 0:
            tok = reduce_step(depth - 1 - i, dx)
    grad_x = dx[None]
    grads = {}

    small = {
        "ffn_norm_w": jnp.stack([jnp.stack([r[0] for r in row]) for row in sg["ffn_norm_w"]]),
        "mod_b": jnp.stack([r[0] for r in sg["mod"]]),
        "mix_norm_w": jnp.stack([r[0] for r in sg["mix_norm_w"]]),
        "ssm_conv_w": jnp.stack(sg["ssm_conv_w"]),
        "ssm_conv_b": jnp.stack([r[0] for r in sg["ssm_conv_b"]]),
        "ssm_dt_bias": jnp.stack([r[0] for r in sg["ssm_dt_bias"]]),
        "ssm_a_log": jnp.stack([r[0] for r in sg["ssm_a_log"]]),
        "ssm_d": jnp.stack([r[0] for r in sg["ssm_d"]]),
        "ssm_norm_w": jnp.stack([r[0] for r in sg["ssm_norm_w"]]),
        "kv_norm_w": d_kvnorm[0],
        "kv_mod_b": d_kvmod[0],
        "b_kv": d_bkv[0],
        "attn_b_q": jnp.stack([r[0] for r in sg["attn_b_q"]]),
        "attn_sinks": jnp.stack([r[0] for r in sg["attn_sinks"]]),
        "attn_b_o": jnp.stack([r[0] for r in sg["attn_b_o"]]),
        "final_norm_w": d_final[0],
    }
    small_like = [small[n].shape for n in SMALL]
    sv_pack, sv_n = _pack([small[n] for n in SMALL])
    sv_all = ag8(sv_pack + tok, name="ag_small_g")
    sv_all = sv_all + reduce_step(depth, sv_all)
    sv_sum = reduce8(sv_all, name="small_g_sum").reshape(-1)[:sv_n]
    for n, gsum in zip(SMALL, _unpack(sv_sum, small_like)):
        grads[n] = chip_cols(gsum, W[n].shape[-1]) if n in SMALL_SHARDED else gsum

    per_dev = [_unpack(sv_all[b].reshape(-1)[:sv_n], small_like) for b in range(N_DEV)]
    i_modb, i_kvb = SMALL.index("mod_b"), SMALL.index("kv_mod_b")
    dmod_all = jnp.stack([chip_cols(p[i_modb], MW) for p in per_dev], axis=1)
    dkv_all = jnp.stack([chip_cols(p[i_kvb], KW) for p in per_dev], axis=0)[None]
    c_t = jnp.transpose(c_all)
    grads["mod_w"] = outer8(c_t, dmod_all, name="mod_w_grad")
    grads["kv_mod_w"] = outer8(c_t, dkv_all, name="kv_mod_w_grad")[0]

    delta, new_m, new_v = {}, {}, {}
    for n in COLUMN_PARALLEL:
        delta[n], new_m[n], new_v[n] = adamw(W[n], grads[n], MOM[n], VAR[n], name=f"adamw_{n}")
    like = [W[n].shape for n in SMALL]
    packs = [_pack([d[n] for n in SMALL])[0] for d in (W, grads, MOM, VAR)]
    n_small = sum(int(W[n].size) for n in SMALL)
    for dst, res in zip((delta, new_m, new_v), adamw(*packs, name="adamw_small")):
        for n, a in zip(SMALL, _unpack(res.reshape(-1)[:n_small], like)):
            dst[n] = a

    reduce_end(delta["mod_w"])
    for n, s in zip(GATHERED, rs_share([gfull[n] for n in GATHERED], name="rs_share")):
        grads[n] = s.reshape(W[n].shape)
        delta[n], new_m[n], new_v[n] = adamw(W[n], grads[n], MOM[n], VAR[n], name=f"adamw_{n}")

    return (loss, grad_x, *[grads[n] for n in WEIGHTS], *[delta[n] for n in WEIGHTS], *[new_m[n] for n in WEIGHTS],
            *[new_v[n] for n in WEIGHTS])
```

```python
import functools

import jax
import jax.numpy as jnp
from jax import lax
from jax.experimental import pallas as pl
from jax.experimental.pallas import tpu as pltpu

F32 = jnp.float32
BF16 = jnp.bfloat16
HIGHEST = lax.Precision.HIGHEST
MESH = pl.DeviceIdType.MESH

EPS = 1e-5
N_MOD = 9
FFN_HALF = 0.5
SSM_HEADDIM = 64
SSM_GROUPS = 8
SSM_STATE = 128
CONV_WIDTH = 4
CHUNK = 128
KV_HEADS = 4
HEAD_DIM = 64
WINDOW = 128
N_CHIPS = 4
N_DEV = 8

ADAM_LR = 0.001
ADAM_B1 = 0.9
ADAM_B2 = 0.999
ADAM_EPS = 1e-08
ADAM_WD = 0.01
ADAM_STEP = 10

LANE = 128
MM_TILE = 1024


def _pick(n, pref, align, whole_if_small=False):
    best = 0
    t = align
    while t <= min(n, pref):
        if n % t == 0:
            best = t
        t += align
    if best == 0 or (whole_if_small and best < 256 and n <= 2048):
        return n
    return best


def _sigmoid(x):
    return 1.0 / (1.0 + jnp.exp(-x))


def _silu(x):
    return x * _sigmoid(x)


def _dsilu(x):
    s = _sigmoid(x)
    return s * (1.0 + x * (1.0 - s))


def _params(*sem):
    return pltpu.CompilerParams(dimension_semantics=sem)


def mm(a, b, *, mode="nn", reduce_s=False, out_dtype=F32, bias=None, resid=None, name):
    a_s = a.ndim == 3
    b_s = b.ndim == 3
    S = a.shape[0] if a_s else (b.shape[0] if b_s else 1)
    a2 = a.shape[-2:]
    b2 = b.shape[-2:]
    if mode == "nn":
        (M, K), (K2, N) = a2, b2
    elif mode == "nt":
        (M, K), (N, K2) = a2, b2
    else:
        (K, M), (K2, N) = a2, b2
    assert K == K2, (a.shape, b.shape, mode)
    batch = (a_s or b_s) and not reduce_s
    sb = S if batch else 1
    sr = S if ((a_s or b_s) and reduce_s) else 1
    tm = _pick(M, MM_TILE // 2 if resid is not None else MM_TILE, LANE if mode == "tn" else 16, True)
    tn = _pick(N, MM_TILE, LANE, True)
    tk = _pick(K, 2 * MM_TILE if mode == "tn" else MM_TILE, LANE if mode != "tn" else 16, True)
    nk = K // tk
    grid = (sb, M // tm, N // tn, sr, nk)

    def s_of(isb, isr):
        return isb if batch else isr

    def a_map(isb, i, j, isr, k):
        idx = (k, i) if mode == "tn" else (i, k)
        return ((s_of(isb, isr),) + idx) if a_s else idx

    def b_map(isb, i, j, isr, k):
        idx = (j, k) if mode == "nt" else (k, j)
        return ((s_of(isb, isr),) + idx) if b_s else idx

    def o_map(isb, i, j, isr, k):
        return (isb, i, j) if batch else (i, j)

    def s_blk(has_s, blk):
        return ((None,) + blk) if has_s else blk

    a_blk = (tk, tm) if mode == "tn" else (tm, tk)
    b_blk = (tn, tk) if mode == "nt" else (tk, tn)
    in_specs = [pl.BlockSpec(s_blk(a_s, a_blk), a_map), pl.BlockSpec(s_blk(b_s, b_blk), b_map)]
    args = [a, b]
    if bias is not None:
        bias_s = bias.ndim == 3
        in_specs.append(
            pl.BlockSpec(
                ((None, 1, tn) if bias_s else (1, tn)),
                (lambda isb, i, j, isr, k: (isb, 0, j)) if bias_s else (lambda isb, i, j, isr, k: (0, j)),
            )
        )
        args.append(bias)
    o_spec = pl.BlockSpec(s_blk(batch, (tm, tn)), o_map)
    out_shape = jax.ShapeDtypeStruct(((sb, M, N) if batch else (M, N)), out_dtype)
    out_specs = o_spec
    if resid is not None:
        assert not batch
        x_res, gate, scale = resid
        in_specs += [pl.BlockSpec((tm, tn), o_map), pl.BlockSpec((1, tn), lambda isb, i, j, isr, k: (0, j))]
        args += [x_res, gate]
        out_shape = (out_shape, jax.ShapeDtypeStruct((M, N), F32))
        out_specs = (o_spec, pl.BlockSpec((tm, tn), o_map))
    dims = {"nn": (((1,), (0,)), ((), ())), "nt": (((1,), (1,)), ((), ())), "tn": (((0,), (0,)), ((), ()))}[mode]
    n_in = len(args)
    n_out = 2 if resid is not None else 1
    one_step = sr * nk == 1

    def body(*refs):
        a_ref, b_ref = refs[0], refs[1]
        bias_ref = refs[2] if bias is not None else None
        o_ref = refs[n_in]

        def finish(r):
            if bias is not None:
                r = r + bias_ref[...]
            o_ref[...] = r.astype(o_ref.dtype)
            if resid is not None:
                refs[n_in + 1][...] = refs[n_in - 2][...] + (scale * refs[n_in - 1][...]) * r

        def part():
            return lax.dot_general(a_ref[...].astype(BF16), b_ref[...].astype(BF16), dims, preferred_element_type=F32)

        if one_step:
            finish(part())
            return
        acc = refs[n_in + n_out]
        isr = pl.program_id(3)
        k = pl.program_id(4)

        @pl.when((isr == 0) & (k == 0))
        def _():
            acc[...] = jnp.zeros_like(acc)

        acc[...] += part()

        @pl.when((isr == sr - 1) & (k == nk - 1))
        def _():
            finish(acc[...])

    return pl.pallas_call(
        body,
        out_shape=out_shape,
        grid=grid,
        in_specs=in_specs,
        out_specs=out_specs,
        scratch_shapes=[] if one_step else [pltpu.VMEM((tm, tn), F32)],
        compiler_params=_params("parallel", "parallel", "parallel", "arbitrary", "arbitrary"),
        name=name,
    )(*args)


def norm_mod_fwd(x, nw, sc, sh, *, name):
    L, D = x.shape
    tl = _pick(L, 512, 16)

    def body(x_ref, nw_ref, sc_ref, sh_ref, h_ref):
        xv = x_ref[...]
        r = lax.rsqrt(jnp.mean(xv * xv, axis=-1, keepdims=True) + EPS)
        n = (xv * r) * nw_ref[...]
        h_ref[...] = (n * (1.0 + sc_ref[...]) + sh_ref[...]).astype(h_ref.dtype)

    row = pl.BlockSpec((1, D), lambda i: (0, 0))
    return pl.pallas_call(
        body,
        out_shape=jax.ShapeDtypeStruct((L, D), BF16),
        grid=(L // tl,),
        in_specs=[pl.BlockSpec((tl, D), lambda i: (i, 0)), row, row, row],
        out_specs=pl.BlockSpec((tl, D), lambda i: (i, 0)),
        compiler_params=_params("parallel"),
        name=name,
    )(x, nw, sc, sh)


def mm_norm_bwd(a, b, x, nw, sc, dx_in, *, name):
    has_s = a.ndim == 3
    assert has_s == (b.ndim == 3)
    sr = a.shape[0] if has_s else 1
    M, K = a.shape[-2:]
    D = b.shape[-2]
    tm = _pick(M, MM_TILE // 2, 16, True)
    tk = _pick(K, MM_TILE, LANE, True)
    nk = K // tk

    def body(a_ref, b_ref, x_ref, nw_ref, sc_ref, dxi_ref, dx_ref, dnw_ref, dsc_ref, dsh_ref, acc):
        i, s, k = pl.program_id(0), pl.program_id(1), pl.program_id(2)

        @pl.when((i == 0) & (s == 0) & (k == 0))
        def _():
            dnw_ref[...] = jnp.zeros_like(dnw_ref)
            dsc_ref[...] = jnp.zeros_like(dsc_ref)
            dsh_ref[...] = jnp.zeros_like(dsh_ref)

        @pl.when((s == 0) & (k == 0))
        def _():
            acc[...] = jnp.zeros_like(acc)

        acc[...] += lax.dot_general(
            a_ref[...].astype(BF16), b_ref[...].astype(BF16), (((1,), (1,)), ((), ())), preferred_element_type=F32
        )

        @pl.when((s == sr - 1) & (k == nk - 1))
        def _():
            dh_v = acc[...]
            xv = x_ref[...]
            r = lax.rsqrt(jnp.mean(xv * xv, axis=-1, keepdims=True) + EPS)
            xhat = xv * r
            nw_v = nw_ref[...]
            n = xhat * nw_v
            dsh_ref[...] += jnp.sum(dh_v, axis=0, keepdims=True)
            dsc_ref[...] += jnp.sum(dh_v * n, axis=0, keepdims=True)
            dn = dh_v * (1.0 + sc_ref[...])
            dnw_ref[...] += jnp.sum(dn * xhat, axis=0, keepdims=True)
            dxhat = dn * nw_v
            dx_ref[...] = dxi_ref[...] + r * (dxhat - xhat * jnp.mean(dxhat * xhat, axis=-1, keepdims=True))

    row = pl.BlockSpec((1, D), lambda i, s, k: (0, 0))
    tile = pl.BlockSpec((tm, D), lambda i, s, k: (i, 0))
    vec = jax.ShapeDtypeStruct((1, D), F32)
    return pl.pallas_call(
        body,
        out_shape=(jax.ShapeDtypeStruct((M, D), F32), vec, vec, vec),
        grid=(M // tm, sr, nk),
        in_specs=[
            pl.BlockSpec((None, tm, tk) if has_s else (tm, tk), (lambda i, s, k: (s, i, k)) if has_s else (lambda i, s, k: (i, k))),
            pl.BlockSpec((None, D, tk) if has_s else (D, tk), (lambda i, s, k: (s, 0, k)) if has_s else (lambda i, s, k: (0, k))),
            tile, row, row, tile,
        ],
        out_specs=(tile, row, row, row),
        scratch_shapes=[pltpu.VMEM((tm, D), F32)],
        compiler_params=_params("arbitrary", "arbitrary", "arbitrary"),
        name=name,
    )(a, b, x, nw, sc, dx_in)


def gate_bwd(f, dx, gate, scale, *, name):
    L, D = f.shape
    tl = _pick(L, 512, 16)

    def body(f_ref, dx_ref, g_ref, df_ref, dg_ref, dfsum_ref):
        @pl.when(pl.program_id(0) == 0)
        def _():
            dg_ref[...] = jnp.zeros_like(dg_ref)
            dfsum_ref[...] = jnp.zeros_like(dfsum_ref)

        dxv = dx_ref[...]
        df = (scale * g_ref[...]) * dxv
        df_ref[...] = df.astype(df_ref.dtype)
        dfsum_ref[...] += jnp.sum(df, axis=0, keepdims=True)
        dg_ref[...] += scale * jnp.sum(f_ref[...] * dxv, axis=0, keepdims=True)

    tile = pl.BlockSpec((tl, D), lambda i: (i, 0))
    row = pl.BlockSpec((1, D), lambda i: (0, 0))
    vec = jax.ShapeDtypeStruct((1, D), F32)
    return pl.pallas_call(
        body,
        out_shape=(jax.ShapeDtypeStruct((L, D), BF16), vec, vec),
        grid=(L // tl,),
        in_specs=[tile, tile, row],
        out_specs=(tile, row, row),
        compiler_params=_params("arbitrary"),
        name=name,
    )(f, dx, gate)


def colsum(x, *, name):
    L, N = x.shape
    tl = _pick(L, 512, 8)

    def body(x_ref, o_ref):
        @pl.when(pl.program_id(0) == 0)
        def _():
            o_ref[...] = jnp.zeros_like(o_ref)

        o_ref[...] += jnp.sum(x_ref[...], axis=0, keepdims=True)

    return pl.pallas_call(
        body,
        out_shape=jax.ShapeDtypeStruct((1, N), F32),
        grid=(L // tl,),
        in_specs=[pl.BlockSpec((tl, N), lambda i: (i, 0))],
        out_specs=pl.BlockSpec((1, N), lambda i: (0, 0)),
        compiler_params=_params("arbitrary"),
        name=name,
    )(x)


def ffn_up(h, wgu, *, name):
    L, D = h.shape
    T = wgu.shape[-1]
    tm = _pick(L, 512, 16)

    def body(h_ref, w_ref, gu_ref, a_ref):
        hb = h_ref[...].astype(BF16)
        g = jnp.dot(hb, w_ref[0].astype(BF16), preferred_element_type=F32)
        u = jnp.dot(hb, w_ref[1].astype(BF16), preferred_element_type=F32)
        gu_ref[0] = g
        gu_ref[1] = u
        a_ref[...] = (_silu(g) * u).astype(a_ref.dtype)

    gu, a = pl.pallas_call(
        body,
        out_shape=(jax.ShapeDtypeStruct((2, 2, L, T), F32), jax.ShapeDtypeStruct((2, L, T), BF16)),
        grid=(2, L // tm),
        in_specs=[
            pl.BlockSpec((tm, D), lambda j, i: (i, 0)),
            pl.BlockSpec((2, None, D, T), lambda j, i: (0, j, 0, 0)),
        ],
        out_specs=(
            pl.BlockSpec((2, None, tm, T), lambda j, i: (0, j, i, 0)),
            pl.BlockSpec((None, tm, T), lambda j, i: (j, i, 0)),
        ),
        compiler_params=_params("parallel", "parallel"),
        name=name,
    )(h, wgu.reshape(2, 2, D, T))
    return gu.reshape(4, L, T), a


def ffn_down_bwd(df, wdn, gu, *, name):
    L, D = df.shape
    T = wdn.shape[1]
    tm = _pick(L, 512, 16)

    def body(df_ref, w_ref, gu_ref, d_ref):
        da = lax.dot_general(
            df_ref[...].astype(BF16), w_ref[...].astype(BF16), (((1,), (1,)), ((), ())), preferred_element_type=F32
        )
        g = gu_ref[0]
        d_ref[0] = (da * gu_ref[1] * _dsilu(g)).astype(d_ref.dtype)
        d_ref[1] = (da * _silu(g)).astype(d_ref.dtype)

    out = pl.pallas_call(
        body,
        out_shape=jax.ShapeDtypeStruct((2, 2, L, T), BF16),
        grid=(2, L // tm),
        in_specs=[
            pl.BlockSpec((tm, D), lambda j, i: (i, 0)),
            pl.BlockSpec((None, T, D), lambda j, i: (j, 0, 0)),
            pl.BlockSpec((2, None, tm, T), lambda j, i: (0, j, i, 0)),
        ],
        out_specs=pl.BlockSpec((2, None, tm, T), lambda j, i: (0, j, i, 0)),
        compiler_params=_params("parallel", "parallel"),
        name=name,
    )(df, wdn, gu.reshape(2, 2, L, T))
    return out.reshape(4, L, T)


def _shift_down(u, k, rows):
    if k == 0:
        return u
    return jnp.where(rows >= k, pltpu.roll(u, k, 0), 0.0)


def _shift_up(u, k, rows, n):
    if k == 0:
        return u
    return jnp.where(rows < n - k, pltpu.roll(u, n - k, 0), 0.0)


def _conv_pre(u, w_ref, b_ref, rows):
    pre = b_ref[...] + w_ref[CONV_WIDTH - 1 : CONV_WIDTH, :] * u
    for k in range(1, CONV_WIDTH):
        pre = pre + w_ref[CONV_WIDTH - 1 - k : CONV_WIDTH - k, :] * _shift_down(u, k, rows)
    return pre


def conv_fwd(zx, conv_w, conv_b, d_inner, *, name):
    L = zx.shape[0]
    C = conv_w.shape[1]
    tc = 256
    off = d_inner // tc

    def body(u_ref, w_ref, b_ref, o_ref):
        rows = lax.broadcasted_iota(jnp.int32, (L, tc), 0)
        o_ref[...] = _silu(_conv_pre(u_ref[...], w_ref, b_ref, rows))

    return pl.pallas_call(
        body,
        out_shape=jax.ShapeDtypeStruct((L, C), F32),
        grid=(C // tc,),
        in_specs=[
            pl.BlockSpec((L, tc), lambda j: (0, off + j)),
            pl.BlockSpec((CONV_WIDTH, tc), lambda j: (0, j)),
            pl.BlockSpec((1, tc), lambda j: (0, j)),
        ],
        out_specs=pl.BlockSpec((L, tc), lambda j: (0, j)),
        compiler_params=_params("parallel"),
        name=name,
    )(zx, conv_w, conv_b)


def conv_bwd(zx, conv_w, conv_b, dxbc, d_inner, *, name):
    L = zx.shape[0]
    C = conv_w.shape[1]
    tc = 256
    off = d_inner // tc

    def body(u_ref, w_ref, b_ref, d_ref, du_ref, dw_ref, db_ref):
        rows = lax.broadcasted_iota(jnp.int32, (L, tc), 0)
        u = u_ref[...]
        dpre = d_ref[...] * _dsilu(_conv_pre(u, w_ref, b_ref, rows))
        db_ref[...] = jnp.sum(dpre, axis=0, keepdims=True)
        du = w_ref[CONV_WIDTH - 1 : CONV_WIDTH, :] * dpre
        dw_ref[CONV_WIDTH - 1 : CONV_WIDTH, :] = jnp.sum(dpre * u, axis=0, keepdims=True)
        for k in range(1, CONV_WIDTH):
            j = CONV_WIDTH - 1 - k
            dw_ref[j : j + 1, :] = jnp.sum(dpre * _shift_down(u, k, rows), axis=0, keepdims=True)
            du = du + w_ref[j : j + 1, :] * _shift_up(dpre, k, rows, L)
        du_ref[...] = du

    return pl.pallas_call(
        body,
        out_shape=(
            jax.ShapeDtypeStruct((L, C), F32),
            jax.ShapeDtypeStruct((CONV_WIDTH, C), F32),
            jax.ShapeDtypeStruct((1, C), F32),
        ),
        grid=(C // tc,),
        in_specs=[
            pl.BlockSpec((L, tc), lambda j: (0, off + j)),
            pl.BlockSpec((CONV_WIDTH, tc), lambda j: (0, j)),
            pl.BlockSpec((1, tc), lambda j: (0, j)),
            pl.BlockSpec((L, tc), lambda j: (0, j)),
        ],
        out_specs=(
            pl.BlockSpec((L, tc), lambda j: (0, j)),
            pl.BlockSpec((CONV_WIDTH, tc), lambda j: (0, j)),
            pl.BlockSpec((1, tc), lambda j: (0, j)),
        ),
        compiler_params=_params("parallel"),
        name=name,
    )(zx, conv_w, conv_b, dxbc)


def _ssd_head(xs, dt, acs, tot, dsk, cb, bm, cm, prev):
    q = xs.shape[0]
    li = lax.broadcasted_iota(jnp.int32, (q, q), 0)
    si = lax.broadcasted_iota(jnp.int32, (q, q), 1)
    causal = li >= si
    lmat = jnp.exp(jnp.where(causal, acs - acs.T, -jnp.inf))
    xdt = xs * dt
    y = jnp.dot((cb * lmat).astype(BF16), xdt.astype(BF16), preferred_element_type=F32)
    y = y + lax.dot_general(
        (cm * jnp.exp(acs)).astype(BF16), prev.astype(BF16), (((1,), (1,)), ((), ())), preferred_element_type=F32
    )
    y = y + dsk * xs
    st = lax.dot_general(
        xdt.astype(BF16), (bm * jnp.exp(tot - acs)).astype(BF16), (((0,), (0,)), ((), ())), preferred_element_type=F32
    )
    return y, prev * jnp.exp(tot) + st


def _pick_lane(v, h):
    lanes = lax.broadcasted_iota(jnp.int32, v.shape, 1)
    return jnp.sum(jnp.where(lanes == h, v, 0.0), axis=1, keepdims=True)


def _tri_cols(cols, upper):
    q = cols[0].shape[0]
    assert 3 * len(cols) <= LANE
    li = lax.broadcasted_iota(jnp.int32, (q, q), 0)
    si = lax.broadcasted_iota(jnp.int32, (q, q), 1)
    tri = ((li <= si) if upper else (li >= si)).astype(BF16)
    lanes = lax.broadcasted_iota(jnp.int32, (q, LANE), 1)
    rhs = jnp.zeros((q, LANE), F32)
    for r, col in enumerate(cols):
        hi = col.astype(BF16).astype(F32)
        mid = (col - hi).astype(BF16).astype(F32)
        lo = col - hi - mid
        for t, term in enumerate((hi, mid, lo)):
            rhs = jnp.where(lanes == 3 * r + t, term, rhs)
    out = jnp.dot(tri, rhs.astype(BF16), preferred_element_type=F32)
    return [jnp.sum(jnp.where((lanes >= 3 * r) & (lanes < 3 * r + 3), out, 0.0), axis=1, keepdims=True)
            for r in range(len(cols))]


def _softplus(x):
    return jnp.maximum(x, 0.0) + jnp.log(1.0 + jnp.exp(-jnp.abs(x)))


def _ssd_specs(L, d_inner, H, nc, rev):
    R = H // SSM_GROUPS
    P, N, Q = SSM_HEADDIM, SSM_STATE, CHUNK
    ngrp = SSM_GROUPS

    def ci(c):
        return (nc - 1 - c) if rev else c

    b_off = d_inner // N
    c_off = b_off + ngrp
    xs = pl.BlockSpec((Q, R * P), lambda c, g: (ci(c), g))
    bm = pl.BlockSpec((Q, N), lambda c, g: (ci(c), b_off + g))
    cm = pl.BlockSpec((Q, N), lambda c, g: (ci(c), c_off + g))
    dt = pl.BlockSpec((Q, H), lambda c, g: (ci(c), 0))
    hv = pl.BlockSpec((1, H), lambda c, g: (0, 0))
    y = pl.BlockSpec((Q, R * P), lambda c, g: (ci(c), g))
    st = pl.BlockSpec((None, R * P, N), lambda c, g: (ci(c), g, 0))
    return R, xs, bm, cm, dt, hv, y, st


def ssd_fwd(xbc, dt_raw, dt_bias, a_log, d_skip, d_inner, *, name):
    L = xbc.shape[0]
    H = dt_raw.shape[1]
    nc = L // CHUNK
    P, N = SSM_HEADDIM, SSM_STATE
    R, xs_s, bm_s, cm_s, dt_s, hv_s, y_s, st_s = _ssd_specs(L, d_inner, H, nc, False)

    def body(xs_ref, bm_ref, cm_ref, dt_ref, bias_ref, alog_ref, dsk_ref, y_ref, st_ref, state):
        c = pl.program_id(0)
        g = pl.program_id(1)

        @pl.when(c == 0)
        def _():
            for r in range(R):
                state[g * R + r] = jnp.zeros((P, N), F32)

        dtb = _softplus(dt_ref[...] + bias_ref[...])
        a_all = -jnp.exp(alog_ref[...])
        bm, cm = bm_ref[...], cm_ref[...]
        cb = lax.dot_general(cm.astype(BF16), bm.astype(BF16), (((1,), (1,)), ((), ())), preferred_element_type=F32)
        dts = [_pick_lane(dtb, g * R + r) for r in range(R)]
        a_cols = [dts[r] * _pick_lane(a_all, g * R + r) for r in range(R)]
        acs = _tri_cols(a_cols, upper=False)
        prevs = [state[g * R + r] for r in range(R)]
        res = []
        for r in range(R):
            res.append(_ssd_head(
                xs_ref[:, r * P : (r + 1) * P],
                dts[r],
                jnp.broadcast_to(acs[r], (CHUNK, CHUNK)),
                jnp.sum(a_cols[r], axis=0, keepdims=True),
                _pick_lane(dsk_ref[...], g * R + r),
                cb,
                bm,
                cm,
                prevs[r],
            ))
        for r in range(R):
            st_ref[r * P : (r + 1) * P, :] = prevs[r]
            y_ref[:, r * P : (r + 1) * P] = res[r][0]
            state[g * R + r] = res[r][1]

    return pl.pallas_call(
        body,
        out_shape=(jax.ShapeDtypeStruct((L, d_inner), F32), jax.ShapeDtypeStruct((nc, H * P, N), F32)),
        grid=(nc, SSM_GROUPS),
        in_specs=[xs_s, bm_s, cm_s, dt_s, hv_s, hv_s, hv_s],
        out_specs=(y_s, st_s),
        scratch_shapes=[pltpu.VMEM((H, P, N), F32)],
        compiler_params=_params("arbitrary", "arbitrary"),
        name=name,
    )(xbc, xbc, xbc, dt_raw, dt_bias, a_log, d_skip)


def ssd_bwd(xbc, dt_raw, dt_bias, a_log, d_skip, states, dy, d_inner, *, name):
    L, C = xbc.shape
    H = dt_raw.shape[1]
    nc = L // CHUNK
    P, N, Q = SSM_HEADDIM, SSM_STATE, CHUNK
    R, xs_s, bm_s, cm_s, dt_s, hv_s, y_s, st_s = _ssd_specs(L, d_inner, H, nc, True)

    def body(xs_ref, bm_ref, cm_ref, dt_ref, bias_ref, alog_ref, dsk_ref, st_ref, dy_ref,
             dxs_ref, dbm_ref, dcm_ref, ddt_ref, dbias_ref, dalog_ref, ddsk_ref, dstate):
        c = pl.program_id(0)
        g = pl.program_id(1)

        @pl.when(c == 0)
        def _():
            for r in range(R):
                dstate[g * R + r] = jnp.zeros((P, N), F32)

        @pl.when((c == 0) & (g == 0))
        def _():
            dbias_ref[...] = jnp.zeros_like(dbias_ref)
            dalog_ref[...] = jnp.zeros_like(dalog_ref)
            ddsk_ref[...] = jnp.zeros_like(ddsk_ref)

        @pl.when(g == 0)
        def _():
            ddt_ref[...] = jnp.zeros_like(ddt_ref)

        pre = dt_ref[...] + bias_ref[...]
        dtb = _softplus(pre)
        a_all = -jnp.exp(alog_ref[...])
        lanes_q = lax.broadcasted_iota(jnp.int32, (Q, H), 1)
        lanes_1 = lax.broadcasted_iota(jnp.int32, (1, H), 1)
        bm = bm_ref[...]
        cm = cm_ref[...]
        nt = (((1,), (1,)), ((), ()))
        cb = lax.dot_general(cm.astype(BF16), bm.astype(BF16), nt, preferred_element_type=F32)
        dts = [_pick_lane(dtb, g * R + r) for r in range(R)]
        a_negs = [_pick_lane(a_all, g * R + r) for r in range(R)]
        a_cols = [dts[r] * a_negs[r] for r in range(R)]
        acs = _tri_cols(a_cols, upper=False)
        dbm = jnp.zeros((Q, N), F32)
        dcm = jnp.zeros((Q, N), F32)
        dcb = jnp.zeros((Q, Q), F32)
        dd_row = jnp.zeros((1, H), F32)
        dstates = [dstate[g * R + r] for r in range(R)]
        dprevs, ddts, dacs_cols, dtots = [], [], [], []
        for r in range(R):
            h = g * R + r
            args = (
                xs_ref[:, r * P : (r + 1) * P],
                dts[r],
                jnp.broadcast_to(acs[r], (Q, Q)),
                jnp.sum(a_cols[r], axis=0, keepdims=True),
                _pick_lane(dsk_ref[...], h),
                cb,
                bm,
                cm,
                st_ref[r * P : (r + 1) * P, :],
            )
            _, vjp = jax.vjp(_ssd_head, *args)
            dxs, ddt, dacs, dtot, dd, dcb_h, dbm_h, dcm_h, dprev = vjp((dy_ref[:, r * P : (r + 1) * P], dstates[r]))
            dxs_ref[:, r * P : (r + 1) * P] = dxs
            dprevs.append(dprev)
            ddts.append(ddt)
            dacs_cols.append(jnp.sum(dacs, axis=1, keepdims=True))
            dtots.append(dtot)
            dbm = dbm + dbm_h
            dcm = dcm + dcm_h
            dcb = dcb + dcb_h
            dd_row = dd_row + jnp.where(lanes_1 == h, dd, 0.0)
        for r in range(R):
            dstate[g * R + r] = dprevs[r]
        ddt_blk = jnp.zeros((Q, H), F32)
        da_row = jnp.zeros((1, H), F32)
        for r, da_col in enumerate(_tri_cols(dacs_cols, upper=True)):
            h = g * R + r
            da_col = da_col + dtots[r]
            ddt_blk = ddt_blk + jnp.where(lanes_q == h, ddts[r] + da_col * a_negs[r], 0.0)
            da_row = da_row + jnp.where(lanes_1 == h, jnp.sum(da_col * dts[r], axis=0, keepdims=True), 0.0)
        dcb16 = dcb.astype(BF16)
        dbm_ref[...] = dbm + lax.dot_general(dcb16, cm.astype(BF16), (((0,), (0,)), ((), ())), preferred_element_type=F32)
        dcm_ref[...] = dcm + jnp.dot(dcb16, bm.astype(BF16), preferred_element_type=F32)
        ddt_pre = ddt_blk * _sigmoid(pre)
        ddt_ref[...] += ddt_pre
        dbias_ref[...] += jnp.sum(ddt_pre, axis=0, keepdims=True)
        dalog_ref[...] += da_row * a_all
        ddsk_ref[...] += dd_row

    ngrp = SSM_GROUPS
    hrow = jax.ShapeDtypeStruct((1, H), F32)
    dxs, dbm, dcm, ddt, dbias, dalog, ddsk = pl.pallas_call(
        body,
        out_shape=(
            jax.ShapeDtypeStruct((L, d_inner), F32),
            jax.ShapeDtypeStruct((L, ngrp * N), F32),
            jax.ShapeDtypeStruct((L, ngrp * N), F32),
            jax.ShapeDtypeStruct((L, H), F32),
            hrow,
            hrow,
            hrow,
        ),
        grid=(nc, ngrp),
        in_specs=[xs_s, bm_s, cm_s, dt_s, hv_s, hv_s, hv_s, st_s, y_s],
        out_specs=(
            y_s,
            pl.BlockSpec((Q, N), lambda c, g: (nc - 1 - c, g)),
            pl.BlockSpec((Q, N), lambda c, g: (nc - 1 - c, g)),
            dt_s,
            hv_s,
            hv_s,
            hv_s,
        ),
        scratch_shapes=[pltpu.VMEM((H, P, N), F32)],
        compiler_params=_params("arbitrary", "arbitrary"),
        name=name,
    )(xbc, xbc, xbc, dt_raw, dt_bias, a_log, d_skip, states, dy)
    return jnp.concatenate([dxs, dbm, dcm], axis=1), ddt, dbias, dalog, ddsk


def gnorm_fwd(y, zx, nw, *, name):
    L, DI = y.shape
    gw = DI // SSM_GROUPS
    tl = _pick(L, 512, 16)

    def body(y_ref, z_ref, nw_ref, o_ref):
        yz = y_ref[...] * _silu(z_ref[...])
        r = lax.rsqrt(jnp.mean(yz * yz, axis=-1, keepdims=True) + EPS)
        o_ref[...] = ((yz * r) * nw_ref[...]).astype(o_ref.dtype)

    tile = pl.BlockSpec((tl, gw), lambda i, g: (i, g))
    return pl.pallas_call(
        body,
        out_shape=jax.ShapeDtypeStruct((L, DI), BF16),
        grid=(L // tl, SSM_GROUPS),
        in_specs=[tile, tile, pl.BlockSpec((1, gw), lambda i, g: (0, g))],
        out_specs=tile,
        compiler_params=_params("parallel", "parallel"),
        name=name,
    )(y, zx, nw)


def gnorm_bwd(y, zx, nw, dout, *, name):
    L, DI = y.shape
    gw = DI // SSM_GROUPS
    tl = _pick(L, 512, 16)

    def body(y_ref, z_ref, nw_ref, do_ref, dy_ref, dz_ref, dnw_ref):
        @pl.when(pl.program_id(1) == 0)
        def _():
            dnw_ref[...] = jnp.zeros_like(dnw_ref)

        yv = y_ref[...]
        zv = z_ref[...]
        sz = _silu(zv)
        yz = yv * sz
        r = lax.rsqrt(jnp.mean(yz * yz, axis=-1, keepdims=True) + EPS)
        n = yz * r
        dov = do_ref[...]
        dnw_ref[...] += jnp.sum(dov * n, axis=0, keepdims=True)
        dn = dov * nw_ref[...]
        dyz = r * (dn - n * jnp.mean(dn * n, axis=-1, keepdims=True))
        dy_ref[...] = dyz * sz
        dz_ref[...] = dyz * yv * _dsilu(zv)

    tile = pl.BlockSpec((tl, gw), lambda g, i: (i, g))
    row = pl.BlockSpec((1, gw), lambda g, i: (0, g))
    return pl.pallas_call(
        body,
        out_shape=(
            jax.ShapeDtypeStruct((L, DI), F32),
            jax.ShapeDtypeStruct((L, DI), F32),
            jax.ShapeDtypeStruct((1, DI), F32),
        ),
        grid=(SSM_GROUPS, L // tl),
        in_specs=[tile, tile, row, tile],
        out_specs=(tile, tile, row),
        compiler_params=_params("parallel", "arbitrary"),
        name=name,
    )(y, zx, nw, dout)


def _attn_head(q, kp, kc, vp, vc, sink, has_prev):
    rows, w = q.shape[0], kc.shape[0]
    nt = (((1,), (1,)), ((), ()))
    qb = q.astype(BF16)
    sc = lax.dot_general(qb, kc.astype(BF16), nt, preferred_element_type=F32) * HEAD_DIM ** -0.5
    sp = lax.dot_general(qb, kp.astype(BF16), nt, preferred_element_type=F32) * HEAD_DIM ** -0.5
    ii = jnp.bitwise_and(lax.broadcasted_iota(jnp.int32, (rows, w), 0), w - 1)
    jj = lax.broadcasted_iota(jnp.int32, (rows, w), 1)
    lc = jnp.where(jj <= ii, sc, -jnp.inf)
    lp = jnp.where((jj > ii) & has_prev, sp, -jnp.inf)
    m = jnp.maximum(jnp.maximum(jnp.max(lc, axis=1, keepdims=True), jnp.max(lp, axis=1, keepdims=True)), sink)
    m = lax.stop_gradient(m)
    pc = jnp.exp(lc - m)
    pp = jnp.exp(lp - m)
    denom = jnp.sum(pc, axis=1, keepdims=True) + jnp.sum(pp, axis=1, keepdims=True) + jnp.exp(sink - m)
    o = jnp.dot((pc / denom).astype(BF16), vc.astype(BF16), preferred_element_type=F32)
    return o + jnp.dot((pp / denom).astype(BF16), vp.astype(BF16), preferred_element_type=F32)


def attn_fwd(q, kv, sinks, *, name):
    L, DQ = q.shape
    heads = DQ // HEAD_DIM
    rep = heads // KV_HEADS
    nb = L // WINDOW
    kw = KV_HEADS * HEAD_DIM
    W, HD = WINDOW, HEAD_DIM

    def body(q_ref, kp_ref, kc_ref, vp_ref, vc_ref, s_ref, o_ref):
        has_prev = pl.program_id(0) > 0
        for kh in range(KV_HEADS):
            ks = slice(kh * HD, (kh + 1) * HD)
            hs = [kh * rep + rr for rr in range(rep)]
            o = _attn_head(
                jnp.concatenate([q_ref[:, h * HD : (h + 1) * HD] for h in hs], axis=0),
                kp_ref[:, ks], kc_ref[:, ks], vp_ref[:, ks], vc_ref[:, ks],
                jnp.concatenate([jnp.broadcast_to(s_ref[:, h : h + 1], (W, 1)) for h in hs], axis=0), has_prev,
            )
            for rr, h in enumerate(hs):
                o_ref[:, h * HD : (h + 1) * HD] = o[rr * W : (rr + 1) * W].astype(o_ref.dtype)

    return pl.pallas_call(
        body,
        out_shape=jax.ShapeDtypeStruct((L, DQ), BF16),
        grid=(nb,),
        in_specs=[
            pl.BlockSpec((W, DQ), lambda n: (n, 0)),
            pl.BlockSpec((W, kw), lambda n: (jnp.maximum(n - 1, 0), 0)),
            pl.BlockSpec((W, kw), lambda n: (n, 0)),
            pl.BlockSpec((W, kw), lambda n: (jnp.maximum(n - 1, 0), 1)),
            pl.BlockSpec((W, kw), lambda n: (n, 1)),
            pl.BlockSpec((1, heads), lambda n: (0, 0)),
        ],
        out_specs=pl.BlockSpec((W, DQ), lambda n: (n, 0)),
        compiler_params=_params("parallel"),
        name=name,
    )(q, kv, kv, kv, kv, sinks)


def attn_bwd(q, kv, sinks, do, dkv_in, *, name):
    L, DQ = q.shape
    heads = DQ // HEAD_DIM
    rep = heads // KV_HEADS
    nb = L // WINDOW
    kw = KV_HEADS * HEAD_DIM
    W, HD = WINDOW, HEAD_DIM

    def blk(n):
        return jnp.minimum(n, nb - 1)

    def prev(n):
        return jnp.maximum(blk(n) - 1, 0)

    def outb(n):
        return jnp.maximum(n - 1, 0)

    def body(q_ref, kp_ref, kc_ref, vp_ref, vc_ref, s_ref, do_ref, dki_ref, dvi_ref,
             dq_ref, dk_ref, dv_ref, ds_ref, dk_cur, dv_cur):
        n = pl.program_id(0)
        has_prev = n > 0

        @pl.when(n == 0)
        def _():
            ds_ref[...] = jnp.zeros_like(ds_ref)
            dk_cur[...] = jnp.zeros_like(dk_cur)
            dv_cur[...] = jnp.zeros_like(dv_cur)

        @pl.when(n == nb)
        def _():
            dk_ref[...] = dki_ref[...] + dk_cur[...]
            dv_ref[...] = dvi_ref[...] + dv_cur[...]

        @pl.when(n < nb)
        def _():
            lanes = lax.broadcasted_iota(jnp.int32, (1, heads), 1)
            ds_row = jnp.zeros((1, heads), F32)
            for kh in range(KV_HEADS):
                ks = slice(kh * HD, (kh + 1) * HD)
                hs = [kh * rep + rr for rr in range(rep)]
                _, vjp = jax.vjp(
                    functools.partial(_attn_head, has_prev=has_prev),
                    jnp.concatenate([q_ref[:, h * HD : (h + 1) * HD] for h in hs], axis=0),
                    kp_ref[:, ks], kc_ref[:, ks], vp_ref[:, ks], vc_ref[:, ks],
                    jnp.concatenate([jnp.broadcast_to(s_ref[:, h : h + 1], (W, 1)) for h in hs], axis=0),
                )
                dq, dkp, dkc, dvp, dvc, dsk = vjp(
                    jnp.concatenate([do_ref[:, h * HD : (h + 1) * HD] for h in hs], axis=0))
                for rr, h in enumerate(hs):
                    dq_ref[:, h * HD : (h + 1) * HD] = dq[rr * W : (rr + 1) * W]
                    ds_row = ds_row + jnp.where(lanes == h, jnp.sum(dsk[rr * W : (rr + 1) * W], axis=0, keepdims=True), 0.0)
                dk_ref[:, ks] = dki_ref[:, ks] + dk_cur[:, ks] + dkp
                dv_ref[:, ks] = dvi_ref[:, ks] + dv_cur[:, ks] + dvp
                dk_cur[:, ks] = dkc
                dv_cur[:, ks] = dvc
            ds_ref[...] += ds_row

    dq, dk, dv, ds = pl.pallas_call(
        body,
        out_shape=(
            jax.ShapeDtypeStruct((L, DQ), F32),
            jax.ShapeDtypeStruct((L, kw), F32),
            jax.ShapeDtypeStruct((L, kw), F32),
            jax.ShapeDtypeStruct((1, heads), F32),
        ),
        grid=(nb + 1,),
        in_specs=[
            pl.BlockSpec((W, DQ), lambda n: (blk(n), 0)),
            pl.BlockSpec((W, kw), lambda n: (prev(n), 0)),
            pl.BlockSpec((W, kw), lambda n: (blk(n), 0)),
            pl.BlockSpec((W, kw), lambda n: (prev(n), 1)),
            pl.BlockSpec((W, kw), lambda n: (blk(n), 1)),
            pl.BlockSpec((1, heads), lambda n: (0, 0)),
            pl.BlockSpec((W, DQ), lambda n: (blk(n), 0)),
            pl.BlockSpec((W, kw), lambda n: (outb(n), 0)),
            pl.BlockSpec((W, kw), lambda n: (outb(n), 1)),
        ],
        out_specs=(
            pl.BlockSpec((W, DQ), lambda n: (blk(n), 0)),
            pl.BlockSpec((W, kw), lambda n: (outb(n), 0)),
            pl.BlockSpec((W, kw), lambda n: (outb(n), 0)),
            pl.BlockSpec((1, heads), lambda n: (0, 0)),
        ),
        scratch_shapes=[pltpu.VMEM((W, kw), F32), pltpu.VMEM((W, kw), F32)],
        compiler_params=_params("arbitrary"),
        name=name,
    )(q, kv, kv, kv, kv, sinks, do, dkv_in, dkv_in)
    return dq, jnp.concatenate([dk, dv], axis=1), ds


def final_loss(x, fw, target, *, name):
    L, D = x.shape
    tl = _pick(L, 512, 8)

    def body(x_ref, fw_ref, t_ref, loss_ref, dx_ref, dfw_ref):
        @pl.when(pl.program_id(0) == 0)
        def _():
            loss_ref[...] = jnp.zeros_like(loss_ref)
            dfw_ref[...] = jnp.zeros_like(dfw_ref)

        xv = x_ref[...]
        fwv = fw_ref[...]
        r = lax.rsqrt(jnp.mean(xv * xv, axis=-1, keepdims=True) + EPS)
        xhat = xv * r
        err = xhat * fwv - t_ref[...]
        loss_ref[...] += 0.5 * jnp.sum(jnp.mean(err * err, axis=-1, keepdims=True), axis=0, keepdims=True)
        dy = err * (1.0 / D)
        dfw_ref[...] += jnp.sum(dy * xhat, axis=0, keepdims=True)
        dxhat = dy * fwv
        dx_ref[...] = r * (dxhat - xhat * jnp.mean(dxhat * xhat, axis=-1, keepdims=True))

    tile = pl.BlockSpec((tl, D), lambda i: (i, 0))
    row = pl.BlockSpec((1, D), lambda i: (0, 0))
    return pl.pallas_call(
        body,
        out_shape=(
            jax.ShapeDtypeStruct((1, 1), F32),
            jax.ShapeDtypeStruct((L, D), F32),
            jax.ShapeDtypeStruct((1, D), F32),
        ),
        grid=(L // tl,),
        in_specs=[tile, row, tile],
        out_specs=(pl.BlockSpec((1, 1), lambda i: (0, 0)), tile, row),
        compiler_params=_params("arbitrary"),
        name=name,
    )(x, fw, target)


def outer8(ct, d, *, name):
    D, B = ct.shape
    S, _, N = d.shape
    tm = _pick(D, 512, 8)
    tn = _pick(N, 256, LANE)

    def body(c_ref, d_ref, o_ref):
        acc = c_ref[:, 0:1] * d_ref[0:1, :]
        for b in range(1, B):
            acc = acc + c_ref[:, b : b + 1] * d_ref[b : b + 1, :]
        o_ref[...] = acc

    return pl.pallas_call(
        body,
        out_shape=jax.ShapeDtypeStruct((S, D, N), F32),
        grid=(S, D // tm, N // tn),
        in_specs=[
            pl.BlockSpec((tm, B), lambda s, i, j: (i, 0)),
            pl.BlockSpec((None, B, tn), lambda s, i, j: (s, 0, j)),
        ],
        out_specs=pl.BlockSpec((None, tm, tn), lambda s, i, j: (s, i, j)),
        compiler_params=_params("parallel", "parallel", "parallel"),
        name=name,
    )(ct, d)


def reduce8(g, *, name):
    nd, R, N = g.shape

    def body(g_ref, o_ref):
        acc = g_ref[0]
        for b in range(1, nd):
            acc = acc + g_ref[b]
        o_ref[...] = acc

    return pl.pallas_call(
        body,
        out_shape=jax.ShapeDtypeStruct((R, N), F32),
        name=name,
    )(g)


def _as3(a):
    if a.ndim == 1:
        return a.reshape(1, 1, -1)
    if a.ndim == 2:
        return a.reshape((1,) + a.shape)
    return a.reshape((-1,) + a.shape[-2:])


def adamw(w, g, m, v, *, name):
    shape = w.shape
    w3, g3, m3, v3 = _as3(w), _as3(g), _as3(m), _as3(v)
    B, R, C = w3.shape
    tr = _pick(R, max(8, (1 << 19) // max(C, 1) // 8 * 8), 8)

    def body(w_ref, g_ref, m_ref, v_ref, d_ref, nm_ref, nv_ref):
        gv = g_ref[...]
        mn = ADAM_B1 * m_ref[...] + (1.0 - ADAM_B1) * gv
        vn = ADAM_B2 * v_ref[...] + (1.0 - ADAM_B2) * (gv * gv)
        m_hat = mn / (1.0 - ADAM_B1 ** ADAM_STEP)
        v_hat = vn / (1.0 - ADAM_B2 ** ADAM_STEP)
        d_ref[...] = -ADAM_LR * (m_hat / (jnp.sqrt(v_hat) + ADAM_EPS) + ADAM_WD * w_ref[...])
        nm_ref[...] = mn
        nv_ref[...] = vn

    tile = pl.BlockSpec((None, tr, C), lambda b, i: (b, i, 0))
    sds = jax.ShapeDtypeStruct((B, R, C), F32)
    d, nm, nv = pl.pallas_call(
        body,
        out_shape=(sds, sds, sds),
        grid=(B, R // tr),
        in_specs=[tile, tile, tile, tile],
        out_specs=(tile, tile, tile),
        compiler_params=_params("parallel", "parallel"),
        name=name,
    )(w3, g3, m3, v3)
    return d.reshape(shape), nm.reshape(shape), nv.reshape(shape)


def _place():
    return lax.axis_index("x"), lax.axis_index("y"), lax.axis_index("c")


def _flip(v, bit):
    return (1 - v) if bit else v


def ag8(v, *, act=None, after=None, name):
    R, N = v.shape
    extra = [] if after is None else [after]

    def body(*refs):
        v_ref = refs[0]
        out_ref, stage, send_sems, recv_sems = refs[1 + len(extra):]
        x, y, c = _place()
        me = 4 * x + 2 * y + c
        val = v_ref[...]
        if act is not None:
            val = act(val)
        stage[...] = val
        out_ref[me] = val
        sends = []
        for k in range(1, N_DEV):
            px, py, pc = _flip(x, k & 4), _flip(y, k & 2), _flip(c, k & 1)
            cp = pltpu.make_async_remote_copy(
                src_ref=stage, dst_ref=out_ref.at[me], send_sem=send_sems.at[k - 1], recv_sem=recv_sems.at[k - 1],
                device_id=(px, py, pc), device_id_type=MESH,
            )
            cp.start()
            sends.append(cp)
        for k in range(1, N_DEV):
            px, py, pc = _flip(x, k & 4), _flip(y, k & 2), _flip(c, k & 1)
            pltpu.make_async_remote_copy(
                src_ref=stage, dst_ref=out_ref.at[4 * px + 2 * py + pc], send_sem=send_sems.at[k - 1],
                recv_sem=recv_sems.at[k - 1], device_id=(px, py, pc), device_id_type=MESH,
            ).wait_recv()
        for cp in sends:
            cp.wait_send()

    return pl.pallas_call(
        body,
        out_shape=jax.ShapeDtypeStruct((N_DEV, R, N), F32),
        in_specs=[pl.BlockSpec(memory_space=pltpu.VMEM)] + [_ANY] * len(extra),
        out_specs=pl.BlockSpec(memory_space=pltpu.VMEM),
        scratch_shapes=[
            pltpu.VMEM((R, N), F32),
            pltpu.SemaphoreType.DMA((N_DEV - 1,)),
            pltpu.SemaphoreType.DMA((N_DEV - 1,)),
        ],
        name=name,
    )(v, *extra)


def _other_chips(x, y):
    chips = [(1 - x, y), (x, 1 - y), (1 - x, 1 - y)]
    return chips, [2 * px + py for px, py in chips]


_HBM = pl.BlockSpec(memory_space=pltpu.HBM)


_SEM = pl.BlockSpec(memory_space=pltpu.SEMAPHORE)
_ANY = pl.BlockSpec(memory_space=pl.ANY)
_EFFECT = pltpu.SideEffectType.DATAFLOW_SIDE_EFFECTING


def _gather_copies(srcs, lands, send_sems, recv_sems):
    x, y, c = _place()
    k_me = 2 * x + y
    chips, kidx = _other_chips(x, y)
    cps = []
    for w in range(len(srcs)):
        for j, (px, py) in enumerate(chips):
            def copy(dst, w=w, j=j, px=px, py=py):
                return pltpu.make_async_remote_copy(
                    src_ref=srcs[w].at[:, c], dst_ref=dst, send_sem=send_sems.at[3 * w + j],
                    recv_sem=recv_sems.at[3 * w + j], device_id=(px, py, c), device_id_type=MESH,
                )
            cps.append((copy(lands[w].at[:, k_me, c]), copy(lands[w].at[:, kidx[j], c])))
    return cps


def _fill_copies(srcs, lands, send_sems, recv_sems):
    x, y, c = _place()
    k_me = 2 * x + y
    _, kidx = _other_chips(x, y)
    sib = (x, y, 1 - c)
    cps = []
    for w in range(len(srcs)):
        own = pltpu.make_async_remote_copy(
            src_ref=srcs[w], dst_ref=lands[w].at[:, k_me], send_sem=send_sems.at[4 * w + 3], recv_sem=recv_sems.at[4 * w + 3],
            device_id=sib, device_id_type=MESH,
        )
        cps.append((own, own))
        for j in range(3):
            def copy(half, w=w, j=j):
                part = lands[w].at[:, kidx[j], half]
                return pltpu.make_async_remote_copy(
                    src_ref=part, dst_ref=part, send_sem=send_sems.at[4 * w + j], recv_sem=recv_sems.at[4 * w + j],
                    device_id=sib, device_id_type=MESH,
                )
            cps.append((copy(c), copy(1 - c)))
    return cps


def _sibling_copies(srcs, lands, send_sems, recv_sems):
    x, y, c = _place()
    cps = []
    for w in range(len(srcs)):
        cp = pltpu.make_async_remote_copy(
            src_ref=srcs[w].at[:, :, 1 - c], dst_ref=lands[w], send_sem=send_sems.at[w], recv_sem=recv_sems.at[w],
            device_id=(x, y, 1 - c), device_id_type=MESH,
        )
        cps.append((cp, cp))
    return cps


def _reduce_copies(srcs, lands, send_sems, recv_sems):
    x, y, c = _place()
    chips, kidx = _other_chips(x, y)
    cps = []
    for w in range(len(srcs)):
        for j, (px, py) in enumerate(chips):
            cp = pltpu.make_async_remote_copy(
                src_ref=srcs[w].at[:, kidx[j]], dst_ref=lands[w].at[j], send_sem=send_sems.at[3 * w + j],
                recv_sem=recv_sems.at[3 * w + j], device_id=(px, py, c), device_id_type=MESH,
            )
            cps.append((cp, cp))
    return cps


def split_start(copies, srcs, land_shapes, after, *, name, lands=None, per=3):
    n = len(srcs)

    def body(*refs):
        src_refs, land_refs = refs[:n], refs[n : 2 * n]
        send_sems, recv_sems = refs[2 * n + 1], refs[2 * n + 2]
        token = refs[-1]
        for cp, _ in copies(src_refs, land_refs, send_sems, recv_sems):
            cp.start()
        token[...] = jnp.zeros_like(token)

    if lands is None:
        lands = [lax.empty(sh, s.dtype) for sh, s in zip(land_shapes, srcs)]
    land_shapes = [a.shape for a in lands]
    lands = [pltpu.with_memory_space_constraint(a, pltpu.HBM) for a in lands]
    srcs = [pltpu.with_memory_space_constraint(s, pltpu.HBM) for s in srcs]
    out = pl.pallas_call(
        body,
        out_shape=(
            pltpu.SemaphoreType.DMA((per * n,)), pltpu.SemaphoreType.DMA((per * n,)),
            *[pltpu.HBM(s.shape, s.dtype) for s in srcs],
            *[pltpu.HBM(sh, s.dtype) for sh, s in zip(land_shapes, srcs)],
            jax.ShapeDtypeStruct((8, LANE), F32),
        ),
        in_specs=[_HBM] * (2 * n) + [_ANY],
        out_specs=(_SEM, _SEM, *([_HBM] * (2 * n)), pl.BlockSpec(memory_space=pltpu.VMEM)),
        input_output_aliases={i: 2 + i for i in range(2 * n)},
        compiler_params=pltpu.CompilerParams(has_side_effects=_EFFECT),
        name=name,
    )(*srcs, *lands, after)
    return out[0], out[1], list(out[2 : 2 + n]), list(out[2 + n : 2 + 2 * n]), out[-1]


def split_wait(copies, send_sems, recv_sems, srcs, lands, after, *, name):
    n = len(srcs)

    def body(*refs):
        src_refs, land_refs = refs[:n], refs[n : 2 * n]
        send_ref, recv_ref = refs[2 * n], refs[2 * n + 1]
        for sent, arrives in copies(src_refs, land_refs, send_ref, recv_ref):
            sent.wait_send()
            arrives.wait_recv()

    out = pl.pallas_call(
        body,
        out_shape=tuple(pltpu.HBM(a.shape, a.dtype) for a in list(srcs) + list(lands)),
        in_specs=[_HBM] * (2 * n) + [_SEM, _SEM, _ANY],
        out_specs=tuple([_HBM] * (2 * n)),
        input_output_aliases={i: i for i in range(2 * n)},
        compiler_params=pltpu.CompilerParams(has_side_effects=_EFFECT),
        name=name,
    )(*srcs, *lands, send_sems, recv_sems, after)
    return list(out[:n]), list(out[n:])


def rs_share(halves, *, name):
    n = len(halves)

    def body(*refs):
        outs = refs[n : 2 * n]
        send_sems, recv_sems = refs[2 * n :]
        x, y, c = _place()
        cps = []
        for w in range(n):
            cp = pltpu.make_async_remote_copy(
                src_ref=outs[w].at[:, c], dst_ref=outs[w].at[:, c], send_sem=send_sems.at[w], recv_sem=recv_sems.at[w],
                device_id=(x, y, 1 - c), device_id_type=MESH,
            )
            cp.start()
            cps.append(cp)
        for w, cp in enumerate(cps):
            cp.wait_send()
            pltpu.make_async_remote_copy(
                src_ref=outs[w].at[:, c], dst_ref=outs[w].at[:, 1 - c], send_sem=send_sems.at[w], recv_sem=recv_sems.at[w],
                device_id=(x, y, 1 - c), device_id_type=MESH,
            ).wait_recv()

    return pl.pallas_call(
        body,
        out_shape=tuple(jax.ShapeDtypeStruct(h.shape, h.dtype) for h in halves),
        in_specs=[_HBM] * n,
        out_specs=tuple([_HBM] * n),
        scratch_shapes=[pltpu.SemaphoreType.DMA((n,)), pltpu.SemaphoreType.DMA((n,))],
        input_output_aliases={w: w for w in range(n)},
        name=name,
    )(*halves)


def _row_tile(R, C):
    return _pick(R, max(16, (1 << 19) // C // 16 * 16), 16)


def _my_core():
    return lax.axis_index("c")


def _my_chip():
    return 2 * lax.axis_index("x") + lax.axis_index("y")


def rs_add_pair(g, r, *, name):
    M, K, _, R, C = g.shape
    tr = _row_tile(R, C)

    def body(g_ref, r_ref, o_ref):
        o_ref[...] = (g_ref[...].astype(F32) + r_ref[...].astype(F32)).astype(o_ref.dtype)

    blk = pl.BlockSpec((None, None, tr, C), lambda m, k, i: (m, k, i, 0))
    return pl.pallas_call(
        body,
        out_shape=jax.ShapeDtypeStruct((M, K, R, C), BF16),
        grid=(M, K, R // tr),
        in_specs=[pl.BlockSpec((None, None, None, tr, C), lambda m, k, i: (m, k, _my_core(), i, 0)), blk],
        out_specs=blk,
        compiler_params=_params("parallel", "parallel", "parallel"),
        name=name,
    )(g, r)


def rs_add_final(g, r, t, full, m0, *, name):
    M, K, _, R, C = g.shape
    tr = _row_tile(R, C)

    def body(g_ref, r_ref, t_ref, full_ref, o_ref):
        acc = g_ref[...].astype(F32) + r_ref[...].astype(F32)
        for j in range(3):
            acc = acc + t_ref[j].astype(F32)
        o_ref[...] = acc

    return pl.pallas_call(
        body,
        out_shape=jax.ShapeDtypeStruct(full.shape, full.dtype),
        grid=(M, R // tr),
        in_specs=[
            pl.BlockSpec((None, None, None, tr, C), lambda m, i: (m, _my_chip(), _my_core(), i, 0)),
            pl.BlockSpec((None, None, tr, C), lambda m, i: (m, _my_chip(), i, 0)),
            pl.BlockSpec((3, None, tr, C), lambda m, i: (0, m, i, 0)),
            _ANY,
        ],
        out_specs=pl.BlockSpec((None, None, tr, C), lambda m, i: (m0 + m, _my_core(), i, 0)),
        input_output_aliases={3: 0},
        compiler_params=_params("parallel", "parallel"),
        name=name,
    )(g, r, t, full)


WEIGHTS = ["ffn_norm_w", "ffn_w_gu", "ffn_w_down", "mod_w", "mod_b", "mix_norm_w", "ssm_w_in", "ssm_conv_w", "ssm_conv_b",
           "ssm_dt_bias", "ssm_a_log", "ssm_d", "ssm_norm_w", "ssm_w_out", "kv_norm_w", "kv_mod_w", "kv_mod_b", "w_kv", "b_kv",
           "attn_w_q", "attn_b_q", "attn_sinks", "attn_w_o", "attn_b_o", "final_norm_w"]
GATHERED = ["ffn_w_gu", "ffn_w_down", "ssm_w_in", "ssm_w_out", "w_kv", "attn_w_q", "attn_w_o"]
COLUMN_PARALLEL = ["mod_w", "kv_mod_w"]
SMALL_SHARDED = ["ffn_norm_w", "ssm_conv_w", "ssm_conv_b", "ssm_norm_w"]
SMALL = [n for n in WEIGHTS if n not in GATHERED and n not in COLUMN_PARALLEL]


def _row_halves(a):
    a = a.reshape((-1,) + a.shape[-2:])
    return a.reshape(a.shape[0], 2, a.shape[1] // 2, a.shape[2])


def _pack(arrs, rows=8):
    flat = jnp.concatenate([a.reshape(-1) for a in arrs])
    n = flat.shape[0]
    pad = (-n) % (rows * LANE)
    return jnp.pad(flat, (0, pad)).reshape(rows, -1), n


def _unpack(flat, like):
    out, o = [], 0
    for s in like:
        k = 1
        for d in s:
            k *= d
        out.append(flat[o : o + k].reshape(s))
        o += k
    return out


def kernel(x, c, ffn_norm_w, ffn_w_gu, ffn_w_down, mod_w, mod_b, mix_norm_w, ssm_w_in, ssm_conv_w, ssm_conv_b, ssm_dt_bias, ssm_a_log, ssm_d, ssm_norm_w, ssm_w_out, kv_norm_w, kv_mod_w, kv_mod_b, w_kv, b_kv, attn_w_q, attn_b_q, attn_sinks, attn_w_o, attn_b_o, final_norm_w, loss_target, m_ffn_norm_w, m_ffn_w_gu, m_ffn_w_down, m_mod_w, m_mod_b, m_mix_norm_w, m_ssm_w_in, m_ssm_conv_w, m_ssm_conv_b, m_ssm_dt_bias, m_ssm_a_log, m_ssm_d, m_ssm_norm_w, m_ssm_w_out, m_kv_norm_w, m_kv_mod_w, m_kv_mod_b, m_w_kv, m_b_kv, m_attn_w_q, m_attn_b_q, m_attn_sinks, m_attn_w_o, m_attn_b_o, m_final_norm_w, v_ffn_norm_w, v_ffn_w_gu, v_ffn_w_down, v_mod_w, v_mod_b, v_mix_norm_w, v_ssm_w_in, v_ssm_conv_w, v_ssm_conv_b, v_ssm_dt_bias, v_ssm_a_log, v_ssm_d, v_ssm_norm_w, v_ssm_w_out, v_kv_norm_w, v_kv_mod_w, v_kv_mod_b, v_w_kv, v_b_kv, v_attn_w_q, v_attn_b_q, v_attn_sinks, v_attn_w_o, v_attn_b_o, v_final_norm_w):
    env = dict(locals())
    W = {n: env[n] for n in WEIGHTS}
    MOM = {n: env["m_" + n] for n in WEIGHTS}
    VAR = {n: env["v_" + n] for n in WEIGHTS}

    ax, ay, ac = _place()
    kme = 2 * ax + ay
    me = 4 * ax + 2 * ay + ac

    xs = x[0]
    target = loss_target[0]
    L, D = xs.shape
    depth, n_a = ffn_w_gu.shape[0], ssm_w_in.shape[0]
    n_b = depth - n_a
    T = ffn_w_gu.shape[-1]
    DI = ssm_w_out.shape[1] * N_CHIPS
    CI = ssm_w_in.shape[2]
    CC = ssm_conv_w.shape[2] * N_CHIPS
    MW = mod_w.shape[2]
    KW = kv_mod_w.shape[1]
    KVD = w_kv.shape[1]

    def chip_cols(a, width):
        return lax.dynamic_slice_in_dim(a, kme * width, width, axis=a.ndim - 1)

    def ffn_items(i, j):
        return [("ffn_w_gu", 2 * i + j, ffn_w_gu[i, j]), ("ffn_w_down", 2 * i + j, ffn_w_down[i, j])]

    def mix_items(i):
        if i < n_a:
            return [("ssm_w_in", i, ssm_w_in[i]), ("ssm_w_out", i, ssm_w_out[i])]
        return [("attn_w_q", i - n_a, attn_w_q[i - n_a]), ("attn_w_o", i - n_a, attn_w_o[i - n_a])]

    def layer_items(i, order):
        kv_items = [("w_kv", 0, w_kv)] if i == n_a else []
        if order == "fwd":
            return kv_items + ffn_items(i, 0) + mix_items(i) + ffn_items(i, 1)
        return ffn_items(i, 1) + mix_items(i) + ffn_items(i, 0) + kv_items

    def sub_items(sub):
        return {"F": ffn_items, "M": mix_items, "KV": lambda: [("w_kv", 0, w_kv)]}[sub[0]](*sub[1:])

    subs = []
    for i in range(depth):
        subs += ([("KV",)] if i == n_a else []) + [("F", i, 0), ("M", i), ("F", i, 1)]
    assert n_a >= 2 and depth >= 2
    cuts = [0, 1, 2, 4, 6] + [3 * (i + 1) + (1 if i >= n_a else 0) for i in range(2, depth)]
    fwd_plan = [subs[a:b] for a, b in zip(cuts[:-1], cuts[1:])]
    fwd_stages = [[it for sub in stage for it in sub_items(sub)] for stage in fwd_plan]
    bwd_stages = [layer_items(i, "bwd") for i in range(depth - 1, 0, -1)] + [ffn_items(0, 1) + mix_items(0), ffn_items(0, 0)]

    gw, inflight = {}, {}

    def chips_begin(s, after):
        keys = [(n, m0) for n, m0, _ in fwd_stages[s]]
        shards = [_row_halves(a.astype(BF16)) for _, _, a in fwd_stages[s]]
        land_shapes = [(sh.shape[0], N_CHIPS) + sh.shape[1:] for sh in shards]
        ssem, rsem, srcs, lands, token = split_start(_gather_copies, shards, land_shapes, after, name=f"gather_start_{s}")
        inflight[s] = (keys, ssem, rsem, srcs, lands)
        return token[0:1, 0:1]

    def n_head(s):
        return 2 if fwd_plan[s][0][0] == "KV" else 1

    def n_first(s):
        return sum(len(sub_items(sub)) for sub in fwd_plan[s][: n_head(s)])

    def cores_begin(s, after):
        keys, ssem, rsem, srcs, lands = inflight.pop(s)
        srcs, lands = split_wait(_gather_copies, ssem, rsem, srcs, lands, after, name=f"gather_wait_{s}")
        k = n_first(s)
        a_sem, b_sem, a_src, a_land, tok = split_start(
            _fill_copies, srcs[:k], None, lands[0], name=f"fill_start_{s}a", lands=lands[:k], per=4)
        if k < len(keys):
            inflight[s] = (keys[k:],) + split_start(
                _fill_copies, srcs[k:], None, tok, name=f"fill_start_{s}b", lands=lands[k:], per=4)
            tok = inflight[s][-1]
        gw.update(zip(keys[:k], split_wait(_fill_copies, a_sem, b_sem, a_src, a_land, tok, name=f"fill_wait_{s}a")[1]))
        return tok[0:1, 0:1]

    def cores_end(s, after):
        if s in inflight:
            keys, ssem, rsem, srcs, lands, _ = inflight.pop(s)
            gw.update(zip(keys, split_wait(_fill_copies, ssem, rsem, srcs, lands, after, name=f"fill_wait_{s}b")[1]))

    def g_gu(i, j):
        return gw["ffn_w_gu", 2 * i + j].reshape(N_CHIPS, D, T)

    def g_dn(i, j):
        return gw["ffn_w_down", 2 * i + j].reshape(2, T, D)

    def g_full(n, m0):
        a = gw[n, m0]
        return a.reshape(N_CHIPS * 2 * a.shape[-2], a.shape[-1])

    sm_like = [W[n].shape for n in SMALL_SHARDED]
    sm_pack, sm_n = _pack([W[n] for n in SMALL_SHARDED])
    sm_gathered = ag8(sm_pack, name="ag_small_w")
    sm_all = sm_gathered[0::2].reshape(N_CHIPS, -1)[:, :sm_n]
    full = {}
    for n, part in zip(SMALL_SHARDED, zip(*[_unpack(sm_all[k], sm_like) for k in range(N_CHIPS)])):
        full[n] = jnp.concatenate(part, axis=-1)

    c_all = ag8(c, act=_silu, after=sm_gathered, name="ag_c").reshape(N_DEV, D)
    c_all = c_all + chips_begin(0, c_all)
    p_mod = mm(c_all, mod_w, bias=chip_cols(mod_b, MW)[:, None, :], name="mod_mm")
    p_kv = mm(c_all, kv_mod_w, bias=chip_cols(kv_mod_b, KW)[None, :], name="kvmod_mm")
    p_all = jnp.concatenate([jnp.transpose(p_mod, (1, 0, 2)).reshape(N_DEV, depth * MW), p_kv], axis=1)
    p_mine = lax.dynamic_index_in_dim(ag8(p_all, name="ag_mod")[0::2], me, axis=1, keepdims=False)
    mod = jnp.transpose(p_mine[:, : depth * MW].reshape(N_CHIPS, depth, MW), (1, 0, 2)).reshape(depth, N_MOD * D)
    kvmod = p_mine[:, depth * MW :].reshape(1, 2 * D)
    mods = [[mod[i : i + 1, j * D : (j + 1) * D] for j in range(N_MOD)] for i in range(depth)]
    kv_shift, kv_scale = kvmod[:, :D], kvmod[:, D:]

    def ffn_fwd(xin, i, j, sh, sc, gt):
        h = norm_mod_fwd(xin, full["ffn_norm_w"][i, j][None], sc, sh, name=f"ffn_norm_{i}_{j}")
        gu, a = ffn_up(h, g_gu(i, j), name=f"ffn_gu_{i}_{j}")
        f, xo = mm(a, g_dn(i, j), reduce_s=True, resid=(xin, gt, FFN_HALF), name=f"ffn_down_{i}_{j}")
        return xo, (xin, gu, a, f, h)

    def ssm_fwd(xin, i, sh, sc, gt):
        h = norm_mod_fwd(xin, mix_norm_w[i][None], sc, sh, name=f"mix_norm_{i}")
        zx4 = mm(h, gw["ssm_w_in", i].reshape(N_CHIPS, D, CI), name=f"ssm_in_{i}")
        zx = jnp.transpose(zx4, (1, 0, 2)).reshape(L, N_CHIPS * CI)
        xbc = conv_fwd(zx, full["ssm_conv_w"][i], full["ssm_conv_b"][i][None], DI, name=f"ssm_conv_{i}")
        dt_raw = zx[:, DI + CC :]
        y, states = ssd_fwd(xbc, dt_raw, ssm_dt_bias[i][None], ssm_a_log[i][None], ssm_d[i][None], DI, name=f"ssd_{i}")
        yn = gnorm_fwd(y, zx, full["ssm_norm_w"][i][None], name=f"ssm_gnorm_{i}")
        f, xo = mm(yn, g_full("ssm_w_out", i), resid=(xin, gt, 1.0), name=f"ssm_out_{i}")
        return xo, (xin, zx, xbc, dt_raw, y, states, yn, f, h)

    def att_fwd(xin, i, kv, sh, sc, gt):
        l = i - n_a
        h = norm_mod_fwd(xin, mix_norm_w[i][None], sc, sh, name=f"mix_norm_{i}")
        q = mm(h, g_full("attn_w_q", l), bias=attn_b_q[l][None], name=f"att_q_{i}")
        o = attn_fwd(q, kv, attn_sinks[l][None], name=f"att_{i}")
        f, xo = mm(o, g_full("attn_w_o", l), bias=attn_b_o[l][None], resid=(xin, gt, 1.0), name=f"att_o_{i}")
        return xo, (xin, q, o, f, h)

    saved = [[None, None, None] for _ in range(depth)]
    xc = xs
    kv = x_kv = None
    n_stage = len(fwd_stages)

    def run_sub(sub, xc, tok):
        nonlocal kv, x_kv
        if sub[0] == "KV":
            x_kv = xc
            hkv = norm_mod_fwd(xc, kv_norm_w[None], kv_scale, kv_shift + tok, name="kv_norm")
            kv = mm(hkv, g_full("w_kv", 0), bias=b_kv[None], name="kv_proj")
            return xc
        i = sub[1]
        if sub[0] == "F":
            sh, sc, gt = mods[i][6 * sub[2] : 6 * sub[2] + 3]
            xc, saved[i][2 * sub[2]] = ffn_fwd(xc, i, sub[2], sh + tok, sc, gt)
            return xc
        sh, sc, gt = mods[i][3:6]
        xc, saved[i][1] = ssm_fwd(xc, i, sh + tok, sc, gt) if i < n_a else att_fwd(xc, i, kv, sh + tok, sc, gt)
        return xc

    def run_stage(s, xc, tok):
        for k, sub in enumerate(fwd_plan[s]):
            if k == n_head(s):
                cores_end(s, xc)
            xc = run_sub(sub, xc, tok if k < n_head(s) else 0.0)
        return xc

    dep = cores_begin(0, kvmod)
    for s in range(n_stage):
        tok = chips_begin(s + 1, dep) if s + 1 < n_stage else 0.0
        xc = run_stage(s, xc, tok)
        if s + 1 < n_stage:
            dep = cores_begin(s + 1, xc)

    loss_part, dx, d_final = final_loss(xc, final_norm_w[None], target, name="loss_head")
    loss = lax.psum(loss_part[0, 0], ("x", "y", "c"))

    wg = {}
    sg = {
        "ffn_norm_w": [[None, None] for _ in range(depth)], "mix_norm_w": [None] * depth, "mod": [None] * depth,
        "ssm_conv_w": [None] * n_a, "ssm_conv_b": [None] * n_a, "ssm_dt_bias": [None] * n_a, "ssm_a_log": [None] * n_a,
        "ssm_d": [None] * n_a, "ssm_norm_w": [None] * n_a, "attn_b_q": [None] * n_b, "attn_sinks": [None] * n_b,
        "attn_b_o": [None] * n_b,
    }

    def ffn_bwd(dxo, i, j, sv, sh, sc, gt):
        xin, gu, a, f, h = sv
        df, dgt, _ = gate_bwd(f, dxo, gt, FFN_HALF, name=f"ffn_res_bwd_{i}_{j}")
        dgu = ffn_down_bwd(df, g_dn(i, j), gu, name=f"ffn_down_dx_{i}_{j}")
        wg["ffn_w_down", 2 * i + j] = mm(a, df, mode="tn", out_dtype=BF16, name=f"ffn_down_dw_{i}_{j}")
        nw = full["ffn_norm_w"][i, j][None]
        dxi, dnw, dsc, dsh = mm_norm_bwd(dgu, g_gu(i, j), xin, nw, sc, dxo, name=f"ffn_gu_dx_{i}_{j}")
        wg["ffn_w_gu", 2 * i + j] = mm(h, dgu, mode="tn", out_dtype=BF16, name=f"ffn_gu_dw_{i}_{j}")
        sg["ffn_norm_w"][i][j] = dnw
        return dxi, (dsh, dsc, dgt)

    def ssm_bwd(dxo, i, sv, sh, sc, gt):
        xin, zx, xbc, dt_raw, y, states, yn, f, h = sv
        df, dgt, _ = gate_bwd(f, dxo, gt, 1.0, name=f"mix_res_bwd_{i}")
        dyn = mm(df, g_full("ssm_w_out", i), mode="nt", name=f"ssm_out_dx_{i}")
        wg["ssm_w_out", i] = mm(yn, df, mode="tn", out_dtype=BF16, name=f"ssm_out_dw_{i}")
        dy, dz, dnorm = gnorm_bwd(y, zx, full["ssm_norm_w"][i][None], dyn, name=f"ssm_gnorm_bwd_{i}")
        dxbc, ddt, dbias, dalog, ddsk = ssd_bwd(
            xbc, dt_raw, ssm_dt_bias[i][None], ssm_a_log[i][None], ssm_d[i][None], states, dy, DI, name=f"ssd_bwd_{i}"
        )
        du, dcw, dcb = conv_bwd(zx, full["ssm_conv_w"][i], full["ssm_conv_b"][i][None], dxbc, DI, name=f"ssm_conv_bwd_{i}")
        dzx = jnp.concatenate([dz, du, ddt], axis=1).astype(BF16)
        dzx4 = jnp.transpose(dzx.reshape(L, N_CHIPS, CI), (1, 0, 2))
        nw = mix_norm_w[i][None]
        dxi, dnw, dsc, dsh = mm_norm_bwd(
            dzx4, gw["ssm_w_in", i].reshape(N_CHIPS, D, CI), xin, nw, sc, dxo, name=f"ssm_in_dx_{i}")
        wg["ssm_w_in", i] = mm(h, dzx4, mode="tn", out_dtype=BF16, name=f"ssm_in_dw_{i}")
        sg["mix_norm_w"][i] = dnw
        sg["ssm_conv_w"][i], sg["ssm_conv_b"][i], sg["ssm_norm_w"][i] = dcw, dcb, dnorm
        sg["ssm_dt_bias"][i], sg["ssm_a_log"][i], sg["ssm_d"][i] = dbias, dalog, ddsk
        return dxi, (dsh, dsc, dgt)

    def att_bwd(dxo, i, sv, dkv, sh, sc, gt):
        l = i - n_a
        xin, q, o, f, h = sv
        df, dgt, dfsum = gate_bwd(f, dxo, gt, 1.0, name=f"mix_res_bwd_{i}")
        do = mm(df, g_full("attn_w_o", l), mode="nt", name=f"att_o_dx_{i}")
        wg["attn_w_o", l] = mm(o, df, mode="tn", out_dtype=BF16, name=f"att_o_dw_{i}")
        dq, dkv, dsink = attn_bwd(q, kv, attn_sinks[l][None], do, dkv, name=f"att_bwd_{i}")
        nw = mix_norm_w[i][None]
        dxi, dnw, dsc, dsh = mm_norm_bwd(dq, g_full("attn_w_q", l), xin, nw, sc, dxo, name=f"att_q_dx_{i}")
        wg["attn_w_q", l] = mm(h, dq, mode="tn", out_dtype=BF16, name=f"att_q_dw_{i}")
        sg["mix_norm_w"][i] = dnw
        sg["attn_b_q"][l], sg["attn_sinks"][l], sg["attn_b_o"][l] = colsum(dq, name=f"att_bq_{i}"), dsink, dfsum
        return dxi, dkv, (dsh, dsc, dgt)

    gfull = {n: lax.empty(_row_halves(W[n]).shape, F32) for n in GATHERED}

    pending, sib = [], []

    def sibling_begin(items):
        parts = []
        for n, m0, a in items:
            m, _, rh, cc = _row_halves(a).shape
            parts.append(wg.pop((n, m0)).reshape(m, N_CHIPS, 2, rh, cc))
        land_shapes = [p.shape[:2] + p.shape[3:] for p in parts]
        tag = f"{items[0][0]}_{items[0][1]}"
        ssem, rsem, srcs, lands, token = split_start(
            _sibling_copies, parts, land_shapes, parts[0], name=f"rs_sibling_start_{tag}", per=1)
        sib.append((tag, [(n, m0) for n, m0, _ in items], ssem, rsem, srcs, lands))
        return token[0:1, 0:1]

    def reduce_step(s, after):
        if pending:
            reduce_end(after)
        items = bwd_stages[s]
        mine, theirs = {}, {}
        while sib:
            tag, keys, ssem, rsem, srcs, lands = sib.pop(0)
            srcs, lands = split_wait(_sibling_copies, ssem, rsem, srcs, lands, after, name=f"rs_sibling_wait_{tag}")
            mine.update(zip(keys, srcs))
            theirs.update(zip(keys, lands))
        parts = [mine[n, m0] for n, m0, _ in items]
        from_sib = [theirs[n, m0] for n, m0, _ in items]
        pair = [rs_add_pair(g, r, name=f"rs_pair_{n}_{m0}") for (n, m0, _), g, r in zip(items, parts, from_sib)]
        land_shapes = [(3, p.shape[0]) + p.shape[2:] for p in pair]
        ssem, rsem, srcs, lands, token = split_start(_reduce_copies, pair, land_shapes, after, name=f"rs_chips_start_{s}")
        pending.append((s, items, parts, from_sib, ssem, rsem, srcs, lands))
        return token[0:1, 0:1]

    def reduce_end(after):
        s, items, parts, from_sib, ssem, rsem, srcs, lands = pending.pop()
        _, from_chips = split_wait(_reduce_copies, ssem, rsem, srcs, lands, after, name=f"rs_chips_wait_{s}")
        for (n, m0, _), g, r, t in zip(items, parts, from_sib, from_chips):
            gfull[n] = rs_add_final(g, r, t, gfull[n], m0, name=f"rs_final_{n}_{m0}")

    dkv = jnp.zeros((L, KVD), F32)
    d_kvnorm = d_kvmod = d_bkv = None
    tok = 0.0
    for i in reversed(range(depth)):
        sh1, sc1, g1, shm, scm, gm, sh2, sc2, g2 = mods[i]
        s1, sm, s2 = saved[i]
        dx, dm2 = ffn_bwd(dx, i, 1, s2, sh2, sc2, g2 + tok)
        tok = sibling_begin(ffn_items(i, 1))
        if i < n_a:
            dx, dmm = ssm_bwd(dx, i, sm, shm, scm, gm + tok)
        else:
            dx, dkv, dmm = att_bwd(dx, i, sm, dkv, shm, scm, gm + tok)
        tok = sibling_begin(mix_items(i))
        if i == 0:
            tok = tok + reduce_step(depth - 1, dx)
        dx, dm1 = ffn_bwd(dx, i, 0, s1, sh1, sc1, g1 + tok)
        tok = sibling_begin(ffn_items(i, 0))
        sg["mod"][i] = jnp.concatenate(list(dm1) + list(dmm) + list(dm2), axis=1)
        if i == n_a:
            d_bkv = colsum(dkv, name="kv_bias_bwd")
            hkv = norm_mod_fwd(x_kv, kv_norm_w[None], kv_scale, kv_shift + tok, name="kv_norm_re")
            wg["w_kv", 0] = mm(hkv, dkv, mode="tn", out_dtype=BF16, name="kv_proj_dw")
            dx, d_kvnorm, dsc, dsh = mm_norm_bwd(
                dkv, g_full("w_kv", 0), x_kv, kv_norm_w[None], kv_scale, dx, name="kv_proj_dx")
            d_kvmod = jnp.concatenate([dsh, dsc], axis=1)
            tok = sibling_begin([("w_kv", 0, w_kv)])
        if i > 0:
            tok = reduce_step(depth - 1 - i, dx)
    grad_x = dx[None]
    grads = {}

    small = {
        "ffn_norm_w": jnp.stack([jnp.stack([r[0] for r in row]) for row in sg["ffn_norm_w"]]),
        "mod_b": jnp.stack([r[0] for r in sg["mod"]]),
        "mix_norm_w": jnp.stack([r[0] for r in sg["mix_norm_w"]]),
        "ssm_conv_w": jnp.stack(sg["ssm_conv_w"]),
        "ssm_conv_b": jnp.stack([r[0] for r in sg["ssm_conv_b"]]),
        "ssm_dt_bias": jnp.stack([r[0] for r in sg["ssm_dt_bias"]]),
        "ssm_a_log": jnp.stack([r[0] for r in sg["ssm_a_log"]]),
        "ssm_d": jnp.stack([r[0] for r in sg["ssm_d"]]),
        "ssm_norm_w": jnp.stack([r[0] for r in sg["ssm_norm_w"]]),
        "kv_norm_w": d_kvnorm[0],
        "kv_mod_b": d_kvmod[0],
        "b_kv": d_bkv[0],
        "attn_b_q": jnp.stack([r[0] for r in sg["attn_b_q"]]),
        "attn_sinks": jnp.stack([r[0] for r in sg["attn_sinks"]]),
        "attn_b_o": jnp.stack([r[0] for r in sg["attn_b_o"]]),
        "final_norm_w": d_final[0],
    }
    small_like = [small[n].shape for n in SMALL]
    sv_pack, sv_n = _pack([small[n] for n in SMALL])
    sv_all = ag8(sv_pack + tok, name="ag_small_g")
    sv_all = sv_all + reduce_step(depth, sv_all)
    sv_sum = reduce8(sv_all, name="small_g_sum").reshape(-1)[:sv_n]
    for n, gsum in zip(SMALL, _unpack(sv_sum, small_like)):
        grads[n] = chip_cols(gsum, W[n].shape[-1]) if n in SMALL_SHARDED else gsum

    per_dev = [_unpack(sv_all[b].reshape(-1)[:sv_n], small_like) for b in range(N_DEV)]
    i_modb, i_kvb = SMALL.index("mod_b"), SMALL.index("kv_mod_b")
    dmod_all = jnp.stack([chip_cols(p[i_modb], MW) for p in per_dev], axis=1)
    dkv_all = jnp.stack([chip_cols(p[i_kvb], KW) for p in per_dev], axis=0)[None]
    c_t = jnp.transpose(c_all)
    grads["mod_w"] = outer8(c_t, dmod_all, name="mod_w_grad")
    grads["kv_mod_w"] = outer8(c_t, dkv_all, name="kv_mod_w_grad")[0]

    delta, new_m, new_v = {}, {}, {}
    for n in COLUMN_PARALLEL:
        delta[n], new_m[n], new_v[n] = adamw(W[n], grads[n], MOM[n], VAR[n], name=f"adamw_{n}")
    like = [W[n].shape for n in SMALL]
    packs = [_pack([d[n] for n in SMALL])[0] for d in (W, grads, MOM, VAR)]
    n_small = sum(int(W[n].size) for n in SMALL)
    for dst, res in zip((delta, new_m, new_v), adamw(*packs, name="adamw_small")):
        for n, a in zip(SMALL, _unpack(res.reshape(-1)[:n_small], like)):
            dst[n] = a

    reduce_end(delta["mod_w"])
    for n, s in zip(GATHERED, rs_share([gfull[n] for n in GATHERED], name="rs_share")):
        grads[n] = s.reshape(W[n].shape)
        delta[n], new_m[n], new_v[n] = adamw(W[n], grads[n], MOM[n], VAR[n], name=f"adamw_{n}")

    return (loss, grad_x, *[grads[n] for n in WEIGHTS], *[delta[n] for n in WEIGHTS], *[new_m[n] for n in WEIGHTS],
            *[new_v[n] for n in WEIGHTS])
```

```python
import functools

import jax
import jax.numpy as jnp
from jax import lax
from jax.experimental import pallas as pl
from jax.experimental.pallas import tpu as pltpu

F32 = jnp.float32
BF16 = jnp.bfloat16
HIGHEST = lax.Precision.HIGHEST
MESH = pl.DeviceIdType.MESH

EPS = 1e-5
N_MOD = 9
FFN_HALF = 0.5
SSM_HEADDIM = 64
SSM_GROUPS = 8
SSM_STATE = 128
CONV_WIDTH = 4
CHUNK = 128
KV_HEADS = 4
HEAD_DIM = 64
WINDOW = 128
N_CHIPS = 4
N_DEV = 8

ADAM_LR = 0.001
ADAM_B1 = 0.9
ADAM_B2 = 0.999
ADAM_EPS = 1e-08
ADAM_WD = 0.01
ADAM_STEP = 10

LANE = 128
MM_TILE = 1024


def _pick(n, pref, align, whole_if_small=False):
    best = 0
    t = align
    while t <= min(n, pref):
        if n % t == 0:
            best = t
        t += align
    if best == 0 or (whole_if_small and best < 256 and n <= 2048):
        return n
    return best


def _sigmoid(x):
    return 1.0 / (1.0 + jnp.exp(-x))


def _silu(x):
    return x * _sigmoid(x)


def _dsilu(x):
    s = _sigmoid(x)
    return s * (1.0 + x * (1.0 - s))


def _params(*sem):
    return pltpu.CompilerParams(dimension_semantics=sem)


def mm(a, b, *, mode="nn", reduce_s=False, out_dtype=F32, bias=None, resid=None, name):
    a_s = a.ndim == 3
    b_s = b.ndim == 3
    S = a.shape[0] if a_s else (b.shape[0] if b_s else 1)
    a2 = a.shape[-2:]
    b2 = b.shape[-2:]
    if mode == "nn":
        (M, K), (K2, N) = a2, b2
    elif mode == "nt":
        (M, K), (N, K2) = a2, b2
    else:
        (K, M), (K2, N) = a2, b2
    assert K == K2, (a.shape, b.shape, mode)
    batch = (a_s or b_s) and not reduce_s
    sb = S if batch else 1
    sr = S if ((a_s or b_s) and reduce_s) else 1
    tm = _pick(M, MM_TILE // 2 if resid is not None else MM_TILE, LANE if mode == "tn" else 16, True)
    tn = _pick(N, MM_TILE, LANE, True)
    tk = _pick(K, 2 * MM_TILE if mode == "tn" else MM_TILE, LANE if mode != "tn" else 16, True)
    nk = K // tk
    grid = (sb, M // tm, N // tn, sr, nk)

    def s_of(isb, isr):
        return isb if batch else isr

    def a_map(isb, i, j, isr, k):
        idx = (k, i) if mode == "tn" else (i, k)
        return ((s_of(isb, isr),) + idx) if a_s else idx

    def b_map(isb, i, j, isr, k):
        idx = (j, k) if mode == "nt" else (k, j)
        return ((s_of(isb, isr),) + idx) if b_s else idx

    def o_map(isb, i, j, isr, k):
        return (isb, i, j) if batch else (i, j)

    def s_blk(has_s, blk):
        return ((None,) + blk) if has_s else blk

    a_blk = (tk, tm) if mode == "tn" else (tm, tk)
    b_blk = (tn, tk) if mode == "nt" else (tk, tn)
    in_specs = [pl.BlockSpec(s_blk(a_s, a_blk), a_map), pl.BlockSpec(s_blk(b_s, b_blk), b_map)]
    args = [a, b]
    if bias is not None:
        bias_s = bias.ndim == 3
        in_specs.append(
            pl.BlockSpec(
                ((None, 1, tn) if bias_s else (1, tn)),
                (lambda isb, i, j, isr, k: (isb, 0, j)) if bias_s else (lambda isb, i, j, isr, k: (0, j)),
            )
        )
        args.append(bias)
    o_spec = pl.BlockSpec(s_blk(batch, (tm, tn)), o_map)
    out_shape = jax.ShapeDtypeStruct(((sb, M, N) if batch else (M, N)), out_dtype)
    out_specs = o_spec
    if resid is not None:
        assert not batch
        x_res, gate, scale = resid
        in_specs += [pl.BlockSpec((tm, tn), o_map), pl.BlockSpec((1, tn), lambda isb, i, j, isr, k: (0, j))]
        args += [x_res, gate]
        out_shape = (out_shape, jax.ShapeDtypeStruct((M, N), F32))
        out_specs = (o_spec, pl.BlockSpec((tm, tn), o_map))
    dims = {"nn": (((1,), (0,)), ((), ())), "nt": (((1,), (1,)), ((), ())), "tn": (((0,), (0,)), ((), ()))}[mode]
    n_in = len(args)
    n_out = 2 if resid is not None else 1
    one_step = sr * nk == 1

    def body(*refs):
        a_ref, b_ref = refs[0], refs[1]
        bias_ref = refs[2] if bias is not None else None
        o_ref = refs[n_in]

        def finish(r):
            if bias is not None:
                r = r + bias_ref[...]
            o_ref[...] = r.astype(o_ref.dtype)
            if resid is not None:
                refs[n_in + 1][...] = refs[n_in - 2][...] + (scale * refs[n_in - 1][...]) * r

        def part():
            return lax.dot_general(a_ref[...].astype(BF16), b_ref[...].astype(BF16), dims, preferred_element_type=F32)

        if one_step:
            finish(part())
            return
        acc = refs[n_in + n_out]
        isr = pl.program_id(3)
        k = pl.program_id(4)

        @pl.when((isr == 0) & (k == 0))
        def _():
            acc[...] = jnp.zeros_like(acc)

        acc[...] += part()

        @pl.when((isr == sr - 1) & (k == nk - 1))
        def _():
            finish(acc[...])

    return pl.pallas_call(
        body,
        out_shape=out_shape,
        grid=grid,
        in_specs=in_specs,
        out_specs=out_specs,
        scratch_shapes=[] if one_step else [pltpu.VMEM((tm, tn), F32)],
        compiler_params=_params("parallel", "parallel", "parallel", "arbitrary", "arbitrary"),
        name=name,
    )(*args)


def norm_mod_fwd(x, nw, sc, sh, *, name):
    L, D = x.shape
    tl = _pick(L, 512, 16)

    def body(x_ref, nw_ref, sc_ref, sh_ref, h_ref):
        xv = x_ref[...]
        r = lax.rsqrt(jnp.mean(xv * xv, axis=-1, keepdims=True) + EPS)
        n = (xv * r) * nw_ref[...]
        h_ref[...] = (n * (1.0 + sc_ref[...]) + sh_ref[...]).astype(h_ref.dtype)

    row = pl.BlockSpec((1, D), lambda i: (0, 0))
    return pl.pallas_call(
        body,
        out_shape=jax.ShapeDtypeStruct((L, D), BF16),
        grid=(L // tl,),
        in_specs=[pl.BlockSpec((tl, D), lambda i: (i, 0)), row, row, row],
        out_specs=pl.BlockSpec((tl, D), lambda i: (i, 0)),
        compiler_params=_params("parallel"),
        name=name,
    )(x, nw, sc, sh)


def mm_norm_bwd(a, b, x, nw, sc, dx_in, *, name):
    has_s = a.ndim == 3
    assert has_s == (b.ndim == 3)
    sr = a.shape[0] if has_s else 1
    M, K = a.shape[-2:]
    D = b.shape[-2]
    tm = _pick(M, MM_TILE // 2, 16, True)
    tk = _pick(K, MM_TILE, LANE, True)
    nk = K // tk

    def body(a_ref, b_ref, x_ref, nw_ref, sc_ref, dxi_ref, dx_ref, dnw_ref, dsc_ref, dsh_ref, acc):
        i, s, k = pl.program_id(0), pl.program_id(1), pl.program_id(2)

        @pl.when((i == 0) & (s == 0) & (k == 0))
        def _():
            dnw_ref[...] = jnp.zeros_like(dnw_ref)
            dsc_ref[...] = jnp.zeros_like(dsc_ref)
            dsh_ref[...] = jnp.zeros_like(dsh_ref)

        @pl.when((s == 0) & (k == 0))
        def _():
            acc[...] = jnp.zeros_like(acc)

        acc[...] += lax.dot_general(
            a_ref[...].astype(BF16), b_ref[...].astype(BF16), (((1,), (1,)), ((), ())), preferred_element_type=F32
        )

        @pl.when((s == sr - 1) & (k == nk - 1))
        def _():
            dh_v = acc[...]
            xv = x_ref[...]
            r = lax.rsqrt(jnp.mean(xv * xv, axis=-1, keepdims=True) + EPS)
            xhat = xv * r
            nw_v = nw_ref[...]
            n = xhat * nw_v
            dsh_ref[...] += jnp.sum(dh_v, axis=0, keepdims=True)
            dsc_ref[...] += jnp.sum(dh_v * n, axis=0, keepdims=True)
            dn = dh_v * (1.0 + sc_ref[...])
            dnw_ref[...] += jnp.sum(dn * xhat, axis=0, keepdims=True)
            dxhat = dn * nw_v
            dx_ref[...] = dxi_ref[...] + r * (dxhat - xhat * jnp.mean(dxhat * xhat, axis=-1, keepdims=True))

    row = pl.BlockSpec((1, D), lambda i, s, k: (0, 0))
    tile = pl.BlockSpec((tm, D), lambda i, s, k: (i, 0))
    vec = jax.ShapeDtypeStruct((1, D), F32)
    return pl.pallas_call(
        body,
        out_shape=(jax.ShapeDtypeStruct((M, D), F32), vec, vec, vec),
        grid=(M // tm, sr, nk),
        in_specs=[
            pl.BlockSpec((None, tm, tk) if has_s else (tm, tk), (lambda i, s, k: (s, i, k)) if has_s else (lambda i, s, k: (i, k))),
            pl.BlockSpec((None, D, tk) if has_s else (D, tk), (lambda i, s, k: (s, 0, k)) if has_s else (lambda i, s, k: (0, k))),
            tile, row, row, tile,
        ],
        out_specs=(tile, row, row, row),
        scratch_shapes=[pltpu.VMEM((tm, D), F32)],
        compiler_params=_params("arbitrary", "arbitrary", "arbitrary"),
        name=name,
    )(a, b, x, nw, sc, dx_in)


def gate_bwd(f, dx, gate, scale, *, name):
    L, D = f.shape
    tl = _pick(L, 512, 16)

    def body(f_ref, dx_ref, g_ref, df_ref, dg_ref, dfsum_ref):
        @pl.when(pl.program_id(0) == 0)
        def _():
            dg_ref[...] = jnp.zeros_like(dg_ref)
            dfsum_ref[...] = jnp.zeros_like(dfsum_ref)

        dxv = dx_ref[...]
        df = (scale * g_ref[...]) * dxv
        df_ref[...] = df.astype(df_ref.dtype)
        dfsum_ref[...] += jnp.sum(df, axis=0, keepdims=True)
        dg_ref[...] += scale * jnp.sum(f_ref[...] * dxv, axis=0, keepdims=True)

    tile = pl.BlockSpec((tl, D), lambda i: (i, 0))
    row = pl.BlockSpec((1, D), lambda i: (0, 0))
    vec = jax.ShapeDtypeStruct((1, D), F32)
    return pl.pallas_call(
        body,
        out_shape=(jax.ShapeDtypeStruct((L, D), BF16), vec, vec),
        grid=(L // tl,),
        in_specs=[tile, tile, row],
        out_specs=(tile, row, row),
        compiler_params=_params("arbitrary"),
        name=name,
    )(f, dx, gate)


def colsum(x, *, name):
    L, N = x.shape
    tl = _pick(L, 512, 8)

    def body(x_ref, o_ref):
        @pl.when(pl.program_id(0) == 0)
        def _():
            o_ref[...] = jnp.zeros_like(o_ref)

        o_ref[...] += jnp.sum(x_ref[...], axis=0, keepdims=True)

    return pl.pallas_call(
        body,
        out_shape=jax.ShapeDtypeStruct((1, N), F32),
        grid=(L // tl,),
        in_specs=[pl.BlockSpec((tl, N), lambda i: (i, 0))],
        out_specs=pl.BlockSpec((1, N), lambda i: (0, 0)),
        compiler_params=_params("arbitrary"),
        name=name,
    )(x)


def ffn_up(h, wgu, *, name):
    L, D = h.shape
    T = wgu.shape[-1]
    tm = _pick(L, 512, 16)

    def body(h_ref, w_ref, gu_ref, a_ref):
        hb = h_ref[...].astype(BF16)
        g = jnp.dot(hb, w_ref[0].astype(BF16), preferred_element_type=F32)
        u = jnp.dot(hb, w_ref[1].astype(BF16), preferred_element_type=F32)
        gu_ref[0] = g
        gu_ref[1] = u
        a_ref[...] = (_silu(g) * u).astype(a_ref.dtype)

    gu, a = pl.pallas_call(
        body,
        out_shape=(jax.ShapeDtypeStruct((2, 2, L, T), F32), jax.ShapeDtypeStruct((2, L, T), BF16)),
        grid=(2, L // tm),
        in_specs=[
            pl.BlockSpec((tm, D), lambda j, i: (i, 0)),
            pl.BlockSpec((2, None, D, T), lambda j, i: (0, j, 0, 0)),
        ],
        out_specs=(
            pl.BlockSpec((2, None, tm, T), lambda j, i: (0, j, i, 0)),
            pl.BlockSpec((None, tm, T), lambda j, i: (j, i, 0)),
        ),
        compiler_params=_params("parallel", "parallel"),
        name=name,
    )(h, wgu.reshape(2, 2, D, T))
    return gu.reshape(4, L, T), a


def ffn_down_bwd(df, wdn, gu, *, name):
    L, D = df.shape
    T = wdn.shape[1]
    tm = _pick(L, 512, 16)

    def body(df_ref, w_ref, gu_ref, d_ref):
        da = lax.dot_general(
            df_ref[...].astype(BF16), w_ref[...].astype(BF16), (((1,), (1,)), ((), ())), preferred_element_type=F32
        )
        g = gu_ref[0]
        d_ref[0] = (da * gu_ref[1] * _dsilu(g)).astype(d_ref.dtype)
        d_ref[1] = (da * _silu(g)).astype(d_ref.dtype)

    out = pl.pallas_call(
        body,
        out_shape=jax.ShapeDtypeStruct((2, 2, L, T), BF16),
        grid=(2, L // tm),
        in_specs=[
            pl.BlockSpec((tm, D), lambda j, i: (i, 0)),
            pl.BlockSpec((None, T, D), lambda j, i: (j, 0, 0)),
            pl.BlockSpec((2, None, tm, T), lambda j, i: (0, j, i, 0)),
        ],
        out_specs=pl.BlockSpec((2, None, tm, T), lambda j, i: (0, j, i, 0)),
        compiler_params=_params("parallel", "parallel"),
        name=name,
    )(df, wdn, gu.reshape(2, 2, L, T))
    return out.reshape(4, L, T)


def _shift_down(u, k, rows):
    if k == 0:
        return u
    return jnp.where(rows >= k, pltpu.roll(u, k, 0), 0.0)


def _shift_up(u, k, rows, n):
    if k == 0:
        return u
    return jnp.where(rows < n - k, pltpu.roll(u, n - k, 0), 0.0)


def _conv_pre(u, w_ref, b_ref, rows):
    pre = b_ref[...] + w_ref[CONV_WIDTH - 1 : CONV_WIDTH, :] * u
    for k in range(1, CONV_WIDTH):
        pre = pre + w_ref[CONV_WIDTH - 1 - k : CONV_WIDTH - k, :] * _shift_down(u, k, rows)
    return pre


def conv_fwd(zx, conv_w, conv_b, d_inner, *, name):
    L = zx.shape[0]
    C = conv_w.shape[1]
    tc = 256
    off = d_inner // tc

    def body(u_ref, w_ref, b_ref, o_ref):
        rows = lax.broadcasted_iota(jnp.int32, (L, tc), 0)
        o_ref[...] = _silu(_conv_pre(u_ref[...], w_ref, b_ref, rows))

    return pl.pallas_call(
        body,
        out_shape=jax.ShapeDtypeStruct((L, C), F32),
        grid=(C // tc,),
        in_specs=[
            pl.BlockSpec((L, tc), lambda j: (0, off + j)),
            pl.BlockSpec((CONV_WIDTH, tc), lambda j: (0, j)),
            pl.BlockSpec((1, tc), lambda j: (0, j)),
        ],
        out_specs=pl.BlockSpec((L, tc), lambda j: (0, j)),
        compiler_params=_params("parallel"),
        name=name,
    )(zx, conv_w, conv_b)


def conv_bwd(zx, conv_w, conv_b, dxbc, d_inner, *, name):
    L = zx.shape[0]
    C = conv_w.shape[1]
    tc = 256
    off = d_inner // tc

    def body(u_ref, w_ref, b_ref, d_ref, du_ref, dw_ref, db_ref):
        rows = lax.broadcasted_iota(jnp.int32, (L, tc), 0)
        u = u_ref[...]
        dpre = d_ref[...] * _dsilu(_conv_pre(u, w_ref, b_ref, rows))
        db_ref[...] = jnp.sum(dpre, axis=0, keepdims=True)
        du = w_ref[CONV_WIDTH - 1 : CONV_WIDTH, :] * dpre
        dw_ref[CONV_WIDTH - 1 : CONV_WIDTH, :] = jnp.sum(dpre * u, axis=0, keepdims=True)
        for k in range(1, CONV_WIDTH):
            j = CONV_WIDTH - 1 - k
            dw_ref[j : j + 1, :] = jnp.sum(dpre * _shift_down(u, k, rows), axis=0, keepdims=True)
            du = du + w_ref[j : j + 1, :] * _shift_up(dpre, k, rows, L)
        du_ref[...] = du

    return pl.pallas_call(
        body,
        out_shape=(
            jax.ShapeDtypeStruct((L, C), F32),
            jax.ShapeDtypeStruct((CONV_WIDTH, C), F32),
            jax.ShapeDtypeStruct((1, C), F32),
        ),
        grid=(C // tc,),
        in_specs=[
            pl.BlockSpec((L, tc), lambda j: (0, off + j)),
            pl.BlockSpec((CONV_WIDTH, tc), lambda j: (0, j)),
            pl.BlockSpec((1, tc), lambda j: (0, j)),
            pl.BlockSpec((L, tc), lambda j: (0, j)),
        ],
        out_specs=(
            pl.BlockSpec((L, tc), lambda j: (0, j)),
            pl.BlockSpec((CONV_WIDTH, tc), lambda j: (0, j)),
            pl.BlockSpec((1, tc), lambda j: (0, j)),
        ),
        compiler_params=_params("parallel"),
        name=name,
    )(zx, conv_w, conv_b, dxbc)


def _ssd_head(xs, dt, acs, tot, dsk, cb, bm, cm, prev):
    q = xs.shape[0]
    li = lax.broadcasted_iota(jnp.int32, (q, q), 0)
    si = lax.broadcasted_iota(jnp.int32, (q, q), 1)
    causal = li >= si
    lmat = jnp.exp(jnp.where(causal, acs - acs.T, -jnp.inf))
    xdt = xs * dt
    y = jnp.dot((cb * lmat).astype(BF16), xdt.astype(BF16), preferred_element_type=F32)
    y = y + lax.dot_general(
        (cm * jnp.exp(acs)).astype(BF16), prev.astype(BF16), (((1,), (1,)), ((), ())), preferred_element_type=F32
    )
    y = y + dsk * xs
    st = lax.dot_general(
        xdt.astype(BF16), (bm * jnp.exp(tot - acs)).astype(BF16), (((0,), (0,)), ((), ())), preferred_element_type=F32
    )
    return y, prev * jnp.exp(tot) + st


def _pick_lane(v, h):
    lanes = lax.broadcasted_iota(jnp.int32, v.shape, 1)
    return jnp.sum(jnp.where(lanes == h, v, 0.0), axis=1, keepdims=True)


def _tri_cols(cols, upper):
    q = cols[0].shape[0]
    assert 3 * len(cols) <= LANE
    li = lax.broadcasted_iota(jnp.int32, (q, q), 0)
    si = lax.broadcasted_iota(jnp.int32, (q, q), 1)
    tri = ((li <= si) if upper else (li >= si)).astype(BF16)
    lanes = lax.broadcasted_iota(jnp.int32, (q, LANE), 1)
    rhs = jnp.zeros((q, LANE), F32)
    for r, col in enumerate(cols):
        hi = col.astype(BF16).astype(F32)
        mid = (col - hi).astype(BF16).astype(F32)
        lo = col - hi - mid
        for t, term in enumerate((hi, mid, lo)):
            rhs = jnp.where(lanes == 3 * r + t, term, rhs)
    out = jnp.dot(tri, rhs.astype(BF16), preferred_element_type=F32)
    return [jnp.sum(jnp.where((lanes >= 3 * r) & (lanes < 3 * r + 3), out, 0.0), axis=1, keepdims=True)
            for r in range(len(cols))]


def _softplus(x):
    return jnp.maximum(x, 0.0) + jnp.log(1.0 + jnp.exp(-jnp.abs(x)))


def _ssd_specs(L, d_inner, H, nc, rev):
    R = H // SSM_GROUPS
    P, N, Q = SSM_HEADDIM, SSM_STATE, CHUNK
    ngrp = SSM_GROUPS

    def ci(c):
        return (nc - 1 - c) if rev else c

    b_off = d_inner // N
    c_off = b_off + ngrp
    xs = pl.BlockSpec((Q, R * P), lambda c, g: (ci(c), g))
    bm = pl.BlockSpec((Q, N), lambda c, g: (ci(c), b_off + g))
    cm = pl.BlockSpec((Q, N), lambda c, g: (ci(c), c_off + g))
    dt = pl.BlockSpec((Q, H), lambda c, g: (ci(c), 0))
    hv = pl.BlockSpec((1, H), lambda c, g: (0, 0))
    y = pl.BlockSpec((Q, R * P), lambda c, g: (ci(c), g))
    st = pl.BlockSpec((None, R * P, N), lambda c, g: (ci(c), g, 0))
    return R, xs, bm, cm, dt, hv, y, st


def ssd_fwd(xbc, dt_raw, dt_bias, a_log, d_skip, d_inner, *, name):
    L = xbc.shape[0]
    H = dt_raw.shape[1]
    nc = L // CHUNK
    P, N = SSM_HEADDIM, SSM_STATE
    R, xs_s, bm_s, cm_s, dt_s, hv_s, y_s, st_s = _ssd_specs(L, d_inner, H, nc, False)

    def body(xs_ref, bm_ref, cm_ref, dt_ref, bias_ref, alog_ref, dsk_ref, y_ref, st_ref, state):
        c = pl.program_id(0)
        g = pl.program_id(1)

        @pl.when(c == 0)
        def _():
            for r in range(R):
                state[g * R + r] = jnp.zeros((P, N), F32)

        dtb = _softplus(dt_ref[...] + bias_ref[...])
        a_all = -jnp.exp(alog_ref[...])
        bm, cm = bm_ref[...], cm_ref[...]
        cb = lax.dot_general(cm.astype(BF16), bm.astype(BF16), (((1,), (1,)), ((), ())), preferred_element_type=F32)
        dts = [_pick_lane(dtb, g * R + r) for r in range(R)]
        a_cols = [dts[r] * _pick_lane(a_all, g * R + r) for r in range(R)]
        acs = _tri_cols(a_cols, upper=False)
        prevs = [state[g * R + r] for r in range(R)]
        res = []
        for r in range(R):
            res.append(_ssd_head(
                xs_ref[:, r * P : (r + 1) * P],
                dts[r],
                jnp.broadcast_to(acs[r], (CHUNK, CHUNK)),
                jnp.sum(a_cols[r], axis=0, keepdims=True),
                _pick_lane(dsk_ref[...], g * R + r),
                cb,
                bm,
                cm,
                prevs[r],
            ))
        for r in range(R):
            st_ref[r * P : (r + 1) * P, :] = prevs[r]
            y_ref[:, r * P : (r + 1) * P] = res[r][0]
            state[g * R + r] = res[r][1]

    return pl.pallas_call(
        body,
        out_shape=(jax.ShapeDtypeStruct((L, d_inner), F32), jax.ShapeDtypeStruct((nc, H * P, N), F32)),
        grid=(nc, SSM_GROUPS),
        in_specs=[xs_s, bm_s, cm_s, dt_s, hv_s, hv_s, hv_s],
        out_specs=(y_s, st_s),
        scratch_shapes=[pltpu.VMEM((H, P, N), F32)],
        compiler_params=_params("arbitrary", "arbitrary"),
        name=name,
    )(xbc, xbc, xbc, dt_raw, dt_bias, a_log, d_skip)


def ssd_bwd(xbc, dt_raw, dt_bias, a_log, d_skip, states, dy, d_inner, *, name):
    L, C = xbc.shape
    H = dt_raw.shape[1]
    nc = L // CHUNK
    P, N, Q = SSM_HEADDIM, SSM_STATE, CHUNK
    R, xs_s, bm_s, cm_s, dt_s, hv_s, y_s, st_s = _ssd_specs(L, d_inner, H, nc, True)

    def body(xs_ref, bm_ref, cm_ref, dt_ref, bias_ref, alog_ref, dsk_ref, st_ref, dy_ref,
             dxs_ref, dbm_ref, dcm_ref, ddt_ref, dbias_ref, dalog_ref, ddsk_ref, dstate):
        c = pl.program_id(0)
        g = pl.program_id(1)

        @pl.when(c == 0)
        def _():
            for r in range(R):
                dstate[g * R + r] = jnp.zeros((P, N), F32)

        @pl.when((c == 0) & (g == 0))
        def _():
            dbias_ref[...] = jnp.zeros_like(dbias_ref)
            dalog_ref[...] = jnp.zeros_like(dalog_ref)
            ddsk_ref[...] = jnp.zeros_like(ddsk_ref)

        @pl.when(g == 0)
        def _():
            ddt_ref[...] = jnp.zeros_like(ddt_ref)

        pre = dt_ref[...] + bias_ref[...]
        dtb = _softplus(pre)
        a_all = -jnp.exp(alog_ref[...])
        lanes_q = lax.broadcasted_iota(jnp.int32, (Q, H), 1)
        lanes_1 = lax.broadcasted_iota(jnp.int32, (1, H), 1)
        bm = bm_ref[...]
        cm = cm_ref[...]
        nt = (((1,), (1,)), ((), ()))
        cb = lax.dot_general(cm.astype(BF16), bm.astype(BF16), nt, preferred_element_type=F32)
        dts = [_pick_lane(dtb, g * R + r) for r in range(R)]
        a_negs = [_pick_lane(a_all, g * R + r) for r in range(R)]
        a_cols = [dts[r] * a_negs[r] for r in range(R)]
        acs = _tri_cols(a_cols, upper=False)
        dbm = jnp.zeros((Q, N), F32)
        dcm = jnp.zeros((Q, N), F32)
        dcb = jnp.zeros((Q, Q), F32)
        dd_row = jnp.zeros((1, H), F32)
        dstates = [dstate[g * R + r] for r in range(R)]
        dprevs, ddts, dacs_cols, dtots = [], [], [], []
        for r in range(R):
            h = g * R + r
            args = (
                xs_ref[:, r * P : (r + 1) * P],
                dts[r],
                jnp.broadcast_to(acs[r], (Q, Q)),
                jnp.sum(a_cols[r], axis=0, keepdims=True),
                _pick_lane(dsk_ref[...], h),
                cb,
                bm,
                cm,
                st_ref[r * P : (r + 1) * P, :],
            )
            _, vjp = jax.vjp(_ssd_head, *args)
            dxs, ddt, dacs, dtot, dd, dcb_h, dbm_h, dcm_h, dprev = vjp((dy_ref[:, r * P : (r + 1) * P], dstates[r]))
            dxs_ref[:, r * P : (r + 1) * P] = dxs
            dprevs.append(dprev)
            ddts.append(ddt)
            dacs_cols.append(jnp.sum(dacs, axis=1, keepdims=True))
            dtots.append(dtot)
            dbm = dbm + dbm_h
            dcm = dcm + dcm_h
            dcb = dcb + dcb_h
            dd_row = dd_row + jnp.where(lanes_1 == h, dd, 0.0)
        for r in range(R):
            dstate[g * R + r] = dprevs[r]
        ddt_blk = jnp.zeros((Q, H), F32)
        da_row = jnp.zeros((1, H), F32)
        for r, da_col in enumerate(_tri_cols(dacs_cols, upper=True)):
            h = g * R + r
            da_col = da_col + dtots[r]
            ddt_blk = ddt_blk + jnp.where(lanes_q == h, ddts[r] + da_col * a_negs[r], 0.0)
            da_row = da_row + jnp.where(lanes_1 == h, jnp.sum(da_col * dts[r], axis=0, keepdims=True), 0.0)
        dcb16 = dcb.astype(BF16)
        dbm_ref[...] = dbm + lax.dot_general(dcb16, cm.astype(BF16), (((0,), (0,)), ((), ())), preferred_element_type=F32)
        dcm_ref[...] = dcm + jnp.dot(dcb16, bm.astype(BF16), preferred_element_type=F32)
        ddt_pre = ddt_blk * _sigmoid(pre)
        ddt_ref[...] += ddt_pre
        dbias_ref[...] += jnp.sum(ddt_pre, axis=0, keepdims=True)
        dalog_ref[...] += da_row * a_all
        ddsk_ref[...] += dd_row

    ngrp = SSM_GROUPS
    hrow = jax.ShapeDtypeStruct((1, H), F32)
    dxs, dbm, dcm, ddt, dbias, dalog, ddsk = pl.pallas_call(
        body,
        out_shape=(
            jax.ShapeDtypeStruct((L, d_inner), F32),
            jax.ShapeDtypeStruct((L, ngrp * N), F32),
            jax.ShapeDtypeStruct((L, ngrp * N), F32),
            jax.ShapeDtypeStruct((L, H), F32),
            hrow,
            hrow,
            hrow,
        ),
        grid=(nc, ngrp),
        in_specs=[xs_s, bm_s, cm_s, dt_s, hv_s, hv_s, hv_s, st_s, y_s],
        out_specs=(
            y_s,
            pl.BlockSpec((Q, N), lambda c, g: (nc - 1 - c, g)),
            pl.BlockSpec((Q, N), lambda c, g: (nc - 1 - c, g)),
            dt_s,
            hv_s,
            hv_s,
            hv_s,
        ),
        scratch_shapes=[pltpu.VMEM((H, P, N), F32)],
        compiler_params=_params("arbitrary", "arbitrary"),
        name=name,
    )(xbc, xbc, xbc, dt_raw, dt_bias, a_log, d_skip, states, dy)
    return jnp.concatenate([dxs, dbm, dcm], axis=1), ddt, dbias, dalog, ddsk


def gnorm_fwd(y, zx, nw, *, name):
    L, DI = y.shape
    gw = DI // SSM_GROUPS
    tl = _pick(L, 512, 16)

    def body(y_ref, z_ref, nw_ref, o_ref):
        yz = y_ref[...] * _silu(z_ref[...])
        r = lax.rsqrt(jnp.mean(yz * yz, axis=-1, keepdims=True) + EPS)
        o_ref[...] = ((yz * r) * nw_ref[...]).astype(o_ref.dtype)

    tile = pl.BlockSpec((tl, gw), lambda i, g: (i, g))
    return pl.pallas_call(
        body,
        out_shape=jax.ShapeDtypeStruct((L, DI), BF16),
        grid=(L // tl, SSM_GROUPS),
        in_specs=[tile, tile, pl.BlockSpec((1, gw), lambda i, g: (0, g))],
        out_specs=tile,
        compiler_params=_params("parallel", "parallel"),
        name=name,
    )(y, zx, nw)


def gnorm_bwd(y, zx, nw, dout, *, name):
    L, DI = y.shape
    gw = DI // SSM_GROUPS
    tl = _pick(L, 512, 16)

    def body(y_ref, z_ref, nw_ref, do_ref, dy_ref, dz_ref, dnw_ref):
        @pl.when(pl.program_id(1) == 0)
        def _():
            dnw_ref[...] = jnp.zeros_like(dnw_ref)

        yv = y_ref[...]
        zv = z_ref[...]
        sz = _silu(zv)
        yz = yv * sz
        r = lax.rsqrt(jnp.mean(yz * yz, axis=-1, keepdims=True) + EPS)
        n = yz * r
        dov = do_ref[...]
        dnw_ref[...] += jnp.sum(dov * n, axis=0, keepdims=True)
        dn = dov * nw_ref[...]
        dyz = r * (dn - n * jnp.mean(dn * n, axis=-1, keepdims=True))
        dy_ref[...] = dyz * sz
        dz_ref[...] = dyz * yv * _dsilu(zv)

    tile = pl.BlockSpec((tl, gw), lambda g, i: (i, g))
    row = pl.BlockSpec((1, gw), lambda g, i: (0, g))
    return pl.pallas_call(
        body,
        out_shape=(
            jax.ShapeDtypeStruct((L, DI), F32),
            jax.ShapeDtypeStruct((L, DI), F32),
            jax.ShapeDtypeStruct((1, DI), F32),
        ),
        grid=(SSM_GROUPS, L // tl),
        in_specs=[tile, tile, row, tile],
        out_specs=(tile, tile, row),
        compiler_params=_params("parallel", "arbitrary"),
        name=name,
    )(y, zx, nw, dout)


def _attn_head(q, kp, kc, vp, vc, sink, has_prev):
    rows, w = q.shape[0], kc.shape[0]
    nt = (((1,), (1,)), ((), ()))
    qb = q.astype(BF16)
    sc = lax.dot_general(qb, kc.astype(BF16), nt, preferred_element_type=F32) * HEAD_DIM ** -0.5
    sp = lax.dot_general(qb, kp.astype(BF16), nt, preferred_element_type=F32) * HEAD_DIM ** -0.5
    ii = jnp.bitwise_and(lax.broadcasted_iota(jnp.int32, (rows, w), 0), w - 1)
    jj = lax.broadcasted_iota(jnp.int32, (rows, w), 1)
    lc = jnp.where(jj <= ii, sc, -jnp.inf)
    lp = jnp.where((jj > ii) & has_prev, sp, -jnp.inf)
    m = jnp.maximum(jnp.maximum(jnp.max(lc, axis=1, keepdims=True), jnp.max(lp, axis=1, keepdims=True)), sink)
    m = lax.stop_gradient(m)
    pc = jnp.exp(lc - m)
    pp = jnp.exp(lp - m)
    denom = jnp.sum(pc, axis=1, keepdims=True) + jnp.sum(pp, axis=1, keepdims=True) + jnp.exp(sink - m)
    o = jnp.dot((pc / denom).astype(BF16), vc.astype(BF16), preferred_element_type=F32)
    return o + jnp.dot((pp / denom).astype(BF16), vp.astype(BF16), preferred_element_type=F32)


def attn_fwd(q, kv, sinks, *, name):
    L, DQ = q.shape
    heads = DQ // HEAD_DIM
    rep = heads // KV_HEADS
    nb = L // WINDOW
    kw = KV_HEADS * HEAD_DIM
    W, HD = WINDOW, HEAD_DIM

    def body(q_ref, kp_ref, kc_ref, vp_ref, vc_ref, s_ref, o_ref):
        has_prev = pl.program_id(0) > 0
        for kh in range(KV_HEADS):
            ks = slice(kh * HD, (kh + 1) * HD)
            hs = [kh * rep + rr for rr in range(rep)]
            o = _attn_head(
                jnp.concatenate([q_ref[:, h * HD : (h + 1) * HD] for h in hs], axis=0),
                kp_ref[:, ks], kc_ref[:, ks], vp_ref[:, ks], vc_ref[:, ks],
                jnp.concatenate([jnp.broadcast_to(s_ref[:, h : h + 1], (W, 1)) for h in hs], axis=0), has_prev,
            )
            for rr, h in enumerate(hs):
                o_ref[:, h * HD : (h + 1) * HD] = o[rr * W : (rr + 1) * W].astype(o_ref.dtype)

    return pl.pallas_call(
        body,
        out_shape=jax.ShapeDtypeStruct((L, DQ), BF16),
        grid=(nb,),
        in_specs=[
            pl.BlockSpec((W, DQ), lambda n: (n, 0)),
            pl.BlockSpec((W, kw), lambda n: (jnp.maximum(n - 1, 0), 0)),
            pl.BlockSpec((W, kw), lambda n: (n, 0)),
            pl.BlockSpec((W, kw), lambda n: (jnp.maximum(n - 1, 0), 1)),
            pl.BlockSpec((W, kw), lambda n: (n, 1)),
            pl.BlockSpec((1, heads), lambda n: (0, 0)),
        ],
        out_specs=pl.BlockSpec((W, DQ), lambda n: (n, 0)),
        compiler_params=_params("parallel"),
        name=name,
    )(q, kv, kv, kv, kv, sinks)


def attn_bwd(q, kv, sinks, do, dkv_in, *, name):
    L, DQ = q.shape
    heads = DQ // HEAD_DIM
    rep = heads // KV_HEADS
    nb = L // WINDOW
    kw = KV_HEADS * HEAD_DIM
    W, HD = WINDOW, HEAD_DIM

    def blk(n):
        return jnp.minimum(n, nb - 1)

    def prev(n):
        return jnp.maximum(blk(n) - 1, 0)

    def outb(n):
        return jnp.maximum(n - 1, 0)

    def body(q_ref, kp_ref, kc_ref, vp_ref, vc_ref, s_ref, do_ref, dki_ref, dvi_ref,
             dq_ref, dk_ref, dv_ref, ds_ref, dk_cur, dv_cur):
        n = pl.program_id(0)
        has_prev = n > 0

        @pl.when(n == 0)
        def _():
            ds_ref[...] = jnp.zeros_like(ds_ref)
            dk_cur[...] = jnp.zeros_like(dk_cur)
            dv_cur[...] = jnp.zeros_like(dv_cur)

        @pl.when(n == nb)
        def _():
            dk_ref[...] = dki_ref[...] + dk_cur[...]
            dv_ref[...] = dvi_ref[...] + dv_cur[...]

        @pl.when(n < nb)
        def _():
            lanes = lax.broadcasted_iota(jnp.int32, (1, heads), 1)
            ds_row = jnp.zeros((1, heads), F32)
            for kh in range(KV_HEADS):
                ks = slice(kh * HD, (kh + 1) * HD)
                hs = [kh * rep + rr for rr in range(rep)]
                _, vjp = jax.vjp(
                    functools.partial(_attn_head, has_prev=has_prev),
                    jnp.concatenate([q_ref[:, h * HD : (h + 1) * HD] for h in hs], axis=0),
                    kp_ref[:, ks], kc_ref[:, ks], vp_ref[:, ks], vc_ref[:, ks],
                    jnp.concatenate([jnp.broadcast_to(s_ref[:, h : h + 1], (W, 1)) for h in hs], axis=0),
                )
                dq, dkp, dkc, dvp, dvc, dsk = vjp(
                    jnp.concatenate([do_ref[:, h * HD : (h + 1) * HD] for h in hs], axis=0))
                for rr, h in enumerate(hs):
                    dq_ref[:, h * HD : (h + 1) * HD] = dq[rr * W : (rr + 1) * W]
                    ds_row = ds_row + jnp.where(lanes == h, jnp.sum(dsk[rr * W : (rr + 1) * W], axis=0, keepdims=True), 0.0)
                dk_ref[:, ks] = dki_ref[:, ks] + dk_cur[:, ks] + dkp
                dv_ref[:, ks] = dvi_ref[:, ks] + dv_cur[:, ks] + dvp
                dk_cur[:, ks] = dkc
                dv_cur[:, ks] = dvc
            ds_ref[...] += ds_row

    dq, dk, dv, ds = pl.pallas_call(
        body,
        out_shape=(
            jax.ShapeDtypeStruct((L, DQ), F32),
            jax.ShapeDtypeStruct((L, kw), F32),
            jax.ShapeDtypeStruct((L, kw), F32),
            jax.ShapeDtypeStruct((1, heads), F32),
        ),
        grid=(nb + 1,),
        in_specs=[
            pl.BlockSpec((W, DQ), lambda n: (blk(n), 0)),
            pl.BlockSpec((W, kw), lambda n: (prev(n), 0)),
            pl.BlockSpec((W, kw), lambda n: (blk(n), 0)),
            pl.BlockSpec((W, kw), lambda n: (prev(n), 1)),
            pl.BlockSpec((W, kw), lambda n: (blk(n), 1)),
            pl.BlockSpec((1, heads), lambda n: (0, 0)),
            pl.BlockSpec((W, DQ), lambda n: (blk(n), 0)),
            pl.BlockSpec((W, kw), lambda n: (outb(n), 0)),
            pl.BlockSpec((W, kw), lambda n: (outb(n), 1)),
        ],
        out_specs=(
            pl.BlockSpec((W, DQ), lambda n: (blk(n), 0)),
            pl.BlockSpec((W, kw), lambda n: (outb(n), 0)),
            pl.BlockSpec((W, kw), lambda n: (outb(n), 0)),
            pl.BlockSpec((1, heads), lambda n: (0, 0)),
        ),
        scratch_shapes=[pltpu.VMEM((W, kw), F32), pltpu.VMEM((W, kw), F32)],
        compiler_params=_params("arbitrary"),
        name=name,
    )(q, kv, kv, kv, kv, sinks, do, dkv_in, dkv_in)
    return dq, jnp.concatenate([dk, dv], axis=1), ds


def final_loss(x, fw, target, *, name):
    L, D = x.shape
    tl = _pick(L, 512, 8)

    def body(x_ref, fw_ref, t_ref, loss_ref, dx_ref, dfw_ref):
        @pl.when(pl.program_id(0) == 0)
        def _():
            loss_ref[...] = jnp.zeros_like(loss_ref)
            dfw_ref[...] = jnp.zeros_like(dfw_ref)

        xv = x_ref[...]
        fwv = fw_ref[...]
        r = lax.rsqrt(jnp.mean(xv * xv, axis=-1, keepdims=True) + EPS)
        xhat = xv * r
        err = xhat * fwv - t_ref[...]
        loss_ref[...] += 0.5 * jnp.sum(jnp.mean(err * err, axis=-1, keepdims=True), axis=0, keepdims=True)
        dy = err * (1.0 / D)
        dfw_ref[...] += jnp.sum(dy * xhat, axis=0, keepdims=True)
        dxhat = dy * fwv
        dx_ref[...] = r * (dxhat - xhat * jnp.mean(dxhat * xhat, axis=-1, keepdims=True))

    tile = pl.BlockSpec((tl, D), lambda i: (i, 0))
    row = pl.BlockSpec((1, D), lambda i: (0, 0))
    return pl.pallas_call(
        body,
        out_shape=(
            jax.ShapeDtypeStruct((1, 1), F32),
            jax.ShapeDtypeStruct((L, D), F32),
            jax.ShapeDtypeStruct((1, D), F32),
        ),
        grid=(L // tl,),
        in_specs=[tile, row, tile],
        out_specs=(pl.BlockSpec((1, 1), lambda i: (0, 0)), tile, row),
        compiler_params=_params("arbitrary"),
        name=name,
    )(x, fw, target)


def outer8(ct, d, *, name):
    D, B = ct.shape
    S, _, N = d.shape
    tm = _pick(D, 512, 8)
    tn = _pick(N, 256, LANE)

    def body(c_ref, d_ref, o_ref):
        acc = c_ref[:, 0:1] * d_ref[0:1, :]
        for b in range(1, B):
            acc = acc + c_ref[:, b : b + 1] * d_ref[b : b + 1, :]
        o_ref[...] = acc

    return pl.pallas_call(
        body,
        out_shape=jax.ShapeDtypeStruct((S, D, N), F32),
        grid=(S, D // tm, N // tn),
        in_specs=[
            pl.BlockSpec((tm, B), lambda s, i, j: (i, 0)),
            pl.BlockSpec((None, B, tn), lambda s, i, j: (s, 0, j)),
        ],
        out_specs=pl.BlockSpec((None, tm, tn), lambda s, i, j: (s, i, j)),
        compiler_params=_params("parallel", "parallel", "parallel"),
        name=name,
    )(ct, d)


def reduce8(g, *, name):
    nd, R, N = g.shape

    def body(g_ref, o_ref):
        acc = g_ref[0]
        for b in range(1, nd):
            acc = acc + g_ref[b]
        o_ref[...] = acc

    return pl.pallas_call(
        body,
        out_shape=jax.ShapeDtypeStruct((R, N), F32),
        name=name,
    )(g)


def _as3(a):
    if a.ndim == 1:
        return a.reshape(1, 1, -1)
    if a.ndim == 2:
        return a.reshape((1,) + a.shape)
    return a.reshape((-1,) + a.shape[-2:])


def adamw(w, g, m, v, *, name):
    shape = w.shape
    w3, g3, m3, v3 = _as3(w), _as3(g), _as3(m), _as3(v)
    B, R, C = w3.shape
    tr = _pick(R, max(8, (1 << 19) // max(C, 1) // 8 * 8), 8)

    def body(w_ref, g_ref, m_ref, v_ref, d_ref, nm_ref, nv_ref):
        gv = g_ref[...]
        mn = ADAM_B1 * m_ref[...] + (1.0 - ADAM_B1) * gv
        vn = ADAM_B2 * v_ref[...] + (1.0 - ADAM_B2) * (gv * gv)
        m_hat = mn / (1.0 - ADAM_B1 ** ADAM_STEP)
        v_hat = vn / (1.0 - ADAM_B2 ** ADAM_STEP)
        d_ref[...] = -ADAM_LR * (m_hat / (jnp.sqrt(v_hat) + ADAM_EPS) + ADAM_WD * w_ref[...])
        nm_ref[...] = mn
        nv_ref[...] = vn

    tile = pl.BlockSpec((None, tr, C), lambda b, i: (b, i, 0))
    sds = jax.ShapeDtypeStruct((B, R, C), F32)
    d, nm, nv = pl.pallas_call(
        body,
        out_shape=(sds, sds, sds),
        grid=(B, R // tr),
        in_specs=[tile, tile, tile, tile],
        out_specs=(tile, tile, tile),
        compiler_params=_params("parallel", "parallel"),
        name=name,
    )(w3, g3, m3, v3)
    return d.reshape(shape), nm.reshape(shape), nv.reshape(shape)


def _place():
    return lax.axis_index("x"), lax.axis_index("y"), lax.axis_index("c")


def _flip(v, bit):
    return (1 - v) if bit else v


def ag8(v, *, act=None, after=None, name):
    R, N = v.shape
    extra = [] if after is None else [after]

    def body(*refs):
        v_ref = refs[0]
        out_ref, stage, send_sems, recv_sems = refs[1 + len(extra):]
        x, y, c = _place()
        me = 4 * x + 2 * y + c
        val = v_ref[...]
        if act is not None:
            val = act(val)
        stage[...] = val
        out_ref[me] = val
        sends = []
        for k in range(1, N_DEV):
            px, py, pc = _flip(x, k & 4), _flip(y, k & 2), _flip(c, k & 1)
            cp = pltpu.make_async_remote_copy(
                src_ref=stage, dst_ref=out_ref.at[me], send_sem=send_sems.at[k - 1], recv_sem=recv_sems.at[k - 1],
                device_id=(px, py, pc), device_id_type=MESH,
            )
            cp.start()
            sends.append(cp)
        for k in range(1, N_DEV):
            px, py, pc = _flip(x, k & 4), _flip(y, k & 2), _flip(c, k & 1)
            pltpu.make_async_remote_copy(
                src_ref=stage, dst_ref=out_ref.at[4 * px + 2 * py + pc], send_sem=send_sems.at[k - 1],
                recv_sem=recv_sems.at[k - 1], device_id=(px, py, pc), device_id_type=MESH,
            ).wait_recv()
        for cp in sends:
            cp.wait_send()

    return pl.pallas_call(
        body,
        out_shape=jax.ShapeDtypeStruct((N_DEV, R, N), F32),
        in_specs=[pl.BlockSpec(memory_space=pltpu.VMEM)] + [_ANY] * len(extra),
        out_specs=pl.BlockSpec(memory_space=pltpu.VMEM),
        scratch_shapes=[
            pltpu.VMEM((R, N), F32),
            pltpu.SemaphoreType.DMA((N_DEV - 1,)),
            pltpu.SemaphoreType.DMA((N_DEV - 1,)),
        ],
        name=name,
    )(v, *extra)


def _other_chips(x, y):
    chips = [(1 - x, y), (x, 1 - y), (1 - x, 1 - y)]
    return chips, [2 * px + py for px, py in chips]


_HBM = pl.BlockSpec(memory_space=pltpu.HBM)


_SEM = pl.BlockSpec(memory_space=pltpu.SEMAPHORE)
_ANY = pl.BlockSpec(memory_space=pl.ANY)
_EFFECT = pltpu.SideEffectType.DATAFLOW_SIDE_EFFECTING


def _gather_copies(srcs, lands, send_sems, recv_sems):
    x, y, c = _place()
    k_me = 2 * x + y
    chips, kidx = _other_chips(x, y)
    cps = []
    for w in range(len(srcs)):
        for j, (px, py) in enumerate(chips):
            def copy(dst, w=w, j=j, px=px, py=py):
                return pltpu.make_async_remote_copy(
                    src_ref=srcs[w].at[:, c], dst_ref=dst, send_sem=send_sems.at[3 * w + j],
                    recv_sem=recv_sems.at[3 * w + j], device_id=(px, py, c), device_id_type=MESH,
                )
            cps.append((copy(lands[w].at[:, k_me, c]), copy(lands[w].at[:, kidx[j], c])))
    return cps


def _fill_copies(srcs, lands, send_sems, recv_sems):
    x, y, c = _place()
    k_me = 2 * x + y
    _, kidx = _other_chips(x, y)
    sib = (x, y, 1 - c)
    cps = []
    for w in range(len(srcs)):
        own = pltpu.make_async_remote_copy(
            src_ref=srcs[w], dst_ref=lands[w].at[:, k_me], send_sem=send_sems.at[4 * w + 3], recv_sem=recv_sems.at[4 * w + 3],
            device_id=sib, device_id_type=MESH,
        )
        cps.append((own, own))
        for j in range(3):
            def copy(half, w=w, j=j):
                part = lands[w].at[:, kidx[j], half]
                return pltpu.make_async_remote_copy(
                    src_ref=part, dst_ref=part, send_sem=send_sems.at[4 * w + j], recv_sem=recv_sems.at[4 * w + j],
                    device_id=sib, device_id_type=MESH,
                )
            cps.append((copy(c), copy(1 - c)))
    return cps


def _sibling_copies(srcs, lands, send_sems, recv_sems):
    x, y, c = _place()
    cps = []
    for w in range(len(srcs)):
        cp = pltpu.make_async_remote_copy(
            src_ref=srcs[w].at[:, :, 1 - c], dst_ref=lands[w], send_sem=send_sems.at[w], recv_sem=recv_sems.at[w],
            device_id=(x, y, 1 - c), device_id_type=MESH,
        )
        cps.append((cp, cp))
    return cps


def _reduce_copies(srcs, lands, send_sems, recv_sems):
    x, y, c = _place()
    chips, kidx = _other_chips(x, y)
    cps = []
    for w in range(len(srcs)):
        for j, (px, py) in enumerate(chips):
            cp = pltpu.make_async_remote_copy(
                src_ref=srcs[w].at[:, kidx[j]], dst_ref=lands[w].at[j], send_sem=send_sems.at[3 * w + j],
                recv_sem=recv_sems.at[3 * w + j], device_id=(px, py, c), device_id_type=MESH,
            )
            cps.append((cp, cp))
    return cps


def split_start(copies, srcs, land_shapes, after, *, name, lands=None, per=3):
    n = len(srcs)

    def body(*refs):
        src_refs, land_refs = refs[:n], refs[n : 2 * n]
        send_sems, recv_sems = refs[2 * n + 1], refs[2 * n + 2]
        token = refs[-1]
        for cp, _ in copies(src_refs, land_refs, send_sems, recv_sems):
            cp.start()
        token[...] = jnp.zeros_like(token)

    if lands is None:
        lands = [lax.empty(sh, s.dtype) for sh, s in zip(land_shapes, srcs)]
    land_shapes = [a.shape for a in lands]
    lands = [pltpu.with_memory_space_constraint(a, pltpu.HBM) for a in lands]
    srcs = [pltpu.with_memory_space_constraint(s, pltpu.HBM) for s in srcs]
    out = pl.pallas_call(
        body,
        out_shape=(
            pltpu.SemaphoreType.DMA((per * n,)), pltpu.SemaphoreType.DMA((per * n,)),
            *[pltpu.HBM(s.shape, s.dtype) for s in srcs],
            *[pltpu.HBM(sh, s.dtype) for sh, s in zip(land_shapes, srcs)],
            jax.ShapeDtypeStruct((8, LANE), F32),
        ),
        in_specs=[_HBM] * (2 * n) + [_ANY],
        out_specs=(_SEM, _SEM, *([_HBM] * (2 * n)), pl.BlockSpec(memory_space=pltpu.VMEM)),
        input_output_aliases={i: 2 + i for i in range(2 * n)},
        compiler_params=pltpu.CompilerParams(has_side_effects=_EFFECT),
        name=name,
    )(*srcs, *lands, after)
    return out[0], out[1], list(out[2 : 2 + n]), list(out[2 + n : 2 + 2 * n]), out[-1]


def split_wait(copies, send_sems, recv_sems, srcs, lands, after, *, name):
    n = len(srcs)

    def body(*refs):
        src_refs, land_refs = refs[:n], refs[n : 2 * n]
        send_ref, recv_ref = refs[2 * n], refs[2 * n + 1]
        for sent, arrives in copies(src_refs, land_refs, send_ref, recv_ref):
            sent.wait_send()
            arrives.wait_recv()

    out = pl.pallas_call(
        body,
        out_shape=tuple(pltpu.HBM(a.shape, a.dtype) for a in list(srcs) + list(lands)),
        in_specs=[_HBM] * (2 * n) + [_SEM, _SEM, _ANY],
        out_specs=tuple([_HBM] * (2 * n)),
        input_output_aliases={i: i for i in range(2 * n)},
        compiler_params=pltpu.CompilerParams(has_side_effects=_EFFECT),
        name=name,
    )(*srcs, *lands, send_sems, recv_sems, after)
    return list(out[:n]), list(out[n:])


def rs_share(halves, *, name):
    n = len(halves)

    def body(*refs):
        outs = refs[n : 2 * n]
        send_sems, recv_sems = refs[2 * n :]
        x, y, c = _place()
        cps = []
        for w in range(n):
            cp = pltpu.make_async_remote_copy(
                src_ref=outs[w].at[:, c], dst_ref=outs[w].at[:, c], send_sem=send_sems.at[w], recv_sem=recv_sems.at[w],
                device_id=(x, y, 1 - c), device_id_type=MESH,
            )
            cp.start()
            cps.append(cp)
        for w, cp in enumerate(cps):
            cp.wait_send()
            pltpu.make_async_remote_copy(
                src_ref=outs[w].at[:, c], dst_ref=outs[w].at[:, 1 - c], send_sem=send_sems.at[w], recv_sem=recv_sems.at[w],
                device_id=(x, y, 1 - c), device_id_type=MESH,
            ).wait_recv()

    return pl.pallas_call(
        body,
        out_shape=tuple(jax.ShapeDtypeStruct(h.shape, h.dtype) for h in halves),
        in_specs=[_HBM] * n,
        out_specs=tuple([_HBM] * n),
        scratch_shapes=[pltpu.SemaphoreType.DMA((n,)), pltpu.SemaphoreType.DMA((n,))],
        input_output_aliases={w: w for w in range(n)},
        name=name,
    )(*halves)


def _row_tile(R, C):
    return _pick(R, max(16, (1 << 19) // C // 16 * 16), 16)


def _my_core():
    return lax.axis_index("c")


def _my_chip():
    return 2 * lax.axis_index("x") + lax.axis_index("y")


def rs_add_pair(g, r, *, name):
    M, K, _, R, C = g.shape
    tr = _row_tile(R, C)

    def body(g_ref, r_ref, o_ref):
        o_ref[...] = (g_ref[...].astype(F32) + r_ref[...].astype(F32)).astype(o_ref.dtype)

    blk = pl.BlockSpec((None, None, tr, C), lambda m, k, i: (m, k, i, 0))
    return pl.pallas_call(
        body,
        out_shape=jax.ShapeDtypeStruct((M, K, R, C), BF16),
        grid=(M, K, R // tr),
        in_specs=[pl.BlockSpec((None, None, None, tr, C), lambda m, k, i: (m, k, _my_core(), i, 0)), blk],
        out_specs=blk,
        compiler_params=_params("parallel", "parallel", "parallel"),
        name=name,
    )(g, r)


def rs_add_final(g, r, t, full, m0, *, name):
    M, K, _, R, C = g.shape
    tr = _row_tile(R, C)

    def body(g_ref, r_ref, t_ref, full_ref, o_ref):
        acc = g_ref[...].astype(F32) + r_ref[...].astype(F32)
        for j in range(3):
            acc = acc + t_ref[j].astype(F32)
        o_ref[...] = acc

    return pl.pallas_call(
        body,
        out_shape=jax.ShapeDtypeStruct(full.shape, full.dtype),
        grid=(M, R // tr),
        in_specs=[
            pl.BlockSpec((None, None, None, tr, C), lambda m, i: (m, _my_chip(), _my_core(), i, 0)),
            pl.BlockSpec((None, None, tr, C), lambda m, i: (m, _my_chip(), i, 0)),
            pl.BlockSpec((3, None, tr, C), lambda m, i: (0, m, i, 0)),
            _ANY,
        ],
        out_specs=pl.BlockSpec((None, None, tr, C), lambda m, i: (m0 + m, _my_core(), i, 0)),
        input_output_aliases={3: 0},
        compiler_params=_params("parallel", "parallel"),
        name=name,
    )(g, r, t, full)


WEIGHTS = ["ffn_norm_w", "ffn_w_gu", "ffn_w_down", "mod_w", "mod_b", "mix_norm_w", "ssm_w_in", "ssm_conv_w", "ssm_conv_b",
           "ssm_dt_bias", "ssm_a_log", "ssm_d", "ssm_norm_w", "ssm_w_out", "kv_norm_w", "kv_mod_w", "kv_mod_b", "w_kv", "b_kv",
           "attn_w_q", "attn_b_q", "attn_sinks", "attn_w_o", "attn_b_o", "final_norm_w"]
GATHERED = ["ffn_w_gu", "ffn_w_down", "ssm_w_in", "ssm_w_out", "w_kv", "attn_w_q", "attn_w_o"]
COLUMN_PARALLEL = ["mod_w", "kv_mod_w"]
SMALL_SHARDED = ["ffn_norm_w", "ssm_conv_w", "ssm_conv_b", "ssm_norm_w"]
SMALL = [n for n in WEIGHTS if n not in GATHERED and n not in COLUMN_PARALLEL]


def _row_halves(a):
    a = a.reshape((-1,) + a.shape[-2:])
    return a.reshape(a.shape[0], 2, a.shape[1] // 2, a.shape[2])


def _pack(arrs, rows=8):
    flat = jnp.concatenate([a.reshape(-1) for a in arrs])
    n = flat.shape[0]
    pad = (-n) % (rows * LANE)
    return jnp.pad(flat, (0, pad)).reshape(rows, -1), n


def _unpack(flat, like):
    out, o = [], 0
    for s in like:
        k = 1
        for d in s:
            k *= d
        out.append(flat[o : o + k].reshape(s))
        o += k
    return out


def kernel(x, c, ffn_norm_w, ffn_w_gu, ffn_w_down, mod_w, mod_b, mix_norm_w, ssm_w_in, ssm_conv_w, ssm_conv_b, ssm_dt_bias, ssm_a_log, ssm_d, ssm_norm_w, ssm_w_out, kv_norm_w, kv_mod_w, kv_mod_b, w_kv, b_kv, attn_w_q, attn_b_q, attn_sinks, attn_w_o, attn_b_o, final_norm_w, loss_target, m_ffn_norm_w, m_ffn_w_gu, m_ffn_w_down, m_mod_w, m_mod_b, m_mix_norm_w, m_ssm_w_in, m_ssm_conv_w, m_ssm_conv_b, m_ssm_dt_bias, m_ssm_a_log, m_ssm_d, m_ssm_norm_w, m_ssm_w_out, m_kv_norm_w, m_kv_mod_w, m_kv_mod_b, m_w_kv, m_b_kv, m_attn_w_q, m_attn_b_q, m_attn_sinks, m_attn_w_o, m_attn_b_o, m_final_norm_w, v_ffn_norm_w, v_ffn_w_gu, v_ffn_w_down, v_mod_w, v_mod_b, v_mix_norm_w, v_ssm_w_in, v_ssm_conv_w, v_ssm_conv_b, v_ssm_dt_bias, v_ssm_a_log, v_ssm_d, v_ssm_norm_w, v_ssm_w_out, v_kv_norm_w, v_kv_mod_w, v_kv_mod_b, v_w_kv, v_b_kv, v_attn_w_q, v_attn_b_q, v_attn_sinks, v_attn_w_o, v_attn_b_o, v_final_norm_w):
    env = dict(locals())
    W = {n: env[n] for n in WEIGHTS}
    MOM = {n: env["m_" + n] for n in WEIGHTS}
    VAR = {n: env["v_" + n] for n in WEIGHTS}

    ax, ay, ac = _place()
    kme = 2 * ax + ay
    me = 4 * ax + 2 * ay + ac

    xs = x[0]
    target = loss_target[0]
    L, D = xs.shape
    depth, n_a = ffn_w_gu.shape[0], ssm_w_in.shape[0]
    n_b = depth - n_a
    T = ffn_w_gu.shape[-1]
    DI = ssm_w_out.shape[1] * N_CHIPS
    CI = ssm_w_in.shape[2]
    CC = ssm_conv_w.shape[2] * N_CHIPS
    MW = mod_w.shape[2]
    KW = kv_mod_w.shape[1]
    KVD = w_kv.shape[1]

    def chip_cols(a, width):
        return lax.dynamic_slice_in_dim(a, kme * width, width, axis=a.ndim - 1)

    def ffn_items(i, j):
        return [("ffn_w_gu", 2 * i + j, ffn_w_gu[i, j]), ("ffn_w_down", 2 * i + j, ffn_w_down[i, j])]

    def mix_items(i):
        if i < n_a:
            return [("ssm_w_in", i, ssm_w_in[i]), ("ssm_w_out", i, ssm_w_out[i])]
        return [("attn_w_q", i - n_a, attn_w_q[i - n_a]), ("attn_w_o", i - n_a, attn_w_o[i - n_a])]

    def layer_items(i, order):
        kv_items = [("w_kv", 0, w_kv)] if i == n_a else []
        if order == "fwd":
            return kv_items + ffn_items(i, 0) + mix_items(i) + ffn_items(i, 1)
        return ffn_items(i, 1) + mix_items(i) + ffn_items(i, 0) + kv_items

    def sub_items(sub):
        return {"F": ffn_items, "M": mix_items, "KV": lambda: [("w_kv", 0, w_kv)]}[sub[0]](*sub[1:])

    subs = []
    for i in range(depth):
        subs += ([("KV",)] if i == n_a else []) + [("F", i, 0), ("M", i), ("F", i, 1)]
    assert n_a >= 2 and depth >= 2
    cuts = [0, 1, 2, 4, 6] + [3 * (i + 1) + (1 if i >= n_a else 0) for i in range(2, depth)]
    fwd_plan = [subs[a:b] for a, b in zip(cuts[:-1], cuts[1:])]
    fwd_stages = [[it for sub in stage for it in sub_items(sub)] for stage in fwd_plan]
    bwd_stages = [layer_items(i, "bwd") for i in range(depth - 1, 0, -1)] + [ffn_items(0, 1) + mix_items(0), ffn_items(0, 0)]

    gw, inflight = {}, {}

    def chips_begin(s, after):
        keys = [(n, m0) for n, m0, _ in fwd_stages[s]]
        shards = [_row_halves(a.astype(BF16)) for _, _, a in fwd_stages[s]]
        land_shapes = [(sh.shape[0], N_CHIPS) + sh.shape[1:] for sh in shards]
        ssem, rsem, srcs, lands, token = split_start(_gather_copies, shards, land_shapes, after, name=f"gather_start_{s}")
        inflight[s] = (keys, ssem, rsem, srcs, lands)
        return token[0:1, 0:1]

    def n_head(s):
        return 2 if fwd_plan[s][0][0] == "KV" else 1

    def n_first(s):
        return sum(len(sub_items(sub)) for sub in fwd_plan[s][: n_head(s)])

    def cores_begin(s, after):
        keys, ssem, rsem, srcs, lands = inflight.pop(s)
        srcs, lands = split_wait(_gather_copies, ssem, rsem, srcs, lands, after, name=f"gather_wait_{s}")
        k = n_first(s)
        a_sem, b_sem, a_src, a_land, tok = split_start(
            _fill_copies, srcs[:k], None, after, name=f"fill_start_{s}a", lands=lands[:k], per=4)
        if k < len(keys):
            inflight[s] = (keys[k:],) + split_start(
                _fill_copies, srcs[k:], None, tok, name=f"fill_start_{s}b", lands=lands[k:], per=4)
            tok = inflight[s][-1]
        gw.update(zip(keys[:k], split_wait(_fill_copies, a_sem, b_sem, a_src, a_land, tok, name=f"fill_wait_{s}a")[1]))
        return tok[0:1, 0:1]

    def cores_end(s, after):
        if s in inflight:
            keys, ssem, rsem, srcs, lands, _ = inflight.pop(s)
            gw.update(zip(keys, split_wait(_fill_copies, ssem, rsem, srcs, lands, after, name=f"fill_wait_{s}b")[1]))

    def g_gu(i, j):
        return gw["ffn_w_gu", 2 * i + j].reshape(N_CHIPS, D, T)

    def g_dn(i, j):
        return gw["ffn_w_down", 2 * i + j].reshape(2, T, D)

    def g_full(n, m0):
        a = gw[n, m0]
        return a.reshape(N_CHIPS * 2 * a.shape[-2], a.shape[-1])

    sm_like = [W[n].shape for n in SMALL_SHARDED]
    sm_pack, sm_n = _pack([W[n] for n in SMALL_SHARDED])
    sm_gathered = ag8(sm_pack, name="ag_small_w")
    sm_all = sm_gathered[0::2].reshape(N_CHIPS, -1)[:, :sm_n]
    full = {}
    for n, part in zip(SMALL_SHARDED, zip(*[_unpack(sm_all[k], sm_like) for k in range(N_CHIPS)])):
        full[n] = jnp.concatenate(part, axis=-1)

    c_all = ag8(c, act=_silu, after=sm_gathered, name="ag_c").reshape(N_DEV, D)
    c_all = c_all + chips_begin(0, c_all)
    p_mod = mm(c_all, mod_w, bias=chip_cols(mod_b, MW)[:, None, :], name="mod_mm")
    p_kv = mm(c_all, kv_mod_w, bias=chip_cols(kv_mod_b, KW)[None, :], name="kvmod_mm")
    p_all = jnp.concatenate([jnp.transpose(p_mod, (1, 0, 2)).reshape(N_DEV, depth * MW), p_kv], axis=1)
    p_mine = lax.dynamic_index_in_dim(ag8(p_all, name="ag_mod")[0::2], me, axis=1, keepdims=False)
    mod = jnp.transpose(p_mine[:, : depth * MW].reshape(N_CHIPS, depth, MW), (1, 0, 2)).reshape(depth, N_MOD * D)
    kvmod = p_mine[:, depth * MW :].reshape(1, 2 * D)
    mods = [[mod[i : i + 1, j * D : (j + 1) * D] for j in range(N_MOD)] for i in range(depth)]
    kv_shift, kv_scale = kvmod[:, :D], kvmod[:, D:]

    def ffn_fwd(xin, i, j, sh, sc, gt):
        h = norm_mod_fwd(xin, full["ffn_norm_w"][i, j][None], sc, sh, name=f"ffn_norm_{i}_{j}")
        gu, a = ffn_up(h, g_gu(i, j), name=f"ffn_gu_{i}_{j}")
        f, xo = mm(a, g_dn(i, j), reduce_s=True, resid=(xin, gt, FFN_HALF), name=f"ffn_down_{i}_{j}")
        return xo, (xin, gu, a, f, h)

    def ssm_fwd(xin, i, sh, sc, gt):
        h = norm_mod_fwd(xin, mix_norm_w[i][None], sc, sh, name=f"mix_norm_{i}")
        zx4 = mm(h, gw["ssm_w_in", i].reshape(N_CHIPS, D, CI), name=f"ssm_in_{i}")
        zx = jnp.transpose(zx4, (1, 0, 2)).reshape(L, N_CHIPS * CI)
        xbc = conv_fwd(zx, full["ssm_conv_w"][i], full["ssm_conv_b"][i][None], DI, name=f"ssm_conv_{i}")
        dt_raw = zx[:, DI + CC :]
        y, states = ssd_fwd(xbc, dt_raw, ssm_dt_bias[i][None], ssm_a_log[i][None], ssm_d[i][None], DI, name=f"ssd_{i}")
        yn = gnorm_fwd(y, zx, full["ssm_norm_w"][i][None], name=f"ssm_gnorm_{i}")
        f, xo = mm(yn, g_full("ssm_w_out", i), resid=(xin, gt, 1.0), name=f"ssm_out_{i}")
        return xo, (xin, zx, xbc, dt_raw, y, states, yn, f, h)

    def att_fwd(xin, i, kv, sh, sc, gt):
        l = i - n_a
        h = norm_mod_fwd(xin, mix_norm_w[i][None], sc, sh, name=f"mix_norm_{i}")
        q = mm(h, g_full("attn_w_q", l), bias=attn_b_q[l][None], name=f"att_q_{i}")
        o = attn_fwd(q, kv, attn_sinks[l][None], name=f"att_{i}")
        f, xo = mm(o, g_full("attn_w_o", l), bias=attn_b_o[l][None], resid=(xin, gt, 1.0), name=f"att_o_{i}")
        return xo, (xin, q, o, f, h)

    saved = [[None, None, None] for _ in range(depth)]
    xc = xs
    kv = x_kv = None
    n_stage = len(fwd_stages)

    def run_sub(sub, xc, tok):
        nonlocal kv, x_kv
        if sub[0] == "KV":
            x_kv = xc
            hkv = norm_mod_fwd(xc, kv_norm_w[None], kv_scale, kv_shift + tok, name="kv_norm")
            kv = mm(hkv, g_full("w_kv", 0), bias=b_kv[None], name="kv_proj")
            return xc
        i = sub[1]
        if sub[0] == "F":
            sh, sc, gt = mods[i][6 * sub[2] : 6 * sub[2] + 3]
            xc, saved[i][2 * sub[2]] = ffn_fwd(xc, i, sub[2], sh + tok, sc, gt)
            return xc
        sh, sc, gt = mods[i][3:6]
        xc, saved[i][1] = ssm_fwd(xc, i, sh + tok, sc, gt) if i < n_a else att_fwd(xc, i, kv, sh + tok, sc, gt)
        return xc

    def run_stage(s, xc, tok):
        for k, sub in enumerate(fwd_plan[s]):
            if k == n_head(s):
                cores_end(s, xc)
            xc = run_sub(sub, xc, tok if k < n_head(s) else 0.0)
        return xc

    dep = cores_begin(0, kvmod)
    for s in range(n_stage):
        tok = chips_begin(s + 1, dep) if s + 1 < n_stage else 0.0
        xc = run_stage(s, xc, tok)
        if s + 1 < n_stage:
            dep = cores_begin(s + 1, xc)

    loss_part, dx, d_final = final_loss(xc, final_norm_w[None], target, name="loss_head")
    loss = lax.psum(loss_part[0, 0], ("x", "y", "c"))

    wg = {}
    sg = {
        "ffn_norm_w": [[None, None] for _ in range(depth)], "mix_norm_w": [None] * depth, "mod": [None] * depth,
        "ssm_conv_w": [None] * n_a, "ssm_conv_b": [None] * n_a, "ssm_dt_bias": [None] * n_a, "ssm_a_log": [None] * n_a,
        "ssm_d": [None] * n_a, "ssm_norm_w": [None] * n_a, "attn_b_q": [None] * n_b, "attn_sinks": [None] * n_b,
        "attn_b_o": [None] * n_b,
    }

    def ffn_bwd(dxo, i, j, sv, sh, sc, gt):
        xin, gu, a, f, h = sv
        df, dgt, _ = gate_bwd(f, dxo, gt, FFN_HALF, name=f"ffn_res_bwd_{i}_{j}")
        dgu = ffn_down_bwd(df, g_dn(i, j), gu, name=f"ffn_down_dx_{i}_{j}")
        wg["ffn_w_down", 2 * i + j] = mm(a, df, mode="tn", out_dtype=BF16, name=f"ffn_down_dw_{i}_{j}")
        nw = full["ffn_norm_w"][i, j][None]
        dxi, dnw, dsc, dsh = mm_norm_bwd(dgu, g_gu(i, j), xin, nw, sc, dxo, name=f"ffn_gu_dx_{i}_{j}")
        wg["ffn_w_gu", 2 * i + j] = mm(h, dgu, mode="tn", out_dtype=BF16, name=f"ffn_gu_dw_{i}_{j}")
        sg["ffn_norm_w"][i][j] = dnw
        return dxi, (dsh, dsc, dgt)

    def ssm_bwd(dxo, i, sv, sh, sc, gt):
        xin, zx, xbc, dt_raw, y, states, yn, f, h = sv
        df, dgt, _ = gate_bwd(f, dxo, gt, 1.0, name=f"mix_res_bwd_{i}")
        dyn = mm(df, g_full("ssm_w_out", i), mode="nt", name=f"ssm_out_dx_{i}")
        wg["ssm_w_out", i] = mm(yn, df, mode="tn", out_dtype=BF16, name=f"ssm_out_dw_{i}")
        dy, dz, dnorm = gnorm_bwd(y, zx, full["ssm_norm_w"][i][None], dyn, name=f"ssm_gnorm_bwd_{i}")
        dxbc, ddt, dbias, dalog, ddsk = ssd_bwd(
            xbc, dt_raw, ssm_dt_bias[i][None], ssm_a_log[i][None], ssm_d[i][None], states, dy, DI, name=f"ssd_bwd_{i}"
        )
        du, dcw, dcb = conv_bwd(zx, full["ssm_conv_w"][i], full["ssm_conv_b"][i][None], dxbc, DI, name=f"ssm_conv_bwd_{i}")
        dzx = jnp.concatenate([dz, du, ddt], axis=1).astype(BF16)
        dzx4 = jnp.transpose(dzx.reshape(L, N_CHIPS, CI), (1, 0, 2))
        nw = mix_norm_w[i][None]
        dxi, dnw, dsc, dsh = mm_norm_bwd(
            dzx4, gw["ssm_w_in", i].reshape(N_CHIPS, D, CI), xin, nw, sc, dxo, name=f"ssm_in_dx_{i}")
        wg["ssm_w_in", i] = mm(h, dzx4, mode="tn", out_dtype=BF16, name=f"ssm_in_dw_{i}")
        sg["mix_norm_w"][i] = dnw
        sg["ssm_conv_w"][i], sg["ssm_conv_b"][i], sg["ssm_norm_w"][i] = dcw, dcb, dnorm
        sg["ssm_dt_bias"][i], sg["ssm_a_log"][i], sg["ssm_d"][i] = dbias, dalog, ddsk
        return dxi, (dsh, dsc, dgt)

    def att_bwd(dxo, i, sv, dkv, sh, sc, gt):
        l = i - n_a
        xin, q, o, f, h = sv
        df, dgt, dfsum = gate_bwd(f, dxo, gt, 1.0, name=f"mix_res_bwd_{i}")
        do = mm(df, g_full("attn_w_o", l), mode="nt", name=f"att_o_dx_{i}")
        wg["attn_w_o", l] = mm(o, df, mode="tn", out_dtype=BF16, name=f"att_o_dw_{i}")
        dq, dkv, dsink = attn_bwd(q, kv, attn_sinks[l][None], do, dkv, name=f"att_bwd_{i}")
        nw = mix_norm_w[i][None]
        dxi, dnw, dsc, dsh = mm_norm_bwd(dq, g_full("attn_w_q", l), xin, nw, sc, dxo, name=f"att_q_dx_{i}")
        wg["attn_w_q", l] = mm(h, dq, mode="tn", out_dtype=BF16, name=f"att_q_dw_{i}")
        sg["mix_norm_w"][i] = dnw
        sg["attn_b_q"][l], sg["attn_sinks"][l], sg["attn_b_o"][l] = colsum(dq, name=f"att_bq_{i}"), dsink, dfsum
        return dxi, dkv, (dsh, dsc, dgt)

    gfull = {n: lax.empty(_row_halves(W[n]).shape, F32) for n in GATHERED}

    pending, sib = [], []

    def sibling_begin(items, after):
        parts = []
        for n, m0, a in items:
            m, _, rh, cc = _row_halves(a).shape
            parts.append(wg.pop((n, m0)).reshape(m, N_CHIPS, 2, rh, cc))
        land_shapes = [p.shape[:2] + p.shape[3:] for p in parts]
        tag = f"{items[0][0]}_{items[0][1]}"
        ssem, rsem, srcs, lands, token = split_start(
            _sibling_copies, parts, land_shapes, after, name=f"rs_sibling_start_{tag}", per=1)
        sib.append((tag, [(n, m0) for n, m0, _ in items], ssem, rsem, srcs, lands))
        return token[0:1, 0:1]

    def reduce_step(s, after):
        if pending:
            reduce_end(after)
        items = bwd_stages[s]
        mine, theirs = {}, {}
        while sib:
            tag, keys, ssem, rsem, srcs, lands = sib.pop(0)
            srcs, lands = split_wait(_sibling_copies, ssem, rsem, srcs, lands, after, name=f"rs_sibling_wait_{tag}")
            mine.update(zip(keys, srcs))
            theirs.update(zip(keys, lands))
        parts = [mine[n, m0] for n, m0, _ in items]
        from_sib = [theirs[n, m0] for n, m0, _ in items]
        pair = [rs_add_pair(g, r, name=f"rs_pair_{n}_{m0}") for (n, m0, _), g, r in zip(items, parts, from_sib)]
        land_shapes = [(3, p.shape[0]) + p.shape[2:] for p in pair]
        ssem, rsem, srcs, lands, token = split_start(_reduce_copies, pair, land_shapes, after, name=f"rs_chips_start_{s}")
        pending.append((s, items, parts, from_sib, ssem, rsem, srcs, lands))
        return token[0:1, 0:1]

    def reduce_end(after):
        s, items, parts, from_sib, ssem, rsem, srcs, lands = pending.pop()
        _, from_chips = split_wait(_reduce_copies, ssem, rsem, srcs, lands, after, name=f"rs_chips_wait_{s}")
        for (n, m0, _), g, r, t in zip(items, parts, from_sib, from_chips):
            gfull[n] = rs_add_final(g, r, t, gfull[n], m0, name=f"rs_final_{n}_{m0}")

    dkv = jnp.zeros((L, KVD), F32)
    d_kvnorm = d_kvmod = d_bkv = None
    tok = 0.0
    for i in reversed(range(depth)):
        sh1, sc1, g1, shm, scm, gm, sh2, sc2, g2 = mods[i]
        s1, sm, s2 = saved[i]
        dx, dm2 = ffn_bwd(dx, i, 1, s2, sh2, sc2, g2 + tok)
        tok = sibling_begin(ffn_items(i, 1), dx)
        if i < n_a:
            dx, dmm = ssm_bwd(dx, i, sm, shm, scm, gm + tok)
        else:
            dx, dkv, dmm = att_bwd(dx, i, sm, dkv, shm, scm, gm + tok)
        tok = sibling_begin(mix_items(i), dx)
        if i == 0:
            tok = tok + reduce_step(depth - 1, dx)
        dx, dm1 = ffn_bwd(dx, i, 0, s1, sh1, sc1, g1 + tok)
        tok = sibling_begin(ffn_items(i, 0), dx)
        sg["mod"][i] = jnp.concatenate(list(dm1) + list(dmm) + list(dm2), axis=1)
        if i == n_a:
            d_bkv = colsum(dkv, name="kv_bias_bwd")
            hkv = norm_mod_fwd(x_kv, kv_norm_w[None], kv_scale, kv_shift + tok, name="kv_norm_re")
            wg["w_kv", 0] = mm(hkv, dkv, mode="tn", out_dtype=BF16, name="kv_proj_dw")
            dx, d_kvnorm, dsc, dsh = mm_norm_bwd(
                dkv, g_full("w_kv", 0), x_kv, kv_norm_w[None], kv_scale, dx, name="kv_proj_dx")
            d_kvmod = jnp.concatenate([dsh, dsc], axis=1)
            tok = sibling_begin([("w_kv", 0, w_kv)], dx)
        if i > 0:
            tok = reduce_step(depth - 1 - i, dx)
    grad_x = dx[None]
    grads = {}

    small = {
        "ffn_norm_w": jnp.stack([jnp.stack([r[0] for r in row]) for row in sg["ffn_norm_w"]]),
        "mod_b": jnp.stack([r[0] for r in sg["mod"]]),
        "mix_norm_w": jnp.stack([r[0] for r in sg["mix_norm_w"]]),
        "ssm_conv_w": jnp.stack(sg["ssm_conv_w"]),
        "ssm_conv_b": jnp.stack([r[0] for r in sg["ssm_conv_b"]]),
        "ssm_dt_bias": jnp.stack([r[0] for r in sg["ssm_dt_bias"]]),
        "ssm_a_log": jnp.stack([r[0] for r in sg["ssm_a_log"]]),
        "ssm_d": jnp.stack([r[0] for r in sg["ssm_d"]]),
        "ssm_norm_w": jnp.stack([r[0] for r in sg["ssm_norm_w"]]),
        "kv_norm_w": d_kvnorm[0],
        "kv_mod_b": d_kvmod[0],
        "b_kv": d_bkv[0],
        "attn_b_q": jnp.stack([r[0] for r in sg["attn_b_q"]]),
        "attn_sinks": jnp.stack([r[0] for r in sg["attn_sinks"]]),
        "attn_b_o": jnp.stack([r[0] for r in sg["attn_b_o"]]),
        "final_norm_w": d_final[0],
    }
    small_like = [small[n].shape for n in SMALL]
    sv_pack, sv_n = _pack([small[n] for n in SMALL])
    sv_all = ag8(sv_pack + tok, name="ag_small_g")
    sv_all = sv_all + reduce_step(depth, sv_all)
    sv_sum = reduce8(sv_all, name="small_g_sum").reshape(-1)[:sv_n]
    for n, gsum in zip(SMALL, _unpack(sv_sum, small_like)):
        grads[n] = chip_cols(gsum, W[n].shape[-1]) if n in SMALL_SHARDED else gsum

    per_dev = [_unpack(sv_all[b].reshape(-1)[:sv_n], small_like) for b in range(N_DEV)]
    i_modb, i_kvb = SMALL.index("mod_b"), SMALL.index("kv_mod_b")
    dmod_all = jnp.stack([chip_cols(p[i_modb], MW) for p in per_dev], axis=1)
    dkv_all = jnp.stack([chip_cols(p[i_kvb], KW) for p in per_dev], axis=0)[None]
    c_t = jnp.transpose(c_all)
    grads["mod_w"] = outer8(c_t, dmod_all, name="mod_w_grad")
    grads["kv_mod_w"] = outer8(c_t, dkv_all, name="kv_mod_w_grad")[0]

    delta, new_m, new_v = {}, {}, {}
    for n in COLUMN_PARALLEL:
        delta[n], new_m[n], new_v[n] = adamw(W[n], grads[n], MOM[n], VAR[n], name=f"adamw_{n}")
    like = [W[n].shape for n in SMALL]
    packs = [_pack([d[n] for n in SMALL])[0] for d in (W, grads, MOM, VAR)]
    n_small = sum(int(W[n].size) for n in SMALL)
    for dst, res in zip((delta, new_m, new_v), adamw(*packs, name="adamw_small")):
        for n, a in zip(SMALL, _unpack(res.reshape(-1)[:n_small], like)):
            dst[n] = a

    reduce_end(delta["mod_w"])
    for n, s in zip(GATHERED, rs_share([gfull[n] for n in GATHERED], name="rs_share")):
        grads[n] = s.reshape(W[n].shape)
        delta[n], new_m[n], new_v[n] = adamw(W[n], grads[n], MOM[n], VAR[n], name=f"adamw_{n}")

    return (loss, grad_x, *[grads[n] for n in WEIGHTS], *[delta[n] for n in WEIGHTS], *[new_m[n] for n in WEIGHTS],
            *[new_v[n] for n in WEIGHTS])
```

```python
import functools

import jax
import jax.numpy as jnp
from jax import lax
from jax.experimental import pallas as pl
from jax.experimental.pallas import tpu as pltpu

F32 = jnp.float32
BF16 = jnp.bfloat16
MESH = pl.DeviceIdType.MESH

EPS = 1e-5
N_MOD = 9
FFN_HALF = 0.5
SSM_HEADDIM = 64
SSM_GROUPS = 8
SSM_STATE = 128
CONV_WIDTH = 4
CHUNK = 128
KV_HEADS = 4
HEAD_DIM = 64
WINDOW = 128
N_CHIPS = 4
N_DEV = 8

ADAM_LR = 0.001
ADAM_B1 = 0.9
ADAM_B2 = 0.999
ADAM_EPS = 1e-08
ADAM_WD = 0.01
ADAM_STEP = 10

LANE = 128
MM_TILE = 1024


def _pick(n, pref, align, whole_if_small=False):
    best = 0
    t = align
    while t <= min(n, pref):
        if n % t == 0:
            best = t
        t += align
    if best == 0 or (whole_if_small and best < 256 and n <= 2048):
        return n
    return best


def _sigmoid(x):
    return 1.0 / (1.0 + jnp.exp(-x))


def _silu(x):
    return x * _sigmoid(x)


def _dsilu(x):
    s = _sigmoid(x)
    return s * (1.0 + x * (1.0 - s))


def _params(*sem):
    return pltpu.CompilerParams(dimension_semantics=sem)


def mm(a, b, *, mode="nn", reduce_s=False, out_dtype=F32, bias=None, resid=None, name):
    a_s = a.ndim == 3
    b_s = b.ndim == 3
    S = a.shape[0] if a_s else (b.shape[0] if b_s else 1)
    a2 = a.shape[-2:]
    b2 = b.shape[-2:]
    if mode == "nn":
        (M, K), (K2, N) = a2, b2
    elif mode == "nt":
        (M, K), (N, K2) = a2, b2
    else:
        (K, M), (K2, N) = a2, b2
    assert K == K2, (a.shape, b.shape, mode)
    batch = (a_s or b_s) and not reduce_s
    sb = S if batch else 1
    sr = S if ((a_s or b_s) and reduce_s) else 1
    tm = _pick(M, MM_TILE // 2 if resid is not None else MM_TILE, LANE if mode == "tn" else 16, True)
    tn = _pick(N, MM_TILE, LANE, True)
    tk = _pick(K, 2 * MM_TILE if mode == "tn" else MM_TILE, LANE if mode != "tn" else 16, True)
    nk = K // tk
    grid = (sb, M // tm, N // tn, sr, nk)

    def s_of(isb, isr):
        return isb if batch else isr

    def a_map(isb, i, j, isr, k):
        idx = (k, i) if mode == "tn" else (i, k)
        return ((s_of(isb, isr),) + idx) if a_s else idx

    def b_map(isb, i, j, isr, k):
        idx = (j, k) if mode == "nt" else (k, j)
        return ((s_of(isb, isr),) + idx) if b_s else idx

    def o_map(isb, i, j, isr, k):
        return (isb, i, j) if batch else (i, j)

    def s_blk(has_s, blk):
        return ((None,) + blk) if has_s else blk

    a_blk = (tk, tm) if mode == "tn" else (tm, tk)
    b_blk = (tn, tk) if mode == "nt" else (tk, tn)
    in_specs = [pl.BlockSpec(s_blk(a_s, a_blk), a_map), pl.BlockSpec(s_blk(b_s, b_blk), b_map)]
    args = [a, b]
    if bias is not None:
        bias_s = bias.ndim == 3
        in_specs.append(
            pl.BlockSpec(
                ((None, 1, tn) if bias_s else (1, tn)),
                (lambda isb, i, j, isr, k: (isb, 0, j)) if bias_s else (lambda isb, i, j, isr, k: (0, j)),
            )
        )
        args.append(bias)
    o_spec = pl.BlockSpec(s_blk(batch, (tm, tn)), o_map)
    out_shape = jax.ShapeDtypeStruct(((sb, M, N) if batch else (M, N)), out_dtype)
    out_specs = o_spec
    if resid is not None:
        assert not batch
        x_res, gate, scale = resid
        in_specs += [pl.BlockSpec((tm, tn), o_map), pl.BlockSpec((1, tn), lambda isb, i, j, isr, k: (0, j))]
        args += [x_res, gate]
        out_shape = (out_shape, jax.ShapeDtypeStruct((M, N), F32))
        out_specs = (o_spec, pl.BlockSpec((tm, tn), o_map))
    dims = {"nn": (((1,), (0,)), ((), ())), "nt": (((1,), (1,)), ((), ())), "tn": (((0,), (0,)), ((), ()))}[mode]
    n_in = len(args)
    n_out = 2 if resid is not None else 1
    one_step = sr * nk == 1

    def body(*refs):
        a_ref, b_ref = refs[0], refs[1]
        bias_ref = refs[2] if bias is not None else None
        o_ref = refs[n_in]

        def finish(r):
            if bias is not None:
                r = r + bias_ref[...]
            o_ref[...] = r.astype(o_ref.dtype)
            if resid is not None:
                refs[n_in + 1][...] = refs[n_in - 2][...] + (scale * refs[n_in - 1][...]) * r

        def part():
            return lax.dot_general(a_ref[...].astype(BF16), b_ref[...].astype(BF16), dims, preferred_element_type=F32)

        if one_step:
            finish(part())
            return
        acc = refs[n_in + n_out]
        isr = pl.program_id(3)
        k = pl.program_id(4)

        @pl.when((isr == 0) & (k == 0))
        def _():
            acc[...] = jnp.zeros_like(acc)

        acc[...] += part()

        @pl.when((isr == sr - 1) & (k == nk - 1))
        def _():
            finish(acc[...])

    return pl.pallas_call(
        body,
        out_shape=out_shape,
        grid=grid,
        in_specs=in_specs,
        out_specs=out_specs,
        scratch_shapes=[] if one_step else [pltpu.VMEM((tm, tn), F32)],
        compiler_params=_params("parallel", "parallel", "parallel", "arbitrary", "arbitrary"),
        name=name,
    )(*args)


def norm_mod_fwd(x, nw, sc, sh, *, name):
    L, D = x.shape
    tl = _pick(L, 512, 16)

    def body(x_ref, nw_ref, sc_ref, sh_ref, h_ref):
        xv = x_ref[...]
        r = lax.rsqrt(jnp.mean(xv * xv, axis=-1, keepdims=True) + EPS)
        n = (xv * r) * nw_ref[...]
        h_ref[...] = (n * (1.0 + sc_ref[...]) + sh_ref[...]).astype(h_ref.dtype)

    row = pl.BlockSpec((1, D), lambda i: (0, 0))
    return pl.pallas_call(
        body,
        out_shape=jax.ShapeDtypeStruct((L, D), BF16),
        grid=(L // tl,),
        in_specs=[pl.BlockSpec((tl, D), lambda i: (i, 0)), row, row, row],
        out_specs=pl.BlockSpec((tl, D), lambda i: (i, 0)),
        compiler_params=_params("parallel"),
        name=name,
    )(x, nw, sc, sh)


def mm_norm_bwd(a, b, x, nw, sc, dx_in, *, name):
    has_s = a.ndim == 3
    assert has_s == (b.ndim == 3)
    sr = a.shape[0] if has_s else 1
    M, K = a.shape[-2:]
    D = b.shape[-2]
    tm = _pick(M, MM_TILE, 16, True)
    tk = _pick(K, MM_TILE, LANE, True)
    nk = K // tk

    def body(a_ref, b_ref, x_ref, nw_ref, sc_ref, dxi_ref, dx_ref, dnw_ref, dsc_ref, dsh_ref, acc):
        i, s, k = pl.program_id(0), pl.program_id(1), pl.program_id(2)

        @pl.when((i == 0) & (s == 0) & (k == 0))
        def _():
            dnw_ref[...] = jnp.zeros_like(dnw_ref)
            dsc_ref[...] = jnp.zeros_like(dsc_ref)
            dsh_ref[...] = jnp.zeros_like(dsh_ref)

        @pl.when((s == 0) & (k == 0))
        def _():
            acc[...] = jnp.zeros_like(acc)

        acc[...] += lax.dot_general(
            a_ref[...].astype(BF16), b_ref[...].astype(BF16), (((1,), (1,)), ((), ())), preferred_element_type=F32
        )

        @pl.when((s == sr - 1) & (k == nk - 1))
        def _():
            dh_v = acc[...]
            xv = x_ref[...]
            r = lax.rsqrt(jnp.mean(xv * xv, axis=-1, keepdims=True) + EPS)
            xhat = xv * r
            nw_v = nw_ref[...]
            n = xhat * nw_v
            dsh_ref[...] += jnp.sum(dh_v, axis=0, keepdims=True)
            dsc_ref[...] += jnp.sum(dh_v * n, axis=0, keepdims=True)
            dn = dh_v * (1.0 + sc_ref[...])
            dnw_ref[...] += jnp.sum(dn * xhat, axis=0, keepdims=True)
            dxhat = dn * nw_v
            dx_ref[...] = dxi_ref[...] + r * (dxhat - xhat * jnp.mean(dxhat * xhat, axis=-1, keepdims=True))

    row = pl.BlockSpec((1, D), lambda i, s, k: (0, 0))
    tile = pl.BlockSpec((tm, D), lambda i, s, k: (i, 0))
    vec = jax.ShapeDtypeStruct((1, D), F32)
    return pl.pallas_call(
        body,
        out_shape=(jax.ShapeDtypeStruct((M, D), F32), vec, vec, vec),
        grid=(M // tm, sr, nk),
        in_specs=[
            pl.BlockSpec((None, tm, tk) if has_s else (tm, tk), (lambda i, s, k: (s, i, k)) if has_s else (lambda i, s, k: (i, k))),
            pl.BlockSpec((None, D, tk) if has_s else (D, tk), (lambda i, s, k: (s, 0, k)) if has_s else (lambda i, s, k: (0, k))),
            tile, row, row, tile,
        ],
        out_specs=(tile, row, row, row),
        scratch_shapes=[pltpu.VMEM((tm, D), F32)],
        compiler_params=_params("arbitrary", "arbitrary", "arbitrary"),
        name=name,
    )(a, b, x, nw, sc, dx_in)


def gate_bwd(f, dx, gate, scale, *, name):
    L, D = f.shape
    tl = _pick(L, 512, 16)

    def body(f_ref, dx_ref, g_ref, df_ref, dg_ref, dfsum_ref):
        @pl.when(pl.program_id(0) == 0)
        def _():
            dg_ref[...] = jnp.zeros_like(dg_ref)
            dfsum_ref[...] = jnp.zeros_like(dfsum_ref)

        dxv = dx_ref[...]
        df = (scale * g_ref[...]) * dxv
        df_ref[...] = df.astype(df_ref.dtype)
        dfsum_ref[...] += jnp.sum(df, axis=0, keepdims=True)
        dg_ref[...] += scale * jnp.sum(f_ref[...] * dxv, axis=0, keepdims=True)

    tile = pl.BlockSpec((tl, D), lambda i: (i, 0))
    row = pl.BlockSpec((1, D), lambda i: (0, 0))
    vec = jax.ShapeDtypeStruct((1, D), F32)
    return pl.pallas_call(
        body,
        out_shape=(jax.ShapeDtypeStruct((L, D), BF16), vec, vec),
        grid=(L // tl,),
        in_specs=[tile, tile, row],
        out_specs=(tile, row, row),
        compiler_params=_params("arbitrary"),
        name=name,
    )(f, dx, gate)


def colsum(x, *, name):
    L, N = x.shape
    tl = _pick(L, 512, 8)

    def body(x_ref, o_ref):
        @pl.when(pl.program_id(0) == 0)
        def _():
            o_ref[...] = jnp.zeros_like(o_ref)

        o_ref[...] += jnp.sum(x_ref[...], axis=0, keepdims=True)

    return pl.pallas_call(
        body,
        out_shape=jax.ShapeDtypeStruct((1, N), F32),
        grid=(L // tl,),
        in_specs=[pl.BlockSpec((tl, N), lambda i: (i, 0))],
        out_specs=pl.BlockSpec((1, N), lambda i: (0, 0)),
        compiler_params=_params("arbitrary"),
        name=name,
    )(x)


def ffn_up(h, wgu, *, name):
    L, D = h.shape
    T = wgu.shape[-1]
    tm = _pick(L, 512, 16)

    def body(h_ref, w_ref, gu_ref, a_ref):
        hb = h_ref[...].astype(BF16)
        g = jnp.dot(hb, w_ref[0].astype(BF16), preferred_element_type=F32)
        u = jnp.dot(hb, w_ref[1].astype(BF16), preferred_element_type=F32)
        gu_ref[0] = g
        gu_ref[1] = u
        a_ref[...] = (_silu(g) * u).astype(a_ref.dtype)

    gu, a = pl.pallas_call(
        body,
        out_shape=(jax.ShapeDtypeStruct((2, 2, L, T), F32), jax.ShapeDtypeStruct((2, L, T), BF16)),
        grid=(2, L // tm),
        in_specs=[
            pl.BlockSpec((tm, D), lambda j, i: (i, 0)),
            pl.BlockSpec((2, None, D, T), lambda j, i: (0, j, 0, 0)),
        ],
        out_specs=(
            pl.BlockSpec((2, None, tm, T), lambda j, i: (0, j, i, 0)),
            pl.BlockSpec((None, tm, T), lambda j, i: (j, i, 0)),
        ),
        compiler_params=_params("parallel", "parallel"),
        name=name,
    )(h, wgu.reshape(2, 2, D, T))
    return gu.reshape(4, L, T), a


def ffn_down_bwd(df, wdn, gu, *, name):
    L, D = df.shape
    T = wdn.shape[1]
    tm = _pick(L, 512, 16)

    def body(df_ref, w_ref, gu_ref, d_ref):
        da = lax.dot_general(
            df_ref[...].astype(BF16), w_ref[...].astype(BF16), (((1,), (1,)), ((), ())), preferred_element_type=F32
        )
        g = gu_ref[0]
        d_ref[0] = (da * gu_ref[1] * _dsilu(g)).astype(d_ref.dtype)
        d_ref[1] = (da * _silu(g)).astype(d_ref.dtype)

    out = pl.pallas_call(
        body,
        out_shape=jax.ShapeDtypeStruct((2, 2, L, T), BF16),
        grid=(2, L // tm),
        in_specs=[
            pl.BlockSpec((tm, D), lambda j, i: (i, 0)),
            pl.BlockSpec((None, T, D), lambda j, i: (j, 0, 0)),
            pl.BlockSpec((2, None, tm, T), lambda j, i: (0, j, i, 0)),
        ],
        out_specs=pl.BlockSpec((2, None, tm, T), lambda j, i: (0, j, i, 0)),
        compiler_params=_params("parallel", "parallel"),
        name=name,
    )(df, wdn, gu.reshape(2, 2, L, T))
    return out.reshape(4, L, T)


def _shift_down(u, k, rows):
    if k == 0:
        return u
    return jnp.where(rows >= k, pltpu.roll(u, k, 0), 0.0)


def _shift_up(u, k, rows, n):
    if k == 0:
        return u
    return jnp.where(rows < n - k, pltpu.roll(u, n - k, 0), 0.0)


def _conv_pre(u, w_ref, b_ref, rows):
    pre = b_ref[...] + w_ref[CONV_WIDTH - 1 : CONV_WIDTH, :] * u
    for k in range(1, CONV_WIDTH):
        pre = pre + w_ref[CONV_WIDTH - 1 - k : CONV_WIDTH - k, :] * _shift_down(u, k, rows)
    return pre


def conv_fwd(zx, conv_w, conv_b, d_inner, *, name):
    L = zx.shape[0]
    C = conv_w.shape[1]
    tc = 256
    off = d_inner // tc

    def body(u_ref, w_ref, b_ref, o_ref):
        rows = lax.broadcasted_iota(jnp.int32, (L, tc), 0)
        o_ref[...] = _silu(_conv_pre(u_ref[...], w_ref, b_ref, rows))

    return pl.pallas_call(
        body,
        out_shape=jax.ShapeDtypeStruct((L, C), F32),
        grid=(C // tc,),
        in_specs=[
            pl.BlockSpec((L, tc), lambda j: (0, off + j)),
            pl.BlockSpec((CONV_WIDTH, tc), lambda j: (0, j)),
            pl.BlockSpec((1, tc), lambda j: (0, j)),
        ],
        out_specs=pl.BlockSpec((L, tc), lambda j: (0, j)),
        compiler_params=_params("parallel"),
        name=name,
    )(zx, conv_w, conv_b)


def conv_bwd(zx, conv_w, conv_b, dxbc, d_inner, *, name):
    L = zx.shape[0]
    C = conv_w.shape[1]
    tc = 256
    off = d_inner // tc

    def body(u_ref, w_ref, b_ref, d_ref, du_ref, dw_ref, db_ref):
        rows = lax.broadcasted_iota(jnp.int32, (L, tc), 0)
        u = u_ref[...]
        dpre = d_ref[...] * _dsilu(_conv_pre(u, w_ref, b_ref, rows))
        db_ref[...] = jnp.sum(dpre, axis=0, keepdims=True)
        du = w_ref[CONV_WIDTH - 1 : CONV_WIDTH, :] * dpre
        dw_ref[CONV_WIDTH - 1 : CONV_WIDTH, :] = jnp.sum(dpre * u, axis=0, keepdims=True)
        for k in range(1, CONV_WIDTH):
            j = CONV_WIDTH - 1 - k
            dw_ref[j : j + 1, :] = jnp.sum(dpre * _shift_down(u, k, rows), axis=0, keepdims=True)
            du = du + w_ref[j : j + 1, :] * _shift_up(dpre, k, rows, L)
        du_ref[...] = du

    return pl.pallas_call(
        body,
        out_shape=(
            jax.ShapeDtypeStruct((L, C), F32),
            jax.ShapeDtypeStruct((CONV_WIDTH, C), F32),
            jax.ShapeDtypeStruct((1, C), F32),
        ),
        grid=(C // tc,),
        in_specs=[
            pl.BlockSpec((L, tc), lambda j: (0, off + j)),
            pl.BlockSpec((CONV_WIDTH, tc), lambda j: (0, j)),
            pl.BlockSpec((1, tc), lambda j: (0, j)),
            pl.BlockSpec((L, tc), lambda j: (0, j)),
        ],
        out_specs=(
            pl.BlockSpec((L, tc), lambda j: (0, j)),
            pl.BlockSpec((CONV_WIDTH, tc), lambda j: (0, j)),
            pl.BlockSpec((1, tc), lambda j: (0, j)),
        ),
        compiler_params=_params("parallel"),
        name=name,
    )(zx, conv_w, conv_b, dxbc)


def _ssd_head(xs, dt, acs, tot, dsk, cb, bm, cm, prev):
    q = xs.shape[0]
    li = lax.broadcasted_iota(jnp.int32, (q, q), 0)
    si = lax.broadcasted_iota(jnp.int32, (q, q), 1)
    causal = li >= si
    lmat = jnp.exp(jnp.where(causal, acs - acs.T, -jnp.inf))
    xdt = xs * dt
    y = jnp.dot((cb * lmat).astype(BF16), xdt.astype(BF16), preferred_element_type=F32)
    y = y + lax.dot_general(
        (cm * jnp.exp(acs)).astype(BF16), prev.astype(BF16), (((1,), (1,)), ((), ())), preferred_element_type=F32
    )
    y = y + dsk * xs
    st = lax.dot_general(
        xdt.astype(BF16), (bm * jnp.exp(tot - acs)).astype(BF16), (((0,), (0,)), ((), ())), preferred_element_type=F32
    )
    return y, prev * jnp.exp(tot) + st


def _pick_lane(v, h):
    lanes = lax.broadcasted_iota(jnp.int32, v.shape, 1)
    return jnp.sum(jnp.where(lanes == h, v, 0.0), axis=1, keepdims=True)


def _tri_cols(cols, upper):
    q = cols[0].shape[0]
    assert 3 * len(cols) <= LANE
    li = lax.broadcasted_iota(jnp.int32, (q, q), 0)
    si = lax.broadcasted_iota(jnp.int32, (q, q), 1)
    tri = ((li <= si) if upper else (li >= si)).astype(BF16)
    lanes = lax.broadcasted_iota(jnp.int32, (q, LANE), 1)
    rhs = jnp.zeros((q, LANE), F32)
    for r, col in enumerate(cols):
        hi = col.astype(BF16).astype(F32)
        mid = (col - hi).astype(BF16).astype(F32)
        lo = col - hi - mid
        for t, term in enumerate((hi, mid, lo)):
            rhs = jnp.where(lanes == 3 * r + t, term, rhs)
    out = jnp.dot(tri, rhs.astype(BF16), preferred_element_type=F32)
    return [jnp.sum(jnp.where((lanes >= 3 * r) & (lanes < 3 * r + 3), out, 0.0), axis=1, keepdims=True)
            for r in range(len(cols))]


def _softplus(x):
    return jnp.maximum(x, 0.0) + jnp.log(1.0 + jnp.exp(-jnp.abs(x)))


def _ssd_specs(L, d_inner, H, nc, rev):
    R = H // SSM_GROUPS
    P, N, Q = SSM_HEADDIM, SSM_STATE, CHUNK
    ngrp = SSM_GROUPS

    def ci(c):
        return (nc - 1 - c) if rev else c

    b_off = d_inner // N
    c_off = b_off + ngrp
    xs = pl.BlockSpec((Q, R * P), lambda c, g: (ci(c), g))
    bm = pl.BlockSpec((Q, N), lambda c, g: (ci(c), b_off + g))
    cm = pl.BlockSpec((Q, N), lambda c, g: (ci(c), c_off + g))
    dt = pl.BlockSpec((Q, H), lambda c, g: (ci(c), 0))
    hv = pl.BlockSpec((1, H), lambda c, g: (0, 0))
    y = pl.BlockSpec((Q, R * P), lambda c, g: (ci(c), g))
    st = pl.BlockSpec((None, R * P, N), lambda c, g: (ci(c), g, 0))
    return R, xs, bm, cm, dt, hv, y, st


def ssd_fwd(xbc, dt_raw, dt_bias, a_log, d_skip, d_inner, *, name):
    L = xbc.shape[0]
    H = dt_raw.shape[1]
    nc = L // CHUNK
    P, N = SSM_HEADDIM, SSM_STATE
    R, xs_s, bm_s, cm_s, dt_s, hv_s, y_s, st_s = _ssd_specs(L, d_inner, H, nc, False)

    def body(xs_ref, bm_ref, cm_ref, dt_ref, bias_ref, alog_ref, dsk_ref, y_ref, st_ref, state):
        c = pl.program_id(0)
        g = pl.program_id(1)

        @pl.when(c == 0)
        def _():
            for r in range(R):
                state[g * R + r] = jnp.zeros((P, N), F32)

        dtb = _softplus(dt_ref[...] + bias_ref[...])
        a_all = -jnp.exp(alog_ref[...])
        bm, cm = bm_ref[...], cm_ref[...]
        cb = lax.dot_general(cm.astype(BF16), bm.astype(BF16), (((1,), (1,)), ((), ())), preferred_element_type=F32)
        dts = [_pick_lane(dtb, g * R + r) for r in range(R)]
        a_cols = [dts[r] * _pick_lane(a_all, g * R + r) for r in range(R)]
        acs = _tri_cols(a_cols, upper=False)
        prevs = [state[g * R + r] for r in range(R)]
        res = []
        for r in range(R):
            res.append(_ssd_head(
                xs_ref[:, r * P : (r + 1) * P],
                dts[r],
                jnp.broadcast_to(acs[r], (CHUNK, CHUNK)),
                jnp.sum(a_cols[r], axis=0, keepdims=True),
                _pick_lane(dsk_ref[...], g * R + r),
                cb,
                bm,
                cm,
                prevs[r],
            ))
        for r in range(R):
            st_ref[r * P : (r + 1) * P, :] = prevs[r]
            y_ref[:, r * P : (r + 1) * P] = res[r][0]
            state[g * R + r] = res[r][1]

    return pl.pallas_call(
        body,
        out_shape=(jax.ShapeDtypeStruct((L, d_inner), F32), jax.ShapeDtypeStruct((nc, H * P, N), F32)),
        grid=(nc, SSM_GROUPS),
        in_specs=[xs_s, bm_s, cm_s, dt_s, hv_s, hv_s, hv_s],
        out_specs=(y_s, st_s),
        scratch_shapes=[pltpu.VMEM((H, P, N), F32)],
        compiler_params=_params("arbitrary", "arbitrary"),
        name=name,
    )(xbc, xbc, xbc, dt_raw, dt_bias, a_log, d_skip)


def ssd_bwd(xbc, dt_raw, dt_bias, a_log, d_skip, states, dy, d_inner, *, name):
    L, C = xbc.shape
    H = dt_raw.shape[1]
    nc = L // CHUNK
    P, N, Q = SSM_HEADDIM, SSM_STATE, CHUNK
    R, xs_s, bm_s, cm_s, dt_s, hv_s, y_s, st_s = _ssd_specs(L, d_inner, H, nc, True)

    def body(xs_ref, bm_ref, cm_ref, dt_ref, bias_ref, alog_ref, dsk_ref, st_ref, dy_ref,
             dxs_ref, dbm_ref, dcm_ref, ddt_ref, dbias_ref, dalog_ref, ddsk_ref, dstate):
        c = pl.program_id(0)
        g = pl.program_id(1)

        @pl.when(c == 0)
        def _():
            for r in range(R):
                dstate[g * R + r] = jnp.zeros((P, N), F32)

        @pl.when((c == 0) & (g == 0))
        def _():
            dbias_ref[...] = jnp.zeros_like(dbias_ref)
            dalog_ref[...] = jnp.zeros_like(dalog_ref)
            ddsk_ref[...] = jnp.zeros_like(ddsk_ref)

        @pl.when(g == 0)
        def _():
            ddt_ref[...] = jnp.zeros_like(ddt_ref)

        pre = dt_ref[...] + bias_ref[...]
        dtb = _softplus(pre)
        a_all = -jnp.exp(alog_ref[...])
        lanes_q = lax.broadcasted_iota(jnp.int32, (Q, H), 1)
        lanes_1 = lax.broadcasted_iota(jnp.int32, (1, H), 1)
        bm = bm_ref[...]
        cm = cm_ref[...]
        nt = (((1,), (1,)), ((), ()))
        cb = lax.dot_general(cm.astype(BF16), bm.astype(BF16), nt, preferred_element_type=F32)
        dts = [_pick_lane(dtb, g * R + r) for r in range(R)]
        a_negs = [_pick_lane(a_all, g * R + r) for r in range(R)]
        a_cols = [dts[r] * a_negs[r] for r in range(R)]
        acs = _tri_cols(a_cols, upper=False)
        dbm = jnp.zeros((Q, N), F32)
        dcm = jnp.zeros((Q, N), F32)
        dcb = jnp.zeros((Q, Q), F32)
        dd_row = jnp.zeros((1, H), F32)
        dstates = [dstate[g * R + r] for r in range(R)]
        dprevs, ddts, dacs_cols, dtots = [], [], [], []
        for r in range(R):
            h = g * R + r
            args = (
                xs_ref[:, r * P : (r + 1) * P],
                dts[r],
                jnp.broadcast_to(acs[r], (Q, Q)),
                jnp.sum(a_cols[r], axis=0, keepdims=True),
                _pick_lane(dsk_ref[...], h),
                cb,
                bm,
                cm,
                st_ref[r * P : (r + 1) * P, :],
            )
            _, vjp = jax.vjp(_ssd_head, *args)
            dxs, ddt, dacs, dtot, dd, dcb_h, dbm_h, dcm_h, dprev = vjp((dy_ref[:, r * P : (r + 1) * P], dstates[r]))
            dxs_ref[:, r * P : (r + 1) * P] = dxs
            dprevs.append(dprev)
            ddts.append(ddt)
            dacs_cols.append(jnp.sum(dacs, axis=1, keepdims=True))
            dtots.append(dtot)
            dbm = dbm + dbm_h
            dcm = dcm + dcm_h
            dcb = dcb + dcb_h
            dd_row = dd_row + jnp.where(lanes_1 == h, dd, 0.0)
        for r in range(R):
            dstate[g * R + r] = dprevs[r]
        ddt_blk = jnp.zeros((Q, H), F32)
        da_row = jnp.zeros((1, H), F32)
        for r, da_col in enumerate(_tri_cols(dacs_cols, upper=True)):
            h = g * R + r
            da_col = da_col + dtots[r]
            ddt_blk = ddt_blk + jnp.where(lanes_q == h, ddts[r] + da_col * a_negs[r], 0.0)
            da_row = da_row + jnp.where(lanes_1 == h, jnp.sum(da_col * dts[r], axis=0, keepdims=True), 0.0)
        dcb16 = dcb.astype(BF16)
        dbm_ref[...] = dbm + lax.dot_general(dcb16, cm.astype(BF16), (((0,), (0,)), ((), ())), preferred_element_type=F32)
        dcm_ref[...] = dcm + jnp.dot(dcb16, bm.astype(BF16), preferred_element_type=F32)
        ddt_pre = ddt_blk * _sigmoid(pre)
        ddt_ref[...] += ddt_pre
        dbias_ref[...] += jnp.sum(ddt_pre, axis=0, keepdims=True)
        dalog_ref[...] += da_row * a_all
        ddsk_ref[...] += dd_row

    ngrp = SSM_GROUPS
    hrow = jax.ShapeDtypeStruct((1, H), F32)
    dxs, dbm, dcm, ddt, dbias, dalog, ddsk = pl.pallas_call(
        body,
        out_shape=(
            jax.ShapeDtypeStruct((L, d_inner), F32),
            jax.ShapeDtypeStruct((L, ngrp * N), F32),
            jax.ShapeDtypeStruct((L, ngrp * N), F32),
            jax.ShapeDtypeStruct((L, H), F32),
            hrow,
            hrow,
            hrow,
        ),
        grid=(nc, ngrp),
        in_specs=[xs_s, bm_s, cm_s, dt_s, hv_s, hv_s, hv_s, st_s, y_s],
        out_specs=(
            y_s,
            pl.BlockSpec((Q, N), lambda c, g: (nc - 1 - c, g)),
            pl.BlockSpec((Q, N), lambda c, g: (nc - 1 - c, g)),
            dt_s,
            hv_s,
            hv_s,
            hv_s,
        ),
        scratch_shapes=[pltpu.VMEM((H, P, N), F32)],
        compiler_params=_params("arbitrary", "arbitrary"),
        name=name,
    )(xbc, xbc, xbc, dt_raw, dt_bias, a_log, d_skip, states, dy)
    return jnp.concatenate([dxs, dbm, dcm], axis=1), ddt, dbias, dalog, ddsk


def gnorm_fwd(y, zx, nw, *, name):
    L, DI = y.shape
    gw = DI // SSM_GROUPS
    tl = _pick(L, 512, 16)

    def body(y_ref, z_ref, nw_ref, o_ref):
        yz = y_ref[...] * _silu(z_ref[...])
        r = lax.rsqrt(jnp.mean(yz * yz, axis=-1, keepdims=True) + EPS)
        o_ref[...] = ((yz * r) * nw_ref[...]).astype(o_ref.dtype)

    tile = pl.BlockSpec((tl, gw), lambda i, g: (i, g))
    return pl.pallas_call(
        body,
        out_shape=jax.ShapeDtypeStruct((L, DI), BF16),
        grid=(L // tl, SSM_GROUPS),
        in_specs=[tile, tile, pl.BlockSpec((1, gw), lambda i, g: (0, g))],
        out_specs=tile,
        compiler_params=_params("parallel", "parallel"),
        name=name,
    )(y, zx, nw)


def gnorm_bwd(y, zx, nw, dout, *, name):
    L, DI = y.shape
    gw = DI // SSM_GROUPS
    tl = _pick(L, 512, 16)

    def body(y_ref, z_ref, nw_ref, do_ref, dy_ref, dz_ref, dnw_ref):
        @pl.when(pl.program_id(1) == 0)
        def _():
            dnw_ref[...] = jnp.zeros_like(dnw_ref)

        yv = y_ref[...]
        zv = z_ref[...]
        sz = _silu(zv)
        yz = yv * sz
        r = lax.rsqrt(jnp.mean(yz * yz, axis=-1, keepdims=True) + EPS)
        n = yz * r
        dov = do_ref[...]
        dnw_ref[...] += jnp.sum(dov * n, axis=0, keepdims=True)
        dn = dov * nw_ref[...]
        dyz = r * (dn - n * jnp.mean(dn * n, axis=-1, keepdims=True))
        dy_ref[...] = dyz * sz
        dz_ref[...] = dyz * yv * _dsilu(zv)

    tile = pl.BlockSpec((tl, gw), lambda g, i: (i, g))
    row = pl.BlockSpec((1, gw), lambda g, i: (0, g))
    return pl.pallas_call(
        body,
        out_shape=(
            jax.ShapeDtypeStruct((L, DI), F32),
            jax.ShapeDtypeStruct((L, DI), F32),
            jax.ShapeDtypeStruct((1, DI), F32),
        ),
        grid=(SSM_GROUPS, L // tl),
        in_specs=[tile, tile, row, tile],
        out_specs=(tile, tile, row),
        compiler_params=_params("parallel", "arbitrary"),
        name=name,
    )(y, zx, nw, dout)


def _attn_head(q, kp, kc, vp, vc, sink, has_prev):
    rows, w = q.shape[0], kc.shape[0]
    nt = (((1,), (1,)), ((), ()))
    qb = q.astype(BF16)
    sc = lax.dot_general(qb, kc.astype(BF16), nt, preferred_element_type=F32) * HEAD_DIM ** -0.5
    sp = lax.dot_general(qb, kp.astype(BF16), nt, preferred_element_type=F32) * HEAD_DIM ** -0.5
    ii = jnp.bitwise_and(lax.broadcasted_iota(jnp.int32, (rows, w), 0), w - 1)
    jj = lax.broadcasted_iota(jnp.int32, (rows, w), 1)
    lc = jnp.where(jj <= ii, sc, -jnp.inf)
    lp = jnp.where((jj > ii) & has_prev, sp, -jnp.inf)
    m = jnp.maximum(jnp.maximum(jnp.max(lc, axis=1, keepdims=True), jnp.max(lp, axis=1, keepdims=True)), sink)
    m = lax.stop_gradient(m)
    pc = jnp.exp(lc - m)
    pp = jnp.exp(lp - m)
    denom = jnp.sum(pc, axis=1, keepdims=True) + jnp.sum(pp, axis=1, keepdims=True) + jnp.exp(sink - m)
    o = jnp.dot((pc / denom).astype(BF16), vc.astype(BF16), preferred_element_type=F32)
    return o + jnp.dot((pp / denom).astype(BF16), vp.astype(BF16), preferred_element_type=F32)


def attn_fwd(q, kv, sinks, *, name):
    L, DQ = q.shape
    heads = DQ // HEAD_DIM
    rep = heads // KV_HEADS
    nb = L // WINDOW
    kw = KV_HEADS * HEAD_DIM
    W, HD = WINDOW, HEAD_DIM

    def body(q_ref, kp_ref, kc_ref, vp_ref, vc_ref, s_ref, o_ref):
        has_prev = pl.program_id(0) > 0
        for kh in range(KV_HEADS):
            ks = slice(kh * HD, (kh + 1) * HD)
            hs = [kh * rep + rr for rr in range(rep)]
            o = _attn_head(
                jnp.concatenate([q_ref[:, h * HD : (h + 1) * HD] for h in hs], axis=0),
                kp_ref[:, ks], kc_ref[:, ks], vp_ref[:, ks], vc_ref[:, ks],
                jnp.concatenate([jnp.broadcast_to(s_ref[:, h : h + 1], (W, 1)) for h in hs], axis=0), has_prev,
            )
            for rr, h in enumerate(hs):
                o_ref[:, h * HD : (h + 1) * HD] = o[rr * W : (rr + 1) * W].astype(o_ref.dtype)

    return pl.pallas_call(
        body,
        out_shape=jax.ShapeDtypeStruct((L, DQ), BF16),
        grid=(nb,),
        in_specs=[
            pl.BlockSpec((W, DQ), lambda n: (n, 0)),
            pl.BlockSpec((W, kw), lambda n: (jnp.maximum(n - 1, 0), 0)),
            pl.BlockSpec((W, kw), lambda n: (n, 0)),
            pl.BlockSpec((W, kw), lambda n: (jnp.maximum(n - 1, 0), 1)),
            pl.BlockSpec((W, kw), lambda n: (n, 1)),
            pl.BlockSpec((1, heads), lambda n: (0, 0)),
        ],
        out_specs=pl.BlockSpec((W, DQ), lambda n: (n, 0)),
        compiler_params=_params("parallel"),
        name=name,
    )(q, kv, kv, kv, kv, sinks)


def attn_bwd(q, kv, sinks, do, dkv_in, *, name):
    L, DQ = q.shape
    heads = DQ // HEAD_DIM
    rep = heads // KV_HEADS
    nb = L // WINDOW
    kw = KV_HEADS * HEAD_DIM
    W, HD = WINDOW, HEAD_DIM

    def blk(n):
        return jnp.minimum(n, nb - 1)

    def prev(n):
        return jnp.maximum(blk(n) - 1, 0)

    def outb(n):
        return jnp.maximum(n - 1, 0)

    def body(q_ref, kp_ref, kc_ref, vp_ref, vc_ref, s_ref, do_ref, dki_ref, dvi_ref,
             dq_ref, dk_ref, dv_ref, ds_ref, dk_cur, dv_cur):
        n = pl.program_id(0)
        has_prev = n > 0

        @pl.when(n == 0)
        def _():
            ds_ref[...] = jnp.zeros_like(ds_ref)
            dk_cur[...] = jnp.zeros_like(dk_cur)
            dv_cur[...] = jnp.zeros_like(dv_cur)

        @pl.when(n == nb)
        def _():
            dk_ref[...] = dki_ref[...] + dk_cur[...]
            dv_ref[...] = dvi_ref[...] + dv_cur[...]

        @pl.when(n < nb)
        def _():
            lanes = lax.broadcasted_iota(jnp.int32, (1, heads), 1)
            ds_row = jnp.zeros((1, heads), F32)
            for kh in range(KV_HEADS):
                ks = slice(kh * HD, (kh + 1) * HD)
                hs = [kh * rep + rr for rr in range(rep)]
                _, vjp = jax.vjp(
                    functools.partial(_attn_head, has_prev=has_prev),
                    jnp.concatenate([q_ref[:, h * HD : (h + 1) * HD] for h in hs], axis=0),
                    kp_ref[:, ks], kc_ref[:, ks], vp_ref[:, ks], vc_ref[:, ks],
                    jnp.concatenate([jnp.broadcast_to(s_ref[:, h : h + 1], (W, 1)) for h in hs], axis=0),
                )
                dq, dkp, dkc, dvp, dvc, dsk = vjp(
                    jnp.concatenate([do_ref[:, h * HD : (h + 1) * HD] for h in hs], axis=0))
                for rr, h in enumerate(hs):
                    dq_ref[:, h * HD : (h + 1) * HD] = dq[rr * W : (rr + 1) * W]
                    ds_row = ds_row + jnp.where(lanes == h, jnp.sum(dsk[rr * W : (rr + 1) * W], axis=0, keepdims=True), 0.0)
                dk_ref[:, ks] = dki_ref[:, ks] + dk_cur[:, ks] + dkp
                dv_ref[:, ks] = dvi_ref[:, ks] + dv_cur[:, ks] + dvp
                dk_cur[:, ks] = dkc
                dv_cur[:, ks] = dvc
            ds_ref[...] += ds_row

    dq, dk, dv, ds = pl.pallas_call(
        body,
        out_shape=(
            jax.ShapeDtypeStruct((L, DQ), F32),
            jax.ShapeDtypeStruct((L, kw), F32),
            jax.ShapeDtypeStruct((L, kw), F32),
            jax.ShapeDtypeStruct((1, heads), F32),
        ),
        grid=(nb + 1,),
        in_specs=[
            pl.BlockSpec((W, DQ), lambda n: (blk(n), 0)),
            pl.BlockSpec((W, kw), lambda n: (prev(n), 0)),
            pl.BlockSpec((W, kw), lambda n: (blk(n), 0)),
            pl.BlockSpec((W, kw), lambda n: (prev(n), 1)),
            pl.BlockSpec((W, kw), lambda n: (blk(n), 1)),
            pl.BlockSpec((1, heads), lambda n: (0, 0)),
            pl.BlockSpec((W, DQ), lambda n: (blk(n), 0)),
            pl.BlockSpec((W, kw), lambda n: (outb(n), 0)),
            pl.BlockSpec((W, kw), lambda n: (outb(n), 1)),
        ],
        out_specs=(
            pl.BlockSpec((W, DQ), lambda n: (blk(n), 0)),
            pl.BlockSpec((W, kw), lambda n: (outb(n), 0)),
            pl.BlockSpec((W, kw), lambda n: (outb(n), 0)),
            pl.BlockSpec((1, heads), lambda n: (0, 0)),
        ),
        scratch_shapes=[pltpu.VMEM((W, kw), F32), pltpu.VMEM((W, kw), F32)],
        compiler_params=_params("arbitrary"),
        name=name,
    )(q, kv, kv, kv, kv, sinks, do, dkv_in, dkv_in)
    return dq, jnp.concatenate([dk, dv], axis=1), ds


def final_loss(x, fw, target, *, name):
    L, D = x.shape
    tl = _pick(L, 512, 8)

    def body(x_ref, fw_ref, t_ref, loss_ref, dx_ref, dfw_ref):
        @pl.when(pl.program_id(0) == 0)
        def _():
            loss_ref[...] = jnp.zeros_like(loss_ref)
            dfw_ref[...] = jnp.zeros_like(dfw_ref)

        xv = x_ref[...]
        fwv = fw_ref[...]
        r = lax.rsqrt(jnp.mean(xv * xv, axis=-1, keepdims=True) + EPS)
        xhat = xv * r
        err = xhat * fwv - t_ref[...]
        loss_ref[...] += 0.5 * jnp.sum(jnp.mean(err * err, axis=-1, keepdims=True), axis=0, keepdims=True)
        dy = err * (1.0 / D)
        dfw_ref[...] += jnp.sum(dy * xhat, axis=0, keepdims=True)
        dxhat = dy * fwv
        dx_ref[...] = r * (dxhat - xhat * jnp.mean(dxhat * xhat, axis=-1, keepdims=True))

    tile = pl.BlockSpec((tl, D), lambda i: (i, 0))
    row = pl.BlockSpec((1, D), lambda i: (0, 0))
    return pl.pallas_call(
        body,
        out_shape=(
            jax.ShapeDtypeStruct((1, 1), F32),
            jax.ShapeDtypeStruct((L, D), F32),
            jax.ShapeDtypeStruct((1, D), F32),
        ),
        grid=(L // tl,),
        in_specs=[tile, row, tile],
        out_specs=(pl.BlockSpec((1, 1), lambda i: (0, 0)), tile, row),
        compiler_params=_params("arbitrary"),
        name=name,
    )(x, fw, target)


def outer8(ct, d, *, name):
    D, B = ct.shape
    S, _, N = d.shape
    tm = _pick(D, 512, 8)
    tn = _pick(N, 256, LANE)

    def body(c_ref, d_ref, o_ref):
        acc = c_ref[:, 0:1] * d_ref[0:1, :]
        for b in range(1, B):
            acc = acc + c_ref[:, b : b + 1] * d_ref[b : b + 1, :]
        o_ref[...] = acc

    return pl.pallas_call(
        body,
        out_shape=jax.ShapeDtypeStruct((S, D, N), F32),
        grid=(S, D // tm, N // tn),
        in_specs=[
            pl.BlockSpec((tm, B), lambda s, i, j: (i, 0)),
            pl.BlockSpec((None, B, tn), lambda s, i, j: (s, 0, j)),
        ],
        out_specs=pl.BlockSpec((None, tm, tn), lambda s, i, j: (s, i, j)),
        compiler_params=_params("parallel", "parallel", "parallel"),
        name=name,
    )(ct, d)


def reduce8(g, *, name):
    nd, R, N = g.shape

    def body(g_ref, o_ref):
        acc = g_ref[0]
        for b in range(1, nd):
            acc = acc + g_ref[b]
        o_ref[...] = acc

    return pl.pallas_call(
        body,
        out_shape=jax.ShapeDtypeStruct((R, N), F32),
        name=name,
    )(g)


def _as3(a):
    if a.ndim == 1:
        return a.reshape(1, 1, -1)
    if a.ndim == 2:
        return a.reshape((1,) + a.shape)
    return a.reshape((-1,) + a.shape[-2:])


def adamw(w, g, m, v, *, name):
    shape = w.shape
    w3, g3, m3, v3 = _as3(w), _as3(g), _as3(m), _as3(v)
    B, R, C = w3.shape
    tr = _pick(R, max(8, (1 << 19) // max(C, 1) // 8 * 8), 8)

    def body(w_ref, g_ref, m_ref, v_ref, d_ref, nm_ref, nv_ref):
        gv = g_ref[...]
        mn = ADAM_B1 * m_ref[...] + (1.0 - ADAM_B1) * gv
        vn = ADAM_B2 * v_ref[...] + (1.0 - ADAM_B2) * (gv * gv)
        m_hat = mn / (1.0 - ADAM_B1 ** ADAM_STEP)
        v_hat = vn / (1.0 - ADAM_B2 ** ADAM_STEP)
        d_ref[...] = -ADAM_LR * (m_hat / (jnp.sqrt(v_hat) + ADAM_EPS) + ADAM_WD * w_ref[...])
        nm_ref[...] = mn
        nv_ref[...] = vn

    tile = pl.BlockSpec((None, tr, C), lambda b, i: (b, i, 0))
    sds = jax.ShapeDtypeStruct((B, R, C), F32)
    d, nm, nv = pl.pallas_call(
        body,
        out_shape=(sds, sds, sds),
        grid=(B, R // tr),
        in_specs=[tile, tile, tile, tile],
        out_specs=(tile, tile, tile),
        compiler_params=_params("parallel", "parallel"),
        name=name,
    )(w3, g3, m3, v3)
    return d.reshape(shape), nm.reshape(shape), nv.reshape(shape)


def _place():
    return lax.axis_index("x"), lax.axis_index("y"), lax.axis_index("c")


def _flip(v, bit):
    return (1 - v) if bit else v


def ag8(v, *, act=None, after=None, name):
    R, N = v.shape
    extra = [] if after is None else [after]

    def body(*refs):
        v_ref = refs[0]
        out_ref, stage, send_sems, recv_sems = refs[1 + len(extra):]
        x, y, c = _place()
        me = 4 * x + 2 * y + c
        val = v_ref[...]
        if act is not None:
            val = act(val)
        stage[...] = val
        out_ref[me] = val
        sends = []
        for k in range(1, N_DEV):
            px, py, pc = _flip(x, k & 4), _flip(y, k & 2), _flip(c, k & 1)
            cp = pltpu.make_async_remote_copy(
                src_ref=stage, dst_ref=out_ref.at[me], send_sem=send_sems.at[k - 1], recv_sem=recv_sems.at[k - 1],
                device_id=(px, py, pc), device_id_type=MESH,
            )
            cp.start()
            sends.append(cp)
        for k in range(1, N_DEV):
            px, py, pc = _flip(x, k & 4), _flip(y, k & 2), _flip(c, k & 1)
            pltpu.make_async_remote_copy(
                src_ref=stage, dst_ref=out_ref.at[4 * px + 2 * py + pc], send_sem=send_sems.at[k - 1],
                recv_sem=recv_sems.at[k - 1], device_id=(px, py, pc), device_id_type=MESH,
            ).wait_recv()
        for cp in sends:
            cp.wait_send()

    return pl.pallas_call(
        body,
        out_shape=jax.ShapeDtypeStruct((N_DEV, R, N), F32),
        in_specs=[pl.BlockSpec(memory_space=pltpu.VMEM)] + [_ANY] * len(extra),
        out_specs=pl.BlockSpec(memory_space=pltpu.VMEM),
        scratch_shapes=[
            pltpu.VMEM((R, N), F32),
            pltpu.SemaphoreType.DMA((N_DEV - 1,)),
            pltpu.SemaphoreType.DMA((N_DEV - 1,)),
        ],
        name=name,
    )(v, *extra)


def _other_chips(x, y):
    chips = [(1 - x, y), (x, 1 - y), (1 - x, 1 - y)]
    return chips, [2 * px + py for px, py in chips]


_HBM = pl.BlockSpec(memory_space=pltpu.HBM)


_SEM = pl.BlockSpec(memory_space=pltpu.SEMAPHORE)
_ANY = pl.BlockSpec(memory_space=pl.ANY)
_EFFECT = pltpu.SideEffectType.DATAFLOW_SIDE_EFFECTING


def _gather_copies(srcs, lands, send_sems, recv_sems):
    x, y, c = _place()
    k_me = 2 * x + y
    chips, kidx = _other_chips(x, y)
    cps = []
    for w in range(len(srcs)):
        for j, (px, py) in enumerate(chips):
            def copy(dst, w=w, j=j, px=px, py=py):
                return pltpu.make_async_remote_copy(
                    src_ref=srcs[w].at[:, c], dst_ref=dst, send_sem=send_sems.at[3 * w + j],
                    recv_sem=recv_sems.at[3 * w + j], device_id=(px, py, c), device_id_type=MESH,
                )
            cps.append((copy(lands[w].at[:, k_me, c]), copy(lands[w].at[:, kidx[j], c])))
    return cps


def _fill_copies(srcs, lands, send_sems, recv_sems):
    x, y, c = _place()
    k_me = 2 * x + y
    _, kidx = _other_chips(x, y)
    sib = (x, y, 1 - c)
    cps = []
    for w in range(len(srcs)):
        own = pltpu.make_async_remote_copy(
            src_ref=srcs[w], dst_ref=lands[w].at[:, k_me], send_sem=send_sems.at[4 * w + 3], recv_sem=recv_sems.at[4 * w + 3],
            device_id=sib, device_id_type=MESH,
        )
        cps.append((own, own))
        for j in range(3):
            def copy(half, w=w, j=j):
                part = lands[w].at[:, kidx[j], half]
                return pltpu.make_async_remote_copy(
                    src_ref=part, dst_ref=part, send_sem=send_sems.at[4 * w + j], recv_sem=recv_sems.at[4 * w + j],
                    device_id=sib, device_id_type=MESH,
                )
            cps.append((copy(c), copy(1 - c)))
    return cps


def _sibling_copies(srcs, lands, send_sems, recv_sems):
    x, y, c = _place()
    cps = []
    for w in range(len(srcs)):
        cp = pltpu.make_async_remote_copy(
            src_ref=srcs[w].at[:, :, 1 - c], dst_ref=lands[w], send_sem=send_sems.at[w], recv_sem=recv_sems.at[w],
            device_id=(x, y, 1 - c), device_id_type=MESH,
        )
        cps.append((cp, cp))
    return cps


def _reduce_copies(srcs, lands, send_sems, recv_sems):
    x, y, c = _place()
    chips, kidx = _other_chips(x, y)
    cps = []
    for w in range(len(srcs)):
        for j, (px, py) in enumerate(chips):
            cp = pltpu.make_async_remote_copy(
                src_ref=srcs[w].at[:, kidx[j]], dst_ref=lands[w].at[j], send_sem=send_sems.at[3 * w + j],
                recv_sem=recv_sems.at[3 * w + j], device_id=(px, py, c), device_id_type=MESH,
            )
            cps.append((cp, cp))
    return cps


def split_start(copies, srcs, land_shapes, after, *, name, lands=None, per=3):
    n = len(srcs)

    def body(*refs):
        src_refs, land_refs = refs[:n], refs[n : 2 * n]
        send_sems, recv_sems = refs[2 * n + 1], refs[2 * n + 2]
        token = refs[-1]
        for cp, _ in copies(src_refs, land_refs, send_sems, recv_sems):
            cp.start()
        token[...] = jnp.zeros_like(token)

    if lands is None:
        lands = [lax.empty(sh, s.dtype) for sh, s in zip(land_shapes, srcs)]
    land_shapes = [a.shape for a in lands]
    lands = [pltpu.with_memory_space_constraint(a, pltpu.HBM) for a in lands]
    srcs = [pltpu.with_memory_space_constraint(s, pltpu.HBM) for s in srcs]
    out = pl.pallas_call(
        body,
        out_shape=(
            pltpu.SemaphoreType.DMA((per * n,)), pltpu.SemaphoreType.DMA((per * n,)),
            *[pltpu.HBM(s.shape, s.dtype) for s in srcs],
            *[pltpu.HBM(sh, s.dtype) for sh, s in zip(land_shapes, srcs)],
            jax.ShapeDtypeStruct((8, LANE), F32),
        ),
        in_specs=[_HBM] * (2 * n) + [_ANY],
        out_specs=(_SEM, _SEM, *([_HBM] * (2 * n)), pl.BlockSpec(memory_space=pltpu.VMEM)),
        input_output_aliases={i: 2 + i for i in range(2 * n)},
        compiler_params=pltpu.CompilerParams(has_side_effects=_EFFECT),
        name=name,
    )(*srcs, *lands, after)
    return out[0], out[1], list(out[2 : 2 + n]), list(out[2 + n : 2 + 2 * n]), out[-1]


def split_wait(copies, send_sems, recv_sems, srcs, lands, after, *, name):
    n = len(srcs)

    def body(*refs):
        src_refs, land_refs = refs[:n], refs[n : 2 * n]
        send_ref, recv_ref = refs[2 * n], refs[2 * n + 1]
        for sent, arrives in copies(src_refs, land_refs, send_ref, recv_ref):
            sent.wait_send()
            arrives.wait_recv()

    out = pl.pallas_call(
        body,
        out_shape=tuple(pltpu.HBM(a.shape, a.dtype) for a in list(srcs) + list(lands)),
        in_specs=[_HBM] * (2 * n) + [_SEM, _SEM, _ANY],
        out_specs=tuple([_HBM] * (2 * n)),
        input_output_aliases={i: i for i in range(2 * n)},
        compiler_params=pltpu.CompilerParams(has_side_effects=_EFFECT),
        name=name,
    )(*srcs, *lands, send_sems, recv_sems, after)
    return list(out[:n]), list(out[n:])


def rs_share(halves, *, name):
    n = len(halves)

    def body(*refs):
        outs = refs[n : 2 * n]
        send_sems, recv_sems = refs[2 * n :]
        x, y, c = _place()
        cps = []
        for w in range(n):
            cp = pltpu.make_async_remote_copy(
                src_ref=outs[w].at[:, c], dst_ref=outs[w].at[:, c], send_sem=send_sems.at[w], recv_sem=recv_sems.at[w],
                device_id=(x, y, 1 - c), device_id_type=MESH,
            )
            cp.start()
            cps.append(cp)
        for w, cp in enumerate(cps):
            cp.wait_send()
            pltpu.make_async_remote_copy(
                src_ref=outs[w].at[:, c], dst_ref=outs[w].at[:, 1 - c], send_sem=send_sems.at[w], recv_sem=recv_sems.at[w],
                device_id=(x, y, 1 - c), device_id_type=MESH,
            ).wait_recv()

    return pl.pallas_call(
        body,
        out_shape=tuple(jax.ShapeDtypeStruct(h.shape, h.dtype) for h in halves),
        in_specs=[_HBM] * n,
        out_specs=tuple([_HBM] * n),
        scratch_shapes=[pltpu.SemaphoreType.DMA((n,)), pltpu.SemaphoreType.DMA((n,))],
        input_output_aliases={w: w for w in range(n)},
        name=name,
    )(*halves)


def _row_tile(R, C):
    return _pick(R, max(16, (1 << 19) // C // 16 * 16), 16)


def _my_core():
    return lax.axis_index("c")


def _my_chip():
    return 2 * lax.axis_index("x") + lax.axis_index("y")


def rs_add_pair(g, r, *, name):
    M, K, _, R, C = g.shape
    tr = _row_tile(R, C)

    def body(g_ref, r_ref, o_ref):
        o_ref[...] = (g_ref[...].astype(F32) + r_ref[...].astype(F32)).astype(o_ref.dtype)

    blk = pl.BlockSpec((None, None, tr, C), lambda m, k, i: (m, k, i, 0))
    return pl.pallas_call(
        body,
        out_shape=jax.ShapeDtypeStruct((M, K, R, C), BF16),
        grid=(M, K, R // tr),
        in_specs=[pl.BlockSpec((None, None, None, tr, C), lambda m, k, i: (m, k, _my_core(), i, 0)), blk],
        out_specs=blk,
        compiler_params=_params("parallel", "parallel", "parallel"),
        name=name,
    )(g, r)


def rs_add_final(g, r, t, full, m0, *, name):
    M, K, _, R, C = g.shape
    tr = _row_tile(R, C)

    def body(g_ref, r_ref, t_ref, full_ref, o_ref):
        acc = g_ref[...].astype(F32) + r_ref[...].astype(F32)
        for j in range(3):
            acc = acc + t_ref[j].astype(F32)
        o_ref[...] = acc

    return pl.pallas_call(
        body,
        out_shape=jax.ShapeDtypeStruct(full.shape, full.dtype),
        grid=(M, R // tr),
        in_specs=[
            pl.BlockSpec((None, None, None, tr, C), lambda m, i: (m, _my_chip(), _my_core(), i, 0)),
            pl.BlockSpec((None, None, tr, C), lambda m, i: (m, _my_chip(), i, 0)),
            pl.BlockSpec((3, None, tr, C), lambda m, i: (0, m, i, 0)),
            _ANY,
        ],
        out_specs=pl.BlockSpec((None, None, tr, C), lambda m, i: (m0 + m, _my_core(), i, 0)),
        input_output_aliases={3: 0},
        compiler_params=_params("parallel", "parallel"),
        name=name,
    )(g, r, t, full)


WEIGHTS = ["ffn_norm_w", "ffn_w_gu", "ffn_w_down", "mod_w", "mod_b", "mix_norm_w", "ssm_w_in", "ssm_conv_w", "ssm_conv_b",
           "ssm_dt_bias", "ssm_a_log", "ssm_d", "ssm_norm_w", "ssm_w_out", "kv_norm_w", "kv_mod_w", "kv_mod_b", "w_kv", "b_kv",
           "attn_w_q", "attn_b_q", "attn_sinks", "attn_w_o", "attn_b_o", "final_norm_w"]
GATHERED = ["ffn_w_gu", "ffn_w_down", "ssm_w_in", "ssm_w_out", "w_kv", "attn_w_q", "attn_w_o"]
COLUMN_PARALLEL = ["mod_w", "kv_mod_w"]
SMALL_SHARDED = ["ffn_norm_w", "ssm_conv_w", "ssm_conv_b", "ssm_norm_w"]
SMALL = [n for n in WEIGHTS if n not in GATHERED and n not in COLUMN_PARALLEL]


def _row_halves(a):
    a = a.reshape((-1,) + a.shape[-2:])
    return a.reshape(a.shape[0], 2, a.shape[1] // 2, a.shape[2])


def _pack(arrs, rows=8):
    flat = jnp.concatenate([a.reshape(-1) for a in arrs])
    n = flat.shape[0]
    pad = (-n) % (rows * LANE)
    return jnp.pad(flat, (0, pad)).reshape(rows, -1), n


def _unpack(flat, like):
    out, o = [], 0
    for s in like:
        k = 1
        for d in s:
            k *= d
        out.append(flat[o : o + k].reshape(s))
        o += k
    return out


def kernel(x, c, ffn_norm_w, ffn_w_gu, ffn_w_down, mod_w, mod_b, mix_norm_w, ssm_w_in, ssm_conv_w, ssm_conv_b, ssm_dt_bias, ssm_a_log, ssm_d, ssm_norm_w, ssm_w_out, kv_norm_w, kv_mod_w, kv_mod_b, w_kv, b_kv, attn_w_q, attn_b_q, attn_sinks, attn_w_o, attn_b_o, final_norm_w, loss_target, m_ffn_norm_w, m_ffn_w_gu, m_ffn_w_down, m_mod_w, m_mod_b, m_mix_norm_w, m_ssm_w_in, m_ssm_conv_w, m_ssm_conv_b, m_ssm_dt_bias, m_ssm_a_log, m_ssm_d, m_ssm_norm_w, m_ssm_w_out, m_kv_norm_w, m_kv_mod_w, m_kv_mod_b, m_w_kv, m_b_kv, m_attn_w_q, m_attn_b_q, m_attn_sinks, m_attn_w_o, m_attn_b_o, m_final_norm_w, v_ffn_norm_w, v_ffn_w_gu, v_ffn_w_down, v_mod_w, v_mod_b, v_mix_norm_w, v_ssm_w_in, v_ssm_conv_w, v_ssm_conv_b, v_ssm_dt_bias, v_ssm_a_log, v_ssm_d, v_ssm_norm_w, v_ssm_w_out, v_kv_norm_w, v_kv_mod_w, v_kv_mod_b, v_w_kv, v_b_kv, v_attn_w_q, v_attn_b_q, v_attn_sinks, v_attn_w_o, v_attn_b_o, v_final_norm_w):
    env = dict(locals())
    W = {n: env[n] for n in WEIGHTS}
    MOM = {n: env["m_" + n] for n in WEIGHTS}
    VAR = {n: env["v_" + n] for n in WEIGHTS}

    ax, ay, ac = _place()
    kme = 2 * ax + ay
    me = 4 * ax + 2 * ay + ac

    xs = x[0]
    target = loss_target[0]
    L, D = xs.shape
    depth, n_a = ffn_w_gu.shape[0], ssm_w_in.shape[0]
    n_b = depth - n_a
    T = ffn_w_gu.shape[-1]
    DI = ssm_w_out.shape[1] * N_CHIPS
    CI = ssm_w_in.shape[2]
    CC = ssm_conv_w.shape[2] * N_CHIPS
    MW = mod_w.shape[2]
    KW = kv_mod_w.shape[1]
    KVD = w_kv.shape[1]

    def chip_cols(a, width):
        return lax.dynamic_slice_in_dim(a, kme * width, width, axis=a.ndim - 1)

    def ffn_items(i, j):
        return [("ffn_w_gu", 2 * i + j, ffn_w_gu[i, j]), ("ffn_w_down", 2 * i + j, ffn_w_down[i, j])]

    def mix_items(i):
        if i < n_a:
            return [("ssm_w_in", i, ssm_w_in[i]), ("ssm_w_out", i, ssm_w_out[i])]
        return [("attn_w_q", i - n_a, attn_w_q[i - n_a]), ("attn_w_o", i - n_a, attn_w_o[i - n_a])]

    def layer_items_bwd(i):
        return ffn_items(i, 1) + mix_items(i) + ffn_items(i, 0) + ([("w_kv", 0, w_kv)] if i == n_a else [])

    def sub_items(sub):
        return {"F": ffn_items, "M": mix_items, "KV": lambda: [("w_kv", 0, w_kv)]}[sub[0]](*sub[1:])

    subs = []
    for i in range(depth):
        subs += ([("KV",)] if i == n_a else []) + [("F", i, 0), ("M", i), ("F", i, 1)]
    assert n_a >= 2 and depth >= 2
    cuts = [0, 1, 2, 4, 6] + [3 * (i + 1) + (1 if i >= n_a else 0) for i in range(2, depth)]
    fwd_plan = [subs[a:b] for a, b in zip(cuts[:-1], cuts[1:])]
    fwd_stages = [[it for sub in stage for it in sub_items(sub)] for stage in fwd_plan]
    bwd_stages = [layer_items_bwd(i) for i in range(depth - 1, 0, -1)] + [ffn_items(0, 1) + mix_items(0), ffn_items(0, 0)]

    gw, inflight = {}, {}

    def chips_begin(s, after):
        keys = [(n, m0) for n, m0, _ in fwd_stages[s]]
        shards = [_row_halves(a.astype(BF16)) for _, _, a in fwd_stages[s]]
        land_shapes = [(sh.shape[0], N_CHIPS) + sh.shape[1:] for sh in shards]
        ssem, rsem, srcs, lands, token = split_start(_gather_copies, shards, land_shapes, after, name=f"gather_start_{s}")
        inflight[s] = (keys, ssem, rsem, srcs, lands)
        return token[0:1, 0:1]

    def n_head(s):
        return 2 if fwd_plan[s][0][0] == "KV" else 1

    def n_first(s):
        return sum(len(sub_items(sub)) for sub in fwd_plan[s][: n_head(s)])

    def cores_begin(s, after):
        keys, ssem, rsem, srcs, lands = inflight.pop(s)
        srcs, lands = split_wait(_gather_copies, ssem, rsem, srcs, lands, after, name=f"gather_wait_{s}")
        k = n_first(s)
        a_sem, b_sem, a_src, a_land, tok = split_start(
            _fill_copies, srcs[:k], None, after, name=f"fill_start_{s}a", lands=lands[:k], per=4)
        if k < len(keys):
            inflight[s] = (keys[k:],) + split_start(
                _fill_copies, srcs[k:], None, tok, name=f"fill_start_{s}b", lands=lands[k:], per=4)
            tok = inflight[s][-1]
        gw.update(zip(keys[:k], split_wait(_fill_copies, a_sem, b_sem, a_src, a_land, tok, name=f"fill_wait_{s}a")[1]))
        return tok[0:1, 0:1]

    def cores_end(s, after):
        if s in inflight:
            keys, ssem, rsem, srcs, lands, _ = inflight.pop(s)
            gw.update(zip(keys, split_wait(_fill_copies, ssem, rsem, srcs, lands, after, name=f"fill_wait_{s}b")[1]))

    def g_gu(i, j):
        return gw["ffn_w_gu", 2 * i + j].reshape(N_CHIPS, D, T)

    def g_dn(i, j):
        return gw["ffn_w_down", 2 * i + j].reshape(2, T, D)

    def g_full(n, m0):
        a = gw[n, m0]
        return a.reshape(N_CHIPS * 2 * a.shape[-2], a.shape[-1])

    sm_like = [W[n].shape for n in SMALL_SHARDED]
    sm_pack, sm_n = _pack([W[n] for n in SMALL_SHARDED])
    sm_gathered = ag8(sm_pack, name="ag_small_w")
    sm_all = sm_gathered[0::2].reshape(N_CHIPS, -1)[:, :sm_n]
    full = {}
    for n, part in zip(SMALL_SHARDED, zip(*[_unpack(sm_all[k], sm_like) for k in range(N_CHIPS)])):
        full[n] = jnp.concatenate(part, axis=-1)

    c_all = ag8(c, act=_silu, after=sm_gathered, name="ag_c").reshape(N_DEV, D)
    c_all = c_all + chips_begin(0, c_all)
    p_mod = mm(c_all, mod_w, bias=chip_cols(mod_b, MW)[:, None, :], name="mod_mm")
    p_kv = mm(c_all, kv_mod_w, bias=chip_cols(kv_mod_b, KW)[None, :], name="kvmod_mm")
    p_all = jnp.concatenate([jnp.transpose(p_mod, (1, 0, 2)).reshape(N_DEV, depth * MW), p_kv], axis=1)
    p_mine = lax.dynamic_index_in_dim(ag8(p_all, name="ag_mod")[0::2], me, axis=1, keepdims=False)
    mod = jnp.transpose(p_mine[:, : depth * MW].reshape(N_CHIPS, depth, MW), (1, 0, 2)).reshape(depth, N_MOD * D)
    kvmod = p_mine[:, depth * MW :].reshape(1, 2 * D)
    mods = [[mod[i : i + 1, j * D : (j + 1) * D] for j in range(N_MOD)] for i in range(depth)]
    kv_shift, kv_scale = kvmod[:, :D], kvmod[:, D:]

    def ffn_fwd(xin, i, j, sh, sc, gt):
        h = norm_mod_fwd(xin, full["ffn_norm_w"][i, j][None], sc, sh, name=f"ffn_norm_{i}_{j}")
        gu, a = ffn_up(h, g_gu(i, j), name=f"ffn_gu_{i}_{j}")
        f, xo = mm(a, g_dn(i, j), reduce_s=True, resid=(xin, gt, FFN_HALF), name=f"ffn_down_{i}_{j}")
        return xo, (xin, gu, a, f, h)

    def ssm_fwd(xin, i, sh, sc, gt):
        h = norm_mod_fwd(xin, mix_norm_w[i][None], sc, sh, name=f"mix_norm_{i}")
        zx4 = mm(h, gw["ssm_w_in", i].reshape(N_CHIPS, D, CI), name=f"ssm_in_{i}")
        zx = jnp.transpose(zx4, (1, 0, 2)).reshape(L, N_CHIPS * CI)
        xbc = conv_fwd(zx, full["ssm_conv_w"][i], full["ssm_conv_b"][i][None], DI, name=f"ssm_conv_{i}")
        dt_raw = zx[:, DI + CC :]
        y, states = ssd_fwd(xbc, dt_raw, ssm_dt_bias[i][None], ssm_a_log[i][None], ssm_d[i][None], DI, name=f"ssd_{i}")
        yn = gnorm_fwd(y, zx, full["ssm_norm_w"][i][None], name=f"ssm_gnorm_{i}")
        f, xo = mm(yn, g_full("ssm_w_out", i), resid=(xin, gt, 1.0), name=f"ssm_out_{i}")
        return xo, (xin, zx, xbc, dt_raw, y, states, yn, f, h)

    def att_fwd(xin, i, kv, sh, sc, gt):
        l = i - n_a
        h = norm_mod_fwd(xin, mix_norm_w[i][None], sc, sh, name=f"mix_norm_{i}")
        q = mm(h, g_full("attn_w_q", l), bias=attn_b_q[l][None], name=f"att_q_{i}")
        o = attn_fwd(q, kv, attn_sinks[l][None], name=f"att_{i}")
        f, xo = mm(o, g_full("attn_w_o", l), bias=attn_b_o[l][None], resid=(xin, gt, 1.0), name=f"att_o_{i}")
        return xo, (xin, q, o, f, h)

    saved = [[None, None, None] for _ in range(depth)]
    xc = xs
    kv = x_kv = None
    n_stage = len(fwd_stages)

    def run_sub(sub, xc, tok):
        nonlocal kv, x_kv
        if sub[0] == "KV":
            x_kv = xc
            hkv = norm_mod_fwd(xc, kv_norm_w[None], kv_scale, kv_shift + tok, name="kv_norm")
            kv = mm(hkv, g_full("w_kv", 0), bias=b_kv[None], name="kv_proj")
            return xc
        i = sub[1]
        if sub[0] == "F":
            sh, sc, gt = mods[i][6 * sub[2] : 6 * sub[2] + 3]
            xc, saved[i][2 * sub[2]] = ffn_fwd(xc, i, sub[2], sh + tok, sc, gt)
            return xc
        sh, sc, gt = mods[i][3:6]
        xc, saved[i][1] = ssm_fwd(xc, i, sh + tok, sc, gt) if i < n_a else att_fwd(xc, i, kv, sh + tok, sc, gt)
        return xc

    def run_stage(s, xc, tok):
        for k, sub in enumerate(fwd_plan[s]):
            if k == n_head(s):
                cores_end(s, xc)
            xc = run_sub(sub, xc, tok if k < n_head(s) else 0.0)
        return xc

    dep = cores_begin(0, kvmod)
    for s in range(n_stage):
        tok = chips_begin(s + 1, dep) if s + 1 < n_stage else 0.0
        xc = run_stage(s, xc, tok)
        if s + 1 < n_stage:
            dep = cores_begin(s + 1, xc)

    loss_part, dx, d_final = final_loss(xc, final_norm_w[None], target, name="loss_head")
    loss = lax.psum(loss_part[0, 0], ("x", "y", "c"))

    wg = {}
    sg = {
        "ffn_norm_w": [[None, None] for _ in range(depth)], "mix_norm_w": [None] * depth, "mod": [None] * depth,
        "ssm_conv_w": [None] * n_a, "ssm_conv_b": [None] * n_a, "ssm_dt_bias": [None] * n_a, "ssm_a_log": [None] * n_a,
        "ssm_d": [None] * n_a, "ssm_norm_w": [None] * n_a, "attn_b_q": [None] * n_b, "attn_sinks": [None] * n_b,
        "attn_b_o": [None] * n_b,
    }

    def ffn_bwd(dxo, i, j, sv, sh, sc, gt):
        xin, gu, a, f, h = sv
        df, dgt, _ = gate_bwd(f, dxo, gt, FFN_HALF, name=f"ffn_res_bwd_{i}_{j}")
        dgu = ffn_down_bwd(df, g_dn(i, j), gu, name=f"ffn_down_dx_{i}_{j}")
        wg["ffn_w_down", 2 * i + j] = mm(a, df, mode="tn", out_dtype=BF16, name=f"ffn_down_dw_{i}_{j}")
        nw = full["ffn_norm_w"][i, j][None]
        dxi, dnw, dsc, dsh = mm_norm_bwd(dgu, g_gu(i, j), xin, nw, sc, dxo, name=f"ffn_gu_dx_{i}_{j}")
        wg["ffn_w_gu", 2 * i + j] = mm(h, dgu, mode="tn", out_dtype=BF16, name=f"ffn_gu_dw_{i}_{j}")
        sg["ffn_norm_w"][i][j] = dnw
        return dxi, (dsh, dsc, dgt)

    def ssm_bwd(dxo, i, sv, sh, sc, gt):
        xin, zx, xbc, dt_raw, y, states, yn, f, h = sv
        df, dgt, _ = gate_bwd(f, dxo, gt, 1.0, name=f"mix_res_bwd_{i}")
        dyn = mm(df, g_full("ssm_w_out", i), mode="nt", name=f"ssm_out_dx_{i}")
        wg["ssm_w_out", i] = mm(yn, df, mode="tn", out_dtype=BF16, name=f"ssm_out_dw_{i}")
        dy, dz, dnorm = gnorm_bwd(y, zx, full["ssm_norm_w"][i][None], dyn, name=f"ssm_gnorm_bwd_{i}")
        dxbc, ddt, dbias, dalog, ddsk = ssd_bwd(
            xbc, dt_raw, ssm_dt_bias[i][None], ssm_a_log[i][None], ssm_d[i][None], states, dy, DI, name=f"ssd_bwd_{i}"
        )
        du, dcw, dcb = conv_bwd(zx, full["ssm_conv_w"][i], full["ssm_conv_b"][i][None], dxbc, DI, name=f"ssm_conv_bwd_{i}")
        dzx = jnp.concatenate([dz, du, ddt], axis=1).astype(BF16)
        dzx4 = jnp.transpose(dzx.reshape(L, N_CHIPS, CI), (1, 0, 2))
        nw = mix_norm_w[i][None]
        dxi, dnw, dsc, dsh = mm_norm_bwd(
            dzx4, gw["ssm_w_in", i].reshape(N_CHIPS, D, CI), xin, nw, sc, dxo, name=f"ssm_in_dx_{i}")
        wg["ssm_w_in", i] = mm(h, dzx4, mode="tn", out_dtype=BF16, name=f"ssm_in_dw_{i}")
        sg["mix_norm_w"][i] = dnw
        sg["ssm_conv_w"][i], sg["ssm_conv_b"][i], sg["ssm_norm_w"][i] = dcw, dcb, dnorm
        sg["ssm_dt_bias"][i], sg["ssm_a_log"][i], sg["ssm_d"][i] = dbias, dalog, ddsk
        return dxi, (dsh, dsc, dgt)

    def att_bwd(dxo, i, sv, dkv, sh, sc, gt):
        l = i - n_a
        xin, q, o, f, h = sv
        df, dgt, dfsum = gate_bwd(f, dxo, gt, 1.0, name=f"mix_res_bwd_{i}")
        do = mm(df, g_full("attn_w_o", l), mode="nt", name=f"att_o_dx_{i}")
        wg["attn_w_o", l] = mm(o, df, mode="tn", out_dtype=BF16, name=f"att_o_dw_{i}")
        dq, dkv, dsink = attn_bwd(q, kv, attn_sinks[l][None], do, dkv, name=f"att_bwd_{i}")
        nw = mix_norm_w[i][None]
        dxi, dnw, dsc, dsh = mm_norm_bwd(dq, g_full("attn_w_q", l), xin, nw, sc, dxo, name=f"att_q_dx_{i}")
        wg["attn_w_q", l] = mm(h, dq, mode="tn", out_dtype=BF16, name=f"att_q_dw_{i}")
        sg["mix_norm_w"][i] = dnw
        sg["attn_b_q"][l], sg["attn_sinks"][l], sg["attn_b_o"][l] = colsum(dq, name=f"att_bq_{i}"), dsink, dfsum
        return dxi, dkv, (dsh, dsc, dgt)

    gfull = {n: lax.empty(_row_halves(W[n]).shape, F32) for n in GATHERED}

    pending, sib = [], []

    def sibling_begin(items, after):
        parts = []
        for n, m0, a in items:
            m, _, rh, cc = _row_halves(a).shape
            parts.append(wg.pop((n, m0)).reshape(m, N_CHIPS, 2, rh, cc))
        land_shapes = [p.shape[:2] + p.shape[3:] for p in parts]
        tag = f"{items[0][0]}_{items[0][1]}"
        ssem, rsem, srcs, lands, token = split_start(
            _sibling_copies, parts, land_shapes, after, name=f"rs_sibling_start_{tag}", per=1)
        sib.append((tag, [(n, m0) for n, m0, _ in items], ssem, rsem, srcs, lands))
        return token[0:1, 0:1]

    def reduce_step(s, after):
        if pending:
            reduce_end(after)
        items = bwd_stages[s]
        mine, theirs = {}, {}
        while sib:
            tag, keys, ssem, rsem, srcs, lands = sib.pop(0)
            srcs, lands = split_wait(_sibling_copies, ssem, rsem, srcs, lands, after, name=f"rs_sibling_wait_{tag}")
            mine.update(zip(keys, srcs))
            theirs.update(zip(keys, lands))
        parts = [mine[n, m0] for n, m0, _ in items]
        from_sib = [theirs[n, m0] for n, m0, _ in items]
        pair = [rs_add_pair(g, r, name=f"rs_pair_{n}_{m0}") for (n, m0, _), g, r in zip(items, parts, from_sib)]
        land_shapes = [(3, p.shape[0]) + p.shape[2:] for p in pair]
        ssem, rsem, srcs, lands, token = split_start(_reduce_copies, pair, land_shapes, after, name=f"rs_chips_start_{s}")
        pending.append((s, items, parts, from_sib, ssem, rsem, srcs, lands))
        return token[0:1, 0:1]

    def reduce_end(after):
        s, items, parts, from_sib, ssem, rsem, srcs, lands = pending.pop()
        _, from_chips = split_wait(_reduce_copies, ssem, rsem, srcs, lands, after, name=f"rs_chips_wait_{s}")
        for (n, m0, _), g, r, t in zip(items, parts, from_sib, from_chips):
            gfull[n] = rs_add_final(g, r, t, gfull[n], m0, name=f"rs_final_{n}_{m0}")

    dkv = jnp.zeros((L, KVD), F32)
    d_kvnorm = d_kvmod = d_bkv = None
    tok = 0.0
    for i in reversed(range(depth)):
        sh1, sc1, g1, shm, scm, gm, sh2, sc2, g2 = mods[i]
        s1, sm, s2 = saved[i]
        dx, dm2 = ffn_bwd(dx, i, 1, s2, sh2, sc2, g2 + tok)
        tok = sibling_begin(ffn_items(i, 1), dx)
        if i < n_a:
            dx, dmm = ssm_bwd(dx, i, sm, shm, scm, gm + tok)
        else:
            dx, dkv, dmm = att_bwd(dx, i, sm, dkv, shm, scm, gm + tok)
        tok = sibling_begin(mix_items(i), dx)
        if i == 0:
            tok = tok + reduce_step(depth - 1, dx)
        dx, dm1 = ffn_bwd(dx, i, 0, s1, sh1, sc1, g1 + tok)
        tok = sibling_begin(ffn_items(i, 0), dx)
        sg["mod"][i] = jnp.concatenate(list(dm1) + list(dmm) + list(dm2), axis=1)
        if i == n_a:
            d_bkv = colsum(dkv, name="kv_bias_bwd")
            hkv = norm_mod_fwd(x_kv, kv_norm_w[None], kv_scale, kv_shift + tok, name="kv_norm_re")
            wg["w_kv", 0] = mm(hkv, dkv, mode="tn", out_dtype=BF16, name="kv_proj_dw")
            dx, d_kvnorm, dsc, dsh = mm_norm_bwd(
                dkv, g_full("w_kv", 0), x_kv, kv_norm_w[None], kv_scale, dx, name="kv_proj_dx")
            d_kvmod = jnp.concatenate([dsh, dsc], axis=1)
            tok = sibling_begin([("w_kv", 0, w_kv)], dx)
        if i > 0:
            tok = reduce_step(depth - 1 - i, dx)
    grad_x = dx[None]
    grads = {}

    small = {
        "ffn_norm_w": jnp.stack([jnp.stack([r[0] for r in row]) for row in sg["ffn_norm_w"]]),
        "mod_b": jnp.stack([r[0] for r in sg["mod"]]),
        "mix_norm_w": jnp.stack([r[0] for r in sg["mix_norm_w"]]),
        "ssm_conv_w": jnp.stack(sg["ssm_conv_w"]),
        "ssm_conv_b": jnp.stack([r[0] for r in sg["ssm_conv_b"]]),
        "ssm_dt_bias": jnp.stack([r[0] for r in sg["ssm_dt_bias"]]),
        "ssm_a_log": jnp.stack([r[0] for r in sg["ssm_a_log"]]),
        "ssm_d": jnp.stack([r[0] for r in sg["ssm_d"]]),
        "ssm_norm_w": jnp.stack([r[0] for r in sg["ssm_norm_w"]]),
        "kv_norm_w": d_kvnorm[0],
        "kv_mod_b": d_kvmod[0],
        "b_kv": d_bkv[0],
        "attn_b_q": jnp.stack([r[0] for r in sg["attn_b_q"]]),
        "attn_sinks": jnp.stack([r[0] for r in sg["attn_sinks"]]),
        "attn_b_o": jnp.stack([r[0] for r in sg["attn_b_o"]]),
        "final_norm_w": d_final[0],
    }
    small_like = [small[n].shape for n in SMALL]
    sv_pack, sv_n = _pack([small[n] for n in SMALL])
    sv_all = ag8(sv_pack + tok, name="ag_small_g")
    sv_all = sv_all + reduce_step(depth, sv_all)
    sv_sum = reduce8(sv_all, name="small_g_sum").reshape(-1)[:sv_n]
    for n, gsum in zip(SMALL, _unpack(sv_sum, small_like)):
        grads[n] = chip_cols(gsum, W[n].shape[-1]) if n in SMALL_SHARDED else gsum

    per_dev = [_unpack(sv_all[b].reshape(-1)[:sv_n], small_like) for b in range(N_DEV)]
    i_modb, i_kvb = SMALL.index("mod_b"), SMALL.index("kv_mod_b")
    dmod_all = jnp.stack([chip_cols(p[i_modb], MW) for p in per_dev], axis=1)
    dkv_all = jnp.stack([chip_cols(p[i_kvb], KW) for p in per_dev], axis=0)[None]
    c_t = jnp.transpose(c_all)
    grads["mod_w"] = outer8(c_t, dmod_all, name="mod_w_grad")
    grads["kv_mod_w"] = outer8(c_t, dkv_all, name="kv_mod_w_grad")[0]

    delta, new_m, new_v = {}, {}, {}
    for n in COLUMN_PARALLEL:
        delta[n], new_m[n], new_v[n] = adamw(W[n], grads[n], MOM[n], VAR[n], name=f"adamw_{n}")
    like = [W[n].shape for n in SMALL]
    packs = [_pack([d[n] for n in SMALL])[0] for d in (W, grads, MOM, VAR)]
    n_small = sum(int(W[n].size) for n in SMALL)
    for dst, res in zip((delta, new_m, new_v), adamw(*packs, name="adamw_small")):
        for n, a in zip(SMALL, _unpack(res.reshape(-1)[:n_small], like)):
            dst[n] = a

    reduce_end(delta["mod_w"])
    for n, s in zip(GATHERED, rs_share([gfull[n] for n in GATHERED], name="rs_share")):
        grads[n] = s.reshape(W[n].shape)
        delta[n], new_m[n], new_v[n] = adamw(W[n], grads[n], MOM[n], VAR[n], name=f"adamw_{n}")

    return (loss, grad_x, *[grads[n] for n in WEIGHTS], *[delta[n] for n in WEIGHTS], *[new_m[n] for n in WEIGHTS],
            *[new_v[n] for n in WEIGHTS])
```

```python
import functools

import jax
import jax.numpy as jnp
from jax import lax
from jax.experimental import pallas as pl
from jax.experimental.pallas import tpu as pltpu

F32 = jnp.float32
BF16 = jnp.bfloat16
HIGHEST = lax.Precision.HIGHEST
MESH = pl.DeviceIdType.MESH

EPS = 1e-5
N_MOD = 9
FFN_HALF = 0.5
SSM_HEADDIM = 64
SSM_GROUPS = 8
SSM_STATE = 128
CONV_WIDTH = 4
CHUNK = 128
KV_HEADS = 4
HEAD_DIM = 64
WINDOW = 128
N_CHIPS = 4
N_DEV = 8

ADAM_LR = 0.001
ADAM_B1 = 0.9
ADAM_B2 = 0.999
ADAM_EPS = 1e-08
ADAM_WD = 0.01
ADAM_STEP = 10

LANE = 128
MM_TILE = 1024


def _pick(n, pref, align, whole_if_small=False):
    best = 0
    t = align
    while t <= min(n, pref):
        if n % t == 0:
            best = t
        t += align
    if best == 0 or (whole_if_small and best < 256 and n <= 2048):
        return n
    return best


def _sigmoid(x):
    return 1.0 / (1.0 + jnp.exp(-x))


def _silu(x):
    return x * _sigmoid(x)


def _dsilu(x):
    s = _sigmoid(x)
    return s * (1.0 + x * (1.0 - s))


def _params(*sem):
    return pltpu.CompilerParams(dimension_semantics=sem)


def mm(a, b, *, mode="nn", reduce_s=False, out_dtype=F32, bias=None, resid=None, name):
    a_s = a.ndim == 3
    b_s = b.ndim == 3
    S = a.shape[0] if a_s else (b.shape[0] if b_s else 1)
    a2 = a.shape[-2:]
    b2 = b.shape[-2:]
    if mode == "nn":
        (M, K), (K2, N) = a2, b2
    elif mode == "nt":
        (M, K), (N, K2) = a2, b2
    else:
        (K, M), (K2, N) = a2, b2
    assert K == K2, (a.shape, b.shape, mode)
    batch = (a_s or b_s) and not reduce_s
    sb = S if batch else 1
    sr = S if ((a_s or b_s) and reduce_s) else 1
    tm = _pick(M, MM_TILE // 2 if resid is not None else MM_TILE, LANE if mode == "tn" else 16, True)
    tn = _pick(N, MM_TILE, LANE, True)
    tk = _pick(K, 2 * MM_TILE if mode == "tn" else MM_TILE, LANE if mode != "tn" else 16, True)
    nk = K // tk
    grid = (sb, M // tm, N // tn, sr, nk)

    def s_of(isb, isr):
        return isb if batch else isr

    def a_map(isb, i, j, isr, k):
        idx = (k, i) if mode == "tn" else (i, k)
        return ((s_of(isb, isr),) + idx) if a_s else idx

    def b_map(isb, i, j, isr, k):
        idx = (j, k) if mode == "nt" else (k, j)
        return ((s_of(isb, isr),) + idx) if b_s else idx

    def o_map(isb, i, j, isr, k):
        return (isb, i, j) if batch else (i, j)

    def s_blk(has_s, blk):
        return ((None,) + blk) if has_s else blk

    a_blk = (tk, tm) if mode == "tn" else (tm, tk)
    b_blk = (tn, tk) if mode == "nt" else (tk, tn)
    in_specs = [pl.BlockSpec(s_blk(a_s, a_blk), a_map), pl.BlockSpec(s_blk(b_s, b_blk), b_map)]
    args = [a, b]
    if bias is not None:
        bias_s = bias.ndim == 3
        in_specs.append(
            pl.BlockSpec(
                ((None, 1, tn) if bias_s else (1, tn)),
                (lambda isb, i, j, isr, k: (isb, 0, j)) if bias_s else (lambda isb, i, j, isr, k: (0, j)),
            )
        )
        args.append(bias)
    o_spec = pl.BlockSpec(s_blk(batch, (tm, tn)), o_map)
    out_shape = jax.ShapeDtypeStruct(((sb, M, N) if batch else (M, N)), out_dtype)
    out_specs = o_spec
    if resid is not None:
        assert not batch
        x_res, gate, scale = resid
        in_specs += [pl.BlockSpec((tm, tn), o_map), pl.BlockSpec((1, tn), lambda isb, i, j, isr, k: (0, j))]
        args += [x_res, gate]
        out_shape = (out_shape, jax.ShapeDtypeStruct((M, N), F32))
        out_specs = (o_spec, pl.BlockSpec((tm, tn), o_map))
    dims = {"nn": (((1,), (0,)), ((), ())), "nt": (((1,), (1,)), ((), ())), "tn": (((0,), (0,)), ((), ()))}[mode]
    n_in = len(args)
    n_out = 2 if resid is not None else 1
    one_step = sr * nk == 1

    def body(*refs):
        a_ref, b_ref = refs[0], refs[1]
        bias_ref = refs[2] if bias is not None else None
        o_ref = refs[n_in]

        def finish(r):
            if bias is not None:
                r = r + bias_ref[...]
            o_ref[...] = r.astype(o_ref.dtype)
            if resid is not None:
                refs[n_in + 1][...] = refs[n_in - 2][...] + (scale * refs[n_in - 1][...]) * r

        def part():
            return lax.dot_general(a_ref[...].astype(BF16), b_ref[...].astype(BF16), dims, preferred_element_type=F32)

        if one_step:
            finish(part())
            return
        acc = refs[n_in + n_out]
        isr = pl.program_id(3)
        k = pl.program_id(4)

        @pl.when((isr == 0) & (k == 0))
        def _():
            acc[...] = jnp.zeros_like(acc)

        acc[...] += part()

        @pl.when((isr == sr - 1) & (k == nk - 1))
        def _():
            finish(acc[...])

    return pl.pallas_call(
        body,
        out_shape=out_shape,
        grid=grid,
        in_specs=in_specs,
        out_specs=out_specs,
        scratch_shapes=[] if one_step else [pltpu.VMEM((tm, tn), F32)],
        compiler_params=_params("parallel", "parallel", "parallel", "arbitrary", "arbitrary"),
        name=name,
    )(*args)


def norm_mod_fwd(x, nw, sc, sh, *, name):
    L, D = x.shape
    tl = _pick(L, 512, 16)

    def body(x_ref, nw_ref, sc_ref, sh_ref, h_ref):
        xv = x_ref[...]
        r = lax.rsqrt(jnp.mean(xv * xv, axis=-1, keepdims=True) + EPS)
        n = (xv * r) * nw_ref[...]
        h_ref[...] = (n * (1.0 + sc_ref[...]) + sh_ref[...]).astype(h_ref.dtype)

    row = pl.BlockSpec((1, D), lambda i: (0, 0))
    return pl.pallas_call(
        body,
        out_shape=jax.ShapeDtypeStruct((L, D), BF16),
        grid=(L // tl,),
        in_specs=[pl.BlockSpec((tl, D), lambda i: (i, 0)), row, row, row],
        out_specs=pl.BlockSpec((tl, D), lambda i: (i, 0)),
        compiler_params=_params("parallel"),
        name=name,
    )(x, nw, sc, sh)


def mm_norm_bwd(a, b, x, nw, sc, dx_in, *, name):
    has_s = a.ndim == 3
    assert has_s == (b.ndim == 3)
    sr = a.shape[0] if has_s else 1
    M, K = a.shape[-2:]
    D = b.shape[-2]
    tm = _pick(M, MM_TILE // 2, 16, True)
    tk = _pick(K, MM_TILE, LANE, True)
    nk = K // tk

    def body(a_ref, b_ref, x_ref, nw_ref, sc_ref, dxi_ref, dx_ref, dnw_ref, dsc_ref, dsh_ref, acc):
        i, s, k = pl.program_id(0), pl.program_id(1), pl.program_id(2)

        @pl.when((i == 0) & (s == 0) & (k == 0))
        def _():
            dnw_ref[...] = jnp.zeros_like(dnw_ref)
            dsc_ref[...] = jnp.zeros_like(dsc_ref)
            dsh_ref[...] = jnp.zeros_like(dsh_ref)

        @pl.when((s == 0) & (k == 0))
        def _():
            acc[...] = jnp.zeros_like(acc)

        acc[...] += lax.dot_general(
            a_ref[...].astype(BF16), b_ref[...].astype(BF16), (((1,), (1,)), ((), ())), preferred_element_type=F32
        )

        @pl.when((s == sr - 1) & (k == nk - 1))
        def _():
            dh_v = acc[...]
            xv = x_ref[...]
            r = lax.rsqrt(jnp.mean(xv * xv, axis=-1, keepdims=True) + EPS)
            xhat = xv * r
            nw_v = nw_ref[...]
            n = xhat * nw_v
            dsh_ref[...] += jnp.sum(dh_v, axis=0, keepdims=True)
            dsc_ref[...] += jnp.sum(dh_v * n, axis=0, keepdims=True)
            dn = dh_v * (1.0 + sc_ref[...])
            dnw_ref[...] += jnp.sum(dn * xhat, axis=0, keepdims=True)
            dxhat = dn * nw_v
            dx_ref[...] = dxi_ref[...] + r * (dxhat - xhat * jnp.mean(dxhat * xhat, axis=-1, keepdims=True))

    row = pl.BlockSpec((1, D), lambda i, s, k: (0, 0))
    tile = pl.BlockSpec((tm, D), lambda i, s, k: (i, 0))
    vec = jax.ShapeDtypeStruct((1, D), F32)
    return pl.pallas_call(
        body,
        out_shape=(jax.ShapeDtypeStruct((M, D), F32), vec, vec, vec),
        grid=(M // tm, sr, nk),
        in_specs=[
            pl.BlockSpec((None, tm, tk) if has_s else (tm, tk), (lambda i, s, k: (s, i, k)) if has_s else (lambda i, s, k: (i, k))),
            pl.BlockSpec((None, D, tk) if has_s else (D, tk), (lambda i, s, k: (s, 0, k)) if has_s else (lambda i, s, k: (0, k))),
            tile, row, row, tile,
        ],
        out_specs=(tile, row, row, row),
        scratch_shapes=[pltpu.VMEM((tm, D), F32)],
        compiler_params=_params("arbitrary", "arbitrary", "arbitrary"),
        name=name,
    )(a, b, x, nw, sc, dx_in)


def gate_bwd(f, dx, gate, scale, *, name):
    L, D = f.shape
    tl = _pick(L, 512, 16)

    def body(f_ref, dx_ref, g_ref, df_ref, dg_ref, dfsum_ref):
        @pl.when(pl.program_id(0) == 0)
        def _():
            dg_ref[...] = jnp.zeros_like(dg_ref)
            dfsum_ref[...] = jnp.zeros_like(dfsum_ref)

        dxv = dx_ref[...]
        df = (scale * g_ref[...]) * dxv
        df_ref[...] = df.astype(df_ref.dtype)
        dfsum_ref[...] += jnp.sum(df, axis=0, keepdims=True)
        dg_ref[...] += scale * jnp.sum(f_ref[...] * dxv, axis=0, keepdims=True)

    tile = pl.BlockSpec((tl, D), lambda i: (i, 0))
    row = pl.BlockSpec((1, D), lambda i: (0, 0))
    vec = jax.ShapeDtypeStruct((1, D), F32)
    return pl.pallas_call(
        body,
        out_shape=(jax.ShapeDtypeStruct((L, D), BF16), vec, vec),
        grid=(L // tl,),
        in_specs=[tile, tile, row],
        out_specs=(tile, row, row),
        compiler_params=_params("arbitrary"),
        name=name,
    )(f, dx, gate)


def colsum(x, *, name):
    L, N = x.shape
    tl = _pick(L, 512, 8)

    def body(x_ref, o_ref):
        @pl.when(pl.program_id(0) == 0)
        def _():
            o_ref[...] = jnp.zeros_like(o_ref)

        o_ref[...] += jnp.sum(x_ref[...], axis=0, keepdims=True)

    return pl.pallas_call(
        body,
        out_shape=jax.ShapeDtypeStruct((1, N), F32),
        grid=(L // tl,),
        in_specs=[pl.BlockSpec((tl, N), lambda i: (i, 0))],
        out_specs=pl.BlockSpec((1, N), lambda i: (0, 0)),
        compiler_params=_params("arbitrary"),
        name=name,
    )(x)


def ffn_up(h, wgu, *, name):
    L, D = h.shape
    T = wgu.shape[-1]
    tm = _pick(L, 512, 16)

    def body(h_ref, w_ref, gu_ref, a_ref):
        hb = h_ref[...].astype(BF16)
        g = jnp.dot(hb, w_ref[0].astype(BF16), preferred_element_type=F32)
        u = jnp.dot(hb, w_ref[1].astype(BF16), preferred_element_type=F32)
        gu_ref[0] = g
        gu_ref[1] = u
        a_ref[...] = (_silu(g) * u).astype(a_ref.dtype)

    gu, a = pl.pallas_call(
        body,
        out_shape=(jax.ShapeDtypeStruct((2, 2, L, T), F32), jax.ShapeDtypeStruct((2, L, T), BF16)),
        grid=(2, L // tm),
        in_specs=[
            pl.BlockSpec((tm, D), lambda j, i: (i, 0)),
            pl.BlockSpec((2, None, D, T), lambda j, i: (0, j, 0, 0)),
        ],
        out_specs=(
            pl.BlockSpec((2, None, tm, T), lambda j, i: (0, j, i, 0)),
            pl.BlockSpec((None, tm, T), lambda j, i: (j, i, 0)),
        ),
        compiler_params=_params("parallel", "parallel"),
        name=name,
    )(h, wgu.reshape(2, 2, D, T))
    return gu.reshape(4, L, T), a


def ffn_down_bwd(df, wdn, gu, *, name):
    L, D = df.shape
    T = wdn.shape[1]
    tm = _pick(L, 512, 16)

    def body(df_ref, w_ref, gu_ref, d_ref):
        da = lax.dot_general(
            df_ref[...].astype(BF16), w_ref[...].astype(BF16), (((1,), (1,)), ((), ())), preferred_element_type=F32
        )
        g = gu_ref[0]
        d_ref[0] = (da * gu_ref[1] * _dsilu(g)).astype(d_ref.dtype)
        d_ref[1] = (da * _silu(g)).astype(d_ref.dtype)

    out = pl.pallas_call(
        body,
        out_shape=jax.ShapeDtypeStruct((2, 2, L, T), BF16),
        grid=(2, L // tm),
        in_specs=[
            pl.BlockSpec((tm, D), lambda j, i: (i, 0)),
            pl.BlockSpec((None, T, D), lambda j, i: (j, 0, 0)),
            pl.BlockSpec((2, None, tm, T), lambda j, i: (0, j, i, 0)),
        ],
        out_specs=pl.BlockSpec((2, None, tm, T), lambda j, i: (0, j, i, 0)),
        compiler_params=_params("parallel", "parallel"),
        name=name,
    )(df, wdn, gu.reshape(2, 2, L, T))
    return out.reshape(4, L, T)


def _shift_down(u, k, rows):
    if k == 0:
        return u
    return jnp.where(rows >= k, pltpu.roll(u, k, 0), 0.0)


def _shift_up(u, k, rows, n):
    if k == 0:
        return u
    return jnp.where(rows < n - k, pltpu.roll(u, n - k, 0), 0.0)


def _conv_pre(u, w_ref, b_ref, rows):
    pre = b_ref[...] + w_ref[CONV_WIDTH - 1 : CONV_WIDTH, :] * u
    for k in range(1, CONV_WIDTH):
        pre = pre + w_ref[CONV_WIDTH - 1 - k : CONV_WIDTH - k, :] * _shift_down(u, k, rows)
    return pre


def conv_fwd(zx, conv_w, conv_b, d_inner, *, name):
    L = zx.shape[0]
    C = conv_w.shape[1]
    tc = 256
    off = d_inner // tc

    def body(u_ref, w_ref, b_ref, o_ref):
        rows = lax.broadcasted_iota(jnp.int32, (L, tc), 0)
        o_ref[...] = _silu(_conv_pre(u_ref[...], w_ref, b_ref, rows))

    return pl.pallas_call(
        body,
        out_shape=jax.ShapeDtypeStruct((L, C), F32),
        grid=(C // tc,),
        in_specs=[
            pl.BlockSpec((L, tc), lambda j: (0, off + j)),
            pl.BlockSpec((CONV_WIDTH, tc), lambda j: (0, j)),
            pl.BlockSpec((1, tc), lambda j: (0, j)),
        ],
        out_specs=pl.BlockSpec((L, tc), lambda j: (0, j)),
        compiler_params=_params("parallel"),
        name=name,
    )(zx, conv_w, conv_b)


def conv_bwd(zx, conv_w, conv_b, dxbc, d_inner, *, name):
    L = zx.shape[0]
    C = conv_w.shape[1]
    tc = 256
    off = d_inner // tc

    def body(u_ref, w_ref, b_ref, d_ref, du_ref, dw_ref, db_ref):
        rows = lax.broadcasted_iota(jnp.int32, (L, tc), 0)
        u = u_ref[...]
        dpre = d_ref[...] * _dsilu(_conv_pre(u, w_ref, b_ref, rows))
        db_ref[...] = jnp.sum(dpre, axis=0, keepdims=True)
        du = w_ref[CONV_WIDTH - 1 : CONV_WIDTH, :] * dpre
        dw_ref[CONV_WIDTH - 1 : CONV_WIDTH, :] = jnp.sum(dpre * u, axis=0, keepdims=True)
        for k in range(1, CONV_WIDTH):
            j = CONV_WIDTH - 1 - k
            dw_ref[j : j + 1, :] = jnp.sum(dpre * _shift_down(u, k, rows), axis=0, keepdims=True)
            du = du + w_ref[j : j + 1, :] * _shift_up(dpre, k, rows, L)
        du_ref[...] = du

    return pl.pallas_call(
        body,
        out_shape=(
            jax.ShapeDtypeStruct((L, C), F32),
            jax.ShapeDtypeStruct((CONV_WIDTH, C), F32),
            jax.ShapeDtypeStruct((1, C), F32),
        ),
        grid=(C // tc,),
        in_specs=[
            pl.BlockSpec((L, tc), lambda j: (0, off + j)),
            pl.BlockSpec((CONV_WIDTH, tc), lambda j: (0, j)),
            pl.BlockSpec((1, tc), lambda j: (0, j)),
            pl.BlockSpec((L, tc), lambda j: (0, j)),
        ],
        out_specs=(
            pl.BlockSpec((L, tc), lambda j: (0, j)),
            pl.BlockSpec((CONV_WIDTH, tc), lambda j: (0, j)),
            pl.BlockSpec((1, tc), lambda j: (0, j)),
        ),
        compiler_params=_params("parallel"),
        name=name,
    )(zx, conv_w, conv_b, dxbc)


def _ssd_head(xs, dt, acs, tot, dsk, cb, bm, cm, prev):
    q = xs.shape[0]
    li = lax.broadcasted_iota(jnp.int32, (q, q), 0)
    si = lax.broadcasted_iota(jnp.int32, (q, q), 1)
    causal = li >= si
    lmat = jnp.exp(jnp.where(causal, acs - acs.T, -jnp.inf))
    xdt = xs * dt
    y = jnp.dot((cb * lmat).astype(BF16), xdt.astype(BF16), preferred_element_type=F32)
    y = y + lax.dot_general(
        (cm * jnp.exp(acs)).astype(BF16), prev.astype(BF16), (((1,), (1,)), ((), ())), preferred_element_type=F32
    )
    y = y + dsk * xs
    st = lax.dot_general(
        xdt.astype(BF16), (bm * jnp.exp(tot - acs)).astype(BF16), (((0,), (0,)), ((), ())), preferred_element_type=F32
    )
    return y, prev * jnp.exp(tot) + st


def _pick_lane(v, h):
    lanes = lax.broadcasted_iota(jnp.int32, v.shape, 1)
    return jnp.sum(jnp.where(lanes == h, v, 0.0), axis=1, keepdims=True)


def _tri_cols(cols, upper):
    q = cols[0].shape[0]
    assert 3 * len(cols) <= LANE
    li = lax.broadcasted_iota(jnp.int32, (q, q), 0)
    si = lax.broadcasted_iota(jnp.int32, (q, q), 1)
    tri = ((li <= si) if upper else (li >= si)).astype(BF16)
    lanes = lax.broadcasted_iota(jnp.int32, (q, LANE), 1)
    rhs = jnp.zeros((q, LANE), F32)
    for r, col in enumerate(cols):
        hi = col.astype(BF16).astype(F32)
        mid = (col - hi).astype(BF16).astype(F32)
        lo = col - hi - mid
        for t, term in enumerate((hi, mid, lo)):
            rhs = jnp.where(lanes == 3 * r + t, term, rhs)
    out = jnp.dot(tri, rhs.astype(BF16), preferred_element_type=F32)
    return [jnp.sum(jnp.where((lanes >= 3 * r) & (lanes < 3 * r + 3), out, 0.0), axis=1, keepdims=True)
            for r in range(len(cols))]


def _softplus(x):
    return jnp.maximum(x, 0.0) + jnp.log(1.0 + jnp.exp(-jnp.abs(x)))


def _ssd_specs(L, d_inner, H, nc, rev):
    R = H // SSM_GROUPS
    P, N, Q = SSM_HEADDIM, SSM_STATE, CHUNK
    ngrp = SSM_GROUPS

    def ci(c):
        return (nc - 1 - c) if rev else c

    b_off = d_inner // N
    c_off = b_off + ngrp
    xs = pl.BlockSpec((Q, R * P), lambda c, g: (ci(c), g))
    bm = pl.BlockSpec((Q, N), lambda c, g: (ci(c), b_off + g))
    cm = pl.BlockSpec((Q, N), lambda c, g: (ci(c), c_off + g))
    dt = pl.BlockSpec((Q, H), lambda c, g: (ci(c), 0))
    hv = pl.BlockSpec((1, H), lambda c, g: (0, 0))
    y = pl.BlockSpec((Q, R * P), lambda c, g: (ci(c), g))
    st = pl.BlockSpec((None, R * P, N), lambda c, g: (ci(c), g, 0))
    return R, xs, bm, cm, dt, hv, y, st


def ssd_fwd(xbc, dt_raw, dt_bias, a_log, d_skip, d_inner, *, name):
    L = xbc.shape[0]
    H = dt_raw.shape[1]
    nc = L // CHUNK
    P, N = SSM_HEADDIM, SSM_STATE
    R, xs_s, bm_s, cm_s, dt_s, hv_s, y_s, st_s = _ssd_specs(L, d_inner, H, nc, False)

    def body(xs_ref, bm_ref, cm_ref, dt_ref, bias_ref, alog_ref, dsk_ref, y_ref, st_ref, state):
        c = pl.program_id(0)
        g = pl.program_id(1)

        @pl.when(c == 0)
        def _():
            for r in range(R):
                state[g * R + r] = jnp.zeros((P, N), F32)

        dtb = _softplus(dt_ref[...] + bias_ref[...])
        a_all = -jnp.exp(alog_ref[...])
        bm, cm = bm_ref[...], cm_ref[...]
        cb = lax.dot_general(cm.astype(BF16), bm.astype(BF16), (((1,), (1,)), ((), ())), preferred_element_type=F32)
        dts = [_pick_lane(dtb, g * R + r) for r in range(R)]
        a_cols = [dts[r] * _pick_lane(a_all, g * R + r) for r in range(R)]
        acs = _tri_cols(a_cols, upper=False)
        prevs = [state[g * R + r] for r in range(R)]
        res = []
        for r in range(R):
            res.append(_ssd_head(
                xs_ref[:, r * P : (r + 1) * P],
                dts[r],
                jnp.broadcast_to(acs[r], (CHUNK, CHUNK)),
                jnp.sum(a_cols[r], axis=0, keepdims=True),
                _pick_lane(dsk_ref[...], g * R + r),
                cb,
                bm,
                cm,
                prevs[r],
            ))
        for r in range(R):
            st_ref[r * P : (r + 1) * P, :] = prevs[r]
            y_ref[:, r * P : (r + 1) * P] = res[r][0]
            state[g * R + r] = res[r][1]

    return pl.pallas_call(
        body,
        out_shape=(jax.ShapeDtypeStruct((L, d_inner), F32), jax.ShapeDtypeStruct((nc, H * P, N), F32)),
        grid=(nc, SSM_GROUPS),
        in_specs=[xs_s, bm_s, cm_s, dt_s, hv_s, hv_s, hv_s],
        out_specs=(y_s, st_s),
        scratch_shapes=[pltpu.VMEM((H, P, N), F32)],
        compiler_params=_params("arbitrary", "arbitrary"),
        name=name,
    )(xbc, xbc, xbc, dt_raw, dt_bias, a_log, d_skip)


def ssd_bwd(xbc, dt_raw, dt_bias, a_log, d_skip, states, dy, d_inner, *, name):
    L, C = xbc.shape
    H = dt_raw.shape[1]
    nc = L // CHUNK
    P, N, Q = SSM_HEADDIM, SSM_STATE, CHUNK
    R, xs_s, bm_s, cm_s, dt_s, hv_s, y_s, st_s = _ssd_specs(L, d_inner, H, nc, True)

    def body(xs_ref, bm_ref, cm_ref, dt_ref, bias_ref, alog_ref, dsk_ref, st_ref, dy_ref,
             dxs_ref, dbm_ref, dcm_ref, ddt_ref, dbias_ref, dalog_ref, ddsk_ref, dstate):
        c = pl.program_id(0)
        g = pl.program_id(1)

        @pl.when(c == 0)
        def _():
            for r in range(R):
                dstate[g * R + r] = jnp.zeros((P, N), F32)

        @pl.when((c == 0) & (g == 0))
        def _():
            dbias_ref[...] = jnp.zeros_like(dbias_ref)
            dalog_ref[...] = jnp.zeros_like(dalog_ref)
            ddsk_ref[...] = jnp.zeros_like(ddsk_ref)

        @pl.when(g == 0)
        def _():
            ddt_ref[...] = jnp.zeros_like(ddt_ref)

        pre = dt_ref[...] + bias_ref[...]
        dtb = _softplus(pre)
        a_all = -jnp.exp(alog_ref[...])
        lanes_q = lax.broadcasted_iota(jnp.int32, (Q, H), 1)
        lanes_1 = lax.broadcasted_iota(jnp.int32, (1, H), 1)
        bm = bm_ref[...]
        cm = cm_ref[...]
        nt = (((1,), (1,)), ((), ()))
        cb = lax.dot_general(cm.astype(BF16), bm.astype(BF16), nt, preferred_element_type=F32)
        dts = [_pick_lane(dtb, g * R + r) for r in range(R)]
        a_negs = [_pick_lane(a_all, g * R + r) for r in range(R)]
        a_cols = [dts[r] * a_negs[r] for r in range(R)]
        acs = _tri_cols(a_cols, upper=False)
        dbm = jnp.zeros((Q, N), F32)
        dcm = jnp.zeros((Q, N), F32)
        dcb = jnp.zeros((Q, Q), F32)
        dd_row = jnp.zeros((1, H), F32)
        dstates = [dstate[g * R + r] for r in range(R)]
        dprevs, ddts, dacs_cols, dtots = [], [], [], []
        for r in range(R):
            h = g * R + r
            args = (
                xs_ref[:, r * P : (r + 1) * P],
                dts[r],
                jnp.broadcast_to(acs[r], (Q, Q)),
                jnp.sum(a_cols[r], axis=0, keepdims=True),
                _pick_lane(dsk_ref[...], h),
                cb,
                bm,
                cm,
                st_ref[r * P : (r + 1) * P, :],
            )
            _, vjp = jax.vjp(_ssd_head, *args)
            dxs, ddt, dacs, dtot, dd, dcb_h, dbm_h, dcm_h, dprev = vjp((dy_ref[:, r * P : (r + 1) * P], dstates[r]))
            dxs_ref[:, r * P : (r + 1) * P] = dxs
            dprevs.append(dprev)
            ddts.append(ddt)
            dacs_cols.append(jnp.sum(dacs, axis=1, keepdims=True))
            dtots.append(dtot)
            dbm = dbm + dbm_h
            dcm = dcm + dcm_h
            dcb = dcb + dcb_h
            dd_row = dd_row + jnp.where(lanes_1 == h, dd, 0.0)
        for r in range(R):
            dstate[g * R + r] = dprevs[r]
        ddt_blk = jnp.zeros((Q, H), F32)
        da_row = jnp.zeros((1, H), F32)
        for r, da_col in enumerate(_tri_cols(dacs_cols, upper=True)):
            h = g * R + r
            da_col = da_col + dtots[r]
            ddt_blk = ddt_blk + jnp.where(lanes_q == h, ddts[r] + da_col * a_negs[r], 0.0)
            da_row = da_row + jnp.where(lanes_1 == h, jnp.sum(da_col * dts[r], axis=0, keepdims=True), 0.0)
        dcb16 = dcb.astype(BF16)
        dbm_ref[...] = dbm + lax.dot_general(dcb16, cm.astype(BF16), (((0,), (0,)), ((), ())), preferred_element_type=F32)
        dcm_ref[...] = dcm + jnp.dot(dcb16, bm.astype(BF16), preferred_element_type=F32)
        ddt_pre = ddt_blk * _sigmoid(pre)
        ddt_ref[...] += ddt_pre
        dbias_ref[...] += jnp.sum(ddt_pre, axis=0, keepdims=True)
        dalog_ref[...] += da_row * a_all
        ddsk_ref[...] += dd_row

    ngrp = SSM_GROUPS
    hrow = jax.ShapeDtypeStruct((1, H), F32)
    dxs, dbm, dcm, ddt, dbias, dalog, ddsk = pl.pallas_call(
        body,
        out_shape=(
            jax.ShapeDtypeStruct((L, d_inner), F32),
            jax.ShapeDtypeStruct((L, ngrp * N), F32),
            jax.ShapeDtypeStruct((L, ngrp * N), F32),
            jax.ShapeDtypeStruct((L, H), F32),
            hrow,
            hrow,
            hrow,
        ),
        grid=(nc, ngrp),
        in_specs=[xs_s, bm_s, cm_s, dt_s, hv_s, hv_s, hv_s, st_s, y_s],
        out_specs=(
            y_s,
            pl.BlockSpec((Q, N), lambda c, g: (nc - 1 - c, g)),
            pl.BlockSpec((Q, N), lambda c, g: (nc - 1 - c, g)),
            dt_s,
            hv_s,
            hv_s,
            hv_s,
        ),
        scratch_shapes=[pltpu.VMEM((H, P, N), F32)],
        compiler_params=_params("arbitrary", "arbitrary"),
        name=name,
    )(xbc, xbc, xbc, dt_raw, dt_bias, a_log, d_skip, states, dy)
    return jnp.concatenate([dxs, dbm, dcm], axis=1), ddt, dbias, dalog, ddsk


def gnorm_fwd(y, zx, nw, *, name):
    L, DI = y.shape
    gw = DI // SSM_GROUPS
    tl = _pick(L, 512, 16)

    def body(y_ref, z_ref, nw_ref, o_ref):
        yz = y_ref[...] * _silu(z_ref[...])
        r = lax.rsqrt(jnp.mean(yz * yz, axis=-1, keepdims=True) + EPS)
        o_ref[...] = ((yz * r) * nw_ref[...]).astype(o_ref.dtype)

    tile = pl.BlockSpec((tl, gw), lambda i, g: (i, g))
    return pl.pallas_call(
        body,
        out_shape=jax.ShapeDtypeStruct((L, DI), BF16),
        grid=(L // tl, SSM_GROUPS),
        in_specs=[tile, tile, pl.BlockSpec((1, gw), lambda i, g: (0, g))],
        out_specs=tile,
        compiler_params=_params("parallel", "parallel"),
        name=name,
    )(y, zx, nw)


def gnorm_bwd(y, zx, nw, dout, *, name):
    L, DI = y.shape
    gw = DI // SSM_GROUPS
    tl = _pick(L, 512, 16)

    def body(y_ref, z_ref, nw_ref, do_ref, dy_ref, dz_ref, dnw_ref):
        @pl.when(pl.program_id(1) == 0)
        def _():
            dnw_ref[...] = jnp.zeros_like(dnw_ref)

        yv = y_ref[...]
        zv = z_ref[...]
        sz = _silu(zv)
        yz = yv * sz
        r = lax.rsqrt(jnp.mean(yz * yz, axis=-1, keepdims=True) + EPS)
        n = yz * r
        dov = do_ref[...]
        dnw_ref[...] += jnp.sum(dov * n, axis=0, keepdims=True)
        dn = dov * nw_ref[...]
        dyz = r * (dn - n * jnp.mean(dn * n, axis=-1, keepdims=True))
        dy_ref[...] = dyz * sz
        dz_ref[...] = dyz * yv * _dsilu(zv)

    tile = pl.BlockSpec((tl, gw), lambda g, i: (i, g))
    row = pl.BlockSpec((1, gw), lambda g, i: (0, g))
    return pl.pallas_call(
        body,
        out_shape=(
            jax.ShapeDtypeStruct((L, DI), F32),
            jax.ShapeDtypeStruct((L, DI), F32),
            jax.ShapeDtypeStruct((1, DI), F32),
        ),
        grid=(SSM_GROUPS, L // tl),
        in_specs=[tile, tile, row, tile],
        out_specs=(tile, tile, row),
        compiler_params=_params("parallel", "arbitrary"),
        name=name,
    )(y, zx, nw, dout)


def _attn_head(q, kp, kc, vp, vc, sink, has_prev):
    rows, w = q.shape[0], kc.shape[0]
    nt = (((1,), (1,)), ((), ()))
    qb = q.astype(BF16)
    sc = lax.dot_general(qb, kc.astype(BF16), nt, preferred_element_type=F32) * HEAD_DIM ** -0.5
    sp = lax.dot_general(qb, kp.astype(BF16), nt, preferred_element_type=F32) * HEAD_DIM ** -0.5
    ii = jnp.bitwise_and(lax.broadcasted_iota(jnp.int32, (rows, w), 0), w - 1)
    jj = lax.broadcasted_iota(jnp.int32, (rows, w), 1)
    lc = jnp.where(jj <= ii, sc, -jnp.inf)
    lp = jnp.where((jj > ii) & has_prev, sp, -jnp.inf)
    m = jnp.maximum(jnp.maximum(jnp.max(lc, axis=1, keepdims=True), jnp.max(lp, axis=1, keepdims=True)), sink)
    m = lax.stop_gradient(m)
    pc = jnp.exp(lc - m)
    pp = jnp.exp(lp - m)
    denom = jnp.sum(pc, axis=1, keepdims=True) + jnp.sum(pp, axis=1, keepdims=True) + jnp.exp(sink - m)
    o = jnp.dot((pc / denom).astype(BF16), vc.astype(BF16), preferred_element_type=F32)
    return o + jnp.dot((pp / denom).astype(BF16), vp.astype(BF16), preferred_element_type=F32)


def attn_fwd(q, kv, sinks, *, name):
    L, DQ = q.shape
    heads = DQ // HEAD_DIM
    rep = heads // KV_HEADS
    nb = L // WINDOW
    kw = KV_HEADS * HEAD_DIM
    W, HD = WINDOW, HEAD_DIM

    def body(q_ref, kp_ref, kc_ref, vp_ref, vc_ref, s_ref, o_ref):
        has_prev = pl.program_id(0) > 0
        for kh in range(KV_HEADS):
            ks = slice(kh * HD, (kh + 1) * HD)
            hs = [kh * rep + rr for rr in range(rep)]
            o = _attn_head(
                jnp.concatenate([q_ref[:, h * HD : (h + 1) * HD] for h in hs], axis=0),
                kp_ref[:, ks], kc_ref[:, ks], vp_ref[:, ks], vc_ref[:, ks],
                jnp.concatenate([jnp.broadcast_to(s_ref[:, h : h + 1], (W, 1)) for h in hs], axis=0), has_prev,
            )
            for rr, h in enumerate(hs):
                o_ref[:, h * HD : (h + 1) * HD] = o[rr * W : (rr + 1) * W].astype(o_ref.dtype)

    return pl.pallas_call(
        body,
        out_shape=jax.ShapeDtypeStruct((L, DQ), BF16),
        grid=(nb,),
        in_specs=[
            pl.BlockSpec((W, DQ), lambda n: (n, 0)),
            pl.BlockSpec((W, kw), lambda n: (jnp.maximum(n - 1, 0), 0)),
            pl.BlockSpec((W, kw), lambda n: (n, 0)),
            pl.BlockSpec((W, kw), lambda n: (jnp.maximum(n - 1, 0), 1)),
            pl.BlockSpec((W, kw), lambda n: (n, 1)),
            pl.BlockSpec((1, heads), lambda n: (0, 0)),
        ],
        out_specs=pl.BlockSpec((W, DQ), lambda n: (n, 0)),
        compiler_params=_params("parallel"),
        name=name,
    )(q, kv, kv, kv, kv, sinks)


def attn_bwd(q, kv, sinks, do, dkv_in, *, name):
    L, DQ = q.shape
    heads = DQ // HEAD_DIM
    rep = heads // KV_HEADS
    nb = L // WINDOW
    kw = KV_HEADS * HEAD_DIM
    W, HD = WINDOW, HEAD_DIM

    def blk(n):
        return jnp.minimum(n, nb - 1)

    def prev(n):
        return jnp.maximum(blk(n) - 1, 0)

    def outb(n):
        return jnp.maximum(n - 1, 0)

    def body(q_ref, kp_ref, kc_ref, vp_ref, vc_ref, s_ref, do_ref, dki_ref, dvi_ref,
             dq_ref, dk_ref, dv_ref, ds_ref, dk_cur, dv_cur):
        n = pl.program_id(0)
        has_prev = n > 0

        @pl.when(n == 0)
        def _():
            ds_ref[...] = jnp.zeros_like(ds_ref)
            dk_cur[...] = jnp.zeros_like(dk_cur)
            dv_cur[...] = jnp.zeros_like(dv_cur)

        @pl.when(n == nb)
        def _():
            dk_ref[...] = dki_ref[...] + dk_cur[...]
            dv_ref[...] = dvi_ref[...] + dv_cur[...]

        @pl.when(n < nb)
        def _():
            lanes = lax.broadcasted_iota(jnp.int32, (1, heads), 1)
            ds_row = jnp.zeros((1, heads), F32)
            for kh in range(KV_HEADS):
                ks = slice(kh * HD, (kh + 1) * HD)
                hs = [kh * rep + rr for rr in range(rep)]
                _, vjp = jax.vjp(
                    functools.partial(_attn_head, has_prev=has_prev),
                    jnp.concatenate([q_ref[:, h * HD : (h + 1) * HD] for h in hs], axis=0),
                    kp_ref[:, ks], kc_ref[:, ks], vp_ref[:, ks], vc_ref[:, ks],
                    jnp.concatenate([jnp.broadcast_to(s_ref[:, h : h + 1], (W, 1)) for h in hs], axis=0),
                )
                dq, dkp, dkc, dvp, dvc, dsk = vjp(
                    jnp.concatenate([do_ref[:, h * HD : (h + 1) * HD] for h in hs], axis=0))
                for rr, h in enumerate(hs):
                    dq_ref[:, h * HD : (h + 1) * HD] = dq[rr * W : (rr + 1) * W]
                    ds_row = ds_row + jnp.where(lanes == h, jnp.sum(dsk[rr * W : (rr + 1) * W], axis=0, keepdims=True), 0.0)
                dk_ref[:, ks] = dki_ref[:, ks] + dk_cur[:, ks] + dkp
                dv_ref[:, ks] = dvi_ref[:, ks] + dv_cur[:, ks] + dvp
                dk_cur[:, ks] = dkc
                dv_cur[:, ks] = dvc
            ds_ref[...] += ds_row

    dq, dk, dv, ds = pl.pallas_call(
        body,
        out_shape=(
            jax.ShapeDtypeStruct((L, DQ), F32),
            jax.ShapeDtypeStruct((L, kw), F32),
            jax.ShapeDtypeStruct((L, kw), F32),
            jax.ShapeDtypeStruct((1, heads), F32),
        ),
        grid=(nb + 1,),
        in_specs=[
            pl.BlockSpec((W, DQ), lambda n: (blk(n), 0)),
            pl.BlockSpec((W, kw), lambda n: (prev(n), 0)),
            pl.BlockSpec((W, kw), lambda n: (blk(n), 0)),
            pl.BlockSpec((W, kw), lambda n: (prev(n), 1)),
            pl.BlockSpec((W, kw), lambda n: (blk(n), 1)),
            pl.BlockSpec((1, heads), lambda n: (0, 0)),
            pl.BlockSpec((W, DQ), lambda n: (blk(n), 0)),
            pl.BlockSpec((W, kw), lambda n: (outb(n), 0)),
            pl.BlockSpec((W, kw), lambda n: (outb(n), 1)),
        ],
        out_specs=(
            pl.BlockSpec((W, DQ), lambda n: (blk(n), 0)),
            pl.BlockSpec((W, kw), lambda n: (outb(n), 0)),
            pl.BlockSpec((W, kw), lambda n: (outb(n), 0)),
            pl.BlockSpec((1, heads), lambda n: (0, 0)),
        ),
        scratch_shapes=[pltpu.VMEM((W, kw), F32), pltpu.VMEM((W, kw), F32)],
        compiler_params=_params("arbitrary"),
        name=name,
    )(q, kv, kv, kv, kv, sinks, do, dkv_in, dkv_in)
    return dq, jnp.concatenate([dk, dv], axis=1), ds


def final_loss(x, fw, target, *, name):
    L, D = x.shape
    tl = _pick(L, 512, 8)

    def body(x_ref, fw_ref, t_ref, loss_ref, dx_ref, dfw_ref):
        @pl.when(pl.program_id(0) == 0)
        def _():
            loss_ref[...] = jnp.zeros_like(loss_ref)
            dfw_ref[...] = jnp.zeros_like(dfw_ref)

        xv = x_ref[...]
        fwv = fw_ref[...]
        r = lax.rsqrt(jnp.mean(xv * xv, axis=-1, keepdims=True) + EPS)
        xhat = xv * r
        err = xhat * fwv - t_ref[...]
        loss_ref[...] += 0.5 * jnp.sum(jnp.mean(err * err, axis=-1, keepdims=True), axis=0, keepdims=True)
        dy = err * (1.0 / D)
        dfw_ref[...] += jnp.sum(dy * xhat, axis=0, keepdims=True)
        dxhat = dy * fwv
        dx_ref[...] = r * (dxhat - xhat * jnp.mean(dxhat * xhat, axis=-1, keepdims=True))

    tile = pl.BlockSpec((tl, D), lambda i: (i, 0))
    row = pl.BlockSpec((1, D), lambda i: (0, 0))
    return pl.pallas_call(
        body,
        out_shape=(
            jax.ShapeDtypeStruct((1, 1), F32),
            jax.ShapeDtypeStruct((L, D), F32),
            jax.ShapeDtypeStruct((1, D), F32),
        ),
        grid=(L // tl,),
        in_specs=[tile, row, tile],
        out_specs=(pl.BlockSpec((1, 1), lambda i: (0, 0)), tile, row),
        compiler_params=_params("arbitrary"),
        name=name,
    )(x, fw, target)


def outer8(ct, d, *, name):
    D, B = ct.shape
    S, _, N = d.shape
    tm = _pick(D, 512, 8)
    tn = _pick(N, 256, LANE)

    def body(c_ref, d_ref, o_ref):
        acc = c_ref[:, 0:1] * d_ref[0:1, :]
        for b in range(1, B):
            acc = acc + c_ref[:, b : b + 1] * d_ref[b : b + 1, :]
        o_ref[...] = acc

    return pl.pallas_call(
        body,
        out_shape=jax.ShapeDtypeStruct((S, D, N), F32),
        grid=(S, D // tm, N // tn),
        in_specs=[
            pl.BlockSpec((tm, B), lambda s, i, j: (i, 0)),
            pl.BlockSpec((None, B, tn), lambda s, i, j: (s, 0, j)),
        ],
        out_specs=pl.BlockSpec((None, tm, tn), lambda s, i, j: (s, i, j)),
        compiler_params=_params("parallel", "parallel", "parallel"),
        name=name,
    )(ct, d)


def reduce8(g, *, name):
    nd, R, N = g.shape

    def body(g_ref, o_ref):
        acc = g_ref[0]
        for b in range(1, nd):
            acc = acc + g_ref[b]
        o_ref[...] = acc

    return pl.pallas_call(
        body,
        out_shape=jax.ShapeDtypeStruct((R, N), F32),
        name=name,
    )(g)


def _as3(a):
    if a.ndim == 1:
        return a.reshape(1, 1, -1)
    if a.ndim == 2:
        return a.reshape((1,) + a.shape)
    return a.reshape((-1,) + a.shape[-2:])


def adamw(w, g, m, v, *, name):
    shape = w.shape
    w3, g3, m3, v3 = _as3(w), _as3(g), _as3(m), _as3(v)
    B, R, C = w3.shape
    tr = _pick(R, max(8, (1 << 19) // max(C, 1) // 8 * 8), 8)

    def body(w_ref, g_ref, m_ref, v_ref, d_ref, nm_ref, nv_ref):
        gv = g_ref[...]
        mn = ADAM_B1 * m_ref[...] + (1.0 - ADAM_B1) * gv
        vn = ADAM_B2 * v_ref[...] + (1.0 - ADAM_B2) * (gv * gv)
        m_hat = mn / (1.0 - ADAM_B1 ** ADAM_STEP)
        v_hat = vn / (1.0 - ADAM_B2 ** ADAM_STEP)
        d_ref[...] = -ADAM_LR * (m_hat / (jnp.sqrt(v_hat) + ADAM_EPS) + ADAM_WD * w_ref[...])
        nm_ref[...] = mn
        nv_ref[...] = vn

    tile = pl.BlockSpec((None, tr, C), lambda b, i: (b, i, 0))
    sds = jax.ShapeDtypeStruct((B, R, C), F32)
    d, nm, nv = pl.pallas_call(
        body,
        out_shape=(sds, sds, sds),
        grid=(B, R // tr),
        in_specs=[tile, tile, tile, tile],
        out_specs=(tile, tile, tile),
        compiler_params=_params("parallel", "parallel"),
        name=name,
    )(w3, g3, m3, v3)
    return d.reshape(shape), nm.reshape(shape), nv.reshape(shape)


def _place():
    return lax.axis_index("x"), lax.axis_index("y"), lax.axis_index("c")


def _flip(v, bit):
    return (1 - v) if bit else v


def ag8(v, *, act=None, after=None, name):
    R, N = v.shape
    extra = [] if after is None else [after]

    def body(*refs):
        v_ref = refs[0]
        out_ref, stage, send_sems, recv_sems = refs[1 + len(extra):]
        x, y, c = _place()
        me = 4 * x + 2 * y + c
        val = v_ref[...]
        if act is not None:
            val = act(val)
        stage[...] = val
        out_ref[me] = val
        sends = []
        for k in range(1, N_DEV):
            px, py, pc = _flip(x, k & 4), _flip(y, k & 2), _flip(c, k & 1)
            cp = pltpu.make_async_remote_copy(
                src_ref=stage, dst_ref=out_ref.at[me], send_sem=send_sems.at[k - 1], recv_sem=recv_sems.at[k - 1],
                device_id=(px, py, pc), device_id_type=MESH,
            )
            cp.start()
            sends.append(cp)
        for k in range(1, N_DEV):
            px, py, pc = _flip(x, k & 4), _flip(y, k & 2), _flip(c, k & 1)
            pltpu.make_async_remote_copy(
                src_ref=stage, dst_ref=out_ref.at[4 * px + 2 * py + pc], send_sem=send_sems.at[k - 1],
                recv_sem=recv_sems.at[k - 1], device_id=(px, py, pc), device_id_type=MESH,
            ).wait_recv()
        for cp in sends:
            cp.wait_send()

    return pl.pallas_call(
        body,
        out_shape=jax.ShapeDtypeStruct((N_DEV, R, N), F32),
        in_specs=[pl.BlockSpec(memory_space=pltpu.VMEM)] + [_ANY] * len(extra),
        out_specs=pl.BlockSpec(memory_space=pltpu.VMEM),
        scratch_shapes=[
            pltpu.VMEM((R, N), F32),
            pltpu.SemaphoreType.DMA((N_DEV - 1,)),
            pltpu.SemaphoreType.DMA((N_DEV - 1,)),
        ],
        name=name,
    )(v, *extra)


def _other_chips(x, y):
    chips = [(1 - x, y), (x, 1 - y), (1 - x, 1 - y)]
    return chips, [2 * px + py for px, py in chips]


_HBM = pl.BlockSpec(memory_space=pltpu.HBM)


_SEM = pl.BlockSpec(memory_space=pltpu.SEMAPHORE)
_ANY = pl.BlockSpec(memory_space=pl.ANY)
_EFFECT = pltpu.SideEffectType.DATAFLOW_SIDE_EFFECTING


def _gather_copies(srcs, lands, send_sems, recv_sems):
    x, y, c = _place()
    k_me = 2 * x + y
    chips, kidx = _other_chips(x, y)
    cps = []
    for w in range(len(srcs)):
        for j, (px, py) in enumerate(chips):
            def copy(dst, w=w, j=j, px=px, py=py):
                return pltpu.make_async_remote_copy(
                    src_ref=srcs[w].at[:, c], dst_ref=dst, send_sem=send_sems.at[3 * w + j],
                    recv_sem=recv_sems.at[3 * w + j], device_id=(px, py, c), device_id_type=MESH,
                )
            cps.append((copy(lands[w].at[:, k_me, c]), copy(lands[w].at[:, kidx[j], c])))
    return cps


def _fill_copies(srcs, lands, send_sems, recv_sems):
    x, y, c = _place()
    k_me = 2 * x + y
    _, kidx = _other_chips(x, y)
    sib = (x, y, 1 - c)
    cps = []
    for w in range(len(srcs)):
        own = pltpu.make_async_remote_copy(
            src_ref=srcs[w], dst_ref=lands[w].at[:, k_me], send_sem=send_sems.at[4 * w + 3], recv_sem=recv_sems.at[4 * w + 3],
            device_id=sib, device_id_type=MESH,
        )
        cps.append((own, own))
        for j in range(3):
            def copy(half, w=w, j=j):
                part = lands[w].at[:, kidx[j], half]
                return pltpu.make_async_remote_copy(
                    src_ref=part, dst_ref=part, send_sem=send_sems.at[4 * w + j], recv_sem=recv_sems.at[4 * w + j],
                    device_id=sib, device_id_type=MESH,
                )
            cps.append((copy(c), copy(1 - c)))
    return cps


def _sibling_copies(srcs, lands, send_sems, recv_sems):
    x, y, c = _place()
    cps = []
    for w in range(len(srcs)):
        cp = pltpu.make_async_remote_copy(
            src_ref=srcs[w].at[:, :, 1 - c], dst_ref=lands[w], send_sem=send_sems.at[w], recv_sem=recv_sems.at[w],
            device_id=(x, y, 1 - c), device_id_type=MESH,
        )
        cps.append((cp, cp))
    return cps


def _reduce_copies(srcs, lands, send_sems, recv_sems):
    x, y, c = _place()
    chips, kidx = _other_chips(x, y)
    cps = []
    for w in range(len(srcs)):
        for j, (px, py) in enumerate(chips):
            cp = pltpu.make_async_remote_copy(
                src_ref=srcs[w].at[:, kidx[j]], dst_ref=lands[w].at[j], send_sem=send_sems.at[3 * w + j],
                recv_sem=recv_sems.at[3 * w + j], device_id=(px, py, c), device_id_type=MESH,
            )
            cps.append((cp, cp))
    return cps


def split_start(copies, srcs, land_shapes, after, *, name, lands=None, per=3):
    n = len(srcs)

    def body(*refs):
        src_refs, land_refs = refs[:n], refs[n : 2 * n]
        send_sems, recv_sems = refs[2 * n + 1], refs[2 * n + 2]
        token = refs[-1]
        for cp, _ in copies(src_refs, land_refs, send_sems, recv_sems):
            cp.start()
        token[...] = jnp.zeros_like(token)

    if lands is None:
        lands = [lax.empty(sh, s.dtype) for sh, s in zip(land_shapes, srcs)]
    land_shapes = [a.shape for a in lands]
    lands = [pltpu.with_memory_space_constraint(a, pltpu.HBM) for a in lands]
    srcs = [pltpu.with_memory_space_constraint(s, pltpu.HBM) for s in srcs]
    out = pl.pallas_call(
        body,
        out_shape=(
            pltpu.SemaphoreType.DMA((per * n,)), pltpu.SemaphoreType.DMA((per * n,)),
            *[pltpu.HBM(s.shape, s.dtype) for s in srcs],
            *[pltpu.HBM(sh, s.dtype) for sh, s in zip(land_shapes, srcs)],
            jax.ShapeDtypeStruct((8, LANE), F32),
        ),
        in_specs=[_HBM] * (2 * n) + [_ANY],
        out_specs=(_SEM, _SEM, *([_HBM] * (2 * n)), pl.BlockSpec(memory_space=pltpu.VMEM)),
        input_output_aliases={i: 2 + i for i in range(2 * n)},
        compiler_params=pltpu.CompilerParams(has_side_effects=_EFFECT),
        name=name,
    )(*srcs, *lands, after)
    return out[0], out[1], list(out[2 : 2 + n]), list(out[2 + n : 2 + 2 * n]), out[-1]


def split_wait(copies, send_sems, recv_sems, srcs, lands, after, *, name):
    n = len(srcs)

    def body(*refs):
        src_refs, land_refs = refs[:n], refs[n : 2 * n]
        send_ref, recv_ref = refs[2 * n], refs[2 * n + 1]
        for sent, arrives in copies(src_refs, land_refs, send_ref, recv_ref):
            sent.wait_send()
            arrives.wait_recv()

    out = pl.pallas_call(
        body,
        out_shape=tuple(pltpu.HBM(a.shape, a.dtype) for a in list(srcs) + list(lands)),
        in_specs=[_HBM] * (2 * n) + [_SEM, _SEM, _ANY],
        out_specs=tuple([_HBM] * (2 * n)),
        input_output_aliases={i: i for i in range(2 * n)},
        compiler_params=pltpu.CompilerParams(has_side_effects=_EFFECT),
        name=name,
    )(*srcs, *lands, send_sems, recv_sems, after)
    return list(out[:n]), list(out[n:])


def rs_share(halves, *, name):
    n = len(halves)

    def body(*refs):
        outs = refs[n : 2 * n]
        send_sems, recv_sems = refs[2 * n :]
        x, y, c = _place()
        cps = []
        for w in range(n):
            cp = pltpu.make_async_remote_copy(
                src_ref=outs[w].at[:, c], dst_ref=outs[w].at[:, c], send_sem=send_sems.at[w], recv_sem=recv_sems.at[w],
                device_id=(x, y, 1 - c), device_id_type=MESH,
            )
            cp.start()
            cps.append(cp)
        for w, cp in enumerate(cps):
            cp.wait_send()
            pltpu.make_async_remote_copy(
                src_ref=outs[w].at[:, c], dst_ref=outs[w].at[:, 1 - c], send_sem=send_sems.at[w], recv_sem=recv_sems.at[w],
                device_id=(x, y, 1 - c), device_id_type=MESH,
            ).wait_recv()

    return pl.pallas_call(
        body,
        out_shape=tuple(jax.ShapeDtypeStruct(h.shape, h.dtype) for h in halves),
        in_specs=[_HBM] * n,
        out_specs=tuple([_HBM] * n),
        scratch_shapes=[pltpu.SemaphoreType.DMA((n,)), pltpu.SemaphoreType.DMA((n,))],
        input_output_aliases={w: w for w in range(n)},
        name=name,
    )(*halves)


def _row_tile(R, C):
    return _pick(R, max(16, (1 << 19) // C // 16 * 16), 16)


def cast_matrix(a3, m0, *, name):
    _, R, C = a3.shape
    tr = _row_tile(R, C)

    def body(a_ref, o_ref):
        o_ref[...] = a_ref[...].astype(o_ref.dtype)

    return pl.pallas_call(
        body,
        out_shape=jax.ShapeDtypeStruct((1, R, C), BF16),
        grid=(R // tr,),
        in_specs=[pl.BlockSpec((None, tr, C), lambda i: (m0, i, 0))],
        out_specs=pl.BlockSpec((None, tr, C), lambda i: (0, i, 0)),
        compiler_params=_params("parallel"),
        name=name,
    )(a3)


def _my_core():
    return lax.axis_index("c")


def _my_chip():
    return 2 * lax.axis_index("x") + lax.axis_index("y")


def rs_add_pair(g, r, *, name):
    M, K, _, R, C = g.shape
    tr = _row_tile(R, C)

    def body(g_ref, r_ref, o_ref):
        o_ref[...] = (g_ref[...].astype(F32) + r_ref[...].astype(F32)).astype(o_ref.dtype)

    blk = pl.BlockSpec((None, None, tr, C), lambda m, k, i: (m, k, i, 0))
    return pl.pallas_call(
        body,
        out_shape=jax.ShapeDtypeStruct((M, K, R, C), BF16),
        grid=(M, K, R // tr),
        in_specs=[pl.BlockSpec((None, None, None, tr, C), lambda m, k, i: (m, k, _my_core(), i, 0)), blk],
        out_specs=blk,
        compiler_params=_params("parallel", "parallel", "parallel"),
        name=name,
    )(g, r)


def rs_add_final(g, r, t, full, m0, *, name):
    M, K, _, R, C = g.shape
    tr = _row_tile(R, C)

    def body(g_ref, r_ref, t_ref, full_ref, o_ref):
        acc = g_ref[...].astype(F32) + r_ref[...].astype(F32)
        for j in range(3):
            acc = acc + t_ref[j].astype(F32)
        o_ref[...] = acc

    return pl.pallas_call(
        body,
        out_shape=jax.ShapeDtypeStruct(full.shape, full.dtype),
        grid=(M, R // tr),
        in_specs=[
            pl.BlockSpec((None, None, None, tr, C), lambda m, i: (m, _my_chip(), _my_core(), i, 0)),
            pl.BlockSpec((None, None, tr, C), lambda m, i: (m, _my_chip(), i, 0)),
            pl.BlockSpec((3, None, tr, C), lambda m, i: (0, m, i, 0)),
            _ANY,
        ],
        out_specs=pl.BlockSpec((None, None, tr, C), lambda m, i: (m0 + m, _my_core(), i, 0)),
        input_output_aliases={3: 0},
        compiler_params=_params("parallel", "parallel"),
        name=name,
    )(g, r, t, full)


WEIGHTS = ["ffn_norm_w", "ffn_w_gu", "ffn_w_down", "mod_w", "mod_b", "mix_norm_w", "ssm_w_in", "ssm_conv_w", "ssm_conv_b",
           "ssm_dt_bias", "ssm_a_log", "ssm_d", "ssm_norm_w", "ssm_w_out", "kv_norm_w", "kv_mod_w", "kv_mod_b", "w_kv", "b_kv",
           "attn_w_q", "attn_b_q", "attn_sinks", "attn_w_o", "attn_b_o", "final_norm_w"]
GATHERED = ["ffn_w_gu", "ffn_w_down", "ssm_w_in", "ssm_w_out", "w_kv", "attn_w_q", "attn_w_o"]
COLUMN_PARALLEL = ["mod_w", "kv_mod_w"]
SMALL_SHARDED = ["ffn_norm_w", "ssm_conv_w", "ssm_conv_b", "ssm_norm_w"]
SMALL = [n for n in WEIGHTS if n not in GATHERED and n not in COLUMN_PARALLEL]


def _row_halves(a):
    a = a.reshape((-1,) + a.shape[-2:])
    return a.reshape(a.shape[0], 2, a.shape[1] // 2, a.shape[2])


def _pack(arrs, rows=8):
    flat = jnp.concatenate([a.reshape(-1) for a in arrs])
    n = flat.shape[0]
    pad = (-n) % (rows * LANE)
    return jnp.pad(flat, (0, pad)).reshape(rows, -1), n


def _unpack(flat, like):
    out, o = [], 0
    for s in like:
        k = 1
        for d in s:
            k *= d
        out.append(flat[o : o + k].reshape(s))
        o += k
    return out


def kernel(x, c, ffn_norm_w, ffn_w_gu, ffn_w_down, mod_w, mod_b, mix_norm_w, ssm_w_in, ssm_conv_w, ssm_conv_b, ssm_dt_bias, ssm_a_log, ssm_d, ssm_norm_w, ssm_w_out, kv_norm_w, kv_mod_w, kv_mod_b, w_kv, b_kv, attn_w_q, attn_b_q, attn_sinks, attn_w_o, attn_b_o, final_norm_w, loss_target, m_ffn_norm_w, m_ffn_w_gu, m_ffn_w_down, m_mod_w, m_mod_b, m_mix_norm_w, m_ssm_w_in, m_ssm_conv_w, m_ssm_conv_b, m_ssm_dt_bias, m_ssm_a_log, m_ssm_d, m_ssm_norm_w, m_ssm_w_out, m_kv_norm_w, m_kv_mod_w, m_kv_mod_b, m_w_kv, m_b_kv, m_attn_w_q, m_attn_b_q, m_attn_sinks, m_attn_w_o, m_attn_b_o, m_final_norm_w, v_ffn_norm_w, v_ffn_w_gu, v_ffn_w_down, v_mod_w, v_mod_b, v_mix_norm_w, v_ssm_w_in, v_ssm_conv_w, v_ssm_conv_b, v_ssm_dt_bias, v_ssm_a_log, v_ssm_d, v_ssm_norm_w, v_ssm_w_out, v_kv_norm_w, v_kv_mod_w, v_kv_mod_b, v_w_kv, v_b_kv, v_attn_w_q, v_attn_b_q, v_attn_sinks, v_attn_w_o, v_attn_b_o, v_final_norm_w):
    env = dict(locals())
    W = {n: env[n] for n in WEIGHTS}
    MOM = {n: env["m_" + n] for n in WEIGHTS}
    VAR = {n: env["v_" + n] for n in WEIGHTS}

    ax, ay, ac = _place()
    kme = 2 * ax + ay
    me = 4 * ax + 2 * ay + ac

    xs = x[0]
    target = loss_target[0]
    L, D = xs.shape
    depth, n_a = ffn_w_gu.shape[0], ssm_w_in.shape[0]
    n_b = depth - n_a
    T = ffn_w_gu.shape[-1]
    DI = ssm_w_out.shape[1] * N_CHIPS
    CI = ssm_w_in.shape[2]
    CC = ssm_conv_w.shape[2] * N_CHIPS
    MW = mod_w.shape[2]
    KW = kv_mod_w.shape[1]
    KVD = w_kv.shape[1]

    def chip_cols(a, width):
        return lax.dynamic_slice_in_dim(a, kme * width, width, axis=a.ndim - 1)

    def ffn_items(i, j):
        return [("ffn_w_gu", 2 * i + j, ffn_w_gu[i, j]), ("ffn_w_down", 2 * i + j, ffn_w_down[i, j])]

    def mix_items(i):
        if i < n_a:
            return [("ssm_w_in", i, ssm_w_in[i]), ("ssm_w_out", i, ssm_w_out[i])]
        return [("attn_w_q", i - n_a, attn_w_q[i - n_a]), ("attn_w_o", i - n_a, attn_w_o[i - n_a])]

    def layer_items(i, order):
        kv_items = [("w_kv", 0, w_kv)] if i == n_a else []
        if order == "fwd":
            return kv_items + ffn_items(i, 0) + mix_items(i) + ffn_items(i, 1)
        return ffn_items(i, 1) + mix_items(i) + ffn_items(i, 0) + kv_items

    def sub_items(sub):
        return {"F": ffn_items, "M": mix_items, "KV": lambda: [("w_kv", 0, w_kv)]}[sub[0]](*sub[1:])

    subs = []
    for i in range(depth):
        subs += ([("KV",)] if i == n_a else []) + [("F", i, 0), ("M", i), ("F", i, 1)]
    assert n_a >= 2 and depth >= 2
    cuts = [0, 1, 2, 4, 6] + [3 * (i + 1) + (1 if i >= n_a else 0) for i in range(2, depth)]
    fwd_plan = [subs[a:b] for a, b in zip(cuts[:-1], cuts[1:])]
    fwd_stages = [[it for sub in stage for it in sub_items(sub)] for stage in fwd_plan]
    bwd_stages = [layer_items(i, "bwd") for i in range(depth - 1, 0, -1)] + [ffn_items(0, 1) + mix_items(0), ffn_items(0, 0)]

    gw, inflight = {}, {}

    def chips_begin(s, after):
        keys = [(n, m0) for n, m0, _ in fwd_stages[s]]
        shards = [
            _row_halves(cast_matrix(W[n].reshape((-1,) + W[n].shape[-2:]), m0, name=f"cast_{n}_{m0}"))
            for n, m0, _ in fwd_stages[s]
        ]
        land_shapes = [(sh.shape[0], N_CHIPS) + sh.shape[1:] for sh in shards]
        ssem, rsem, srcs, lands, token = split_start(_gather_copies, shards, land_shapes, after, name=f"gather_start_{s}")
        inflight[s] = (keys, ssem, rsem, srcs, lands)
        return token[0:1, 0:1]

    def n_head(s):
        return 2 if fwd_plan[s][0][0] == "KV" else 1

    def n_first(s):
        return sum(len(sub_items(sub)) for sub in fwd_plan[s][: n_head(s)])

    def cores_begin(s, after):
        keys, ssem, rsem, srcs, lands = inflight.pop(s)
        srcs, lands = split_wait(_gather_copies, ssem, rsem, srcs, lands, after, name=f"gather_wait_{s}")
        k = n_first(s)
        a_sem, b_sem, a_src, a_land, tok = split_start(
            _fill_copies, srcs[:k], None, after, name=f"fill_start_{s}a", lands=lands[:k], per=4)
        if k < len(keys):
            inflight[s] = (keys[k:],) + split_start(
                _fill_copies, srcs[k:], None, tok, name=f"fill_start_{s}b", lands=lands[k:], per=4)
            tok = inflight[s][-1]
        gw.update(zip(keys[:k], split_wait(_fill_copies, a_sem, b_sem, a_src, a_land, tok, name=f"fill_wait_{s}a")[1]))
        return tok[0:1, 0:1]

    def cores_end(s, after):
        if s in inflight:
            keys, ssem, rsem, srcs, lands, _ = inflight.pop(s)
            gw.update(zip(keys, split_wait(_fill_copies, ssem, rsem, srcs, lands, after, name=f"fill_wait_{s}b")[1]))

    def g_gu(i, j):
        return gw["ffn_w_gu", 2 * i + j].reshape(N_CHIPS, D, T)

    def g_dn(i, j):
        return gw["ffn_w_down", 2 * i + j].reshape(2, T, D)

    def g_full(n, m0):
        a = gw[n, m0]
        return a.reshape(N_CHIPS * 2 * a.shape[-2], a.shape[-1])

    sm_like = [W[n].shape for n in SMALL_SHARDED]
    sm_pack, sm_n = _pack([W[n] for n in SMALL_SHARDED])
    sm_gathered = ag8(sm_pack, name="ag_small_w")
    sm_all = sm_gathered[0::2].reshape(N_CHIPS, -1)[:, :sm_n]
    full = {}
    for n, part in zip(SMALL_SHARDED, zip(*[_unpack(sm_all[k], sm_like) for k in range(N_CHIPS)])):
        full[n] = jnp.concatenate(part, axis=-1)

    c_all = ag8(c, act=_silu, after=sm_gathered, name="ag_c").reshape(N_DEV, D)
    c_all = c_all + chips_begin(0, c_all)
    p_mod = mm(c_all, mod_w, bias=chip_cols(mod_b, MW)[:, None, :], name="mod_mm")
    p_kv = mm(c_all, kv_mod_w, bias=chip_cols(kv_mod_b, KW)[None, :], name="kvmod_mm")
    p_all = jnp.concatenate([jnp.transpose(p_mod, (1, 0, 2)).reshape(N_DEV, depth * MW), p_kv], axis=1)
    p_mine = lax.dynamic_index_in_dim(ag8(p_all, name="ag_mod")[0::2], me, axis=1, keepdims=False)
    mod = jnp.transpose(p_mine[:, : depth * MW].reshape(N_CHIPS, depth, MW), (1, 0, 2)).reshape(depth, N_MOD * D)
    kvmod = p_mine[:, depth * MW :].reshape(1, 2 * D)
    mods = [[mod[i : i + 1, j * D : (j + 1) * D] for j in range(N_MOD)] for i in range(depth)]
    kv_shift, kv_scale = kvmod[:, :D], kvmod[:, D:]

    def ffn_fwd(xin, i, j, sh, sc, gt):
        h = norm_mod_fwd(xin, full["ffn_norm_w"][i, j][None], sc, sh, name=f"ffn_norm_{i}_{j}")
        gu, a = ffn_up(h, g_gu(i, j), name=f"ffn_gu_{i}_{j}")
        f, xo = mm(a, g_dn(i, j), reduce_s=True, resid=(xin, gt, FFN_HALF), name=f"ffn_down_{i}_{j}")
        return xo, (xin, gu, a, f, h)

    def ssm_fwd(xin, i, sh, sc, gt):
        h = norm_mod_fwd(xin, mix_norm_w[i][None], sc, sh, name=f"mix_norm_{i}")
        zx4 = mm(h, gw["ssm_w_in", i].reshape(N_CHIPS, D, CI), name=f"ssm_in_{i}")
        zx = jnp.transpose(zx4, (1, 0, 2)).reshape(L, N_CHIPS * CI)
        xbc = conv_fwd(zx, full["ssm_conv_w"][i], full["ssm_conv_b"][i][None], DI, name=f"ssm_conv_{i}")
        dt_raw = zx[:, DI + CC :]
        y, states = ssd_fwd(xbc, dt_raw, ssm_dt_bias[i][None], ssm_a_log[i][None], ssm_d[i][None], DI, name=f"ssd_{i}")
        yn = gnorm_fwd(y, zx, full["ssm_norm_w"][i][None], name=f"ssm_gnorm_{i}")
        f, xo = mm(yn, g_full("ssm_w_out", i), resid=(xin, gt, 1.0), name=f"ssm_out_{i}")
        return xo, (xin, zx, xbc, dt_raw, y, states, yn, f, h)

    def att_fwd(xin, i, kv, sh, sc, gt):
        l = i - n_a
        h = norm_mod_fwd(xin, mix_norm_w[i][None], sc, sh, name=f"mix_norm_{i}")
        q = mm(h, g_full("attn_w_q", l), bias=attn_b_q[l][None], name=f"att_q_{i}")
        o = attn_fwd(q, kv, attn_sinks[l][None], name=f"att_{i}")
        f, xo = mm(o, g_full("attn_w_o", l), bias=attn_b_o[l][None], resid=(xin, gt, 1.0), name=f"att_o_{i}")
        return xo, (xin, q, o, f, h)

    saved = [[None, None, None] for _ in range(depth)]
    xc = xs
    kv = x_kv = None
    n_stage = len(fwd_stages)

    def run_sub(sub, xc, tok):
        nonlocal kv, x_kv
        if sub[0] == "KV":
            x_kv = xc
            hkv = norm_mod_fwd(xc, kv_norm_w[None], kv_scale, kv_shift + tok, name="kv_norm")
            kv = mm(hkv, g_full("w_kv", 0), bias=b_kv[None], name="kv_proj")
            return xc
        i = sub[1]
        if sub[0] == "F":
            sh, sc, gt = mods[i][6 * sub[2] : 6 * sub[2] + 3]
            xc, saved[i][2 * sub[2]] = ffn_fwd(xc, i, sub[2], sh + tok, sc, gt)
            return xc
        sh, sc, gt = mods[i][3:6]
        xc, saved[i][1] = ssm_fwd(xc, i, sh + tok, sc, gt) if i < n_a else att_fwd(xc, i, kv, sh + tok, sc, gt)
        return xc

    def run_stage(s, xc, tok):
        for k, sub in enumerate(fwd_plan[s]):
            if k == n_head(s):
                cores_end(s, xc)
            xc = run_sub(sub, xc, tok if k < n_head(s) else 0.0)
        return xc

    dep = cores_begin(0, kvmod)
    for s in range(n_stage):
        tok = chips_begin(s + 1, dep) if s + 1 < n_stage else 0.0
        xc = run_stage(s, xc, tok)
        if s + 1 < n_stage:
            dep = cores_begin(s + 1, xc)

    loss_part, dx, d_final = final_loss(xc, final_norm_w[None], target, name="loss_head")
    loss = lax.psum(loss_part[0, 0], ("x", "y", "c"))

    wg = {}
    sg = {
        "ffn_norm_w": [[None, None] for _ in range(depth)], "mix_norm_w": [None] * depth, "mod": [None] * depth,
        "ssm_conv_w": [None] * n_a, "ssm_conv_b": [None] * n_a, "ssm_dt_bias": [None] * n_a, "ssm_a_log": [None] * n_a,
        "ssm_d": [None] * n_a, "ssm_norm_w": [None] * n_a, "attn_b_q": [None] * n_b, "attn_sinks": [None] * n_b,
        "attn_b_o": [None] * n_b,
    }

    def ffn_bwd(dxo, i, j, sv, sh, sc, gt):
        xin, gu, a, f, h = sv
        df, dgt, _ = gate_bwd(f, dxo, gt, FFN_HALF, name=f"ffn_res_bwd_{i}_{j}")
        dgu = ffn_down_bwd(df, g_dn(i, j), gu, name=f"ffn_down_dx_{i}_{j}")
        wg["ffn_w_down", 2 * i + j] = mm(a, df, mode="tn", out_dtype=BF16, name=f"ffn_down_dw_{i}_{j}")
        nw = full["ffn_norm_w"][i, j][None]
        dxi, dnw, dsc, dsh = mm_norm_bwd(dgu, g_gu(i, j), xin, nw, sc, dxo, name=f"ffn_gu_dx_{i}_{j}")
        wg["ffn_w_gu", 2 * i + j] = mm(h, dgu, mode="tn", out_dtype=BF16, name=f"ffn_gu_dw_{i}_{j}")
        sg["ffn_norm_w"][i][j] = dnw
        return dxi, (dsh, dsc, dgt)

    def ssm_bwd(dxo, i, sv, sh, sc, gt):
        xin, zx, xbc, dt_raw, y, states, yn, f, h = sv
        df, dgt, _ = gate_bwd(f, dxo, gt, 1.0, name=f"mix_res_bwd_{i}")
        dyn = mm(df, g_full("ssm_w_out", i), mode="nt", name=f"ssm_out_dx_{i}")
        wg["ssm_w_out", i] = mm(yn, df, mode="tn", out_dtype=BF16, name=f"ssm_out_dw_{i}")
        dy, dz, dnorm = gnorm_bwd(y, zx, full["ssm_norm_w"][i][None], dyn, name=f"ssm_gnorm_bwd_{i}")
        dxbc, ddt, dbias, dalog, ddsk = ssd_bwd(
            xbc, dt_raw, ssm_dt_bias[i][None], ssm_a_log[i][None], ssm_d[i][None], states, dy, DI, name=f"ssd_bwd_{i}"
        )
        du, dcw, dcb = conv_bwd(zx, full["ssm_conv_w"][i], full["ssm_conv_b"][i][None], dxbc, DI, name=f"ssm_conv_bwd_{i}")
        dzx = jnp.concatenate([dz, du, ddt], axis=1).astype(BF16)
        dzx4 = jnp.transpose(dzx.reshape(L, N_CHIPS, CI), (1, 0, 2))
        nw = mix_norm_w[i][None]
        dxi, dnw, dsc, dsh = mm_norm_bwd(
            dzx4, gw["ssm_w_in", i].reshape(N_CHIPS, D, CI), xin, nw, sc, dxo, name=f"ssm_in_dx_{i}")
        wg["ssm_w_in", i] = mm(h, dzx4, mode="tn", out_dtype=BF16, name=f"ssm_in_dw_{i}")
        sg["mix_norm_w"][i] = dnw
        sg["ssm_conv_w"][i], sg["ssm_conv_b"][i], sg["ssm_norm_w"][i] = dcw, dcb, dnorm
        sg["ssm_dt_bias"][i], sg["ssm_a_log"][i], sg["ssm_d"][i] = dbias, dalog, ddsk
        return dxi, (dsh, dsc, dgt)

    def att_bwd(dxo, i, sv, dkv, sh, sc, gt):
        l = i - n_a
        xin, q, o, f, h = sv
        df, dgt, dfsum = gate_bwd(f, dxo, gt, 1.0, name=f"mix_res_bwd_{i}")
        do = mm(df, g_full("attn_w_o", l), mode="nt", name=f"att_o_dx_{i}")
        wg["attn_w_o", l] = mm(o, df, mode="tn", out_dtype=BF16, name=f"att_o_dw_{i}")
        dq, dkv, dsink = attn_bwd(q, kv, attn_sinks[l][None], do, dkv, name=f"att_bwd_{i}")
        nw = mix_norm_w[i][None]
        dxi, dnw, dsc, dsh = mm_norm_bwd(dq, g_full("attn_w_q", l), xin, nw, sc, dxo, name=f"att_q_dx_{i}")
        wg["attn_w_q", l] = mm(h, dq, mode="tn", out_dtype=BF16, name=f"att_q_dw_{i}")
        sg["mix_norm_w"][i] = dnw
        sg["attn_b_q"][l], sg["attn_sinks"][l], sg["attn_b_o"][l] = colsum(dq, name=f"att_bq_{i}"), dsink, dfsum
        return dxi, dkv, (dsh, dsc, dgt)

    gfull = {n: lax.empty(_row_halves(W[n]).shape, F32) for n in GATHERED}

    pending, sib = [], []

    def sibling_begin(items, after):
        parts = []
        for n, m0, a in items:
            m, _, rh, cc = _row_halves(a).shape
            parts.append(wg.pop((n, m0)).reshape(m, N_CHIPS, 2, rh, cc))
        land_shapes = [p.shape[:2] + p.shape[3:] for p in parts]
        tag = f"{items[0][0]}_{items[0][1]}"
        ssem, rsem, srcs, lands, token = split_start(
            _sibling_copies, parts, land_shapes, after, name=f"rs_sibling_start_{tag}", per=1)
        sib.append((tag, [(n, m0) for n, m0, _ in items], ssem, rsem, srcs, lands))
        return token[0:1, 0:1]

    def reduce_step(s, after):
        if pending:
            reduce_end(after)
        items = bwd_stages[s]
        mine, theirs = {}, {}
        while sib:
            tag, keys, ssem, rsem, srcs, lands = sib.pop(0)
            srcs, lands = split_wait(_sibling_copies, ssem, rsem, srcs, lands, after, name=f"rs_sibling_wait_{tag}")
            mine.update(zip(keys, srcs))
            theirs.update(zip(keys, lands))
        parts = [mine[n, m0] for n, m0, _ in items]
        from_sib = [theirs[n, m0] for n, m0, _ in items]
        pair = [rs_add_pair(g, r, name=f"rs_pair_{n}_{m0}") for (n, m0, _), g, r in zip(items, parts, from_sib)]
        land_shapes = [(3, p.shape[0]) + p.shape[2:] for p in pair]
        ssem, rsem, srcs, lands, token = split_start(_reduce_copies, pair, land_shapes, after, name=f"rs_chips_start_{s}")
        pending.append((s, items, parts, from_sib, ssem, rsem, srcs, lands))
        return token[0:1, 0:1]

    def reduce_end(after):
        s, items, parts, from_sib, ssem, rsem, srcs, lands = pending.pop()
        _, from_chips = split_wait(_reduce_copies, ssem, rsem, srcs, lands, after, name=f"rs_chips_wait_{s}")
        for (n, m0, _), g, r, t in zip(items, parts, from_sib, from_chips):
            gfull[n] = rs_add_final(g, r, t, gfull[n], m0, name=f"rs_final_{n}_{m0}")

    dkv = jnp.zeros((L, KVD), F32)
    d_kvnorm = d_kvmod = d_bkv = None
    tok = 0.0
    for i in reversed(range(depth)):
        sh1, sc1, g1, shm, scm, gm, sh2, sc2, g2 = mods[i]
        s1, sm, s2 = saved[i]
        dx, dm2 = ffn_bwd(dx, i, 1, s2, sh2, sc2, g2 + tok)
        tok = sibling_begin(ffn_items(i, 1), dx)
        if i < n_a:
            dx, dmm = ssm_bwd(dx, i, sm, shm, scm, gm + tok)
        else:
            dx, dkv, dmm = att_bwd(dx, i, sm, dkv, shm, scm, gm + tok)
        tok = sibling_begin(mix_items(i), dx)
        if i == 0:
            tok = tok + reduce_step(depth - 1, dx)
        dx, dm1 = ffn_bwd(dx, i, 0, s1, sh1, sc1, g1 + tok)
        tok = sibling_begin(ffn_items(i, 0), dx)
        sg["mod"][i] = jnp.concatenate(list(dm1) + list(dmm) + list(dm2), axis=1)
        if i == n_a:
            d_bkv = colsum(dkv, name="kv_bias_bwd")
            hkv = norm_mod_fwd(x_kv, kv_norm_w[None], kv_scale, kv_shift + tok, name="kv_norm_re")
            wg["w_kv", 0] = mm(hkv, dkv, mode="tn", out_dtype=BF16, name="kv_proj_dw")
            dx, d_kvnorm, dsc, dsh = mm_norm_bwd(
                dkv, g_full("w_kv", 0), x_kv, kv_norm_w[None], kv_scale, dx, name="kv_proj_dx")
            d_kvmod = jnp.concatenate([dsh, dsc], axis=1)
            tok = sibling_begin([("w_kv", 0, w_kv)], dx)
        if i > 0:
            tok = reduce_step(depth - 1 - i, dx)
    grad_x = dx[None]
    grads = {}

    small = {
        "ffn_norm_w": jnp.stack([jnp.stack([r[0] for r in row]) for row in sg["ffn_norm_w"]]),
        "mod_b": jnp.stack([r[0] for r in sg["mod"]]),
        "mix_norm_w": jnp.stack([r[0] for r in sg["mix_norm_w"]]),
        "ssm_conv_w": jnp.stack(sg["ssm_conv_w"]),
        "ssm_conv_b": jnp.stack([r[0] for r in sg["ssm_conv_b"]]),
        "ssm_dt_bias": jnp.stack([r[0] for r in sg["ssm_dt_bias"]]),
        "ssm_a_log": jnp.stack([r[0] for r in sg["ssm_a_log"]]),
        "ssm_d": jnp.stack([r[0] for r in sg["ssm_d"]]),
        "ssm_norm_w": jnp.stack([r[0] for r in sg["ssm_norm_w"]]),
        "kv_norm_w": d_kvnorm[0],
        "kv_mod_b": d_kvmod[0],
        "b_kv": d_bkv[0],
        "attn_b_q": jnp.stack([r[0] for r in sg["attn_b_q"]]),
        "attn_sinks": jnp.stack([r[0] for r in sg["attn_sinks"]]),
        "attn_b_o": jnp.stack([r[0] for r in sg["attn_b_o"]]),
        "final_norm_w": d_final[0],
    }
    small_like = [small[n].shape for n in SMALL]
    sv_pack, sv_n = _pack([small[n] for n in SMALL])
    sv_all = ag8(sv_pack + tok, name="ag_small_g")
    sv_all = sv_all + reduce_step(depth, sv_all)
    sv_sum = reduce8(sv_all, name="small_g_sum").reshape(-1)[:sv_n]
    for n, gsum in zip(SMALL, _unpack(sv_sum, small_like)):
        grads[n] = chip_cols(gsum, W[n].shape[-1]) if n in SMALL_SHARDED else gsum

    per_dev = [_unpack(sv_all[b].reshape(-1)[:sv_n], small_like) for b in range(N_DEV)]
    i_modb, i_kvb = SMALL.index("mod_b"), SMALL.index("kv_mod_b")
    dmod_all = jnp.stack([chip_cols(p[i_modb], MW) for p in per_dev], axis=1)
    dkv_all = jnp.stack([chip_cols(p[i_kvb], KW) for p in per_dev], axis=0)[None]
    c_t = jnp.transpose(c_all)
    grads["mod_w"] = outer8(c_t, dmod_all, name="mod_w_grad")
    grads["kv_mod_w"] = outer8(c_t, dkv_all, name="kv_mod_w_grad")[0]

    delta, new_m, new_v = {}, {}, {}
    for n in COLUMN_PARALLEL:
        delta[n], new_m[n], new_v[n] = adamw(W[n], grads[n], MOM[n], VAR[n], name=f"adamw_{n}")
    like = [W[n].shape for n in SMALL]
    packs = [_pack([d[n] for n in SMALL])[0] for d in (W, grads, MOM, VAR)]
    n_small = sum(int(W[n].size) for n in SMALL)
    for dst, res in zip((delta, new_m, new_v), adamw(*packs, name="adamw_small")):
        for n, a in zip(SMALL, _unpack(res.reshape(-1)[:n_small], like)):
            dst[n] = a

    reduce_end(delta["mod_w"])
    for n, s in zip(GATHERED, rs_share([gfull[n] for n in GATHERED], name="rs_share")):
        grads[n] = s.reshape(W[n].shape)
        delta[n], new_m[n], new_v[n] = adamw(W[n], grads[n], MOM[n], VAR[n], name=f"adamw_{n}")

    return (loss, grad_x, *[grads[n] for n in WEIGHTS], *[delta[n] for n in WEIGHTS], *[new_m[n] for n in WEIGHTS],
            *[new_v[n] for n in WEIGHTS])
```

```python
import functools

import jax
import jax.numpy as jnp
from jax import lax
from jax.experimental import pallas as pl
from jax.experimental.pallas import tpu as pltpu

F32 = jnp.float32
BF16 = jnp.bfloat16
HIGHEST = lax.Precision.HIGHEST
MESH = pl.DeviceIdType.MESH

EPS = 1e-5
N_MOD = 9
FFN_HALF = 0.5
SSM_HEADDIM = 64
SSM_GROUPS = 8
SSM_STATE = 128
CONV_WIDTH = 4
CHUNK = 128
KV_HEADS = 4
HEAD_DIM = 64
WINDOW = 128
N_CHIPS = 4
N_DEV = 8

ADAM_LR = 0.001
ADAM_B1 = 0.9
ADAM_B2 = 0.999
ADAM_EPS = 1e-08
ADAM_WD = 0.01
ADAM_STEP = 10

LANE = 128
MM_TILE = 1024


def _pick(n, pref, align, whole_if_small=False):
    best = 0
    t = align
    while t <= min(n, pref):
        if n % t == 0:
            best = t
        t += align
    if best == 0 or (whole_if_small and best < 256 and n <= 2048):
        return n
    return best


def _sigmoid(x):
    return 1.0 / (1.0 + jnp.exp(-x))


def _silu(x):
    return x * _sigmoid(x)


def _dsilu(x):
    s = _sigmoid(x)
    return s * (1.0 + x * (1.0 - s))


def _params(*sem):
    return pltpu.CompilerParams(dimension_semantics=sem)


def mm(a, b, *, mode="nn", reduce_s=False, out_dtype=F32, bias=None, resid=None, name):
    a_s = a.ndim == 3
    b_s = b.ndim == 3
    S = a.shape[0] if a_s else (b.shape[0] if b_s else 1)
    a2 = a.shape[-2:]
    b2 = b.shape[-2:]
    if mode == "nn":
        (M, K), (K2, N) = a2, b2
    elif mode == "nt":
        (M, K), (N, K2) = a2, b2
    else:
        (K, M), (K2, N) = a2, b2
    assert K == K2, (a.shape, b.shape, mode)
    batch = (a_s or b_s) and not reduce_s
    sb = S if batch else 1
    sr = S if ((a_s or b_s) and reduce_s) else 1
    tm = _pick(M, MM_TILE // 2 if resid is not None else MM_TILE, LANE if mode == "tn" else 16, True)
    tn = _pick(N, MM_TILE, LANE, True)
    tk = _pick(K, 2 * MM_TILE if mode == "tn" else MM_TILE, LANE if mode != "tn" else 16, True)
    nk = K // tk
    grid = (sb, M // tm, N // tn, sr, nk)

    def s_of(isb, isr):
        return isb if batch else isr

    def a_map(isb, i, j, isr, k):
        idx = (k, i) if mode == "tn" else (i, k)
        return ((s_of(isb, isr),) + idx) if a_s else idx

    def b_map(isb, i, j, isr, k):
        idx = (j, k) if mode == "nt" else (k, j)
        return ((s_of(isb, isr),) + idx) if b_s else idx

    def o_map(isb, i, j, isr, k):
        return (isb, i, j) if batch else (i, j)

    def s_blk(has_s, blk):
        return ((None,) + blk) if has_s else blk

    a_blk = (tk, tm) if mode == "tn" else (tm, tk)
    b_blk = (tn, tk) if mode == "nt" else (tk, tn)
    in_specs = [pl.BlockSpec(s_blk(a_s, a_blk), a_map), pl.BlockSpec(s_blk(b_s, b_blk), b_map)]
    args = [a, b]
    if bias is not None:
        bias_s = bias.ndim == 3
        in_specs.append(
            pl.BlockSpec(
                ((None, 1, tn) if bias_s else (1, tn)),
                (lambda isb, i, j, isr, k: (isb, 0, j)) if bias_s else (lambda isb, i, j, isr, k: (0, j)),
            )
        )
        args.append(bias)
    o_spec = pl.BlockSpec(s_blk(batch, (tm, tn)), o_map)
    out_shape = jax.ShapeDtypeStruct(((sb, M, N) if batch else (M, N)), out_dtype)
    out_specs = o_spec
    if resid is not None:
        assert not batch
        x_res, gate, scale = resid
        in_specs += [pl.BlockSpec((tm, tn), o_map), pl.BlockSpec((1, tn), lambda isb, i, j, isr, k: (0, j))]
        args += [x_res, gate]
        out_shape = (out_shape, jax.ShapeDtypeStruct((M, N), F32))
        out_specs = (o_spec, pl.BlockSpec((tm, tn), o_map))
    dims = {"nn": (((1,), (0,)), ((), ())), "nt": (((1,), (1,)), ((), ())), "tn": (((0,), (0,)), ((), ()))}[mode]
    n_in = len(args)
    n_out = 2 if resid is not None else 1
    one_step = sr * nk == 1

    def body(*refs):
        a_ref, b_ref = refs[0], refs[1]
        bias_ref = refs[2] if bias is not None else None
        o_ref = refs[n_in]

        def finish(r):
            if bias is not None:
                r = r + bias_ref[...]
            o_ref[...] = r.astype(o_ref.dtype)
            if resid is not None:
                refs[n_in + 1][...] = refs[n_in - 2][...] + (scale * refs[n_in - 1][...]) * r

        def part():
            return lax.dot_general(a_ref[...].astype(BF16), b_ref[...].astype(BF16), dims, preferred_element_type=F32)

        if one_step:
            finish(part())
            return
        acc = refs[n_in + n_out]
        isr = pl.program_id(3)
        k = pl.program_id(4)

        @pl.when((isr == 0) & (k == 0))
        def _():
            acc[...] = jnp.zeros_like(acc)

        acc[...] += part()

        @pl.when((isr == sr - 1) & (k == nk - 1))
        def _():
            finish(acc[...])

    return pl.pallas_call(
        body,
        out_shape=out_shape,
        grid=grid,
        in_specs=in_specs,
        out_specs=out_specs,
        scratch_shapes=[] if one_step else [pltpu.VMEM((tm, tn), F32)],
        compiler_params=_params("parallel", "parallel", "parallel", "arbitrary", "arbitrary"),
        name=name,
    )(*args)


def norm_mod_fwd(x, nw, sc, sh, *, name):
    L, D = x.shape
    tl = _pick(L, 512, 16)

    def body(x_ref, nw_ref, sc_ref, sh_ref, h_ref):
        xv = x_ref[...]
        r = lax.rsqrt(jnp.mean(xv * xv, axis=-1, keepdims=True) + EPS)
        n = (xv * r) * nw_ref[...]
        h_ref[...] = (n * (1.0 + sc_ref[...]) + sh_ref[...]).astype(h_ref.dtype)

    row = pl.BlockSpec((1, D), lambda i: (0, 0))
    return pl.pallas_call(
        body,
        out_shape=jax.ShapeDtypeStruct((L, D), BF16),
        grid=(L // tl,),
        in_specs=[pl.BlockSpec((tl, D), lambda i: (i, 0)), row, row, row],
        out_specs=pl.BlockSpec((tl, D), lambda i: (i, 0)),
        compiler_params=_params("parallel"),
        name=name,
    )(x, nw, sc, sh)


def mm_norm_bwd(a, b, x, nw, sc, dx_in, *, name):
    has_s = a.ndim == 3
    assert has_s == (b.ndim == 3)
    sr = a.shape[0] if has_s else 1
    M, K = a.shape[-2:]
    D = b.shape[-2]
    tm = _pick(M, MM_TILE // 2, 16, True)
    tk = _pick(K, MM_TILE, LANE, True)
    nk = K // tk

    def body(a_ref, b_ref, x_ref, nw_ref, sc_ref, dxi_ref, dx_ref, dnw_ref, dsc_ref, dsh_ref, acc):
        i, s, k = pl.program_id(0), pl.program_id(1), pl.program_id(2)

        @pl.when((i == 0) & (s == 0) & (k == 0))
        def _():
            dnw_ref[...] = jnp.zeros_like(dnw_ref)
            dsc_ref[...] = jnp.zeros_like(dsc_ref)
            dsh_ref[...] = jnp.zeros_like(dsh_ref)

        @pl.when((s == 0) & (k == 0))
        def _():
            acc[...] = jnp.zeros_like(acc)

        acc[...] += lax.dot_general(
            a_ref[...].astype(BF16), b_ref[...].astype(BF16), (((1,), (1,)), ((), ())), preferred_element_type=F32
        )

        @pl.when((s == sr - 1) & (k == nk - 1))
        def _():
            dh_v = acc[...]
            xv = x_ref[...]
            r = lax.rsqrt(jnp.mean(xv * xv, axis=-1, keepdims=True) + EPS)
            xhat = xv * r
            nw_v = nw_ref[...]
            n = xhat * nw_v
            dsh_ref[...] += jnp.sum(dh_v, axis=0, keepdims=True)
            dsc_ref[...] += jnp.sum(dh_v * n, axis=0, keepdims=True)
            dn = dh_v * (1.0 + sc_ref[...])
            dnw_ref[...] += jnp.sum(dn * xhat, axis=0, keepdims=True)
            dxhat = dn * nw_v
            dx_ref[...] = dxi_ref[...] + r * (dxhat - xhat * jnp.mean(dxhat * xhat, axis=-1, keepdims=True))

    row = pl.BlockSpec((1, D), lambda i, s, k: (0, 0))
    tile = pl.BlockSpec((tm, D), lambda i, s, k: (i, 0))
    vec = jax.ShapeDtypeStruct((1, D), F32)
    return pl.pallas_call(
        body,
        out_shape=(jax.ShapeDtypeStruct((M, D), F32), vec, vec, vec),
        grid=(M // tm, sr, nk),
        in_specs=[
            pl.BlockSpec((None, tm, tk) if has_s else (tm, tk), (lambda i, s, k: (s, i, k)) if has_s else (lambda i, s, k: (i, k))),
            pl.BlockSpec((None, D, tk) if has_s else (D, tk), (lambda i, s, k: (s, 0, k)) if has_s else (lambda i, s, k: (0, k))),
            tile, row, row, tile,
        ],
        out_specs=(tile, row, row, row),
        scratch_shapes=[pltpu.VMEM((tm, D), F32)],
        compiler_params=_params("arbitrary", "arbitrary", "arbitrary"),
        name=name,
    )(a, b, x, nw, sc, dx_in)


def gate_bwd(f, dx, gate, scale, *, name):
    L, D = f.shape
    tl = _pick(L, 512, 16)

    def body(f_ref, dx_ref, g_ref, df_ref, dg_ref, dfsum_ref):
        @pl.when(pl.program_id(0) == 0)
        def _():
            dg_ref[...] = jnp.zeros_like(dg_ref)
            dfsum_ref[...] = jnp.zeros_like(dfsum_ref)

        dxv = dx_ref[...]
        df = (scale * g_ref[...]) * dxv
        df_ref[...] = df.astype(df_ref.dtype)
        dfsum_ref[...] += jnp.sum(df, axis=0, keepdims=True)
        dg_ref[...] += scale * jnp.sum(f_ref[...] * dxv, axis=0, keepdims=True)

    tile = pl.BlockSpec((tl, D), lambda i: (i, 0))
    row = pl.BlockSpec((1, D), lambda i: (0, 0))
    vec = jax.ShapeDtypeStruct((1, D), F32)
    return pl.pallas_call(
        body,
        out_shape=(jax.ShapeDtypeStruct((L, D), BF16), vec, vec),
        grid=(L // tl,),
        in_specs=[tile, tile, row],
        out_specs=(tile, row, row),
        compiler_params=_params("arbitrary"),
        name=name,
    )(f, dx, gate)


def colsum(x, *, name):
    L, N = x.shape
    tl = _pick(L, 512, 8)

    def body(x_ref, o_ref):
        @pl.when(pl.program_id(0) == 0)
        def _():
            o_ref[...] = jnp.zeros_like(o_ref)

        o_ref[...] += jnp.sum(x_ref[...], axis=0, keepdims=True)

    return pl.pallas_call(
        body,
        out_shape=jax.ShapeDtypeStruct((1, N), F32),
        grid=(L // tl,),
        in_specs=[pl.BlockSpec((tl, N), lambda i: (i, 0))],
        out_specs=pl.BlockSpec((1, N), lambda i: (0, 0)),
        compiler_params=_params("arbitrary"),
        name=name,
    )(x)


def ffn_up(h, wgu, *, name):
    L, D = h.shape
    T = wgu.shape[-1]
    tm = _pick(L, 512, 16)

    def body(h_ref, w_ref, gu_ref, a_ref):
        hb = h_ref[...].astype(BF16)
        g = jnp.dot(hb, w_ref[0].astype(BF16), preferred_element_type=F32)
        u = jnp.dot(hb, w_ref[1].astype(BF16), preferred_element_type=F32)
        gu_ref[0] = g
        gu_ref[1] = u
        a_ref[...] = (_silu(g) * u).astype(a_ref.dtype)

    gu, a = pl.pallas_call(
        body,
        out_shape=(jax.ShapeDtypeStruct((2, 2, L, T), F32), jax.ShapeDtypeStruct((2, L, T), BF16)),
        grid=(2, L // tm),
        in_specs=[
            pl.BlockSpec((tm, D), lambda j, i: (i, 0)),
            pl.BlockSpec((2, None, D, T), lambda j, i: (0, j, 0, 0)),
        ],
        out_specs=(
            pl.BlockSpec((2, None, tm, T), lambda j, i: (0, j, i, 0)),
            pl.BlockSpec((None, tm, T), lambda j, i: (j, i, 0)),
        ),
        compiler_params=_params("parallel", "parallel"),
        name=name,
    )(h, wgu.reshape(2, 2, D, T))
    return gu.reshape(4, L, T), a


def ffn_down_bwd(df, wdn, gu, *, name):
    L, D = df.shape
    T = wdn.shape[1]
    tm = _pick(L, 512, 16)

    def body(df_ref, w_ref, gu_ref, d_ref):
        da = lax.dot_general(
            df_ref[...].astype(BF16), w_ref[...].astype(BF16), (((1,), (1,)), ((), ())), preferred_element_type=F32
        )
        g = gu_ref[0]
        d_ref[0] = (da * gu_ref[1] * _dsilu(g)).astype(d_ref.dtype)
        d_ref[1] = (da * _silu(g)).astype(d_ref.dtype)

    out = pl.pallas_call(
        body,
        out_shape=jax.ShapeDtypeStruct((2, 2, L, T), BF16),
        grid=(2, L // tm),
        in_specs=[
            pl.BlockSpec((tm, D), lambda j, i: (i, 0)),
            pl.BlockSpec((None, T, D), lambda j, i: (j, 0, 0)),
            pl.BlockSpec((2, None, tm, T), lambda j, i: (0, j, i, 0)),
        ],
        out_specs=pl.BlockSpec((2, None, tm, T), lambda j, i: (0, j, i, 0)),
        compiler_params=_params("parallel", "parallel"),
        name=name,
    )(df, wdn, gu.reshape(2, 2, L, T))
    return out.reshape(4, L, T)


def _shift_down(u, k, rows):
    if k == 0:
        return u
    return jnp.where(rows >= k, pltpu.roll(u, k, 0), 0.0)


def _shift_up(u, k, rows, n):
    if k == 0:
        return u
    return jnp.where(rows < n - k, pltpu.roll(u, n - k, 0), 0.0)


def _conv_pre(u, w_ref, b_ref, rows):
    pre = b_ref[...] + w_ref[CONV_WIDTH - 1 : CONV_WIDTH, :] * u
    for k in range(1, CONV_WIDTH):
        pre = pre + w_ref[CONV_WIDTH - 1 - k : CONV_WIDTH - k, :] * _shift_down(u, k, rows)
    return pre


def conv_fwd(zx, conv_w, conv_b, d_inner, *, name):
    L = zx.shape[0]
    C = conv_w.shape[1]
    tc = 256
    off = d_inner // tc

    def body(u_ref, w_ref, b_ref, o_ref):
        rows = lax.broadcasted_iota(jnp.int32, (L, tc), 0)
        o_ref[...] = _silu(_conv_pre(u_ref[...], w_ref, b_ref, rows))

    return pl.pallas_call(
        body,
        out_shape=jax.ShapeDtypeStruct((L, C), F32),
        grid=(C // tc,),
        in_specs=[
            pl.BlockSpec((L, tc), lambda j: (0, off + j)),
            pl.BlockSpec((CONV_WIDTH, tc), lambda j: (0, j)),
            pl.BlockSpec((1, tc), lambda j: (0, j)),
        ],
        out_specs=pl.BlockSpec((L, tc), lambda j: (0, j)),
        compiler_params=_params("parallel"),
        name=name,
    )(zx, conv_w, conv_b)


def conv_bwd(zx, conv_w, conv_b, dxbc, d_inner, *, name):
    L = zx.shape[0]
    C = conv_w.shape[1]
    tc = 256
    off = d_inner // tc

    def body(u_ref, w_ref, b_ref, d_ref, du_ref, dw_ref, db_ref):
        rows = lax.broadcasted_iota(jnp.int32, (L, tc), 0)
        u = u_ref[...]
        dpre = d_ref[...] * _dsilu(_conv_pre(u, w_ref, b_ref, rows))
        db_ref[...] = jnp.sum(dpre, axis=0, keepdims=True)
        du = w_ref[CONV_WIDTH - 1 : CONV_WIDTH, :] * dpre
        dw_ref[CONV_WIDTH - 1 : CONV_WIDTH, :] = jnp.sum(dpre * u, axis=0, keepdims=True)
        for k in range(1, CONV_WIDTH):
            j = CONV_WIDTH - 1 - k
            dw_ref[j : j + 1, :] = jnp.sum(dpre * _shift_down(u, k, rows), axis=0, keepdims=True)
            du = du + w_ref[j : j + 1, :] * _shift_up(dpre, k, rows, L)
        du_ref[...] = du

    return pl.pallas_call(
        body,
        out_shape=(
            jax.ShapeDtypeStruct((L, C), F32),
            jax.ShapeDtypeStruct((CONV_WIDTH, C), F32),
            jax.ShapeDtypeStruct((1, C), F32),
        ),
        grid=(C // tc,),
        in_specs=[
            pl.BlockSpec((L, tc), lambda j: (0, off + j)),
            pl.BlockSpec((CONV_WIDTH, tc), lambda j: (0, j)),
            pl.BlockSpec((1, tc), lambda j: (0, j)),
            pl.BlockSpec((L, tc), lambda j: (0, j)),
        ],
        out_specs=(
            pl.BlockSpec((L, tc), lambda j: (0, j)),
            pl.BlockSpec((CONV_WIDTH, tc), lambda j: (0, j)),
            pl.BlockSpec((1, tc), lambda j: (0, j)),
        ),
        compiler_params=_params("parallel"),
        name=name,
    )(zx, conv_w, conv_b, dxbc)


def _ssd_head(xs, dt, acs, tot, dsk, cb, bm, cm, prev):
    q = xs.shape[0]
    li = lax.broadcasted_iota(jnp.int32, (q, q), 0)
    si = lax.broadcasted_iota(jnp.int32, (q, q), 1)
    causal = li >= si
    lmat = jnp.exp(jnp.where(causal, acs - acs.T, -jnp.inf))
    xdt = xs * dt
    y = jnp.dot((cb * lmat).astype(BF16), xdt.astype(BF16), preferred_element_type=F32)
    y = y + lax.dot_general(
        (cm * jnp.exp(acs)).astype(BF16), prev.astype(BF16), (((1,), (1,)), ((), ())), preferred_element_type=F32
    )
    y = y + dsk * xs
    st = lax.dot_general(
        xdt.astype(BF16), (bm * jnp.exp(tot - acs)).astype(BF16), (((0,), (0,)), ((), ())), preferred_element_type=F32
    )
    return y, prev * jnp.exp(tot) + st


def _pick_lane(v, h):
    lanes = lax.broadcasted_iota(jnp.int32, v.shape, 1)
    return jnp.sum(jnp.where(lanes == h, v, 0.0), axis=1, keepdims=True)


def _tri_cols(cols, upper):
    q = cols[0].shape[0]
    assert 3 * len(cols) <= LANE
    li = lax.broadcasted_iota(jnp.int32, (q, q), 0)
    si = lax.broadcasted_iota(jnp.int32, (q, q), 1)
    tri = ((li <= si) if upper else (li >= si)).astype(BF16)
    lanes = lax.broadcasted_iota(jnp.int32, (q, LANE), 1)
    rhs = jnp.zeros((q, LANE), F32)
    for r, col in enumerate(cols):
        hi = col.astype(BF16).astype(F32)
        mid = (col - hi).astype(BF16).astype(F32)
        lo = col - hi - mid
        for t, term in enumerate((hi, mid, lo)):
            rhs = jnp.where(lanes == 3 * r + t, term, rhs)
    out = jnp.dot(tri, rhs.astype(BF16), preferred_element_type=F32)
    return [jnp.sum(jnp.where((lanes >= 3 * r) & (lanes < 3 * r + 3), out, 0.0), axis=1, keepdims=True)
            for r in range(len(cols))]


def _softplus(x):
    return jnp.maximum(x, 0.0) + jnp.log(1.0 + jnp.exp(-jnp.abs(x)))


def _ssd_specs(L, d_inner, H, nc, rev):
    R = H // SSM_GROUPS
    P, N, Q = SSM_HEADDIM, SSM_STATE, CHUNK
    ngrp = SSM_GROUPS

    def ci(c):
        return (nc - 1 - c) if rev else c

    b_off = d_inner // N
    c_off = b_off + ngrp
    xs = pl.BlockSpec((Q, R * P), lambda c, g: (ci(c), g))
    bm = pl.BlockSpec((Q, N), lambda c, g: (ci(c), b_off + g))
    cm = pl.BlockSpec((Q, N), lambda c, g: (ci(c), c_off + g))
    dt = pl.BlockSpec((Q, H), lambda c, g: (ci(c), 0))
    hv = pl.BlockSpec((1, H), lambda c, g: (0, 0))
    y = pl.BlockSpec((Q, R * P), lambda c, g: (ci(c), g))
    st = pl.BlockSpec((None, R * P, N), lambda c, g: (ci(c), g, 0))
    return R, xs, bm, cm, dt, hv, y, st


def ssd_fwd(xbc, dt_raw, dt_bias, a_log, d_skip, d_inner, *, name):
    L = xbc.shape[0]
    H = dt_raw.shape[1]
    nc = L // CHUNK
    P, N = SSM_HEADDIM, SSM_STATE
    R, xs_s, bm_s, cm_s, dt_s, hv_s, y_s, st_s = _ssd_specs(L, d_inner, H, nc, False)

    def body(xs_ref, bm_ref, cm_ref, dt_ref, bias_ref, alog_ref, dsk_ref, y_ref, st_ref, state):
        c = pl.program_id(0)
        g = pl.program_id(1)

        @pl.when(c == 0)
        def _():
            for r in range(R):
                state[g * R + r] = jnp.zeros((P, N), F32)

        dtb = _softplus(dt_ref[...] + bias_ref[...])
        a_all = -jnp.exp(alog_ref[...])
        bm, cm = bm_ref[...], cm_ref[...]
        cb = lax.dot_general(cm.astype(BF16), bm.astype(BF16), (((1,), (1,)), ((), ())), preferred_element_type=F32)
        dts = [_pick_lane(dtb, g * R + r) for r in range(R)]
        a_cols = [dts[r] * _pick_lane(a_all, g * R + r) for r in range(R)]
        acs = _tri_cols(a_cols, upper=False)
        prevs = [state[g * R + r] for r in range(R)]
        res = []
        for r in range(R):
            res.append(_ssd_head(
                xs_ref[:, r * P : (r + 1) * P],
                dts[r],
                jnp.broadcast_to(acs[r], (CHUNK, CHUNK)),
                jnp.sum(a_cols[r], axis=0, keepdims=True),
                _pick_lane(dsk_ref[...], g * R + r),
                cb,
                bm,
                cm,
                prevs[r],
            ))
        for r in range(R):
            st_ref[r * P : (r + 1) * P, :] = prevs[r]
            y_ref[:, r * P : (r + 1) * P] = res[r][0]
            state[g * R + r] = res[r][1]

    return pl.pallas_call(
        body,
        out_shape=(jax.ShapeDtypeStruct((L, d_inner), F32), jax.ShapeDtypeStruct((nc, H * P, N), F32)),
        grid=(nc, SSM_GROUPS),
        in_specs=[xs_s, bm_s, cm_s, dt_s, hv_s, hv_s, hv_s],
        out_specs=(y_s, st_s),
        scratch_shapes=[pltpu.VMEM((H, P, N), F32)],
        compiler_params=_params("arbitrary", "arbitrary"),
        name=name,
    )(xbc, xbc, xbc, dt_raw, dt_bias, a_log, d_skip)


def ssd_bwd(xbc, dt_raw, dt_bias, a_log, d_skip, states, dy, d_inner, *, name):
    L, C = xbc.shape
    H = dt_raw.shape[1]
    nc = L // CHUNK
    P, N, Q = SSM_HEADDIM, SSM_STATE, CHUNK
    R, xs_s, bm_s, cm_s, dt_s, hv_s, y_s, st_s = _ssd_specs(L, d_inner, H, nc, True)

    def body(xs_ref, bm_ref, cm_ref, dt_ref, bias_ref, alog_ref, dsk_ref, st_ref, dy_ref,
             dxs_ref, dbm_ref, dcm_ref, ddt_ref, dbias_ref, dalog_ref, ddsk_ref, dstate):
        c = pl.program_id(0)
        g = pl.program_id(1)

        @pl.when(c == 0)
        def _():
            for r in range(R):
                dstate[g * R + r] = jnp.zeros((P, N), F32)

        @pl.when((c == 0) & (g == 0))
        def _():
            dbias_ref[...] = jnp.zeros_like(dbias_ref)
            dalog_ref[...] = jnp.zeros_like(dalog_ref)
            ddsk_ref[...] = jnp.zeros_like(ddsk_ref)

        @pl.when(g == 0)
        def _():
            ddt_ref[...] = jnp.zeros_like(ddt_ref)

        pre = dt_ref[...] + bias_ref[...]
        dtb = _softplus(pre)
        a_all = -jnp.exp(alog_ref[...])
        lanes_q = lax.broadcasted_iota(jnp.int32, (Q, H), 1)
        lanes_1 = lax.broadcasted_iota(jnp.int32, (1, H), 1)
        bm = bm_ref[...]
        cm = cm_ref[...]
        nt = (((1,), (1,)), ((), ()))
        cb = lax.dot_general(cm.astype(BF16), bm.astype(BF16), nt, preferred_element_type=F32)
        dts = [_pick_lane(dtb, g * R + r) for r in range(R)]
        a_negs = [_pick_lane(a_all, g * R + r) for r in range(R)]
        a_cols = [dts[r] * a_negs[r] for r in range(R)]
        acs = _tri_cols(a_cols, upper=False)
        dbm = jnp.zeros((Q, N), F32)
        dcm = jnp.zeros((Q, N), F32)
        dcb = jnp.zeros((Q, Q), F32)
        dd_row = jnp.zeros((1, H), F32)
        dstates = [dstate[g * R + r] for r in range(R)]
        dprevs, ddts, dacs_cols, dtots = [], [], [], []
        for r in range(R):
            h = g * R + r
            args = (
                xs_ref[:, r * P : (r + 1) * P],
                dts[r],
                jnp.broadcast_to(acs[r], (Q, Q)),
                jnp.sum(a_cols[r], axis=0, keepdims=True),
                _pick_lane(dsk_ref[...], h),
                cb,
                bm,
                cm,
                st_ref[r * P : (r + 1) * P, :],
            )
            _, vjp = jax.vjp(_ssd_head, *args)
            dxs, ddt, dacs, dtot, dd, dcb_h, dbm_h, dcm_h, dprev = vjp((dy_ref[:, r * P : (r + 1) * P], dstates[r]))
            dxs_ref[:, r * P : (r + 1) * P] = dxs
            dprevs.append(dprev)
            ddts.append(ddt)
            dacs_cols.append(jnp.sum(dacs, axis=1, keepdims=True))
            dtots.append(dtot)
            dbm = dbm + dbm_h
            dcm = dcm + dcm_h
            dcb = dcb + dcb_h
            dd_row = dd_row + jnp.where(lanes_1 == h, dd, 0.0)
        for r in range(R):
            dstate[g * R + r] = dprevs[r]
        ddt_blk = jnp.zeros((Q, H), F32)
        da_row = jnp.zeros((1, H), F32)
        for r, da_col in enumerate(_tri_cols(dacs_cols, upper=True)):
            h = g * R + r
            da_col = da_col + dtots[r]
            ddt_blk = ddt_blk + jnp.where(lanes_q == h, ddts[r] + da_col * a_negs[r], 0.0)
            da_row = da_row + jnp.where(lanes_1 == h, jnp.sum(da_col * dts[r], axis=0, keepdims=True), 0.0)
        dcb16 = dcb.astype(BF16)
        dbm_ref[...] = dbm + lax.dot_general(dcb16, cm.astype(BF16), (((0,), (0,)), ((), ())), preferred_element_type=F32)
        dcm_ref[...] = dcm + jnp.dot(dcb16, bm.astype(BF16), preferred_element_type=F32)
        ddt_pre = ddt_blk * _sigmoid(pre)
        ddt_ref[...] += ddt_pre
        dbias_ref[...] += jnp.sum(ddt_pre, axis=0, keepdims=True)
        dalog_ref[...] += da_row * a_all
        ddsk_ref[...] += dd_row

    ngrp = SSM_GROUPS
    hrow = jax.ShapeDtypeStruct((1, H), F32)
    dxs, dbm, dcm, ddt, dbias, dalog, ddsk = pl.pallas_call(
        body,
        out_shape=(
            jax.ShapeDtypeStruct((L, d_inner), F32),
            jax.ShapeDtypeStruct((L, ngrp * N), F32),
            jax.ShapeDtypeStruct((L, ngrp * N), F32),
            jax.ShapeDtypeStruct((L, H), F32),
            hrow,
            hrow,
            hrow,
        ),
        grid=(nc, ngrp),
        in_specs=[xs_s, bm_s, cm_s, dt_s, hv_s, hv_s, hv_s, st_s, y_s],
        out_specs=(
            y_s,
            pl.BlockSpec((Q, N), lambda c, g: (nc - 1 - c, g)),
            pl.BlockSpec((Q, N), lambda c, g: (nc - 1 - c, g)),
            dt_s,
            hv_s,
            hv_s,
            hv_s,
        ),
        scratch_shapes=[pltpu.VMEM((H, P, N), F32)],
        compiler_params=_params("arbitrary", "arbitrary"),
        name=name,
    )(xbc, xbc, xbc, dt_raw, dt_bias, a_log, d_skip, states, dy)
    return jnp.concatenate([dxs, dbm, dcm], axis=1), ddt, dbias, dalog, ddsk


def gnorm_fwd(y, zx, nw, *, name):
    L, DI = y.shape
    gw = DI // SSM_GROUPS
    tl = _pick(L, 512, 16)

    def body(y_ref, z_ref, nw_ref, o_ref):
        yz = y_ref[...] * _silu(z_ref[...])
        r = lax.rsqrt(jnp.mean(yz * yz, axis=-1, keepdims=True) + EPS)
        o_ref[...] = ((yz * r) * nw_ref[...]).astype(o_ref.dtype)

    tile = pl.BlockSpec((tl, gw), lambda i, g: (i, g))
    return pl.pallas_call(
        body,
        out_shape=jax.ShapeDtypeStruct((L, DI), BF16),
        grid=(L // tl, SSM_GROUPS),
        in_specs=[tile, tile, pl.BlockSpec((1, gw), lambda i, g: (0, g))],
        out_specs=tile,
        compiler_params=_params("parallel", "parallel"),
        name=name,
    )(y, zx, nw)


def gnorm_bwd(y, zx, nw, dout, *, name):
    L, DI = y.shape
    gw = DI // SSM_GROUPS
    tl = _pick(L, 512, 16)

    def body(y_ref, z_ref, nw_ref, do_ref, dy_ref, dz_ref, dnw_ref):
        @pl.when(pl.program_id(1) == 0)
        def _():
            dnw_ref[...] = jnp.zeros_like(dnw_ref)

        yv = y_ref[...]
        zv = z_ref[...]
        sz = _silu(zv)
        yz = yv * sz
        r = lax.rsqrt(jnp.mean(yz * yz, axis=-1, keepdims=True) + EPS)
        n = yz * r
        dov = do_ref[...]
        dnw_ref[...] += jnp.sum(dov * n, axis=0, keepdims=True)
        dn = dov * nw_ref[...]
        dyz = r * (dn - n * jnp.mean(dn * n, axis=-1, keepdims=True))
        dy_ref[...] = dyz * sz
        dz_ref[...] = dyz * yv * _dsilu(zv)

    tile = pl.BlockSpec((tl, gw), lambda g, i: (i, g))
    row = pl.BlockSpec((1, gw), lambda g, i: (0, g))
    return pl.pallas_call(
        body,
        out_shape=(
            jax.ShapeDtypeStruct((L, DI), F32),
            jax.ShapeDtypeStruct((L, DI), F32),
            jax.ShapeDtypeStruct((1, DI), F32),
        ),
        grid=(SSM_GROUPS, L // tl),
        in_specs=[tile, tile, row, tile],
        out_specs=(tile, tile, row),
        compiler_params=_params("parallel", "arbitrary"),
        name=name,
    )(y, zx, nw, dout)


def _attn_head(q, kp, kc, vp, vc, sink, has_prev):
    rows, w = q.shape[0], kc.shape[0]
    nt = (((1,), (1,)), ((), ()))
    qb = q.astype(BF16)
    sc = lax.dot_general(qb, kc.astype(BF16), nt, preferred_element_type=F32) * HEAD_DIM ** -0.5
    sp = lax.dot_general(qb, kp.astype(BF16), nt, preferred_element_type=F32) * HEAD_DIM ** -0.5
    ii = jnp.bitwise_and(lax.broadcasted_iota(jnp.int32, (rows, w), 0), w - 1)
    jj = lax.broadcasted_iota(jnp.int32, (rows, w), 1)
    lc = jnp.where(jj <= ii, sc, -jnp.inf)
    lp = jnp.where((jj > ii) & has_prev, sp, -jnp.inf)
    m = jnp.maximum(jnp.maximum(jnp.max(lc, axis=1, keepdims=True), jnp.max(lp, axis=1, keepdims=True)), sink)
    m = lax.stop_gradient(m)
    pc = jnp.exp(lc - m)
    pp = jnp.exp(lp - m)
    denom = jnp.sum(pc, axis=1, keepdims=True) + jnp.sum(pp, axis=1, keepdims=True) + jnp.exp(sink - m)
    o = jnp.dot((pc / denom).astype(BF16), vc.astype(BF16), preferred_element_type=F32)
    return o + jnp.dot((pp / denom).astype(BF16), vp.astype(BF16), preferred_element_type=F32)


def attn_fwd(q, kv, sinks, *, name):
    L, DQ = q.shape
    heads = DQ // HEAD_DIM
    rep = heads // KV_HEADS
    nb = L // WINDOW
    kw = KV_HEADS * HEAD_DIM
    W, HD = WINDOW, HEAD_DIM

    def body(q_ref, kp_ref, kc_ref, vp_ref, vc_ref, s_ref, o_ref):
        has_prev = pl.program_id(0) > 0
        for kh in range(KV_HEADS):
            ks = slice(kh * HD, (kh + 1) * HD)
            hs = [kh * rep + rr for rr in range(rep)]
            o = _attn_head(
                jnp.concatenate([q_ref[:, h * HD : (h + 1) * HD] for h in hs], axis=0),
                kp_ref[:, ks], kc_ref[:, ks], vp_ref[:, ks], vc_ref[:, ks],
                jnp.concatenate([jnp.broadcast_to(s_ref[:, h : h + 1], (W, 1)) for h in hs], axis=0), has_prev,
            )
            for rr, h in enumerate(hs):
                o_ref[:, h * HD : (h + 1) * HD] = o[rr * W : (rr + 1) * W].astype(o_ref.dtype)

    return pl.pallas_call(
        body,
        out_shape=jax.ShapeDtypeStruct((L, DQ), BF16),
        grid=(nb,),
        in_specs=[
            pl.BlockSpec((W, DQ), lambda n: (n, 0)),
            pl.BlockSpec((W, kw), lambda n: (jnp.maximum(n - 1, 0), 0)),
            pl.BlockSpec((W, kw), lambda n: (n, 0)),
            pl.BlockSpec((W, kw), lambda n: (jnp.maximum(n - 1, 0), 1)),
            pl.BlockSpec((W, kw), lambda n: (n, 1)),
            pl.BlockSpec((1, heads), lambda n: (0, 0)),
        ],
        out_specs=pl.BlockSpec((W, DQ), lambda n: (n, 0)),
        compiler_params=_params("parallel"),
        name=name,
    )(q, kv, kv, kv, kv, sinks)


def attn_bwd(q, kv, sinks, do, dkv_in, *, name):
    L, DQ = q.shape
    heads = DQ // HEAD_DIM
    rep = heads // KV_HEADS
    nb = L // WINDOW
    kw = KV_HEADS * HEAD_DIM
    W, HD = WINDOW, HEAD_DIM

    def blk(n):
        return jnp.minimum(n, nb - 1)

    def prev(n):
        return jnp.maximum(blk(n) - 1, 0)

    def outb(n):
        return jnp.maximum(n - 1, 0)

    def body(q_ref, kp_ref, kc_ref, vp_ref, vc_ref, s_ref, do_ref, dki_ref, dvi_ref,
             dq_ref, dk_ref, dv_ref, ds_ref, dk_cur, dv_cur):
        n = pl.program_id(0)
        has_prev = n > 0

        @pl.when(n == 0)
        def _():
            ds_ref[...] = jnp.zeros_like(ds_ref)
            dk_cur[...] = jnp.zeros_like(dk_cur)
            dv_cur[...] = jnp.zeros_like(dv_cur)

        @pl.when(n == nb)
        def _():
            dk_ref[...] = dki_ref[...] + dk_cur[...]
            dv_ref[...] = dvi_ref[...] + dv_cur[...]

        @pl.when(n < nb)
        def _():
            lanes = lax.broadcasted_iota(jnp.int32, (1, heads), 1)
            ds_row = jnp.zeros((1, heads), F32)
            for kh in range(KV_HEADS):
                ks = slice(kh * HD, (kh + 1) * HD)
                hs = [kh * rep + rr for rr in range(rep)]
                _, vjp = jax.vjp(
                    functools.partial(_attn_head, has_prev=has_prev),
                    jnp.concatenate([q_ref[:, h * HD : (h + 1) * HD] for h in hs], axis=0),
                    kp_ref[:, ks], kc_ref[:, ks], vp_ref[:, ks], vc_ref[:, ks],
                    jnp.concatenate([jnp.broadcast_to(s_ref[:, h : h + 1], (W, 1)) for h in hs], axis=0),
                )
                dq, dkp, dkc, dvp, dvc, dsk = vjp(
                    jnp.concatenate([do_ref[:, h * HD : (h + 1) * HD] for h in hs], axis=0))
                for rr, h in enumerate(hs):
                    dq_ref[:, h * HD : (h + 1) * HD] = dq[rr * W : (rr + 1) * W]
                    ds_row = ds_row + jnp.where(lanes == h, jnp.sum(dsk[rr * W : (rr + 1) * W], axis=0, keepdims=True), 0.0)
                dk_ref[:, ks] = dki_ref[:, ks] + dk_cur[:, ks] + dkp
                dv_ref[:, ks] = dvi_ref[:, ks] + dv_cur[:, ks] + dvp
                dk_cur[:, ks] = dkc
                dv_cur[:, ks] = dvc
            ds_ref[...] += ds_row

    dq, dk, dv, ds = pl.pallas_call(
        body,
        out_shape=(
            jax.ShapeDtypeStruct((L, DQ), F32),
            jax.ShapeDtypeStruct((L, kw), F32),
            jax.ShapeDtypeStruct((L, kw), F32),
            jax.ShapeDtypeStruct((1, heads), F32),
        ),
        grid=(nb + 1,),
        in_specs=[
            pl.BlockSpec((W, DQ), lambda n: (blk(n), 0)),
            pl.BlockSpec((W, kw), lambda n: (prev(n), 0)),
            pl.BlockSpec((W, kw), lambda n: (blk(n), 0)),
            pl.BlockSpec((W, kw), lambda n: (prev(n), 1)),
            pl.BlockSpec((W, kw), lambda n: (blk(n), 1)),
            pl.BlockSpec((1, heads), lambda n: (0, 0)),
            pl.BlockSpec((W, DQ), lambda n: (blk(n), 0)),
            pl.BlockSpec((W, kw), lambda n: (outb(n), 0)),
            pl.BlockSpec((W, kw), lambda n: (outb(n), 1)),
        ],
        out_specs=(
            pl.BlockSpec((W, DQ), lambda n: (blk(n), 0)),
            pl.BlockSpec((W, kw), lambda n: (outb(n), 0)),
            pl.BlockSpec((W, kw), lambda n: (outb(n), 0)),
            pl.BlockSpec((1, heads), lambda n: (0, 0)),
        ),
        scratch_shapes=[pltpu.VMEM((W, kw), F32), pltpu.VMEM((W, kw), F32)],
        compiler_params=_params("arbitrary"),
        name=name,
    )(q, kv, kv, kv, kv, sinks, do, dkv_in, dkv_in)
    return dq, jnp.concatenate([dk, dv], axis=1), ds


def final_loss(x, fw, target, *, name):
    L, D = x.shape
    tl = _pick(L, 512, 8)

    def body(x_ref, fw_ref, t_ref, loss_ref, dx_ref, dfw_ref):
        @pl.when(pl.program_id(0) == 0)
        def _():
            loss_ref[...] = jnp.zeros_like(loss_ref)
            dfw_ref[...] = jnp.zeros_like(dfw_ref)

        xv = x_ref[...]
        fwv = fw_ref[...]
        r = lax.rsqrt(jnp.mean(xv * xv, axis=-1, keepdims=True) + EPS)
        xhat = xv * r
        err = xhat * fwv - t_ref[...]
        loss_ref[...] += 0.5 * jnp.sum(jnp.mean(err * err, axis=-1, keepdims=True), axis=0, keepdims=True)
        dy = err * (1.0 / D)
        dfw_ref[...] += jnp.sum(dy * xhat, axis=0, keepdims=True)
        dxhat = dy * fwv
        dx_ref[...] = r * (dxhat - xhat * jnp.mean(dxhat * xhat, axis=-1, keepdims=True))

    tile = pl.BlockSpec((tl, D), lambda i: (i, 0))
    row = pl.BlockSpec((1, D), lambda i: (0, 0))
    return pl.pallas_call(
        body,
        out_shape=(
            jax.ShapeDtypeStruct((1, 1), F32),
            jax.ShapeDtypeStruct((L, D), F32),
            jax.ShapeDtypeStruct((1, D), F32),
        ),
        grid=(L // tl,),
        in_specs=[tile, row, tile],
        out_specs=(pl.BlockSpec((1, 1), lambda i: (0, 0)), tile, row),
        compiler_params=_params("arbitrary"),
        name=name,
    )(x, fw, target)


def outer8(ct, d, *, name):
    D, B = ct.shape
    S, _, N = d.shape
    tm = _pick(D, 512, 8)
    tn = _pick(N, 256, LANE)

    def body(c_ref, d_ref, o_ref):
        acc = c_ref[:, 0:1] * d_ref[0:1, :]
        for b in range(1, B):
            acc = acc + c_ref[:, b : b + 1] * d_ref[b : b + 1, :]
        o_ref[...] = acc

    return pl.pallas_call(
        body,
        out_shape=jax.ShapeDtypeStruct((S, D, N), F32),
        grid=(S, D // tm, N // tn),
        in_specs=[
            pl.BlockSpec((tm, B), lambda s, i, j: (i, 0)),
            pl.BlockSpec((None, B, tn), lambda s, i, j: (s, 0, j)),
        ],
        out_specs=pl.BlockSpec((None, tm, tn), lambda s, i, j: (s, i, j)),
        compiler_params=_params("parallel", "parallel", "parallel"),
        name=name,
    )(ct, d)


def reduce8(g, *, name):
    nd, R, N = g.shape

    def body(g_ref, o_ref):
        acc = g_ref[0]
        for b in range(1, nd):
            acc = acc + g_ref[b]
        o_ref[...] = acc

    return pl.pallas_call(
        body,
        out_shape=jax.ShapeDtypeStruct((R, N), F32),
        name=name,
    )(g)


def _as3(a):
    if a.ndim == 1:
        return a.reshape(1, 1, -1)
    if a.ndim == 2:
        return a.reshape((1,) + a.shape)
    return a.reshape((-1,) + a.shape[-2:])


def adamw(w, g, m, v, *, name):
    shape = w.shape
    w3, g3, m3, v3 = _as3(w), _as3(g), _as3(m), _as3(v)
    B, R, C = w3.shape
    tr = _pick(R, max(8, (1 << 19) // max(C, 1) // 8 * 8), 8)

    def body(w_ref, g_ref, m_ref, v_ref, d_ref, nm_ref, nv_ref):
        gv = g_ref[...]
        mn = ADAM_B1 * m_ref[...] + (1.0 - ADAM_B1) * gv
        vn = ADAM_B2 * v_ref[...] + (1.0 - ADAM_B2) * (gv * gv)
        m_hat = mn / (1.0 - ADAM_B1 ** ADAM_STEP)
        v_hat = vn / (1.0 - ADAM_B2 ** ADAM_STEP)
        d_ref[...] = -ADAM_LR * (m_hat / (jnp.sqrt(v_hat) + ADAM_EPS) + ADAM_WD * w_ref[...])
        nm_ref[...] = mn
        nv_ref[...] = vn

    tile = pl.BlockSpec((None, tr, C), lambda b, i: (b, i, 0))
    sds = jax.ShapeDtypeStruct((B, R, C), F32)
    d, nm, nv = pl.pallas_call(
        body,
        out_shape=(sds, sds, sds),
        grid=(B, R // tr),
        in_specs=[tile, tile, tile, tile],
        out_specs=(tile, tile, tile),
        compiler_params=_params("parallel", "parallel"),
        name=name,
    )(w3, g3, m3, v3)
    return d.reshape(shape), nm.reshape(shape), nv.reshape(shape)


def _place():
    return lax.axis_index("x"), lax.axis_index("y"), lax.axis_index("c")


def _flip(v, bit):
    return (1 - v) if bit else v


def ag8(v, *, act=None, after=None, name):
    R, N = v.shape
    extra = [] if after is None else [after]

    def body(*refs):
        v_ref = refs[0]
        out_ref, stage, send_sems, recv_sems = refs[1 + len(extra):]
        x, y, c = _place()
        me = 4 * x + 2 * y + c
        val = v_ref[...]
        if act is not None:
            val = act(val)
        stage[...] = val
        out_ref[me] = val
        sends = []
        for k in range(1, N_DEV):
            px, py, pc = _flip(x, k & 4), _flip(y, k & 2), _flip(c, k & 1)
            cp = pltpu.make_async_remote_copy(
                src_ref=stage, dst_ref=out_ref.at[me], send_sem=send_sems.at[k - 1], recv_sem=recv_sems.at[k - 1],
                device_id=(px, py, pc), device_id_type=MESH,
            )
            cp.start()
            sends.append(cp)
        for k in range(1, N_DEV):
            px, py, pc = _flip(x, k & 4), _flip(y, k & 2), _flip(c, k & 1)
            pltpu.make_async_remote_copy(
                src_ref=stage, dst_ref=out_ref.at[4 * px + 2 * py + pc], send_sem=send_sems.at[k - 1],
                recv_sem=recv_sems.at[k - 1], device_id=(px, py, pc), device_id_type=MESH,
            ).wait_recv()
        for cp in sends:
            cp.wait_send()

    return pl.pallas_call(
        body,
        out_shape=jax.ShapeDtypeStruct((N_DEV, R, N), F32),
        in_specs=[pl.BlockSpec(memory_space=pltpu.VMEM)] + [_ANY] * len(extra),
        out_specs=pl.BlockSpec(memory_space=pltpu.VMEM),
        scratch_shapes=[
            pltpu.VMEM((R, N), F32),
            pltpu.SemaphoreType.DMA((N_DEV - 1,)),
            pltpu.SemaphoreType.DMA((N_DEV - 1,)),
        ],
        name=name,
    )(v, *extra)


def _other_chips(x, y):
    chips = [(1 - x, y), (x, 1 - y), (1 - x, 1 - y)]
    return chips, [2 * px + py for px, py in chips]


_HBM = pl.BlockSpec(memory_space=pltpu.HBM)


_SEM = pl.BlockSpec(memory_space=pltpu.SEMAPHORE)
_ANY = pl.BlockSpec(memory_space=pl.ANY)
_EFFECT = pltpu.SideEffectType.DATAFLOW_SIDE_EFFECTING


def _gather_copies(srcs, lands, send_sems, recv_sems):
    x, y, c = _place()
    k_me = 2 * x + y
    chips, kidx = _other_chips(x, y)
    cps = []
    for w in range(len(srcs)):
        for j, (px, py) in enumerate(chips):
            def copy(dst, w=w, j=j, px=px, py=py):
                return pltpu.make_async_remote_copy(
                    src_ref=srcs[w].at[:, c], dst_ref=dst, send_sem=send_sems.at[3 * w + j],
                    recv_sem=recv_sems.at[3 * w + j], device_id=(px, py, c), device_id_type=MESH,
                )
            cps.append((copy(lands[w].at[:, k_me, c]), copy(lands[w].at[:, kidx[j], c])))
    return cps


def _fill_copies(srcs, lands, send_sems, recv_sems):
    x, y, c = _place()
    k_me = 2 * x + y
    _, kidx = _other_chips(x, y)
    sib = (x, y, 1 - c)
    cps = []
    for w in range(len(srcs)):
        own = pltpu.make_async_remote_copy(
            src_ref=srcs[w], dst_ref=lands[w].at[:, k_me], send_sem=send_sems.at[4 * w + 3], recv_sem=recv_sems.at[4 * w + 3],
            device_id=sib, device_id_type=MESH,
        )
        cps.append((own, own))
        for j in range(3):
            def copy(half, w=w, j=j):
                part = lands[w].at[:, kidx[j], half]
                return pltpu.make_async_remote_copy(
                    src_ref=part, dst_ref=part, send_sem=send_sems.at[4 * w + j], recv_sem=recv_sems.at[4 * w + j],
                    device_id=sib, device_id_type=MESH,
                )
            cps.append((copy(c), copy(1 - c)))
    return cps


def _sibling_copies(srcs, lands, send_sems, recv_sems):
    x, y, c = _place()
    cps = []
    for w in range(len(srcs)):
        cp = pltpu.make_async_remote_copy(
            src_ref=srcs[w].at[:, :, 1 - c], dst_ref=lands[w], send_sem=send_sems.at[w], recv_sem=recv_sems.at[w],
            device_id=(x, y, 1 - c), device_id_type=MESH,
        )
        cps.append((cp, cp))
    return cps


def _reduce_copies(srcs, lands, send_sems, recv_sems):
    x, y, c = _place()
    chips, kidx = _other_chips(x, y)
    cps = []
    for w in range(len(srcs)):
        for j, (px, py) in enumerate(chips):
            cp = pltpu.make_async_remote_copy(
                src_ref=srcs[w].at[:, kidx[j]], dst_ref=lands[w].at[j], send_sem=send_sems.at[3 * w + j],
                recv_sem=recv_sems.at[3 * w + j], device_id=(px, py, c), device_id_type=MESH,
            )
            cps.append((cp, cp))
    return cps


def split_start(copies, srcs, land_shapes, after, *, name, lands=None, per=3):
    n = len(srcs)

    def body(*refs):
        src_refs, land_refs = refs[:n], refs[n : 2 * n]
        send_sems, recv_sems = refs[2 * n + 1], refs[2 * n + 2]
        token = refs[-1]
        for cp, _ in copies(src_refs, land_refs, send_sems, recv_sems):
            cp.start()
        token[...] = jnp.zeros_like(token)

    if lands is None:
        lands = [lax.empty(sh, s.dtype) for sh, s in zip(land_shapes, srcs)]
    land_shapes = [a.shape for a in lands]
    lands = [pltpu.with_memory_space_constraint(a, pltpu.HBM) for a in lands]
    srcs = [pltpu.with_memory_space_constraint(s, pltpu.HBM) for s in srcs]
    out = pl.pallas_call(
        body,
        out_shape=(
            pltpu.SemaphoreType.DMA((per * n,)), pltpu.SemaphoreType.DMA((per * n,)),
            *[pltpu.HBM(s.shape, s.dtype) for s in srcs],
            *[pltpu.HBM(sh, s.dtype) for sh, s in zip(land_shapes, srcs)],
            jax.ShapeDtypeStruct((8, LANE), F32),
        ),
        in_specs=[_HBM] * (2 * n) + [_ANY],
        out_specs=(_SEM, _SEM, *([_HBM] * (2 * n)), pl.BlockSpec(memory_space=pltpu.VMEM)),
        input_output_aliases={i: 2 + i for i in range(2 * n)},
        compiler_params=pltpu.CompilerParams(has_side_effects=_EFFECT),
        name=name,
    )(*srcs, *lands, after)
    return out[0], out[1], list(out[2 : 2 + n]), list(out[2 + n : 2 + 2 * n]), out[-1]


def split_wait(copies, send_sems, recv_sems, srcs, lands, after, *, name):
    n = len(srcs)

    def body(*refs):
        src_refs, land_refs = refs[:n], refs[n : 2 * n]
        send_ref, recv_ref = refs[2 * n], refs[2 * n + 1]
        for sent, arrives in copies(src_refs, land_refs, send_ref, recv_ref):
            sent.wait_send()
            arrives.wait_recv()

    out = pl.pallas_call(
        body,
        out_shape=tuple(pltpu.HBM(a.shape, a.dtype) for a in list(srcs) + list(lands)),
        in_specs=[_HBM] * (2 * n) + [_SEM, _SEM, _ANY],
        out_specs=tuple([_HBM] * (2 * n)),
        input_output_aliases={i: i for i in range(2 * n)},
        compiler_params=pltpu.CompilerParams(has_side_effects=_EFFECT),
        name=name,
    )(*srcs, *lands, send_sems, recv_sems, after)
    return list(out[:n]), list(out[n:])


def rs_share(halves, *, name):
    n = len(halves)

    def body(*refs):
        outs = refs[n : 2 * n]
        send_sems, recv_sems = refs[2 * n :]
        x, y, c = _place()
        cps = []
        for w in range(n):
            cp = pltpu.make_async_remote_copy(
                src_ref=outs[w].at[:, c], dst_ref=outs[w].at[:, c], send_sem=send_sems.at[w], recv_sem=recv_sems.at[w],
                device_id=(x, y, 1 - c), device_id_type=MESH,
            )
            cp.start()
            cps.append(cp)
        for w, cp in enumerate(cps):
            cp.wait_send()
            pltpu.make_async_remote_copy(
                src_ref=outs[w].at[:, c], dst_ref=outs[w].at[:, 1 - c], send_sem=send_sems.at[w], recv_sem=recv_sems.at[w],
                device_id=(x, y, 1 - c), device_id_type=MESH,
            ).wait_recv()

    return pl.pallas_call(
        body,
        out_shape=tuple(jax.ShapeDtypeStruct(h.shape, h.dtype) for h in halves),
        in_specs=[_HBM] * n,
        out_specs=tuple([_HBM] * n),
        scratch_shapes=[pltpu.SemaphoreType.DMA((n,)), pltpu.SemaphoreType.DMA((n,))],
        input_output_aliases={w: w for w in range(n)},
        name=name,
    )(*halves)


def _row_tile(R, C):
    return _pick(R, max(16, (1 << 19) // C // 16 * 16), 16)


def _my_core():
    return lax.axis_index("c")


def _my_chip():
    return 2 * lax.axis_index("x") + lax.axis_index("y")


def rs_add_pair(g, r, *, name):
    M, K, _, R, C = g.shape
    tr = _row_tile(R, C)

    def body(g_ref, r_ref, o_ref):
        o_ref[...] = (g_ref[...].astype(F32) + r_ref[...].astype(F32)).astype(o_ref.dtype)

    blk = pl.BlockSpec((None, K, tr, C), lambda m, i: (m, 0, i, 0))
    return pl.pallas_call(
        body,
        out_shape=jax.ShapeDtypeStruct((M, K, R, C), BF16),
        grid=(M, R // tr),
        in_specs=[pl.BlockSpec((None, K, None, tr, C), lambda m, i: (m, 0, _my_core(), i, 0)), blk],
        out_specs=blk,
        compiler_params=_params("parallel", "parallel"),
        name=name,
    )(g, r)


def rs_add_final(g, r, t, full, m0, *, name):
    M, K, _, R, C = g.shape
    tr = _row_tile(R, C)

    def body(g_ref, r_ref, t_ref, full_ref, o_ref):
        acc = g_ref[...].astype(F32) + r_ref[...].astype(F32)
        for j in range(3):
            acc = acc + t_ref[j].astype(F32)
        o_ref[...] = acc

    return pl.pallas_call(
        body,
        out_shape=jax.ShapeDtypeStruct(full.shape, full.dtype),
        grid=(M, R // tr),
        in_specs=[
            pl.BlockSpec((None, None, None, tr, C), lambda m, i: (m, _my_chip(), _my_core(), i, 0)),
            pl.BlockSpec((None, None, tr, C), lambda m, i: (m, _my_chip(), i, 0)),
            pl.BlockSpec((3, None, tr, C), lambda m, i: (0, m, i, 0)),
            _ANY,
        ],
        out_specs=pl.BlockSpec((None, None, tr, C), lambda m, i: (m0 + m, _my_core(), i, 0)),
        input_output_aliases={3: 0},
        compiler_params=_params("parallel", "parallel"),
        name=name,
    )(g, r, t, full)


WEIGHTS = ["ffn_norm_w", "ffn_w_gu", "ffn_w_down", "mod_w", "mod_b", "mix_norm_w", "ssm_w_in", "ssm_conv_w", "ssm_conv_b",
           "ssm_dt_bias", "ssm_a_log", "ssm_d", "ssm_norm_w", "ssm_w_out", "kv_norm_w", "kv_mod_w", "kv_mod_b", "w_kv", "b_kv",
           "attn_w_q", "attn_b_q", "attn_sinks", "attn_w_o", "attn_b_o", "final_norm_w"]
GATHERED = ["ffn_w_gu", "ffn_w_down", "ssm_w_in", "ssm_w_out", "w_kv", "attn_w_q", "attn_w_o"]
COLUMN_PARALLEL = ["mod_w", "kv_mod_w"]
SMALL_SHARDED = ["ffn_norm_w", "ssm_conv_w", "ssm_conv_b", "ssm_norm_w"]
SMALL = [n for n in WEIGHTS if n not in GATHERED and n not in COLUMN_PARALLEL]


def _row_halves(a):
    a = a.reshape((-1,) + a.shape[-2:])
    return a.reshape(a.shape[0], 2, a.shape[1] // 2, a.shape[2])


def _pack(arrs, rows=8):
    flat = jnp.concatenate([a.reshape(-1) for a in arrs])
    n = flat.shape[0]
    pad = (-n) % (rows * LANE)
    return jnp.pad(flat, (0, pad)).reshape(rows, -1), n


def _unpack(flat, like):
    out, o = [], 0
    for s in like:
        k = 1
        for d in s:
            k *= d
        out.append(flat[o : o + k].reshape(s))
        o += k
    return out


def kernel(x, c, ffn_norm_w, ffn_w_gu, ffn_w_down, mod_w, mod_b, mix_norm_w, ssm_w_in, ssm_conv_w, ssm_conv_b, ssm_dt_bias, ssm_a_log, ssm_d, ssm_norm_w, ssm_w_out, kv_norm_w, kv_mod_w, kv_mod_b, w_kv, b_kv, attn_w_q, attn_b_q, attn_sinks, attn_w_o, attn_b_o, final_norm_w, loss_target, m_ffn_norm_w, m_ffn_w_gu, m_ffn_w_down, m_mod_w, m_mod_b, m_mix_norm_w, m_ssm_w_in, m_ssm_conv_w, m_ssm_conv_b, m_ssm_dt_bias, m_ssm_a_log, m_ssm_d, m_ssm_norm_w, m_ssm_w_out, m_kv_norm_w, m_kv_mod_w, m_kv_mod_b, m_w_kv, m_b_kv, m_attn_w_q, m_attn_b_q, m_attn_sinks, m_attn_w_o, m_attn_b_o, m_final_norm_w, v_ffn_norm_w, v_ffn_w_gu, v_ffn_w_down, v_mod_w, v_mod_b, v_mix_norm_w, v_ssm_w_in, v_ssm_conv_w, v_ssm_conv_b, v_ssm_dt_bias, v_ssm_a_log, v_ssm_d, v_ssm_norm_w, v_ssm_w_out, v_kv_norm_w, v_kv_mod_w, v_kv_mod_b, v_w_kv, v_b_kv, v_attn_w_q, v_attn_b_q, v_attn_sinks, v_attn_w_o, v_attn_b_o, v_final_norm_w):
    env = dict(locals())
    W = {n: env[n] for n in WEIGHTS}
    MOM = {n: env["m_" + n] for n in WEIGHTS}
    VAR = {n: env["v_" + n] for n in WEIGHTS}

    ax, ay, ac = _place()
    kme = 2 * ax + ay
    me = 4 * ax + 2 * ay + ac

    xs = x[0]
    target = loss_target[0]
    L, D = xs.shape
    depth, n_a = ffn_w_gu.shape[0], ssm_w_in.shape[0]
    n_b = depth - n_a
    T = ffn_w_gu.shape[-1]
    DI = ssm_w_out.shape[1] * N_CHIPS
    CI = ssm_w_in.shape[2]
    CC = ssm_conv_w.shape[2] * N_CHIPS
    MW = mod_w.shape[2]
    KW = kv_mod_w.shape[1]
    KVD = w_kv.shape[1]

    def chip_cols(a, width):
        return lax.dynamic_slice_in_dim(a, kme * width, width, axis=a.ndim - 1)

    def ffn_items(i, j):
        return [("ffn_w_gu", 2 * i + j, ffn_w_gu[i, j]), ("ffn_w_down", 2 * i + j, ffn_w_down[i, j])]

    def mix_items(i):
        if i < n_a:
            return [("ssm_w_in", i, ssm_w_in[i]), ("ssm_w_out", i, ssm_w_out[i])]
        return [("attn_w_q", i - n_a, attn_w_q[i - n_a]), ("attn_w_o", i - n_a, attn_w_o[i - n_a])]

    def layer_items(i, order):
        kv_items = [("w_kv", 0, w_kv)] if i == n_a else []
        if order == "fwd":
            return kv_items + ffn_items(i, 0) + mix_items(i) + ffn_items(i, 1)
        return ffn_items(i, 1) + mix_items(i) + ffn_items(i, 0) + kv_items

    def sub_items(sub):
        return {"F": ffn_items, "M": mix_items, "KV": lambda: [("w_kv", 0, w_kv)]}[sub[0]](*sub[1:])

    subs = []
    for i in range(depth):
        subs += ([("KV",)] if i == n_a else []) + [("F", i, 0), ("M", i), ("F", i, 1)]
    assert n_a >= 2 and depth >= 2
    cuts = [0, 1, 2, 4, 6] + [3 * (i + 1) + (1 if i >= n_a else 0) for i in range(2, depth)]
    fwd_plan = [subs[a:b] for a, b in zip(cuts[:-1], cuts[1:])]
    fwd_stages = [[it for sub in stage for it in sub_items(sub)] for stage in fwd_plan]
    bwd_stages = [layer_items(i, "bwd") for i in range(depth - 1, 0, -1)] + [ffn_items(0, 1) + mix_items(0), ffn_items(0, 0)]

    gw, inflight = {}, {}

    def chips_begin(s, after):
        keys = [(n, m0) for n, m0, _ in fwd_stages[s]]
        shards = [_row_halves(a.astype(BF16)) for _, _, a in fwd_stages[s]]
        land_shapes = [(sh.shape[0], N_CHIPS) + sh.shape[1:] for sh in shards]
        ssem, rsem, srcs, lands, token = split_start(_gather_copies, shards, land_shapes, after, name=f"gather_start_{s}")
        inflight[s] = (keys, ssem, rsem, srcs, lands)
        return token[0:1, 0:1]

    def n_head(s):
        return 2 if fwd_plan[s][0][0] == "KV" else 1

    def n_first(s):
        return sum(len(sub_items(sub)) for sub in fwd_plan[s][: n_head(s)])

    def cores_begin(s, after):
        keys, ssem, rsem, srcs, lands = inflight.pop(s)
        srcs, lands = split_wait(_gather_copies, ssem, rsem, srcs, lands, after, name=f"gather_wait_{s}")
        k = n_first(s)
        a_sem, b_sem, a_src, a_land, tok = split_start(
            _fill_copies, srcs[:k], None, after, name=f"fill_start_{s}a", lands=lands[:k], per=4)
        if k < len(keys):
            inflight[s] = (keys[k:],) + split_start(
                _fill_copies, srcs[k:], None, tok, name=f"fill_start_{s}b", lands=lands[k:], per=4)
            tok = inflight[s][-1]
        gw.update(zip(keys[:k], split_wait(_fill_copies, a_sem, b_sem, a_src, a_land, tok, name=f"fill_wait_{s}a")[1]))
        return tok[0:1, 0:1]

    def cores_end(s, after):
        if s in inflight:
            keys, ssem, rsem, srcs, lands, _ = inflight.pop(s)
            gw.update(zip(keys, split_wait(_fill_copies, ssem, rsem, srcs, lands, after, name=f"fill_wait_{s}b")[1]))

    def g_gu(i, j):
        return gw["ffn_w_gu", 2 * i + j].reshape(N_CHIPS, D, T)

    def g_dn(i, j):
        return gw["ffn_w_down", 2 * i + j].reshape(2, T, D)

    def g_full(n, m0):
        a = gw[n, m0]
        return a.reshape(N_CHIPS * 2 * a.shape[-2], a.shape[-1])

    sm_like = [W[n].shape for n in SMALL_SHARDED]
    sm_pack, sm_n = _pack([W[n] for n in SMALL_SHARDED])
    sm_gathered = ag8(sm_pack, name="ag_small_w")
    sm_all = sm_gathered[0::2].reshape(N_CHIPS, -1)[:, :sm_n]
    full = {}
    for n, part in zip(SMALL_SHARDED, zip(*[_unpack(sm_all[k], sm_like) for k in range(N_CHIPS)])):
        full[n] = jnp.concatenate(part, axis=-1)

    c_all = ag8(c, act=_silu, after=sm_gathered, name="ag_c").reshape(N_DEV, D)
    c_all = c_all + chips_begin(0, c_all)
    p_mod = mm(c_all, mod_w, bias=chip_cols(mod_b, MW)[:, None, :], name="mod_mm")
    p_kv = mm(c_all, kv_mod_w, bias=chip_cols(kv_mod_b, KW)[None, :], name="kvmod_mm")
    p_all = jnp.concatenate([jnp.transpose(p_mod, (1, 0, 2)).reshape(N_DEV, depth * MW), p_kv], axis=1)
    p_mine = lax.dynamic_index_in_dim(ag8(p_all, name="ag_mod")[0::2], me, axis=1, keepdims=False)
    mod = jnp.transpose(p_mine[:, : depth * MW].reshape(N_CHIPS, depth, MW), (1, 0, 2)).reshape(depth, N_MOD * D)
    kvmod = p_mine[:, depth * MW :].reshape(1, 2 * D)
    mods = [[mod[i : i + 1, j * D : (j + 1) * D] for j in range(N_MOD)] for i in range(depth)]
    kv_shift, kv_scale = kvmod[:, :D], kvmod[:, D:]

    def ffn_fwd(xin, i, j, sh, sc, gt):
        h = norm_mod_fwd(xin, full["ffn_norm_w"][i, j][None], sc, sh, name=f"ffn_norm_{i}_{j}")
        gu, a = ffn_up(h, g_gu(i, j), name=f"ffn_gu_{i}_{j}")
        f, xo = mm(a, g_dn(i, j), reduce_s=True, resid=(xin, gt, FFN_HALF), name=f"ffn_down_{i}_{j}")
        return xo, (xin, gu, a, f, h)

    def ssm_fwd(xin, i, sh, sc, gt):
        h = norm_mod_fwd(xin, mix_norm_w[i][None], sc, sh, name=f"mix_norm_{i}")
        zx4 = mm(h, gw["ssm_w_in", i].reshape(N_CHIPS, D, CI), name=f"ssm_in_{i}")
        zx = jnp.transpose(zx4, (1, 0, 2)).reshape(L, N_CHIPS * CI)
        xbc = conv_fwd(zx, full["ssm_conv_w"][i], full["ssm_conv_b"][i][None], DI, name=f"ssm_conv_{i}")
        dt_raw = zx[:, DI + CC :]
        y, states = ssd_fwd(xbc, dt_raw, ssm_dt_bias[i][None], ssm_a_log[i][None], ssm_d[i][None], DI, name=f"ssd_{i}")
        yn = gnorm_fwd(y, zx, full["ssm_norm_w"][i][None], name=f"ssm_gnorm_{i}")
        f, xo = mm(yn, g_full("ssm_w_out", i), resid=(xin, gt, 1.0), name=f"ssm_out_{i}")
        return xo, (xin, zx, xbc, dt_raw, y, states, yn, f, h)

    def att_fwd(xin, i, kv, sh, sc, gt):
        l = i - n_a
        h = norm_mod_fwd(xin, mix_norm_w[i][None], sc, sh, name=f"mix_norm_{i}")
        q = mm(h, g_full("attn_w_q", l), bias=attn_b_q[l][None], name=f"att_q_{i}")
        o = attn_fwd(q, kv, attn_sinks[l][None], name=f"att_{i}")
        f, xo = mm(o, g_full("attn_w_o", l), bias=attn_b_o[l][None], resid=(xin, gt, 1.0), name=f"att_o_{i}")
        return xo, (xin, q, o, f, h)

    saved = [[None, None, None] for _ in range(depth)]
    xc = xs
    kv = x_kv = None
    n_stage = len(fwd_stages)

    def run_sub(sub, xc, tok):
        nonlocal kv, x_kv
        if sub[0] == "KV":
            x_kv = xc
            hkv = norm_mod_fwd(xc, kv_norm_w[None], kv_scale, kv_shift + tok, name="kv_norm")
            kv = mm(hkv, g_full("w_kv", 0), bias=b_kv[None], name="kv_proj")
            return xc
        i = sub[1]
        if sub[0] == "F":
            sh, sc, gt = mods[i][6 * sub[2] : 6 * sub[2] + 3]
            xc, saved[i][2 * sub[2]] = ffn_fwd(xc, i, sub[2], sh + tok, sc, gt)
            return xc
        sh, sc, gt = mods[i][3:6]
        xc, saved[i][1] = ssm_fwd(xc, i, sh + tok, sc, gt) if i < n_a else att_fwd(xc, i, kv, sh + tok, sc, gt)
        return xc

    def run_stage(s, xc, tok):
        for k, sub in enumerate(fwd_plan[s]):
            if k == n_head(s):
                cores_end(s, xc)
            xc = run_sub(sub, xc, tok if k < n_head(s) else 0.0)
        return xc

    dep = cores_begin(0, kvmod)
    for s in range(n_stage):
        tok = chips_begin(s + 1, dep) if s + 1 < n_stage else 0.0
        xc = run_stage(s, xc, tok)
        if s + 1 < n_stage:
            dep = cores_begin(s + 1, xc)

    loss_part, dx, d_final = final_loss(xc, final_norm_w[None], target, name="loss_head")
    loss = lax.psum(loss_part[0, 0], ("x", "y", "c"))

    wg = {}
    sg = {
        "ffn_norm_w": [[None, None] for _ in range(depth)], "mix_norm_w": [None] * depth, "mod": [None] * depth,
        "ssm_conv_w": [None] * n_a, "ssm_conv_b": [None] * n_a, "ssm_dt_bias": [None] * n_a, "ssm_a_log": [None] * n_a,
        "ssm_d": [None] * n_a, "ssm_norm_w": [None] * n_a, "attn_b_q": [None] * n_b, "attn_sinks": [None] * n_b,
        "attn_b_o": [None] * n_b,
    }

    def ffn_bwd(dxo, i, j, sv, sh, sc, gt):
        xin, gu, a, f, h = sv
        df, dgt, _ = gate_bwd(f, dxo, gt, FFN_HALF, name=f"ffn_res_bwd_{i}_{j}")
        dgu = ffn_down_bwd(df, g_dn(i, j), gu, name=f"ffn_down_dx_{i}_{j}")
        wg["ffn_w_down", 2 * i + j] = mm(a, df, mode="tn", out_dtype=BF16, name=f"ffn_down_dw_{i}_{j}")
        nw = full["ffn_norm_w"][i, j][None]
        dxi, dnw, dsc, dsh = mm_norm_bwd(dgu, g_gu(i, j), xin, nw, sc, dxo, name=f"ffn_gu_dx_{i}_{j}")
        wg["ffn_w_gu", 2 * i + j] = mm(h, dgu, mode="tn", out_dtype=BF16, name=f"ffn_gu_dw_{i}_{j}")
        sg["ffn_norm_w"][i][j] = dnw
        return dxi, (dsh, dsc, dgt)

    def ssm_bwd(dxo, i, sv, sh, sc, gt):
        xin, zx, xbc, dt_raw, y, states, yn, f, h = sv
        df, dgt, _ = gate_bwd(f, dxo, gt, 1.0, name=f"mix_res_bwd_{i}")
        dyn = mm(df, g_full("ssm_w_out", i), mode="nt", name=f"ssm_out_dx_{i}")
        wg["ssm_w_out", i] = mm(yn, df, mode="tn", out_dtype=BF16, name=f"ssm_out_dw_{i}")
        dy, dz, dnorm = gnorm_bwd(y, zx, full["ssm_norm_w"][i][None], dyn, name=f"ssm_gnorm_bwd_{i}")
        dxbc, ddt, dbias, dalog, ddsk = ssd_bwd(
            xbc, dt_raw, ssm_dt_bias[i][None], ssm_a_log[i][None], ssm_d[i][None], states, dy, DI, name=f"ssd_bwd_{i}"
        )
        du, dcw, dcb = conv_bwd(zx, full["ssm_conv_w"][i], full["ssm_conv_b"][i][None], dxbc, DI, name=f"ssm_conv_bwd_{i}")
        dzx = jnp.concatenate([dz, du, ddt], axis=1).astype(BF16)
        dzx4 = jnp.transpose(dzx.reshape(L, N_CHIPS, CI), (1, 0, 2))
        nw = mix_norm_w[i][None]
        dxi, dnw, dsc, dsh = mm_norm_bwd(
            dzx4, gw["ssm_w_in", i].reshape(N_CHIPS, D, CI), xin, nw, sc, dxo, name=f"ssm_in_dx_{i}")
        wg["ssm_w_in", i] = mm(h, dzx4, mode="tn", out_dtype=BF16, name=f"ssm_in_dw_{i}")
        sg["mix_norm_w"][i] = dnw
        sg["ssm_conv_w"][i], sg["ssm_conv_b"][i], sg["ssm_norm_w"][i] = dcw, dcb, dnorm
        sg["ssm_dt_bias"][i], sg["ssm_a_log"][i], sg["ssm_d"][i] = dbias, dalog, ddsk
        return dxi, (dsh, dsc, dgt)

    def att_bwd(dxo, i, sv, dkv, sh, sc, gt):
        l = i - n_a
        xin, q, o, f, h = sv
        df, dgt, dfsum = gate_bwd(f, dxo, gt, 1.0, name=f"mix_res_bwd_{i}")
        do = mm(df, g_full("attn_w_o", l), mode="nt", name=f"att_o_dx_{i}")
        wg["attn_w_o", l] = mm(o, df, mode="tn", out_dtype=BF16, name=f"att_o_dw_{i}")
        dq, dkv, dsink = attn_bwd(q, kv, attn_sinks[l][None], do, dkv, name=f"att_bwd_{i}")
        nw = mix_norm_w[i][None]
        dxi, dnw, dsc, dsh = mm_norm_bwd(dq, g_full("attn_w_q", l), xin, nw, sc, dxo, name=f"att_q_dx_{i}")
        wg["attn_w_q", l] = mm(h, dq, mode="tn", out_dtype=BF16, name=f"att_q_dw_{i}")
        sg["mix_norm_w"][i] = dnw
        sg["attn_b_q"][l], sg["attn_sinks"][l], sg["attn_b_o"][l] = colsum(dq, name=f"att_bq_{i}"), dsink, dfsum
        return dxi, dkv, (dsh, dsc, dgt)

    gfull = {n: lax.empty(_row_halves(W[n]).shape, F32) for n in GATHERED}

    pending, sib = [], []

    def sibling_begin(items, after):
        parts = []
        for n, m0, a in items:
            m, _, rh, cc = _row_halves(a).shape
            parts.append(wg.pop((n, m0)).reshape(m, N_CHIPS, 2, rh, cc))
        land_shapes = [p.shape[:2] + p.shape[3:] for p in parts]
        tag = f"{items[0][0]}_{items[0][1]}"
        ssem, rsem, srcs, lands, token = split_start(
            _sibling_copies, parts, land_shapes, after, name=f"rs_sibling_start_{tag}", per=1)
        sib.append((tag, [(n, m0) for n, m0, _ in items], ssem, rsem, srcs, lands))
        return token[0:1, 0:1]

    def reduce_step(s, after):
        if pending:
            reduce_end(after)
        items = bwd_stages[s]
        mine, theirs = {}, {}
        while sib:
            tag, keys, ssem, rsem, srcs, lands = sib.pop(0)
            srcs, lands = split_wait(_sibling_copies, ssem, rsem, srcs, lands, after, name=f"rs_sibling_wait_{tag}")
            mine.update(zip(keys, srcs))
            theirs.update(zip(keys, lands))
        parts = [mine[n, m0] for n, m0, _ in items]
        from_sib = [theirs[n, m0] for n, m0, _ in items]
        pair = [rs_add_pair(g, r, name=f"rs_pair_{n}_{m0}") for (n, m0, _), g, r in zip(items, parts, from_sib)]
        land_shapes = [(3, p.shape[0]) + p.shape[2:] for p in pair]
        ssem, rsem, srcs, lands, token = split_start(_reduce_copies, pair, land_shapes, after, name=f"rs_chips_start_{s}")
        pending.append((s, items, parts, from_sib, ssem, rsem, srcs, lands))
        return token[0:1, 0:1]

    def reduce_end(after):
        s, items, parts, from_sib, ssem, rsem, srcs, lands = pending.pop()
        _, from_chips = split_wait(_reduce_copies, ssem, rsem, srcs, lands, after, name=f"rs_chips_wait_{s}")
        for (n, m0, _), g, r, t in zip(items, parts, from_sib, from_chips):
            gfull[n] = rs_add_final(g, r, t, gfull[n], m0, name=f"rs_final_{n}_{m0}")

    dkv = jnp.zeros((L, KVD), F32)
    d_kvnorm = d_kvmod = d_bkv = None
    tok = 0.0
    for i in reversed(range(depth)):
        sh1, sc1, g1, shm, scm, gm, sh2, sc2, g2 = mods[i]
        s1, sm, s2 = saved[i]
        dx, dm2 = ffn_bwd(dx, i, 1, s2, sh2, sc2, g2 + tok)
        tok = sibling_begin(ffn_items(i, 1), dx)
        if i < n_a:
            dx, dmm = ssm_bwd(dx, i, sm, shm, scm, gm + tok)
        else:
            dx, dkv, dmm = att_bwd(dx, i, sm, dkv, shm, scm, gm + tok)
        tok = sibling_begin(mix_items(i), dx)
        if i == 0:
            tok = tok + reduce_step(depth - 1, dx)
        dx, dm1 = ffn_bwd(dx, i, 0, s1, sh1, sc1, g1 + tok)
        tok = sibling_begin(ffn_items(i, 0), dx)
        sg["mod"][i] = jnp.concatenate(list(dm1) + list(dmm) + list(dm2), axis=1)
        if i == n_a:
            d_bkv = colsum(dkv, name="kv_bias_bwd")
            hkv = norm_mod_fwd(x_kv, kv_norm_w[None], kv_scale, kv_shift + tok, name="kv_norm_re")
            wg["w_kv", 0] = mm(hkv, dkv, mode="tn", out_dtype=BF16, name="kv_proj_dw")
            dx, d_kvnorm, dsc, dsh = mm_norm_bwd(
                dkv, g_full("w_kv", 0), x_kv, kv_norm_w[None], kv_scale, dx, name="kv_proj_dx")
            d_kvmod = jnp.concatenate([dsh, dsc], axis=1)
            tok = sibling_begin([("w_kv", 0, w_kv)], dx)
        if i > 0:
            tok = reduce_step(depth - 1 - i, dx)
    grad_x = dx[None]
    grads = {}

    small = {
        "ffn_norm_w": jnp.stack([jnp.stack([r[0] for r in row]) for row in sg["ffn_norm_w"]]),
        "mod_b": jnp.stack([r[0] for r in sg["mod"]]),
        "mix_norm_w": jnp.stack([r[0] for r in sg["mix_norm_w"]]),
        "ssm_conv_w": jnp.stack(sg["ssm_conv_w"]),
        "ssm_conv_b": jnp.stack([r[0] for r in sg["ssm_conv_b"]]),
        "ssm_dt_bias": jnp.stack([r[0] for r in sg["ssm_dt_bias"]]),
        "ssm_a_log": jnp.stack([r[0] for r in sg["ssm_a_log"]]),
        "ssm_d": jnp.stack([r[0] for r in sg["ssm_d"]]),
        "ssm_norm_w": jnp.stack([r[0] for r in sg["ssm_norm_w"]]),
        "kv_norm_w": d_kvnorm[0],
        "kv_mod_b": d_kvmod[0],
        "b_kv": d_bkv[0],
        "attn_b_q": jnp.stack([r[0] for r in sg["attn_b_q"]]),
        "attn_sinks": jnp.stack([r[0] for r in sg["attn_sinks"]]),
        "attn_b_o": jnp.stack([r[0] for r in sg["attn_b_o"]]),
        "final_norm_w": d_final[0],
    }
    small_like = [small[n].shape for n in SMALL]
    sv_pack, sv_n = _pack([small[n] for n in SMALL])
    sv_all = ag8(sv_pack + tok, name="ag_small_g")
    sv_all = sv_all + reduce_step(depth, sv_all)
    sv_sum = reduce8(sv_all, name="small_g_sum").reshape(-1)[:sv_n]
    for n, gsum in zip(SMALL, _unpack(sv_sum, small_like)):
        grads[n] = chip_cols(gsum, W[n].shape[-1]) if n in SMALL_SHARDED else gsum

    per_dev = [_unpack(sv_all[b].reshape(-1)[:sv_n], small_like) for b in range(N_DEV)]
    i_modb, i_kvb = SMALL.index("mod_b"), SMALL.index("kv_mod_b")
    dmod_all = jnp.stack([chip_cols(p[i_modb], MW) for p in per_dev], axis=1)
    dkv_all = jnp.stack([chip_cols(p[i_kvb], KW) for p in per_dev], axis=0)[None]
    c_t = jnp.transpose(c_all)
    grads["mod_w"] = outer8(c_t, dmod_all, name="mod_w_grad")
    grads["kv_mod_w"] = outer8(c_t, dkv_all, name="kv_mod_w_grad")[0]

    delta, new_m, new_v = {}, {}, {}
    for n in COLUMN_PARALLEL:
        delta[n], new_m[n], new_v[n] = adamw(W[n], grads[n], MOM[n], VAR[n], name=f"adamw_{n}")
    like = [W[n].shape for n in SMALL]
    packs = [_pack([d[n] for n in SMALL])[0] for d in (W, grads, MOM, VAR)]
    n_small = sum(int(W[n].size) for n in SMALL)
    for dst, res in zip((delta, new_m, new_v), adamw(*packs, name="adamw_small")):
        for n, a in zip(SMALL, _unpack(res.reshape(-1)[:n_small], like)):
            dst[n] = a

    reduce_end(delta["mod_w"])
    for n, s in zip(GATHERED, rs_share([gfull[n] for n in GATHERED], name="rs_share")):
        grads[n] = s.reshape(W[n].shape)
        delta[n], new_m[n], new_v[n] = adamw(W[n], grads[n], MOM[n], VAR[n], name=f"adamw_{n}")

    return (loss, grad_x, *[grads[n] for n in WEIGHTS], *[delta[n] for n in WEIGHTS], *[new_m[n] for n in WEIGHTS],
            *[new_v[n] for n in WEIGHTS])
```

```python
import functools

import jax
import jax.numpy as jnp
from jax import lax
from jax.experimental import pallas as pl
from jax.experimental.pallas import tpu as pltpu

F32 = jnp.float32
BF16 = jnp.bfloat16
HIGHEST = lax.Precision.HIGHEST
MESH = pl.DeviceIdType.MESH

EPS = 1e-5
N_MOD = 9
FFN_HALF = 0.5
SSM_HEADDIM = 64
SSM_GROUPS = 8
SSM_STATE = 128
CONV_WIDTH = 4
CHUNK = 128
KV_HEADS = 4
HEAD_DIM = 64
WINDOW = 128
N_CHIPS = 4
N_DEV = 8

ADAM_LR = 0.001
ADAM_B1 = 0.9
ADAM_B2 = 0.999
ADAM_EPS = 1e-08
ADAM_WD = 0.01
ADAM_STEP = 10

LANE = 128
MM_TILE = 1024


def _pick(n, pref, align, whole_if_small=False):
    best = 0
    t = align
    while t <= min(n, pref):
        if n % t == 0:
            best = t
        t += align
    if best == 0 or (whole_if_small and best < 256 and n <= 2048):
        return n
    return best


def _sigmoid(x):
    return 1.0 / (1.0 + jnp.exp(-x))


def _silu(x):
    return x * _sigmoid(x)


def _dsilu(x):
    s = _sigmoid(x)
    return s * (1.0 + x * (1.0 - s))


def _params(*sem):
    return pltpu.CompilerParams(dimension_semantics=sem)


def mm(a, b, *, mode="nn", reduce_s=False, out_dtype=F32, bias=None, resid=None, name):
    a_s = a.ndim == 3
    b_s = b.ndim == 3
    S = a.shape[0] if a_s else (b.shape[0] if b_s else 1)
    a2 = a.shape[-2:]
    b2 = b.shape[-2:]
    if mode == "nn":
        (M, K), (K2, N) = a2, b2
    elif mode == "nt":
        (M, K), (N, K2) = a2, b2
    else:
        (K, M), (K2, N) = a2, b2
    assert K == K2, (a.shape, b.shape, mode)
    batch = (a_s or b_s) and not reduce_s
    sb = S if batch else 1
    sr = S if ((a_s or b_s) and reduce_s) else 1
    tm = _pick(M, MM_TILE // 2 if resid is not None else MM_TILE, LANE if mode == "tn" else 16, True)
    tn = _pick(N, MM_TILE, LANE, True)
    tk = _pick(K, 2 * MM_TILE if mode == "tn" else MM_TILE, LANE if mode != "tn" else 16, True)
    nk = K // tk
    grid = (sb, M // tm, N // tn, sr, nk)

    def s_of(isb, isr):
        return isb if batch else isr

    def a_map(isb, i, j, isr, k):
        idx = (k, i) if mode == "tn" else (i, k)
        return ((s_of(isb, isr),) + idx) if a_s else idx

    def b_map(isb, i, j, isr, k):
        idx = (j, k) if mode == "nt" else (k, j)
        return ((s_of(isb, isr),) + idx) if b_s else idx

    def o_map(isb, i, j, isr, k):
        return (isb, i, j) if batch else (i, j)

    def s_blk(has_s, blk):
        return ((None,) + blk) if has_s else blk

    a_blk = (tk, tm) if mode == "tn" else (tm, tk)
    b_blk = (tn, tk) if mode == "nt" else (tk, tn)
    in_specs = [pl.BlockSpec(s_blk(a_s, a_blk), a_map), pl.BlockSpec(s_blk(b_s, b_blk), b_map)]
    args = [a, b]
    if bias is not None:
        bias_s = bias.ndim == 3
        in_specs.append(
            pl.BlockSpec(
                ((None, 1, tn) if bias_s else (1, tn)),
                (lambda isb, i, j, isr, k: (isb, 0, j)) if bias_s else (lambda isb, i, j, isr, k: (0, j)),
            )
        )
        args.append(bias)
    o_spec = pl.BlockSpec(s_blk(batch, (tm, tn)), o_map)
    out_shape = jax.ShapeDtypeStruct(((sb, M, N) if batch else (M, N)), out_dtype)
    out_specs = o_spec
    if resid is not None:
        assert not batch
        x_res, gate, scale = resid
        in_specs += [pl.BlockSpec((tm, tn), o_map), pl.BlockSpec((1, tn), lambda isb, i, j, isr, k: (0, j))]
        args += [x_res, gate]
        out_shape = (out_shape, jax.ShapeDtypeStruct((M, N), F32))
        out_specs = (o_spec, pl.BlockSpec((tm, tn), o_map))
    dims = {"nn": (((1,), (0,)), ((), ())), "nt": (((1,), (1,)), ((), ())), "tn": (((0,), (0,)), ((), ()))}[mode]
    n_in = len(args)
    n_out = 2 if resid is not None else 1
    one_step = sr * nk == 1

    def body(*refs):
        a_ref, b_ref = refs[0], refs[1]
        bias_ref = refs[2] if bias is not None else None
        o_ref = refs[n_in]

        def finish(r):
            if bias is not None:
                r = r + bias_ref[...]
            o_ref[...] = r.astype(o_ref.dtype)
            if resid is not None:
                refs[n_in + 1][...] = refs[n_in - 2][...] + (scale * refs[n_in - 1][...]) * r

        def part():
            return lax.dot_general(a_ref[...].astype(BF16), b_ref[...].astype(BF16), dims, preferred_element_type=F32)

        if one_step:
            finish(part())
            return
        acc = refs[n_in + n_out]
        isr = pl.program_id(3)
        k = pl.program_id(4)

        @pl.when((isr == 0) & (k == 0))
        def _():
            acc[...] = jnp.zeros_like(acc)

        acc[...] += part()

        @pl.when((isr == sr - 1) & (k == nk - 1))
        def _():
            finish(acc[...])

    return pl.pallas_call(
        body,
        out_shape=out_shape,
        grid=grid,
        in_specs=in_specs,
        out_specs=out_specs,
        scratch_shapes=[] if one_step else [pltpu.VMEM((tm, tn), F32)],
        compiler_params=_params("parallel", "parallel", "parallel", "arbitrary", "arbitrary"),
        name=name,
    )(*args)


def norm_mod_fwd(x, nw, sc, sh, *, name):
    L, D = x.shape
    tl = _pick(L, 512, 16)

    def body(x_ref, nw_ref, sc_ref, sh_ref, h_ref):
        xv = x_ref[...]
        r = lax.rsqrt(jnp.mean(xv * xv, axis=-1, keepdims=True) + EPS)
        n = (xv * r) * nw_ref[...]
        h_ref[...] = (n * (1.0 + sc_ref[...]) + sh_ref[...]).astype(h_ref.dtype)

    row = pl.BlockSpec((1, D), lambda i: (0, 0))
    return pl.pallas_call(
        body,
        out_shape=jax.ShapeDtypeStruct((L, D), BF16),
        grid=(L // tl,),
        in_specs=[pl.BlockSpec((tl, D), lambda i: (i, 0)), row, row, row],
        out_specs=pl.BlockSpec((tl, D), lambda i: (i, 0)),
        compiler_params=_params("parallel"),
        name=name,
    )(x, nw, sc, sh)


def mm_norm_bwd(a, b, x, nw, sc, dx_in, *, name):
    has_s = a.ndim == 3
    assert has_s == (b.ndim == 3)
    sr = a.shape[0] if has_s else 1
    M, K = a.shape[-2:]
    D = b.shape[-2]
    tm = _pick(M, MM_TILE // 2, 16, True)
    tk = _pick(K, MM_TILE, LANE, True)
    nk = K // tk

    def body(a_ref, b_ref, x_ref, nw_ref, sc_ref, dxi_ref, dx_ref, dnw_ref, dsc_ref, dsh_ref, acc):
        i, s, k = pl.program_id(0), pl.program_id(1), pl.program_id(2)

        @pl.when((i == 0) & (s == 0) & (k == 0))
        def _():
            dnw_ref[...] = jnp.zeros_like(dnw_ref)
            dsc_ref[...] = jnp.zeros_like(dsc_ref)
            dsh_ref[...] = jnp.zeros_like(dsh_ref)

        @pl.when((s == 0) & (k == 0))
        def _():
            acc[...] = jnp.zeros_like(acc)

        acc[...] += lax.dot_general(
            a_ref[...].astype(BF16), b_ref[...].astype(BF16), (((1,), (1,)), ((), ())), preferred_element_type=F32
        )

        @pl.when((s == sr - 1) & (k == nk - 1))
        def _():
            dh_v = acc[...]
            xv = x_ref[...]
            r = lax.rsqrt(jnp.mean(xv * xv, axis=-1, keepdims=True) + EPS)
            xhat = xv * r
            nw_v = nw_ref[...]
            n = xhat * nw_v
            dsh_ref[...] += jnp.sum(dh_v, axis=0, keepdims=True)
            dsc_ref[...] += jnp.sum(dh_v * n, axis=0, keepdims=True)
            dn = dh_v * (1.0 + sc_ref[...])
            dnw_ref[...] += jnp.sum(dn * xhat, axis=0, keepdims=True)
            dxhat = dn * nw_v
            dx_ref[...] = dxi_ref[...] + r * (dxhat - xhat * jnp.mean(dxhat * xhat, axis=-1, keepdims=True))

    row = pl.BlockSpec((1, D), lambda i, s, k: (0, 0))
    tile = pl.BlockSpec((tm, D), lambda i, s, k: (i, 0))
    vec = jax.ShapeDtypeStruct((1, D), F32)
    return pl.pallas_call(
        body,
        out_shape=(jax.ShapeDtypeStruct((M, D), F32), vec, vec, vec),
        grid=(M // tm, sr, nk),
        in_specs=[
            pl.BlockSpec((None, tm, tk) if has_s else (tm, tk), (lambda i, s, k: (s, i, k)) if has_s else (lambda i, s, k: (i, k))),
            pl.BlockSpec((None, D, tk) if has_s else (D, tk), (lambda i, s, k: (s, 0, k)) if has_s else (lambda i, s, k: (0, k))),
            tile, row, row, tile,
        ],
        out_specs=(tile, row, row, row),
        scratch_shapes=[pltpu.VMEM((tm, D), F32)],
        compiler_params=_params("arbitrary", "arbitrary", "arbitrary"),
        name=name,
    )(a, b, x, nw, sc, dx_in)


def gate_bwd(f, dx, gate, scale, *, name):
    L, D = f.shape
    tl = _pick(L, 512, 16)

    def body(f_ref, dx_ref, g_ref, df_ref, dg_ref, dfsum_ref):
        @pl.when(pl.program_id(0) == 0)
        def _():
            dg_ref[...] = jnp.zeros_like(dg_ref)
            dfsum_ref[...] = jnp.zeros_like(dfsum_ref)

        dxv = dx_ref[...]
        df = (scale * g_ref[...]) * dxv
        df_ref[...] = df.astype(df_ref.dtype)
        dfsum_ref[...] += jnp.sum(df, axis=0, keepdims=True)
        dg_ref[...] += scale * jnp.sum(f_ref[...] * dxv, axis=0, keepdims=True)

    tile = pl.BlockSpec((tl, D), lambda i: (i, 0))
    row = pl.BlockSpec((1, D), lambda i: (0, 0))
    vec = jax.ShapeDtypeStruct((1, D), F32)
    return pl.pallas_call(
        body,
        out_shape=(jax.ShapeDtypeStruct((L, D), BF16), vec, vec),
        grid=(L // tl,),
        in_specs=[tile, tile, row],
        out_specs=(tile, row, row),
        compiler_params=_params("arbitrary"),
        name=name,
    )(f, dx, gate)


def colsum(x, *, name):
    L, N = x.shape
    tl = _pick(L, 512, 8)

    def body(x_ref, o_ref):
        @pl.when(pl.program_id(0) == 0)
        def _():
            o_ref[...] = jnp.zeros_like(o_ref)

        o_ref[...] += jnp.sum(x_ref[...], axis=0, keepdims=True)

    return pl.pallas_call(
        body,
        out_shape=jax.ShapeDtypeStruct((1, N), F32),
        grid=(L // tl,),
        in_specs=[pl.BlockSpec((tl, N), lambda i: (i, 0))],
        out_specs=pl.BlockSpec((1, N), lambda i: (0, 0)),
        compiler_params=_params("arbitrary"),
        name=name,
    )(x)


def ffn_up(h, wgu, *, name):
    L, D = h.shape
    T = wgu.shape[-1]
    tm = _pick(L, 512, 16)

    def body(h_ref, w_ref, gu_ref, a_ref):
        hb = h_ref[...].astype(BF16)
        g = jnp.dot(hb, w_ref[0].astype(BF16), preferred_element_type=F32)
        u = jnp.dot(hb, w_ref[1].astype(BF16), preferred_element_type=F32)
        gu_ref[0] = g
        gu_ref[1] = u
        a_ref[...] = (_silu(g) * u).astype(a_ref.dtype)

    gu, a = pl.pallas_call(
        body,
        out_shape=(jax.ShapeDtypeStruct((2, 2, L, T), F32), jax.ShapeDtypeStruct((2, L, T), BF16)),
        grid=(2, L // tm),
        in_specs=[
            pl.BlockSpec((tm, D), lambda j, i: (i, 0)),
            pl.BlockSpec((2, None, D, T), lambda j, i: (0, j, 0, 0)),
        ],
        out_specs=(
            pl.BlockSpec((2, None, tm, T), lambda j, i: (0, j, i, 0)),
            pl.BlockSpec((None, tm, T), lambda j, i: (j, i, 0)),
        ),
        compiler_params=_params("parallel", "parallel"),
        name=name,
    )(h, wgu.reshape(2, 2, D, T))
    return gu.reshape(4, L, T), a


def ffn_down_bwd(df, wdn, gu, *, name):
    L, D = df.shape
    T = wdn.shape[1]
    tm = _pick(L, 512, 16)

    def body(df_ref, w_ref, gu_ref, d_ref):
        da = lax.dot_general(
            df_ref[...].astype(BF16), w_ref[...].astype(BF16), (((1,), (1,)), ((), ())), preferred_element_type=F32
        )
        g = gu_ref[0]
        d_ref[0] = (da * gu_ref[1] * _dsilu(g)).astype(d_ref.dtype)
        d_ref[1] = (da * _silu(g)).astype(d_ref.dtype)

    out = pl.pallas_call(
        body,
        out_shape=jax.ShapeDtypeStruct((2, 2, L, T), BF16),
        grid=(2, L // tm),
        in_specs=[
            pl.BlockSpec((tm, D), lambda j, i: (i, 0)),
            pl.BlockSpec((None, T, D), lambda j, i: (j, 0, 0)),
            pl.BlockSpec((2, None, tm, T), lambda j, i: (0, j, i, 0)),
        ],
        out_specs=pl.BlockSpec((2, None, tm, T), lambda j, i: (0, j, i, 0)),
        compiler_params=_params("parallel", "parallel"),
        name=name,
    )(df, wdn, gu.reshape(2, 2, L, T))
    return out.reshape(4, L, T)


def _shift_down(u, k, rows):
    if k == 0:
        return u
    return jnp.where(rows >= k, pltpu.roll(u, k, 0), 0.0)


def _shift_up(u, k, rows, n):
    if k == 0:
        return u
    return jnp.where(rows < n - k, pltpu.roll(u, n - k, 0), 0.0)


def _conv_pre(u, w_ref, b_ref, rows):
    pre = b_ref[...] + w_ref[CONV_WIDTH - 1 : CONV_WIDTH, :] * u
    for k in range(1, CONV_WIDTH):
        pre = pre + w_ref[CONV_WIDTH - 1 - k : CONV_WIDTH - k, :] * _shift_down(u, k, rows)
    return pre


def conv_fwd(zx, conv_w, conv_b, d_inner, *, name):
    L = zx.shape[0]
    C = conv_w.shape[1]
    tc = 256
    off = d_inner // tc

    def body(u_ref, w_ref, b_ref, o_ref):
        rows = lax.broadcasted_iota(jnp.int32, (L, tc), 0)
        o_ref[...] = _silu(_conv_pre(u_ref[...], w_ref, b_ref, rows))

    return pl.pallas_call(
        body,
        out_shape=jax.ShapeDtypeStruct((L, C), F32),
        grid=(C // tc,),
        in_specs=[
            pl.BlockSpec((L, tc), lambda j: (0, off + j)),
            pl.BlockSpec((CONV_WIDTH, tc), lambda j: (0, j)),
            pl.BlockSpec((1, tc), lambda j: (0, j)),
        ],
        out_specs=pl.BlockSpec((L, tc), lambda j: (0, j)),
        compiler_params=_params("parallel"),
        name=name,
    )(zx, conv_w, conv_b)


def conv_bwd(zx, conv_w, conv_b, dxbc, d_inner, *, name):
    L = zx.shape[0]
    C = conv_w.shape[1]
    tc = 256
    off = d_inner // tc

    def body(u_ref, w_ref, b_ref, d_ref, du_ref, dw_ref, db_ref):
        rows = lax.broadcasted_iota(jnp.int32, (L, tc), 0)
        u = u_ref[...]
        dpre = d_ref[...] * _dsilu(_conv_pre(u, w_ref, b_ref, rows))
        db_ref[...] = jnp.sum(dpre, axis=0, keepdims=True)
        du = w_ref[CONV_WIDTH - 1 : CONV_WIDTH, :] * dpre
        dw_ref[CONV_WIDTH - 1 : CONV_WIDTH, :] = jnp.sum(dpre * u, axis=0, keepdims=True)
        for k in range(1, CONV_WIDTH):
            j = CONV_WIDTH - 1 - k
            dw_ref[j : j + 1, :] = jnp.sum(dpre * _shift_down(u, k, rows), axis=0, keepdims=True)
            du = du + w_ref[j : j + 1, :] * _shift_up(dpre, k, rows, L)
        du_ref[...] = du

    return pl.pallas_call(
        body,
        out_shape=(
            jax.ShapeDtypeStruct((L, C), F32),
            jax.ShapeDtypeStruct((CONV_WIDTH, C), F32),
            jax.ShapeDtypeStruct((1, C), F32),
        ),
        grid=(C // tc,),
        in_specs=[
            pl.BlockSpec((L, tc), lambda j: (0, off + j)),
            pl.BlockSpec((CONV_WIDTH, tc), lambda j: (0, j)),
            pl.BlockSpec((1, tc), lambda j: (0, j)),
            pl.BlockSpec((L, tc), lambda j: (0, j)),
        ],
        out_specs=(
            pl.BlockSpec((L, tc), lambda j: (0, j)),
            pl.BlockSpec((CONV_WIDTH, tc), lambda j: (0, j)),
            pl.BlockSpec((1, tc), lambda j: (0, j)),
        ),
        compiler_params=_params("parallel"),
        name=name,
    )(zx, conv_w, conv_b, dxbc)


def _ssd_head(xs, dt, acs, tot, dsk, cb, bm, cm, prev):
    q = xs.shape[0]
    li = lax.broadcasted_iota(jnp.int32, (q, q), 0)
    si = lax.broadcasted_iota(jnp.int32, (q, q), 1)
    causal = li >= si
    lmat = jnp.exp(jnp.where(causal, acs - acs.T, -jnp.inf))
    xdt = xs * dt
    y = jnp.dot((cb * lmat).astype(BF16), xdt.astype(BF16), preferred_element_type=F32)
    y = y + lax.dot_general(
        (cm * jnp.exp(acs)).astype(BF16), prev.astype(BF16), (((1,), (1,)), ((), ())), preferred_element_type=F32
    )
    y = y + dsk * xs
    st = lax.dot_general(
        xdt.astype(BF16), (bm * jnp.exp(tot - acs)).astype(BF16), (((0,), (0,)), ((), ())), preferred_element_type=F32
    )
    return y, prev * jnp.exp(tot) + st


def _pick_lane(v, h):
    lanes = lax.broadcasted_iota(jnp.int32, v.shape, 1)
    return jnp.sum(jnp.where(lanes == h, v, 0.0), axis=1, keepdims=True)


def _tri_cols(cols, upper):
    q = cols[0].shape[0]
    assert 3 * len(cols) <= LANE
    li = lax.broadcasted_iota(jnp.int32, (q, q), 0)
    si = lax.broadcasted_iota(jnp.int32, (q, q), 1)
    tri = ((li <= si) if upper else (li >= si)).astype(BF16)
    lanes = lax.broadcasted_iota(jnp.int32, (q, LANE), 1)
    rhs = jnp.zeros((q, LANE), F32)
    for r, col in enumerate(cols):
        hi = col.astype(BF16).astype(F32)
        mid = (col - hi).astype(BF16).astype(F32)
        lo = col - hi - mid
        for t, term in enumerate((hi, mid, lo)):
            rhs = jnp.where(lanes == 3 * r + t, term, rhs)
    out = jnp.dot(tri, rhs.astype(BF16), preferred_element_type=F32)
    return [jnp.sum(jnp.where((lanes >= 3 * r) & (lanes < 3 * r + 3), out, 0.0), axis=1, keepdims=True)
            for r in range(len(cols))]


def _softplus(x):
    return jnp.maximum(x, 0.0) + jnp.log(1.0 + jnp.exp(-jnp.abs(x)))


def _ssd_specs(L, d_inner, H, nc, rev):
    R = H // SSM_GROUPS
    P, N, Q = SSM_HEADDIM, SSM_STATE, CHUNK
    ngrp = SSM_GROUPS

    def ci(c):
        return (nc - 1 - c) if rev else c

    b_off = d_inner // N
    c_off = b_off + ngrp
    xs = pl.BlockSpec((Q, R * P), lambda c, g: (ci(c), g))
    bm = pl.BlockSpec((Q, N), lambda c, g: (ci(c), b_off + g))
    cm = pl.BlockSpec((Q, N), lambda c, g: (ci(c), c_off + g))
    dt = pl.BlockSpec((Q, H), lambda c, g: (ci(c), 0))
    hv = pl.BlockSpec((1, H), lambda c, g: (0, 0))
    y = pl.BlockSpec((Q, R * P), lambda c, g: (ci(c), g))
    st = pl.BlockSpec((None, R * P, N), lambda c, g: (ci(c), g, 0))
    return R, xs, bm, cm, dt, hv, y, st


def ssd_fwd(xbc, dt_raw, dt_bias, a_log, d_skip, d_inner, *, name):
    L = xbc.shape[0]
    H = dt_raw.shape[1]
    nc = L // CHUNK
    P, N = SSM_HEADDIM, SSM_STATE
    R, xs_s, bm_s, cm_s, dt_s, hv_s, y_s, st_s = _ssd_specs(L, d_inner, H, nc, False)

    def body(xs_ref, bm_ref, cm_ref, dt_ref, bias_ref, alog_ref, dsk_ref, y_ref, st_ref, state):
        c = pl.program_id(0)
        g = pl.program_id(1)

        @pl.when(c == 0)
        def _():
            for r in range(R):
                state[g * R + r] = jnp.zeros((P, N), F32)

        dtb = _softplus(dt_ref[...] + bias_ref[...])
        a_all = -jnp.exp(alog_ref[...])
        bm, cm = bm_ref[...], cm_ref[...]
        cb = lax.dot_general(cm.astype(BF16), bm.astype(BF16), (((1,), (1,)), ((), ())), preferred_element_type=F32)
        dts = [_pick_lane(dtb, g * R + r) for r in range(R)]
        a_cols = [dts[r] * _pick_lane(a_all, g * R + r) for r in range(R)]
        acs = _tri_cols(a_cols, upper=False)
        prevs = [state[g * R + r] for r in range(R)]
        res = []
        for r in range(R):
            res.append(_ssd_head(
                xs_ref[:, r * P : (r + 1) * P],
                dts[r],
                jnp.broadcast_to(acs[r], (CHUNK, CHUNK)),
                jnp.sum(a_cols[r], axis=0, keepdims=True),
                _pick_lane(dsk_ref[...], g * R + r),
                cb,
                bm,
                cm,
                prevs[r],
            ))
        for r in range(R):
            st_ref[r * P : (r + 1) * P, :] = prevs[r]
            y_ref[:, r * P : (r + 1) * P] = res[r][0]
            state[g * R + r] = res[r][1]

    return pl.pallas_call(
        body,
        out_shape=(jax.ShapeDtypeStruct((L, d_inner), F32), jax.ShapeDtypeStruct((nc, H * P, N), F32)),
        grid=(nc, SSM_GROUPS),
        in_specs=[xs_s, bm_s, cm_s, dt_s, hv_s, hv_s, hv_s],
        out_specs=(y_s, st_s),
        scratch_shapes=[pltpu.VMEM((H, P, N), F32)],
        compiler_params=_params("arbitrary", "arbitrary"),
        name=name,
    )(xbc, xbc, xbc, dt_raw, dt_bias, a_log, d_skip)


def ssd_bwd(xbc, dt_raw, dt_bias, a_log, d_skip, states, dy, d_inner, *, name):
    L, C = xbc.shape
    H = dt_raw.shape[1]
    nc = L // CHUNK
    P, N, Q = SSM_HEADDIM, SSM_STATE, CHUNK
    R, xs_s, bm_s, cm_s, dt_s, hv_s, y_s, st_s = _ssd_specs(L, d_inner, H, nc, True)

    def body(xs_ref, bm_ref, cm_ref, dt_ref, bias_ref, alog_ref, dsk_ref, st_ref, dy_ref,
             dxs_ref, dbm_ref, dcm_ref, ddt_ref, dbias_ref, dalog_ref, ddsk_ref, dstate):
        c = pl.program_id(0)
        g = pl.program_id(1)

        @pl.when(c == 0)
        def _():
            for r in range(R):
                dstate[g * R + r] = jnp.zeros((P, N), F32)

        @pl.when((c == 0) & (g == 0))
        def _():
            dbias_ref[...] = jnp.zeros_like(dbias_ref)
            dalog_ref[...] = jnp.zeros_like(dalog_ref)
            ddsk_ref[...] = jnp.zeros_like(ddsk_ref)

        @pl.when(g == 0)
        def _():
            ddt_ref[...] = jnp.zeros_like(ddt_ref)

        pre = dt_ref[...] + bias_ref[...]
        dtb = _softplus(pre)
        a_all = -jnp.exp(alog_ref[...])
        lanes_q = lax.broadcasted_iota(jnp.int32, (Q, H), 1)
        lanes_1 = lax.broadcasted_iota(jnp.int32, (1, H), 1)
        bm = bm_ref[...]
        cm = cm_ref[...]
        nt = (((1,), (1,)), ((), ()))
        cb = lax.dot_general(cm.astype(BF16), bm.astype(BF16), nt, preferred_element_type=F32)
        dts = [_pick_lane(dtb, g * R + r) for r in range(R)]
        a_negs = [_pick_lane(a_all, g * R + r) for r in range(R)]
        a_cols = [dts[r] * a_negs[r] for r in range(R)]
        acs = _tri_cols(a_cols, upper=False)
        dbm = jnp.zeros((Q, N), F32)
        dcm = jnp.zeros((Q, N), F32)
        dcb = jnp.zeros((Q, Q), F32)
        dd_row = jnp.zeros((1, H), F32)
        dstates = [dstate[g * R + r] for r in range(R)]
        dprevs, ddts, dacs_cols, dtots = [], [], [], []
        for r in range(R):
            h = g * R + r
            args = (
                xs_ref[:, r * P : (r + 1) * P],
                dts[r],
                jnp.broadcast_to(acs[r], (Q, Q)),
                jnp.sum(a_cols[r], axis=0, keepdims=True),
                _pick_lane(dsk_ref[...], h),
                cb,
                bm,
                cm,
                st_ref[r * P : (r + 1) * P, :],
            )
            _, vjp = jax.vjp(_ssd_head, *args)
            dxs, ddt, dacs, dtot, dd, dcb_h, dbm_h, dcm_h, dprev = vjp((dy_ref[:, r * P : (r + 1) * P], dstates[r]))
            dxs_ref[:, r * P : (r + 1) * P] = dxs
            dprevs.append(dprev)
            ddts.append(ddt)
            dacs_cols.append(jnp.sum(dacs, axis=1, keepdims=True))
            dtots.append(dtot)
            dbm = dbm + dbm_h
            dcm = dcm + dcm_h
            dcb = dcb + dcb_h
            dd_row = dd_row + jnp.where(lanes_1 == h, dd, 0.0)
        for r in range(R):
            dstate[g * R + r] = dprevs[r]
        ddt_blk = jnp.zeros((Q, H), F32)
        da_row = jnp.zeros((1, H), F32)
        for r, da_col in enumerate(_tri_cols(dacs_cols, upper=True)):
            h = g * R + r
            da_col = da_col + dtots[r]
            ddt_blk = ddt_blk + jnp.where(lanes_q == h, ddts[r] + da_col * a_negs[r], 0.0)
            da_row = da_row + jnp.where(lanes_1 == h, jnp.sum(da_col * dts[r], axis=0, keepdims=True), 0.0)
        dcb16 = dcb.astype(BF16)
        dbm_ref[...] = dbm + lax.dot_general(dcb16, cm.astype(BF16), (((0,), (0,)), ((), ())), preferred_element_type=F32)
        dcm_ref[...] = dcm + jnp.dot(dcb16, bm.astype(BF16), preferred_element_type=F32)
        ddt_pre = ddt_blk * _sigmoid(pre)
        ddt_ref[...] += ddt_pre
        dbias_ref[...] += jnp.sum(ddt_pre, axis=0, keepdims=True)
        dalog_ref[...] += da_row * a_all
        ddsk_ref[...] += dd_row

    ngrp = SSM_GROUPS
    hrow = jax.ShapeDtypeStruct((1, H), F32)
    dxs, dbm, dcm, ddt, dbias, dalog, ddsk = pl.pallas_call(
        body,
        out_shape=(
            jax.ShapeDtypeStruct((L, d_inner), F32),
            jax.ShapeDtypeStruct((L, ngrp * N), F32),
            jax.ShapeDtypeStruct((L, ngrp * N), F32),
            jax.ShapeDtypeStruct((L, H), F32),
            hrow,
            hrow,
            hrow,
        ),
        grid=(nc, ngrp),
        in_specs=[xs_s, bm_s, cm_s, dt_s, hv_s, hv_s, hv_s, st_s, y_s],
        out_specs=(
            y_s,
            pl.BlockSpec((Q, N), lambda c, g: (nc - 1 - c, g)),
            pl.BlockSpec((Q, N), lambda c, g: (nc - 1 - c, g)),
            dt_s,
            hv_s,
            hv_s,
            hv_s,
        ),
        scratch_shapes=[pltpu.VMEM((H, P, N), F32)],
        compiler_params=_params("arbitrary", "arbitrary"),
        name=name,
    )(xbc, xbc, xbc, dt_raw, dt_bias, a_log, d_skip, states, dy)
    return jnp.concatenate([dxs, dbm, dcm], axis=1), ddt, dbias, dalog, ddsk


def gnorm_fwd(y, zx, nw, *, name):
    L, DI = y.shape
    gw = DI // SSM_GROUPS
    tl = _pick(L, 512, 16)

    def body(y_ref, z_ref, nw_ref, o_ref):
        yz = y_ref[...] * _silu(z_ref[...])
        r = lax.rsqrt(jnp.mean(yz * yz, axis=-1, keepdims=True) + EPS)
        o_ref[...] = ((yz * r) * nw_ref[...]).astype(o_ref.dtype)

    tile = pl.BlockSpec((tl, gw), lambda i, g: (i, g))
    return pl.pallas_call(
        body,
        out_shape=jax.ShapeDtypeStruct((L, DI), BF16),
        grid=(L // tl, SSM_GROUPS),
        in_specs=[tile, tile, pl.BlockSpec((1, gw), lambda i, g: (0, g))],
        out_specs=tile,
        compiler_params=_params("parallel", "parallel"),
        name=name,
    )(y, zx, nw)


def gnorm_bwd(y, zx, nw, dout, *, name):
    L, DI = y.shape
    gw = DI // SSM_GROUPS
    tl = _pick(L, 512, 16)

    def body(y_ref, z_ref, nw_ref, do_ref, dy_ref, dz_ref, dnw_ref):
        @pl.when(pl.program_id(1) == 0)
        def _():
            dnw_ref[...] = jnp.zeros_like(dnw_ref)

        yv = y_ref[...]
        zv = z_ref[...]
        sz = _silu(zv)
        yz = yv * sz
        r = lax.rsqrt(jnp.mean(yz * yz, axis=-1, keepdims=True) + EPS)
        n = yz * r
        dov = do_ref[...]
        dnw_ref[...] += jnp.sum(dov * n, axis=0, keepdims=True)
        dn = dov * nw_ref[...]
        dyz = r * (dn - n * jnp.mean(dn * n, axis=-1, keepdims=True))
        dy_ref[...] = dyz * sz
        dz_ref[...] = dyz * yv * _dsilu(zv)

    tile = pl.BlockSpec((tl, gw), lambda g, i: (i, g))
    row = pl.BlockSpec((1, gw), lambda g, i: (0, g))
    return pl.pallas_call(
        body,
        out_shape=(
            jax.ShapeDtypeStruct((L, DI), F32),
            jax.ShapeDtypeStruct((L, DI), F32),
            jax.ShapeDtypeStruct((1, DI), F32),
        ),
        grid=(SSM_GROUPS, L // tl),
        in_specs=[tile, tile, row, tile],
        out_specs=(tile, tile, row),
        compiler_params=_params("parallel", "arbitrary"),
        name=name,
    )(y, zx, nw, dout)


def _attn_head(q, kp, kc, vp, vc, sink, has_prev):
    rows, w = q.shape[0], kc.shape[0]
    nt = (((1,), (1,)), ((), ()))
    qb = q.astype(BF16)
    sc = lax.dot_general(qb, kc.astype(BF16), nt, preferred_element_type=F32) * HEAD_DIM ** -0.5
    sp = lax.dot_general(qb, kp.astype(BF16), nt, preferred_element_type=F32) * HEAD_DIM ** -0.5
    ii = jnp.bitwise_and(lax.broadcasted_iota(jnp.int32, (rows, w), 0), w - 1)
    jj = lax.broadcasted_iota(jnp.int32, (rows, w), 1)
    lc = jnp.where(jj <= ii, sc, -jnp.inf)
    lp = jnp.where((jj > ii) & has_prev, sp, -jnp.inf)
    m = jnp.maximum(jnp.maximum(jnp.max(lc, axis=1, keepdims=True), jnp.max(lp, axis=1, keepdims=True)), sink)
    m = lax.stop_gradient(m)
    pc = jnp.exp(lc - m)
    pp = jnp.exp(lp - m)
    denom = jnp.sum(pc, axis=1, keepdims=True) + jnp.sum(pp, axis=1, keepdims=True) + jnp.exp(sink - m)
    o = jnp.dot((pc / denom).astype(BF16), vc.astype(BF16), preferred_element_type=F32)
    return o + jnp.dot((pp / denom).astype(BF16), vp.astype(BF16), preferred_element_type=F32)


def attn_fwd(q, kv, sinks, *, name):
    L, DQ = q.shape
    heads = DQ // HEAD_DIM
    rep = heads // KV_HEADS
    nb = L // WINDOW
    kw = KV_HEADS * HEAD_DIM
    W, HD = WINDOW, HEAD_DIM

    def body(q_ref, kp_ref, kc_ref, vp_ref, vc_ref, s_ref, o_ref):
        has_prev = pl.program_id(0) > 0
        for kh in range(KV_HEADS):
            ks = slice(kh * HD, (kh + 1) * HD)
            hs = [kh * rep + rr for rr in range(rep)]
            o = _attn_head(
                jnp.concatenate([q_ref[:, h * HD : (h + 1) * HD] for h in hs], axis=0),
                kp_ref[:, ks], kc_ref[:, ks], vp_ref[:, ks], vc_ref[:, ks],
                jnp.concatenate([jnp.broadcast_to(s_ref[:, h : h + 1], (W, 1)) for h in hs], axis=0), has_prev,
            )
            for rr, h in enumerate(hs):
                o_ref[:, h * HD : (h + 1) * HD] = o[rr * W : (rr + 1) * W].astype(o_ref.dtype)

    return pl.pallas_call(
        body,
        out_shape=jax.ShapeDtypeStruct((L, DQ), BF16),
        grid=(nb,),
        in_specs=[
            pl.BlockSpec((W, DQ), lambda n: (n, 0)),
            pl.BlockSpec((W, kw), lambda n: (jnp.maximum(n - 1, 0), 0)),
            pl.BlockSpec((W, kw), lambda n: (n, 0)),
            pl.BlockSpec((W, kw), lambda n: (jnp.maximum(n - 1, 0), 1)),
            pl.BlockSpec((W, kw), lambda n: (n, 1)),
            pl.BlockSpec((1, heads), lambda n: (0, 0)),
        ],
        out_specs=pl.BlockSpec((W, DQ), lambda n: (n, 0)),
        compiler_params=_params("parallel"),
        name=name,
    )(q, kv, kv, kv, kv, sinks)


def attn_bwd(q, kv, sinks, do, dkv_in, *, name):
    L, DQ = q.shape
    heads = DQ // HEAD_DIM
    rep = heads // KV_HEADS
    nb = L // WINDOW
    kw = KV_HEADS * HEAD_DIM
    W, HD = WINDOW, HEAD_DIM

    def blk(n):
        return jnp.minimum(n, nb - 1)

    def prev(n):
        return jnp.maximum(blk(n) - 1, 0)

    def outb(n):
        return jnp.maximum(n - 1, 0)

    def body(q_ref, kp_ref, kc_ref, vp_ref, vc_ref, s_ref, do_ref, dki_ref, dvi_ref,
             dq_ref, dk_ref, dv_ref, ds_ref, dk_cur, dv_cur):
        n = pl.program_id(0)
        has_prev = n > 0

        @pl.when(n == 0)
        def _():
            ds_ref[...] = jnp.zeros_like(ds_ref)
            dk_cur[...] = jnp.zeros_like(dk_cur)
            dv_cur[...] = jnp.zeros_like(dv_cur)

        @pl.when(n == nb)
        def _():
            dk_ref[...] = dki_ref[...] + dk_cur[...]
            dv_ref[...] = dvi_ref[...] + dv_cur[...]

        @pl.when(n < nb)
        def _():
            lanes = lax.broadcasted_iota(jnp.int32, (1, heads), 1)
            ds_row = jnp.zeros((1, heads), F32)
            for kh in range(KV_HEADS):
                ks = slice(kh * HD, (kh + 1) * HD)
                hs = [kh * rep + rr for rr in range(rep)]
                _, vjp = jax.vjp(
                    functools.partial(_attn_head, has_prev=has_prev),
                    jnp.concatenate([q_ref[:, h * HD : (h + 1) * HD] for h in hs], axis=0),
                    kp_ref[:, ks], kc_ref[:, ks], vp_ref[:, ks], vc_ref[:, ks],
                    jnp.concatenate([jnp.broadcast_to(s_ref[:, h : h + 1], (W, 1)) for h in hs], axis=0),
                )
                dq, dkp, dkc, dvp, dvc, dsk = vjp(
                    jnp.concatenate([do_ref[:, h * HD : (h + 1) * HD] for h in hs], axis=0))
                for rr, h in enumerate(hs):
                    dq_ref[:, h * HD : (h + 1) * HD] = dq[rr * W : (rr + 1) * W]
                    ds_row = ds_row + jnp.where(lanes == h, jnp.sum(dsk[rr * W : (rr + 1) * W], axis=0, keepdims=True), 0.0)
                dk_ref[:, ks] = dki_ref[:, ks] + dk_cur[:, ks] + dkp
                dv_ref[:, ks] = dvi_ref[:, ks] + dv_cur[:, ks] + dvp
                dk_cur[:, ks] = dkc
                dv_cur[:, ks] = dvc
            ds_ref[...] += ds_row

    dq, dk, dv, ds = pl.pallas_call(
        body,
        out_shape=(
            jax.ShapeDtypeStruct((L, DQ), F32),
            jax.ShapeDtypeStruct((L, kw), F32),
            jax.ShapeDtypeStruct((L, kw), F32),
            jax.ShapeDtypeStruct((1, heads), F32),
        ),
        grid=(nb + 1,),
        in_specs=[
            pl.BlockSpec((W, DQ), lambda n: (blk(n), 0)),
            pl.BlockSpec((W, kw), lambda n: (prev(n), 0)),
            pl.BlockSpec((W, kw), lambda n: (blk(n), 0)),
            pl.BlockSpec((W, kw), lambda n: (prev(n), 1)),
            pl.BlockSpec((W, kw), lambda n: (blk(n), 1)),
            pl.BlockSpec((1, heads), lambda n: (0, 0)),
            pl.BlockSpec((W, DQ), lambda n: (blk(n), 0)),
            pl.BlockSpec((W, kw), lambda n: (outb(n), 0)),
            pl.BlockSpec((W, kw), lambda n: (outb(n), 1)),
        ],
        out_specs=(
            pl.BlockSpec((W, DQ), lambda n: (blk(n), 0)),
            pl.BlockSpec((W, kw), lambda n: (outb(n), 0)),
            pl.BlockSpec((W, kw), lambda n: (outb(n), 0)),
            pl.BlockSpec((1, heads), lambda n: (0, 0)),
        ),
        scratch_shapes=[pltpu.VMEM((W, kw), F32), pltpu.VMEM((W, kw), F32)],
        compiler_params=_params("arbitrary"),
        name=name,
    )(q, kv, kv, kv, kv, sinks, do, dkv_in, dkv_in)
    return dq, jnp.concatenate([dk, dv], axis=1), ds


def final_loss(x, fw, target, *, name):
    L, D = x.shape
    tl = _pick(L, 512, 8)

    def body(x_ref, fw_ref, t_ref, loss_ref, dx_ref, dfw_ref):
        @pl.when(pl.program_id(0) == 0)
        def _():
            loss_ref[...] = jnp.zeros_like(loss_ref)
            dfw_ref[...] = jnp.zeros_like(dfw_ref)

        xv = x_ref[...]
        fwv = fw_ref[...]
        r = lax.rsqrt(jnp.mean(xv * xv, axis=-1, keepdims=True) + EPS)
        xhat = xv * r
        err = xhat * fwv - t_ref[...]
        loss_ref[...] += 0.5 * jnp.sum(jnp.mean(err * err, axis=-1, keepdims=True), axis=0, keepdims=True)
        dy = err * (1.0 / D)
        dfw_ref[...] += jnp.sum(dy * xhat, axis=0, keepdims=True)
        dxhat = dy * fwv
        dx_ref[...] = r * (dxhat - xhat * jnp.mean(dxhat * xhat, axis=-1, keepdims=True))

    tile = pl.BlockSpec((tl, D), lambda i: (i, 0))
    row = pl.BlockSpec((1, D), lambda i: (0, 0))
    return pl.pallas_call(
        body,
        out_shape=(
            jax.ShapeDtypeStruct((1, 1), F32),
            jax.ShapeDtypeStruct((L, D), F32),
            jax.ShapeDtypeStruct((1, D), F32),
        ),
        grid=(L // tl,),
        in_specs=[tile, row, tile],
        out_specs=(pl.BlockSpec((1, 1), lambda i: (0, 0)), tile, row),
        compiler_params=_params("arbitrary"),
        name=name,
    )(x, fw, target)


def outer8(ct, d, *, name):
    D, B = ct.shape
    S, _, N = d.shape
    tm = _pick(D, 512, 8)
    tn = _pick(N, 256, LANE)

    def body(c_ref, d_ref, o_ref):
        acc = c_ref[:, 0:1] * d_ref[0:1, :]
        for b in range(1, B):
            acc = acc + c_ref[:, b : b + 1] * d_ref[b : b + 1, :]
        o_ref[...] = acc

    return pl.pallas_call(
        body,
        out_shape=jax.ShapeDtypeStruct((S, D, N), F32),
        grid=(S, D // tm, N // tn),
        in_specs=[
            pl.BlockSpec((tm, B), lambda s, i, j: (i, 0)),
            pl.BlockSpec((None, B, tn), lambda s, i, j: (s, 0, j)),
        ],
        out_specs=pl.BlockSpec((None, tm, tn), lambda s, i, j: (s, i, j)),
        compiler_params=_params("parallel", "parallel", "parallel"),
        name=name,
    )(ct, d)


def reduce8(g, *, name):
    nd, R, N = g.shape

    def body(g_ref, o_ref):
        acc = g_ref[0]
        for b in range(1, nd):
            acc = acc + g_ref[b]
        o_ref[...] = acc

    return pl.pallas_call(
        body,
        out_shape=jax.ShapeDtypeStruct((R, N), F32),
        name=name,
    )(g)


def _as3(a):
    if a.ndim == 1:
        return a.reshape(1, 1, -1)
    if a.ndim == 2:
        return a.reshape((1,) + a.shape)
    return a.reshape((-1,) + a.shape[-2:])


def adamw(w, g, m, v, *, name):
    shape = w.shape
    w3, g3, m3, v3 = _as3(w), _as3(g), _as3(m), _as3(v)
    B, R, C = w3.shape
    tr = _pick(R, max(8, (1 << 19) // max(C, 1) // 8 * 8), 8)

    def body(w_ref, g_ref, m_ref, v_ref, d_ref, nm_ref, nv_ref):
        gv = g_ref[...]
        mn = ADAM_B1 * m_ref[...] + (1.0 - ADAM_B1) * gv
        vn = ADAM_B2 * v_ref[...] + (1.0 - ADAM_B2) * (gv * gv)
        m_hat = mn / (1.0 - ADAM_B1 ** ADAM_STEP)
        v_hat = vn / (1.0 - ADAM_B2 ** ADAM_STEP)
        d_ref[...] = -ADAM_LR * (m_hat / (jnp.sqrt(v_hat) + ADAM_EPS) + ADAM_WD * w_ref[...])
        nm_ref[...] = mn
        nv_ref[...] = vn

    tile = pl.BlockSpec((None, tr, C), lambda b, i: (b, i, 0))
    sds = jax.ShapeDtypeStruct((B, R, C), F32)
    d, nm, nv = pl.pallas_call(
        body,
        out_shape=(sds, sds, sds),
        grid=(B, R // tr),
        in_specs=[tile, tile, tile, tile],
        out_specs=(tile, tile, tile),
        compiler_params=_params("parallel", "parallel"),
        name=name,
    )(w3, g3, m3, v3)
    return d.reshape(shape), nm.reshape(shape), nv.reshape(shape)


def _place():
    return lax.axis_index("x"), lax.axis_index("y"), lax.axis_index("c")


def _flip(v, bit):
    return (1 - v) if bit else v


def ag8(v, *, act=None, after=None, name):
    R, N = v.shape
    extra = [] if after is None else [after]

    def body(*refs):
        v_ref = refs[0]
        out_ref, stage, send_sems, recv_sems = refs[1 + len(extra):]
        x, y, c = _place()
        me = 4 * x + 2 * y + c
        val = v_ref[...]
        if act is not None:
            val = act(val)
        stage[...] = val
        out_ref[me] = val
        sends = []
        for k in range(1, N_DEV):
            px, py, pc = _flip(x, k & 4), _flip(y, k & 2), _flip(c, k & 1)
            cp = pltpu.make_async_remote_copy(
                src_ref=stage, dst_ref=out_ref.at[me], send_sem=send_sems.at[k - 1], recv_sem=recv_sems.at[k - 1],
                device_id=(px, py, pc), device_id_type=MESH,
            )
            cp.start()
            sends.append(cp)
        for k in range(1, N_DEV):
            px, py, pc = _flip(x, k & 4), _flip(y, k & 2), _flip(c, k & 1)
            pltpu.make_async_remote_copy(
                src_ref=stage, dst_ref=out_ref.at[4 * px + 2 * py + pc], send_sem=send_sems.at[k - 1],
                recv_sem=recv_sems.at[k - 1], device_id=(px, py, pc), device_id_type=MESH,
            ).wait_recv()
        for cp in sends:
            cp.wait_send()

    return pl.pallas_call(
        body,
        out_shape=jax.ShapeDtypeStruct((N_DEV, R, N), F32),
        in_specs=[pl.BlockSpec(memory_space=pltpu.VMEM)] + [_ANY] * len(extra),
        out_specs=pl.BlockSpec(memory_space=pltpu.VMEM),
        scratch_shapes=[
            pltpu.VMEM((R, N), F32),
            pltpu.SemaphoreType.DMA((N_DEV - 1,)),
            pltpu.SemaphoreType.DMA((N_DEV - 1,)),
        ],
        name=name,
    )(v, *extra)


def _other_chips(x, y):
    chips = [(1 - x, y), (x, 1 - y), (1 - x, 1 - y)]
    return chips, [2 * px + py for px, py in chips]


_HBM = pl.BlockSpec(memory_space=pltpu.HBM)


_SEM = pl.BlockSpec(memory_space=pltpu.SEMAPHORE)
_ANY = pl.BlockSpec(memory_space=pl.ANY)
_EFFECT = pltpu.SideEffectType.DATAFLOW_SIDE_EFFECTING


def _gather_copies(srcs, lands, send_sems, recv_sems):
    x, y, c = _place()
    k_me = 2 * x + y
    chips, kidx = _other_chips(x, y)
    cps = []
    for w in range(len(srcs)):
        for j, (px, py) in enumerate(chips):
            def copy(dst, w=w, j=j, px=px, py=py):
                return pltpu.make_async_remote_copy(
                    src_ref=srcs[w].at[:, c], dst_ref=dst, send_sem=send_sems.at[3 * w + j],
                    recv_sem=recv_sems.at[3 * w + j], device_id=(px, py, c), device_id_type=MESH,
                )
            cps.append((copy(lands[w].at[:, k_me, c]), copy(lands[w].at[:, kidx[j], c])))
    return cps


def _fill_copies(srcs, lands, send_sems, recv_sems):
    x, y, c = _place()
    k_me = 2 * x + y
    _, kidx = _other_chips(x, y)
    sib = (x, y, 1 - c)
    cps = []
    for w in range(len(srcs)):
        own = pltpu.make_async_remote_copy(
            src_ref=srcs[w], dst_ref=lands[w].at[:, k_me], send_sem=send_sems.at[4 * w + 3], recv_sem=recv_sems.at[4 * w + 3],
            device_id=sib, device_id_type=MESH,
        )
        cps.append((own, own))
        for j in range(3):
            def copy(half, w=w, j=j):
                part = lands[w].at[:, kidx[j], half]
                return pltpu.make_async_remote_copy(
                    src_ref=part, dst_ref=part, send_sem=send_sems.at[4 * w + j], recv_sem=recv_sems.at[4 * w + j],
                    device_id=sib, device_id_type=MESH,
                )
            cps.append((copy(c), copy(1 - c)))
    return cps


def _sibling_copies(srcs, lands, send_sems, recv_sems):
    x, y, c = _place()
    cps = []
    for w in range(len(srcs)):
        cp = pltpu.make_async_remote_copy(
            src_ref=srcs[w].at[:, :, 1 - c], dst_ref=lands[w], send_sem=send_sems.at[w], recv_sem=recv_sems.at[w],
            device_id=(x, y, 1 - c), device_id_type=MESH,
        )
        cps.append((cp, cp))
    return cps


def _reduce_copies(srcs, lands, send_sems, recv_sems):
    x, y, c = _place()
    chips, kidx = _other_chips(x, y)
    cps = []
    for w in range(len(srcs)):
        for j, (px, py) in enumerate(chips):
            cp = pltpu.make_async_remote_copy(
                src_ref=srcs[w].at[:, kidx[j]], dst_ref=lands[w].at[j], send_sem=send_sems.at[3 * w + j],
                recv_sem=recv_sems.at[3 * w + j], device_id=(px, py, c), device_id_type=MESH,
            )
            cps.append((cp, cp))
    return cps


def split_start(copies, srcs, land_shapes, after, *, name, lands=None, per=3):
    n = len(srcs)

    def body(*refs):
        src_refs, land_refs = refs[:n], refs[n : 2 * n]
        send_sems, recv_sems = refs[2 * n + 1], refs[2 * n + 2]
        token = refs[-1]
        for cp, _ in copies(src_refs, land_refs, send_sems, recv_sems):
            cp.start()
        token[...] = jnp.zeros_like(token)

    if lands is None:
        lands = [lax.empty(sh, s.dtype) for sh, s in zip(land_shapes, srcs)]
    land_shapes = [a.shape for a in lands]
    lands = [pltpu.with_memory_space_constraint(a, pltpu.HBM) for a in lands]
    srcs = [pltpu.with_memory_space_constraint(s, pltpu.HBM) for s in srcs]
    out = pl.pallas_call(
        body,
        out_shape=(
            pltpu.SemaphoreType.DMA((per * n,)), pltpu.SemaphoreType.DMA((per * n,)),
            *[pltpu.HBM(s.shape, s.dtype) for s in srcs],
            *[pltpu.HBM(sh, s.dtype) for sh, s in zip(land_shapes, srcs)],
            jax.ShapeDtypeStruct((8, LANE), F32),
        ),
        in_specs=[_HBM] * (2 * n) + [_ANY],
        out_specs=(_SEM, _SEM, *([_HBM] * (2 * n)), pl.BlockSpec(memory_space=pltpu.VMEM)),
        input_output_aliases={i: 2 + i for i in range(2 * n)},
        compiler_params=pltpu.CompilerParams(has_side_effects=_EFFECT),
        name=name,
    )(*srcs, *lands, after)
    return out[0], out[1], list(out[2 : 2 + n]), list(out[2 + n : 2 + 2 * n]), out[-1]


def split_wait(copies, send_sems, recv_sems, srcs, lands, after, *, name):
    n = len(srcs)

    def body(*refs):
        src_refs, land_refs = refs[:n], refs[n : 2 * n]
        send_ref, recv_ref = refs[2 * n], refs[2 * n + 1]
        for sent, arrives in copies(src_refs, land_refs, send_ref, recv_ref):
            sent.wait_send()
            arrives.wait_recv()

    out = pl.pallas_call(
        body,
        out_shape=tuple(pltpu.HBM(a.shape, a.dtype) for a in list(srcs) + list(lands)),
        in_specs=[_HBM] * (2 * n) + [_SEM, _SEM, _ANY],
        out_specs=tuple([_HBM] * (2 * n)),
        input_output_aliases={i: i for i in range(2 * n)},
        compiler_params=pltpu.CompilerParams(has_side_effects=_EFFECT),
        name=name,
    )(*srcs, *lands, send_sems, recv_sems, after)
    return list(out[:n]), list(out[n:])


def rs_share(halves, *, name):
    n = len(halves)

    def body(*refs):
        outs = refs[n : 2 * n]
        send_sems, recv_sems = refs[2 * n :]
        x, y, c = _place()
        cps = []
        for w in range(n):
            cp = pltpu.make_async_remote_copy(
                src_ref=outs[w].at[:, c], dst_ref=outs[w].at[:, c], send_sem=send_sems.at[w], recv_sem=recv_sems.at[w],
                device_id=(x, y, 1 - c), device_id_type=MESH,
            )
            cp.start()
            cps.append(cp)
        for w, cp in enumerate(cps):
            cp.wait_send()
            pltpu.make_async_remote_copy(
                src_ref=outs[w].at[:, c], dst_ref=outs[w].at[:, 1 - c], send_sem=send_sems.at[w], recv_sem=recv_sems.at[w],
                device_id=(x, y, 1 - c), device_id_type=MESH,
            ).wait_recv()

    return pl.pallas_call(
        body,
        out_shape=tuple(jax.ShapeDtypeStruct(h.shape, h.dtype) for h in halves),
        in_specs=[_HBM] * n,
        out_specs=tuple([_HBM] * n),
        scratch_shapes=[pltpu.SemaphoreType.DMA((n,)), pltpu.SemaphoreType.DMA((n,))],
        input_output_aliases={w: w for w in range(n)},
        name=name,
    )(*halves)


def _row_tile(R, C):
    return _pick(R, max(16, (1 << 19) // C // 16 * 16), 16)


def _my_core():
    return lax.axis_index("c")


def _my_chip():
    return 2 * lax.axis_index("x") + lax.axis_index("y")


def rs_add_pair(g, r, *, name):
    M, K, _, R, C = g.shape
    tr = _row_tile(R, C)

    def body(g_ref, r_ref, o_ref):
        o_ref[...] = (g_ref[...].astype(F32) + r_ref[...].astype(F32)).astype(o_ref.dtype)

    blk = pl.BlockSpec((None, K, tr, C), lambda m, i: (m, 0, i, 0))
    return pl.pallas_call(
        body,
        out_shape=jax.ShapeDtypeStruct((M, K, R, C), BF16),
        grid=(M, R // tr),
        in_specs=[pl.BlockSpec((None, K, None, tr, C), lambda m, i: (m, 0, _my_core(), i, 0)), blk],
        out_specs=blk,
        compiler_params=_params("parallel", "parallel"),
        name=name,
    )(g, r)


def rs_add_final(g, r, t, full, m0, *, name):
    M, K, _, R, C = g.shape
    tr = _row_tile(R, C)

    def body(g_ref, r_ref, t_ref, full_ref, o_ref):
        acc = g_ref[...].astype(F32) + r_ref[...].astype(F32)
        for j in range(3):
            acc = acc + t_ref[j].astype(F32)
        o_ref[...] = acc

    return pl.pallas_call(
        body,
        out_shape=jax.ShapeDtypeStruct(full.shape, full.dtype),
        grid=(M, R // tr),
        in_specs=[
            pl.BlockSpec((None, None, None, tr, C), lambda m, i: (m, _my_chip(), _my_core(), i, 0)),
            pl.BlockSpec((None, None, tr, C), lambda m, i: (m, _my_chip(), i, 0)),
            pl.BlockSpec((3, None, tr, C), lambda m, i: (0, m, i, 0)),
            _ANY,
        ],
        out_specs=pl.BlockSpec((None, None, tr, C), lambda m, i: (m0 + m, _my_core(), i, 0)),
        input_output_aliases={3: 0},
        compiler_params=_params("parallel", "parallel"),
        name=name,
    )(g, r, t, full)


WEIGHTS = ["ffn_norm_w", "ffn_w_gu", "ffn_w_down", "mod_w", "mod_b", "mix_norm_w", "ssm_w_in", "ssm_conv_w", "ssm_conv_b",
           "ssm_dt_bias", "ssm_a_log", "ssm_d", "ssm_norm_w", "ssm_w_out", "kv_norm_w", "kv_mod_w", "kv_mod_b", "w_kv", "b_kv",
           "attn_w_q", "attn_b_q", "attn_sinks", "attn_w_o", "attn_b_o", "final_norm_w"]
GATHERED = ["ffn_w_gu", "ffn_w_down", "ssm_w_in", "ssm_w_out", "w_kv", "attn_w_q", "attn_w_o"]
COLUMN_PARALLEL = ["mod_w", "kv_mod_w"]
SMALL_SHARDED = ["ffn_norm_w", "ssm_conv_w", "ssm_conv_b", "ssm_norm_w"]
SMALL = [n for n in WEIGHTS if n not in GATHERED and n not in COLUMN_PARALLEL]


def _row_halves(a):
    a = a.reshape((-1,) + a.shape[-2:])
    return a.reshape(a.shape[0], 2, a.shape[1] // 2, a.shape[2])


def _pack(arrs, rows=8):
    flat = jnp.concatenate([a.reshape(-1) for a in arrs])
    n = flat.shape[0]
    pad = (-n) % (rows * LANE)
    return jnp.pad(flat, (0, pad)).reshape(rows, -1), n


def _unpack(flat, like):
    out, o = [], 0
    for s in like:
        k = 1
        for d in s:
            k *= d
        out.append(flat[o : o + k].reshape(s))
        o += k
    return out


def kernel(x, c, ffn_norm_w, ffn_w_gu, ffn_w_down, mod_w, mod_b, mix_norm_w, ssm_w_in, ssm_conv_w, ssm_conv_b, ssm_dt_bias, ssm_a_log, ssm_d, ssm_norm_w, ssm_w_out, kv_norm_w, kv_mod_w, kv_mod_b, w_kv, b_kv, attn_w_q, attn_b_q, attn_sinks, attn_w_o, attn_b_o, final_norm_w, loss_target, m_ffn_norm_w, m_ffn_w_gu, m_ffn_w_down, m_mod_w, m_mod_b, m_mix_norm_w, m_ssm_w_in, m_ssm_conv_w, m_ssm_conv_b, m_ssm_dt_bias, m_ssm_a_log, m_ssm_d, m_ssm_norm_w, m_ssm_w_out, m_kv_norm_w, m_kv_mod_w, m_kv_mod_b, m_w_kv, m_b_kv, m_attn_w_q, m_attn_b_q, m_attn_sinks, m_attn_w_o, m_attn_b_o, m_final_norm_w, v_ffn_norm_w, v_ffn_w_gu, v_ffn_w_down, v_mod_w, v_mod_b, v_mix_norm_w, v_ssm_w_in, v_ssm_conv_w, v_ssm_conv_b, v_ssm_dt_bias, v_ssm_a_log, v_ssm_d, v_ssm_norm_w, v_ssm_w_out, v_kv_norm_w, v_kv_mod_w, v_kv_mod_b, v_w_kv, v_b_kv, v_attn_w_q, v_attn_b_q, v_attn_sinks, v_attn_w_o, v_attn_b_o, v_final_norm_w):
    env = dict(locals())
    W = {n: env[n] for n in WEIGHTS}
    MOM = {n: env["m_" + n] for n in WEIGHTS}
    VAR = {n: env["v_" + n] for n in WEIGHTS}

    ax, ay, ac = _place()
    kme = 2 * ax + ay
    me = 4 * ax + 2 * ay + ac

    xs = x[0]
    target = loss_target[0]
    L, D = xs.shape
    depth, n_a = ffn_w_gu.shape[0], ssm_w_in.shape[0]
    n_b = depth - n_a
    T = ffn_w_gu.shape[-1]
    DI = ssm_w_out.shape[1] * N_CHIPS
    CI = ssm_w_in.shape[2]
    CC = ssm_conv_w.shape[2] * N_CHIPS
    MW = mod_w.shape[2]
    KW = kv_mod_w.shape[1]
    KVD = w_kv.shape[1]

    def chip_cols(a, width):
        return lax.dynamic_slice_in_dim(a, kme * width, width, axis=a.ndim - 1)

    def ffn_items(i, j):
        return [("ffn_w_gu", 2 * i + j, ffn_w_gu[i, j]), ("ffn_w_down", 2 * i + j, ffn_w_down[i, j])]

    def mix_items(i):
        if i < n_a:
            return [("ssm_w_in", i, ssm_w_in[i]), ("ssm_w_out", i, ssm_w_out[i])]
        return [("attn_w_q", i - n_a, attn_w_q[i - n_a]), ("attn_w_o", i - n_a, attn_w_o[i - n_a])]

    def layer_items(i, order):
        kv_items = [("w_kv", 0, w_kv)] if i == n_a else []
        if order == "fwd":
            return kv_items + ffn_items(i, 0) + mix_items(i) + ffn_items(i, 1)
        return ffn_items(i, 1) + mix_items(i) + ffn_items(i, 0) + kv_items

    def sub_items(sub):
        return {"F": ffn_items, "M": mix_items, "KV": lambda: [("w_kv", 0, w_kv)]}[sub[0]](*sub[1:])

    subs = []
    for i in range(depth):
        subs += ([("KV",)] if i == n_a else []) + [("F", i, 0), ("M", i), ("F", i, 1)]
    assert n_a >= 2 and depth >= 2
    cuts = [0, 1, 2, 4, 6] + [3 * (i + 1) + (1 if i >= n_a else 0) for i in range(2, depth)]
    fwd_plan = [subs[a:b] for a, b in zip(cuts[:-1], cuts[1:])]
    fwd_stages = [[it for sub in stage for it in sub_items(sub)] for stage in fwd_plan]
    bwd_stages = [layer_items(i, "bwd") for i in range(depth - 1, 0, -1)] + [ffn_items(0, 1) + mix_items(0), ffn_items(0, 0)]

    gw, inflight = {}, {}

    def chips_begin(s, after):
        keys = [(n, m0) for n, m0, _ in fwd_stages[s]]
        shards = [_row_halves(a.astype(BF16)) for _, _, a in fwd_stages[s]]
        land_shapes = [(sh.shape[0], N_CHIPS) + sh.shape[1:] for sh in shards]
        ssem, rsem, srcs, lands, token = split_start(_gather_copies, shards, land_shapes, after, name=f"gather_start_{s}")
        inflight[s] = (keys, ssem, rsem, srcs, lands)
        return token[0:1, 0:1]

    def n_head(s):
        return 2 if fwd_plan[s][0][0] == "KV" else 1

    def n_first(s):
        return sum(len(sub_items(sub)) for sub in fwd_plan[s][: n_head(s)])

    def cores_begin(s, after):
        keys, ssem, rsem, srcs, lands = inflight.pop(s)
        srcs, lands = split_wait(_gather_copies, ssem, rsem, srcs, lands, after, name=f"gather_wait_{s}")
        k = n_first(s)
        a_sem, b_sem, a_src, a_land, tok = split_start(
            _fill_copies, srcs[:k], None, after, name=f"fill_start_{s}a", lands=lands[:k], per=4)
        if k < len(keys):
            inflight[s] = (keys[k:],) + split_start(
                _fill_copies, srcs[k:], None, tok, name=f"fill_start_{s}b", lands=lands[k:], per=4)
            tok = inflight[s][-1]
        gw.update(zip(keys[:k], split_wait(_fill_copies, a_sem, b_sem, a_src, a_land, tok, name=f"fill_wait_{s}a")[1]))
        return tok[0:1, 0:1]

    def cores_end(s, after):
        if s in inflight:
            keys, ssem, rsem, srcs, lands, _ = inflight.pop(s)
            gw.update(zip(keys, split_wait(_fill_copies, ssem, rsem, srcs, lands, after, name=f"fill_wait_{s}b")[1]))

    def g_gu(i, j):
        return gw["ffn_w_gu", 2 * i + j].reshape(N_CHIPS, D, T)

    def g_dn(i, j):
        return gw["ffn_w_down", 2 * i + j].reshape(2, T, D)

    def g_full(n, m0):
        a = gw[n, m0]
        return a.reshape(N_CHIPS * 2 * a.shape[-2], a.shape[-1])

    sm_like = [W[n].shape for n in SMALL_SHARDED]
    sm_pack, sm_n = _pack([W[n] for n in SMALL_SHARDED])
    sm_gathered = ag8(sm_pack, name="ag_small_w")
    sm_all = sm_gathered[0::2].reshape(N_CHIPS, -1)[:, :sm_n]
    full = {}
    for n, part in zip(SMALL_SHARDED, zip(*[_unpack(sm_all[k], sm_like) for k in range(N_CHIPS)])):
        full[n] = jnp.concatenate(part, axis=-1)

    c_all = ag8(c, act=_silu, after=sm_gathered, name="ag_c").reshape(N_DEV, D)
    c_all = c_all + chips_begin(0, c_all)
    p_mod = mm(c_all, mod_w, bias=chip_cols(mod_b, MW)[:, None, :], name="mod_mm")
    p_kv = mm(c_all, kv_mod_w, bias=chip_cols(kv_mod_b, KW)[None, :], name="kvmod_mm")
    p_all = jnp.concatenate([jnp.transpose(p_mod, (1, 0, 2)).reshape(N_DEV, depth * MW), p_kv], axis=1)
    p_mine = lax.dynamic_index_in_dim(ag8(p_all, name="ag_mod")[0::2], me, axis=1, keepdims=False)
    mod = jnp.transpose(p_mine[:, : depth * MW].reshape(N_CHIPS, depth, MW), (1, 0, 2)).reshape(depth, N_MOD * D)
    kvmod = p_mine[:, depth * MW :].reshape(1, 2 * D)
    mods = [[mod[i : i + 1, j * D : (j + 1) * D] for j in range(N_MOD)] for i in range(depth)]
    kv_shift, kv_scale = kvmod[:, :D], kvmod[:, D:]

    def ffn_fwd(xin, i, j, sh, sc, gt):
        h = norm_mod_fwd(xin, full["ffn_norm_w"][i, j][None], sc, sh, name=f"ffn_norm_{i}_{j}")
        gu, a = ffn_up(h, g_gu(i, j), name=f"ffn_gu_{i}_{j}")
        f, xo = mm(a, g_dn(i, j), reduce_s=True, resid=(xin, gt, FFN_HALF), name=f"ffn_down_{i}_{j}")
        return xo, (xin, gu, a, f, h)

    def ssm_fwd(xin, i, sh, sc, gt):
        h = norm_mod_fwd(xin, mix_norm_w[i][None], sc, sh, name=f"mix_norm_{i}")
        zx4 = mm(h, gw["ssm_w_in", i].reshape(N_CHIPS, D, CI), name=f"ssm_in_{i}")
        zx = jnp.transpose(zx4, (1, 0, 2)).reshape(L, N_CHIPS * CI)
        xbc = conv_fwd(zx, full["ssm_conv_w"][i], full["ssm_conv_b"][i][None], DI, name=f"ssm_conv_{i}")
        dt_raw = zx[:, DI + CC :]
        y, states = ssd_fwd(xbc, dt_raw, ssm_dt_bias[i][None], ssm_a_log[i][None], ssm_d[i][None], DI, name=f"ssd_{i}")
        yn = gnorm_fwd(y, zx, full["ssm_norm_w"][i][None], name=f"ssm_gnorm_{i}")
        f, xo = mm(yn, g_full("ssm_w_out", i), resid=(xin, gt, 1.0), name=f"ssm_out_{i}")
        return xo, (xin, zx, xbc, dt_raw, y, states, yn, f, h)

    def att_fwd(xin, i, kv, sh, sc, gt):
        l = i - n_a
        h = norm_mod_fwd(xin, mix_norm_w[i][None], sc, sh, name=f"mix_norm_{i}")
        q = mm(h, g_full("attn_w_q", l), bias=attn_b_q[l][None], name=f"att_q_{i}")
        o = attn_fwd(q, kv, attn_sinks[l][None], name=f"att_{i}")
        f, xo = mm(o, g_full("attn_w_o", l), bias=attn_b_o[l][None], resid=(xin, gt, 1.0), name=f"att_o_{i}")
        return xo, (xin, q, o, f, h)

    saved = [[None, None, None] for _ in range(depth)]
    xc = xs
    kv = x_kv = None
    n_stage = len(fwd_stages)

    def run_sub(sub, xc, tok):
        nonlocal kv, x_kv
        if sub[0] == "KV":
            x_kv = xc
            hkv = norm_mod_fwd(xc, kv_norm_w[None], kv_scale, kv_shift + tok, name="kv_norm")
            kv = mm(hkv, g_full("w_kv", 0), bias=b_kv[None], name="kv_proj")
            return xc
        i = sub[1]
        if sub[0] == "F":
            sh, sc, gt = mods[i][6 * sub[2] : 6 * sub[2] + 3]
            xc, saved[i][2 * sub[2]] = ffn_fwd(xc, i, sub[2], sh + tok, sc, gt)
            return xc
        sh, sc, gt = mods[i][3:6]
        xc, saved[i][1] = ssm_fwd(xc, i, sh + tok, sc, gt) if i < n_a else att_fwd(xc, i, kv, sh + tok, sc, gt)
        return xc

    def run_stage(s, xc, tok):
        for k, sub in enumerate(fwd_plan[s]):
            if k == n_head(s):
                cores_end(s, xc)
            xc = run_sub(sub, xc, tok if k < n_head(s) else 0.0)
        return xc

    dep = cores_begin(0, kvmod)
    for s in range(n_stage):
        tok = chips_begin(s + 1, dep) if s + 1 < n_stage else 0.0
        xc = run_stage(s, xc, tok)
        if s + 1 < n_stage:
            dep = cores_begin(s + 1, xc)

    loss_part, dx, d_final = final_loss(xc, final_norm_w[None], target, name="loss_head")
    loss = lax.psum(loss_part[0, 0], ("x", "y", "c"))

    wg = {}
    sg = {
        "ffn_norm_w": [[None, None] for _ in range(depth)], "mix_norm_w": [None] * depth, "mod": [None] * depth,
        "ssm_conv_w": [None] * n_a, "ssm_conv_b": [None] * n_a, "ssm_dt_bias": [None] * n_a, "ssm_a_log": [None] * n_a,
        "ssm_d": [None] * n_a, "ssm_norm_w": [None] * n_a, "attn_b_q": [None] * n_b, "attn_sinks": [None] * n_b,
        "attn_b_o": [None] * n_b,
    }

    def ffn_bwd(dxo, i, j, sv, sh, sc, gt):
        xin, gu, a, f, h = sv
        df, dgt, _ = gate_bwd(f, dxo, gt, FFN_HALF, name=f"ffn_res_bwd_{i}_{j}")
        dgu = ffn_down_bwd(df, g_dn(i, j), gu, name=f"ffn_down_dx_{i}_{j}")
        wg["ffn_w_down", 2 * i + j] = mm(a, df, mode="tn", out_dtype=BF16, name=f"ffn_down_dw_{i}_{j}")
        nw = full["ffn_norm_w"][i, j][None]
        dxi, dnw, dsc, dsh = mm_norm_bwd(dgu, g_gu(i, j), xin, nw, sc, dxo, name=f"ffn_gu_dx_{i}_{j}")
        wg["ffn_w_gu", 2 * i + j] = mm(h, dgu, mode="tn", out_dtype=BF16, name=f"ffn_gu_dw_{i}_{j}")
        sg["ffn_norm_w"][i][j] = dnw
        return dxi, (dsh, dsc, dgt)

    def ssm_bwd(dxo, i, sv, sh, sc, gt):
        xin, zx, xbc, dt_raw, y, states, yn, f, h = sv
        df, dgt, _ = gate_bwd(f, dxo, gt, 1.0, name=f"mix_res_bwd_{i}")
        dyn = mm(df, g_full("ssm_w_out", i), mode="nt", name=f"ssm_out_dx_{i}")
        wg["ssm_w_out", i] = mm(yn, df, mode="tn", out_dtype=BF16, name=f"ssm_out_dw_{i}")
        dy, dz, dnorm = gnorm_bwd(y, zx, full["ssm_norm_w"][i][None], dyn, name=f"ssm_gnorm_bwd_{i}")
        dxbc, ddt, dbias, dalog, ddsk = ssd_bwd(
            xbc, dt_raw, ssm_dt_bias[i][None], ssm_a_log[i][None], ssm_d[i][None], states, dy, DI, name=f"ssd_bwd_{i}"
        )
        du, dcw, dcb = conv_bwd(zx, full["ssm_conv_w"][i], full["ssm_conv_b"][i][None], dxbc, DI, name=f"ssm_conv_bwd_{i}")
        dzx = jnp.concatenate([dz, du, ddt], axis=1).astype(BF16)
        dzx4 = jnp.transpose(dzx.reshape(L, N_CHIPS, CI), (1, 0, 2))
        nw = mix_norm_w[i][None]
        dxi, dnw, dsc, dsh = mm_norm_bwd(
            dzx4, gw["ssm_w_in", i].reshape(N_CHIPS, D, CI), xin, nw, sc, dxo, name=f"ssm_in_dx_{i}")
        wg["ssm_w_in", i] = mm(h, dzx4, mode="tn", out_dtype=BF16, name=f"ssm_in_dw_{i}")
        sg["mix_norm_w"][i] = dnw
        sg["ssm_conv_w"][i], sg["ssm_conv_b"][i], sg["ssm_norm_w"][i] = dcw, dcb, dnorm
        sg["ssm_dt_bias"][i], sg["ssm_a_log"][i], sg["ssm_d"][i] = dbias, dalog, ddsk
        return dxi, (dsh, dsc, dgt)

    def att_bwd(dxo, i, sv, dkv, sh, sc, gt):
        l = i - n_a
        xin, q, o, f, h = sv
        df, dgt, dfsum = gate_bwd(f, dxo, gt, 1.0, name=f"mix_res_bwd_{i}")
        do = mm(df, g_full("attn_w_o", l), mode="nt", name=f"att_o_dx_{i}")
        wg["attn_w_o", l] = mm(o, df, mode="tn", out_dtype=BF16, name=f"att_o_dw_{i}")
        dq, dkv, dsink = attn_bwd(q, kv, attn_sinks[l][None], do, dkv, name=f"att_bwd_{i}")
        nw = mix_norm_w[i][None]
        dxi, dnw, dsc, dsh = mm_norm_bwd(dq, g_full("attn_w_q", l), xin, nw, sc, dxo, name=f"att_q_dx_{i}")
        wg["attn_w_q", l] = mm(h, dq, mode="tn", out_dtype=BF16, name=f"att_q_dw_{i}")
        sg["mix_norm_w"][i] = dnw
        sg["attn_b_q"][l], sg["attn_sinks"][l], sg["attn_b_o"][l] = colsum(dq, name=f"att_bq_{i}"), dsink, dfsum
        return dxi, dkv, (dsh, dsc, dgt)

    gfull = {n: lax.empty(_row_halves(W[n]).shape, F32) for n in GATHERED}

    pending, sib = [], []

    def sibling_begin(items, after):
        parts = []
        for n, m0, a in items:
            m, _, rh, cc = _row_halves(a).shape
            parts.append(wg.pop((n, m0)).reshape(m, N_CHIPS, 2, rh, cc))
        land_shapes = [p.shape[:2] + p.shape[3:] for p in parts]
        tag = f"{items[0][0]}_{items[0][1]}"
        ssem, rsem, srcs, lands, token = split_start(
            _sibling_copies, parts, land_shapes, after, name=f"rs_sibling_start_{tag}", per=1)
        sib.append((tag, [(n, m0) for n, m0, _ in items], ssem, rsem, srcs, lands))
        return token[0:1, 0:1]

    def reduce_step(s, after):
        if pending:
            reduce_end(after)
        items = bwd_stages[s]
        mine, theirs = {}, {}
        while sib:
            tag, keys, ssem, rsem, srcs, lands = sib.pop(0)
            srcs, lands = split_wait(_sibling_copies, ssem, rsem, srcs, lands, after, name=f"rs_sibling_wait_{tag}")
            mine.update(zip(keys, srcs))
            theirs.update(zip(keys, lands))
        parts = [mine[n, m0] for n, m0, _ in items]
        from_sib = [theirs[n, m0] for n, m0, _ in items]
        pair = [rs_add_pair(g, r, name=f"rs_pair_{n}_{m0}") for (n, m0, _), g, r in zip(items, parts, from_sib)]
        land_shapes = [(3, p.shape[0]) + p.shape[2:] for p in pair]
        ssem, rsem, srcs, lands, token = split_start(_reduce_copies, pair, land_shapes, after, name=f"rs_chips_start_{s}")
        pending.append((s, items, parts, from_sib, ssem, rsem, srcs, lands))
        return token[0:1, 0:1]

    def reduce_end(after):
        s, items, parts, from_sib, ssem, rsem, srcs, lands = pending.pop()
        _, from_chips = split_wait(_reduce_copies, ssem, rsem, srcs, lands, after, name=f"rs_chips_wait_{s}")
        for (n, m0, _), g, r, t in zip(items, parts, from_sib, from_chips):
            gfull[n] = rs_add_final(g, r, t, gfull[n], m0, name=f"rs_final_{n}_{m0}")

    dkv = jnp.zeros((L, KVD), F32)
    d_kvnorm = d_kvmod = d_bkv = None
    tok = 0.0
    for i in reversed(range(depth)):
        sh1, sc1, g1, shm, scm, gm, sh2, sc2, g2 = mods[i]
        s1, sm, s2 = saved[i]
        dx, dm2 = ffn_bwd(dx, i, 1, s2, sh2, sc2, g2 + tok)
        if i < n_a:
            dx, dmm = ssm_bwd(dx, i, sm, shm, scm, gm)
        else:
            dx, dkv, dmm = att_bwd(dx, i, sm, dkv, shm, scm, gm)
        tok = sibling_begin(ffn_items(i, 1) + mix_items(i), dx)
        if i == 0:
            tok = tok + reduce_step(depth - 1, dx)
        dx, dm1 = ffn_bwd(dx, i, 0, s1, sh1, sc1, g1 + tok)
        tok = sibling_begin(ffn_items(i, 0), dx)
        sg["mod"][i] = jnp.concatenate(list(dm1) + list(dmm) + list(dm2), axis=1)
        if i == n_a:
            d_bkv = colsum(dkv, name="kv_bias_bwd")
            hkv = norm_mod_fwd(x_kv, kv_norm_w[None], kv_scale, kv_shift + tok, name="kv_norm_re")
            wg["w_kv", 0] = mm(hkv, dkv, mode="tn", out_dtype=BF16, name="kv_proj_dw")
            dx, d_kvnorm, dsc, dsh = mm_norm_bwd(
                dkv, g_full("w_kv", 0), x_kv, kv_norm_w[None], kv_scale, dx, name="kv_proj_dx")
            d_kvmod = jnp.concatenate([dsh, dsc], axis=1)
            tok = sibling_begin([("w_kv", 0, w_kv)], dx)
        if i > 0:
            tok = reduce_step(depth - 1 - i, dx)
    grad_x = dx[None]
    grads = {}

    small = {
        "ffn_norm_w": jnp.stack([jnp.stack([r[0] for r in row]) for row in sg["ffn_norm_w"]]),
        "mod_b": jnp.stack([r[0] for r in sg["mod"]]),
        "mix_norm_w": jnp.stack([r[0] for r in sg["mix_norm_w"]]),
        "ssm_conv_w": jnp.stack(sg["ssm_conv_w"]),
        "ssm_conv_b": jnp.stack([r[0] for r in sg["ssm_conv_b"]]),
        "ssm_dt_bias": jnp.stack([r[0] for r in sg["ssm_dt_bias"]]),
        "ssm_a_log": jnp.stack([r[0] for r in sg["ssm_a_log"]]),
        "ssm_d": jnp.stack([r[0] for r in sg["ssm_d"]]),
        "ssm_norm_w": jnp.stack([r[0] for r in sg["ssm_norm_w"]]),
        "kv_norm_w": d_kvnorm[0],
        "kv_mod_b": d_kvmod[0],
        "b_kv": d_bkv[0],
        "attn_b_q": jnp.stack([r[0] for r in sg["attn_b_q"]]),
        "attn_sinks": jnp.stack([r[0] for r in sg["attn_sinks"]]),
        "attn_b_o": jnp.stack([r[0] for r in sg["attn_b_o"]]),
        "final_norm_w": d_final[0],
    }
    small_like = [small[n].shape for n in SMALL]
    sv_pack, sv_n = _pack([small[n] for n in SMALL])
    sv_all = ag8(sv_pack + tok, name="ag_small_g")
    sv_all = sv_all + reduce_step(depth, sv_all)
    sv_sum = reduce8(sv_all, name="small_g_sum").reshape(-1)[:sv_n]
    for n, gsum in zip(SMALL, _unpack(sv_sum, small_like)):
        grads[n] = chip_cols(gsum, W[n].shape[-1]) if n in SMALL_SHARDED else gsum

    per_dev = [_unpack(sv_all[b].reshape(-1)[:sv_n], small_like) for b in range(N_DEV)]
    i_modb, i_kvb = SMALL.index("mod_b"), SMALL.index("kv_mod_b")
    dmod_all = jnp.stack([chip_cols(p[i_modb], MW) for p in per_dev], axis=1)
    dkv_all = jnp.stack([chip_cols(p[i_kvb], KW) for p in per_dev], axis=0)[None]
    c_t = jnp.transpose(c_all)
    grads["mod_w"] = outer8(c_t, dmod_all, name="mod_w_grad")
    grads["kv_mod_w"] = outer8(c_t, dkv_all, name="kv_mod_w_grad")[0]

    delta, new_m, new_v = {}, {}, {}
    for n in COLUMN_PARALLEL:
        delta[n], new_m[n], new_v[n] = adamw(W[n], grads[n], MOM[n], VAR[n], name=f"adamw_{n}")
    like = [W[n].shape for n in SMALL]
    packs = [_pack([d[n] for n in SMALL])[0] for d in (W, grads, MOM, VAR)]
    n_small = sum(int(W[n].size) for n in SMALL)
    for dst, res in zip((delta, new_m, new_v), adamw(*packs, name="adamw_small")):
        for n, a in zip(SMALL, _unpack(res.reshape(-1)[:n_small], like)):
            dst[n] = a

    reduce_end(delta["mod_w"])
    for n, s in zip(GATHERED, rs_share([gfull[n] for n in GATHERED], name="rs_share")):
        grads[n] = s.reshape(W[n].shape)
        delta[n], new_m[n], new_v[n] = adamw(W[n], grads[n], MOM[n], VAR[n], name=f"adamw_{n}")

    return (loss, grad_x, *[grads[n] for n in WEIGHTS], *[delta[n] for n in WEIGHTS], *[new_m[n] for n in WEIGHTS],
            *[new_v[n] for n in WEIGHTS])
```

```python
import functools

import jax
import jax.numpy as jnp
from jax import lax
from jax.experimental import pallas as pl
from jax.experimental.pallas import tpu as pltpu

F32 = jnp.float32
BF16 = jnp.bfloat16
HIGHEST = lax.Precision.HIGHEST
MESH = pl.DeviceIdType.MESH

EPS = 1e-5
N_MOD = 9
FFN_HALF = 0.5
SSM_HEADDIM = 64
SSM_GROUPS = 8
SSM_STATE = 128
CONV_WIDTH = 4
CHUNK = 128
KV_HEADS = 4
HEAD_DIM = 64
WINDOW = 128
N_CHIPS = 4
N_DEV = 8

ADAM_LR = 0.001
ADAM_B1 = 0.9
ADAM_B2 = 0.999
ADAM_EPS = 1e-08
ADAM_WD = 0.01
ADAM_STEP = 10

LANE = 128
MM_TILE = 1024


def _pick(n, pref, align, whole_if_small=False):
    best = 0
    t = align
    while t <= min(n, pref):
        if n % t == 0:
            best = t
        t += align
    if best == 0 or (whole_if_small and best < 256 and n <= 2048):
        return n
    return best


def _sigmoid(x):
    return 1.0 / (1.0 + jnp.exp(-x))


def _silu(x):
    return x * _sigmoid(x)


def _dsilu(x):
    s = _sigmoid(x)
    return s * (1.0 + x * (1.0 - s))


def _params(*sem):
    return pltpu.CompilerParams(dimension_semantics=sem)


def mm(a, b, *, mode="nn", reduce_s=False, out_dtype=F32, bias=None, resid=None, name):
    a_s = a.ndim == 3
    b_s = b.ndim == 3
    S = a.shape[0] if a_s else (b.shape[0] if b_s else 1)
    a2 = a.shape[-2:]
    b2 = b.shape[-2:]
    if mode == "nn":
        (M, K), (K2, N) = a2, b2
    elif mode == "nt":
        (M, K), (N, K2) = a2, b2
    else:
        (K, M), (K2, N) = a2, b2
    assert K == K2, (a.shape, b.shape, mode)
    batch = (a_s or b_s) and not reduce_s
    sb = S if batch else 1
    sr = S if ((a_s or b_s) and reduce_s) else 1
    tm = _pick(M, MM_TILE // 2 if resid is not None else MM_TILE, LANE if mode == "tn" else 16, True)
    tn = _pick(N, MM_TILE, LANE, True)
    tk = _pick(K, 2 * MM_TILE if mode == "tn" else MM_TILE, LANE if mode != "tn" else 16, True)
    nk = K // tk
    grid = (sb, M // tm, N // tn, sr, nk)

    def s_of(isb, isr):
        return isb if batch else isr

    def a_map(isb, i, j, isr, k):
        idx = (k, i) if mode == "tn" else (i, k)
        return ((s_of(isb, isr),) + idx) if a_s else idx

    def b_map(isb, i, j, isr, k):
        idx = (j, k) if mode == "nt" else (k, j)
        return ((s_of(isb, isr),) + idx) if b_s else idx

    def o_map(isb, i, j, isr, k):
        return (isb, i, j) if batch else (i, j)

    def s_blk(has_s, blk):
        return ((None,) + blk) if has_s else blk

    a_blk = (tk, tm) if mode == "tn" else (tm, tk)
    b_blk = (tn, tk) if mode == "nt" else (tk, tn)
    in_specs = [pl.BlockSpec(s_blk(a_s, a_blk), a_map), pl.BlockSpec(s_blk(b_s, b_blk), b_map)]
    args = [a, b]
    if bias is not None:
        bias_s = bias.ndim == 3
        in_specs.append(
            pl.BlockSpec(
                ((None, 1, tn) if bias_s else (1, tn)),
                (lambda isb, i, j, isr, k: (isb, 0, j)) if bias_s else (lambda isb, i, j, isr, k: (0, j)),
            )
        )
        args.append(bias)
    o_spec = pl.BlockSpec(s_blk(batch, (tm, tn)), o_map)
    out_shape = jax.ShapeDtypeStruct(((sb, M, N) if batch else (M, N)), out_dtype)
    out_specs = o_spec
    if resid is not None:
        assert not batch
        x_res, gate, scale = resid
        in_specs += [pl.BlockSpec((tm, tn), o_map), pl.BlockSpec((1, tn), lambda isb, i, j, isr, k: (0, j))]
        args += [x_res, gate]
        out_shape = (out_shape, jax.ShapeDtypeStruct((M, N), F32))
        out_specs = (o_spec, pl.BlockSpec((tm, tn), o_map))
    dims = {"nn": (((1,), (0,)), ((), ())), "nt": (((1,), (1,)), ((), ())), "tn": (((0,), (0,)), ((), ()))}[mode]
    n_in = len(args)
    n_out = 2 if resid is not None else 1
    one_step = sr * nk == 1

    def body(*refs):
        a_ref, b_ref = refs[0], refs[1]
        bias_ref = refs[2] if bias is not None else None
        o_ref = refs[n_in]

        def finish(r):
            if bias is not None:
                r = r + bias_ref[...]
            o_ref[...] = r.astype(o_ref.dtype)
            if resid is not None:
                refs[n_in + 1][...] = refs[n_in - 2][...] + (scale * refs[n_in - 1][...]) * r

        def part():
            return lax.dot_general(a_ref[...].astype(BF16), b_ref[...].astype(BF16), dims, preferred_element_type=F32)

        if one_step:
            finish(part())
            return
        acc = refs[n_in + n_out]
        isr = pl.program_id(3)
        k = pl.program_id(4)

        @pl.when((isr == 0) & (k == 0))
        def _():
            acc[...] = jnp.zeros_like(acc)

        acc[...] += part()

        @pl.when((isr == sr - 1) & (k == nk - 1))
        def _():
            finish(acc[...])

    return pl.pallas_call(
        body,
        out_shape=out_shape,
        grid=grid,
        in_specs=in_specs,
        out_specs=out_specs,
        scratch_shapes=[] if one_step else [pltpu.VMEM((tm, tn), F32)],
        compiler_params=_params("parallel", "parallel", "parallel", "arbitrary", "arbitrary"),
        name=name,
    )(*args)


def norm_mod_fwd(x, nw, sc, sh, *, name):
    L, D = x.shape
    tl = _pick(L, 512, 16)

    def body(x_ref, nw_ref, sc_ref, sh_ref, h_ref):
        xv = x_ref[...]
        r = lax.rsqrt(jnp.mean(xv * xv, axis=-1, keepdims=True) + EPS)
        n = (xv * r) * nw_ref[...]
        h_ref[...] = (n * (1.0 + sc_ref[...]) + sh_ref[...]).astype(h_ref.dtype)

    row = pl.BlockSpec((1, D), lambda i: (0, 0))
    return pl.pallas_call(
        body,
        out_shape=jax.ShapeDtypeStruct((L, D), BF16),
        grid=(L // tl,),
        in_specs=[pl.BlockSpec((tl, D), lambda i: (i, 0)), row, row, row],
        out_specs=pl.BlockSpec((tl, D), lambda i: (i, 0)),
        compiler_params=_params("parallel"),
        name=name,
    )(x, nw, sc, sh)


def mm_norm_bwd(a, b, x, nw, sc, dx_in, *, name):
    has_s = a.ndim == 3
    assert has_s == (b.ndim == 3)
    sr = a.shape[0] if has_s else 1
    M, K = a.shape[-2:]
    D = b.shape[-2]
    tm = _pick(M, MM_TILE // 2, 16, True)
    tk = _pick(K, MM_TILE, LANE, True)
    nk = K // tk

    def body(a_ref, b_ref, x_ref, nw_ref, sc_ref, dxi_ref, dx_ref, dnw_ref, dsc_ref, dsh_ref, acc):
        i, s, k = pl.program_id(0), pl.program_id(1), pl.program_id(2)

        @pl.when((i == 0) & (s == 0) & (k == 0))
        def _():
            dnw_ref[...] = jnp.zeros_like(dnw_ref)
            dsc_ref[...] = jnp.zeros_like(dsc_ref)
            dsh_ref[...] = jnp.zeros_like(dsh_ref)

        @pl.when((s == 0) & (k == 0))
        def _():
            acc[...] = jnp.zeros_like(acc)

        acc[...] += lax.dot_general(
            a_ref[...].astype(BF16), b_ref[...].astype(BF16), (((1,), (1,)), ((), ())), preferred_element_type=F32
        )

        @pl.when((s == sr - 1) & (k == nk - 1))
        def _():
            dh_v = acc[...]
            xv = x_ref[...]
            r = lax.rsqrt(jnp.mean(xv * xv, axis=-1, keepdims=True) + EPS)
            xhat = xv * r
            nw_v = nw_ref[...]
            n = xhat * nw_v
            dsh_ref[...] += jnp.sum(dh_v, axis=0, keepdims=True)
            dsc_ref[...] += jnp.sum(dh_v * n, axis=0, keepdims=True)
            dn = dh_v * (1.0 + sc_ref[...])
            dnw_ref[...] += jnp.sum(dn * xhat, axis=0, keepdims=True)
            dxhat = dn * nw_v
            dx_ref[...] = dxi_ref[...] + r * (dxhat - xhat * jnp.mean(dxhat * xhat, axis=-1, keepdims=True))

    row = pl.BlockSpec((1, D), lambda i, s, k: (0, 0))
    tile = pl.BlockSpec((tm, D), lambda i, s, k: (i, 0))
    vec = jax.ShapeDtypeStruct((1, D), F32)
    return pl.pallas_call(
        body,
        out_shape=(jax.ShapeDtypeStruct((M, D), F32), vec, vec, vec),
        grid=(M // tm, sr, nk),
        in_specs=[
            pl.BlockSpec((None, tm, tk) if has_s else (tm, tk), (lambda i, s, k: (s, i, k)) if has_s else (lambda i, s, k: (i, k))),
            pl.BlockSpec((None, D, tk) if has_s else (D, tk), (lambda i, s, k: (s, 0, k)) if has_s else (lambda i, s, k: (0, k))),
            tile, row, row, tile,
        ],
        out_specs=(tile, row, row, row),
        scratch_shapes=[pltpu.VMEM((tm, D), F32)],
        compiler_params=_params("arbitrary", "arbitrary", "arbitrary"),
        name=name,
    )(a, b, x, nw, sc, dx_in)


def gate_bwd(f, dx, gate, scale, *, name):
    L, D = f.shape
    tl = _pick(L, 512, 16)

    def body(f_ref, dx_ref, g_ref, df_ref, dg_ref, dfsum_ref):
        @pl.when(pl.program_id(0) == 0)
        def _():
            dg_ref[...] = jnp.zeros_like(dg_ref)
            dfsum_ref[...] = jnp.zeros_like(dfsum_ref)

        dxv = dx_ref[...]
        df = (scale * g_ref[...]) * dxv
        df_ref[...] = df.astype(df_ref.dtype)
        dfsum_ref[...] += jnp.sum(df, axis=0, keepdims=True)
        dg_ref[...] += scale * jnp.sum(f_ref[...] * dxv, axis=0, keepdims=True)

    tile = pl.BlockSpec((tl, D), lambda i: (i, 0))
    row = pl.BlockSpec((1, D), lambda i: (0, 0))
    vec = jax.ShapeDtypeStruct((1, D), F32)
    return pl.pallas_call(
        body,
        out_shape=(jax.ShapeDtypeStruct((L, D), BF16), vec, vec),
        grid=(L // tl,),
        in_specs=[tile, tile, row],
        out_specs=(tile, row, row),
        compiler_params=_params("arbitrary"),
        name=name,
    )(f, dx, gate)


def colsum(x, *, name):
    L, N = x.shape
    tl = _pick(L, 512, 8)

    def body(x_ref, o_ref):
        @pl.when(pl.program_id(0) == 0)
        def _():
            o_ref[...] = jnp.zeros_like(o_ref)

        o_ref[...] += jnp.sum(x_ref[...], axis=0, keepdims=True)

    return pl.pallas_call(
        body,
        out_shape=jax.ShapeDtypeStruct((1, N), F32),
        grid=(L // tl,),
        in_specs=[pl.BlockSpec((tl, N), lambda i: (i, 0))],
        out_specs=pl.BlockSpec((1, N), lambda i: (0, 0)),
        compiler_params=_params("arbitrary"),
        name=name,
    )(x)


def ffn_up(h, wgu, *, name):
    L, D = h.shape
    T = wgu.shape[-1]
    tm = _pick(L, 512, 16)

    def body(h_ref, w_ref, gu_ref, a_ref):
        hb = h_ref[...].astype(BF16)
        g = jnp.dot(hb, w_ref[0].astype(BF16), preferred_element_type=F32)
        u = jnp.dot(hb, w_ref[1].astype(BF16), preferred_element_type=F32)
        gu_ref[0] = g
        gu_ref[1] = u
        a_ref[...] = (_silu(g) * u).astype(a_ref.dtype)

    gu, a = pl.pallas_call(
        body,
        out_shape=(jax.ShapeDtypeStruct((2, 2, L, T), F32), jax.ShapeDtypeStruct((2, L, T), BF16)),
        grid=(2, L // tm),
        in_specs=[
            pl.BlockSpec((tm, D), lambda j, i: (i, 0)),
            pl.BlockSpec((2, None, D, T), lambda j, i: (0, j, 0, 0)),
        ],
        out_specs=(
            pl.BlockSpec((2, None, tm, T), lambda j, i: (0, j, i, 0)),
            pl.BlockSpec((None, tm, T), lambda j, i: (j, i, 0)),
        ),
        compiler_params=_params("parallel", "parallel"),
        name=name,
    )(h, wgu.reshape(2, 2, D, T))
    return gu.reshape(4, L, T), a


def ffn_down_bwd(df, wdn, gu, *, name):
    L, D = df.shape
    T = wdn.shape[1]
    tm = _pick(L, 512, 16)

    def body(df_ref, w_ref, gu_ref, d_ref):
        da = lax.dot_general(
            df_ref[...].astype(BF16), w_ref[...].astype(BF16), (((1,), (1,)), ((), ())), preferred_element_type=F32
        )
        g = gu_ref[0]
        d_ref[0] = (da * gu_ref[1] * _dsilu(g)).astype(d_ref.dtype)
        d_ref[1] = (da * _silu(g)).astype(d_ref.dtype)

    out = pl.pallas_call(
        body,
        out_shape=jax.ShapeDtypeStruct((2, 2, L, T), BF16),
        grid=(2, L // tm),
        in_specs=[
            pl.BlockSpec((tm, D), lambda j, i: (i, 0)),
            pl.BlockSpec((None, T, D), lambda j, i: (j, 0, 0)),
            pl.BlockSpec((2, None, tm, T), lambda j, i: (0, j, i, 0)),
        ],
        out_specs=pl.BlockSpec((2, None, tm, T), lambda j, i: (0, j, i, 0)),
        compiler_params=_params("parallel", "parallel"),
        name=name,
    )(df, wdn, gu.reshape(2, 2, L, T))
    return out.reshape(4, L, T)


def _shift_down(u, k, rows):
    if k == 0:
        return u
    return jnp.where(rows >= k, pltpu.roll(u, k, 0), 0.0)


def _shift_up(u, k, rows, n):
    if k == 0:
        return u
    return jnp.where(rows < n - k, pltpu.roll(u, n - k, 0), 0.0)


def _conv_pre(u, w_ref, b_ref, rows):
    pre = b_ref[...] + w_ref[CONV_WIDTH - 1 : CONV_WIDTH, :] * u
    for k in range(1, CONV_WIDTH):
        pre = pre + w_ref[CONV_WIDTH - 1 - k : CONV_WIDTH - k, :] * _shift_down(u, k, rows)
    return pre


def conv_fwd(zx, conv_w, conv_b, d_inner, *, name):
    L = zx.shape[0]
    C = conv_w.shape[1]
    tc = 256
    off = d_inner // tc

    def body(u_ref, w_ref, b_ref, o_ref):
        rows = lax.broadcasted_iota(jnp.int32, (L, tc), 0)
        o_ref[...] = _silu(_conv_pre(u_ref[...], w_ref, b_ref, rows))

    return pl.pallas_call(
        body,
        out_shape=jax.ShapeDtypeStruct((L, C), F32),
        grid=(C // tc,),
        in_specs=[
            pl.BlockSpec((L, tc), lambda j: (0, off + j)),
            pl.BlockSpec((CONV_WIDTH, tc), lambda j: (0, j)),
            pl.BlockSpec((1, tc), lambda j: (0, j)),
        ],
        out_specs=pl.BlockSpec((L, tc), lambda j: (0, j)),
        compiler_params=_params("parallel"),
        name=name,
    )(zx, conv_w, conv_b)


def conv_bwd(zx, conv_w, conv_b, dxbc, d_inner, *, name):
    L = zx.shape[0]
    C = conv_w.shape[1]
    tc = 256
    off = d_inner // tc

    def body(u_ref, w_ref, b_ref, d_ref, du_ref, dw_ref, db_ref):
        rows = lax.broadcasted_iota(jnp.int32, (L, tc), 0)
        u = u_ref[...]
        dpre = d_ref[...] * _dsilu(_conv_pre(u, w_ref, b_ref, rows))
        db_ref[...] = jnp.sum(dpre, axis=0, keepdims=True)
        du = w_ref[CONV_WIDTH - 1 : CONV_WIDTH, :] * dpre
        dw_ref[CONV_WIDTH - 1 : CONV_WIDTH, :] = jnp.sum(dpre * u, axis=0, keepdims=True)
        for k in range(1, CONV_WIDTH):
            j = CONV_WIDTH - 1 - k
            dw_ref[j : j + 1, :] = jnp.sum(dpre * _shift_down(u, k, rows), axis=0, keepdims=True)
            du = du + w_ref[j : j + 1, :] * _shift_up(dpre, k, rows, L)
        du_ref[...] = du

    return pl.pallas_call(
        body,
        out_shape=(
            jax.ShapeDtypeStruct((L, C), F32),
            jax.ShapeDtypeStruct((CONV_WIDTH, C), F32),
            jax.ShapeDtypeStruct((1, C), F32),
        ),
        grid=(C // tc,),
        in_specs=[
            pl.BlockSpec((L, tc), lambda j: (0, off + j)),
            pl.BlockSpec((CONV_WIDTH, tc), lambda j: (0, j)),
            pl.BlockSpec((1, tc), lambda j: (0, j)),
            pl.BlockSpec((L, tc), lambda j: (0, j)),
        ],
        out_specs=(
            pl.BlockSpec((L, tc), lambda j: (0, j)),
            pl.BlockSpec((CONV_WIDTH, tc), lambda j: (0, j)),
            pl.BlockSpec((1, tc), lambda j: (0, j)),
        ),
        compiler_params=_params("parallel"),
        name=name,
    )(zx, conv_w, conv_b, dxbc)


def _ssd_head(xs, dt, acs, tot, dsk, cb, bm, cm, prev):
    q = xs.shape[0]
    li = lax.broadcasted_iota(jnp.int32, (q, q), 0)
    si = lax.broadcasted_iota(jnp.int32, (q, q), 1)
    causal = li >= si
    lmat = jnp.exp(jnp.where(causal, acs - acs.T, -jnp.inf))
    xdt = xs * dt
    y = jnp.dot((cb * lmat).astype(BF16), xdt.astype(BF16), preferred_element_type=F32)
    y = y + lax.dot_general(
        (cm * jnp.exp(acs)).astype(BF16), prev.astype(BF16), (((1,), (1,)), ((), ())), preferred_element_type=F32
    )
    y = y + dsk * xs
    st = lax.dot_general(
        xdt.astype(BF16), (bm * jnp.exp(tot - acs)).astype(BF16), (((0,), (0,)), ((), ())), preferred_element_type=F32
    )
    return y, prev * jnp.exp(tot) + st


def _pick_lane(v, h):
    lanes = lax.broadcasted_iota(jnp.int32, v.shape, 1)
    return jnp.sum(jnp.where(lanes == h, v, 0.0), axis=1, keepdims=True)


def _tri_cols(cols, upper):
    q = cols[0].shape[0]
    assert 3 * len(cols) <= LANE
    li = lax.broadcasted_iota(jnp.int32, (q, q), 0)
    si = lax.broadcasted_iota(jnp.int32, (q, q), 1)
    tri = ((li <= si) if upper else (li >= si)).astype(BF16)
    lanes = lax.broadcasted_iota(jnp.int32, (q, LANE), 1)
    rhs = jnp.zeros((q, LANE), F32)
    for r, col in enumerate(cols):
        hi = col.astype(BF16).astype(F32)
        mid = (col - hi).astype(BF16).astype(F32)
        lo = col - hi - mid
        for t, term in enumerate((hi, mid, lo)):
            rhs = jnp.where(lanes == 3 * r + t, term, rhs)
    out = jnp.dot(tri, rhs.astype(BF16), preferred_element_type=F32)
    return [jnp.sum(jnp.where((lanes >= 3 * r) & (lanes < 3 * r + 3), out, 0.0), axis=1, keepdims=True)
            for r in range(len(cols))]


def _softplus(x):
    return jnp.maximum(x, 0.0) + jnp.log(1.0 + jnp.exp(-jnp.abs(x)))


def _ssd_specs(L, d_inner, H, nc, rev):
    R = H // SSM_GROUPS
    P, N, Q = SSM_HEADDIM, SSM_STATE, CHUNK
    ngrp = SSM_GROUPS

    def ci(c):
        return (nc - 1 - c) if rev else c

    b_off = d_inner // N
    c_off = b_off + ngrp
    xs = pl.BlockSpec((Q, R * P), lambda c, g: (ci(c), g))
    bm = pl.BlockSpec((Q, N), lambda c, g: (ci(c), b_off + g))
    cm = pl.BlockSpec((Q, N), lambda c, g: (ci(c), c_off + g))
    dt = pl.BlockSpec((Q, H), lambda c, g: (ci(c), 0))
    hv = pl.BlockSpec((1, H), lambda c, g: (0, 0))
    y = pl.BlockSpec((Q, R * P), lambda c, g: (ci(c), g))
    st = pl.BlockSpec((None, R * P, N), lambda c, g: (ci(c), g, 0))
    return R, xs, bm, cm, dt, hv, y, st


def ssd_fwd(xbc, dt_raw, dt_bias, a_log, d_skip, d_inner, *, name):
    L = xbc.shape[0]
    H = dt_raw.shape[1]
    nc = L // CHUNK
    P, N = SSM_HEADDIM, SSM_STATE
    R, xs_s, bm_s, cm_s, dt_s, hv_s, y_s, st_s = _ssd_specs(L, d_inner, H, nc, False)

    def body(xs_ref, bm_ref, cm_ref, dt_ref, bias_ref, alog_ref, dsk_ref, y_ref, st_ref, state):
        c = pl.program_id(0)
        g = pl.program_id(1)

        @pl.when(c == 0)
        def _():
            for r in range(R):
                state[g * R + r] = jnp.zeros((P, N), F32)

        dtb = _softplus(dt_ref[...] + bias_ref[...])
        a_all = -jnp.exp(alog_ref[...])
        bm, cm = bm_ref[...], cm_ref[...]
        cb = lax.dot_general(cm.astype(BF16), bm.astype(BF16), (((1,), (1,)), ((), ())), preferred_element_type=F32)
        dts = [_pick_lane(dtb, g * R + r) for r in range(R)]
        a_cols = [dts[r] * _pick_lane(a_all, g * R + r) for r in range(R)]
        acs = _tri_cols(a_cols, upper=False)
        prevs = [state[g * R + r] for r in range(R)]
        res = []
        for r in range(R):
            res.append(_ssd_head(
                xs_ref[:, r * P : (r + 1) * P],
                dts[r],
                jnp.broadcast_to(acs[r], (CHUNK, CHUNK)),
                jnp.sum(a_cols[r], axis=0, keepdims=True),
                _pick_lane(dsk_ref[...], g * R + r),
                cb,
                bm,
                cm,
                prevs[r],
            ))
        for r in range(R):
            st_ref[r * P : (r + 1) * P, :] = prevs[r]
            y_ref[:, r * P : (r + 1) * P] = res[r][0]
            state[g * R + r] = res[r][1]

    return pl.pallas_call(
        body,
        out_shape=(jax.ShapeDtypeStruct((L, d_inner), F32), jax.ShapeDtypeStruct((nc, H * P, N), F32)),
        grid=(nc, SSM_GROUPS),
        in_specs=[xs_s, bm_s, cm_s, dt_s, hv_s, hv_s, hv_s],
        out_specs=(y_s, st_s),
        scratch_shapes=[pltpu.VMEM((H, P, N), F32)],
        compiler_params=_params("arbitrary", "arbitrary"),
        name=name,
    )(xbc, xbc, xbc, dt_raw, dt_bias, a_log, d_skip)


def ssd_bwd(xbc, dt_raw, dt_bias, a_log, d_skip, states, dy, d_inner, *, name):
    L, C = xbc.shape
    H = dt_raw.shape[1]
    nc = L // CHUNK
    P, N, Q = SSM_HEADDIM, SSM_STATE, CHUNK
    R, xs_s, bm_s, cm_s, dt_s, hv_s, y_s, st_s = _ssd_specs(L, d_inner, H, nc, True)

    def body(xs_ref, bm_ref, cm_ref, dt_ref, bias_ref, alog_ref, dsk_ref, st_ref, dy_ref,
             dxs_ref, dbm_ref, dcm_ref, ddt_ref, dbias_ref, dalog_ref, ddsk_ref, dstate):
        c = pl.program_id(0)
        g = pl.program_id(1)

        @pl.when(c == 0)
        def _():
            for r in range(R):
                dstate[g * R + r] = jnp.zeros((P, N), F32)

        @pl.when((c == 0) & (g == 0))
        def _():
            dbias_ref[...] = jnp.zeros_like(dbias_ref)
            dalog_ref[...] = jnp.zeros_like(dalog_ref)
            ddsk_ref[...] = jnp.zeros_like(ddsk_ref)

        @pl.when(g == 0)
        def _():
            ddt_ref[...] = jnp.zeros_like(ddt_ref)

        pre = dt_ref[...] + bias_ref[...]
        dtb = _softplus(pre)
        a_all = -jnp.exp(alog_ref[...])
        lanes_q = lax.broadcasted_iota(jnp.int32, (Q, H), 1)
        lanes_1 = lax.broadcasted_iota(jnp.int32, (1, H), 1)
        bm = bm_ref[...]
        cm = cm_ref[...]
        nt = (((1,), (1,)), ((), ()))
        cb = lax.dot_general(cm.astype(BF16), bm.astype(BF16), nt, preferred_element_type=F32)
        dts = [_pick_lane(dtb, g * R + r) for r in range(R)]
        a_negs = [_pick_lane(a_all, g * R + r) for r in range(R)]
        a_cols = [dts[r] * a_negs[r] for r in range(R)]
        acs = _tri_cols(a_cols, upper=False)
        dbm = jnp.zeros((Q, N), F32)
        dcm = jnp.zeros((Q, N), F32)
        dcb = jnp.zeros((Q, Q), F32)
        dd_row = jnp.zeros((1, H), F32)
        dstates = [dstate[g * R + r] for r in range(R)]
        dprevs, ddts, dacs_cols, dtots = [], [], [], []
        for r in range(R):
            h = g * R + r
            args = (
                xs_ref[:, r * P : (r + 1) * P],
                dts[r],
                jnp.broadcast_to(acs[r], (Q, Q)),
                jnp.sum(a_cols[r], axis=0, keepdims=True),
                _pick_lane(dsk_ref[...], h),
                cb,
                bm,
                cm,
                st_ref[r * P : (r + 1) * P, :],
            )
            _, vjp = jax.vjp(_ssd_head, *args)
            dxs, ddt, dacs, dtot, dd, dcb_h, dbm_h, dcm_h, dprev = vjp((dy_ref[:, r * P : (r + 1) * P], dstates[r]))
            dxs_ref[:, r * P : (r + 1) * P] = dxs
            dprevs.append(dprev)
            ddts.append(ddt)
            dacs_cols.append(jnp.sum(dacs, axis=1, keepdims=True))
            dtots.append(dtot)
            dbm = dbm + dbm_h
            dcm = dcm + dcm_h
            dcb = dcb + dcb_h
            dd_row = dd_row + jnp.where(lanes_1 == h, dd, 0.0)
        for r in range(R):
            dstate[g * R + r] = dprevs[r]
        ddt_blk = jnp.zeros((Q, H), F32)
        da_row = jnp.zeros((1, H), F32)
        for r, da_col in enumerate(_tri_cols(dacs_cols, upper=True)):
            h = g * R + r
            da_col = da_col + dtots[r]
            ddt_blk = ddt_blk + jnp.where(lanes_q == h, ddts[r] + da_col * a_negs[r], 0.0)
            da_row = da_row + jnp.where(lanes_1 == h, jnp.sum(da_col * dts[r], axis=0, keepdims=True), 0.0)
        dcb16 = dcb.astype(BF16)
        dbm_ref[...] = dbm + lax.dot_general(dcb16, cm.astype(BF16), (((0,), (0,)), ((), ())), preferred_element_type=F32)
        dcm_ref[...] = dcm + jnp.dot(dcb16, bm.astype(BF16), preferred_element_type=F32)
        ddt_pre = ddt_blk * _sigmoid(pre)
        ddt_ref[...] += ddt_pre
        dbias_ref[...] += jnp.sum(ddt_pre, axis=0, keepdims=True)
        dalog_ref[...] += da_row * a_all
        ddsk_ref[...] += dd_row

    ngrp = SSM_GROUPS
    hrow = jax.ShapeDtypeStruct((1, H), F32)
    dxs, dbm, dcm, ddt, dbias, dalog, ddsk = pl.pallas_call(
        body,
        out_shape=(
            jax.ShapeDtypeStruct((L, d_inner), F32),
            jax.ShapeDtypeStruct((L, ngrp * N), F32),
            jax.ShapeDtypeStruct((L, ngrp * N), F32),
            jax.ShapeDtypeStruct((L, H), F32),
            hrow,
            hrow,
            hrow,
        ),
        grid=(nc, ngrp),
        in_specs=[xs_s, bm_s, cm_s, dt_s, hv_s, hv_s, hv_s, st_s, y_s],
        out_specs=(
            y_s,
            pl.BlockSpec((Q, N), lambda c, g: (nc - 1 - c, g)),
            pl.BlockSpec((Q, N), lambda c, g: (nc - 1 - c, g)),
            dt_s,
            hv_s,
            hv_s,
            hv_s,
        ),
        scratch_shapes=[pltpu.VMEM((H, P, N), F32)],
        compiler_params=_params("arbitrary", "arbitrary"),
        name=name,
    )(xbc, xbc, xbc, dt_raw, dt_bias, a_log, d_skip, states, dy)
    return jnp.concatenate([dxs, dbm, dcm], axis=1), ddt, dbias, dalog, ddsk


def gnorm_fwd(y, zx, nw, *, name):
    L, DI = y.shape
    gw = DI // SSM_GROUPS
    tl = _pick(L, 512, 16)

    def body(y_ref, z_ref, nw_ref, o_ref):
        yz = y_ref[...] * _silu(z_ref[...])
        r = lax.rsqrt(jnp.mean(yz * yz, axis=-1, keepdims=True) + EPS)
        o_ref[...] = ((yz * r) * nw_ref[...]).astype(o_ref.dtype)

    tile = pl.BlockSpec((tl, gw), lambda i, g: (i, g))
    return pl.pallas_call(
        body,
        out_shape=jax.ShapeDtypeStruct((L, DI), BF16),
        grid=(L // tl, SSM_GROUPS),
        in_specs=[tile, tile, pl.BlockSpec((1, gw), lambda i, g: (0, g))],
        out_specs=tile,
        compiler_params=_params("parallel", "parallel"),
        name=name,
    )(y, zx, nw)


def gnorm_bwd(y, zx, nw, dout, *, name):
    L, DI = y.shape
    gw = DI // SSM_GROUPS
    tl = _pick(L, 512, 16)

    def body(y_ref, z_ref, nw_ref, do_ref, dy_ref, dz_ref, dnw_ref):
        @pl.when(pl.program_id(1) == 0)
        def _():
            dnw_ref[...] = jnp.zeros_like(dnw_ref)

        yv = y_ref[...]
        zv = z_ref[...]
        sz = _silu(zv)
        yz = yv * sz
        r = lax.rsqrt(jnp.mean(yz * yz, axis=-1, keepdims=True) + EPS)
        n = yz * r
        dov = do_ref[...]
        dnw_ref[...] += jnp.sum(dov * n, axis=0, keepdims=True)
        dn = dov * nw_ref[...]
        dyz = r * (dn - n * jnp.mean(dn * n, axis=-1, keepdims=True))
        dy_ref[...] = dyz * sz
        dz_ref[...] = dyz * yv * _dsilu(zv)

    tile = pl.BlockSpec((tl, gw), lambda g, i: (i, g))
    row = pl.BlockSpec((1, gw), lambda g, i: (0, g))
    return pl.pallas_call(
        body,
        out_shape=(
            jax.ShapeDtypeStruct((L, DI), F32),
            jax.ShapeDtypeStruct((L, DI), F32),
            jax.ShapeDtypeStruct((1, DI), F32),
        ),
        grid=(SSM_GROUPS, L // tl),
        in_specs=[tile, tile, row, tile],
        out_specs=(tile, tile, row),
        compiler_params=_params("parallel", "arbitrary"),
        name=name,
    )(y, zx, nw, dout)


def _attn_head(q, kp, kc, vp, vc, sink, has_prev):
    rows, w = q.shape[0], kc.shape[0]
    nt = (((1,), (1,)), ((), ()))
    qb = q.astype(BF16)
    sc = lax.dot_general(qb, kc.astype(BF16), nt, preferred_element_type=F32) * HEAD_DIM ** -0.5
    sp = lax.dot_general(qb, kp.astype(BF16), nt, preferred_element_type=F32) * HEAD_DIM ** -0.5
    ii = jnp.bitwise_and(lax.broadcasted_iota(jnp.int32, (rows, w), 0), w - 1)
    jj = lax.broadcasted_iota(jnp.int32, (rows, w), 1)
    lc = jnp.where(jj <= ii, sc, -jnp.inf)
    lp = jnp.where((jj > ii) & has_prev, sp, -jnp.inf)
    m = jnp.maximum(jnp.maximum(jnp.max(lc, axis=1, keepdims=True), jnp.max(lp, axis=1, keepdims=True)), sink)
    m = lax.stop_gradient(m)
    pc = jnp.exp(lc - m)
    pp = jnp.exp(lp - m)
    denom = jnp.sum(pc, axis=1, keepdims=True) + jnp.sum(pp, axis=1, keepdims=True) + jnp.exp(sink - m)
    o = jnp.dot((pc / denom).astype(BF16), vc.astype(BF16), preferred_element_type=F32)
    return o + jnp.dot((pp / denom).astype(BF16), vp.astype(BF16), preferred_element_type=F32)


def attn_fwd(q, kv, sinks, *, name):
    L, DQ = q.shape
    heads = DQ // HEAD_DIM
    rep = heads // KV_HEADS
    nb = L // WINDOW
    kw = KV_HEADS * HEAD_DIM
    W, HD = WINDOW, HEAD_DIM

    def body(q_ref, kp_ref, kc_ref, vp_ref, vc_ref, s_ref, o_ref):
        has_prev = pl.program_id(0) > 0
        for kh in range(KV_HEADS):
            ks = slice(kh * HD, (kh + 1) * HD)
            hs = [kh * rep + rr for rr in range(rep)]
            o = _attn_head(
                jnp.concatenate([q_ref[:, h * HD : (h + 1) * HD] for h in hs], axis=0),
                kp_ref[:, ks], kc_ref[:, ks], vp_ref[:, ks], vc_ref[:, ks],
                jnp.concatenate([jnp.broadcast_to(s_ref[:, h : h + 1], (W, 1)) for h in hs], axis=0), has_prev,
            )
            for rr, h in enumerate(hs):
                o_ref[:, h * HD : (h + 1) * HD] = o[rr * W : (rr + 1) * W].astype(o_ref.dtype)

    return pl.pallas_call(
        body,
        out_shape=jax.ShapeDtypeStruct((L, DQ), BF16),
        grid=(nb,),
        in_specs=[
            pl.BlockSpec((W, DQ), lambda n: (n, 0)),
            pl.BlockSpec((W, kw), lambda n: (jnp.maximum(n - 1, 0), 0)),
            pl.BlockSpec((W, kw), lambda n: (n, 0)),
            pl.BlockSpec((W, kw), lambda n: (jnp.maximum(n - 1, 0), 1)),
            pl.BlockSpec((W, kw), lambda n: (n, 1)),
            pl.BlockSpec((1, heads), lambda n: (0, 0)),
        ],
        out_specs=pl.BlockSpec((W, DQ), lambda n: (n, 0)),
        compiler_params=_params("parallel"),
        name=name,
    )(q, kv, kv, kv, kv, sinks)


def attn_bwd(q, kv, sinks, do, dkv_in, *, name):
    L, DQ = q.shape
    heads = DQ // HEAD_DIM
    rep = heads // KV_HEADS
    nb = L // WINDOW
    kw = KV_HEADS * HEAD_DIM
    W, HD = WINDOW, HEAD_DIM

    def blk(n):
        return jnp.minimum(n, nb - 1)

    def prev(n):
        return jnp.maximum(blk(n) - 1, 0)

    def outb(n):
        return jnp.maximum(n - 1, 0)

    def body(q_ref, kp_ref, kc_ref, vp_ref, vc_ref, s_ref, do_ref, dki_ref, dvi_ref,
             dq_ref, dk_ref, dv_ref, ds_ref, dbq_ref, dk_cur, dv_cur):
        n = pl.program_id(0)
        has_prev = n > 0

        @pl.when(n == 0)
        def _():
            ds_ref[...] = jnp.zeros_like(ds_ref)
            dbq_ref[...] = jnp.zeros_like(dbq_ref)
            dk_cur[...] = jnp.zeros_like(dk_cur)
            dv_cur[...] = jnp.zeros_like(dv_cur)

        @pl.when(n == nb)
        def _():
            dk_ref[...] = dki_ref[...] + dk_cur[...]
            dv_ref[...] = dvi_ref[...] + dv_cur[...]

        @pl.when(n < nb)
        def _():
            lanes = lax.broadcasted_iota(jnp.int32, (1, heads), 1)
            ds_row = jnp.zeros((1, heads), F32)
            for kh in range(KV_HEADS):
                ks = slice(kh * HD, (kh + 1) * HD)
                hs = [kh * rep + rr for rr in range(rep)]
                _, vjp = jax.vjp(
                    functools.partial(_attn_head, has_prev=has_prev),
                    jnp.concatenate([q_ref[:, h * HD : (h + 1) * HD] for h in hs], axis=0),
                    kp_ref[:, ks], kc_ref[:, ks], vp_ref[:, ks], vc_ref[:, ks],
                    jnp.concatenate([jnp.broadcast_to(s_ref[:, h : h + 1], (W, 1)) for h in hs], axis=0),
                )
                dq, dkp, dkc, dvp, dvc, dsk = vjp(
                    jnp.concatenate([do_ref[:, h * HD : (h + 1) * HD] for h in hs], axis=0))
                for rr, h in enumerate(hs):
                    dq_h = dq[rr * W : (rr + 1) * W]
                    dq_ref[:, h * HD : (h + 1) * HD] = dq_h.astype(dq_ref.dtype)
                    dbq_ref[:, h * HD : (h + 1) * HD] += jnp.sum(dq_h, axis=0, keepdims=True)
                    ds_row = ds_row + jnp.where(lanes == h, jnp.sum(dsk[rr * W : (rr + 1) * W], axis=0, keepdims=True), 0.0)
                dk_ref[:, ks] = dki_ref[:, ks] + dk_cur[:, ks] + dkp
                dv_ref[:, ks] = dvi_ref[:, ks] + dv_cur[:, ks] + dvp
                dk_cur[:, ks] = dkc
                dv_cur[:, ks] = dvc
            ds_ref[...] += ds_row

    dq, dk, dv, ds, dbq = pl.pallas_call(
        body,
        out_shape=(
            jax.ShapeDtypeStruct((L, DQ), BF16),
            jax.ShapeDtypeStruct((L, kw), F32),
            jax.ShapeDtypeStruct((L, kw), F32),
            jax.ShapeDtypeStruct((1, heads), F32),
            jax.ShapeDtypeStruct((1, DQ), F32),
        ),
        grid=(nb + 1,),
        in_specs=[
            pl.BlockSpec((W, DQ), lambda n: (blk(n), 0)),
            pl.BlockSpec((W, kw), lambda n: (prev(n), 0)),
            pl.BlockSpec((W, kw), lambda n: (blk(n), 0)),
            pl.BlockSpec((W, kw), lambda n: (prev(n), 1)),
            pl.BlockSpec((W, kw), lambda n: (blk(n), 1)),
            pl.BlockSpec((1, heads), lambda n: (0, 0)),
            pl.BlockSpec((W, DQ), lambda n: (blk(n), 0)),
            pl.BlockSpec((W, kw), lambda n: (outb(n), 0)),
            pl.BlockSpec((W, kw), lambda n: (outb(n), 1)),
        ],
        out_specs=(
            pl.BlockSpec((W, DQ), lambda n: (blk(n), 0)),
            pl.BlockSpec((W, kw), lambda n: (outb(n), 0)),
            pl.BlockSpec((W, kw), lambda n: (outb(n), 0)),
            pl.BlockSpec((1, heads), lambda n: (0, 0)),
            pl.BlockSpec((1, DQ), lambda n: (0, 0)),
        ),
        scratch_shapes=[pltpu.VMEM((W, kw), F32), pltpu.VMEM((W, kw), F32)],
        compiler_params=_params("arbitrary"),
        name=name,
    )(q, kv, kv, kv, kv, sinks, do, dkv_in, dkv_in)
    return dq, jnp.concatenate([dk, dv], axis=1), ds, dbq


def final_loss(x, fw, target, *, name):
    L, D = x.shape
    tl = _pick(L, 512, 8)

    def body(x_ref, fw_ref, t_ref, loss_ref, dx_ref, dfw_ref):
        @pl.when(pl.program_id(0) == 0)
        def _():
            loss_ref[...] = jnp.zeros_like(loss_ref)
            dfw_ref[...] = jnp.zeros_like(dfw_ref)

        xv = x_ref[...]
        fwv = fw_ref[...]
        r = lax.rsqrt(jnp.mean(xv * xv, axis=-1, keepdims=True) + EPS)
        xhat = xv * r
        err = xhat * fwv - t_ref[...]
        loss_ref[...] += 0.5 * jnp.sum(jnp.mean(err * err, axis=-1, keepdims=True), axis=0, keepdims=True)
        dy = err * (1.0 / D)
        dfw_ref[...] += jnp.sum(dy * xhat, axis=0, keepdims=True)
        dxhat = dy * fwv
        dx_ref[...] = r * (dxhat - xhat * jnp.mean(dxhat * xhat, axis=-1, keepdims=True))

    tile = pl.BlockSpec((tl, D), lambda i: (i, 0))
    row = pl.BlockSpec((1, D), lambda i: (0, 0))
    return pl.pallas_call(
        body,
        out_shape=(
            jax.ShapeDtypeStruct((1, 1), F32),
            jax.ShapeDtypeStruct((L, D), F32),
            jax.ShapeDtypeStruct((1, D), F32),
        ),
        grid=(L // tl,),
        in_specs=[tile, row, tile],
        out_specs=(pl.BlockSpec((1, 1), lambda i: (0, 0)), tile, row),
        compiler_params=_params("arbitrary"),
        name=name,
    )(x, fw, target)


def outer8(ct, d, *, name):
    D, B = ct.shape
    S, _, N = d.shape
    tm = _pick(D, 512, 8)
    tn = _pick(N, 256, LANE)

    def body(c_ref, d_ref, o_ref):
        acc = c_ref[:, 0:1] * d_ref[0:1, :]
        for b in range(1, B):
            acc = acc + c_ref[:, b : b + 1] * d_ref[b : b + 1, :]
        o_ref[...] = acc

    return pl.pallas_call(
        body,
        out_shape=jax.ShapeDtypeStruct((S, D, N), F32),
        grid=(S, D // tm, N // tn),
        in_specs=[
            pl.BlockSpec((tm, B), lambda s, i, j: (i, 0)),
            pl.BlockSpec((None, B, tn), lambda s, i, j: (s, 0, j)),
        ],
        out_specs=pl.BlockSpec((None, tm, tn), lambda s, i, j: (s, i, j)),
        compiler_params=_params("parallel", "parallel", "parallel"),
        name=name,
    )(ct, d)


def reduce8(g, *, name):
    nd, R, N = g.shape

    def body(g_ref, o_ref):
        acc = g_ref[0]
        for b in range(1, nd):
            acc = acc + g_ref[b]
        o_ref[...] = acc

    return pl.pallas_call(
        body,
        out_shape=jax.ShapeDtypeStruct((R, N), F32),
        name=name,
    )(g)


def _as3(a):
    if a.ndim == 1:
        return a.reshape(1, 1, -1)
    if a.ndim == 2:
        return a.reshape((1,) + a.shape)
    return a.reshape((-1,) + a.shape[-2:])


def adamw(w, g, m, v, *, name):
    shape = w.shape
    w3, g3, m3, v3 = _as3(w), _as3(g), _as3(m), _as3(v)
    B, R, C = w3.shape
    tr = _pick(R, max(8, (1 << 19) // max(C, 1) // 8 * 8), 8)

    def body(w_ref, g_ref, m_ref, v_ref, d_ref, nm_ref, nv_ref):
        gv = g_ref[...]
        mn = ADAM_B1 * m_ref[...] + (1.0 - ADAM_B1) * gv
        vn = ADAM_B2 * v_ref[...] + (1.0 - ADAM_B2) * (gv * gv)
        m_hat = mn / (1.0 - ADAM_B1 ** ADAM_STEP)
        v_hat = vn / (1.0 - ADAM_B2 ** ADAM_STEP)
        d_ref[...] = -ADAM_LR * (m_hat / (jnp.sqrt(v_hat) + ADAM_EPS) + ADAM_WD * w_ref[...])
        nm_ref[...] = mn
        nv_ref[...] = vn

    tile = pl.BlockSpec((None, tr, C), lambda b, i: (b, i, 0))
    sds = jax.ShapeDtypeStruct((B, R, C), F32)
    d, nm, nv = pl.pallas_call(
        body,
        out_shape=(sds, sds, sds),
        grid=(B, R // tr),
        in_specs=[tile, tile, tile, tile],
        out_specs=(tile, tile, tile),
        compiler_params=_params("parallel", "parallel"),
        name=name,
    )(w3, g3, m3, v3)
    return d.reshape(shape), nm.reshape(shape), nv.reshape(shape)


def _place():
    return lax.axis_index("x"), lax.axis_index("y"), lax.axis_index("c")


def _flip(v, bit):
    return (1 - v) if bit else v


def ag8(v, *, act=None, after=None, name):
    R, N = v.shape
    extra = [] if after is None else [after]

    def body(*refs):
        v_ref = refs[0]
        out_ref, stage, send_sems, recv_sems = refs[1 + len(extra):]
        x, y, c = _place()
        me = 4 * x + 2 * y + c
        val = v_ref[...]
        if act is not None:
            val = act(val)
        stage[...] = val
        out_ref[me] = val
        sends = []
        for k in range(1, N_DEV):
            px, py, pc = _flip(x, k & 4), _flip(y, k & 2), _flip(c, k & 1)
            cp = pltpu.make_async_remote_copy(
                src_ref=stage, dst_ref=out_ref.at[me], send_sem=send_sems.at[k - 1], recv_sem=recv_sems.at[k - 1],
                device_id=(px, py, pc), device_id_type=MESH,
            )
            cp.start()
            sends.append(cp)
        for k in range(1, N_DEV):
            px, py, pc = _flip(x, k & 4), _flip(y, k & 2), _flip(c, k & 1)
            pltpu.make_async_remote_copy(
                src_ref=stage, dst_ref=out_ref.at[4 * px + 2 * py + pc], send_sem=send_sems.at[k - 1],
                recv_sem=recv_sems.at[k - 1], device_id=(px, py, pc), device_id_type=MESH,
            ).wait_recv()
        for cp in sends:
            cp.wait_send()

    return pl.pallas_call(
        body,
        out_shape=jax.ShapeDtypeStruct((N_DEV, R, N), F32),
        in_specs=[pl.BlockSpec(memory_space=pltpu.VMEM)] + [_ANY] * len(extra),
        out_specs=pl.BlockSpec(memory_space=pltpu.VMEM),
        scratch_shapes=[
            pltpu.VMEM((R, N), F32),
            pltpu.SemaphoreType.DMA((N_DEV - 1,)),
            pltpu.SemaphoreType.DMA((N_DEV - 1,)),
        ],
        name=name,
    )(v, *extra)


def _other_chips(x, y):
    chips = [(1 - x, y), (x, 1 - y), (1 - x, 1 - y)]
    return chips, [2 * px + py for px, py in chips]


_HBM = pl.BlockSpec(memory_space=pltpu.HBM)


_SEM = pl.BlockSpec(memory_space=pltpu.SEMAPHORE)
_ANY = pl.BlockSpec(memory_space=pl.ANY)
_EFFECT = pltpu.SideEffectType.DATAFLOW_SIDE_EFFECTING


def _gather_copies(srcs, lands, send_sems, recv_sems):
    x, y, c = _place()
    k_me = 2 * x + y
    chips, kidx = _other_chips(x, y)
    cps = []
    for w in range(len(srcs)):
        for j, (px, py) in enumerate(chips):
            def copy(dst, w=w, j=j, px=px, py=py):
                return pltpu.make_async_remote_copy(
                    src_ref=srcs[w].at[:, c], dst_ref=dst, send_sem=send_sems.at[3 * w + j],
                    recv_sem=recv_sems.at[3 * w + j], device_id=(px, py, c), device_id_type=MESH,
                )
            cps.append((copy(lands[w].at[:, k_me, c]), copy(lands[w].at[:, kidx[j], c])))
    return cps


def _fill_copies(srcs, lands, send_sems, recv_sems):
    x, y, c = _place()
    k_me = 2 * x + y
    _, kidx = _other_chips(x, y)
    sib = (x, y, 1 - c)
    cps = []
    for w in range(len(srcs)):
        own = pltpu.make_async_remote_copy(
            src_ref=srcs[w], dst_ref=lands[w].at[:, k_me], send_sem=send_sems.at[4 * w + 3], recv_sem=recv_sems.at[4 * w + 3],
            device_id=sib, device_id_type=MESH,
        )
        cps.append((own, own))
        for j in range(3):
            def copy(half, w=w, j=j):
                part = lands[w].at[:, kidx[j], half]
                return pltpu.make_async_remote_copy(
                    src_ref=part, dst_ref=part, send_sem=send_sems.at[4 * w + j], recv_sem=recv_sems.at[4 * w + j],
                    device_id=sib, device_id_type=MESH,
                )
            cps.append((copy(c), copy(1 - c)))
    return cps


def _sibling_copies(srcs, lands, send_sems, recv_sems):
    x, y, c = _place()
    cps = []
    for w in range(len(srcs)):
        cp = pltpu.make_async_remote_copy(
            src_ref=srcs[w].at[:, :, 1 - c], dst_ref=lands[w], send_sem=send_sems.at[w], recv_sem=recv_sems.at[w],
            device_id=(x, y, 1 - c), device_id_type=MESH,
        )
        cps.append((cp, cp))
    return cps


def _reduce_copies(srcs, lands, send_sems, recv_sems):
    x, y, c = _place()
    chips, kidx = _other_chips(x, y)
    cps = []
    for w in range(len(srcs)):
        for j, (px, py) in enumerate(chips):
            cp = pltpu.make_async_remote_copy(
                src_ref=srcs[w].at[:, kidx[j]], dst_ref=lands[w].at[j], send_sem=send_sems.at[3 * w + j],
                recv_sem=recv_sems.at[3 * w + j], device_id=(px, py, c), device_id_type=MESH,
            )
            cps.append((cp, cp))
    return cps


def split_start(copies, srcs, land_shapes, after, *, name, lands=None, per=3):
    n = len(srcs)

    def body(*refs):
        src_refs, land_refs = refs[:n], refs[n : 2 * n]
        send_sems, recv_sems = refs[2 * n + 1], refs[2 * n + 2]
        token = refs[-1]
        for cp, _ in copies(src_refs, land_refs, send_sems, recv_sems):
            cp.start()
        token[...] = jnp.zeros_like(token)

    if lands is None:
        lands = [lax.empty(sh, s.dtype) for sh, s in zip(land_shapes, srcs)]
    land_shapes = [a.shape for a in lands]
    lands = [pltpu.with_memory_space_constraint(a, pltpu.HBM) for a in lands]
    srcs = [pltpu.with_memory_space_constraint(s, pltpu.HBM) for s in srcs]
    out = pl.pallas_call(
        body,
        out_shape=(
            pltpu.SemaphoreType.DMA((per * n,)), pltpu.SemaphoreType.DMA((per * n,)),
            *[pltpu.HBM(s.shape, s.dtype) for s in srcs],
            *[pltpu.HBM(sh, s.dtype) for sh, s in zip(land_shapes, srcs)],
            jax.ShapeDtypeStruct((8, LANE), F32),
        ),
        in_specs=[_HBM] * (2 * n) + [_ANY],
        out_specs=(_SEM, _SEM, *([_HBM] * (2 * n)), pl.BlockSpec(memory_space=pltpu.VMEM)),
        input_output_aliases={i: 2 + i for i in range(2 * n)},
        compiler_params=pltpu.CompilerParams(has_side_effects=_EFFECT),
        name=name,
    )(*srcs, *lands, after)
    return out[0], out[1], list(out[2 : 2 + n]), list(out[2 + n : 2 + 2 * n]), out[-1]


def split_wait(copies, send_sems, recv_sems, srcs, lands, after, *, name):
    n = len(srcs)

    def body(*refs):
        src_refs, land_refs = refs[:n], refs[n : 2 * n]
        send_ref, recv_ref = refs[2 * n], refs[2 * n + 1]
        for sent, arrives in copies(src_refs, land_refs, send_ref, recv_ref):
            sent.wait_send()
            arrives.wait_recv()

    out = pl.pallas_call(
        body,
        out_shape=tuple(pltpu.HBM(a.shape, a.dtype) for a in list(srcs) + list(lands)),
        in_specs=[_HBM] * (2 * n) + [_SEM, _SEM, _ANY],
        out_specs=tuple([_HBM] * (2 * n)),
        input_output_aliases={i: i for i in range(2 * n)},
        compiler_params=pltpu.CompilerParams(has_side_effects=_EFFECT),
        name=name,
    )(*srcs, *lands, send_sems, recv_sems, after)
    return list(out[:n]), list(out[n:])


def rs_share(halves, *, name):
    n = len(halves)

    def body(*refs):
        outs = refs[n : 2 * n]
        send_sems, recv_sems = refs[2 * n :]
        x, y, c = _place()
        cps = []
        for w in range(n):
            cp = pltpu.make_async_remote_copy(
                src_ref=outs[w].at[:, c], dst_ref=outs[w].at[:, c], send_sem=send_sems.at[w], recv_sem=recv_sems.at[w],
                device_id=(x, y, 1 - c), device_id_type=MESH,
            )
            cp.start()
            cps.append(cp)
        for w, cp in enumerate(cps):
            cp.wait_send()
            pltpu.make_async_remote_copy(
                src_ref=outs[w].at[:, c], dst_ref=outs[w].at[:, 1 - c], send_sem=send_sems.at[w], recv_sem=recv_sems.at[w],
                device_id=(x, y, 1 - c), device_id_type=MESH,
            ).wait_recv()

    return pl.pallas_call(
        body,
        out_shape=tuple(jax.ShapeDtypeStruct(h.shape, h.dtype) for h in halves),
        in_specs=[_HBM] * n,
        out_specs=tuple([_HBM] * n),
        scratch_shapes=[pltpu.SemaphoreType.DMA((n,)), pltpu.SemaphoreType.DMA((n,))],
        input_output_aliases={w: w for w in range(n)},
        name=name,
    )(*halves)


def _row_tile(R, C):
    return _pick(R, max(16, (1 << 19) // C // 16 * 16), 16)


def _my_core():
    return lax.axis_index("c")


def _my_chip():
    return 2 * lax.axis_index("x") + lax.axis_index("y")


def rs_add_pair(g, r, *, name):
    M, K, _, R, C = g.shape
    tr = _row_tile(R, C)

    def body(g_ref, r_ref, o_ref):
        o_ref[...] = (g_ref[...].astype(F32) + r_ref[...].astype(F32)).astype(o_ref.dtype)

    blk = pl.BlockSpec((None, K, tr, C), lambda m, i: (m, 0, i, 0))
    return pl.pallas_call(
        body,
        out_shape=jax.ShapeDtypeStruct((M, K, R, C), BF16),
        grid=(M, R // tr),
        in_specs=[pl.BlockSpec((None, K, None, tr, C), lambda m, i: (m, 0, _my_core(), i, 0)), blk],
        out_specs=blk,
        compiler_params=_params("parallel", "parallel"),
        name=name,
    )(g, r)


def rs_add_final(g, r, t, full, m0, *, name):
    M, K, _, R, C = g.shape
    tr = _row_tile(R, C)

    def body(g_ref, r_ref, t_ref, full_ref, o_ref):
        acc = g_ref[...].astype(F32) + r_ref[...].astype(F32)
        for j in range(3):
            acc = acc + t_ref[j].astype(F32)
        o_ref[...] = acc

    return pl.pallas_call(
        body,
        out_shape=jax.ShapeDtypeStruct(full.shape, full.dtype),
        grid=(M, R // tr),
        in_specs=[
            pl.BlockSpec((None, None, None, tr, C), lambda m, i: (m, _my_chip(), _my_core(), i, 0)),
            pl.BlockSpec((None, None, tr, C), lambda m, i: (m, _my_chip(), i, 0)),
            pl.BlockSpec((3, None, tr, C), lambda m, i: (0, m, i, 0)),
            _ANY,
        ],
        out_specs=pl.BlockSpec((None, None, tr, C), lambda m, i: (m0 + m, _my_core(), i, 0)),
        input_output_aliases={3: 0},
        compiler_params=_params("parallel", "parallel"),
        name=name,
    )(g, r, t, full)


WEIGHTS = ["ffn_norm_w", "ffn_w_gu", "ffn_w_down", "mod_w", "mod_b", "mix_norm_w", "ssm_w_in", "ssm_conv_w", "ssm_conv_b",
           "ssm_dt_bias", "ssm_a_log", "ssm_d", "ssm_norm_w", "ssm_w_out", "kv_norm_w", "kv_mod_w", "kv_mod_b", "w_kv", "b_kv",
           "attn_w_q", "attn_b_q", "attn_sinks", "attn_w_o", "attn_b_o", "final_norm_w"]
GATHERED = ["ffn_w_gu", "ffn_w_down", "ssm_w_in", "ssm_w_out", "w_kv", "attn_w_q", "attn_w_o"]
COLUMN_PARALLEL = ["mod_w", "kv_mod_w"]
SMALL_SHARDED = ["ffn_norm_w", "ssm_conv_w", "ssm_conv_b", "ssm_norm_w"]
SMALL = [n for n in WEIGHTS if n not in GATHERED and n not in COLUMN_PARALLEL]


def _row_halves(a):
    a = a.reshape((-1,) + a.shape[-2:])
    return a.reshape(a.shape[0], 2, a.shape[1] // 2, a.shape[2])


def _pack(arrs, rows=8):
    flat = jnp.concatenate([a.reshape(-1) for a in arrs])
    n = flat.shape[0]
    pad = (-n) % (rows * LANE)
    return jnp.pad(flat, (0, pad)).reshape(rows, -1), n


def _unpack(flat, like):
    out, o = [], 0
    for s in like:
        k = 1
        for d in s:
            k *= d
        out.append(flat[o : o + k].reshape(s))
        o += k
    return out


def kernel(x, c, ffn_norm_w, ffn_w_gu, ffn_w_down, mod_w, mod_b, mix_norm_w, ssm_w_in, ssm_conv_w, ssm_conv_b, ssm_dt_bias, ssm_a_log, ssm_d, ssm_norm_w, ssm_w_out, kv_norm_w, kv_mod_w, kv_mod_b, w_kv, b_kv, attn_w_q, attn_b_q, attn_sinks, attn_w_o, attn_b_o, final_norm_w, loss_target, m_ffn_norm_w, m_ffn_w_gu, m_ffn_w_down, m_mod_w, m_mod_b, m_mix_norm_w, m_ssm_w_in, m_ssm_conv_w, m_ssm_conv_b, m_ssm_dt_bias, m_ssm_a_log, m_ssm_d, m_ssm_norm_w, m_ssm_w_out, m_kv_norm_w, m_kv_mod_w, m_kv_mod_b, m_w_kv, m_b_kv, m_attn_w_q, m_attn_b_q, m_attn_sinks, m_attn_w_o, m_attn_b_o, m_final_norm_w, v_ffn_norm_w, v_ffn_w_gu, v_ffn_w_down, v_mod_w, v_mod_b, v_mix_norm_w, v_ssm_w_in, v_ssm_conv_w, v_ssm_conv_b, v_ssm_dt_bias, v_ssm_a_log, v_ssm_d, v_ssm_norm_w, v_ssm_w_out, v_kv_norm_w, v_kv_mod_w, v_kv_mod_b, v_w_kv, v_b_kv, v_attn_w_q, v_attn_b_q, v_attn_sinks, v_attn_w_o, v_attn_b_o, v_final_norm_w):
    env = dict(locals())
    W = {n: env[n] for n in WEIGHTS}
    MOM = {n: env["m_" + n] for n in WEIGHTS}
    VAR = {n: env["v_" + n] for n in WEIGHTS}

    ax, ay, ac = _place()
    kme = 2 * ax + ay
    me = 4 * ax + 2 * ay + ac

    xs = x[0]
    target = loss_target[0]
    L, D = xs.shape
    depth, n_a = ffn_w_gu.shape[0], ssm_w_in.shape[0]
    n_b = depth - n_a
    T = ffn_w_gu.shape[-1]
    DI = ssm_w_out.shape[1] * N_CHIPS
    CI = ssm_w_in.shape[2]
    CC = ssm_conv_w.shape[2] * N_CHIPS
    MW = mod_w.shape[2]
    KW = kv_mod_w.shape[1]
    KVD = w_kv.shape[1]

    def chip_cols(a, width):
        return lax.dynamic_slice_in_dim(a, kme * width, width, axis=a.ndim - 1)

    def ffn_items(i, j):
        return [("ffn_w_gu", 2 * i + j, ffn_w_gu[i, j]), ("ffn_w_down", 2 * i + j, ffn_w_down[i, j])]

    def mix_items(i):
        if i < n_a:
            return [("ssm_w_in", i, ssm_w_in[i]), ("ssm_w_out", i, ssm_w_out[i])]
        return [("attn_w_q", i - n_a, attn_w_q[i - n_a]), ("attn_w_o", i - n_a, attn_w_o[i - n_a])]

    def layer_items(i, order):
        kv_items = [("w_kv", 0, w_kv)] if i == n_a else []
        if order == "fwd":
            return kv_items + ffn_items(i, 0) + mix_items(i) + ffn_items(i, 1)
        return ffn_items(i, 1) + mix_items(i) + ffn_items(i, 0) + kv_items

    def sub_items(sub):
        return {"F": ffn_items, "M": mix_items, "KV": lambda: [("w_kv", 0, w_kv)]}[sub[0]](*sub[1:])

    subs = []
    for i in range(depth):
        subs += ([("KV",)] if i == n_a else []) + [("F", i, 0), ("M", i), ("F", i, 1)]
    assert n_a >= 2 and depth >= 2
    cuts = [0, 1, 2, 4, 6] + [3 * (i + 1) + (1 if i >= n_a else 0) for i in range(2, depth)]
    fwd_plan = [subs[a:b] for a, b in zip(cuts[:-1], cuts[1:])]
    fwd_stages = [[it for sub in stage for it in sub_items(sub)] for stage in fwd_plan]
    bwd_stages = [layer_items(i, "bwd") for i in range(depth - 1, 0, -1)] + [ffn_items(0, 1) + mix_items(0), ffn_items(0, 0)]

    gw, inflight = {}, {}

    def chips_begin(s, after):
        keys = [(n, m0) for n, m0, _ in fwd_stages[s]]
        shards = [_row_halves(a.astype(BF16)) for _, _, a in fwd_stages[s]]
        land_shapes = [(sh.shape[0], N_CHIPS) + sh.shape[1:] for sh in shards]
        ssem, rsem, srcs, lands, token = split_start(_gather_copies, shards, land_shapes, after, name=f"gather_start_{s}")
        inflight[s] = (keys, ssem, rsem, srcs, lands)
        return token[0:1, 0:1]

    def n_head(s):
        return 2 if fwd_plan[s][0][0] == "KV" else 1

    def n_first(s):
        return sum(len(sub_items(sub)) for sub in fwd_plan[s][: n_head(s)])

    def cores_begin(s, after):
        keys, ssem, rsem, srcs, lands = inflight.pop(s)
        srcs, lands = split_wait(_gather_copies, ssem, rsem, srcs, lands, after, name=f"gather_wait_{s}")
        k = n_first(s)
        a_sem, b_sem, a_src, a_land, tok = split_start(
            _fill_copies, srcs[:k], None, after, name=f"fill_start_{s}a", lands=lands[:k], per=4)
        if k < len(keys):
            inflight[s] = (keys[k:],) + split_start(
                _fill_copies, srcs[k:], None, tok, name=f"fill_start_{s}b", lands=lands[k:], per=4)
            tok = inflight[s][-1]
        gw.update(zip(keys[:k], split_wait(_fill_copies, a_sem, b_sem, a_src, a_land, tok, name=f"fill_wait_{s}a")[1]))
        return tok[0:1, 0:1]

    def cores_end(s, after):
        if s in inflight:
            keys, ssem, rsem, srcs, lands, _ = inflight.pop(s)
            gw.update(zip(keys, split_wait(_fill_copies, ssem, rsem, srcs, lands, after, name=f"fill_wait_{s}b")[1]))

    def g_gu(i, j):
        return gw["ffn_w_gu", 2 * i + j].reshape(N_CHIPS, D, T)

    def g_dn(i, j):
        return gw["ffn_w_down", 2 * i + j].reshape(2, T, D)

    def g_full(n, m0):
        a = gw[n, m0]
        return a.reshape(N_CHIPS * 2 * a.shape[-2], a.shape[-1])

    sm_like = [W[n].shape for n in SMALL_SHARDED]
    sm_pack, sm_n = _pack([W[n] for n in SMALL_SHARDED])
    sm_gathered = ag8(sm_pack, name="ag_small_w")
    sm_all = sm_gathered[0::2].reshape(N_CHIPS, -1)[:, :sm_n]
    full = {}
    for n, part in zip(SMALL_SHARDED, zip(*[_unpack(sm_all[k], sm_like) for k in range(N_CHIPS)])):
        full[n] = jnp.concatenate(part, axis=-1)

    c_all = ag8(c, act=_silu, after=sm_gathered, name="ag_c").reshape(N_DEV, D)
    c_all = c_all + chips_begin(0, c_all)
    p_mod = mm(c_all, mod_w, bias=chip_cols(mod_b, MW)[:, None, :], name="mod_mm")
    p_kv = mm(c_all, kv_mod_w, bias=chip_cols(kv_mod_b, KW)[None, :], name="kvmod_mm")
    p_all = jnp.concatenate([jnp.transpose(p_mod, (1, 0, 2)).reshape(N_DEV, depth * MW), p_kv], axis=1)
    p_mine = lax.dynamic_index_in_dim(ag8(p_all, name="ag_mod")[0::2], me, axis=1, keepdims=False)
    mod = jnp.transpose(p_mine[:, : depth * MW].reshape(N_CHIPS, depth, MW), (1, 0, 2)).reshape(depth, N_MOD * D)
    kvmod = p_mine[:, depth * MW :].reshape(1, 2 * D)
    mods = [[mod[i : i + 1, j * D : (j + 1) * D] for j in range(N_MOD)] for i in range(depth)]
    kv_shift, kv_scale = kvmod[:, :D], kvmod[:, D:]

    def ffn_fwd(xin, i, j, sh, sc, gt):
        h = norm_mod_fwd(xin, full["ffn_norm_w"][i, j][None], sc, sh, name=f"ffn_norm_{i}_{j}")
        gu, a = ffn_up(h, g_gu(i, j), name=f"ffn_gu_{i}_{j}")
        f, xo = mm(a, g_dn(i, j), reduce_s=True, resid=(xin, gt, FFN_HALF), name=f"ffn_down_{i}_{j}")
        return xo, (xin, gu, a, f, h)

    def ssm_fwd(xin, i, sh, sc, gt):
        h = norm_mod_fwd(xin, mix_norm_w[i][None], sc, sh, name=f"mix_norm_{i}")
        zx4 = mm(h, gw["ssm_w_in", i].reshape(N_CHIPS, D, CI), name=f"ssm_in_{i}")
        zx = jnp.transpose(zx4, (1, 0, 2)).reshape(L, N_CHIPS * CI)
        xbc = conv_fwd(zx, full["ssm_conv_w"][i], full["ssm_conv_b"][i][None], DI, name=f"ssm_conv_{i}")
        dt_raw = zx[:, DI + CC :]
        y, states = ssd_fwd(xbc, dt_raw, ssm_dt_bias[i][None], ssm_a_log[i][None], ssm_d[i][None], DI, name=f"ssd_{i}")
        yn = gnorm_fwd(y, zx, full["ssm_norm_w"][i][None], name=f"ssm_gnorm_{i}")
        f, xo = mm(yn, g_full("ssm_w_out", i), resid=(xin, gt, 1.0), name=f"ssm_out_{i}")
        return xo, (xin, zx, xbc, dt_raw, y, states, yn, f, h)

    def att_fwd(xin, i, kv, sh, sc, gt):
        l = i - n_a
        h = norm_mod_fwd(xin, mix_norm_w[i][None], sc, sh, name=f"mix_norm_{i}")
        q = mm(h, g_full("attn_w_q", l), bias=attn_b_q[l][None], name=f"att_q_{i}")
        o = attn_fwd(q, kv, attn_sinks[l][None], name=f"att_{i}")
        f, xo = mm(o, g_full("attn_w_o", l), bias=attn_b_o[l][None], resid=(xin, gt, 1.0), name=f"att_o_{i}")
        return xo, (xin, q, o, f, h)

    saved = [[None, None, None] for _ in range(depth)]
    xc = xs
    kv = x_kv = None
    n_stage = len(fwd_stages)

    def run_sub(sub, xc, tok):
        nonlocal kv, x_kv
        if sub[0] == "KV":
            x_kv = xc
            hkv = norm_mod_fwd(xc, kv_norm_w[None], kv_scale, kv_shift + tok, name="kv_norm")
            kv = mm(hkv, g_full("w_kv", 0), bias=b_kv[None], name="kv_proj")
            return xc
        i = sub[1]
        if sub[0] == "F":
            sh, sc, gt = mods[i][6 * sub[2] : 6 * sub[2] + 3]
            xc, saved[i][2 * sub[2]] = ffn_fwd(xc, i, sub[2], sh + tok, sc, gt)
            return xc
        sh, sc, gt = mods[i][3:6]
        xc, saved[i][1] = ssm_fwd(xc, i, sh + tok, sc, gt) if i < n_a else att_fwd(xc, i, kv, sh + tok, sc, gt)
        return xc

    def run_stage(s, xc, tok):
        for k, sub in enumerate(fwd_plan[s]):
            if k == n_head(s):
                cores_end(s, xc)
            xc = run_sub(sub, xc, tok if k < n_head(s) else 0.0)
        return xc

    dep = cores_begin(0, kvmod)
    for s in range(n_stage):
        tok = chips_begin(s + 1, dep) if s + 1 < n_stage else 0.0
        xc = run_stage(s, xc, tok)
        if s + 1 < n_stage:
            dep = cores_begin(s + 1, xc)

    loss_part, dx, d_final = final_loss(xc, final_norm_w[None], target, name="loss_head")
    loss = lax.psum(loss_part[0, 0], ("x", "y", "c"))

    wg = {}
    sg = {
        "ffn_norm_w": [[None, None] for _ in range(depth)], "mix_norm_w": [None] * depth, "mod": [None] * depth,
        "ssm_conv_w": [None] * n_a, "ssm_conv_b": [None] * n_a, "ssm_dt_bias": [None] * n_a, "ssm_a_log": [None] * n_a,
        "ssm_d": [None] * n_a, "ssm_norm_w": [None] * n_a, "attn_b_q": [None] * n_b, "attn_sinks": [None] * n_b,
        "attn_b_o": [None] * n_b,
    }

    def ffn_bwd(dxo, i, j, sv, sh, sc, gt):
        xin, gu, a, f, h = sv
        df, dgt, _ = gate_bwd(f, dxo, gt, FFN_HALF, name=f"ffn_res_bwd_{i}_{j}")
        dgu = ffn_down_bwd(df, g_dn(i, j), gu, name=f"ffn_down_dx_{i}_{j}")
        wg["ffn_w_down", 2 * i + j] = mm(a, df, mode="tn", out_dtype=BF16, name=f"ffn_down_dw_{i}_{j}")
        nw = full["ffn_norm_w"][i, j][None]
        dxi, dnw, dsc, dsh = mm_norm_bwd(dgu, g_gu(i, j), xin, nw, sc, dxo, name=f"ffn_gu_dx_{i}_{j}")
        wg["ffn_w_gu", 2 * i + j] = mm(h, dgu, mode="tn", out_dtype=BF16, name=f"ffn_gu_dw_{i}_{j}")
        sg["ffn_norm_w"][i][j] = dnw
        return dxi, (dsh, dsc, dgt)

    def ssm_bwd(dxo, i, sv, sh, sc, gt):
        xin, zx, xbc, dt_raw, y, states, yn, f, h = sv
        df, dgt, _ = gate_bwd(f, dxo, gt, 1.0, name=f"mix_res_bwd_{i}")
        dyn = mm(df, g_full("ssm_w_out", i), mode="nt", name=f"ssm_out_dx_{i}")
        wg["ssm_w_out", i] = mm(yn, df, mode="tn", out_dtype=BF16, name=f"ssm_out_dw_{i}")
        dy, dz, dnorm = gnorm_bwd(y, zx, full["ssm_norm_w"][i][None], dyn, name=f"ssm_gnorm_bwd_{i}")
        dxbc, ddt, dbias, dalog, ddsk = ssd_bwd(
            xbc, dt_raw, ssm_dt_bias[i][None], ssm_a_log[i][None], ssm_d[i][None], states, dy, DI, name=f"ssd_bwd_{i}"
        )
        du, dcw, dcb = conv_bwd(zx, full["ssm_conv_w"][i], full["ssm_conv_b"][i][None], dxbc, DI, name=f"ssm_conv_bwd_{i}")
        dzx = jnp.concatenate([dz, du, ddt], axis=1).astype(BF16)
        dzx4 = jnp.transpose(dzx.reshape(L, N_CHIPS, CI), (1, 0, 2))
        nw = mix_norm_w[i][None]
        dxi, dnw, dsc, dsh = mm_norm_bwd(
            dzx4, gw["ssm_w_in", i].reshape(N_CHIPS, D, CI), xin, nw, sc, dxo, name=f"ssm_in_dx_{i}")
        wg["ssm_w_in", i] = mm(h, dzx4, mode="tn", out_dtype=BF16, name=f"ssm_in_dw_{i}")
        sg["mix_norm_w"][i] = dnw
        sg["ssm_conv_w"][i], sg["ssm_conv_b"][i], sg["ssm_norm_w"][i] = dcw, dcb, dnorm
        sg["ssm_dt_bias"][i], sg["ssm_a_log"][i], sg["ssm_d"][i] = dbias, dalog, ddsk
        return dxi, (dsh, dsc, dgt)

    def att_bwd(dxo, i, sv, dkv, sh, sc, gt):
        l = i - n_a
        xin, q, o, f, h = sv
        df, dgt, dfsum = gate_bwd(f, dxo, gt, 1.0, name=f"mix_res_bwd_{i}")
        do = mm(df, g_full("attn_w_o", l), mode="nt", name=f"att_o_dx_{i}")
        wg["attn_w_o", l] = mm(o, df, mode="tn", out_dtype=BF16, name=f"att_o_dw_{i}")
        dq, dkv, dsink, dbq = attn_bwd(q, kv, attn_sinks[l][None], do, dkv, name=f"att_bwd_{i}")
        nw = mix_norm_w[i][None]
        dxi, dnw, dsc, dsh = mm_norm_bwd(dq, g_full("attn_w_q", l), xin, nw, sc, dxo, name=f"att_q_dx_{i}")
        wg["attn_w_q", l] = mm(h, dq, mode="tn", out_dtype=BF16, name=f"att_q_dw_{i}")
        sg["mix_norm_w"][i] = dnw
        sg["attn_b_q"][l], sg["attn_sinks"][l], sg["attn_b_o"][l] = dbq, dsink, dfsum
        return dxi, dkv, (dsh, dsc, dgt)

    gfull = {n: lax.empty(_row_halves(W[n]).shape, F32) for n in GATHERED}

    pending, sib = [], []

    def sibling_begin(items, after):
        parts = []
        for n, m0, a in items:
            m, _, rh, cc = _row_halves(a).shape
            parts.append(wg.pop((n, m0)).reshape(m, N_CHIPS, 2, rh, cc))
        land_shapes = [p.shape[:2] + p.shape[3:] for p in parts]
        tag = f"{items[0][0]}_{items[0][1]}"
        ssem, rsem, srcs, lands, token = split_start(
            _sibling_copies, parts, land_shapes, after, name=f"rs_sibling_start_{tag}", per=1)
        sib.append((tag, [(n, m0) for n, m0, _ in items], ssem, rsem, srcs, lands))
        return token[0:1, 0:1]

    def reduce_step(s, after):
        if pending:
            reduce_end(after)
        items = bwd_stages[s]
        mine, theirs = {}, {}
        while sib:
            tag, keys, ssem, rsem, srcs, lands = sib.pop(0)
            srcs, lands = split_wait(_sibling_copies, ssem, rsem, srcs, lands, after, name=f"rs_sibling_wait_{tag}")
            mine.update(zip(keys, srcs))
            theirs.update(zip(keys, lands))
        parts = [mine[n, m0] for n, m0, _ in items]
        from_sib = [theirs[n, m0] for n, m0, _ in items]
        pair = [rs_add_pair(g, r, name=f"rs_pair_{n}_{m0}") for (n, m0, _), g, r in zip(items, parts, from_sib)]
        land_shapes = [(3, p.shape[0]) + p.shape[2:] for p in pair]
        ssem, rsem, srcs, lands, token = split_start(_reduce_copies, pair, land_shapes, after, name=f"rs_chips_start_{s}")
        pending.append((s, items, parts, from_sib, ssem, rsem, srcs, lands))
        return token[0:1, 0:1]

    def reduce_end(after):
        s, items, parts, from_sib, ssem, rsem, srcs, lands = pending.pop()
        _, from_chips = split_wait(_reduce_copies, ssem, rsem, srcs, lands, after, name=f"rs_chips_wait_{s}")
        for (n, m0, _), g, r, t in zip(items, parts, from_sib, from_chips):
            gfull[n] = rs_add_final(g, r, t, gfull[n], m0, name=f"rs_final_{n}_{m0}")

    dkv = jnp.zeros((L, KVD), F32)
    d_kvnorm = d_kvmod = d_bkv = None
    tok = 0.0
    for i in reversed(range(depth)):
        sh1, sc1, g1, shm, scm, gm, sh2, sc2, g2 = mods[i]
        s1, sm, s2 = saved[i]
        dx, dm2 = ffn_bwd(dx, i, 1, s2, sh2, sc2, g2 + tok)
        if i < n_a:
            dx, dmm = ssm_bwd(dx, i, sm, shm, scm, gm)
        else:
            dx, dkv, dmm = att_bwd(dx, i, sm, dkv, shm, scm, gm)
        tok = sibling_begin(ffn_items(i, 1) + mix_items(i), dx)
        if i == 0:
            tok = tok + reduce_step(depth - 1, dx)
        dx, dm1 = ffn_bwd(dx, i, 0, s1, sh1, sc1, g1 + tok)
        tok = sibling_begin(ffn_items(i, 0), dx)
        sg["mod"][i] = jnp.concatenate(list(dm1) + list(dmm) + list(dm2), axis=1)
        if i == n_a:
            d_bkv = colsum(dkv, name="kv_bias_bwd")
            hkv = norm_mod_fwd(x_kv, kv_norm_w[None], kv_scale, kv_shift + tok, name="kv_norm_re")
            wg["w_kv", 0] = mm(hkv, dkv, mode="tn", out_dtype=BF16, name="kv_proj_dw")
            dx, d_kvnorm, dsc, dsh = mm_norm_bwd(
                dkv, g_full("w_kv", 0), x_kv, kv_norm_w[None], kv_scale, dx, name="kv_proj_dx")
            d_kvmod = jnp.concatenate([dsh, dsc], axis=1)
            tok = sibling_begin([("w_kv", 0, w_kv)], dx)
        if i > 0:
            tok = reduce_step(depth - 1 - i, dx)
    grad_x = dx[None]
    grads = {}

    small = {
        "ffn_norm_w": jnp.stack([jnp.stack([r[0] for r in row]) for row in sg["ffn_norm_w"]]),
        "mod_b": jnp.stack([r[0] for r in sg["mod"]]),
        "mix_norm_w": jnp.stack([r[0] for r in sg["mix_norm_w"]]),
        "ssm_conv_w": jnp.stack(sg["ssm_conv_w"]),
        "ssm_conv_b": jnp.stack([r[0] for r in sg["ssm_conv_b"]]),
        "ssm_dt_bias": jnp.stack([r[0] for r in sg["ssm_dt_bias"]]),
        "ssm_a_log": jnp.stack([r[0] for r in sg["ssm_a_log"]]),
        "ssm_d": jnp.stack([r[0] for r in sg["ssm_d"]]),
        "ssm_norm_w": jnp.stack([r[0] for r in sg["ssm_norm_w"]]),
        "kv_norm_w": d_kvnorm[0],
        "kv_mod_b": d_kvmod[0],
        "b_kv": d_bkv[0],
        "attn_b_q": jnp.stack([r[0] for r in sg["attn_b_q"]]),
        "attn_sinks": jnp.stack([r[0] for r in sg["attn_sinks"]]),
        "attn_b_o": jnp.stack([r[0] for r in sg["attn_b_o"]]),
        "final_norm_w": d_final[0],
    }
    small_like = [small[n].shape for n in SMALL]
    sv_pack, sv_n = _pack([small[n] for n in SMALL])
    sv_all = ag8(sv_pack + tok, name="ag_small_g")
    sv_all = sv_all + reduce_step(depth, sv_all)
    sv_sum = reduce8(sv_all, name="small_g_sum").reshape(-1)[:sv_n]
    for n, gsum in zip(SMALL, _unpack(sv_sum, small_like)):
        grads[n] = chip_cols(gsum, W[n].shape[-1]) if n in SMALL_SHARDED else gsum

    per_dev = [_unpack(sv_all[b].reshape(-1)[:sv_n], small_like) for b in range(N_DEV)]
    i_modb, i_kvb = SMALL.index("mod_b"), SMALL.index("kv_mod_b")
    dmod_all = jnp.stack([chip_cols(p[i_modb], MW) for p in per_dev], axis=1)
    dkv_all = jnp.stack([chip_cols(p[i_kvb], KW) for p in per_dev], axis=0)[None]
    c_t = jnp.transpose(c_all)
    grads["mod_w"] = outer8(c_t, dmod_all, name="mod_w_grad")
    grads["kv_mod_w"] = outer8(c_t, dkv_all, name="kv_mod_w_grad")[0]

    delta, new_m, new_v = {}, {}, {}
    for n in COLUMN_PARALLEL:
        delta[n], new_m[n], new_v[n] = adamw(W[n], grads[n], MOM[n], VAR[n], name=f"adamw_{n}")
    like = [W[n].shape for n in SMALL]
    packs = [_pack([d[n] for n in SMALL])[0] for d in (W, grads, MOM, VAR)]
    n_small = sum(int(W[n].size) for n in SMALL)
    for dst, res in zip((delta, new_m, new_v), adamw(*packs, name="adamw_small")):
        for n, a in zip(SMALL, _unpack(res.reshape(-1)[:n_small], like)):
            dst[n] = a

    reduce_end(delta["mod_w"])
    for n, s in zip(GATHERED, rs_share([gfull[n] for n in GATHERED], name="rs_share")):
        grads[n] = s.reshape(W[n].shape)
        delta[n], new_m[n], new_v[n] = adamw(W[n], grads[n], MOM[n], VAR[n], name=f"adamw_{n}")

    return (loss, grad_x, *[grads[n] for n in WEIGHTS], *[delta[n] for n in WEIGHTS], *[new_m[n] for n in WEIGHTS],
            *[new_v[n] for n in WEIGHTS])
```
